```python
import jax, jax.numpy as jnp
from jax import lax
import numpy as np

D_MODEL = 1024
BATCH = 16
SEQ = 2048
DEPTH = 2

RG_WIDTH = 1024
RG_BLOCKS = 16
RG_BLOCK_DIM = RG_WIDTH // RG_BLOCKS
CONV_WIDTH = 4
RG_C = 8.0
SB_HEADS = 8
SB_HEAD_DIM = 64
SB_WIDTH = SB_HEADS * SB_HEAD_DIM
FOX_HEADS = 8
FOX_HEAD_DIM = 64
FOX_WIDTH = FOX_HEADS * FOX_HEAD_DIM
N_BRANCHES = 3
D_FF = 2816
Q_BLOCK = 128
N_SUBLAYERS = 3
EPS = 1e-6
IN_SIZES = (RG_WIDTH, RG_WIDTH, 3 * SB_WIDTH, 3 * FOX_WIDTH, FOX_HEADS, N_BRANCHES * D_MODEL)
N_IN = sum(IN_SIZES)

kernel_name = "hybrid_rglru_stickbreak_fox_macaron_adaln"


def _rmsnorm(x, gain):
    x32 = x.astype(jnp.float32)
    y = x32 * lax.rsqrt(jnp.mean(x32 * x32, axis=-1, keepdims=True) + EPS)
    return y.astype(x.dtype) * gain


def _modulate(h, shift, scale):
    return h * (1.0 + scale[:, None, :]) + shift[:, None, :]


def _swiglu(h, w1, w3, w2):
    return (jax.nn.silu(h @ w1) * (h @ w3)) @ w2


def _causal_depthwise_conv(x, w, b):
    y = lax.conv_general_dilated(
        x, w[:, None, :].astype(x.dtype), window_strides=(1,), padding=[(CONV_WIDTH - 1, 0)],
        dimension_numbers=('NWC', 'WIO', 'NWC'), feature_group_count=x.shape[-1])
    return y + b


def _block_diag(x, w, b):
    bsz, slen, _ = x.shape
    xb = x.reshape(bsz, slen, RG_BLOCKS, RG_BLOCK_DIM)
    return jnp.einsum('bsnd,nde->bsne', xb, w).reshape(bsz, slen, RG_WIDTH) + b


def _rg_lru(x, w_a, b_a, w_x, b_x, lam):
    x32 = x.astype(jnp.float32)
    r = jax.nn.sigmoid(_block_diag(x, w_a, b_a).astype(jnp.float32))
    i = jax.nn.sigmoid(_block_diag(x, w_x, b_x).astype(jnp.float32))
    log_a = -RG_C * r * jax.nn.softplus(-lam.astype(jnp.float32))
    a = jnp.exp(log_a)
    u = jnp.sqrt(-jnp.expm1(2.0 * log_a)) * (i * x32)

    def combine(left, right):
        a_l, b_l = left
        a_r, b_r = right
        return a_l * a_r, a_r * b_l + b_r

    _, h = lax.associative_scan(combine, (a, u), axis=1)
    return h.astype(x.dtype)


def _split_qkv(t, n_heads, head_dim):
    bsz, slen, _ = t.shape
    t = t.reshape(bsz, slen, 3, n_heads, head_dim).transpose(2, 0, 3, 1, 4)
    return t[0], t[1], t[2]


def _merge_heads(o):
    bsz, nh, slen, hd = o.shape
    return o.transpose(0, 2, 1, 3).reshape(bsz, slen, nh * hd)


def _stick_breaking_attention(q, k, v):
    slen, hd = q.shape[2], q.shape[3]
    scale = hd ** -0.5
    outs = []
    for blk in range(slen // Q_BLOCK):
        q0, q1 = blk * Q_BLOCK, (blk + 1) * Q_BLOCK
        z = jnp.einsum('bhqd,bhkd->bhqk', q[:, :, q0:q1], k[:, :, :q1],
                       preferred_element_type=jnp.float32) * scale
        t_idx = q0 + jnp.arange(Q_BLOCK)[:, None]
        s_idx = jnp.arange(q1)[None, :]
        strict = s_idx < t_idx
        log_keep = jnp.where(strict, jax.nn.log_sigmoid(-z), 0.0)
        suffix = lax.cumsum(log_keep, axis=3, reverse=True) - log_keep
        w = jnp.where(strict, jnp.exp(jax.nn.log_sigmoid(z) + suffix), 0.0)
        outs.append(jnp.einsum('bhqk,bhkd->bhqd', w.astype(v.dtype), v[:, :, :q1]))
    return jnp.concatenate(outs, axis=2)


def _forgetting_attention(q, k, v, log_f):
    slen, hd = q.shape[2], q.shape[3]
    scale = hd ** -0.5
    cum = lax.cumsum(log_f, axis=2)
    outs = []
    for blk in range(slen // Q_BLOCK):
        q0, q1 = blk * Q_BLOCK, (blk + 1) * Q_BLOCK
        z = jnp.einsum('bhqd,bhkd->bhqk', q[:, :, q0:q1], k[:, :, :q1],
                       preferred_element_type=jnp.float32) * scale
        z = z + cum[:, :, q0:q1, None] - cum[:, :, None, :q1]
        causal = jnp.arange(q1)[None, :] <= (q0 + jnp.arange(Q_BLOCK)[:, None])
        p = jax.nn.softmax(jnp.where(causal, z, -jnp.inf), axis=-1)
        outs.append(jnp.einsum('bhqk,bhkd->bhqd', p.astype(v.dtype), v[:, :, :q1]))
    return jnp.concatenate(outs, axis=2)


def _hybrid_mixer(h, w_in, conv_w, conv_b, rg_wa, rg_ba, rg_wx, rg_bx, rg_lam, fox_bf, merge_b,
                  w_rg, w_sb, w_fox, w_o):
    bsz, slen, _ = h.shape
    proj = h @ w_in
    cuts = np.cumsum(IN_SIZES)[:-1].tolist()
    rg_x, rg_gate, sb_qkv, fox_qkv, fox_f, merge = jnp.split(proj, cuts, axis=-1)
    xa = _causal_depthwise_conv(rg_x, conv_w, conv_b)
    ya = jax.nn.gelu(rg_gate) * _rg_lru(xa, rg_wa, rg_ba, rg_wx, rg_bx, rg_lam)
    q_b, k_b, v_b = _split_qkv(sb_qkv, SB_HEADS, SB_HEAD_DIM)
    yb = _merge_heads(_stick_breaking_attention(q_b, k_b, v_b))
    q_c, k_c, v_c = _split_qkv(fox_qkv, FOX_HEADS, FOX_HEAD_DIM)
    log_f = jax.nn.log_sigmoid((fox_f + fox_bf).astype(jnp.float32)).transpose(0, 2, 1)
    yc = _merge_heads(_forgetting_attention(q_c, k_c, v_c, log_f))
    g = jax.nn.sigmoid(merge + merge_b).reshape(bsz, slen, N_BRANCHES, D_MODEL)
    mixed = g[:, :, 0] * (ya @ w_rg) + g[:, :, 1] * (yb @ w_sb) + g[:, :, 2] * (yc @ w_fox)
    return mixed @ w_o


def _fwd_setup_inputs(seed: int = 0) -> dict:
    key = jax.random.key(seed)
    ks = jax.random.split(key, 32)
    f32 = jnp.float32
    L, D = DEPTH, D_MODEL

    def nrm(k, shape, fan_in, mult=1.0):
        return jax.random.normal(k, shape, f32) * (mult * fan_in ** -0.5)

    def gain(k, shape):
        return 1.0 + 0.01 * jax.random.normal(k, shape, f32)

    def small(k, shape):
        return 0.01 * jax.random.normal(k, shape, f32)

    a_c = jax.random.uniform(ks[12], (L, RG_WIDTH), f32, 0.9, 0.999)
    a = a_c ** (1.0 / RG_C)
    rg_lam = jnp.log(a) - jnp.log1p(-a)
    return {
        "x": jax.random.normal(ks[0], (BATCH, SEQ, D), f32),
        "c": jax.random.normal(ks[1], (BATCH, D), f32),
        "ffn1_norm": gain(ks[2], (L, D)),
        "ffn1_w1": nrm(ks[3], (L, D, D_FF), D),
        "ffn1_w3": nrm(ks[4], (L, D, D_FF), D),
        "ffn1_w2": nrm(ks[5], (L, D_FF, D), D_FF),
        "mix_norm": gain(ks[6], (L, D)),
        "w_in": nrm(ks[7], (L, D, N_IN), D),
        "conv_w": nrm(ks[8], (L, CONV_WIDTH, RG_WIDTH), CONV_WIDTH),
        "conv_b": small(ks[9], (L, RG_WIDTH)),
        "rg_wa": nrm(ks[10], (L, RG_BLOCKS, RG_BLOCK_DIM, RG_BLOCK_DIM), RG_BLOCK_DIM),
        "rg_ba": small(ks[11], (L, RG_WIDTH)),
        "rg_wx": nrm(ks[13], (L, RG_BLOCKS, RG_BLOCK_DIM, RG_BLOCK_DIM), RG_BLOCK_DIM),
        "rg_bx": small(ks[14], (L, RG_WIDTH)),
        "rg_lam": rg_lam,
        "fox_bf": jax.random.uniform(ks[15], (L, FOX_HEADS), f32, 2.0, 5.0),
        "merge_b": small(ks[16], (L, N_BRANCHES * D)),
        "w_rg": nrm(ks[17], (L, RG_WIDTH, D), RG_WIDTH),
        "w_sb": nrm(ks[18], (L, SB_WIDTH, D), SB_WIDTH),
        "w_fox": nrm(ks[19], (L, FOX_WIDTH, D), FOX_WIDTH),
        "w_o": nrm(ks[20], (L, D, D), D),
        "ffn2_norm": gain(ks[21], (L, D)),
        "ffn2_w1": nrm(ks[22], (L, D, D_FF), D),
        "ffn2_w3": nrm(ks[23], (L, D, D_FF), D),
        "ffn2_w2": nrm(ks[24], (L, D_FF, D), D_FF),
        "ada_w": nrm(ks[25], (L, D, N_SUBLAYERS * 3 * D), D, 0.1),
        "ada_b": small(ks[26], (L, N_SUBLAYERS * 3 * D)),
        "final_norm": gain(ks[27], (D,)),
        "final_ada_w": nrm(ks[28], (D, 2 * D), D, 0.1),
        "final_ada_b": small(ks[29], (2 * D,)),
    }


def _fwd_reference(x, c, ffn1_norm, ffn1_w1, ffn1_w3, ffn1_w2, mix_norm, w_in, conv_w, conv_b,
              rg_wa, rg_ba, rg_wx, rg_bx, rg_lam, fox_bf, merge_b, w_rg, w_sb, w_fox, w_o,
              ffn2_norm, ffn2_w1, ffn2_w3, ffn2_w2, ada_w, ada_b, final_norm, final_ada_w,
              final_ada_b):
    bsz = x.shape[0]
    c_act = jax.nn.silu(c)
    for l in range(DEPTH):
        mod = (c_act @ ada_w[l] + ada_b[l]).reshape(bsz, N_SUBLAYERS, 3, D_MODEL)
        h = _modulate(_rmsnorm(x, ffn1_norm[l]), mod[:, 0, 0], mod[:, 0, 1])
        x = x + 0.5 * (1.0 + mod[:, 0, 2])[:, None, :] * _swiglu(h, ffn1_w1[l], ffn1_w3[l], ffn1_w2[l])
        h = _modulate(_rmsnorm(x, mix_norm[l]), mod[:, 1, 0], mod[:, 1, 1])
        y = _hybrid_mixer(h, w_in[l], conv_w[l], conv_b[l], rg_wa[l], rg_ba[l], rg_wx[l], rg_bx[l],
                          rg_lam[l], fox_bf[l], merge_b[l], w_rg[l], w_sb[l], w_fox[l], w_o[l])
        x = x + (1.0 + mod[:, 1, 2])[:, None, :] * y
        h = _modulate(_rmsnorm(x, ffn2_norm[l]), mod[:, 2, 0], mod[:, 2, 1])
        x = x + 0.5 * (1.0 + mod[:, 2, 2])[:, None, :] * _swiglu(h, ffn2_w1[l], ffn2_w3[l], ffn2_w2[l])
    fm = (c_act @ final_ada_w + final_ada_b).reshape(bsz, 2, D_MODEL)
    return _modulate(_rmsnorm(x, final_norm), fm[:, 0], fm[:, 1])


import jax as _jax
import jax.numpy as _jnp

TWIN_FORMAT = 'train_step'
FWD_PARAMS = ['x', 'c', 'ffn1_norm', 'ffn1_w1', 'ffn1_w3', 'ffn1_w2', 'mix_norm', 'w_in', 'conv_w', 'conv_b', 'rg_wa', 'rg_ba', 'rg_wx', 'rg_bx', 'rg_lam', 'fox_bf', 'merge_b', 'w_rg', 'w_sb', 'w_fox', 'w_o', 'ffn2_norm', 'ffn2_w1', 'ffn2_w3', 'ffn2_w2', 'ada_w', 'ada_b', 'final_norm', 'final_ada_w', 'final_ada_b']
TWIN_WEIGHTS = ['ffn1_norm', 'ffn1_w1', 'ffn1_w3', 'ffn1_w2', 'mix_norm', 'w_in', 'conv_w', 'conv_b', 'rg_wa', 'rg_ba', 'rg_wx', 'rg_bx', 'rg_lam', 'fox_bf', 'merge_b', 'w_rg', 'w_sb', 'w_fox', 'w_o', 'ffn2_norm', 'ffn2_w1', 'ffn2_w3', 'ffn2_w2', 'ada_w', 'ada_b', 'final_norm', 'final_ada_w', 'final_ada_b']
TWIN_DIFF_INPUT = 'x'
TWIN_INPUTS = ['x', 'c', 'ffn1_norm', 'ffn1_w1', 'ffn1_w3', 'ffn1_w2', 'mix_norm', 'w_in', 'conv_w', 'conv_b', 'rg_wa', 'rg_ba', 'rg_wx', 'rg_bx', 'rg_lam', 'fox_bf', 'merge_b', 'w_rg', 'w_sb', 'w_fox', 'w_o', 'ffn2_norm', 'ffn2_w1', 'ffn2_w3', 'ffn2_w2', 'ada_w', 'ada_b', 'final_norm', 'final_ada_w', 'final_ada_b', 'loss_target', 'm_ffn1_norm', 'm_ffn1_w1', 'm_ffn1_w3', 'm_ffn1_w2', 'm_mix_norm', 'm_w_in', 'm_conv_w', 'm_conv_b', 'm_rg_wa', 'm_rg_ba', 'm_rg_wx', 'm_rg_bx', 'm_rg_lam', 'm_fox_bf', 'm_merge_b', 'm_w_rg', 'm_w_sb', 'm_w_fox', 'm_w_o', 'm_ffn2_norm', 'm_ffn2_w1', 'm_ffn2_w3', 'm_ffn2_w2', 'm_ada_w', 'm_ada_b', 'm_final_norm', 'm_final_ada_w', 'm_final_ada_b', 'v_ffn1_norm', 'v_ffn1_w1', 'v_ffn1_w3', 'v_ffn1_w2', 'v_mix_norm', 'v_w_in', 'v_conv_w', 'v_conv_b', 'v_rg_wa', 'v_rg_ba', 'v_rg_wx', 'v_rg_bx', 'v_rg_lam', 'v_fox_bf', 'v_merge_b', 'v_w_rg', 'v_w_sb', 'v_w_fox', 'v_w_o', 'v_ffn2_norm', 'v_ffn2_w1', 'v_ffn2_w3', 'v_ffn2_w2', 'v_ada_w', 'v_ada_b', 'v_final_norm', 'v_final_ada_w', 'v_final_ada_b']
TWIN_OUTPUTS = ['loss', 'grad_x', 'grad_ffn1_norm', 'grad_ffn1_w1', 'grad_ffn1_w3', 'grad_ffn1_w2', 'grad_mix_norm', 'grad_w_in', 'grad_conv_w', 'grad_conv_b', 'grad_rg_wa', 'grad_rg_ba', 'grad_rg_wx', 'grad_rg_bx', 'grad_rg_lam', 'grad_fox_bf', 'grad_merge_b', 'grad_w_rg', 'grad_w_sb', 'grad_w_fox', 'grad_w_o', 'grad_ffn2_norm', 'grad_ffn2_w1', 'grad_ffn2_w3', 'grad_ffn2_w2', 'grad_ada_w', 'grad_ada_b', 'grad_final_norm', 'grad_final_ada_w', 'grad_final_ada_b', 'delta_ffn1_norm', 'delta_ffn1_w1', 'delta_ffn1_w3', 'delta_ffn1_w2', 'delta_mix_norm', 'delta_w_in', 'delta_conv_w', 'delta_conv_b', 'delta_rg_wa', 'delta_rg_ba', 'delta_rg_wx', 'delta_rg_bx', 'delta_rg_lam', 'delta_fox_bf', 'delta_merge_b', 'delta_w_rg', 'delta_w_sb', 'delta_w_fox', 'delta_w_o', 'delta_ffn2_norm', 'delta_ffn2_w1', 'delta_ffn2_w3', 'delta_ffn2_w2', 'delta_ada_w', 'delta_ada_b', 'delta_final_norm', 'delta_final_ada_w', 'delta_final_ada_b', 'new_m_ffn1_norm', 'new_m_ffn1_w1', 'new_m_ffn1_w3', 'new_m_ffn1_w2', 'new_m_mix_norm', 'new_m_w_in', 'new_m_conv_w', 'new_m_conv_b', 'new_m_rg_wa', 'new_m_rg_ba', 'new_m_rg_wx', 'new_m_rg_bx', 'new_m_rg_lam', 'new_m_fox_bf', 'new_m_merge_b', 'new_m_w_rg', 'new_m_w_sb', 'new_m_w_fox', 'new_m_w_o', 'new_m_ffn2_norm', 'new_m_ffn2_w1', 'new_m_ffn2_w3', 'new_m_ffn2_w2', 'new_m_ada_w', 'new_m_ada_b', 'new_m_final_norm', 'new_m_final_ada_w', 'new_m_final_ada_b', 'new_v_ffn1_norm', 'new_v_ffn1_w1', 'new_v_ffn1_w3', 'new_v_ffn1_w2', 'new_v_mix_norm', 'new_v_w_in', 'new_v_conv_w', 'new_v_conv_b', 'new_v_rg_wa', 'new_v_rg_ba', 'new_v_rg_wx', 'new_v_rg_bx', 'new_v_rg_lam', 'new_v_fox_bf', 'new_v_merge_b', 'new_v_w_rg', 'new_v_w_sb', 'new_v_w_fox', 'new_v_w_o', 'new_v_ffn2_norm', 'new_v_ffn2_w1', 'new_v_ffn2_w3', 'new_v_ffn2_w2', 'new_v_ada_w', 'new_v_ada_b', 'new_v_final_norm', 'new_v_final_ada_w', 'new_v_final_ada_b']
TWIN_LEAF_KINDS = {'loss': 'loss', 'grad_x': 'grad_x', 'grad_ffn1_norm': 'grad_w', 'grad_ffn1_w1': 'grad_w', 'grad_ffn1_w3': 'grad_w', 'grad_ffn1_w2': 'grad_w', 'grad_mix_norm': 'grad_w', 'grad_w_in': 'grad_w', 'grad_conv_w': 'grad_w', 'grad_conv_b': 'grad_w', 'grad_rg_wa': 'grad_w', 'grad_rg_ba': 'grad_w', 'grad_rg_wx': 'grad_w', 'grad_rg_bx': 'grad_w', 'grad_rg_lam': 'grad_w', 'grad_fox_bf': 'grad_w', 'grad_merge_b': 'grad_w', 'grad_w_rg': 'grad_w', 'grad_w_sb': 'grad_w', 'grad_w_fox': 'grad_w', 'grad_w_o': 'grad_w', 'grad_ffn2_norm': 'grad_w', 'grad_ffn2_w1': 'grad_w', 'grad_ffn2_w3': 'grad_w', 'grad_ffn2_w2': 'grad_w', 'grad_ada_w': 'grad_w', 'grad_ada_b': 'grad_w', 'grad_final_norm': 'grad_w', 'grad_final_ada_w': 'grad_w', 'grad_final_ada_b': 'grad_w', 'delta_ffn1_norm': 'delta_w', 'delta_ffn1_w1': 'delta_w', 'delta_ffn1_w3': 'delta_w', 'delta_ffn1_w2': 'delta_w', 'delta_mix_norm': 'delta_w', 'delta_w_in': 'delta_w', 'delta_conv_w': 'delta_w', 'delta_conv_b': 'delta_w', 'delta_rg_wa': 'delta_w', 'delta_rg_ba': 'delta_w', 'delta_rg_wx': 'delta_w', 'delta_rg_bx': 'delta_w', 'delta_rg_lam': 'delta_w', 'delta_fox_bf': 'delta_w', 'delta_merge_b': 'delta_w', 'delta_w_rg': 'delta_w', 'delta_w_sb': 'delta_w', 'delta_w_fox': 'delta_w', 'delta_w_o': 'delta_w', 'delta_ffn2_norm': 'delta_w', 'delta_ffn2_w1': 'delta_w', 'delta_ffn2_w3': 'delta_w', 'delta_ffn2_w2': 'delta_w', 'delta_ada_w': 'delta_w', 'delta_ada_b': 'delta_w', 'delta_final_norm': 'delta_w', 'delta_final_ada_w': 'delta_w', 'delta_final_ada_b': 'delta_w', 'new_m_ffn1_norm': 'new_m', 'new_m_ffn1_w1': 'new_m', 'new_m_ffn1_w3': 'new_m', 'new_m_ffn1_w2': 'new_m', 'new_m_mix_norm': 'new_m', 'new_m_w_in': 'new_m', 'new_m_conv_w': 'new_m', 'new_m_conv_b': 'new_m', 'new_m_rg_wa': 'new_m', 'new_m_rg_ba': 'new_m', 'new_m_rg_wx': 'new_m', 'new_m_rg_bx': 'new_m', 'new_m_rg_lam': 'new_m', 'new_m_fox_bf': 'new_m', 'new_m_merge_b': 'new_m', 'new_m_w_rg': 'new_m', 'new_m_w_sb': 'new_m', 'new_m_w_fox': 'new_m', 'new_m_w_o': 'new_m', 'new_m_ffn2_norm': 'new_m', 'new_m_ffn2_w1': 'new_m', 'new_m_ffn2_w3': 'new_m', 'new_m_ffn2_w2': 'new_m', 'new_m_ada_w': 'new_m', 'new_m_ada_b': 'new_m', 'new_m_final_norm': 'new_m', 'new_m_final_ada_w': 'new_m', 'new_m_final_ada_b': 'new_m', 'new_v_ffn1_norm': 'new_v', 'new_v_ffn1_w1': 'new_v', 'new_v_ffn1_w3': 'new_v', 'new_v_ffn1_w2': 'new_v', 'new_v_mix_norm': 'new_v', 'new_v_w_in': 'new_v', 'new_v_conv_w': 'new_v', 'new_v_conv_b': 'new_v', 'new_v_rg_wa': 'new_v', 'new_v_rg_ba': 'new_v', 'new_v_rg_wx': 'new_v', 'new_v_rg_bx': 'new_v', 'new_v_rg_lam': 'new_v', 'new_v_fox_bf': 'new_v', 'new_v_merge_b': 'new_v', 'new_v_w_rg': 'new_v', 'new_v_w_sb': 'new_v', 'new_v_w_fox': 'new_v', 'new_v_w_o': 'new_v', 'new_v_ffn2_norm': 'new_v', 'new_v_ffn2_w1': 'new_v', 'new_v_ffn2_w3': 'new_v', 'new_v_ffn2_w2': 'new_v', 'new_v_ada_w': 'new_v', 'new_v_ada_b': 'new_v', 'new_v_final_norm': 'new_v', 'new_v_final_ada_w': 'new_v', 'new_v_final_ada_b': 'new_v'}


def _forward(args):
    return _fwd_reference(*[args[k] for k in FWD_PARAMS])


def _output_shape():
    out = _jax.eval_shape(lambda: _forward(_fwd_setup_inputs(0)))
    return out.shape, out.dtype

N_MICROBATCH = 1
ADAM_LR = 0.001
ADAM_B1 = 0.9
ADAM_B2 = 0.999
ADAM_EPS = 1e-08
ADAM_WD = 0.01
ADAM_STEP = 10
PER_EXAMPLE_BATCH_AXIS = {'x': 0, 'c': 0, 'loss_target': 0}
SHARED_INPUTS = []
_WEIGHT_DTYPES = {'ffn1_norm': _jnp.float32, 'ffn1_w1': _jnp.float32, 'ffn1_w3': _jnp.float32, 'ffn1_w2': _jnp.float32, 'mix_norm': _jnp.float32, 'w_in': _jnp.float32, 'conv_w': _jnp.float32, 'conv_b': _jnp.float32, 'rg_wa': _jnp.float32, 'rg_ba': _jnp.float32, 'rg_wx': _jnp.float32, 'rg_bx': _jnp.float32, 'rg_lam': _jnp.float32, 'fox_bf': _jnp.float32, 'merge_b': _jnp.float32, 'w_rg': _jnp.float32, 'w_sb': _jnp.float32, 'w_fox': _jnp.float32, 'w_o': _jnp.float32, 'ffn2_norm': _jnp.float32, 'ffn2_w1': _jnp.float32, 'ffn2_w3': _jnp.float32, 'ffn2_w2': _jnp.float32, 'ada_w': _jnp.float32, 'ada_b': _jnp.float32, 'final_norm': _jnp.float32, 'final_ada_w': _jnp.float32, 'final_ada_b': _jnp.float32}
MOMENT_SCALE = {'ffn1_norm': 8.132552e-02, 'ffn1_w1': 3.489236e-02, 'ffn1_w3': 3.414100e-02, 'ffn1_w2': 5.668465e-02, 'mix_norm': 1.152283e-01, 'w_in': 4.199293e-02, 'conv_w': 8.319628e-02, 'conv_b': 1.020948e+00, 'rg_wa': 3.041729e-02, 'rg_ba': 2.360873e-02, 'rg_wx': 5.703871e-02, 'rg_bx': 2.905689e-02, 'rg_lam': 4.338345e-02, 'fox_bf': 3.485919e-01, 'merge_b': 2.369493e-02, 'w_rg': 8.201745e-02, 'w_sb': 6.115180e-02, 'w_fox': 2.801241e-02, 'w_o': 9.962562e-02, 'ffn2_norm': 6.905695e-02, 'ffn2_w1': 2.873827e-02, 'ffn2_w3': 2.787435e-02, 'ffn2_w2': 4.624490e-02, 'ada_w': 1.939266e-01, 'ada_b': 3.680223e-01, 'final_norm': 3.233654e+01, 'final_ada_w': 8.234653e+00, 'final_ada_b': 2.281167e+01}


def _to_microbatches(a, axis):
    t = _jnp.moveaxis(a, axis, 0)
    t = t.reshape((N_MICROBATCH, t.shape[0] // N_MICROBATCH) + t.shape[1:])
    return _jnp.moveaxis(t, 1, axis + 1)


def setup_inputs(seed: int = 0) -> dict:
    inp = _fwd_setup_inputs(seed)
    key = _jax.random.fold_in(_jax.random.key(seed), 7919)
    shape, _ = _output_shape()
    out = dict(inp)
    out["loss_target"] = _jax.random.normal(_jax.random.fold_in(key, 0), shape, _jnp.float32)
    for i, name in enumerate(TWIN_WEIGHTS):
        w = inp[name].astype(_jnp.float32)
        if MOMENT_SCALE is None:
            s = _jnp.sqrt(_jnp.mean(_jnp.square(w)) + 1e-30)
        else:
            s = MOMENT_SCALE[name]
        km, kv = _jax.random.split(_jax.random.fold_in(key, i + 1))
        out[name] = w
        out["m_" + name] = s * _jax.random.normal(km, w.shape, _jnp.float32)
        out["v_" + name] = (s * s) * _jax.random.uniform(kv, w.shape, _jnp.float32, 0.5, 1.5)
    if N_MICROBATCH > 1:
        for name, axis in PER_EXAMPLE_BATCH_AXIS.items():
            out[name] = _to_microbatches(out[name], axis)
    return {'x': out['x'], 'c': out['c'], 'ffn1_norm': out['ffn1_norm'], 'ffn1_w1': out['ffn1_w1'], 'ffn1_w3': out['ffn1_w3'], 'ffn1_w2': out['ffn1_w2'], 'mix_norm': out['mix_norm'], 'w_in': out['w_in'], 'conv_w': out['conv_w'], 'conv_b': out['conv_b'], 'rg_wa': out['rg_wa'], 'rg_ba': out['rg_ba'], 'rg_wx': out['rg_wx'], 'rg_bx': out['rg_bx'], 'rg_lam': out['rg_lam'], 'fox_bf': out['fox_bf'], 'merge_b': out['merge_b'], 'w_rg': out['w_rg'], 'w_sb': out['w_sb'], 'w_fox': out['w_fox'], 'w_o': out['w_o'], 'ffn2_norm': out['ffn2_norm'], 'ffn2_w1': out['ffn2_w1'], 'ffn2_w3': out['ffn2_w3'], 'ffn2_w2': out['ffn2_w2'], 'ada_w': out['ada_w'], 'ada_b': out['ada_b'], 'final_norm': out['final_norm'], 'final_ada_w': out['final_ada_w'], 'final_ada_b': out['final_ada_b'], 'loss_target': out['loss_target'], 'm_ffn1_norm': out['m_ffn1_norm'], 'm_ffn1_w1': out['m_ffn1_w1'], 'm_ffn1_w3': out['m_ffn1_w3'], 'm_ffn1_w2': out['m_ffn1_w2'], 'm_mix_norm': out['m_mix_norm'], 'm_w_in': out['m_w_in'], 'm_conv_w': out['m_conv_w'], 'm_conv_b': out['m_conv_b'], 'm_rg_wa': out['m_rg_wa'], 'm_rg_ba': out['m_rg_ba'], 'm_rg_wx': out['m_rg_wx'], 'm_rg_bx': out['m_rg_bx'], 'm_rg_lam': out['m_rg_lam'], 'm_fox_bf': out['m_fox_bf'], 'm_merge_b': out['m_merge_b'], 'm_w_rg': out['m_w_rg'], 'm_w_sb': out['m_w_sb'], 'm_w_fox': out['m_w_fox'], 'm_w_o': out['m_w_o'], 'm_ffn2_norm': out['m_ffn2_norm'], 'm_ffn2_w1': out['m_ffn2_w1'], 'm_ffn2_w3': out['m_ffn2_w3'], 'm_ffn2_w2': out['m_ffn2_w2'], 'm_ada_w': out['m_ada_w'], 'm_ada_b': out['m_ada_b'], 'm_final_norm': out['m_final_norm'], 'm_final_ada_w': out['m_final_ada_w'], 'm_final_ada_b': out['m_final_ada_b'], 'v_ffn1_norm': out['v_ffn1_norm'], 'v_ffn1_w1': out['v_ffn1_w1'], 'v_ffn1_w3': out['v_ffn1_w3'], 'v_ffn1_w2': out['v_ffn1_w2'], 'v_mix_norm': out['v_mix_norm'], 'v_w_in': out['v_w_in'], 'v_conv_w': out['v_conv_w'], 'v_conv_b': out['v_conv_b'], 'v_rg_wa': out['v_rg_wa'], 'v_rg_ba': out['v_rg_ba'], 'v_rg_wx': out['v_rg_wx'], 'v_rg_bx': out['v_rg_bx'], 'v_rg_lam': out['v_rg_lam'], 'v_fox_bf': out['v_fox_bf'], 'v_merge_b': out['v_merge_b'], 'v_w_rg': out['v_w_rg'], 'v_w_sb': out['v_w_sb'], 'v_w_fox': out['v_w_fox'], 'v_w_o': out['v_w_o'], 'v_ffn2_norm': out['v_ffn2_norm'], 'v_ffn2_w1': out['v_ffn2_w1'], 'v_ffn2_w3': out['v_ffn2_w3'], 'v_ffn2_w2': out['v_ffn2_w2'], 'v_ada_w': out['v_ada_w'], 'v_ada_b': out['v_ada_b'], 'v_final_norm': out['v_final_norm'], 'v_final_ada_w': out['v_final_ada_w'], 'v_final_ada_b': out['v_final_ada_b']}


def _loss(weights, diff, rest, loss_target):
    with _jax.named_scope("forward"):
        args = {**rest, TWIN_DIFF_INPUT: diff, **{k: w.astype(_WEIGHT_DTYPES[k]) for k, w in weights.items()}}
        y = _forward(args)
    with _jax.named_scope("loss_head"):
        err = _jnp.square(y.astype(_jnp.float32) - loss_target)
        return 0.5 * _jnp.sum(_jnp.mean(err, axis=-1)) if err.ndim else 0.5 * err


def _adamw(w, g, m, v):
    m = ADAM_B1 * m + (1.0 - ADAM_B1) * g
    v = ADAM_B2 * v + (1.0 - ADAM_B2) * _jnp.square(g)
    m_hat = m / (1.0 - ADAM_B1 ** ADAM_STEP)
    v_hat = v / (1.0 - ADAM_B2 ** ADAM_STEP)
    delta = -ADAM_LR * (m_hat / (_jnp.sqrt(v_hat) + ADAM_EPS) + ADAM_WD * w)
    return delta, m, v


def reference(x, c, ffn1_norm, ffn1_w1, ffn1_w3, ffn1_w2, mix_norm, w_in, conv_w, conv_b, rg_wa, rg_ba, rg_wx, rg_bx, rg_lam, fox_bf, merge_b, w_rg, w_sb, w_fox, w_o, ffn2_norm, ffn2_w1, ffn2_w3, ffn2_w2, ada_w, ada_b, final_norm, final_ada_w, final_ada_b, loss_target, m_ffn1_norm, m_ffn1_w1, m_ffn1_w3, m_ffn1_w2, m_mix_norm, m_w_in, m_conv_w, m_conv_b, m_rg_wa, m_rg_ba, m_rg_wx, m_rg_bx, m_rg_lam, m_fox_bf, m_merge_b, m_w_rg, m_w_sb, m_w_fox, m_w_o, m_ffn2_norm, m_ffn2_w1, m_ffn2_w3, m_ffn2_w2, m_ada_w, m_ada_b, m_final_norm, m_final_ada_w, m_final_ada_b, v_ffn1_norm, v_ffn1_w1, v_ffn1_w3, v_ffn1_w2, v_mix_norm, v_w_in, v_conv_w, v_conv_b, v_rg_wa, v_rg_ba, v_rg_wx, v_rg_bx, v_rg_lam, v_fox_bf, v_merge_b, v_w_rg, v_w_sb, v_w_fox, v_w_o, v_ffn2_norm, v_ffn2_w1, v_ffn2_w3, v_ffn2_w2, v_ada_w, v_ada_b, v_final_norm, v_final_ada_w, v_final_ada_b):
    given = dict(x=x, c=c, ffn1_norm=ffn1_norm, ffn1_w1=ffn1_w1, ffn1_w3=ffn1_w3, ffn1_w2=ffn1_w2, mix_norm=mix_norm, w_in=w_in, conv_w=conv_w, conv_b=conv_b, rg_wa=rg_wa, rg_ba=rg_ba, rg_wx=rg_wx, rg_bx=rg_bx, rg_lam=rg_lam, fox_bf=fox_bf, merge_b=merge_b, w_rg=w_rg, w_sb=w_sb, w_fox=w_fox, w_o=w_o, ffn2_norm=ffn2_norm, ffn2_w1=ffn2_w1, ffn2_w3=ffn2_w3, ffn2_w2=ffn2_w2, ada_w=ada_w, ada_b=ada_b, final_norm=final_norm, final_ada_w=final_ada_w, final_ada_b=final_ada_b, loss_target=loss_target, m_ffn1_norm=m_ffn1_norm, m_ffn1_w1=m_ffn1_w1, m_ffn1_w3=m_ffn1_w3, m_ffn1_w2=m_ffn1_w2, m_mix_norm=m_mix_norm, m_w_in=m_w_in, m_conv_w=m_conv_w, m_conv_b=m_conv_b, m_rg_wa=m_rg_wa, m_rg_ba=m_rg_ba, m_rg_wx=m_rg_wx, m_rg_bx=m_rg_bx, m_rg_lam=m_rg_lam, m_fox_bf=m_fox_bf, m_merge_b=m_merge_b, m_w_rg=m_w_rg, m_w_sb=m_w_sb, m_w_fox=m_w_fox, m_w_o=m_w_o, m_ffn2_norm=m_ffn2_norm, m_ffn2_w1=m_ffn2_w1, m_ffn2_w3=m_ffn2_w3, m_ffn2_w2=m_ffn2_w2, m_ada_w=m_ada_w, m_ada_b=m_ada_b, m_final_norm=m_final_norm, m_final_ada_w=m_final_ada_w, m_final_ada_b=m_final_ada_b, v_ffn1_norm=v_ffn1_norm, v_ffn1_w1=v_ffn1_w1, v_ffn1_w3=v_ffn1_w3, v_ffn1_w2=v_ffn1_w2, v_mix_norm=v_mix_norm, v_w_in=v_w_in, v_conv_w=v_conv_w, v_conv_b=v_conv_b, v_rg_wa=v_rg_wa, v_rg_ba=v_rg_ba, v_rg_wx=v_rg_wx, v_rg_bx=v_rg_bx, v_rg_lam=v_rg_lam, v_fox_bf=v_fox_bf, v_merge_b=v_merge_b, v_w_rg=v_w_rg, v_w_sb=v_w_sb, v_w_fox=v_w_fox, v_w_o=v_w_o, v_ffn2_norm=v_ffn2_norm, v_ffn2_w1=v_ffn2_w1, v_ffn2_w3=v_ffn2_w3, v_ffn2_w2=v_ffn2_w2, v_ada_w=v_ada_w, v_ada_b=v_ada_b, v_final_norm=v_final_norm, v_final_ada_w=v_final_ada_w, v_final_ada_b=v_final_ada_b)
    weights = {n: given[n] for n in TWIN_WEIGHTS}
    shared = {n: given[n] for n in SHARED_INPUTS}
    per_example = {n: given[n] for n in ['x', 'c']}
    grad_fn = _jax.value_and_grad(_loss, argnums=(0, 1))

    def one_microbatch(ex, loss_target):
        ex = dict(ex)
        diff = ex.pop(TWIN_DIFF_INPUT)
        return grad_fn(weights, diff, {**shared, **ex}, loss_target)

    if N_MICROBATCH == 1:
        loss, (grad_w, grad_x) = one_microbatch(per_example, given["loss_target"])
    else:
        def body(carry, xs):
            loss_sum, grad_sum = carry
            l_k, (gw_k, gx_k) = one_microbatch(xs[0], xs[1])
            with _jax.named_scope("update"):
                return (loss_sum + l_k, _jax.tree.map(_jnp.add, grad_sum, gw_k)), gx_k

        init = (_jnp.zeros((), _jnp.float32), _jax.tree.map(_jnp.zeros_like, weights))
        (loss, grad_w), grad_x = _jax.lax.scan(body, init, (per_example, given["loss_target"]))
    with _jax.named_scope("update"):
        delta_w, new_m, new_v = {}, {}, {}
        for n in TWIN_WEIGHTS:
            delta_w[n], new_m[n], new_v[n] = _adamw(weights[n], grad_w[n], given["m_" + n], given["v_" + n])
    return (loss, grad_x, *[grad_w[n] for n in TWIN_WEIGHTS], *[delta_w[n] for n in TWIN_WEIGHTS],
            *[new_m[n] for n in TWIN_WEIGHTS], *[new_v[n] for n in TWIN_WEIGHTS])
```

```python
import functools

import numpy as np
import jax
import jax.numpy as jnp
from jax import lax
from jax.experimental import pallas as pl
from jax.experimental.pallas import tpu as pltpu

F32 = jnp.float32
BF16 = jnp.bfloat16
MESH = pl.DeviceIdType.MESH

N_DEV = 8
D = 1024
SEQ = 2048
B_LOC = 2
N_TOK = B_LOC * SEQ
DEPTH = 2
D_FF = 2816
RG_BLOCKS = 16
RG_C = 8.0
N_HEADS = 8
HEAD_DIM = 64
ATT_W = N_HEADS * HEAD_DIM
LANES = 128
EPS = 1e-6
ATT_SCALE = HEAD_DIM ** -0.5
CONV_K = 4

ADAM_LR = 0.001
ADAM_B1 = 0.9
ADAM_B2 = 0.999
ADAM_EPS = 1e-08
ADAM_WD = 0.01
ADAM_STEP = 10

EW_ROWS = 256
ATT_BLK = 256


def _pick_tile(dim, target):
    best = None
    for t in range(LANES, min(dim, target) + 1, LANES):
        if dim % t == 0:
            best = t
    return best if best is not None else dim


_DIMS = {"nn": (((1,), (0,)), ((), ())), "nt": (((1,), (1,)), ((), ())), "tn": (((0,), (0,)), ((), ()))}


def matmul(name, a_list, b_list, mode, out_dtype=F32, tm=512, tn=512):
    if not isinstance(a_list, (list, tuple)):
        a_list, b_list = [a_list], [b_list]
    n = len(a_list)
    m_dim = a_list[0].shape[1] if mode == "tn" else a_list[0].shape[0]
    n_dim = b_list[0].shape[0] if mode == "nt" else b_list[0].shape[1]
    tm, tn = _pick_tile(m_dim, tm), _pick_tile(n_dim, tn)
    dims = _DIMS[mode]

    def body(*refs):
        o_ref = refs[-1]
        acc = None
        for a_ref, b_ref in zip(refs[:n], refs[n:2 * n]):
            d = lax.dot_general(a_ref[...].astype(BF16), b_ref[...].astype(BF16), dims, preferred_element_type=F32)
            acc = d if acc is None else acc + d
        o_ref[...] = acc.astype(o_ref.dtype)

    in_specs = []
    for a in a_list:
        if mode == "tn":
            in_specs.append(pl.BlockSpec((a.shape[0], tm), lambda i, j: (0, i)))
        else:
            in_specs.append(pl.BlockSpec((tm, a.shape[1]), lambda i, j: (i, 0)))
    for b in b_list:
        if mode == "nt":
            in_specs.append(pl.BlockSpec((tn, b.shape[1]), lambda i, j: (j, 0)))
        else:
            in_specs.append(pl.BlockSpec((b.shape[0], tn), lambda i, j: (0, j)))
    return pl.pallas_call(
        body, name=name, grid=(m_dim // tm, n_dim // tn), in_specs=in_specs,
        out_specs=pl.BlockSpec((tm, tn), lambda i, j: (i, j)),
        out_shape=jax.ShapeDtypeStruct((m_dim, n_dim), out_dtype),
        compiler_params=pltpu.CompilerParams(dimension_semantics=("parallel", "parallel")),
    )(*a_list, *b_list)


def _row_spec(w, tm):
    return pl.BlockSpec((tm, w), lambda i: (i, 0))


def _bparam_spec(w, tiles_per_batch):
    return pl.BlockSpec((None, 1, w), lambda i: (i // tiles_per_batch, 0, 0))


def _gparam_spec(shape):
    return pl.BlockSpec(shape, lambda i: (0, 0))


def ew_fwd(name, fn, rows, bparams, gparams, out_widths, out_dtypes, tm=EW_ROWS):
    n_rows = rows[0].shape[0]
    tm = min(tm, n_rows)
    tpb = max(SEQ // tm, 1)
    nr, nb, ng = len(rows), len(bparams), len(gparams)

    def body(*refs):
        vals = [r[...] for r in refs[:nr + nb + ng]]
        outs = fn(*vals)
        if not isinstance(outs, (tuple, list)):
            outs = (outs,)
        for o_ref, o in zip(refs[nr + nb + ng:], outs):
            o_ref[...] = o.astype(o_ref.dtype)

    in_specs = ([_row_spec(r.shape[1], tm) for r in rows] + [_bparam_spec(p.shape[2], tpb) for p in bparams]
                + [_gparam_spec(g.shape) for g in gparams])
    outs = pl.pallas_call(
        body, name=name, grid=(n_rows // tm,), in_specs=in_specs,
        out_specs=[_row_spec(w, tm) for w in out_widths],
        out_shape=[jax.ShapeDtypeStruct((n_rows, w), dt) for w, dt in zip(out_widths, out_dtypes)],
        compiler_params=pltpu.CompilerParams(dimension_semantics=("parallel",)),
    )(*rows, *bparams, *gparams)
    return outs


def ew_bwd(name, fn, rows, bparams, gparams, cts, row_grad_dtypes, adds=(), tm=EW_ROWS):
    n_rows = rows[0].shape[0]
    tm = min(tm, n_rows)
    tpb = max(SEQ // tm, 1)
    nr, nb, ng, nc = len(rows), len(bparams), len(gparams), len(cts)
    adds = list(adds) + [None] * (nr - len(adds))
    add_idx = [k for k in range(nr) if adds[k] is not None]
    want = [k for k in range(nr) if row_grad_dtypes[k] is not None]

    def body(*refs):
        pos = nr + nb + ng
        vals = [r[...] for r in refs[:pos]]
        ct_vals = [r[...].astype(F32) for r in refs[pos:pos + nc]]
        pos += nc
        add_vals = {k: refs[pos + q][...] for q, k in enumerate(add_idx)}
        pos += len(add_idx)
        out_refs = refs[pos:]
        f32_vals = [v.astype(F32) for v in vals]
        outs, vjp = jax.vjp(lambda *a: fn(*a), *f32_vals)
        single = not isinstance(outs, (tuple, list))
        grads = vjp(ct_vals[0].astype(outs.dtype) if single else tuple(c.astype(o.dtype) for c, o in zip(ct_vals, outs)))
        i = pl.program_id(0)
        q = 0
        for k in want:
            g = grads[k]
            if k in add_vals:
                g = g + add_vals[k].astype(F32)
            out_refs[q][...] = g.astype(out_refs[q].dtype)
            q += 1
        for k in range(nb):
            ref = out_refs[q]
            q += 1

            @pl.when(i % tpb == 0)
            def _():
                ref[...] = jnp.zeros_like(ref)

            ref[...] += grads[nr + k]
        for k in range(ng):
            ref = out_refs[q]
            q += 1

            @pl.when(i == 0)
            def _():
                ref[...] = jnp.zeros_like(ref)

            ref[...] += grads[nr + nb + k]

    in_specs = ([_row_spec(r.shape[1], tm) for r in rows] + [_bparam_spec(p.shape[2], tpb) for p in bparams]
                + [_gparam_spec(g.shape) for g in gparams] + [_row_spec(c.shape[1], tm) for c in cts]
                + [_row_spec(adds[k].shape[1], tm) for k in add_idx])
    out_specs = ([_row_spec(rows[k].shape[1], tm) for k in want] + [_bparam_spec(p.shape[2], tpb) for p in bparams]
                 + [_gparam_spec(g.shape) for g in gparams])
    out_shape = ([jax.ShapeDtypeStruct(rows[k].shape, row_grad_dtypes[k]) for k in want]
                 + [jax.ShapeDtypeStruct(p.shape, F32) for p in bparams] + [jax.ShapeDtypeStruct(g.shape, F32) for g in gparams])
    outs = pl.pallas_call(
        body, name=name, grid=(n_rows // tm,), in_specs=in_specs, out_specs=out_specs, out_shape=out_shape,
        compiler_params=pltpu.CompilerParams(dimension_semantics=("arbitrary",)),
    )(*rows, *bparams, *gparams, *cts, *[adds[k] for k in add_idx])
    d_rows = list(outs[:len(want)])
    d_b = list(outs[len(want):len(want) + nb])
    d_g = list(outs[len(want) + nb:])
    return d_rows, d_b, d_g


def f_norm_mod(x, shift, scale, gain):
    x = x.astype(F32)
    y = x * lax.rsqrt(jnp.mean(x * x, axis=-1, keepdims=True) + EPS)
    return (y * gain) * (1.0 + scale) + shift


def f_swiglu(a, b3):
    a = a.astype(F32)
    return (a * jax.nn.sigmoid(a)) * b3.astype(F32)


def f_resid(coef, x, y, gate):
    return x.astype(F32) + (coef * (1.0 + gate)) * y.astype(F32)


def f_rg_gates(pre_r, pre_i, xa, ba, bx, lam):
    r = jax.nn.sigmoid(pre_r + ba)
    i = jax.nn.sigmoid(pre_i + bx)
    softplus_neg_lam = jnp.maximum(-lam, 0.0) + jnp.log(1.0 + jnp.exp(-jnp.abs(lam)))
    log_a = (-RG_C) * r * softplus_neg_lam
    a = jnp.exp(log_a)
    u = jnp.sqrt(1.0 - a * a) * (i * xa)
    return a, u


def f_gelu_mul(gate, hs):
    g = gate.astype(F32)
    gelu = 0.5 * g * (1.0 + jnp.tanh(0.7978845608028654 * (g + 0.044715 * g * g * g)))
    return gelu * hs.astype(F32)


def f_log_sigmoid_bias(f, bf):
    z = f.astype(F32) + bf
    return jnp.minimum(z, 0.0) - jnp.log(1.0 + jnp.exp(-jnp.abs(z)))


def f_merge(mg, pa, pb, pc, merge_b):
    g = jax.nn.sigmoid(mg.astype(F32) + merge_b)
    return g[:, :D] * pa.astype(F32) + g[:, D:2 * D] * pb.astype(F32) + g[:, 2 * D:] * pc.astype(F32)


CONV_CB = 256
SCAN_CB = 512
CUM_RB = 512


def _shift_down(x, d):
    if d == 0:
        return x
    rows = lax.broadcasted_iota(jnp.int32, x.shape, 0)
    return jnp.where(rows >= d, pltpu.roll(x, d, axis=0), 0.0)


def _shift_up(x, d):
    if d == 0:
        return x
    s = x.shape[0]
    rows = lax.broadcasted_iota(jnp.int32, x.shape, 0)
    return jnp.where(rows < s - d, pltpu.roll(x, s - d, axis=0), 0.0)


def conv_fwd(x, w8, b):
    n, c = x.shape
    nb = n // SEQ

    def body(x_ref, w_ref, b_ref, y_ref):
        xv = x_ref[...]
        acc = jnp.broadcast_to(b_ref[...], xv.shape)
        for k in range(CONV_K):
            acc = acc + w_ref[k:k + 1, :] * _shift_down(xv, CONV_K - 1 - k)
        y_ref[...] = acc

    return pl.pallas_call(
        body, name="conv_fwd", grid=(c // CONV_CB, nb),
        in_specs=[pl.BlockSpec((SEQ, CONV_CB), lambda j, i: (i, j)), pl.BlockSpec((8, CONV_CB), lambda j, i: (0, j)),
                  pl.BlockSpec((1, CONV_CB), lambda j, i: (0, j))],
        out_specs=pl.BlockSpec((SEQ, CONV_CB), lambda j, i: (i, j)),
        out_shape=jax.ShapeDtypeStruct((n, c), F32),
        compiler_params=pltpu.CompilerParams(dimension_semantics=("parallel", "parallel")),
    )(x, w8, b)


def conv_bwd(x, w8, dy1, dy2):
    n, c = x.shape
    nb = n // SEQ

    def body(x_ref, w_ref, dy1_ref, dy2_ref, dx_ref, dwb_ref):
        xv = x_ref[...]
        dy = dy1_ref[...] + dy2_ref[...]
        dx = jnp.zeros_like(xv)
        parts = []
        for k in range(CONV_K):
            d = CONV_K - 1 - k
            dx = dx + w_ref[k:k + 1, :] * _shift_up(dy, d)
            parts.append(jnp.sum(dy * _shift_down(xv, d), axis=0, keepdims=True))
        parts.append(jnp.sum(dy, axis=0, keepdims=True))
        parts.append(jnp.zeros((8 - len(parts), xv.shape[1]), F32))
        dx_ref[...] = dx

        @pl.when(pl.program_id(1) == 0)
        def _():
            dwb_ref[...] = jnp.zeros_like(dwb_ref)

        dwb_ref[...] += jnp.concatenate(parts, axis=0)

    return pl.pallas_call(
        body, name="conv_bwd", grid=(c // CONV_CB, nb),
        in_specs=[pl.BlockSpec((SEQ, CONV_CB), lambda j, i: (i, j)), pl.BlockSpec((8, CONV_CB), lambda j, i: (0, j)),
                  pl.BlockSpec((SEQ, CONV_CB), lambda j, i: (i, j)), pl.BlockSpec((SEQ, CONV_CB), lambda j, i: (i, j))],
        out_specs=[pl.BlockSpec((SEQ, CONV_CB), lambda j, i: (i, j)), pl.BlockSpec((8, CONV_CB), lambda j, i: (0, j))],
        out_shape=[jax.ShapeDtypeStruct((n, c), F32), jax.ShapeDtypeStruct((8, c), F32)],
        compiler_params=pltpu.CompilerParams(dimension_semantics=("parallel", "arbitrary")),
    )(x, w8, dy1, dy2)


def scan_fwd(a, u):
    n, c = a.shape

    def body(a_ref, u_ref, h_ref):
        def step(t, h):
            h = a_ref[pl.ds(t, 1), :] * h + u_ref[pl.ds(t, 1), :]
            h_ref[pl.ds(t, 1), :] = h
            return h

        lax.fori_loop(0, SEQ, step, jnp.zeros((1, SCAN_CB), F32), unroll=8)

    spec = pl.BlockSpec((SEQ, SCAN_CB), lambda i, j: (i, j))
    return pl.pallas_call(
        body, name="scan_fwd", grid=(n // SEQ, c // SCAN_CB), in_specs=[spec, spec], out_specs=spec,
        out_shape=jax.ShapeDtypeStruct((n, c), F32),
        compiler_params=pltpu.CompilerParams(dimension_semantics=("parallel", "parallel")),
    )(a, u)


def scan_bwd(a, h, g):
    n, c = a.shape

    def body(a_ref, h_ref, g_ref, da_ref, du_ref):
        def step(k, carry):
            t = SEQ - 1 - k
            dh = g_ref[pl.ds(t, 1), :] + carry
            du_ref[pl.ds(t, 1), :] = dh
            h_prev = jnp.where(t > 0, h_ref[pl.ds(jnp.maximum(t - 1, 0), 1), :], 0.0)
            da_ref[pl.ds(t, 1), :] = dh * h_prev
            return a_ref[pl.ds(t, 1), :] * dh

        lax.fori_loop(0, SEQ, step, jnp.zeros((1, SCAN_CB), F32), unroll=8)

    spec = pl.BlockSpec((SEQ, SCAN_CB), lambda i, j: (i, j))
    return pl.pallas_call(
        body, name="scan_bwd", grid=(n // SEQ, c // SCAN_CB), in_specs=[spec, spec, spec], out_specs=[spec, spec],
        out_shape=[jax.ShapeDtypeStruct((n, c), F32), jax.ShapeDtypeStruct((n, c), F32)],
        compiler_params=pltpu.CompilerParams(dimension_semantics=("parallel", "parallel")),
    )(a, h, g)


def _split3_dot(m, x):
    hi = x.astype(BF16)
    r1 = x - hi.astype(F32)
    mid = r1.astype(BF16)
    lo = (r1 - mid.astype(F32)).astype(BF16)
    dot = functools.partial(jnp.dot, preferred_element_type=F32)
    return dot(m, hi) + dot(m, mid) + dot(m, lo)


def seq_cumsum(name, xs, signs, reverse):
    n, w = xs[0].shape
    nx = len(xs)
    rb = min(CUM_RB, SEQ)

    def body(*refs):
        x = None
        for r, sg in zip(refs[:nx], signs):
            x = sg * r[...] if x is None else x + sg * r[...]
        q0 = pl.program_id(1) * rb
        row = q0 + lax.broadcasted_iota(jnp.int32, (rb, SEQ), 0)
        col = lax.broadcasted_iota(jnp.int32, (rb, SEQ), 1)
        tri = ((col >= row) if reverse else (col <= row)).astype(BF16)
        refs[nx][...] = _split3_dot(tri, x)

    return pl.pallas_call(
        body, name=name, grid=(n // SEQ, SEQ // rb),
        in_specs=[pl.BlockSpec((SEQ, w), lambda i, j: (i, 0)) for _ in xs],
        out_specs=pl.BlockSpec((rb, w), lambda i, j: (i * (SEQ // rb) + j, 0)),
        out_shape=jax.ShapeDtypeStruct((n, w), F32),
        compiler_params=pltpu.CompilerParams(dimension_semantics=("parallel", "parallel")),
    )(*xs)


N_PAIRS = N_HEADS // 2


def _dot_nt(a, b):
    return lax.dot_general(a, b, _DIMS["nt"], preferred_element_type=F32)


def _dot_tn(a, b):
    return lax.dot_general(a, b, _DIMS["tn"], preferred_element_type=F32)


def _dot_nn(a, b):
    return lax.dot_general(a, b, _DIMS["nn"], preferred_element_type=F32)


def _split2_dot(x, m):
    hi = x.astype(BF16)
    lo = (x - hi.astype(F32)).astype(BF16)
    return _dot_nn(hi, m) + _dot_nn(lo, m)


def _head_mask(j):
    lane = lax.broadcasted_iota(jnp.int32, (1, LANES), 1)
    return (lane // HEAD_DIM) == j


def _lane_pick(x, h):
    lane = lax.broadcasted_iota(jnp.int32, x.shape, 1)
    return jnp.sum(jnp.where(lane == h, x, 0.0), axis=1, keepdims=True)


def _lane_put(col, h):
    lane = lax.broadcasted_iota(jnp.int32, (col.shape[0], LANES), 1)
    return jnp.where(lane == h, col, 0.0)


def _softplus(z):
    return jnp.maximum(z, 0.0) + jnp.log(1.0 + jnp.exp(-jnp.abs(z)))


def _qkv_specs():
    return [pl.BlockSpec((SEQ, LANES), lambda b, p: (b, p)),
            pl.BlockSpec((SEQ, LANES), lambda b, p: (b, N_PAIRS + p)),
            pl.BlockSpec((SEQ, LANES), lambda b, p: (b, 2 * N_PAIRS + p))]


def _pair_spec():
    return pl.BlockSpec((SEQ, LANES), lambda b, p: (b, p))


def _sb_scores(qm, kb, q0, k0):
    t = ATT_BLK
    row = lax.broadcasted_iota(jnp.int32, (t, t), 0)
    col = lax.broadcasted_iota(jnp.int32, (t, t), 1)
    z = _dot_nt(qm, kb)
    strict = (k0 + col) < (q0 + row)
    sp = _softplus(z)
    return strict, jnp.where(strict, -sp, 0.0), z - sp


def _key_order(after):
    t = ATT_BLK
    row = lax.broadcasted_iota(jnp.int32, (t, t), 0)
    col = lax.broadcasted_iota(jnp.int32, (t, t), 1)
    return ((row > col) if after else (row <= col)).astype(BF16)


def sb_attn_fwd(qkv):
    n = qkv.shape[0]
    t = ATT_BLK

    def body(q_ref, k_ref, v_ref, o_ref, tot_ref):
        pair = pl.program_id(1)

        @pl.when(pair == 0)
        def _():
            tot_ref[...] = jnp.zeros_like(tot_ref)

        def q_block(qi, _):
            q0 = pl.multiple_of(qi * t, t)
            qb = q_ref[pl.ds(q0, t), :]
            accs = []
            tot_rows = jnp.zeros((t, LANES), F32)
            for j in range(2):
                qm = (jnp.where(_head_mask(j), qb, 0.0) * ATT_SCALE).astype(BF16)

                def k_block(kk, carry):
                    run_l, acc = carry
                    k0 = pl.multiple_of((qi - kk) * t, t)
                    kb = k_ref[pl.ds(k0, t), :].astype(BF16)
                    vb = v_ref[pl.ds(k0, t), :].astype(BF16)
                    strict, log_keep, log_beta = _sb_scores(qm, kb, q0, k0)
                    att = jnp.where(strict, jnp.exp(log_beta + _split2_dot(log_keep, _key_order(True)) + run_l), 0.0)
                    acc = acc + _dot_nn(att.astype(BF16), vb)
                    return run_l + jnp.sum(log_keep, axis=1, keepdims=True), acc

                run_l, acc = lax.fori_loop(0, qi + 1, k_block, (jnp.zeros((t, 1), F32), jnp.zeros((t, LANES), F32)))
                accs.append(acc)
                tot_rows = tot_rows + _lane_put(run_l, 2 * pair + j)
            o_ref[pl.ds(q0, t), :] = jnp.where(_head_mask(0), accs[0], accs[1])
            tot_ref[pl.ds(q0, t), :] += tot_rows
            return 0

        lax.fori_loop(0, SEQ // t, q_block, 0)

    batch_spec = pl.BlockSpec((SEQ, LANES), lambda b, p: (b, 0))
    return pl.pallas_call(
        body, name="sb_attn_fwd", grid=(n // SEQ, N_PAIRS), in_specs=_qkv_specs(), out_specs=[_pair_spec(), batch_spec],
        out_shape=[jax.ShapeDtypeStruct((n, ATT_W), F32), jax.ShapeDtypeStruct((n, LANES), F32)],
        compiler_params=pltpu.CompilerParams(dimension_semantics=("parallel", "arbitrary")),
    )(qkv, qkv, qkv)


def sb_attn_bwd(qkv, tot, do):
    n = qkv.shape[0]
    t = ATT_BLK

    def body(q_ref, k_ref, v_ref, tot_ref, do_ref, dq_ref, dk_ref, dv_ref):
        pair = pl.program_id(1)
        dk_ref[...] = jnp.zeros_like(dk_ref)
        dv_ref[...] = jnp.zeros_like(dv_ref)

        def q_block(qi, _):
            q0 = pl.multiple_of(qi * t, t)
            qb = q_ref[pl.ds(q0, t), :]
            tot_q = tot_ref[pl.ds(q0, t), :]
            dob = do_ref[pl.ds(q0, t), :].astype(F32)
            dqs = []
            for j in range(2):
                hm = _head_mask(j)
                qm = (jnp.where(hm, qb, 0.0) * ATT_SCALE).astype(BF16)
                dom = jnp.where(hm, dob, 0.0).astype(BF16)
                total_l = _lane_pick(tot_q, 2 * pair + j)

                def k_block(ki, carry):
                    run_l, run_g, dq = carry
                    k0 = pl.multiple_of(ki * t, t)
                    kb = k_ref[pl.ds(k0, t), :].astype(BF16)
                    vb = v_ref[pl.ds(k0, t), :].astype(BF16)
                    strict, log_keep, log_beta = _sb_scores(qm, kb, q0, k0)
                    upto = _key_order(False)
                    right_l = total_l - (run_l + _split2_dot(log_keep, upto))
                    att = jnp.where(strict, jnp.exp(log_beta + right_l), 0.0)
                    g = att * _dot_nt(dom, vb)
                    dv_ref[pl.ds(k0, t), :] += _dot_tn(att.astype(BF16), dom)
                    dz = jnp.where(strict, g - jnp.exp(log_beta) * (run_g + _split2_dot(g, upto)), 0.0).astype(BF16)
                    dk_ref[pl.ds(k0, t), :] += _dot_tn(dz, qm)
                    return (run_l + jnp.sum(log_keep, axis=1, keepdims=True), run_g + jnp.sum(g, axis=1, keepdims=True),
                            dq + _dot_nn(dz, kb))

                zero = jnp.zeros((t, 1), F32)
                _, _, dq = lax.fori_loop(0, qi + 1, k_block, (zero, zero, jnp.zeros((t, LANES), F32)))
                dqs.append(dq)
            dq_ref[pl.ds(q0, t), :] = jnp.where(_head_mask(0), dqs[0], dqs[1]) * ATT_SCALE
            return 0

        lax.fori_loop(0, SEQ // t, q_block, 0)

    out = jax.ShapeDtypeStruct((n, ATT_W), F32)
    batch_spec = pl.BlockSpec((SEQ, LANES), lambda b, p: (b, 0))
    return pl.pallas_call(
        body, name="sb_attn_bwd", grid=(n // SEQ, N_PAIRS), in_specs=_qkv_specs() + [batch_spec, _pair_spec()],
        out_specs=[_pair_spec()] * 3, out_shape=[out, out, out],
        compiler_params=pltpu.CompilerParams(dimension_semantics=("parallel", "parallel")),
    )(qkv, qkv, qkv, tot, do)


NEG_BIG = -1e30


def _fox_logits(qm, kb, q0, k0, cq, ck):
    t = ATT_BLK
    row = lax.broadcasted_iota(jnp.int32, (t, t), 0)
    col = lax.broadcasted_iota(jnp.int32, (t, t), 1)
    causal = (k0 + col) <= (q0 + row)
    return causal, _dot_nt(qm, kb) + cq - ck


def fox_attn_fwd(qkv, cum, cum_t):
    n = qkv.shape[0]
    t = ATT_BLK

    def body(q_ref, k_ref, v_ref, cum_ref, cumt_ref, o_ref, lse_ref):
        pair = pl.program_id(1)

        @pl.when(pair == 0)
        def _():
            lse_ref[...] = jnp.zeros_like(lse_ref)

        def q_block(qi, _):
            q0 = pl.multiple_of(qi * t, t)
            qb = q_ref[pl.ds(q0, t), :]
            cum_q = cum_ref[pl.ds(q0, t), :]
            accs = []
            lse_rows = jnp.zeros((t, LANES), F32)
            for j in range(2):
                h = 2 * pair + j
                qm = (jnp.where(_head_mask(j), qb, 0.0) * ATT_SCALE).astype(BF16)
                cq = _lane_pick(cum_q, h)

                def k_block(ki, carry):
                    m, l, acc = carry
                    k0 = pl.multiple_of(ki * t, t)
                    kb = k_ref[pl.ds(k0, t), :].astype(BF16)
                    vb = v_ref[pl.ds(k0, t), :].astype(BF16)
                    ck = cumt_ref[pl.ds(h, 1), pl.ds(k0, t)]
                    causal, z = _fox_logits(qm, kb, q0, k0, cq, ck)
                    z = jnp.where(causal, z, NEG_BIG)
                    m_new = jnp.maximum(m, jnp.max(z, axis=1, keepdims=True))
                    p = jnp.where(causal, jnp.exp(z - m_new), 0.0)
                    alpha = jnp.exp(m - m_new)
                    return m_new, alpha * l + jnp.sum(p, axis=1, keepdims=True), alpha * acc + _dot_nn(p.astype(BF16), vb)

                init = (jnp.full((t, 1), NEG_BIG, F32), jnp.zeros((t, 1), F32), jnp.zeros((t, LANES), F32))
                m, l, acc = lax.fori_loop(0, qi + 1, k_block, init)
                accs.append(acc / l)
                lse_rows = lse_rows + _lane_put(m + jnp.log(l), h)
            o_ref[pl.ds(q0, t), :] = jnp.where(_head_mask(0), accs[0], accs[1])
            lse_ref[pl.ds(q0, t), :] += lse_rows
            return 0

        lax.fori_loop(0, SEQ // t, q_block, 0)

    batch_spec = pl.BlockSpec((SEQ, LANES), lambda b, p: (b, 0))
    return pl.pallas_call(
        body, name="fox_attn_fwd", grid=(n // SEQ, N_PAIRS),
        in_specs=_qkv_specs() + [batch_spec, pl.BlockSpec((None, N_HEADS, SEQ), lambda b, p: (b, 0, 0))],
        out_specs=[_pair_spec(), batch_spec],
        out_shape=[jax.ShapeDtypeStruct((n, ATT_W), F32), jax.ShapeDtypeStruct((n, LANES), F32)],
        compiler_params=pltpu.CompilerParams(dimension_semantics=("parallel", "arbitrary")),
    )(qkv, qkv, qkv, cum, cum_t)


def fox_attn_bwd(qkv, cum, cum_t, lse, o, do):
    n = qkv.shape[0]
    t = ATT_BLK

    def body(q_ref, k_ref, v_ref, cum_ref, cumt_ref, lse_ref, o_ref, do_ref, dq_ref, dk_ref, dv_ref, dcq_ref, dck_ref):
        pair = pl.program_id(1)
        dk_ref[...] = jnp.zeros_like(dk_ref)
        dv_ref[...] = jnp.zeros_like(dv_ref)

        @pl.when(pair == 0)
        def _():
            dcq_ref[...] = jnp.zeros_like(dcq_ref)
            dck_ref[...] = jnp.zeros_like(dck_ref)

        def q_block(qi, _):
            q0 = pl.multiple_of(qi * t, t)
            qb = q_ref[pl.ds(q0, t), :]
            ob = o_ref[pl.ds(q0, t), :]
            dob = do_ref[pl.ds(q0, t), :].astype(F32)
            cum_q = cum_ref[pl.ds(q0, t), :]
            lse_q = lse_ref[pl.ds(q0, t), :]
            dqs = []
            dcq_rows = jnp.zeros((t, LANES), F32)
            for j in range(2):
                h = 2 * pair + j
                hm = _head_mask(j)
                qm = (jnp.where(hm, qb, 0.0) * ATT_SCALE).astype(BF16)
                dom32 = jnp.where(hm, dob, 0.0)
                dom = dom32.astype(BF16)
                delta = jnp.sum(dom32 * ob, axis=1, keepdims=True)
                cq = _lane_pick(cum_q, h)
                lq = _lane_pick(lse_q, h)

                def k_block(ki, carry):
                    dq, dcq = carry
                    k0 = pl.multiple_of(ki * t, t)
                    kb = k_ref[pl.ds(k0, t), :].astype(BF16)
                    vb = v_ref[pl.ds(k0, t), :].astype(BF16)
                    ck = cumt_ref[pl.ds(h, 1), pl.ds(k0, t)]
                    causal, z = _fox_logits(qm, kb, q0, k0, cq, ck)
                    p = jnp.where(causal, jnp.exp(jnp.where(causal, z, NEG_BIG) - lq), 0.0)
                    dv_ref[pl.ds(k0, t), :] += _dot_tn(p.astype(BF16), dom)
                    dz = p * (_dot_nt(dom, vb) - delta)
                    dzb = dz.astype(BF16)
                    dk_ref[pl.ds(k0, t), :] += _dot_tn(dzb, qm)
                    dck_ref[pl.ds(h, 1), pl.ds(k0, t)] += jnp.sum(dz, axis=0, keepdims=True)
                    return dq + _dot_nn(dzb, kb), dcq + jnp.sum(dz, axis=1, keepdims=True)

                dq, dcq = lax.fori_loop(0, qi + 1, k_block, (jnp.zeros((t, LANES), F32), jnp.zeros((t, 1), F32)))
                dqs.append(dq)
                dcq_rows = dcq_rows + _lane_put(dcq, h)
            dq_ref[pl.ds(q0, t), :] = jnp.where(_head_mask(0), dqs[0], dqs[1]) * ATT_SCALE
            dcq_ref[pl.ds(q0, t), :] += dcq_rows
            return 0

        lax.fori_loop(0, SEQ // t, q_block, 0)

    batch_spec = pl.BlockSpec((SEQ, LANES), lambda b, p: (b, 0))
    t_spec = pl.BlockSpec((None, N_HEADS, SEQ), lambda b, p: (b, 0, 0))
    out = jax.ShapeDtypeStruct((n, ATT_W), F32)
    return pl.pallas_call(
        body, name="fox_attn_bwd", grid=(n // SEQ, N_PAIRS),
        in_specs=_qkv_specs() + [batch_spec, t_spec, batch_spec, _pair_spec(), _pair_spec()],
        out_specs=[_pair_spec()] * 3 + [batch_spec, t_spec],
        out_shape=[out, out, out, jax.ShapeDtypeStruct((n, LANES), F32), jax.ShapeDtypeStruct((n // SEQ, N_HEADS, SEQ), F32)],
        compiler_params=pltpu.CompilerParams(dimension_semantics=("parallel", "arbitrary")),
    )(qkv, qkv, qkv, cum, cum_t, lse, o, do)


_HBM = pl.BlockSpec(memory_space=pl.ANY)


def _my_place():
    return lax.axis_index("x"), lax.axis_index("y"), lax.axis_index("c")


def my_index():
    mx, my, mc = _my_place()
    return 4 * mx + 2 * my + mc


def all_gather(name, x):
    def body(x_ref, out_ref, send_sems, recv_sems, local_sem):
        mx, my, mc = _my_place()
        me, sibling = (mx, my, mc), (mx, my, 1 - mc)
        chips = [(1 - mx, my), (mx, 1 - my), (1 - mx, 1 - my)]

        def slot(px, py, pc):
            return out_ref.at[4 * px + 2 * py + pc]

        def copy(k, block, to, src=None):
            return pltpu.make_async_remote_copy(
                src_ref=slot(*block) if src is None else src, dst_ref=slot(*block),
                send_sem=send_sems.at[k], recv_sem=recv_sems.at[k], device_id=to, device_id_type=MESH)

        mine = pltpu.make_async_copy(x_ref, slot(*me), local_sem)
        mine.start()
        first = [copy(0, me, sibling, src=x_ref)]
        first += [copy(1 + j, me, (*chip, mc), src=x_ref) for j, chip in enumerate(chips)]
        for cp in first:
            cp.start()
        passed = [copy(4 + j, (*chip, mc), sibling) for j, chip in enumerate(chips)]
        for j, chip in enumerate(chips):
            copy(1 + j, (*chip, mc), me).wait_recv()
            passed[j].start()
        copy(0, sibling, me).wait_recv()
        for j, chip in enumerate(chips):
            copy(4 + j, (*chip, 1 - mc), me).wait_recv()
        for cp in first + passed:
            cp.wait_send()
        mine.wait()

    return pl.pallas_call(
        body, name=name, in_specs=[_HBM], out_specs=_HBM,
        out_shape=jax.ShapeDtypeStruct((N_DEV,) + x.shape, x.dtype),
        scratch_shapes=[pltpu.SemaphoreType.DMA((7,)), pltpu.SemaphoreType.DMA((7,)), pltpu.SemaphoreType.DMA],
    )(x)


def swap_with_sibling(name, g):
    def body(g_ref, out_ref, send_sems, recv_sems):
        mx, my, mc = _my_place()
        copies = [pltpu.make_async_remote_copy(
            src_ref=g_ref.at[k, 1 - mc], dst_ref=out_ref.at[k], send_sem=send_sems.at[k], recv_sem=recv_sems.at[k],
            device_id=(mx, my, 1 - mc), device_id_type=MESH) for k in range(4)]
        for cp in copies:
            cp.start()
        for cp in copies:
            cp.wait_recv()
        for cp in copies:
            cp.wait_send()

    return pl.pallas_call(
        body, name=name, in_specs=[_HBM], out_specs=_HBM,
        out_shape=jax.ShapeDtypeStruct((4,) + g.shape[2:], g.dtype),
        scratch_shapes=[pltpu.SemaphoreType.DMA((4,)), pltpu.SemaphoreType.DMA((4,))],
    )(g)


def swap_with_chips(name, t4):
    def body(t_ref, out_ref, send_sems, recv_sems):
        mx, my, mc = _my_place()
        chips = [(1 - mx, my), (mx, 1 - my), (1 - mx, 1 - my)]
        copies = [pltpu.make_async_remote_copy(
            src_ref=t_ref.at[2 * px + py], dst_ref=out_ref.at[k], send_sem=send_sems.at[k], recv_sem=recv_sems.at[k],
            device_id=(px, py, mc), device_id_type=MESH) for k, (px, py) in enumerate(chips)]
        for cp in copies:
            cp.start()
        for cp in copies:
            cp.wait_recv()
        for cp in copies:
            cp.wait_send()

    return pl.pallas_call(
        body, name=name, in_specs=[_HBM], out_specs=_HBM,
        out_shape=jax.ShapeDtypeStruct((3,) + t4.shape[1:], t4.dtype),
        scratch_shapes=[pltpu.SemaphoreType.DMA((3,)), pltpu.SemaphoreType.DMA((3,))],
    )(t4)


def _pick_rows(n, target):
    best = None
    for t in range(8, min(n, target) + 1, 8):
        if n % t == 0:
            best = t
    return best if best is not None else n


def add_blocks(name, parts, out_dtype, rows=512):
    r, w = parts[0].shape
    tr = _pick_rows(r, rows)

    def body(*refs):
        acc = refs[0][...].astype(F32)
        for ref in refs[1:-1]:
            acc = acc + ref[...].astype(F32)
        refs[-1][...] = acc.astype(refs[-1].dtype)

    spec = pl.BlockSpec((tr, w), lambda i: (i, 0))
    return pl.pallas_call(
        body, name=name, grid=(r // tr,), in_specs=[spec] * len(parts), out_specs=spec,
        out_shape=jax.ShapeDtypeStruct((r, w), out_dtype),
        compiler_params=pltpu.CompilerParams(dimension_semantics=("parallel",)),
    )(*parts)


def sum_rows(name, x):
    def body(x_ref, o_ref):
        o_ref[...] = jnp.sum(x_ref[...], axis=0, keepdims=True)

    return pl.pallas_call(body, name=name, out_shape=jax.ShapeDtypeStruct((1, x.shape[1]), F32))(x)


def reduce_scatter(g8):
    _, r, w = g8.shape
    mx, my, mc = _my_place()
    g = g8.reshape(4, 2, r, w)
    from_sibling = swap_with_sibling("rs_sibling", g)
    mine = lax.dynamic_index_in_dim(g, mc, axis=1, keepdims=False)
    chip_sum = add_blocks("rs_chip_sum", [mine.reshape(4 * r, w), from_sibling.reshape(4 * r, w)], BF16).reshape(4, r, w)
    from_chips = swap_with_chips("rs_chips", chip_sum)
    own = lax.dynamic_index_in_dim(chip_sum, 2 * mx + my, axis=0, keepdims=False)
    return add_blocks("rs_total", [own, from_chips[0], from_chips[1], from_chips[2]], F32)


def _pack(arrays, width, row_mult, dtype, lead=0):
    parts, metas = [], []
    for a in arrays:
        lead_shape = a.shape[:lead]
        size = int(np.prod(a.shape[lead:]))
        chunk = row_mult * width
        padded = -(-size // chunk) * chunk
        flat = a.astype(dtype).reshape(lead_shape + (size,))
        if padded != size:
            flat = jnp.pad(flat, [(0, 0)] * lead + [(0, padded - size)])
        parts.append(flat.reshape(lead_shape + (padded // width, width)))
        metas.append((a.shape[lead:], size, padded // width))
    return jnp.concatenate(parts, axis=lead), metas


def _unpack(slab, metas, lead=0):
    out, r0 = [], 0
    for shape, size, rows in metas:
        part = lax.slice_in_dim(slab, r0, r0 + rows, axis=lead)
        lead_shape = part.shape[:lead]
        flat = part.reshape(lead_shape + (rows * part.shape[-1],))
        out.append(lax.slice_in_dim(flat, 0, size, axis=lead).reshape(lead_shape + tuple(shape)))
        r0 += rows
    return out


def _f_adamw(w, g, m, v):
    m = ADAM_B1 * m + (1.0 - ADAM_B1) * g
    v = ADAM_B2 * v + (1.0 - ADAM_B2) * (g * g)
    m_hat = m / (1.0 - ADAM_B1 ** ADAM_STEP)
    v_hat = v / (1.0 - ADAM_B2 ** ADAM_STEP)
    delta = (-ADAM_LR) * (m_hat / (jnp.sqrt(v_hat) + ADAM_EPS) + ADAM_WD * w)
    return delta, m, v


def adamw(name, w, g, m, v):
    shape = w.shape
    w2 = shape[-1]
    flat = [a.reshape(-1, w2) for a in (w, g, m, v)]
    tm = _pick_rows(flat[0].shape[0], 256)
    outs = ew_fwd(name, _f_adamw, flat, [], [], [w2] * 3, [F32] * 3, tm=tm)
    return [o.reshape(shape) for o in outs]


WEIGHTS = ["ffn1_norm", "ffn1_w1", "ffn1_w3", "ffn1_w2", "mix_norm", "w_in", "conv_w", "conv_b", "rg_wa", "rg_ba", "rg_wx",
           "rg_bx", "rg_lam", "fox_bf", "merge_b", "w_rg", "w_sb", "w_fox", "w_o", "ffn2_norm", "ffn2_w1", "ffn2_w3",
           "ffn2_w2", "ada_w", "ada_b", "final_norm", "final_ada_w", "final_ada_b"]
GATHERED = {"ffn1_w1": 2, "ffn1_w3": 2, "ffn1_w2": 1, "w_in": 2, "w_rg": 1, "w_sb": 2, "w_fox": 2, "w_o": 1,
            "ffn2_w1": 2, "ffn2_w3": 2, "ffn2_w2": 1}
REPLICATED = ["ffn1_norm", "mix_norm", "conv_b", "rg_wa", "rg_ba", "rg_wx", "rg_bx", "rg_lam", "fox_bf", "merge_b",
              "ffn2_norm", "final_norm"]
IN_CUTS = (0, 1024, 2048, 3584, 5120, 5128, 8200)


def _unshard(g, axis):
    g = jnp.moveaxis(g, 0, axis)
    shape = g.shape
    return g.reshape(shape[:axis] + (shape[axis] * shape[axis + 1],) + shape[axis + 2:])


def _reshard(full, axis):
    shape = full.shape
    g = full.reshape(shape[:axis] + (N_DEV, shape[axis] // N_DEV) + shape[axis + 1:])
    return jnp.moveaxis(g, axis, 0)


def _block_diag(w):
    nb, bd, _ = w.shape
    eye = jnp.eye(nb, dtype=bool)[:, None, :, None]
    return jnp.where(eye, w[:, :, None, :], 0.0).reshape(nb * bd, nb * bd)


def _diag_blocks(m, nb=RG_BLOCKS):
    bd = m.shape[0] // nb
    return jnp.stack([m[k * bd:(k + 1) * bd, k * bd:(k + 1) * bd] for k in range(nb)])


def _pad_lanes(a, width=LANES):
    return jnp.pad(a, [(0, 0)] * (a.ndim - 1) + [(0, width - a.shape[-1])])


def _bp(m, k, which):
    return m[:, k, which][:, None, :]


def _f_silu(c):
    return c * jax.nn.sigmoid(c)


def _f_add_bias(a, b):
    return a + b


def _ffn_fwd(tag, x, shift, scale, gate, gain, w1, w3, w2):
    h = ew_fwd(f"ffn_norm_{tag}", f_norm_mod, [x], [shift, scale], [gain], [D], [BF16])[0]
    a = matmul(f"ffn_up1_{tag}", h, w1, "nn", tn=1408)
    b3 = matmul(f"ffn_up3_{tag}", h, w3, "nn", tn=1408)
    s = ew_fwd(f"ffn_act_{tag}", f_swiglu, [a, b3], [], [], [D_FF], [BF16])[0]
    y = matmul(f"ffn_down_{tag}", s, w2, "nn")
    xo = ew_fwd(f"ffn_res_{tag}", functools.partial(f_resid, 0.5), [x, y], [gate], [], [D], [F32])[0]
    return xo, (x, h, a, b3, s, y)


def _ffn_bwd(tag, dxo, saved, shift, scale, gate, gain, w1, w3, w2):
    x, h, a, b3, s, y = saved
    (dy,), (dgate,), _ = ew_bwd(f"ffn_res_bwd_{tag}", functools.partial(f_resid, 0.5), [x, y], [gate], [], [dxo], [None, BF16])
    ds = matmul(f"ffn_down_dx_{tag}", dy, w2, "nt", tn=1408)
    (da, db3), _, _ = ew_bwd(f"ffn_act_bwd_{tag}", f_swiglu, [a, b3], [], [], [ds], [BF16, BF16])
    dw2 = matmul(f"ffn_dw2_{tag}", s, dy, "tn", tm=256)
    dw1 = matmul(f"ffn_dw1_{tag}", h, da, "tn", tm=256)
    dw3 = matmul(f"ffn_dw3_{tag}", h, db3, "tn", tm=256)
    dh = matmul(f"ffn_up_dx_{tag}", [da, db3], [w1, w3], "nt")
    (dx,), (dshift, dscale), (dgain,) = ew_bwd(f"ffn_norm_bwd_{tag}", f_norm_mod, [x], [shift, scale], [gain], [dh], [F32],
                                               adds=[dxo])
    return dx, (dshift, dscale, dgate), dgain, dw1, dw3, dw2


def _mixer_fwd(tag, x, shift, scale, gate, p):
    h = ew_fwd(f"mix_norm_{tag}", f_norm_mod, [x], [shift, scale], [p["gain"]], [D], [BF16])[0]
    rgx = matmul(f"in_rgx_{tag}", h, p["w_rgx"], "nn")
    rgate = matmul(f"in_gate_{tag}", h, p["w_gate"], "nn")
    sbqkv = matmul(f"in_sb_{tag}", h, p["w_sbqkv"], "nn")
    foxqkv = matmul(f"in_fox_{tag}", h, p["w_foxqkv"], "nn")
    ff = matmul(f"in_forget_{tag}", h, p["w_f"], "nn")
    mg = matmul(f"in_merge_{tag}", h, p["w_merge"], "nn")
    xa = conv_fwd(rgx, p["conv_w8"], p["conv_b"])
    pre_r = matmul(f"rg_a_{tag}", xa, p["wa_bd"], "nn")
    pre_i = matmul(f"rg_x_{tag}", xa, p["wx_bd"], "nn")
    a, u = ew_fwd(f"rg_gates_{tag}", f_rg_gates, [pre_r, pre_i, xa], [], [p["ba"], p["bx"], p["lam"]], [D, D], [F32, F32])
    hs = scan_fwd(a, u)
    ya = ew_fwd(f"rg_out_{tag}", f_gelu_mul, [rgate, hs], [], [], [D], [BF16])[0]
    yb, sb_tot = sb_attn_fwd(sbqkv)
    lf = ew_fwd(f"fox_logf_{tag}", f_log_sigmoid_bias, [ff], [], [p["bf"]], [LANES], [F32])[0]
    cum = seq_cumsum(f"fox_cum_{tag}", [lf], [1.0], False)
    cum_t = cum.reshape(-1, SEQ, LANES)[:, :, :N_HEADS].transpose(0, 2, 1)
    yc, lse = fox_attn_fwd(foxqkv, cum, cum_t)
    pa = matmul(f"out_rg_{tag}", ya, p["w_rg"], "nn")
    pb = matmul(f"out_sb_{tag}", yb, p["w_sb"], "nn")
    pc = matmul(f"out_fox_{tag}", yc, p["w_fox"], "nn")
    mixed = ew_fwd(f"merge_{tag}", f_merge, [mg, pa, pb, pc], [], [p["merge_b"]], [D], [BF16])[0]
    y = matmul(f"out_o_{tag}", mixed, p["w_o"], "nn")
    xo = ew_fwd(f"mix_res_{tag}", functools.partial(f_resid, 1.0), [x, y], [gate], [], [D], [F32])[0]
    saved = dict(x=x, h=h, rgx=rgx, rgate=rgate, sbqkv=sbqkv, foxqkv=foxqkv, ff=ff, mg=mg, xa=xa, pre_r=pre_r, pre_i=pre_i,
                 a=a, hs=hs, ya=ya, yb=yb, sb_tot=sb_tot, cum=cum, cum_t=cum_t, yc=yc, lse=lse, pa=pa, pb=pb, pc=pc,
                 mixed=mixed, y=y)
    return xo, saved


def _mixer_bwd(tag, dxo, s, shift, scale, gate, p):
    (dy,), (dgate,), _ = ew_bwd(f"mix_res_bwd_{tag}", functools.partial(f_resid, 1.0), [s["x"], s["y"]], [gate], [], [dxo],
                                [None, BF16])
    dmixed = matmul(f"out_o_dx_{tag}", dy, p["w_o"], "nt")
    g = {"w_o": matmul(f"out_o_dw_{tag}", s["mixed"], dy, "tn", tm=256)}
    (dmg, dpa, dpb, dpc), _, (g["merge_b"],) = ew_bwd(
        f"merge_bwd_{tag}", f_merge, [s["mg"], s["pa"], s["pb"], s["pc"]], [], [p["merge_b"]], [dmixed], [BF16] * 4)
    dya = matmul(f"out_rg_dx_{tag}", dpa, p["w_rg"], "nt")
    g["w_rg"] = matmul(f"out_rg_dw_{tag}", s["ya"], dpa, "tn", tm=256)
    dyb = matmul(f"out_sb_dx_{tag}", dpb, p["w_sb"], "nt", out_dtype=BF16)
    g["w_sb"] = matmul(f"out_sb_dw_{tag}", s["yb"], dpb, "tn", tm=256)
    dyc = matmul(f"out_fox_dx_{tag}", dpc, p["w_fox"], "nt", out_dtype=BF16)
    g["w_fox"] = matmul(f"out_fox_dw_{tag}", s["yc"], dpc, "tn", tm=256)
    dq_c, dk_c, dv_c, dcq, dck = fox_attn_bwd(s["foxqkv"], s["cum"], s["cum_t"], s["lse"], s["yc"], dyc)
    dck_rows = _pad_lanes(dck.transpose(0, 2, 1).reshape(-1, N_HEADS))
    dlf = seq_cumsum(f"fox_cum_bwd_{tag}", [dcq, dck_rows], [1.0, -1.0], True)
    (dff,), _, (dbf,) = ew_bwd(f"fox_logf_bwd_{tag}", f_log_sigmoid_bias, [s["ff"]], [], [p["bf"]], [dlf], [BF16])
    g["fox_bf"] = dbf[0, :N_HEADS]
    dq_b, dk_b, dv_b = sb_attn_bwd(s["sbqkv"], s["sb_tot"], dyb)
    (drgate, dhs), _, _ = ew_bwd(f"rg_out_bwd_{tag}", f_gelu_mul, [s["rgate"], s["hs"]], [], [], [dya], [BF16, F32])
    da, du = scan_bwd(s["a"], s["hs"], dhs)
    (dpre_r, dpre_i, dxa1), _, (g["rg_ba"], g["rg_bx"], g["rg_lam"]) = ew_bwd(
        f"rg_gates_bwd_{tag}", f_rg_gates, [s["pre_r"], s["pre_i"], s["xa"]], [], [p["ba"], p["bx"], p["lam"]], [da, du],
        [BF16, BF16, F32])
    dxa2 = matmul(f"rg_dx_{tag}", [dpre_r, dpre_i], [p["wa_bd"], p["wx_bd"]], "nt")
    g["rg_wa"] = _diag_blocks(matmul(f"rg_a_dw_{tag}", s["xa"], dpre_r, "tn", tm=256))
    g["rg_wx"] = _diag_blocks(matmul(f"rg_x_dw_{tag}", s["xa"], dpre_i, "tn", tm=256))
    drgx, dwb = conv_bwd(s["rgx"], p["conv_w8"], dxa1, dxa2)
    g["conv_w"] = dwb[:CONV_K]
    g["conv_b"] = dwb[CONV_K]
    cots = [drgx, drgate, dq_b, dk_b, dv_b, dq_c, dk_c, dv_c, dff, dmg]
    w_sb3 = [p["w_sbqkv"][:, k * ATT_W:(k + 1) * ATT_W] for k in range(3)]
    w_fox3 = [p["w_foxqkv"][:, k * ATT_W:(k + 1) * ATT_W] for k in range(3)]
    ws = [p["w_rgx"], p["w_gate"]] + w_sb3 + w_fox3 + [p["w_f"], p["w_merge"]]
    dh = matmul(f"in_dx_{tag}", cots, ws, "nt", tm=256)
    dws = [matmul(f"in_dw{k}_{tag}", s["h"], ct, "tn", tm=256) for k, ct in enumerate(cots)]
    dws[8] = dws[8][:, :N_HEADS]
    g["w_in"] = jnp.concatenate(dws, axis=1)
    (dx,), (dshift, dscale), (g["mix_norm"],) = ew_bwd(f"mix_norm_bwd_{tag}", f_norm_mod, [s["x"]], [shift, scale], [p["gain"]],
                                                       [dh], [F32], adds=[dxo])
    return dx, (dshift, dscale, dgate), g


def _final_loss(x, target, shift, scale, gain):
    n = x.shape[0]
    tm = EW_ROWS
    tpb = SEQ // tm

    def body(x_ref, t_ref, sh_ref, sc_ref, g_ref, loss_ref, dx_ref, dsh_ref, dsc_ref, dg_ref):
        i = pl.program_id(0)
        out, vjp = jax.vjp(f_norm_mod, x_ref[...], sh_ref[...], sc_ref[...], g_ref[...])
        diff = out - t_ref[...]
        dx, dsh, dsc, dg = vjp(diff * (1.0 / D))
        dx_ref[...] = dx
        sq = jnp.sum(jnp.sum(diff * diff, axis=1, keepdims=True), axis=0, keepdims=True)

        @pl.when(i % tpb == 0)
        def _():
            dsh_ref[...] = jnp.zeros_like(dsh_ref)
            dsc_ref[...] = jnp.zeros_like(dsc_ref)

        @pl.when(i == 0)
        def _():
            dg_ref[...] = jnp.zeros_like(dg_ref)
            loss_ref[...] = jnp.zeros_like(loss_ref)

        dsh_ref[...] += dsh
        dsc_ref[...] += dsc
        dg_ref[...] += dg
        loss_ref[...] += jnp.broadcast_to(sq, (1, LANES)) * (0.5 / D)

    row, bp, gp = _row_spec(D, tm), _bparam_spec(D, tpb), _gparam_spec((1, D))
    return pl.pallas_call(
        body, name="final_loss", grid=(n // tm,), in_specs=[row, row, bp, bp, gp],
        out_specs=[_gparam_spec((1, LANES)), row, bp, bp, gp],
        out_shape=[jax.ShapeDtypeStruct((1, LANES), F32), jax.ShapeDtypeStruct((n, D), F32),
                   jax.ShapeDtypeStruct(shift.shape, F32), jax.ShapeDtypeStruct(scale.shape, F32),
                   jax.ShapeDtypeStruct((1, D), F32)],
        compiler_params=pltpu.CompilerParams(dimension_semantics=("arbitrary",)),
    )(x, target, shift, scale, gain)


def kernel(x, c, ffn1_norm, ffn1_w1, ffn1_w3, ffn1_w2, mix_norm, w_in, conv_w, conv_b, rg_wa, rg_ba, rg_wx, rg_bx, rg_lam, fox_bf, merge_b, w_rg, w_sb, w_fox, w_o, ffn2_norm, ffn2_w1, ffn2_w3, ffn2_w2, ada_w, ada_b, final_norm, final_ada_w, final_ada_b, loss_target, m_ffn1_norm, m_ffn1_w1, m_ffn1_w3, m_ffn1_w2, m_mix_norm, m_w_in, m_conv_w, m_conv_b, m_rg_wa, m_rg_ba, m_rg_wx, m_rg_bx, m_rg_lam, m_fox_bf, m_merge_b, m_w_rg, m_w_sb, m_w_fox, m_w_o, m_ffn2_norm, m_ffn2_w1, m_ffn2_w3, m_ffn2_w2, m_ada_w, m_ada_b, m_final_norm, m_final_ada_w, m_final_ada_b, v_ffn1_norm, v_ffn1_w1, v_ffn1_w3, v_ffn1_w2, v_mix_norm, v_w_in, v_conv_w, v_conv_b, v_rg_wa, v_rg_ba, v_rg_wx, v_rg_bx, v_rg_lam, v_fox_bf, v_merge_b, v_w_rg, v_w_sb, v_w_fox, v_w_o, v_ffn2_norm, v_ffn2_w1, v_ffn2_w3, v_ffn2_w2, v_ada_w, v_ada_b, v_final_norm, v_final_ada_w, v_final_ada_b):
    given = dict(zip(["x", "c"] + WEIGHTS + ["loss_target"] + ["m_" + n for n in WEIGHTS] + ["v_" + n for n in WEIGHTS],
                     (x, c, ffn1_norm, ffn1_w1, ffn1_w3, ffn1_w2, mix_norm, w_in, conv_w, conv_b, rg_wa, rg_ba, rg_wx, rg_bx, rg_lam, fox_bf, merge_b, w_rg, w_sb, w_fox, w_o, ffn2_norm, ffn2_w1, ffn2_w3, ffn2_w2, ada_w, ada_b, final_norm, final_ada_w, final_ada_b, loss_target, m_ffn1_norm, m_ffn1_w1, m_ffn1_w3, m_ffn1_w2, m_mix_norm, m_w_in, m_conv_w, m_conv_b, m_rg_wa, m_rg_ba, m_rg_wx, m_rg_bx, m_rg_lam, m_fox_bf, m_merge_b, m_w_rg, m_w_sb, m_w_fox, m_w_o, m_ffn2_norm, m_ffn2_w1, m_ffn2_w3, m_ffn2_w2, m_ada_w, m_ada_b, m_final_norm, m_final_ada_w, m_final_ada_b, v_ffn1_norm, v_ffn1_w1, v_ffn1_w3, v_ffn1_w2, v_mix_norm, v_w_in, v_conv_w, v_conv_b, v_rg_wa, v_rg_ba, v_rg_wx, v_rg_bx, v_rg_lam, v_fox_bf, v_merge_b, v_w_rg, v_w_sb, v_w_fox, v_w_o, v_ffn2_norm, v_ffn2_w1, v_ffn2_w3, v_ffn2_w2, v_ada_w, v_ada_b, v_final_norm, v_final_ada_w, v_final_ada_b)))
    idx = my_index()
    n_batch = N_DEV * B_LOC
    ada_cols = ada_w.shape[2]
    fin_cols = final_ada_w.shape[1]

    small_in, small_in_meta = _pack([c, conv_w], LANES, 8, F32)
    c_parts, conv_w_parts = _unpack(all_gather("gather_c_conv", small_in), small_in_meta, lead=1)
    c_all = c_parts.reshape(n_batch, D)
    conv_w_all = _unshard(conv_w_parts, 2)
    c_act = ew_fwd("c_silu", _f_silu, [c_all], [], [], [D], [F32])[0]
    mod_cols = [matmul(f"ada_proj_{l}", c_act, ada_w[l], "nn") for l in range(DEPTH)]
    mod_cols.append(matmul("ada_proj_final", c_act, final_ada_w, "nn"))
    mod_g = all_gather("gather_mod", jnp.concatenate(mod_cols, axis=1))
    mods = []
    for l in range(DEPTH):
        full = mod_g[:, :, l * ada_cols:(l + 1) * ada_cols].transpose(1, 0, 2).reshape(n_batch, N_DEV * ada_cols)
        full = ew_fwd(f"ada_bias_{l}", _f_add_bias, [full], [], [ada_b[l][None]], [full.shape[1]], [F32])[0]
        mods.append(lax.dynamic_slice_in_dim(full, idx * B_LOC, B_LOC, axis=0).reshape(B_LOC, 3, 3, D))
    fm = mod_g[:, :, DEPTH * ada_cols:].transpose(1, 0, 2).reshape(n_batch, N_DEV * fin_cols)
    fm = ew_fwd("ada_bias_final", _f_add_bias, [fm], [], [final_ada_b[None]], [fm.shape[1]], [F32])[0]
    fm = lax.dynamic_slice_in_dim(fm, idx * B_LOC, B_LOC, axis=0).reshape(B_LOC, 2, D)

    names = list(GATHERED)
    slab, slab_meta = _pack([given[n] for n in names], D, 16, BF16)
    gathered = _unpack(all_gather("gather_weights", slab), slab_meta, lead=1)
    full_w = {n: _unshard(g, GATHERED[n]) for n, g in zip(names, gathered)}

    def layer_params(l):
        wi = full_w["w_in"][l]
        cut = IN_CUTS
        return dict(
            gain=mix_norm[l][None], w_rgx=wi[:, cut[0]:cut[1]], w_gate=wi[:, cut[1]:cut[2]], w_sbqkv=wi[:, cut[2]:cut[3]],
            w_foxqkv=wi[:, cut[3]:cut[4]], w_f=_pad_lanes(wi[:, cut[4]:cut[5]]), w_merge=wi[:, cut[5]:cut[6]],
            conv_w8=jnp.pad(conv_w_all[l], ((0, 8 - CONV_K), (0, 0))), conv_b=conv_b[l][None],
            wa_bd=_block_diag(rg_wa[l]), wx_bd=_block_diag(rg_wx[l]), ba=rg_ba[l][None], bx=rg_bx[l][None], lam=rg_lam[l][None],
            bf=_pad_lanes(fox_bf[l][None]), merge_b=merge_b[l][None], w_rg=full_w["w_rg"][l], w_sb=full_w["w_sb"][l],
            w_fox=full_w["w_fox"][l], w_o=full_w["w_o"][l])

    n_tok = x.shape[0] * x.shape[1]
    h = x.reshape(n_tok, D)
    saved = []
    for l in range(DEPTH):
        m = mods[l]
        p = layer_params(l)
        h, s1 = _ffn_fwd(f"a{l}", h, _bp(m, 0, 0), _bp(m, 0, 1), _bp(m, 0, 2), ffn1_norm[l][None], full_w["ffn1_w1"][l],
                         full_w["ffn1_w3"][l], full_w["ffn1_w2"][l])
        h, s2 = _mixer_fwd(f"{l}", h, _bp(m, 1, 0), _bp(m, 1, 1), _bp(m, 1, 2), p)
        h, s3 = _ffn_fwd(f"b{l}", h, _bp(m, 2, 0), _bp(m, 2, 1), _bp(m, 2, 2), ffn2_norm[l][None], full_w["ffn2_w1"][l],
                         full_w["ffn2_w3"][l], full_w["ffn2_w2"][l])
        saved.append((s1, s2, s3, p))
    loss_row, dh, dfshift, dfscale, dgain_final = _final_loss(h, loss_target.reshape(n_tok, D), fm[:, 0][:, None, :],
                                                              fm[:, 1][:, None, :], final_norm[None])

    grads = {n: [None] * DEPTH for n in WEIGHTS}
    d_mods = [None] * DEPTH
    for l in reversed(range(DEPTH)):
        m = mods[l]
        s1, s2, s3, p = saved[l]
        dh, dm3, grads["ffn2_norm"][l], grads["ffn2_w1"][l], grads["ffn2_w3"][l], grads["ffn2_w2"][l] = _ffn_bwd(
            f"b{l}", dh, s3, _bp(m, 2, 0), _bp(m, 2, 1), _bp(m, 2, 2), ffn2_norm[l][None], full_w["ffn2_w1"][l],
            full_w["ffn2_w3"][l], full_w["ffn2_w2"][l])
        dh, dm2, gm = _mixer_bwd(f"{l}", dh, s2, _bp(m, 1, 0), _bp(m, 1, 1), _bp(m, 1, 2), p)
        for n, gval in gm.items():
            grads[n][l] = gval
        dh, dm1, grads["ffn1_norm"][l], grads["ffn1_w1"][l], grads["ffn1_w3"][l], grads["ffn1_w2"][l] = _ffn_bwd(
            f"a{l}", dh, s1, _bp(m, 0, 0), _bp(m, 0, 1), _bp(m, 0, 2), ffn1_norm[l][None], full_w["ffn1_w1"][l],
            full_w["ffn1_w3"][l], full_w["ffn1_w2"][l])
        d_mods[l] = jnp.concatenate([t.reshape(B_LOC, D) for dm in (dm1, dm2, dm3) for t in dm], axis=1)
    grad_x = dh.reshape(x.shape)
    d_fm = jnp.concatenate([dfshift.reshape(B_LOC, D), dfscale.reshape(B_LOC, D)], axis=1)

    rep = {n: jnp.stack([t.reshape(given[n].shape[1:]) for t in grads[n]]) for n in REPLICATED if n != "final_norm"}
    rep["final_norm"] = dgain_final.reshape(D)
    rep["conv_w"] = jnp.stack(grads["conv_w"])
    rep_names = list(rep)
    rep_slab, rep_meta = _pack([rep[n] for n in rep_names], LANES, 8, F32)
    mod_slab, mod_meta = _pack(d_mods + [d_fm], LANES, 8, F32)
    small_g = all_gather("gather_small_grads", jnp.concatenate([mod_slab, rep_slab], axis=0))
    d_mod_all = [t.reshape(n_batch, -1) for t in _unpack(small_g[:, :mod_slab.shape[0]], mod_meta, lead=1)]
    rep_sum = add_blocks("sum_small_grads", [small_g[k, mod_slab.shape[0]:] for k in range(N_DEV)], F32)
    rep_grad = dict(zip(rep_names, _unpack(rep_sum, rep_meta)))
    final_g = {n: rep_grad[n] for n in REPLICATED}
    final_g["conv_w"] = lax.dynamic_slice_in_dim(rep_grad["conv_w"], idx * conv_w.shape[2], conv_w.shape[2], axis=2)
    final_g["ada_b"] = jnp.stack([sum_rows(f"ada_b_grad_{l}", d_mod_all[l])[0] for l in range(DEPTH)])
    final_g["final_ada_b"] = sum_rows("final_ada_b_grad", d_mod_all[DEPTH])[0]
    final_g["ada_w"] = jnp.stack([
        matmul(f"ada_w_grad_{l}", c_act, lax.dynamic_slice_in_dim(d_mod_all[l], idx * ada_cols, ada_cols, axis=1), "tn")
        for l in range(DEPTH)])
    final_g["final_ada_w"] = matmul(
        "final_ada_w_grad", c_act, lax.dynamic_slice_in_dim(d_mod_all[DEPTH], idx * fin_cols, fin_cols, axis=1), "tn")

    g_slab, g_meta = _pack([_reshard(jnp.stack(grads[n]), GATHERED[n]) for n in names], D, 16, BF16, lead=1)
    for n, gval in zip(names, _unpack(reduce_scatter(g_slab), slab_meta)):
        final_g[n] = gval

    delta, new_m, new_v = {}, {}, {}
    sharded = names + ["ada_w", "final_ada_w", "conv_w"]
    for n in sharded:
        delta[n], new_m[n], new_v[n] = adamw(f"adamw_{n}", given[n], final_g[n], given["m_" + n], given["v_" + n])
    rep_all = [n for n in WEIGHTS if n not in sharded]
    packed = [_pack([src[n] for n in rep_all], LANES, 8, F32)[0]
              for src in (given, final_g, {n: given["m_" + n] for n in rep_all}, {n: given["v_" + n] for n in rep_all})]
    rep_meta_all = _pack([given[n] for n in rep_all], LANES, 8, F32)[1]
    for store, slab_out in zip((delta, new_m, new_v), adamw("adamw_replicated", *packed)):
        store.update(zip(rep_all, _unpack(slab_out, rep_meta_all)))

    loss = lax.psum(loss_row[0, 0], ("x", "y", "c"))
    return (loss, grad_x, *[final_g[n] for n in WEIGHTS], *[delta[n] for n in WEIGHTS], *[new_m[n] for n in WEIGHTS],
            *[new_v[n] for n in WEIGHTS])
```

```python
import functools

import numpy as np
import jax
import jax.numpy as jnp
from jax import lax
from jax.experimental import pallas as pl
from jax.experimental.pallas import tpu as pltpu

F32 = jnp.float32
BF16 = jnp.bfloat16
MESH = pl.DeviceIdType.MESH

N_DEV = 8
D = 1024
SEQ = 2048
B_LOC = 2
N_TOK = B_LOC * SEQ
DEPTH = 2
D_FF = 2816
RG_BLOCKS = 16
RG_C = 8.0
N_HEADS = 8
HEAD_DIM = 64
ATT_W = N_HEADS * HEAD_DIM
LANES = 128
EPS = 1e-6
ATT_SCALE = HEAD_DIM ** -0.5
CONV_K = 4

ADAM_LR = 0.001
ADAM_B1 = 0.9
ADAM_B2 = 0.999
ADAM_EPS = 1e-08
ADAM_WD = 0.01
ADAM_STEP = 10

EW_ROWS = 256
ATT_BLK = 256


def _pick_tile(dim, target):
    best = None
    for t in range(LANES, min(dim, target) + 1, LANES):
        if dim % t == 0:
            best = t
    return best if best is not None else dim


_DIMS = {"nn": (((1,), (0,)), ((), ())), "nt": (((1,), (1,)), ((), ())), "tn": (((0,), (0,)), ((), ()))}


def matmul(name, a_list, b_list, mode, out_dtype=F32, tm=512, tn=512):
    if not isinstance(a_list, (list, tuple)):
        a_list, b_list = [a_list], [b_list]
    n = len(a_list)
    m_dim = a_list[0].shape[1] if mode == "tn" else a_list[0].shape[0]
    n_dim = b_list[0].shape[0] if mode == "nt" else b_list[0].shape[1]
    tm, tn = _pick_tile(m_dim, tm), _pick_tile(n_dim, tn)
    dims = _DIMS[mode]

    def body(*refs):
        o_ref = refs[-1]
        acc = None
        for a_ref, b_ref in zip(refs[:n], refs[n:2 * n]):
            d = lax.dot_general(a_ref[...].astype(BF16), b_ref[...].astype(BF16), dims, preferred_element_type=F32)
            acc = d if acc is None else acc + d
        o_ref[...] = acc.astype(o_ref.dtype)

    in_specs = []
    for a in a_list:
        if mode == "tn":
            in_specs.append(pl.BlockSpec((a.shape[0], tm), lambda i, j: (0, i)))
        else:
            in_specs.append(pl.BlockSpec((tm, a.shape[1]), lambda i, j: (i, 0)))
    for b in b_list:
        if mode == "nt":
            in_specs.append(pl.BlockSpec((tn, b.shape[1]), lambda i, j: (j, 0)))
        else:
            in_specs.append(pl.BlockSpec((b.shape[0], tn), lambda i, j: (0, j)))
    return pl.pallas_call(
        body, name=name, grid=(m_dim // tm, n_dim // tn), in_specs=in_specs,
        out_specs=pl.BlockSpec((tm, tn), lambda i, j: (i, j)),
        out_shape=jax.ShapeDtypeStruct((m_dim, n_dim), out_dtype),
        compiler_params=pltpu.CompilerParams(dimension_semantics=("parallel", "parallel")),
    )(*a_list, *b_list)


def _row_spec(w, tm):
    return pl.BlockSpec((tm, w), lambda i: (i, 0))


def _bparam_spec(w, tiles_per_batch):
    return pl.BlockSpec((None, 1, w), lambda i: (i // tiles_per_batch, 0, 0))


def _gparam_spec(shape):
    return pl.BlockSpec(shape, lambda i: (0, 0))


def ew_fwd(name, fn, rows, bparams, gparams, out_widths, out_dtypes, tm=EW_ROWS):
    n_rows = rows[0].shape[0]
    tm = min(tm, n_rows)
    tpb = max(SEQ // tm, 1)
    nr, nb, ng = len(rows), len(bparams), len(gparams)

    def body(*refs):
        vals = [r[...] for r in refs[:nr + nb + ng]]
        outs = fn(*vals)
        if not isinstance(outs, (tuple, list)):
            outs = (outs,)
        for o_ref, o in zip(refs[nr + nb + ng:], outs):
            o_ref[...] = o.astype(o_ref.dtype)

    in_specs = ([_row_spec(r.shape[1], tm) for r in rows] + [_bparam_spec(p.shape[2], tpb) for p in bparams]
                + [_gparam_spec(g.shape) for g in gparams])
    outs = pl.pallas_call(
        body, name=name, grid=(n_rows // tm,), in_specs=in_specs,
        out_specs=[_row_spec(w, tm) for w in out_widths],
        out_shape=[jax.ShapeDtypeStruct((n_rows, w), dt) for w, dt in zip(out_widths, out_dtypes)],
        compiler_params=pltpu.CompilerParams(dimension_semantics=("parallel",)),
    )(*rows, *bparams, *gparams)
    return outs


def ew_bwd(name, fn, rows, bparams, gparams, cts, row_grad_dtypes, adds=(), tm=EW_ROWS):
    n_rows = rows[0].shape[0]
    tm = min(tm, n_rows)
    tpb = max(SEQ // tm, 1)
    nr, nb, ng, nc = len(rows), len(bparams), len(gparams), len(cts)
    adds = list(adds) + [None] * (nr - len(adds))
    add_idx = [k for k in range(nr) if adds[k] is not None]
    want = [k for k in range(nr) if row_grad_dtypes[k] is not None]

    def body(*refs):
        pos = nr + nb + ng
        vals = [r[...] for r in refs[:pos]]
        ct_vals = [r[...].astype(F32) for r in refs[pos:pos + nc]]
        pos += nc
        add_vals = {k: refs[pos + q][...] for q, k in enumerate(add_idx)}
        pos += len(add_idx)
        out_refs = refs[pos:]
        f32_vals = [v.astype(F32) for v in vals]
        outs, vjp = jax.vjp(lambda *a: fn(*a), *f32_vals)
        single = not isinstance(outs, (tuple, list))
        grads = vjp(ct_vals[0].astype(outs.dtype) if single else tuple(c.astype(o.dtype) for c, o in zip(ct_vals, outs)))
        i = pl.program_id(0)
        q = 0
        for k in want:
            g = grads[k]
            if k in add_vals:
                g = g + add_vals[k].astype(F32)
            out_refs[q][...] = g.astype(out_refs[q].dtype)
            q += 1
        for k in range(nb):
            ref = out_refs[q]
            q += 1

            @pl.when(i % tpb == 0)
            def _():
                ref[...] = jnp.zeros_like(ref)

            ref[...] += grads[nr + k]
        for k in range(ng):
            ref = out_refs[q]
            q += 1

            @pl.when(i == 0)
            def _():
                ref[...] = jnp.zeros_like(ref)

            ref[...] += grads[nr + nb + k]

    in_specs = ([_row_spec(r.shape[1], tm) for r in rows] + [_bparam_spec(p.shape[2], tpb) for p in bparams]
                + [_gparam_spec(g.shape) for g in gparams] + [_row_spec(c.shape[1], tm) for c in cts]
                + [_row_spec(adds[k].shape[1], tm) for k in add_idx])
    out_specs = ([_row_spec(rows[k].shape[1], tm) for k in want] + [_bparam_spec(p.shape[2], tpb) for p in bparams]
                 + [_gparam_spec(g.shape) for g in gparams])
    out_shape = ([jax.ShapeDtypeStruct(rows[k].shape, row_grad_dtypes[k]) for k in want]
                 + [jax.ShapeDtypeStruct(p.shape, F32) for p in bparams] + [jax.ShapeDtypeStruct(g.shape, F32) for g in gparams])
    outs = pl.pallas_call(
        body, name=name, grid=(n_rows // tm,), in_specs=in_specs, out_specs=out_specs, out_shape=out_shape,
        compiler_params=pltpu.CompilerParams(dimension_semantics=("arbitrary",)),
    )(*rows, *bparams, *gparams, *cts, *[adds[k] for k in add_idx])
    d_rows = list(outs[:len(want)])
    d_b = list(outs[len(want):len(want) + nb])
    d_g = list(outs[len(want) + nb:])
    return d_rows, d_b, d_g


def f_norm_mod(x, shift, scale, gain):
    x = x.astype(F32)
    y = x * lax.rsqrt(jnp.mean(x * x, axis=-1, keepdims=True) + EPS)
    return (y * gain) * (1.0 + scale) + shift


def f_swiglu(a, b3):
    a = a.astype(F32)
    return (a * jax.nn.sigmoid(a)) * b3.astype(F32)


def f_resid(coef, x, y, gate):
    return x.astype(F32) + (coef * (1.0 + gate)) * y.astype(F32)


def f_rg_gates(pre_r, pre_i, xa, ba, bx, lam):
    r = jax.nn.sigmoid(pre_r + ba)
    i = jax.nn.sigmoid(pre_i + bx)
    softplus_neg_lam = jnp.maximum(-lam, 0.0) + jnp.log(1.0 + jnp.exp(-jnp.abs(lam)))
    log_a = (-RG_C) * r * softplus_neg_lam
    a = jnp.exp(log_a)
    u = jnp.sqrt(1.0 - a * a) * (i * xa)
    return a, u


def f_gelu_mul(gate, hs):
    g = gate.astype(F32)
    gelu = 0.5 * g * (1.0 + jnp.tanh(0.7978845608028654 * (g + 0.044715 * g * g * g)))
    return gelu * hs.astype(F32)


def f_log_sigmoid_bias(f, bf):
    z = f.astype(F32) + bf
    return jnp.minimum(z, 0.0) - jnp.log(1.0 + jnp.exp(-jnp.abs(z)))


def f_merge(mg, pa, pb, pc, merge_b):
    g = jax.nn.sigmoid(mg.astype(F32) + merge_b)
    return g[:, :D] * pa.astype(F32) + g[:, D:2 * D] * pb.astype(F32) + g[:, 2 * D:] * pc.astype(F32)


CONV_CB = 256
SCAN_CB = 512
CUM_RB = 512


def _shift_down(x, d):
    if d == 0:
        return x
    rows = lax.broadcasted_iota(jnp.int32, x.shape, 0)
    return jnp.where(rows >= d, pltpu.roll(x, d, axis=0), 0.0)


def _shift_up(x, d):
    if d == 0:
        return x
    s = x.shape[0]
    rows = lax.broadcasted_iota(jnp.int32, x.shape, 0)
    return jnp.where(rows < s - d, pltpu.roll(x, s - d, axis=0), 0.0)


def conv_fwd(x, w8, b):
    n, c = x.shape
    nb = n // SEQ

    def body(x_ref, w_ref, b_ref, y_ref):
        xv = x_ref[...]
        acc = jnp.broadcast_to(b_ref[...], xv.shape)
        for k in range(CONV_K):
            acc = acc + w_ref[k:k + 1, :] * _shift_down(xv, CONV_K - 1 - k)
        y_ref[...] = acc

    return pl.pallas_call(
        body, name="conv_fwd", grid=(c // CONV_CB, nb),
        in_specs=[pl.BlockSpec((SEQ, CONV_CB), lambda j, i: (i, j)), pl.BlockSpec((8, CONV_CB), lambda j, i: (0, j)),
                  pl.BlockSpec((1, CONV_CB), lambda j, i: (0, j))],
        out_specs=pl.BlockSpec((SEQ, CONV_CB), lambda j, i: (i, j)),
        out_shape=jax.ShapeDtypeStruct((n, c), F32),
        compiler_params=pltpu.CompilerParams(dimension_semantics=("parallel", "parallel")),
    )(x, w8, b)


def conv_bwd(x, w8, dy1, dy2):
    n, c = x.shape
    nb = n // SEQ

    def body(x_ref, w_ref, dy1_ref, dy2_ref, dx_ref, dwb_ref):
        xv = x_ref[...]
        dy = dy1_ref[...] + dy2_ref[...]
        dx = jnp.zeros_like(xv)
        parts = []
        for k in range(CONV_K):
            d = CONV_K - 1 - k
            dx = dx + w_ref[k:k + 1, :] * _shift_up(dy, d)
            parts.append(jnp.sum(dy * _shift_down(xv, d), axis=0, keepdims=True))
        parts.append(jnp.sum(dy, axis=0, keepdims=True))
        parts.append(jnp.zeros((8 - len(parts), xv.shape[1]), F32))
        dx_ref[...] = dx

        @pl.when(pl.program_id(1) == 0)
        def _():
            dwb_ref[...] = jnp.zeros_like(dwb_ref)

        dwb_ref[...] += jnp.concatenate(parts, axis=0)

    return pl.pallas_call(
        body, name="conv_bwd", grid=(c // CONV_CB, nb),
        in_specs=[pl.BlockSpec((SEQ, CONV_CB), lambda j, i: (i, j)), pl.BlockSpec((8, CONV_CB), lambda j, i: (0, j)),
                  pl.BlockSpec((SEQ, CONV_CB), lambda j, i: (i, j)), pl.BlockSpec((SEQ, CONV_CB), lambda j, i: (i, j))],
        out_specs=[pl.BlockSpec((SEQ, CONV_CB), lambda j, i: (i, j)), pl.BlockSpec((8, CONV_CB), lambda j, i: (0, j))],
        out_shape=[jax.ShapeDtypeStruct((n, c), F32), jax.ShapeDtypeStruct((8, c), F32)],
        compiler_params=pltpu.CompilerParams(dimension_semantics=("parallel", "arbitrary")),
    )(x, w8, dy1, dy2)


def scan_fwd(a, u):
    n, c = a.shape

    def body(a_ref, u_ref, h_ref):
        def step(t, h):
            h = a_ref[pl.ds(t, 1), :] * h + u_ref[pl.ds(t, 1), :]
            h_ref[pl.ds(t, 1), :] = h
            return h

        lax.fori_loop(0, SEQ, step, jnp.zeros((1, SCAN_CB), F32), unroll=8)

    spec = pl.BlockSpec((SEQ, SCAN_CB), lambda i, j: (i, j))
    return pl.pallas_call(
        body, name="scan_fwd", grid=(n // SEQ, c // SCAN_CB), in_specs=[spec, spec], out_specs=spec,
        out_shape=jax.ShapeDtypeStruct((n, c), F32),
        compiler_params=pltpu.CompilerParams(dimension_semantics=("parallel", "parallel")),
    )(a, u)


def scan_bwd(a, h, g):
    n, c = a.shape

    def body(a_ref, h_ref, g_ref, da_ref, du_ref):
        def step(k, carry):
            t = SEQ - 1 - k
            dh = g_ref[pl.ds(t, 1), :] + carry
            du_ref[pl.ds(t, 1), :] = dh
            h_prev = jnp.where(t > 0, h_ref[pl.ds(jnp.maximum(t - 1, 0), 1), :], 0.0)
            da_ref[pl.ds(t, 1), :] = dh * h_prev
            return a_ref[pl.ds(t, 1), :] * dh

        lax.fori_loop(0, SEQ, step, jnp.zeros((1, SCAN_CB), F32), unroll=8)

    spec = pl.BlockSpec((SEQ, SCAN_CB), lambda i, j: (i, j))
    return pl.pallas_call(
        body, name="scan_bwd", grid=(n // SEQ, c // SCAN_CB), in_specs=[spec, spec, spec], out_specs=[spec, spec],
        out_shape=[jax.ShapeDtypeStruct((n, c), F32), jax.ShapeDtypeStruct((n, c), F32)],
        compiler_params=pltpu.CompilerParams(dimension_semantics=("parallel", "parallel")),
    )(a, h, g)


def _split3_dot(m, x):
    hi = x.astype(BF16)
    r1 = x - hi.astype(F32)
    mid = r1.astype(BF16)
    lo = (r1 - mid.astype(F32)).astype(BF16)
    dot = functools.partial(jnp.dot, preferred_element_type=F32)
    return dot(m, hi) + dot(m, mid) + dot(m, lo)


def seq_cumsum(name, xs, signs, reverse):
    n, w = xs[0].shape
    nx = len(xs)
    rb = min(CUM_RB, SEQ)

    def body(*refs):
        x = None
        for r, sg in zip(refs[:nx], signs):
            x = sg * r[...] if x is None else x + sg * r[...]
        q0 = pl.program_id(1) * rb
        row = q0 + lax.broadcasted_iota(jnp.int32, (rb, SEQ), 0)
        col = lax.broadcasted_iota(jnp.int32, (rb, SEQ), 1)
        tri = ((col >= row) if reverse else (col <= row)).astype(BF16)
        refs[nx][...] = _split3_dot(tri, x)

    return pl.pallas_call(
        body, name=name, grid=(n // SEQ, SEQ // rb),
        in_specs=[pl.BlockSpec((SEQ, w), lambda i, j: (i, 0)) for _ in xs],
        out_specs=pl.BlockSpec((rb, w), lambda i, j: (i * (SEQ // rb) + j, 0)),
        out_shape=jax.ShapeDtypeStruct((n, w), F32),
        compiler_params=pltpu.CompilerParams(dimension_semantics=("parallel", "parallel")),
    )(*xs)


N_PAIRS = N_HEADS // 2


def _dot_nt(a, b):
    return lax.dot_general(a, b, _DIMS["nt"], preferred_element_type=F32)


def _dot_tn(a, b):
    return lax.dot_general(a, b, _DIMS["tn"], preferred_element_type=F32)


def _dot_nn(a, b):
    return lax.dot_general(a, b, _DIMS["nn"], preferred_element_type=F32)


def _split2_dot(x, m):
    hi = x.astype(BF16)
    lo = (x - hi.astype(F32)).astype(BF16)
    return _dot_nn(hi, m) + _dot_nn(lo, m)


def _head_mask(j):
    lane = lax.broadcasted_iota(jnp.int32, (1, LANES), 1)
    return (lane // HEAD_DIM) == j


def _lane_pick(x, h):
    lane = lax.broadcasted_iota(jnp.int32, x.shape, 1)
    return jnp.sum(jnp.where(lane == h, x, 0.0), axis=1, keepdims=True)


def _lane_put(col, h):
    lane = lax.broadcasted_iota(jnp.int32, (col.shape[0], LANES), 1)
    return jnp.where(lane == h, col, 0.0)


def _softplus(z):
    return jnp.maximum(z, 0.0) + jnp.log(1.0 + jnp.exp(-jnp.abs(z)))


def _qkv_specs():
    return [pl.BlockSpec((SEQ, LANES), lambda b, p: (b, p)),
            pl.BlockSpec((SEQ, LANES), lambda b, p: (b, N_PAIRS + p)),
            pl.BlockSpec((SEQ, LANES), lambda b, p: (b, 2 * N_PAIRS + p))]


def _pair_spec():
    return pl.BlockSpec((SEQ, LANES), lambda b, p: (b, p))


def _below_diagonal(strictly):
    t = ATT_BLK
    row = lax.broadcasted_iota(jnp.int32, (t, t), 0)
    col = lax.broadcasted_iota(jnp.int32, (t, t), 1)
    return (row > col) if strictly else (row >= col)


def _over_key_blocks(qi, step, init, reverse):
    t = ATT_BLK
    q0 = pl.multiple_of(qi * t, t)

    def off_diagonal(kk, carry):
        ki = (qi - 1 - kk) if reverse else kk
        return step(pl.multiple_of(ki * t, t), carry, False)

    if reverse:
        return lax.fori_loop(0, qi, off_diagonal, step(q0, init, True))
    return step(q0, lax.fori_loop(0, qi, off_diagonal, init), True)


def _masked_q(qb, j):
    return (jnp.where(_head_mask(j), qb, 0.0) * ATT_SCALE).astype(BF16)


def sb_attn_fwd(qkv):
    n = qkv.shape[0]
    t = ATT_BLK

    def body(q_ref, k_ref, v_ref, o_ref, tot_ref):
        pair = pl.program_id(1)
        strict = _below_diagonal(True)
        later = strict.astype(BF16)

        @pl.when(pair == 0)
        def _():
            tot_ref[...] = jnp.zeros_like(tot_ref)

        def q_block(qi, _):
            q0 = pl.multiple_of(qi * t, t)
            qb = q_ref[pl.ds(q0, t), :]
            qms = [_masked_q(qb, j) for j in range(2)]

            def step(k0, carry, diagonal):
                kb = k_ref[pl.ds(k0, t), :].astype(BF16)
                vb = v_ref[pl.ds(k0, t), :].astype(BF16)
                new = []
                for j in range(2):
                    run_l, acc = carry[j]
                    z = _dot_nt(qms[j], kb)
                    sp = _softplus(z)
                    log_keep = jnp.where(strict, -sp, 0.0) if diagonal else -sp
                    att = jnp.exp((z - sp) + _split2_dot(log_keep, later) + run_l)
                    if diagonal:
                        att = jnp.where(strict, att, 0.0)
                    new.append((run_l + jnp.sum(log_keep, axis=1, keepdims=True), acc + _dot_nn(att.astype(BF16), vb)))
                return tuple(new)

            init = ((jnp.zeros((t, 1), F32), jnp.zeros((t, LANES), F32)),) * 2
            (tot0, acc0), (tot1, acc1) = _over_key_blocks(qi, step, init, reverse=True)
            o_ref[pl.ds(q0, t), :] = jnp.where(_head_mask(0), acc0, acc1)
            tot_ref[pl.ds(q0, t), :] += _lane_put(tot0, 2 * pair) + _lane_put(tot1, 2 * pair + 1)
            return 0

        lax.fori_loop(0, SEQ // t, q_block, 0)

    batch_spec = pl.BlockSpec((SEQ, LANES), lambda b, p: (b, 0))
    return pl.pallas_call(
        body, name="sb_attn_fwd", grid=(n // SEQ, N_PAIRS), in_specs=_qkv_specs(), out_specs=[_pair_spec(), batch_spec],
        out_shape=[jax.ShapeDtypeStruct((n, ATT_W), F32), jax.ShapeDtypeStruct((n, LANES), F32)],
        compiler_params=pltpu.CompilerParams(dimension_semantics=("parallel", "arbitrary")),
    )(qkv, qkv, qkv)


def sb_attn_bwd(qkv, tot, do):
    n = qkv.shape[0]
    t = ATT_BLK

    def body(q_ref, k_ref, v_ref, tot_ref, do_ref, dq_ref, dk_ref, dv_ref):
        pair = pl.program_id(1)
        strict = _below_diagonal(True)
        upto = jnp.logical_not(strict).astype(BF16)
        dk_ref[...] = jnp.zeros_like(dk_ref)
        dv_ref[...] = jnp.zeros_like(dv_ref)

        def q_block(qi, _):
            q0 = pl.multiple_of(qi * t, t)
            qb = q_ref[pl.ds(q0, t), :]
            tot_q = tot_ref[pl.ds(q0, t), :]
            dob = do_ref[pl.ds(q0, t), :].astype(F32)
            qms = [_masked_q(qb, j) for j in range(2)]
            doms = [jnp.where(_head_mask(j), dob, 0.0).astype(BF16) for j in range(2)]
            totals = [_lane_pick(tot_q, 2 * pair + j) for j in range(2)]

            def step(k0, carry, diagonal):
                kb = k_ref[pl.ds(k0, t), :].astype(BF16)
                vb = v_ref[pl.ds(k0, t), :].astype(BF16)
                new = []
                dk = jnp.zeros((t, LANES), F32)
                dv = jnp.zeros((t, LANES), F32)
                for j in range(2):
                    run_l, run_g, dq = carry[j]
                    z = _dot_nt(qms[j], kb)
                    sp = _softplus(z)
                    log_keep = jnp.where(strict, -sp, 0.0) if diagonal else -sp
                    log_beta = z - sp
                    att = jnp.exp(log_beta + (totals[j] - (run_l + _split2_dot(log_keep, upto))))
                    if diagonal:
                        att = jnp.where(strict, att, 0.0)
                    g = att * _dot_nt(doms[j], vb)
                    dv = dv + _dot_tn(att.astype(BF16), doms[j])
                    dz = g - jnp.exp(log_beta) * (run_g + _split2_dot(g, upto))
                    if diagonal:
                        dz = jnp.where(strict, dz, 0.0)
                    dz = dz.astype(BF16)
                    dk = dk + _dot_tn(dz, qms[j])
                    new.append((run_l + jnp.sum(log_keep, axis=1, keepdims=True), run_g + jnp.sum(g, axis=1, keepdims=True),
                                dq + _dot_nn(dz, kb)))
                dk_ref[pl.ds(k0, t), :] += dk
                dv_ref[pl.ds(k0, t), :] += dv
                return tuple(new)

            zero = jnp.zeros((t, 1), F32)
            init = ((zero, zero, jnp.zeros((t, LANES), F32)),) * 2
            (_, _, dq0), (_, _, dq1) = _over_key_blocks(qi, step, init, reverse=False)
            dq_ref[pl.ds(q0, t), :] = jnp.where(_head_mask(0), dq0, dq1) * ATT_SCALE
            return 0

        lax.fori_loop(0, SEQ // t, q_block, 0)

    out = jax.ShapeDtypeStruct((n, ATT_W), F32)
    batch_spec = pl.BlockSpec((SEQ, LANES), lambda b, p: (b, 0))
    return pl.pallas_call(
        body, name="sb_attn_bwd", grid=(n // SEQ, N_PAIRS), in_specs=_qkv_specs() + [batch_spec, _pair_spec()],
        out_specs=[_pair_spec()] * 3, out_shape=[out, out, out],
        compiler_params=pltpu.CompilerParams(dimension_semantics=("parallel", "parallel")),
    )(qkv, qkv, qkv, tot, do)


NEG_BIG = -1e30


def fox_attn_fwd(qkv, cum, cum_t):
    n = qkv.shape[0]
    t = ATT_BLK

    def body(q_ref, k_ref, v_ref, cum_ref, cumt_ref, o_ref, lse_ref):
        pair = pl.program_id(1)
        causal = _below_diagonal(False)

        @pl.when(pair == 0)
        def _():
            lse_ref[...] = jnp.zeros_like(lse_ref)

        def q_block(qi, _):
            q0 = pl.multiple_of(qi * t, t)
            qb = q_ref[pl.ds(q0, t), :]
            cum_q = cum_ref[pl.ds(q0, t), :]
            qms = [_masked_q(qb, j) for j in range(2)]
            cqs = [_lane_pick(cum_q, 2 * pair + j) for j in range(2)]

            def step(k0, carry, diagonal):
                kb = k_ref[pl.ds(k0, t), :].astype(BF16)
                vb = v_ref[pl.ds(k0, t), :].astype(BF16)
                new = []
                for j in range(2):
                    m, l, acc = carry[j]
                    z = _dot_nt(qms[j], kb) + cqs[j] - cumt_ref[pl.ds(2 * pair + j, 1), pl.ds(k0, t)]
                    if diagonal:
                        z = jnp.where(causal, z, NEG_BIG)
                    m_new = jnp.maximum(m, jnp.max(z, axis=1, keepdims=True))
                    p = jnp.exp(z - m_new)
                    alpha = jnp.exp(m - m_new)
                    new.append((m_new, alpha * l + jnp.sum(p, axis=1, keepdims=True), alpha * acc + _dot_nn(p.astype(BF16), vb)))
                return tuple(new)

            init = ((jnp.full((t, 1), NEG_BIG, F32), jnp.zeros((t, 1), F32), jnp.zeros((t, LANES), F32)),) * 2
            (m0, l0, acc0), (m1, l1, acc1) = _over_key_blocks(qi, step, init, reverse=False)
            o_ref[pl.ds(q0, t), :] = jnp.where(_head_mask(0), acc0 / l0, acc1 / l1)
            lse_ref[pl.ds(q0, t), :] += _lane_put(m0 + jnp.log(l0), 2 * pair) + _lane_put(m1 + jnp.log(l1), 2 * pair + 1)
            return 0

        lax.fori_loop(0, SEQ // t, q_block, 0)

    batch_spec = pl.BlockSpec((SEQ, LANES), lambda b, p: (b, 0))
    return pl.pallas_call(
        body, name="fox_attn_fwd", grid=(n // SEQ, N_PAIRS),
        in_specs=_qkv_specs() + [batch_spec, pl.BlockSpec((None, N_HEADS, SEQ), lambda b, p: (b, 0, 0))],
        out_specs=[_pair_spec(), batch_spec],
        out_shape=[jax.ShapeDtypeStruct((n, ATT_W), F32), jax.ShapeDtypeStruct((n, LANES), F32)],
        compiler_params=pltpu.CompilerParams(dimension_semantics=("parallel", "arbitrary")),
    )(qkv, qkv, qkv, cum, cum_t)


def fox_attn_bwd(qkv, cum, cum_t, lse, o, do):
    n = qkv.shape[0]
    t = ATT_BLK

    def body(q_ref, k_ref, v_ref, cum_ref, cumt_ref, lse_ref, o_ref, do_ref, dq_ref, dk_ref, dv_ref, dcq_ref, dck_ref):
        pair = pl.program_id(1)
        causal = _below_diagonal(False)
        dk_ref[...] = jnp.zeros_like(dk_ref)
        dv_ref[...] = jnp.zeros_like(dv_ref)

        @pl.when(pair == 0)
        def _():
            dcq_ref[...] = jnp.zeros_like(dcq_ref)
            dck_ref[...] = jnp.zeros_like(dck_ref)

        def q_block(qi, _):
            q0 = pl.multiple_of(qi * t, t)
            qb = q_ref[pl.ds(q0, t), :]
            ob = o_ref[pl.ds(q0, t), :]
            dob = do_ref[pl.ds(q0, t), :].astype(F32)
            cum_q = cum_ref[pl.ds(q0, t), :]
            lse_q = lse_ref[pl.ds(q0, t), :]
            qms = [_masked_q(qb, j) for j in range(2)]
            dom32 = [jnp.where(_head_mask(j), dob, 0.0) for j in range(2)]
            doms = [d.astype(BF16) for d in dom32]
            deltas = [jnp.sum(d * ob, axis=1, keepdims=True) for d in dom32]
            cqs = [_lane_pick(cum_q, 2 * pair + j) for j in range(2)]
            lqs = [_lane_pick(lse_q, 2 * pair + j) for j in range(2)]

            def step(k0, carry, diagonal):
                kb = k_ref[pl.ds(k0, t), :].astype(BF16)
                vb = v_ref[pl.ds(k0, t), :].astype(BF16)
                new = []
                dk = jnp.zeros((t, LANES), F32)
                dv = jnp.zeros((t, LANES), F32)
                for j in range(2):
                    dq, dcq = carry[j]
                    z = _dot_nt(qms[j], kb) + cqs[j] - cumt_ref[pl.ds(2 * pair + j, 1), pl.ds(k0, t)]
                    if diagonal:
                        z = jnp.where(causal, z, NEG_BIG)
                    p = jnp.exp(z - lqs[j])
                    dv = dv + _dot_tn(p.astype(BF16), doms[j])
                    dz = p * (_dot_nt(doms[j], vb) - deltas[j])
                    dzb = dz.astype(BF16)
                    dk = dk + _dot_tn(dzb, qms[j])
                    dck_ref[pl.ds(2 * pair + j, 1), pl.ds(k0, t)] += jnp.sum(dz, axis=0, keepdims=True)
                    new.append((dq + _dot_nn(dzb, kb), dcq + jnp.sum(dz, axis=1, keepdims=True)))
                dk_ref[pl.ds(k0, t), :] += dk
                dv_ref[pl.ds(k0, t), :] += dv
                return tuple(new)

            init = ((jnp.zeros((t, LANES), F32), jnp.zeros((t, 1), F32)),) * 2
            (dq0, dcq0), (dq1, dcq1) = _over_key_blocks(qi, step, init, reverse=False)
            dq_ref[pl.ds(q0, t), :] = jnp.where(_head_mask(0), dq0, dq1) * ATT_SCALE
            dcq_ref[pl.ds(q0, t), :] += _lane_put(dcq0, 2 * pair) + _lane_put(dcq1, 2 * pair + 1)
            return 0

        lax.fori_loop(0, SEQ // t, q_block, 0)

    batch_spec = pl.BlockSpec((SEQ, LANES), lambda b, p: (b, 0))
    t_spec = pl.BlockSpec((None, N_HEADS, SEQ), lambda b, p: (b, 0, 0))
    out = jax.ShapeDtypeStruct((n, ATT_W), F32)
    return pl.pallas_call(
        body, name="fox_attn_bwd", grid=(n // SEQ, N_PAIRS),
        in_specs=_qkv_specs() + [batch_spec, t_spec, batch_spec, _pair_spec(), _pair_spec()],
        out_specs=[_pair_spec()] * 3 + [batch_spec, t_spec],
        out_shape=[out, out, out, jax.ShapeDtypeStruct((n, LANES), F32), jax.ShapeDtypeStruct((n // SEQ, N_HEADS, SEQ), F32)],
        compiler_params=pltpu.CompilerParams(dimension_semantics=("parallel", "arbitrary")),
    )(qkv, qkv, qkv, cum, cum_t, lse, o, do)


_HBM = pl.BlockSpec(memory_space=pl.ANY)


def _my_place():
    return lax.axis_index("x"), lax.axis_index("y"), lax.axis_index("c")


def my_index():
    mx, my, mc = _my_place()
    return 4 * mx + 2 * my + mc


def all_gather(name, xs):
    single = not isinstance(xs, (list, tuple))
    xs = [xs] if single else list(xs)
    na = len(xs)

    def body(*refs):
        x_refs, out_refs = refs[:na], refs[na:2 * na]
        send_sems, recv_sems, local_sems = refs[2 * na:]
        mx, my, mc = _my_place()
        me, sibling = (mx, my, mc), (mx, my, 1 - mc)
        chips = [(1 - mx, my), (mx, 1 - my), (1 - mx, 1 - my)]

        def slot(a, px, py, pc):
            return out_refs[a].at[4 * px + 2 * py + pc]

        def copy(a, k, block, to, src=None):
            return pltpu.make_async_remote_copy(
                src_ref=slot(a, *block) if src is None else src, dst_ref=slot(a, *block),
                send_sem=send_sems.at[7 * a + k], recv_sem=recv_sems.at[7 * a + k], device_id=to, device_id_type=MESH)

        mine = [pltpu.make_async_copy(x_refs[a], slot(a, *me), local_sems.at[a]) for a in range(na)]
        for cp in mine:
            cp.start()
        first = []
        for j, chip in enumerate(chips):
            first += [copy(a, 1 + j, me, (*chip, mc), src=x_refs[a]) for a in range(na)]
        first += [copy(a, 0, me, sibling, src=x_refs[a]) for a in range(na)]
        for cp in first:
            cp.start()
        passed = []
        for j, chip in enumerate(chips):
            for a in range(na):
                copy(a, 1 + j, (*chip, mc), me).wait_recv()
                passed.append(copy(a, 4 + j, (*chip, mc), sibling))
                passed[-1].start()
        for a in range(na):
            copy(a, 0, sibling, me).wait_recv()
        for j, chip in enumerate(chips):
            for a in range(na):
                copy(a, 4 + j, (*chip, 1 - mc), me).wait_recv()
        for cp in first + passed:
            cp.wait_send()
        for cp in mine:
            cp.wait()

    outs = pl.pallas_call(
        body, name=name, in_specs=[_HBM] * na, out_specs=[_HBM] * na,
        out_shape=[jax.ShapeDtypeStruct((N_DEV,) + x.shape, x.dtype) for x in xs],
        scratch_shapes=[pltpu.SemaphoreType.DMA((7 * na,)), pltpu.SemaphoreType.DMA((7 * na,)), pltpu.SemaphoreType.DMA((na,))],
    )(*xs)
    return outs[0] if single else list(outs)


def swap_with_sibling(name, gs):
    na = len(gs)

    def body(*refs):
        g_refs, out_refs, send_sems, recv_sems = refs[:na], refs[na:2 * na], refs[2 * na], refs[2 * na + 1]
        mx, my, mc = _my_place()
        copies = [pltpu.make_async_remote_copy(
            src_ref=g_refs[a].at[k, 1 - mc], dst_ref=out_refs[a].at[k], send_sem=send_sems.at[4 * a + k],
            recv_sem=recv_sems.at[4 * a + k], device_id=(mx, my, 1 - mc), device_id_type=MESH)
            for a in range(na) for k in range(4)]
        for cp in copies:
            cp.start()
        for cp in copies:
            cp.wait_recv()
        for cp in copies:
            cp.wait_send()

    return list(pl.pallas_call(
        body, name=name, in_specs=[_HBM] * na, out_specs=[_HBM] * na,
        out_shape=[jax.ShapeDtypeStruct((4,) + g.shape[2:], g.dtype) for g in gs],
        scratch_shapes=[pltpu.SemaphoreType.DMA((4 * na,)), pltpu.SemaphoreType.DMA((4 * na,))],
    )(*gs))


def swap_with_chips(name, ts):
    na = len(ts)

    def body(*refs):
        t_refs, out_refs, send_sems, recv_sems = refs[:na], refs[na:2 * na], refs[2 * na], refs[2 * na + 1]
        mx, my, mc = _my_place()
        chips = [(1 - mx, my), (mx, 1 - my), (1 - mx, 1 - my)]
        copies = [pltpu.make_async_remote_copy(
            src_ref=t_refs[a].at[2 * px + py], dst_ref=out_refs[a].at[k], send_sem=send_sems.at[3 * a + k],
            recv_sem=recv_sems.at[3 * a + k], device_id=(px, py, mc), device_id_type=MESH)
            for a in range(na) for k, (px, py) in enumerate(chips)]
        for cp in copies:
            cp.start()
        for cp in copies:
            cp.wait_recv()
        for cp in copies:
            cp.wait_send()

    return list(pl.pallas_call(
        body, name=name, in_specs=[_HBM] * na, out_specs=[_HBM] * na,
        out_shape=[jax.ShapeDtypeStruct((3,) + t.shape[1:], t.dtype) for t in ts],
        scratch_shapes=[pltpu.SemaphoreType.DMA((3 * na,)), pltpu.SemaphoreType.DMA((3 * na,))],
    )(*ts))


def _pick_rows(n, target):
    best = None
    for t in range(8, min(n, target) + 1, 8):
        if n % t == 0:
            best = t
    return best if best is not None else n


def add_blocks(name, parts, out_dtype, rows=512):
    r, w = parts[0].shape
    tr = _pick_rows(r, rows)

    def body(*refs):
        acc = refs[0][...].astype(F32)
        for ref in refs[1:-1]:
            acc = acc + ref[...].astype(F32)
        refs[-1][...] = acc.astype(refs[-1].dtype)

    spec = pl.BlockSpec((tr, w), lambda i: (i, 0))
    return pl.pallas_call(
        body, name=name, grid=(r // tr,), in_specs=[spec] * len(parts), out_specs=spec,
        out_shape=jax.ShapeDtypeStruct((r, w), out_dtype),
        compiler_params=pltpu.CompilerParams(dimension_semantics=("parallel",)),
    )(*parts)


def sum_rows(name, x):
    def body(x_ref, o_ref):
        o_ref[...] = jnp.sum(x_ref[...], axis=0, keepdims=True)

    return pl.pallas_call(body, name=name, out_shape=jax.ShapeDtypeStruct((1, x.shape[1]), F32))(x)


def reduce_scatter(names, g8s):
    mx, my, mc = _my_place()
    gs = [g8.reshape((4, 2) + g8.shape[1:]) for g8 in g8s]
    from_sibling = swap_with_sibling("rs_sibling", gs)
    chip_sums = []
    for n, g, fs in zip(names, gs, from_sibling):
        w = g.shape[-1]
        mine = lax.dynamic_index_in_dim(g, mc, axis=1, keepdims=False)
        chip_sums.append(add_blocks(f"rs_chip_sum_{n}", [mine.reshape(-1, w), fs.reshape(-1, w)], BF16).reshape(fs.shape))
    from_chips = swap_with_chips("rs_chips", chip_sums)
    outs = []
    for n, cs, fc in zip(names, chip_sums, from_chips):
        w = cs.shape[-1]
        own = lax.dynamic_index_in_dim(cs, 2 * mx + my, axis=0, keepdims=False)
        outs.append(add_blocks(f"rs_total_{n}", [own.reshape(-1, w)] + [fc[k].reshape(-1, w) for k in range(3)], F32)
                    .reshape(cs.shape[1:]))
    return outs


def _pack(arrays, width, row_mult, dtype, lead=0):
    parts, metas = [], []
    for a in arrays:
        lead_shape = a.shape[:lead]
        size = int(np.prod(a.shape[lead:]))
        chunk = row_mult * width
        padded = -(-size // chunk) * chunk
        flat = a.astype(dtype).reshape(lead_shape + (size,))
        if padded != size:
            flat = jnp.pad(flat, [(0, 0)] * lead + [(0, padded - size)])
        parts.append(flat.reshape(lead_shape + (padded // width, width)))
        metas.append((a.shape[lead:], size, padded // width))
    return jnp.concatenate(parts, axis=lead), metas


def _unpack(slab, metas, lead=0):
    out, r0 = [], 0
    for shape, size, rows in metas:
        part = lax.slice_in_dim(slab, r0, r0 + rows, axis=lead)
        lead_shape = part.shape[:lead]
        flat = part.reshape(lead_shape + (rows * part.shape[-1],))
        out.append(lax.slice_in_dim(flat, 0, size, axis=lead).reshape(lead_shape + tuple(shape)))
        r0 += rows
    return out


def _f_adamw(w, g, m, v):
    m = ADAM_B1 * m + (1.0 - ADAM_B1) * g
    v = ADAM_B2 * v + (1.0 - ADAM_B2) * (g * g)
    m_hat = m / (1.0 - ADAM_B1 ** ADAM_STEP)
    v_hat = v / (1.0 - ADAM_B2 ** ADAM_STEP)
    delta = (-ADAM_LR) * (m_hat / (jnp.sqrt(v_hat) + ADAM_EPS) + ADAM_WD * w)
    return delta, m, v


def adamw(name, w, g, m, v):
    shape = w.shape
    w2 = shape[-1]
    flat = [a.reshape(-1, w2) for a in (w, g, m, v)]
    tm = _pick_rows(flat[0].shape[0], 256)
    outs = ew_fwd(name, _f_adamw, flat, [], [], [w2] * 3, [F32] * 3, tm=tm)
    return [o.reshape(shape) for o in outs]


WEIGHTS = ["ffn1_norm", "ffn1_w1", "ffn1_w3", "ffn1_w2", "mix_norm", "w_in", "conv_w", "conv_b", "rg_wa", "rg_ba", "rg_wx",
           "rg_bx", "rg_lam", "fox_bf", "merge_b", "w_rg", "w_sb", "w_fox", "w_o", "ffn2_norm", "ffn2_w1", "ffn2_w3",
           "ffn2_w2", "ada_w", "ada_b", "final_norm", "final_ada_w", "final_ada_b"]
GATHERED = {"ffn1_w1": 2, "ffn1_w3": 2, "ffn1_w2": 1, "w_in": 2, "w_rg": 1, "w_sb": 2, "w_fox": 2, "w_o": 1,
            "ffn2_w1": 2, "ffn2_w3": 2, "ffn2_w2": 1}
REPLICATED = ["ffn1_norm", "mix_norm", "conv_b", "rg_wa", "rg_ba", "rg_wx", "rg_bx", "rg_lam", "fox_bf", "merge_b",
              "ffn2_norm", "final_norm"]
IN_CUTS = (0, 1024, 2048, 3584, 5120, 5128, 8200)


def _unshard(g, axis):
    g = jnp.moveaxis(g, 0, axis)
    shape = g.shape
    return g.reshape(shape[:axis] + (shape[axis] * shape[axis + 1],) + shape[axis + 2:])


def _reshard(full, axis):
    shape = full.shape
    g = full.reshape(shape[:axis] + (N_DEV, shape[axis] // N_DEV) + shape[axis + 1:])
    return jnp.moveaxis(g, axis, 0)


def _block_diag(w):
    nb, bd, _ = w.shape
    eye = jnp.eye(nb, dtype=bool)[:, None, :, None]
    return jnp.where(eye, w[:, :, None, :], 0.0).reshape(nb * bd, nb * bd)


def _diag_blocks(m, nb=RG_BLOCKS):
    bd = m.shape[0] // nb
    return jnp.stack([m[k * bd:(k + 1) * bd, k * bd:(k + 1) * bd] for k in range(nb)])


def _pad_lanes(a, width=LANES):
    return jnp.pad(a, [(0, 0)] * (a.ndim - 1) + [(0, width - a.shape[-1])])


def _bp(m, k, which):
    return m[:, k, which][:, None, :]


def _f_silu(c):
    return c * jax.nn.sigmoid(c)


def _f_add_bias(a, b):
    return a + b


def _ffn_fwd(tag, x, shift, scale, gate, gain, w1, w3, w2):
    h = ew_fwd(f"ffn_norm_{tag}", f_norm_mod, [x], [shift, scale], [gain], [D], [BF16])[0]
    a = matmul(f"ffn_up1_{tag}", h, w1, "nn", tn=1408)
    b3 = matmul(f"ffn_up3_{tag}", h, w3, "nn", tn=1408)
    s = ew_fwd(f"ffn_act_{tag}", f_swiglu, [a, b3], [], [], [D_FF], [BF16])[0]
    y = matmul(f"ffn_down_{tag}", s, w2, "nn")
    xo = ew_fwd(f"ffn_res_{tag}", functools.partial(f_resid, 0.5), [x, y], [gate], [], [D], [F32])[0]
    return xo, (x, h, a, b3, s, y)


def _ffn_bwd(tag, dxo, saved, shift, scale, gate, gain, w1, w3, w2):
    x, h, a, b3, s, y = saved
    (dy,), (dgate,), _ = ew_bwd(f"ffn_res_bwd_{tag}", functools.partial(f_resid, 0.5), [x, y], [gate], [], [dxo], [None, BF16])
    ds = matmul(f"ffn_down_dx_{tag}", dy, w2, "nt", tn=1408)
    (da, db3), _, _ = ew_bwd(f"ffn_act_bwd_{tag}", f_swiglu, [a, b3], [], [], [ds], [BF16, BF16])
    dw2 = matmul(f"ffn_dw2_{tag}", s, dy, "tn", tm=256)
    dw1 = matmul(f"ffn_dw1_{tag}", h, da, "tn", tm=256)
    dw3 = matmul(f"ffn_dw3_{tag}", h, db3, "tn", tm=256)
    dh = matmul(f"ffn_up_dx_{tag}", [da, db3], [w1, w3], "nt")
    (dx,), (dshift, dscale), (dgain,) = ew_bwd(f"ffn_norm_bwd_{tag}", f_norm_mod, [x], [shift, scale], [gain], [dh], [F32],
                                               adds=[dxo])
    return dx, (dshift, dscale, dgate), dgain, dw1, dw3, dw2


def _mixer_fwd(tag, x, shift, scale, gate, p):
    h = ew_fwd(f"mix_norm_{tag}", f_norm_mod, [x], [shift, scale], [p["gain"]], [D], [BF16])[0]
    rgx = matmul(f"in_rgx_{tag}", h, p["w_rgx"], "nn")
    rgate = matmul(f"in_gate_{tag}", h, p["w_gate"], "nn")
    sbqkv = matmul(f"in_sb_{tag}", h, p["w_sbqkv"], "nn")
    foxqkv = matmul(f"in_fox_{tag}", h, p["w_foxqkv"], "nn")
    ff = matmul(f"in_forget_{tag}", h, p["w_f"], "nn")
    mg = matmul(f"in_merge_{tag}", h, p["w_merge"], "nn")
    xa = conv_fwd(rgx, p["conv_w8"], p["conv_b"])
    pre_r = matmul(f"rg_a_{tag}", xa, p["wa_bd"], "nn")
    pre_i = matmul(f"rg_x_{tag}", xa, p["wx_bd"], "nn")
    a, u = ew_fwd(f"rg_gates_{tag}", f_rg_gates, [pre_r, pre_i, xa], [], [p["ba"], p["bx"], p["lam"]], [D, D], [F32, F32])
    hs = scan_fwd(a, u)
    ya = ew_fwd(f"rg_out_{tag}", f_gelu_mul, [rgate, hs], [], [], [D], [BF16])[0]
    yb, sb_tot = sb_attn_fwd(sbqkv)
    lf = ew_fwd(f"fox_logf_{tag}", f_log_sigmoid_bias, [ff], [], [p["bf"]], [LANES], [F32])[0]
    cum = seq_cumsum(f"fox_cum_{tag}", [lf], [1.0], False)
    cum_t = cum.reshape(-1, SEQ, LANES)[:, :, :N_HEADS].transpose(0, 2, 1)
    yc, lse = fox_attn_fwd(foxqkv, cum, cum_t)
    pa = matmul(f"out_rg_{tag}", ya, p["w_rg"], "nn")
    pb = matmul(f"out_sb_{tag}", yb, p["w_sb"], "nn")
    pc = matmul(f"out_fox_{tag}", yc, p["w_fox"], "nn")
    mixed = ew_fwd(f"merge_{tag}", f_merge, [mg, pa, pb, pc], [], [p["merge_b"]], [D], [BF16])[0]
    y = matmul(f"out_o_{tag}", mixed, p["w_o"], "nn")
    xo = ew_fwd(f"mix_res_{tag}", functools.partial(f_resid, 1.0), [x, y], [gate], [], [D], [F32])[0]
    saved = dict(x=x, h=h, rgx=rgx, rgate=rgate, sbqkv=sbqkv, foxqkv=foxqkv, ff=ff, mg=mg, xa=xa, pre_r=pre_r, pre_i=pre_i,
                 a=a, hs=hs, ya=ya, yb=yb, sb_tot=sb_tot, cum=cum, cum_t=cum_t, yc=yc, lse=lse, pa=pa, pb=pb, pc=pc,
                 mixed=mixed, y=y)
    return xo, saved


def _mixer_bwd(tag, dxo, s, shift, scale, gate, p):
    (dy,), (dgate,), _ = ew_bwd(f"mix_res_bwd_{tag}", functools.partial(f_resid, 1.0), [s["x"], s["y"]], [gate], [], [dxo],
                                [None, BF16])
    dmixed = matmul(f"out_o_dx_{tag}", dy, p["w_o"], "nt")
    g = {"w_o": matmul(f"out_o_dw_{tag}", s["mixed"], dy, "tn", tm=256)}
    (dmg, dpa, dpb, dpc), _, (g["merge_b"],) = ew_bwd(
        f"merge_bwd_{tag}", f_merge, [s["mg"], s["pa"], s["pb"], s["pc"]], [], [p["merge_b"]], [dmixed], [BF16] * 4)
    dya = matmul(f"out_rg_dx_{tag}", dpa, p["w_rg"], "nt")
    g["w_rg"] = matmul(f"out_rg_dw_{tag}", s["ya"], dpa, "tn", tm=256)
    dyb = matmul(f"out_sb_dx_{tag}", dpb, p["w_sb"], "nt", out_dtype=BF16)
    g["w_sb"] = matmul(f"out_sb_dw_{tag}", s["yb"], dpb, "tn", tm=256)
    dyc = matmul(f"out_fox_dx_{tag}", dpc, p["w_fox"], "nt", out_dtype=BF16)
    g["w_fox"] = matmul(f"out_fox_dw_{tag}", s["yc"], dpc, "tn", tm=256)
    dq_c, dk_c, dv_c, dcq, dck = fox_attn_bwd(s["foxqkv"], s["cum"], s["cum_t"], s["lse"], s["yc"], dyc)
    dck_rows = _pad_lanes(dck.transpose(0, 2, 1).reshape(-1, N_HEADS))
    dlf = seq_cumsum(f"fox_cum_bwd_{tag}", [dcq, dck_rows], [1.0, -1.0], True)
    (dff,), _, (dbf,) = ew_bwd(f"fox_logf_bwd_{tag}", f_log_sigmoid_bias, [s["ff"]], [], [p["bf"]], [dlf], [BF16])
    g["fox_bf"] = dbf[0, :N_HEADS]
    dq_b, dk_b, dv_b = sb_attn_bwd(s["sbqkv"], s["sb_tot"], dyb)
    (drgate, dhs), _, _ = ew_bwd(f"rg_out_bwd_{tag}", f_gelu_mul, [s["rgate"], s["hs"]], [], [], [dya], [BF16, F32])
    da, du = scan_bwd(s["a"], s["hs"], dhs)
    (dpre_r, dpre_i, dxa1), _, (g["rg_ba"], g["rg_bx"], g["rg_lam"]) = ew_bwd(
        f"rg_gates_bwd_{tag}", f_rg_gates, [s["pre_r"], s["pre_i"], s["xa"]], [], [p["ba"], p["bx"], p["lam"]], [da, du],
        [BF16, BF16, F32])
    dxa2 = matmul(f"rg_dx_{tag}", [dpre_r, dpre_i], [p["wa_bd"], p["wx_bd"]], "nt")
    g["rg_wa"] = _diag_blocks(matmul(f"rg_a_dw_{tag}", s["xa"], dpre_r, "tn", tm=256))
    g["rg_wx"] = _diag_blocks(matmul(f"rg_x_dw_{tag}", s["xa"], dpre_i, "tn", tm=256))
    drgx, dwb = conv_bwd(s["rgx"], p["conv_w8"], dxa1, dxa2)
    g["conv_w"] = dwb[:CONV_K]
    g["conv_b"] = dwb[CONV_K]
    cots = [drgx, drgate, dq_b, dk_b, dv_b, dq_c, dk_c, dv_c, dff, dmg]
    w_sb3 = [p["w_sbqkv"][:, k * ATT_W:(k + 1) * ATT_W] for k in range(3)]
    w_fox3 = [p["w_foxqkv"][:, k * ATT_W:(k + 1) * ATT_W] for k in range(3)]
    ws = [p["w_rgx"], p["w_gate"]] + w_sb3 + w_fox3 + [p["w_f"], p["w_merge"]]
    dh = matmul(f"in_dx_{tag}", cots, ws, "nt", tm=256)
    dws = [matmul(f"in_dw{k}_{tag}", s["h"], ct, "tn", tm=256) for k, ct in enumerate(cots)]
    dws[8] = dws[8][:, :N_HEADS]
    g["w_in"] = jnp.concatenate(dws, axis=1)
    (dx,), (dshift, dscale), (g["mix_norm"],) = ew_bwd(f"mix_norm_bwd_{tag}", f_norm_mod, [s["x"]], [shift, scale], [p["gain"]],
                                                       [dh], [F32], adds=[dxo])
    return dx, (dshift, dscale, dgate), g


def _final_loss(x, target, shift, scale, gain):
    n = x.shape[0]
    tm = EW_ROWS
    tpb = SEQ // tm

    def body(x_ref, t_ref, sh_ref, sc_ref, g_ref, loss_ref, dx_ref, dsh_ref, dsc_ref, dg_ref):
        i = pl.program_id(0)
        out, vjp = jax.vjp(f_norm_mod, x_ref[...], sh_ref[...], sc_ref[...], g_ref[...])
        diff = out - t_ref[...]
        dx, dsh, dsc, dg = vjp(diff * (1.0 / D))
        dx_ref[...] = dx
        sq = jnp.sum(jnp.sum(diff * diff, axis=1, keepdims=True), axis=0, keepdims=True)

        @pl.when(i % tpb == 0)
        def _():
            dsh_ref[...] = jnp.zeros_like(dsh_ref)
            dsc_ref[...] = jnp.zeros_like(dsc_ref)

        @pl.when(i == 0)
        def _():
            dg_ref[...] = jnp.zeros_like(dg_ref)
            loss_ref[...] = jnp.zeros_like(loss_ref)

        dsh_ref[...] += dsh
        dsc_ref[...] += dsc
        dg_ref[...] += dg
        loss_ref[...] += jnp.broadcast_to(sq, (1, LANES)) * (0.5 / D)

    row, bp, gp = _row_spec(D, tm), _bparam_spec(D, tpb), _gparam_spec((1, D))
    return pl.pallas_call(
        body, name="final_loss", grid=(n // tm,), in_specs=[row, row, bp, bp, gp],
        out_specs=[_gparam_spec((1, LANES)), row, bp, bp, gp],
        out_shape=[jax.ShapeDtypeStruct((1, LANES), F32), jax.ShapeDtypeStruct((n, D), F32),
                   jax.ShapeDtypeStruct(shift.shape, F32), jax.ShapeDtypeStruct(scale.shape, F32),
                   jax.ShapeDtypeStruct((1, D), F32)],
        compiler_params=pltpu.CompilerParams(dimension_semantics=("arbitrary",)),
    )(x, target, shift, scale, gain)


def kernel(x, c, ffn1_norm, ffn1_w1, ffn1_w3, ffn1_w2, mix_norm, w_in, conv_w, conv_b, rg_wa, rg_ba, rg_wx, rg_bx, rg_lam, fox_bf, merge_b, w_rg, w_sb, w_fox, w_o, ffn2_norm, ffn2_w1, ffn2_w3, ffn2_w2, ada_w, ada_b, final_norm, final_ada_w, final_ada_b, loss_target, m_ffn1_norm, m_ffn1_w1, m_ffn1_w3, m_ffn1_w2, m_mix_norm, m_w_in, m_conv_w, m_conv_b, m_rg_wa, m_rg_ba, m_rg_wx, m_rg_bx, m_rg_lam, m_fox_bf, m_merge_b, m_w_rg, m_w_sb, m_w_fox, m_w_o, m_ffn2_norm, m_ffn2_w1, m_ffn2_w3, m_ffn2_w2, m_ada_w, m_ada_b, m_final_norm, m_final_ada_w, m_final_ada_b, v_ffn1_norm, v_ffn1_w1, v_ffn1_w3, v_ffn1_w2, v_mix_norm, v_w_in, v_conv_w, v_conv_b, v_rg_wa, v_rg_ba, v_rg_wx, v_rg_bx, v_rg_lam, v_fox_bf, v_merge_b, v_w_rg, v_w_sb, v_w_fox, v_w_o, v_ffn2_norm, v_ffn2_w1, v_ffn2_w3, v_ffn2_w2, v_ada_w, v_ada_b, v_final_norm, v_final_ada_w, v_final_ada_b):
    given = dict(zip(["x", "c"] + WEIGHTS + ["loss_target"] + ["m_" + n for n in WEIGHTS] + ["v_" + n for n in WEIGHTS],
                     (x, c, ffn1_norm, ffn1_w1, ffn1_w3, ffn1_w2, mix_norm, w_in, conv_w, conv_b, rg_wa, rg_ba, rg_wx, rg_bx, rg_lam, fox_bf, merge_b, w_rg, w_sb, w_fox, w_o, ffn2_norm, ffn2_w1, ffn2_w3, ffn2_w2, ada_w, ada_b, final_norm, final_ada_w, final_ada_b, loss_target, m_ffn1_norm, m_ffn1_w1, m_ffn1_w3, m_ffn1_w2, m_mix_norm, m_w_in, m_conv_w, m_conv_b, m_rg_wa, m_rg_ba, m_rg_wx, m_rg_bx, m_rg_lam, m_fox_bf, m_merge_b, m_w_rg, m_w_sb, m_w_fox, m_w_o, m_ffn2_norm, m_ffn2_w1, m_ffn2_w3, m_ffn2_w2, m_ada_w, m_ada_b, m_final_norm, m_final_ada_w, m_final_ada_b, v_ffn1_norm, v_ffn1_w1, v_ffn1_w3, v_ffn1_w2, v_mix_norm, v_w_in, v_conv_w, v_conv_b, v_rg_wa, v_rg_ba, v_rg_wx, v_rg_bx, v_rg_lam, v_fox_bf, v_merge_b, v_w_rg, v_w_sb, v_w_fox, v_w_o, v_ffn2_norm, v_ffn2_w1, v_ffn2_w3, v_ffn2_w2, v_ada_w, v_ada_b, v_final_norm, v_final_ada_w, v_final_ada_b)))
    idx = my_index()
    n_batch = N_DEV * B_LOC
    ada_cols = ada_w.shape[2]
    fin_cols = final_ada_w.shape[1]

    small_in, small_in_meta = _pack([c, conv_w], LANES, 8, F32)
    c_parts, conv_w_parts = _unpack(all_gather("gather_c_conv", small_in), small_in_meta, lead=1)
    c_all = c_parts.reshape(n_batch, D)
    conv_w_all = _unshard(conv_w_parts, 2)
    c_act = ew_fwd("c_silu", _f_silu, [c_all], [], [], [D], [F32])[0]
    mod_cols = [matmul(f"ada_proj_{l}", c_act, ada_w[l], "nn") for l in range(DEPTH)]
    mod_cols.append(matmul("ada_proj_final", c_act, final_ada_w, "nn"))
    mod_g = all_gather("gather_mod", jnp.concatenate(mod_cols, axis=1))
    mods = []
    for l in range(DEPTH):
        full = mod_g[:, :, l * ada_cols:(l + 1) * ada_cols].transpose(1, 0, 2).reshape(n_batch, N_DEV * ada_cols)
        full = ew_fwd(f"ada_bias_{l}", _f_add_bias, [full], [], [ada_b[l][None]], [full.shape[1]], [F32])[0]
        mods.append(lax.dynamic_slice_in_dim(full, idx * B_LOC, B_LOC, axis=0).reshape(B_LOC, 3, 3, D))
    fm = mod_g[:, :, DEPTH * ada_cols:].transpose(1, 0, 2).reshape(n_batch, N_DEV * fin_cols)
    fm = ew_fwd("ada_bias_final", _f_add_bias, [fm], [], [final_ada_b[None]], [fm.shape[1]], [F32])[0]
    fm = lax.dynamic_slice_in_dim(fm, idx * B_LOC, B_LOC, axis=0).reshape(B_LOC, 2, D)

    names = list(GATHERED)
    gathered = all_gather("gather_weights", [given[n].astype(BF16) for n in names])
    full_w = {n: _unshard(g, GATHERED[n]) for n, g in zip(names, gathered)}

    def layer_params(l):
        wi = full_w["w_in"][l]
        cut = IN_CUTS
        return dict(
            gain=mix_norm[l][None], w_rgx=wi[:, cut[0]:cut[1]], w_gate=wi[:, cut[1]:cut[2]], w_sbqkv=wi[:, cut[2]:cut[3]],
            w_foxqkv=wi[:, cut[3]:cut[4]], w_f=_pad_lanes(wi[:, cut[4]:cut[5]]), w_merge=wi[:, cut[5]:cut[6]],
            conv_w8=jnp.pad(conv_w_all[l], ((0, 8 - CONV_K), (0, 0))), conv_b=conv_b[l][None],
            wa_bd=_block_diag(rg_wa[l]), wx_bd=_block_diag(rg_wx[l]), ba=rg_ba[l][None], bx=rg_bx[l][None], lam=rg_lam[l][None],
            bf=_pad_lanes(fox_bf[l][None]), merge_b=merge_b[l][None], w_rg=full_w["w_rg"][l], w_sb=full_w["w_sb"][l],
            w_fox=full_w["w_fox"][l], w_o=full_w["w_o"][l])

    n_tok = x.shape[0] * x.shape[1]
    h = x.reshape(n_tok, D)
    saved = []
    for l in range(DEPTH):
        m = mods[l]
        p = layer_params(l)
        h, s1 = _ffn_fwd(f"a{l}", h, _bp(m, 0, 0), _bp(m, 0, 1), _bp(m, 0, 2), ffn1_norm[l][None], full_w["ffn1_w1"][l],
                         full_w["ffn1_w3"][l], full_w["ffn1_w2"][l])
        h, s2 = _mixer_fwd(f"{l}", h, _bp(m, 1, 0), _bp(m, 1, 1), _bp(m, 1, 2), p)
        h, s3 = _ffn_fwd(f"b{l}", h, _bp(m, 2, 0), _bp(m, 2, 1), _bp(m, 2, 2), ffn2_norm[l][None], full_w["ffn2_w1"][l],
                         full_w["ffn2_w3"][l], full_w["ffn2_w2"][l])
        saved.append((s1, s2, s3, p))
    loss_row, dh, dfshift, dfscale, dgain_final = _final_loss(h, loss_target.reshape(n_tok, D), fm[:, 0][:, None, :],
                                                              fm[:, 1][:, None, :], final_norm[None])

    grads = {n: [None] * DEPTH for n in WEIGHTS}
    d_mods = [None] * DEPTH
    for l in reversed(range(DEPTH)):
        m = mods[l]
        s1, s2, s3, p = saved[l]
        dh, dm3, grads["ffn2_norm"][l], grads["ffn2_w1"][l], grads["ffn2_w3"][l], grads["ffn2_w2"][l] = _ffn_bwd(
            f"b{l}", dh, s3, _bp(m, 2, 0), _bp(m, 2, 1), _bp(m, 2, 2), ffn2_norm[l][None], full_w["ffn2_w1"][l],
            full_w["ffn2_w3"][l], full_w["ffn2_w2"][l])
        dh, dm2, gm = _mixer_bwd(f"{l}", dh, s2, _bp(m, 1, 0), _bp(m, 1, 1), _bp(m, 1, 2), p)
        for n, gval in gm.items():
            grads[n][l] = gval
        dh, dm1, grads["ffn1_norm"][l], grads["ffn1_w1"][l], grads["ffn1_w3"][l], grads["ffn1_w2"][l] = _ffn_bwd(
            f"a{l}", dh, s1, _bp(m, 0, 0), _bp(m, 0, 1), _bp(m, 0, 2), ffn1_norm[l][None], full_w["ffn1_w1"][l],
            full_w["ffn1_w3"][l], full_w["ffn1_w2"][l])
        d_mods[l] = jnp.concatenate([t.reshape(B_LOC, D) for dm in (dm1, dm2, dm3) for t in dm], axis=1)
    grad_x = dh.reshape(x.shape)
    d_fm = jnp.concatenate([dfshift.reshape(B_LOC, D), dfscale.reshape(B_LOC, D)], axis=1)

    rep = {n: jnp.stack([t.reshape(given[n].shape[1:]) for t in grads[n]]) for n in REPLICATED if n != "final_norm"}
    rep["final_norm"] = dgain_final.reshape(D)
    rep["conv_w"] = jnp.stack(grads["conv_w"])
    rep_names = list(rep)
    rep_slab, rep_meta = _pack([rep[n] for n in rep_names], LANES, 8, F32)
    mod_slab, mod_meta = _pack(d_mods + [d_fm], LANES, 8, F32)
    small_g = all_gather("gather_small_grads", jnp.concatenate([mod_slab, rep_slab], axis=0))
    d_mod_all = [t.reshape(n_batch, -1) for t in _unpack(small_g[:, :mod_slab.shape[0]], mod_meta, lead=1)]
    rep_sum = add_blocks("sum_small_grads", [small_g[k, mod_slab.shape[0]:] for k in range(N_DEV)], F32)
    rep_grad = dict(zip(rep_names, _unpack(rep_sum, rep_meta)))
    final_g = {n: rep_grad[n] for n in REPLICATED}
    final_g["conv_w"] = lax.dynamic_slice_in_dim(rep_grad["conv_w"], idx * conv_w.shape[2], conv_w.shape[2], axis=2)
    final_g["ada_b"] = jnp.stack([sum_rows(f"ada_b_grad_{l}", d_mod_all[l])[0] for l in range(DEPTH)])
    final_g["final_ada_b"] = sum_rows("final_ada_b_grad", d_mod_all[DEPTH])[0]
    final_g["ada_w"] = jnp.stack([
        matmul(f"ada_w_grad_{l}", c_act, lax.dynamic_slice_in_dim(d_mod_all[l], idx * ada_cols, ada_cols, axis=1), "tn")
        for l in range(DEPTH)])
    final_g["final_ada_w"] = matmul(
        "final_ada_w_grad", c_act, lax.dynamic_slice_in_dim(d_mod_all[DEPTH], idx * fin_cols, fin_cols, axis=1), "tn")

    g8s = [_reshard(jnp.stack(grads[n]), GATHERED[n]).astype(BF16) for n in names]
    for n, gval in zip(names, reduce_scatter(names, g8s)):
        final_g[n] = gval

    delta, new_m, new_v = {}, {}, {}
    sharded = names + ["ada_w", "final_ada_w", "conv_w"]
    for n in sharded:
        delta[n], new_m[n], new_v[n] = adamw(f"adamw_{n}", given[n], final_g[n], given["m_" + n], given["v_" + n])
    rep_all = [n for n in WEIGHTS if n not in sharded]
    packed = [_pack([src[n] for n in rep_all], LANES, 8, F32)[0]
              for src in (given, final_g, {n: given["m_" + n] for n in rep_all}, {n: given["v_" + n] for n in rep_all})]
    rep_meta_all = _pack([given[n] for n in rep_all], LANES, 8, F32)[1]
    for store, slab_out in zip((delta, new_m, new_v), adamw("adamw_replicated", *packed)):
        store.update(zip(rep_all, _unpack(slab_out, rep_meta_all)))

    loss = lax.psum(loss_row[0, 0], ("x", "y", "c"))
    return (loss, grad_x, *[final_g[n] for n in WEIGHTS], *[delta[n] for n in WEIGHTS], *[new_m[n] for n in WEIGHTS],
            *[new_v[n] for n in WEIGHTS])
```

```python
import functools

import numpy as np
import jax
import jax.numpy as jnp
from jax import lax
from jax.experimental import pallas as pl
from jax.experimental.pallas import tpu as pltpu

F32 = jnp.float32
BF16 = jnp.bfloat16
MESH = pl.DeviceIdType.MESH

N_DEV = 8
D = 1024
SEQ = 2048
B_LOC = 2
N_TOK = B_LOC * SEQ
DEPTH = 2
D_FF = 2816
RG_BLOCKS = 16
RG_C = 8.0
N_HEADS = 8
HEAD_DIM = 64
ATT_W = N_HEADS * HEAD_DIM
LANES = 128
EPS = 1e-6
ATT_SCALE = HEAD_DIM ** -0.5
CONV_K = 4

ADAM_LR = 0.001
ADAM_B1 = 0.9
ADAM_B2 = 0.999
ADAM_EPS = 1e-08
ADAM_WD = 0.01
ADAM_STEP = 10

EW_ROWS = 256
ATT_BLK = 256


def _pick_tile(dim, target):
    best = None
    for t in range(LANES, min(dim, target) + 1, LANES):
        if dim % t == 0:
            best = t
    return best if best is not None else dim


_DIMS = {"nn": (((1,), (0,)), ((), ())), "nt": (((1,), (1,)), ((), ())), "tn": (((0,), (0,)), ((), ()))}


def matmul(name, a_list, b_list, mode, out_dtype=F32, tm=1024, tn=512):
    if not isinstance(a_list, (list, tuple)):
        a_list, b_list = [a_list], [b_list]
    n = len(a_list)
    m_dim = a_list[0].shape[1] if mode == "tn" else a_list[0].shape[0]
    n_dim = b_list[0].shape[0] if mode == "nt" else b_list[0].shape[1]
    tm, tn = _pick_tile(m_dim, tm), _pick_tile(n_dim, tn)
    dims = _DIMS[mode]

    def body(*refs):
        o_ref = refs[-1]
        acc = None
        for a_ref, b_ref in zip(refs[:n], refs[n:2 * n]):
            d = lax.dot_general(a_ref[...].astype(BF16), b_ref[...].astype(BF16), dims, preferred_element_type=F32)
            acc = d if acc is None else acc + d
        o_ref[...] = acc.astype(o_ref.dtype)

    in_specs = []
    for a in a_list:
        if mode == "tn":
            in_specs.append(pl.BlockSpec((a.shape[0], tm), lambda i, j: (0, i)))
        else:
            in_specs.append(pl.BlockSpec((tm, a.shape[1]), lambda i, j: (i, 0)))
    for b in b_list:
        if mode == "nt":
            in_specs.append(pl.BlockSpec((tn, b.shape[1]), lambda i, j: (j, 0)))
        else:
            in_specs.append(pl.BlockSpec((b.shape[0], tn), lambda i, j: (0, j)))
    return pl.pallas_call(
        body, name=name, grid=(m_dim // tm, n_dim // tn), in_specs=in_specs,
        out_specs=pl.BlockSpec((tm, tn), lambda i, j: (i, j)),
        out_shape=jax.ShapeDtypeStruct((m_dim, n_dim), out_dtype),
        compiler_params=pltpu.CompilerParams(dimension_semantics=("parallel", "parallel")),
    )(*a_list, *b_list)


def _row_spec(w, tm):
    return pl.BlockSpec((tm, w), lambda i: (i, 0))


def _bparam_spec(w, tiles_per_batch):
    return pl.BlockSpec((None, 1, w), lambda i: (i // tiles_per_batch, 0, 0))


def _gparam_spec(shape):
    return pl.BlockSpec(shape, lambda i: (0, 0))


def ew_fwd(name, fn, rows, bparams, gparams, out_widths, out_dtypes, tm=EW_ROWS):
    n_rows = rows[0].shape[0]
    tm = min(tm, n_rows)
    tpb = max(SEQ // tm, 1)
    nr, nb, ng = len(rows), len(bparams), len(gparams)

    def body(*refs):
        vals = [r[...] for r in refs[:nr + nb + ng]]
        outs = fn(*vals)
        if not isinstance(outs, (tuple, list)):
            outs = (outs,)
        for o_ref, o in zip(refs[nr + nb + ng:], outs):
            o_ref[...] = o.astype(o_ref.dtype)

    in_specs = ([_row_spec(r.shape[1], tm) for r in rows] + [_bparam_spec(p.shape[2], tpb) for p in bparams]
                + [_gparam_spec(g.shape) for g in gparams])
    outs = pl.pallas_call(
        body, name=name, grid=(n_rows // tm,), in_specs=in_specs,
        out_specs=[_row_spec(w, tm) for w in out_widths],
        out_shape=[jax.ShapeDtypeStruct((n_rows, w), dt) for w, dt in zip(out_widths, out_dtypes)],
        compiler_params=pltpu.CompilerParams(dimension_semantics=("parallel",)),
    )(*rows, *bparams, *gparams)
    return outs


def ew_bwd(name, fn, rows, bparams, gparams, cts, row_grad_dtypes, adds=(), tm=EW_ROWS):
    n_rows = rows[0].shape[0]
    tm = min(tm, n_rows)
    tpb = max(SEQ // tm, 1)
    nr, nb, ng, nc = len(rows), len(bparams), len(gparams), len(cts)
    adds = list(adds) + [None] * (nr - len(adds))
    add_idx = [k for k in range(nr) if adds[k] is not None]
    want = [k for k in range(nr) if row_grad_dtypes[k] is not None]

    def body(*refs):
        pos = nr + nb + ng
        vals = [r[...] for r in refs[:pos]]
        ct_vals = [r[...].astype(F32) for r in refs[pos:pos + nc]]
        pos += nc
        add_vals = {k: refs[pos + q][...] for q, k in enumerate(add_idx)}
        pos += len(add_idx)
        out_refs = refs[pos:]
        f32_vals = [v.astype(F32) for v in vals]
        outs, vjp = jax.vjp(lambda *a: fn(*a), *f32_vals)
        single = not isinstance(outs, (tuple, list))
        grads = vjp(ct_vals[0].astype(outs.dtype) if single else tuple(c.astype(o.dtype) for c, o in zip(ct_vals, outs)))
        i = pl.program_id(0)
        q = 0
        for k in want:
            g = grads[k]
            if k in add_vals:
                g = g + add_vals[k].astype(F32)
            out_refs[q][...] = g.astype(out_refs[q].dtype)
            q += 1
        for k in range(nb):
            ref = out_refs[q]
            q += 1

            @pl.when(i % tpb == 0)
            def _():
                ref[...] = jnp.zeros_like(ref)

            ref[...] += grads[nr + k]
        for k in range(ng):
            ref = out_refs[q]
            q += 1

            @pl.when(i == 0)
            def _():
                ref[...] = jnp.zeros_like(ref)

            ref[...] += grads[nr + nb + k]

    in_specs = ([_row_spec(r.shape[1], tm) for r in rows] + [_bparam_spec(p.shape[2], tpb) for p in bparams]
                + [_gparam_spec(g.shape) for g in gparams] + [_row_spec(c.shape[1], tm) for c in cts]
                + [_row_spec(adds[k].shape[1], tm) for k in add_idx])
    out_specs = ([_row_spec(rows[k].shape[1], tm) for k in want] + [_bparam_spec(p.shape[2], tpb) for p in bparams]
                 + [_gparam_spec(g.shape) for g in gparams])
    out_shape = ([jax.ShapeDtypeStruct(rows[k].shape, row_grad_dtypes[k]) for k in want]
                 + [jax.ShapeDtypeStruct(p.shape, F32) for p in bparams] + [jax.ShapeDtypeStruct(g.shape, F32) for g in gparams])
    outs = pl.pallas_call(
        body, name=name, grid=(n_rows // tm,), in_specs=in_specs, out_specs=out_specs, out_shape=out_shape,
        compiler_params=pltpu.CompilerParams(dimension_semantics=("arbitrary",)),
    )(*rows, *bparams, *gparams, *cts, *[adds[k] for k in add_idx])
    d_rows = list(outs[:len(want)])
    d_b = list(outs[len(want):len(want) + nb])
    d_g = list(outs[len(want) + nb:])
    return d_rows, d_b, d_g


def f_norm_mod(x, shift, scale, gain):
    x = x.astype(F32)
    y = x * lax.rsqrt(jnp.mean(x * x, axis=-1, keepdims=True) + EPS)
    return (y * gain) * (1.0 + scale) + shift


def f_swiglu(a, b3):
    a = a.astype(F32)
    return (a * jax.nn.sigmoid(a)) * b3.astype(F32)


def f_resid(coef, x, y, gate):
    return x.astype(F32) + (coef * (1.0 + gate)) * y.astype(F32)


def f_rg_gates(pre_r, pre_i, xa, ba, bx, lam):
    r = jax.nn.sigmoid(pre_r + ba)
    i = jax.nn.sigmoid(pre_i + bx)
    softplus_neg_lam = jnp.maximum(-lam, 0.0) + jnp.log(1.0 + jnp.exp(-jnp.abs(lam)))
    log_a = (-RG_C) * r * softplus_neg_lam
    a = jnp.exp(log_a)
    u = jnp.sqrt(1.0 - a * a) * (i * xa)
    return a, u


def f_gelu_mul(gate, hs):
    g = gate.astype(F32)
    gelu = 0.5 * g * (1.0 + jnp.tanh(0.7978845608028654 * (g + 0.044715 * g * g * g)))
    return gelu * hs.astype(F32)


def f_log_sigmoid_bias(f, bf):
    z = f.astype(F32) + bf
    return jnp.minimum(z, 0.0) - jnp.log(1.0 + jnp.exp(-jnp.abs(z)))


def f_merge(mg, pa, pb, pc, merge_b):
    g = jax.nn.sigmoid(mg.astype(F32) + merge_b)
    return g[:, :D] * pa.astype(F32) + g[:, D:2 * D] * pb.astype(F32) + g[:, 2 * D:] * pc.astype(F32)


CONV_CB = 256
SCAN_CB = 512
CUM_RB = 512


def _shift_down(x, d):
    if d == 0:
        return x
    rows = lax.broadcasted_iota(jnp.int32, x.shape, 0)
    return jnp.where(rows >= d, pltpu.roll(x, d, axis=0), 0.0)


def _shift_up(x, d):
    if d == 0:
        return x
    s = x.shape[0]
    rows = lax.broadcasted_iota(jnp.int32, x.shape, 0)
    return jnp.where(rows < s - d, pltpu.roll(x, s - d, axis=0), 0.0)


def conv_fwd(x, w8, b):
    n, c = x.shape
    nb = n // SEQ

    def body(x_ref, w_ref, b_ref, y_ref):
        xv = x_ref[...]
        acc = jnp.broadcast_to(b_ref[...], xv.shape)
        for k in range(CONV_K):
            acc = acc + w_ref[k:k + 1, :] * _shift_down(xv, CONV_K - 1 - k)
        y_ref[...] = acc

    return pl.pallas_call(
        body, name="conv_fwd", grid=(c // CONV_CB, nb),
        in_specs=[pl.BlockSpec((SEQ, CONV_CB), lambda j, i: (i, j)), pl.BlockSpec((8, CONV_CB), lambda j, i: (0, j)),
                  pl.BlockSpec((1, CONV_CB), lambda j, i: (0, j))],
        out_specs=pl.BlockSpec((SEQ, CONV_CB), lambda j, i: (i, j)),
        out_shape=jax.ShapeDtypeStruct((n, c), F32),
        compiler_params=pltpu.CompilerParams(dimension_semantics=("parallel", "parallel")),
    )(x, w8, b)


def conv_bwd(x, w8, dy1, dy2):
    n, c = x.shape
    nb = n // SEQ

    def body(x_ref, w_ref, dy1_ref, dy2_ref, dx_ref, dwb_ref):
        xv = x_ref[...]
        dy = dy1_ref[...] + dy2_ref[...]
        dx = jnp.zeros_like(xv)
        parts = []
        for k in range(CONV_K):
            d = CONV_K - 1 - k
            dx = dx + w_ref[k:k + 1, :] * _shift_up(dy, d)
            parts.append(jnp.sum(dy * _shift_down(xv, d), axis=0, keepdims=True))
        parts.append(jnp.sum(dy, axis=0, keepdims=True))
        parts.append(jnp.zeros((8 - len(parts), xv.shape[1]), F32))
        dx_ref[...] = dx.astype(BF16)

        @pl.when(pl.program_id(1) == 0)
        def _():
            dwb_ref[...] = jnp.zeros_like(dwb_ref)

        dwb_ref[...] += jnp.concatenate(parts, axis=0)

    return pl.pallas_call(
        body, name="conv_bwd", grid=(c // CONV_CB, nb),
        in_specs=[pl.BlockSpec((SEQ, CONV_CB), lambda j, i: (i, j)), pl.BlockSpec((8, CONV_CB), lambda j, i: (0, j)),
                  pl.BlockSpec((SEQ, CONV_CB), lambda j, i: (i, j)), pl.BlockSpec((SEQ, CONV_CB), lambda j, i: (i, j))],
        out_specs=[pl.BlockSpec((SEQ, CONV_CB), lambda j, i: (i, j)), pl.BlockSpec((8, CONV_CB), lambda j, i: (0, j))],
        out_shape=[jax.ShapeDtypeStruct((n, c), BF16), jax.ShapeDtypeStruct((8, c), F32)],
        compiler_params=pltpu.CompilerParams(dimension_semantics=("parallel", "arbitrary")),
    )(x, w8, dy1, dy2)


def scan_fwd(a, u):
    n, c = a.shape

    def body(a_ref, u_ref, h_ref):
        def step(t, h):
            h = a_ref[pl.ds(t, 1), :] * h + u_ref[pl.ds(t, 1), :]
            h_ref[pl.ds(t, 1), :] = h
            return h

        lax.fori_loop(0, SEQ, step, jnp.zeros((1, SCAN_CB), F32), unroll=8)

    spec = pl.BlockSpec((SEQ, SCAN_CB), lambda i, j: (i, j))
    return pl.pallas_call(
        body, name="scan_fwd", grid=(n // SEQ, c // SCAN_CB), in_specs=[spec, spec], out_specs=spec,
        out_shape=jax.ShapeDtypeStruct((n, c), F32),
        compiler_params=pltpu.CompilerParams(dimension_semantics=("parallel", "parallel")),
    )(a, u)


def scan_bwd(a, h, g):
    n, c = a.shape

    def body(a_ref, h_ref, g_ref, da_ref, du_ref):
        def step(k, carry):
            t = SEQ - 1 - k
            dh = g_ref[pl.ds(t, 1), :] + carry
            du_ref[pl.ds(t, 1), :] = dh
            h_prev = jnp.where(t > 0, h_ref[pl.ds(jnp.maximum(t - 1, 0), 1), :], 0.0)
            da_ref[pl.ds(t, 1), :] = dh * h_prev
            return a_ref[pl.ds(t, 1), :] * dh

        lax.fori_loop(0, SEQ, step, jnp.zeros((1, SCAN_CB), F32), unroll=8)

    spec = pl.BlockSpec((SEQ, SCAN_CB), lambda i, j: (i, j))
    return pl.pallas_call(
        body, name="scan_bwd", grid=(n // SEQ, c // SCAN_CB), in_specs=[spec, spec, spec], out_specs=[spec, spec],
        out_shape=[jax.ShapeDtypeStruct((n, c), F32), jax.ShapeDtypeStruct((n, c), F32)],
        compiler_params=pltpu.CompilerParams(dimension_semantics=("parallel", "parallel")),
    )(a, h, g)


def _split3_dot(m, x):
    hi = x.astype(BF16)
    r1 = x - hi.astype(F32)
    mid = r1.astype(BF16)
    lo = (r1 - mid.astype(F32)).astype(BF16)
    dot = functools.partial(jnp.dot, preferred_element_type=F32)
    return dot(m, hi) + dot(m, mid) + dot(m, lo)


def seq_cumsum(name, xs, signs, reverse):
    n, w = xs[0].shape
    nx = len(xs)
    rb = min(CUM_RB, SEQ)

    def body(*refs):
        x = None
        for r, sg in zip(refs[:nx], signs):
            x = sg * r[...] if x is None else x + sg * r[...]
        q0 = pl.program_id(1) * rb
        row = q0 + lax.broadcasted_iota(jnp.int32, (rb, SEQ), 0)
        col = lax.broadcasted_iota(jnp.int32, (rb, SEQ), 1)
        tri = ((col >= row) if reverse else (col <= row)).astype(BF16)
        refs[nx][...] = _split3_dot(tri, x)

    return pl.pallas_call(
        body, name=name, grid=(n // SEQ, SEQ // rb),
        in_specs=[pl.BlockSpec((SEQ, w), lambda i, j: (i, 0)) for _ in xs],
        out_specs=pl.BlockSpec((rb, w), lambda i, j: (i * (SEQ // rb) + j, 0)),
        out_shape=jax.ShapeDtypeStruct((n, w), F32),
        compiler_params=pltpu.CompilerParams(dimension_semantics=("parallel", "parallel")),
    )(*xs)


N_PAIRS = N_HEADS // 2


def _dot_nt(a, b):
    return lax.dot_general(a, b, _DIMS["nt"], preferred_element_type=F32)


def _dot_tn(a, b):
    return lax.dot_general(a, b, _DIMS["tn"], preferred_element_type=F32)


def _dot_nn(a, b):
    return lax.dot_general(a, b, _DIMS["nn"], preferred_element_type=F32)


def _split2_dot(x, m):
    hi = x.astype(BF16)
    lo = (x - hi.astype(F32)).astype(BF16)
    return _dot_nn(hi, m) + _dot_nn(lo, m)


def _head_mask(j):
    lane = lax.broadcasted_iota(jnp.int32, (1, LANES), 1)
    return (lane // HEAD_DIM) == j


def _lane_pick(x, h):
    lane = lax.broadcasted_iota(jnp.int32, x.shape, 1)
    return jnp.sum(jnp.where(lane == h, x, 0.0), axis=1, keepdims=True)


def _lane_put(col, h):
    lane = lax.broadcasted_iota(jnp.int32, (col.shape[0], LANES), 1)
    return jnp.where(lane == h, col, 0.0)


def _softplus(z):
    return jnp.maximum(z, 0.0) + jnp.log(1.0 + jnp.exp(-jnp.abs(z)))


def _qkv_specs():
    return [pl.BlockSpec((SEQ, LANES), lambda b, p: (b, p)),
            pl.BlockSpec((SEQ, LANES), lambda b, p: (b, N_PAIRS + p)),
            pl.BlockSpec((SEQ, LANES), lambda b, p: (b, 2 * N_PAIRS + p))]


def _pair_spec():
    return pl.BlockSpec((SEQ, LANES), lambda b, p: (b, p))


def _below_diagonal(strictly):
    t = ATT_BLK
    row = lax.broadcasted_iota(jnp.int32, (t, t), 0)
    col = lax.broadcasted_iota(jnp.int32, (t, t), 1)
    return (row > col) if strictly else (row >= col)


def _over_key_blocks(qi, step, init, reverse):
    t = ATT_BLK
    q0 = pl.multiple_of(qi * t, t)

    def off_diagonal(kk, carry):
        ki = (qi - 1 - kk) if reverse else kk
        return step(pl.multiple_of(ki * t, t), carry, False)

    if reverse:
        return lax.fori_loop(0, qi, off_diagonal, step(q0, init, True))
    return step(q0, lax.fori_loop(0, qi, off_diagonal, init), True)


def _masked_q(qb, j):
    return (jnp.where(_head_mask(j), qb, 0.0) * ATT_SCALE).astype(BF16)


def sb_attn_fwd(qkv):
    n = qkv.shape[0]
    t = ATT_BLK

    def body(q_ref, k_ref, v_ref, o_ref, tot_ref):
        pair = pl.program_id(1)
        strict = _below_diagonal(True)
        later = strict.astype(BF16)

        @pl.when(pair == 0)
        def _():
            tot_ref[...] = jnp.zeros_like(tot_ref)

        def q_block(qi, _):
            q0 = pl.multiple_of(qi * t, t)
            qb = q_ref[pl.ds(q0, t), :]
            qms = [_masked_q(qb, j) for j in range(2)]

            def step(k0, carry, diagonal):
                kb = k_ref[pl.ds(k0, t), :].astype(BF16)
                vb = v_ref[pl.ds(k0, t), :].astype(BF16)
                new = []
                for j in range(2):
                    run_l, acc = carry[j]
                    z = _dot_nt(qms[j], kb)
                    sp = _softplus(z)
                    log_keep = jnp.where(strict, -sp, 0.0) if diagonal else -sp
                    att = jnp.exp((z - sp) + _split2_dot(log_keep, later) + run_l)
                    if diagonal:
                        att = jnp.where(strict, att, 0.0)
                    new.append((run_l + jnp.sum(log_keep, axis=1, keepdims=True), acc + _dot_nn(att.astype(BF16), vb)))
                return tuple(new)

            init = ((jnp.zeros((t, 1), F32), jnp.zeros((t, LANES), F32)),) * 2
            (tot0, acc0), (tot1, acc1) = _over_key_blocks(qi, step, init, reverse=True)
            o_ref[pl.ds(q0, t), :] = jnp.where(_head_mask(0), acc0, acc1)
            tot_ref[pl.ds(q0, t), :] += _lane_put(tot0, 2 * pair) + _lane_put(tot1, 2 * pair + 1)
            return 0

        lax.fori_loop(0, SEQ // t, q_block, 0)

    batch_spec = pl.BlockSpec((SEQ, LANES), lambda b, p: (b, 0))
    return pl.pallas_call(
        body, name="sb_attn_fwd", grid=(n // SEQ, N_PAIRS), in_specs=_qkv_specs(), out_specs=[_pair_spec(), batch_spec],
        out_shape=[jax.ShapeDtypeStruct((n, ATT_W), F32), jax.ShapeDtypeStruct((n, LANES), F32)],
        compiler_params=pltpu.CompilerParams(dimension_semantics=("parallel", "arbitrary")),
    )(qkv, qkv, qkv)


def sb_attn_bwd(qkv, tot, do):
    n = qkv.shape[0]
    t = ATT_BLK

    def body(q_ref, k_ref, v_ref, tot_ref, do_ref, dq_ref, dk_ref, dv_ref, dk_acc, dv_acc):
        pair = pl.program_id(1)
        strict = _below_diagonal(True)
        upto = jnp.logical_not(strict).astype(BF16)
        dk_acc[...] = jnp.zeros_like(dk_acc)
        dv_acc[...] = jnp.zeros_like(dv_acc)

        def q_block(qi, _):
            q0 = pl.multiple_of(qi * t, t)
            qb = q_ref[pl.ds(q0, t), :]
            tot_q = tot_ref[pl.ds(q0, t), :]
            dob = do_ref[pl.ds(q0, t), :].astype(F32)
            qms = [_masked_q(qb, j) for j in range(2)]
            doms = [jnp.where(_head_mask(j), dob, 0.0).astype(BF16) for j in range(2)]
            totals = [_lane_pick(tot_q, 2 * pair + j) for j in range(2)]

            def step(k0, carry, diagonal):
                kb = k_ref[pl.ds(k0, t), :].astype(BF16)
                vb = v_ref[pl.ds(k0, t), :].astype(BF16)
                new = []
                dk = jnp.zeros((t, LANES), F32)
                dv = jnp.zeros((t, LANES), F32)
                for j in range(2):
                    run_l, run_g, dq = carry[j]
                    z = _dot_nt(qms[j], kb)
                    sp = _softplus(z)
                    log_keep = jnp.where(strict, -sp, 0.0) if diagonal else -sp
                    log_beta = z - sp
                    att = jnp.exp(log_beta + (totals[j] - (run_l + _split2_dot(log_keep, upto))))
                    if diagonal:
                        att = jnp.where(strict, att, 0.0)
                    g = att * _dot_nt(doms[j], vb)
                    dv = dv + _dot_tn(att.astype(BF16), doms[j])
                    dz = g - jnp.exp(log_beta) * (run_g + _split2_dot(g, upto))
                    if diagonal:
                        dz = jnp.where(strict, dz, 0.0)
                    dz = dz.astype(BF16)
                    dk = dk + _dot_tn(dz, qms[j])
                    new.append((run_l + jnp.sum(log_keep, axis=1, keepdims=True), run_g + jnp.sum(g, axis=1, keepdims=True),
                                dq + _dot_nn(dz, kb)))
                dk_acc[pl.ds(k0, t), :] += dk
                dv_acc[pl.ds(k0, t), :] += dv
                return tuple(new)

            zero = jnp.zeros((t, 1), F32)
            init = ((zero, zero, jnp.zeros((t, LANES), F32)),) * 2
            (_, _, dq0), (_, _, dq1) = _over_key_blocks(qi, step, init, reverse=False)
            dq_ref[pl.ds(q0, t), :] = (jnp.where(_head_mask(0), dq0, dq1) * ATT_SCALE).astype(BF16)
            return 0

        lax.fori_loop(0, SEQ // t, q_block, 0)
        dk_ref[...] = dk_acc[...].astype(BF16)
        dv_ref[...] = dv_acc[...].astype(BF16)

    out = jax.ShapeDtypeStruct((n, ATT_W), BF16)
    batch_spec = pl.BlockSpec((SEQ, LANES), lambda b, p: (b, 0))
    return pl.pallas_call(
        body, name="sb_attn_bwd", grid=(n // SEQ, N_PAIRS), in_specs=_qkv_specs() + [batch_spec, _pair_spec()],
        out_specs=[_pair_spec()] * 3, out_shape=[out, out, out],
        scratch_shapes=[pltpu.VMEM((SEQ, LANES), F32), pltpu.VMEM((SEQ, LANES), F32)],
        compiler_params=pltpu.CompilerParams(dimension_semantics=("parallel", "parallel")),
    )(qkv, qkv, qkv, tot, do)


NEG_BIG = -1e30


def fox_attn_fwd(qkv, cum, cum_t):
    n = qkv.shape[0]
    t = ATT_BLK

    def body(q_ref, k_ref, v_ref, cum_ref, cumt_ref, o_ref, lse_ref):
        pair = pl.program_id(1)
        causal = _below_diagonal(False)

        @pl.when(pair == 0)
        def _():
            lse_ref[...] = jnp.zeros_like(lse_ref)

        def q_block(qi, _):
            q0 = pl.multiple_of(qi * t, t)
            qb = q_ref[pl.ds(q0, t), :]
            cum_q = cum_ref[pl.ds(q0, t), :]
            qms = [_masked_q(qb, j) for j in range(2)]
            cqs = [_lane_pick(cum_q, 2 * pair + j) for j in range(2)]

            def step(k0, carry, diagonal):
                kb = k_ref[pl.ds(k0, t), :].astype(BF16)
                vb = v_ref[pl.ds(k0, t), :].astype(BF16)
                new = []
                for j in range(2):
                    m, l, acc = carry[j]
                    z = _dot_nt(qms[j], kb) + cqs[j] - cumt_ref[pl.ds(2 * pair + j, 1), pl.ds(k0, t)]
                    if diagonal:
                        z = jnp.where(causal, z, NEG_BIG)
                    m_new = jnp.maximum(m, jnp.max(z, axis=1, keepdims=True))
                    p = jnp.exp(z - m_new)
                    alpha = jnp.exp(m - m_new)
                    new.append((m_new, alpha * l + jnp.sum(p, axis=1, keepdims=True), alpha * acc + _dot_nn(p.astype(BF16), vb)))
                return tuple(new)

            init = ((jnp.full((t, 1), NEG_BIG, F32), jnp.zeros((t, 1), F32), jnp.zeros((t, LANES), F32)),) * 2
            (m0, l0, acc0), (m1, l1, acc1) = _over_key_blocks(qi, step, init, reverse=False)
            o_ref[pl.ds(q0, t), :] = jnp.where(_head_mask(0), acc0 / l0, acc1 / l1)
            lse_ref[pl.ds(q0, t), :] += _lane_put(m0 + jnp.log(l0), 2 * pair) + _lane_put(m1 + jnp.log(l1), 2 * pair + 1)
            return 0

        lax.fori_loop(0, SEQ // t, q_block, 0)

    batch_spec = pl.BlockSpec((SEQ, LANES), lambda b, p: (b, 0))
    return pl.pallas_call(
        body, name="fox_attn_fwd", grid=(n // SEQ, N_PAIRS),
        in_specs=_qkv_specs() + [batch_spec, pl.BlockSpec((None, N_HEADS, SEQ), lambda b, p: (b, 0, 0))],
        out_specs=[_pair_spec(), batch_spec],
        out_shape=[jax.ShapeDtypeStruct((n, ATT_W), F32), jax.ShapeDtypeStruct((n, LANES), F32)],
        compiler_params=pltpu.CompilerParams(dimension_semantics=("parallel", "arbitrary")),
    )(qkv, qkv, qkv, cum, cum_t)


def fox_attn_bwd(qkv, cum, cum_t, lse, o, do):
    n = qkv.shape[0]
    t = ATT_BLK

    def body(q_ref, k_ref, v_ref, cum_ref, cumt_ref, lse_ref, o_ref, do_ref, dq_ref, dk_ref, dv_ref, dcq_ref, dck_ref,
             dk_acc, dv_acc):
        pair = pl.program_id(1)
        causal = _below_diagonal(False)
        dk_acc[...] = jnp.zeros_like(dk_acc)
        dv_acc[...] = jnp.zeros_like(dv_acc)

        @pl.when(pair == 0)
        def _():
            dcq_ref[...] = jnp.zeros_like(dcq_ref)
            dck_ref[...] = jnp.zeros_like(dck_ref)

        def q_block(qi, _):
            q0 = pl.multiple_of(qi * t, t)
            qb = q_ref[pl.ds(q0, t), :]
            ob = o_ref[pl.ds(q0, t), :]
            dob = do_ref[pl.ds(q0, t), :].astype(F32)
            cum_q = cum_ref[pl.ds(q0, t), :]
            lse_q = lse_ref[pl.ds(q0, t), :]
            qms = [_masked_q(qb, j) for j in range(2)]
            dom32 = [jnp.where(_head_mask(j), dob, 0.0) for j in range(2)]
            doms = [d.astype(BF16) for d in dom32]
            deltas = [jnp.sum(d * ob, axis=1, keepdims=True) for d in dom32]
            cqs = [_lane_pick(cum_q, 2 * pair + j) for j in range(2)]
            lqs = [_lane_pick(lse_q, 2 * pair + j) for j in range(2)]

            def step(k0, carry, diagonal):
                kb = k_ref[pl.ds(k0, t), :].astype(BF16)
                vb = v_ref[pl.ds(k0, t), :].astype(BF16)
                new = []
                dk = jnp.zeros((t, LANES), F32)
                dv = jnp.zeros((t, LANES), F32)
                for j in range(2):
                    dq, dcq = carry[j]
                    z = _dot_nt(qms[j], kb) + cqs[j] - cumt_ref[pl.ds(2 * pair + j, 1), pl.ds(k0, t)]
                    if diagonal:
                        z = jnp.where(causal, z, NEG_BIG)
                    p = jnp.exp(z - lqs[j])
                    dv = dv + _dot_tn(p.astype(BF16), doms[j])
                    dz = p * (_dot_nt(doms[j], vb) - deltas[j])
                    dzb = dz.astype(BF16)
                    dk = dk + _dot_tn(dzb, qms[j])
                    dck_ref[pl.ds(2 * pair + j, 1), pl.ds(k0, t)] += jnp.sum(dz, axis=0, keepdims=True)
                    new.append((dq + _dot_nn(dzb, kb), dcq + jnp.sum(dz, axis=1, keepdims=True)))
                dk_acc[pl.ds(k0, t), :] += dk
                dv_acc[pl.ds(k0, t), :] += dv
                return tuple(new)

            init = ((jnp.zeros((t, LANES), F32), jnp.zeros((t, 1), F32)),) * 2
            (dq0, dcq0), (dq1, dcq1) = _over_key_blocks(qi, step, init, reverse=False)
            dq_ref[pl.ds(q0, t), :] = (jnp.where(_head_mask(0), dq0, dq1) * ATT_SCALE).astype(BF16)
            dcq_ref[pl.ds(q0, t), :] += _lane_put(dcq0, 2 * pair) + _lane_put(dcq1, 2 * pair + 1)
            return 0

        lax.fori_loop(0, SEQ // t, q_block, 0)
        dk_ref[...] = dk_acc[...].astype(BF16)
        dv_ref[...] = dv_acc[...].astype(BF16)

    batch_spec = pl.BlockSpec((SEQ, LANES), lambda b, p: (b, 0))
    t_spec = pl.BlockSpec((None, N_HEADS, SEQ), lambda b, p: (b, 0, 0))
    out = jax.ShapeDtypeStruct((n, ATT_W), BF16)
    return pl.pallas_call(
        body, name="fox_attn_bwd", grid=(n // SEQ, N_PAIRS),
        in_specs=_qkv_specs() + [batch_spec, t_spec, batch_spec, _pair_spec(), _pair_spec()],
        out_specs=[_pair_spec()] * 3 + [batch_spec, t_spec],
        scratch_shapes=[pltpu.VMEM((SEQ, LANES), F32), pltpu.VMEM((SEQ, LANES), F32)],
        out_shape=[out, out, out, jax.ShapeDtypeStruct((n, LANES), F32), jax.ShapeDtypeStruct((n // SEQ, N_HEADS, SEQ), F32)],
        compiler_params=pltpu.CompilerParams(dimension_semantics=("parallel", "arbitrary")),
    )(qkv, qkv, qkv, cum, cum_t, lse, o, do)


_HBM = pl.BlockSpec(memory_space=pl.ANY)


def _my_place():
    return lax.axis_index("x"), lax.axis_index("y"), lax.axis_index("c")


def my_index():
    mx, my, mc = _my_place()
    return 4 * mx + 2 * my + mc


def all_gather(name, xs):
    single = not isinstance(xs, (list, tuple))
    xs = [xs] if single else list(xs)
    na = len(xs)

    def body(*refs):
        x_refs, out_refs = refs[:na], refs[na:2 * na]
        send_sems, recv_sems, local_sems = refs[2 * na:]
        mx, my, mc = _my_place()
        me, sibling = (mx, my, mc), (mx, my, 1 - mc)
        chips = [(1 - mx, my), (mx, 1 - my), (1 - mx, 1 - my)]

        def slot(a, px, py, pc):
            return out_refs[a].at[4 * px + 2 * py + pc]

        def copy(a, k, block, to, src=None):
            return pltpu.make_async_remote_copy(
                src_ref=slot(a, *block) if src is None else src, dst_ref=slot(a, *block),
                send_sem=send_sems.at[7 * a + k], recv_sem=recv_sems.at[7 * a + k], device_id=to, device_id_type=MESH)

        mine = [pltpu.make_async_copy(x_refs[a], slot(a, *me), local_sems.at[a]) for a in range(na)]
        for cp in mine:
            cp.start()
        first = []
        for j, chip in enumerate(chips):
            first += [copy(a, 1 + j, me, (*chip, mc), src=x_refs[a]) for a in range(na)]
        first += [copy(a, 0, me, sibling, src=x_refs[a]) for a in range(na)]
        for cp in first:
            cp.start()
        passed = []
        for j, chip in enumerate(chips):
            for a in range(na):
                copy(a, 1 + j, (*chip, mc), me).wait_recv()
                passed.append(copy(a, 4 + j, (*chip, mc), sibling))
                passed[-1].start()
        for a in range(na):
            copy(a, 0, sibling, me).wait_recv()
        for j, chip in enumerate(chips):
            for a in range(na):
                copy(a, 4 + j, (*chip, 1 - mc), me).wait_recv()
        for cp in first + passed:
            cp.wait_send()
        for cp in mine:
            cp.wait()

    outs = pl.pallas_call(
        body, name=name, in_specs=[_HBM] * na, out_specs=[_HBM] * na,
        out_shape=[jax.ShapeDtypeStruct((N_DEV,) + x.shape, x.dtype) for x in xs],
        scratch_shapes=[pltpu.SemaphoreType.DMA((7 * na,)), pltpu.SemaphoreType.DMA((7 * na,)), pltpu.SemaphoreType.DMA((na,))],
    )(*xs)
    return outs[0] if single else list(outs)


def swap_with_sibling(name, gs):
    na = len(gs)

    def body(*refs):
        g_refs, out_refs, send_sems, recv_sems = refs[:na], refs[na:2 * na], refs[2 * na], refs[2 * na + 1]
        mx, my, mc = _my_place()
        copies = [pltpu.make_async_remote_copy(
            src_ref=g_refs[a].at[k, 1 - mc], dst_ref=out_refs[a].at[k], send_sem=send_sems.at[4 * a + k],
            recv_sem=recv_sems.at[4 * a + k], device_id=(mx, my, 1 - mc), device_id_type=MESH)
            for a in range(na) for k in range(4)]
        for cp in copies:
            cp.start()
        for cp in copies:
            cp.wait_recv()
        for cp in copies:
            cp.wait_send()

    return list(pl.pallas_call(
        body, name=name, in_specs=[_HBM] * na, out_specs=[_HBM] * na,
        out_shape=[jax.ShapeDtypeStruct((4,) + g.shape[2:], g.dtype) for g in gs],
        scratch_shapes=[pltpu.SemaphoreType.DMA((4 * na,)), pltpu.SemaphoreType.DMA((4 * na,))],
    )(*gs))


def swap_with_chips(name, ts):
    na = len(ts)

    def body(*refs):
        t_refs, out_refs, send_sems, recv_sems = refs[:na], refs[na:2 * na], refs[2 * na], refs[2 * na + 1]
        mx, my, mc = _my_place()
        chips = [(1 - mx, my), (mx, 1 - my), (1 - mx, 1 - my)]
        copies = [pltpu.make_async_remote_copy(
            src_ref=t_refs[a].at[2 * px + py], dst_ref=out_refs[a].at[k], send_sem=send_sems.at[3 * a + k],
            recv_sem=recv_sems.at[3 * a + k], device_id=(px, py, mc), device_id_type=MESH)
            for a in range(na) for k, (px, py) in enumerate(chips)]
        for cp in copies:
            cp.start()
        for cp in copies:
            cp.wait_recv()
        for cp in copies:
            cp.wait_send()

    return list(pl.pallas_call(
        body, name=name, in_specs=[_HBM] * na, out_specs=[_HBM] * na,
        out_shape=[jax.ShapeDtypeStruct((3,) + t.shape[1:], t.dtype) for t in ts],
        scratch_shapes=[pltpu.SemaphoreType.DMA((3 * na,)), pltpu.SemaphoreType.DMA((3 * na,))],
    )(*ts))


def _pick_rows(n, target):
    best = None
    for t in range(8, min(n, target) + 1, 8):
        if n % t == 0:
            best = t
    return best if best is not None else n


def add_blocks(name, parts, out_dtype, rows=512):
    r, w = parts[0].shape
    tr = _pick_rows(r, rows)

    def body(*refs):
        acc = refs[0][...].astype(F32)
        for ref in refs[1:-1]:
            acc = acc + ref[...].astype(F32)
        refs[-1][...] = acc.astype(refs[-1].dtype)

    spec = pl.BlockSpec((tr, w), lambda i: (i, 0))
    return pl.pallas_call(
        body, name=name, grid=(r // tr,), in_specs=[spec] * len(parts), out_specs=spec,
        out_shape=jax.ShapeDtypeStruct((r, w), out_dtype),
        compiler_params=pltpu.CompilerParams(dimension_semantics=("parallel",)),
    )(*parts)


def sum_rows(name, x):
    def body(x_ref, o_ref):
        o_ref[...] = jnp.sum(x_ref[...], axis=0, keepdims=True)

    return pl.pallas_call(body, name=name, out_shape=jax.ShapeDtypeStruct((1, x.shape[1]), F32))(x)


def reduce_scatter(names, g8s):
    mx, my, mc = _my_place()
    gs = [g8.reshape((4, 2) + g8.shape[1:]) for g8 in g8s]
    from_sibling = swap_with_sibling("rs_sibling", gs)
    chip_sums = []
    for n, g, fs in zip(names, gs, from_sibling):
        w = g.shape[-1]
        mine = lax.dynamic_index_in_dim(g, mc, axis=1, keepdims=False)
        chip_sums.append(add_blocks(f"rs_chip_sum_{n}", [mine.reshape(-1, w), fs.reshape(-1, w)], BF16).reshape(fs.shape))
    from_chips = swap_with_chips("rs_chips", chip_sums)
    outs = []
    for n, cs, fc in zip(names, chip_sums, from_chips):
        w = cs.shape[-1]
        own = lax.dynamic_index_in_dim(cs, 2 * mx + my, axis=0, keepdims=False)
        outs.append(add_blocks(f"rs_total_{n}", [own.reshape(-1, w)] + [fc[k].reshape(-1, w) for k in range(3)], F32)
                    .reshape(cs.shape[1:]))
    return outs


def _pack(arrays, width, row_mult, dtype, lead=0):
    parts, metas = [], []
    for a in arrays:
        lead_shape = a.shape[:lead]
        size = int(np.prod(a.shape[lead:]))
        chunk = row_mult * width
        padded = -(-size // chunk) * chunk
        flat = a.astype(dtype).reshape(lead_shape + (size,))
        if padded != size:
            flat = jnp.pad(flat, [(0, 0)] * lead + [(0, padded - size)])
        parts.append(flat.reshape(lead_shape + (padded // width, width)))
        metas.append((a.shape[lead:], size, padded // width))
    return jnp.concatenate(parts, axis=lead), metas


def _unpack(slab, metas, lead=0):
    out, r0 = [], 0
    for shape, size, rows in metas:
        part = lax.slice_in_dim(slab, r0, r0 + rows, axis=lead)
        lead_shape = part.shape[:lead]
        flat = part.reshape(lead_shape + (rows * part.shape[-1],))
        out.append(lax.slice_in_dim(flat, 0, size, axis=lead).reshape(lead_shape + tuple(shape)))
        r0 += rows
    return out


def _f_adamw(w, g, m, v):
    m = ADAM_B1 * m + (1.0 - ADAM_B1) * g
    v = ADAM_B2 * v + (1.0 - ADAM_B2) * (g * g)
    m_hat = m / (1.0 - ADAM_B1 ** ADAM_STEP)
    v_hat = v / (1.0 - ADAM_B2 ** ADAM_STEP)
    delta = (-ADAM_LR) * (m_hat / (jnp.sqrt(v_hat) + ADAM_EPS) + ADAM_WD * w)
    return delta, m, v


def adamw(name, w, g, m, v):
    shape = w.shape
    w2 = shape[-1]
    flat = [a.reshape(-1, w2) for a in (w, g, m, v)]
    tm = _pick_rows(flat[0].shape[0], 256)
    outs = ew_fwd(name, _f_adamw, flat, [], [], [w2] * 3, [F32] * 3, tm=tm)
    return [o.reshape(shape) for o in outs]


WEIGHTS = ["ffn1_norm", "ffn1_w1", "ffn1_w3", "ffn1_w2", "mix_norm", "w_in", "conv_w", "conv_b", "rg_wa", "rg_ba", "rg_wx",
           "rg_bx", "rg_lam", "fox_bf", "merge_b", "w_rg", "w_sb", "w_fox", "w_o", "ffn2_norm", "ffn2_w1", "ffn2_w3",
           "ffn2_w2", "ada_w", "ada_b", "final_norm", "final_ada_w", "final_ada_b"]
GATHERED = {"ffn1_w1": 2, "ffn1_w3": 2, "ffn1_w2": 1, "w_in": 2, "w_rg": 1, "w_sb": 2, "w_fox": 2, "w_o": 1,
            "ffn2_w1": 2, "ffn2_w3": 2, "ffn2_w2": 1}
REPLICATED = ["ffn1_norm", "mix_norm", "conv_b", "rg_wa", "rg_ba", "rg_wx", "rg_bx", "rg_lam", "fox_bf", "merge_b",
              "ffn2_norm", "final_norm"]
IN_CUTS = (0, 1024, 2048, 3584, 5120, 5128, 8200)


def _unshard(g, axis):
    g = jnp.moveaxis(g, 0, axis)
    shape = g.shape
    return g.reshape(shape[:axis] + (shape[axis] * shape[axis + 1],) + shape[axis + 2:])


def _reshard(full, axis):
    shape = full.shape
    g = full.reshape(shape[:axis] + (N_DEV, shape[axis] // N_DEV) + shape[axis + 1:])
    return jnp.moveaxis(g, axis, 0)


def _block_diag(w):
    nb, bd, _ = w.shape
    eye = jnp.eye(nb, dtype=bool)[:, None, :, None]
    return jnp.where(eye, w[:, :, None, :], 0.0).reshape(nb * bd, nb * bd)


def _diag_blocks(m, nb=RG_BLOCKS):
    bd = m.shape[0] // nb
    return jnp.stack([m[k * bd:(k + 1) * bd, k * bd:(k + 1) * bd] for k in range(nb)])


def _pad_lanes(a, width=LANES):
    return jnp.pad(a, [(0, 0)] * (a.ndim - 1) + [(0, width - a.shape[-1])])


def _bp(m, k, which):
    return m[:, k, which][:, None, :]


def _f_silu(c):
    return c * jax.nn.sigmoid(c)


def _f_add_bias(a, b):
    return a + b


FFN_TM = 512
FFN_TN = 1408


def ffn_up(name, h, w1, w3):
    n, k = h.shape
    f = w1.shape[1]
    tm, tn = min(FFN_TM, n), _pick_tile(f, FFN_TN)

    def body(h_ref, w1_ref, w3_ref, a_ref, b_ref, s_ref):
        hv = h_ref[...]
        a = jnp.dot(hv, w1_ref[...], preferred_element_type=F32)
        b = jnp.dot(hv, w3_ref[...], preferred_element_type=F32)
        a_ref[...] = a.astype(BF16)
        b_ref[...] = b.astype(BF16)
        s_ref[...] = ((a * jax.nn.sigmoid(a)) * b).astype(BF16)

    wspec = pl.BlockSpec((k, tn), lambda i, j: (0, j))
    ospec = pl.BlockSpec((tm, tn), lambda i, j: (i, j))
    out = jax.ShapeDtypeStruct((n, f), BF16)
    return pl.pallas_call(
        body, name=name, grid=(n // tm, f // tn), in_specs=[pl.BlockSpec((tm, k), lambda i, j: (i, 0)), wspec, wspec],
        out_specs=[ospec] * 3, out_shape=[out] * 3,
        compiler_params=pltpu.CompilerParams(dimension_semantics=("parallel", "parallel")),
    )(h, w1, w3)


def ffn_down_dx(name, dy, w2, a, b):
    n, k = dy.shape
    f = w2.shape[0]
    tm, tn = min(FFN_TM, n), _pick_tile(f, FFN_TN)

    def body(dy_ref, w2_ref, a_ref, b_ref, da_ref, db_ref):
        ds = _dot_nt(dy_ref[...], w2_ref[...])
        av = a_ref[...].astype(F32)
        sig = jax.nn.sigmoid(av)
        da_ref[...] = (ds * b_ref[...].astype(F32) * (sig * (1.0 + av * (1.0 - sig)))).astype(BF16)
        db_ref[...] = (ds * (av * sig)).astype(BF16)

    ospec = pl.BlockSpec((tm, tn), lambda i, j: (i, j))
    out = jax.ShapeDtypeStruct((n, f), BF16)
    return pl.pallas_call(
        body, name=name, grid=(n // tm, f // tn),
        in_specs=[pl.BlockSpec((tm, k), lambda i, j: (i, 0)), pl.BlockSpec((tn, k), lambda i, j: (j, 0)), ospec, ospec],
        out_specs=[ospec] * 2, out_shape=[out] * 2,
        compiler_params=pltpu.CompilerParams(dimension_semantics=("parallel", "parallel")),
    )(dy, w2, a, b)


def _ffn_fwd(tag, x, shift, scale, gate, gain, w1, w3, w2):
    h = ew_fwd(f"ffn_norm_{tag}", f_norm_mod, [x], [shift, scale], [gain], [D], [BF16])[0]
    a, b3, s = ffn_up(f"ffn_up_{tag}", h, w1, w3)
    y = matmul(f"ffn_down_{tag}", s, w2, "nn", tm=1024)
    xo = ew_fwd(f"ffn_res_{tag}", functools.partial(f_resid, 0.5), [x, y], [gate], [], [D], [F32])[0]
    return xo, (x, h, a, b3, s, y)


def _ffn_bwd(tag, dxo, saved, shift, scale, gate, gain, w1, w3, w2):
    x, h, a, b3, s, y = saved
    (dy,), (dgate,), _ = ew_bwd(f"ffn_res_bwd_{tag}", functools.partial(f_resid, 0.5), [x, y], [gate], [], [dxo], [None, BF16])
    da, db3 = ffn_down_dx(f"ffn_down_dx_{tag}", dy, w2, a, b3)
    dw2 = matmul(f"ffn_dw2_{tag}", s, dy, "tn", tm=1408, tn=256)
    dw1 = matmul(f"ffn_dw1_{tag}", h, da, "tn", tm=1024, tn=256)
    dw3 = matmul(f"ffn_dw3_{tag}", h, db3, "tn", tm=1024, tn=256)
    dh = matmul(f"ffn_up_dx_{tag}", [da, db3], [w1, w3], "nt", tm=1024)
    (dx,), (dshift, dscale), (dgain,) = ew_bwd(f"ffn_norm_bwd_{tag}", f_norm_mod, [x], [shift, scale], [gain], [dh], [F32],
                                               adds=[dxo])
    return dx, (dshift, dscale, dgate), dgain, dw1, dw3, dw2


def _mixer_fwd(tag, x, shift, scale, gate, p):
    h = ew_fwd(f"mix_norm_{tag}", f_norm_mod, [x], [shift, scale], [p["gain"]], [D], [BF16])[0]
    rgx = matmul(f"in_rgx_{tag}", h, p["w_rgx"], "nn")
    rgate = matmul(f"in_gate_{tag}", h, p["w_gate"], "nn")
    sbqkv = matmul(f"in_sb_{tag}", h, p["w_sbqkv"], "nn")
    foxqkv = matmul(f"in_fox_{tag}", h, p["w_foxqkv"], "nn")
    ff = matmul(f"in_forget_{tag}", h, p["w_f"], "nn")
    mg = matmul(f"in_merge_{tag}", h, p["w_merge"], "nn")
    xa = conv_fwd(rgx, p["conv_w8"], p["conv_b"])
    pre_r = matmul(f"rg_a_{tag}", xa, p["wa_bd"], "nn")
    pre_i = matmul(f"rg_x_{tag}", xa, p["wx_bd"], "nn")
    a, u = ew_fwd(f"rg_gates_{tag}", f_rg_gates, [pre_r, pre_i, xa], [], [p["ba"], p["bx"], p["lam"]], [D, D], [F32, F32])
    hs = scan_fwd(a, u)
    ya = ew_fwd(f"rg_out_{tag}", f_gelu_mul, [rgate, hs], [], [], [D], [BF16])[0]
    yb, sb_tot = sb_attn_fwd(sbqkv)
    lf = ew_fwd(f"fox_logf_{tag}", f_log_sigmoid_bias, [ff], [], [p["bf"]], [LANES], [F32])[0]
    cum = seq_cumsum(f"fox_cum_{tag}", [lf], [1.0], False)
    cum_t = cum.reshape(-1, SEQ, LANES)[:, :, :N_HEADS].transpose(0, 2, 1)
    yc, lse = fox_attn_fwd(foxqkv, cum, cum_t)
    pa = matmul(f"out_rg_{tag}", ya, p["w_rg"], "nn")
    pb = matmul(f"out_sb_{tag}", yb, p["w_sb"], "nn")
    pc = matmul(f"out_fox_{tag}", yc, p["w_fox"], "nn")
    mixed = ew_fwd(f"merge_{tag}", f_merge, [mg, pa, pb, pc], [], [p["merge_b"]], [D], [BF16])[0]
    y = matmul(f"out_o_{tag}", mixed, p["w_o"], "nn")
    xo = ew_fwd(f"mix_res_{tag}", functools.partial(f_resid, 1.0), [x, y], [gate], [], [D], [F32])[0]
    saved = dict(x=x, h=h, rgx=rgx, rgate=rgate, sbqkv=sbqkv, foxqkv=foxqkv, ff=ff, mg=mg, xa=xa, pre_r=pre_r, pre_i=pre_i,
                 a=a, hs=hs, ya=ya, yb=yb, sb_tot=sb_tot, cum=cum, cum_t=cum_t, yc=yc, lse=lse, pa=pa, pb=pb, pc=pc,
                 mixed=mixed, y=y)
    return xo, saved


def _mixer_bwd(tag, dxo, s, shift, scale, gate, p):
    (dy,), (dgate,), _ = ew_bwd(f"mix_res_bwd_{tag}", functools.partial(f_resid, 1.0), [s["x"], s["y"]], [gate], [], [dxo],
                                [None, BF16])
    dmixed = matmul(f"out_o_dx_{tag}", dy, p["w_o"], "nt")
    g = {"w_o": matmul(f"out_o_dw_{tag}", s["mixed"], dy, "tn", tm=1024, tn=256)}
    (dmg, dpa, dpb, dpc), _, (g["merge_b"],) = ew_bwd(
        f"merge_bwd_{tag}", f_merge, [s["mg"], s["pa"], s["pb"], s["pc"]], [], [p["merge_b"]], [dmixed], [BF16] * 4)
    dya = matmul(f"out_rg_dx_{tag}", dpa, p["w_rg"], "nt")
    g["w_rg"] = matmul(f"out_rg_dw_{tag}", s["ya"], dpa, "tn", tm=1024, tn=256)
    dyb = matmul(f"out_sb_dx_{tag}", dpb, p["w_sb"], "nt", out_dtype=BF16)
    g["w_sb"] = matmul(f"out_sb_dw_{tag}", s["yb"], dpb, "tn", tm=1024, tn=256)
    dyc = matmul(f"out_fox_dx_{tag}", dpc, p["w_fox"], "nt", out_dtype=BF16)
    g["w_fox"] = matmul(f"out_fox_dw_{tag}", s["yc"], dpc, "tn", tm=1024, tn=256)
    dq_c, dk_c, dv_c, dcq, dck = fox_attn_bwd(s["foxqkv"], s["cum"], s["cum_t"], s["lse"], s["yc"], dyc)
    dck_rows = _pad_lanes(dck.transpose(0, 2, 1).reshape(-1, N_HEADS))
    dlf = seq_cumsum(f"fox_cum_bwd_{tag}", [dcq, dck_rows], [1.0, -1.0], True)
    (dff,), _, (dbf,) = ew_bwd(f"fox_logf_bwd_{tag}", f_log_sigmoid_bias, [s["ff"]], [], [p["bf"]], [dlf], [BF16])
    g["fox_bf"] = dbf[0, :N_HEADS]
    dq_b, dk_b, dv_b = sb_attn_bwd(s["sbqkv"], s["sb_tot"], dyb)
    (drgate, dhs), _, _ = ew_bwd(f"rg_out_bwd_{tag}", f_gelu_mul, [s["rgate"], s["hs"]], [], [], [dya], [BF16, F32])
    da, du = scan_bwd(s["a"], s["hs"], dhs)
    (dpre_r, dpre_i, dxa1), _, (g["rg_ba"], g["rg_bx"], g["rg_lam"]) = ew_bwd(
        f"rg_gates_bwd_{tag}", f_rg_gates, [s["pre_r"], s["pre_i"], s["xa"]], [], [p["ba"], p["bx"], p["lam"]], [da, du],
        [BF16, BF16, F32])
    dxa2 = matmul(f"rg_dx_{tag}", [dpre_r, dpre_i], [p["wa_bd"], p["wx_bd"]], "nt")
    g["rg_wa"] = _diag_blocks(matmul(f"rg_a_dw_{tag}", s["xa"], dpre_r, "tn", tm=512, tn=256))
    g["rg_wx"] = _diag_blocks(matmul(f"rg_x_dw_{tag}", s["xa"], dpre_i, "tn", tm=512, tn=256))
    drgx, dwb = conv_bwd(s["rgx"], p["conv_w8"], dxa1, dxa2)
    g["conv_w"] = dwb[:CONV_K]
    g["conv_b"] = dwb[CONV_K]
    cots = [drgx, drgate, dq_b, dk_b, dv_b, dq_c, dk_c, dv_c, dff, dmg]
    w_sb3 = [p["w_sbqkv"][:, k * ATT_W:(k + 1) * ATT_W] for k in range(3)]
    w_fox3 = [p["w_foxqkv"][:, k * ATT_W:(k + 1) * ATT_W] for k in range(3)]
    ws = [p["w_rgx"], p["w_gate"]] + w_sb3 + w_fox3 + [p["w_f"], p["w_merge"]]
    dh = matmul(f"in_dx_{tag}", cots, ws, "nt", tm=512)
    dws = [matmul(f"in_dw{k}_{tag}", s["h"], ct, "tn", tm=1024, tn=256) for k, ct in enumerate(cots)]
    dws[8] = dws[8][:, :N_HEADS]
    g["w_in"] = jnp.concatenate(dws, axis=1)
    (dx,), (dshift, dscale), (g["mix_norm"],) = ew_bwd(f"mix_norm_bwd_{tag}", f_norm_mod, [s["x"]], [shift, scale], [p["gain"]],
                                                       [dh], [F32], adds=[dxo])
    return dx, (dshift, dscale, dgate), g


def _final_loss(x, target, shift, scale, gain):
    n = x.shape[0]
    tm = EW_ROWS
    tpb = SEQ // tm

    def body(x_ref, t_ref, sh_ref, sc_ref, g_ref, loss_ref, dx_ref, dsh_ref, dsc_ref, dg_ref):
        i = pl.program_id(0)
        out, vjp = jax.vjp(f_norm_mod, x_ref[...], sh_ref[...], sc_ref[...], g_ref[...])
        diff = out - t_ref[...]
        dx, dsh, dsc, dg = vjp(diff * (1.0 / D))
        dx_ref[...] = dx
        sq = jnp.sum(jnp.sum(diff * diff, axis=1, keepdims=True), axis=0, keepdims=True)

        @pl.when(i % tpb == 0)
        def _():
            dsh_ref[...] = jnp.zeros_like(dsh_ref)
            dsc_ref[...] = jnp.zeros_like(dsc_ref)

        @pl.when(i == 0)
        def _():
            dg_ref[...] = jnp.zeros_like(dg_ref)
            loss_ref[...] = jnp.zeros_like(loss_ref)

        dsh_ref[...] += dsh
        dsc_ref[...] += dsc
        dg_ref[...] += dg
        loss_ref[...] += jnp.broadcast_to(sq, (1, LANES)) * (0.5 / D)

    row, bp, gp = _row_spec(D, tm), _bparam_spec(D, tpb), _gparam_spec((1, D))
    return pl.pallas_call(
        body, name="final_loss", grid=(n // tm,), in_specs=[row, row, bp, bp, gp],
        out_specs=[_gparam_spec((1, LANES)), row, bp, bp, gp],
        out_shape=[jax.ShapeDtypeStruct((1, LANES), F32), jax.ShapeDtypeStruct((n, D), F32),
                   jax.ShapeDtypeStruct(shift.shape, F32), jax.ShapeDtypeStruct(scale.shape, F32),
                   jax.ShapeDtypeStruct((1, D), F32)],
        compiler_params=pltpu.CompilerParams(dimension_semantics=("arbitrary",)),
    )(x, target, shift, scale, gain)


def kernel(x, c, ffn1_norm, ffn1_w1, ffn1_w3, ffn1_w2, mix_norm, w_in, conv_w, conv_b, rg_wa, rg_ba, rg_wx, rg_bx, rg_lam, fox_bf, merge_b, w_rg, w_sb, w_fox, w_o, ffn2_norm, ffn2_w1, ffn2_w3, ffn2_w2, ada_w, ada_b, final_norm, final_ada_w, final_ada_b, loss_target, m_ffn1_norm, m_ffn1_w1, m_ffn1_w3, m_ffn1_w2, m_mix_norm, m_w_in, m_conv_w, m_conv_b, m_rg_wa, m_rg_ba, m_rg_wx, m_rg_bx, m_rg_lam, m_fox_bf, m_merge_b, m_w_rg, m_w_sb, m_w_fox, m_w_o, m_ffn2_norm, m_ffn2_w1, m_ffn2_w3, m_ffn2_w2, m_ada_w, m_ada_b, m_final_norm, m_final_ada_w, m_final_ada_b, v_ffn1_norm, v_ffn1_w1, v_ffn1_w3, v_ffn1_w2, v_mix_norm, v_w_in, v_conv_w, v_conv_b, v_rg_wa, v_rg_ba, v_rg_wx, v_rg_bx, v_rg_lam, v_fox_bf, v_merge_b, v_w_rg, v_w_sb, v_w_fox, v_w_o, v_ffn2_norm, v_ffn2_w1, v_ffn2_w3, v_ffn2_w2, v_ada_w, v_ada_b, v_final_norm, v_final_ada_w, v_final_ada_b):
    given = dict(zip(["x", "c"] + WEIGHTS + ["loss_target"] + ["m_" + n for n in WEIGHTS] + ["v_" + n for n in WEIGHTS],
                     (x, c, ffn1_norm, ffn1_w1, ffn1_w3, ffn1_w2, mix_norm, w_in, conv_w, conv_b, rg_wa, rg_ba, rg_wx, rg_bx, rg_lam, fox_bf, merge_b, w_rg, w_sb, w_fox, w_o, ffn2_norm, ffn2_w1, ffn2_w3, ffn2_w2, ada_w, ada_b, final_norm, final_ada_w, final_ada_b, loss_target, m_ffn1_norm, m_ffn1_w1, m_ffn1_w3, m_ffn1_w2, m_mix_norm, m_w_in, m_conv_w, m_conv_b, m_rg_wa, m_rg_ba, m_rg_wx, m_rg_bx, m_rg_lam, m_fox_bf, m_merge_b, m_w_rg, m_w_sb, m_w_fox, m_w_o, m_ffn2_norm, m_ffn2_w1, m_ffn2_w3, m_ffn2_w2, m_ada_w, m_ada_b, m_final_norm, m_final_ada_w, m_final_ada_b, v_ffn1_norm, v_ffn1_w1, v_ffn1_w3, v_ffn1_w2, v_mix_norm, v_w_in, v_conv_w, v_conv_b, v_rg_wa, v_rg_ba, v_rg_wx, v_rg_bx, v_rg_lam, v_fox_bf, v_merge_b, v_w_rg, v_w_sb, v_w_fox, v_w_o, v_ffn2_norm, v_ffn2_w1, v_ffn2_w3, v_ffn2_w2, v_ada_w, v_ada_b, v_final_norm, v_final_ada_w, v_final_ada_b)))
    idx = my_index()
    n_batch = N_DEV * B_LOC
    ada_cols = ada_w.shape[2]
    fin_cols = final_ada_w.shape[1]

    small_in, small_in_meta = _pack([c, conv_w], LANES, 8, F32)
    c_parts, conv_w_parts = _unpack(all_gather("gather_c_conv", small_in), small_in_meta, lead=1)
    c_all = c_parts.reshape(n_batch, D)
    conv_w_all = _unshard(conv_w_parts, 2)
    c_act = ew_fwd("c_silu", _f_silu, [c_all], [], [], [D], [F32])[0]
    mod_cols = [matmul(f"ada_proj_{l}", c_act, ada_w[l], "nn") for l in range(DEPTH)]
    mod_cols.append(matmul("ada_proj_final", c_act, final_ada_w, "nn"))
    mod_g = all_gather("gather_mod", jnp.concatenate(mod_cols, axis=1))
    mods = []
    for l in range(DEPTH):
        full = mod_g[:, :, l * ada_cols:(l + 1) * ada_cols].transpose(1, 0, 2).reshape(n_batch, N_DEV * ada_cols)
        full = ew_fwd(f"ada_bias_{l}", _f_add_bias, [full], [], [ada_b[l][None]], [full.shape[1]], [F32])[0]
        mods.append(lax.dynamic_slice_in_dim(full, idx * B_LOC, B_LOC, axis=0).reshape(B_LOC, 3, 3, D))
    fm = mod_g[:, :, DEPTH * ada_cols:].transpose(1, 0, 2).reshape(n_batch, N_DEV * fin_cols)
    fm = ew_fwd("ada_bias_final", _f_add_bias, [fm], [], [final_ada_b[None]], [fm.shape[1]], [F32])[0]
    fm = lax.dynamic_slice_in_dim(fm, idx * B_LOC, B_LOC, axis=0).reshape(B_LOC, 2, D)

    names = list(GATHERED)
    gathered = all_gather("gather_weights", [given[n].astype(BF16) for n in names])
    full_w = {n: _unshard(g, GATHERED[n]) for n, g in zip(names, gathered)}

    def layer_params(l):
        wi = full_w["w_in"][l]
        cut = IN_CUTS
        return dict(
            gain=mix_norm[l][None], w_rgx=wi[:, cut[0]:cut[1]], w_gate=wi[:, cut[1]:cut[2]], w_sbqkv=wi[:, cut[2]:cut[3]],
            w_foxqkv=wi[:, cut[3]:cut[4]], w_f=_pad_lanes(wi[:, cut[4]:cut[5]]), w_merge=wi[:, cut[5]:cut[6]],
            conv_w8=jnp.pad(conv_w_all[l], ((0, 8 - CONV_K), (0, 0))), conv_b=conv_b[l][None],
            wa_bd=_block_diag(rg_wa[l]), wx_bd=_block_diag(rg_wx[l]), ba=rg_ba[l][None], bx=rg_bx[l][None], lam=rg_lam[l][None],
            bf=_pad_lanes(fox_bf[l][None]), merge_b=merge_b[l][None], w_rg=full_w["w_rg"][l], w_sb=full_w["w_sb"][l],
            w_fox=full_w["w_fox"][l], w_o=full_w["w_o"][l])

    n_tok = x.shape[0] * x.shape[1]
    h = x.reshape(n_tok, D)
    saved = []
    for l in range(DEPTH):
        m = mods[l]
        p = layer_params(l)
        h, s1 = _ffn_fwd(f"a{l}", h, _bp(m, 0, 0), _bp(m, 0, 1), _bp(m, 0, 2), ffn1_norm[l][None], full_w["ffn1_w1"][l],
                         full_w["ffn1_w3"][l], full_w["ffn1_w2"][l])
        h, s2 = _mixer_fwd(f"{l}", h, _bp(m, 1, 0), _bp(m, 1, 1), _bp(m, 1, 2), p)
        h, s3 = _ffn_fwd(f"b{l}", h, _bp(m, 2, 0), _bp(m, 2, 1), _bp(m, 2, 2), ffn2_norm[l][None], full_w["ffn2_w1"][l],
                         full_w["ffn2_w3"][l], full_w["ffn2_w2"][l])
        saved.append((s1, s2, s3, p))
    loss_row, dh, dfshift, dfscale, dgain_final = _final_loss(h, loss_target.reshape(n_tok, D), fm[:, 0][:, None, :],
                                                              fm[:, 1][:, None, :], final_norm[None])

    grads = {n: [None] * DEPTH for n in WEIGHTS}
    d_mods = [None] * DEPTH
    for l in reversed(range(DEPTH)):
        m = mods[l]
        s1, s2, s3, p = saved[l]
        dh, dm3, grads["ffn2_norm"][l], grads["ffn2_w1"][l], grads["ffn2_w3"][l], grads["ffn2_w2"][l] = _ffn_bwd(
            f"b{l}", dh, s3, _bp(m, 2, 0), _bp(m, 2, 1), _bp(m, 2, 2), ffn2_norm[l][None], full_w["ffn2_w1"][l],
            full_w["ffn2_w3"][l], full_w["ffn2_w2"][l])
        dh, dm2, gm = _mixer_bwd(f"{l}", dh, s2, _bp(m, 1, 0), _bp(m, 1, 1), _bp(m, 1, 2), p)
        for n, gval in gm.items():
            grads[n][l] = gval
        dh, dm1, grads["ffn1_norm"][l], grads["ffn1_w1"][l], grads["ffn1_w3"][l], grads["ffn1_w2"][l] = _ffn_bwd(
            f"a{l}", dh, s1, _bp(m, 0, 0), _bp(m, 0, 1), _bp(m, 0, 2), ffn1_norm[l][None], full_w["ffn1_w1"][l],
            full_w["ffn1_w3"][l], full_w["ffn1_w2"][l])
        d_mods[l] = jnp.concatenate([t.reshape(B_LOC, D) for dm in (dm1, dm2, dm3) for t in dm], axis=1)
    grad_x = dh.reshape(x.shape)
    d_fm = jnp.concatenate([dfshift.reshape(B_LOC, D), dfscale.reshape(B_LOC, D)], axis=1)

    rep = {n: jnp.stack([t.reshape(given[n].shape[1:]) for t in grads[n]]) for n in REPLICATED if n != "final_norm"}
    rep["final_norm"] = dgain_final.reshape(D)
    rep["conv_w"] = jnp.stack(grads["conv_w"])
    rep_names = list(rep)
    rep_slab, rep_meta = _pack([rep[n] for n in rep_names], LANES, 8, F32)
    mod_slab, mod_meta = _pack(d_mods + [d_fm], LANES, 8, F32)
    small_g = all_gather("gather_small_grads", jnp.concatenate([mod_slab, rep_slab], axis=0))
    d_mod_all = [t.reshape(n_batch, -1) for t in _unpack(small_g[:, :mod_slab.shape[0]], mod_meta, lead=1)]
    rep_sum = add_blocks("sum_small_grads", [small_g[k, mod_slab.shape[0]:] for k in range(N_DEV)], F32)
    rep_grad = dict(zip(rep_names, _unpack(rep_sum, rep_meta)))
    final_g = {n: rep_grad[n] for n in REPLICATED}
    final_g["conv_w"] = lax.dynamic_slice_in_dim(rep_grad["conv_w"], idx * conv_w.shape[2], conv_w.shape[2], axis=2)
    final_g["ada_b"] = jnp.stack([sum_rows(f"ada_b_grad_{l}", d_mod_all[l])[0] for l in range(DEPTH)])
    final_g["final_ada_b"] = sum_rows("final_ada_b_grad", d_mod_all[DEPTH])[0]
    final_g["ada_w"] = jnp.stack([
        matmul(f"ada_w_grad_{l}", c_act, lax.dynamic_slice_in_dim(d_mod_all[l], idx * ada_cols, ada_cols, axis=1), "tn")
        for l in range(DEPTH)])
    final_g["final_ada_w"] = matmul(
        "final_ada_w_grad", c_act, lax.dynamic_slice_in_dim(d_mod_all[DEPTH], idx * fin_cols, fin_cols, axis=1), "tn")

    g8s = [_reshard(jnp.stack(grads[n]), GATHERED[n]).astype(BF16) for n in names]
    for n, gval in zip(names, reduce_scatter(names, g8s)):
        final_g[n] = gval

    delta, new_m, new_v = {}, {}, {}
    sharded = names + ["ada_w", "final_ada_w", "conv_w"]
    for n in sharded:
        delta[n], new_m[n], new_v[n] = adamw(f"adamw_{n}", given[n], final_g[n], given["m_" + n], given["v_" + n])
    rep_all = [n for n in WEIGHTS if n not in sharded]
    packed = [_pack([src[n] for n in rep_all], LANES, 8, F32)[0]
              for src in (given, final_g, {n: given["m_" + n] for n in rep_all}, {n: given["v_" + n] for n in rep_all})]
    rep_meta_all = _pack([given[n] for n in rep_all], LANES, 8, F32)[1]
    for store, slab_out in zip((delta, new_m, new_v), adamw("adamw_replicated", *packed)):
        store.update(zip(rep_all, _unpack(slab_out, rep_meta_all)))

    loss = lax.psum(loss_row[0, 0], ("x", "y", "c"))
    return (loss, grad_x, *[final_g[n] for n in WEIGHTS], *[delta[n] for n in WEIGHTS], *[new_m[n] for n in WEIGHTS],
            *[new_v[n] for n in WEIGHTS])
```

```python
import functools

import numpy as np
import jax
import jax.numpy as jnp
from jax import lax
from jax.experimental import pallas as pl
from jax.experimental.pallas import tpu as pltpu

F32 = jnp.float32
BF16 = jnp.bfloat16
MESH = pl.DeviceIdType.MESH

N_DEV = 8
D = 1024
SEQ = 2048
B_LOC = 2
N_TOK = B_LOC * SEQ
DEPTH = 2
D_FF = 2816
RG_BLOCKS = 16
RG_C = 8.0
N_HEADS = 8
HEAD_DIM = 64
ATT_W = N_HEADS * HEAD_DIM
LANES = 128
EPS = 1e-6
ATT_SCALE = HEAD_DIM ** -0.5
CONV_K = 4

ADAM_LR = 0.001
ADAM_B1 = 0.9
ADAM_B2 = 0.999
ADAM_EPS = 1e-08
ADAM_WD = 0.01
ADAM_STEP = 10

EW_ROWS = 256
ATT_BLK = 256


def _pick_tile(dim, target):
    best = None
    for t in range(LANES, min(dim, target) + 1, LANES):
        if dim % t == 0:
            best = t
    return best if best is not None else dim


_DIMS = {"nn": (((1,), (0,)), ((), ())), "nt": (((1,), (1,)), ((), ())), "tn": (((0,), (0,)), ((), ()))}


def matmul(name, a_list, b_list, mode, out_dtype=F32, tm=1024, tn=512):
    if not isinstance(a_list, (list, tuple)):
        a_list, b_list = [a_list], [b_list]
    n = len(a_list)
    m_dim = a_list[0].shape[1] if mode == "tn" else a_list[0].shape[0]
    n_dim = b_list[0].shape[0] if mode == "nt" else b_list[0].shape[1]
    tm, tn = _pick_tile(m_dim, tm), _pick_tile(n_dim, tn)
    dims = _DIMS[mode]

    def body(*refs):
        o_ref = refs[-1]
        acc = None
        for a_ref, b_ref in zip(refs[:n], refs[n:2 * n]):
            d = lax.dot_general(a_ref[...].astype(BF16), b_ref[...].astype(BF16), dims, preferred_element_type=F32)
            acc = d if acc is None else acc + d
        o_ref[...] = acc.astype(o_ref.dtype)

    in_specs = []
    for a in a_list:
        if mode == "tn":
            in_specs.append(pl.BlockSpec((a.shape[0], tm), lambda i, j: (0, i)))
        else:
            in_specs.append(pl.BlockSpec((tm, a.shape[1]), lambda i, j: (i, 0)))
    for b in b_list:
        if mode == "nt":
            in_specs.append(pl.BlockSpec((tn, b.shape[1]), lambda i, j: (j, 0)))
        else:
            in_specs.append(pl.BlockSpec((b.shape[0], tn), lambda i, j: (0, j)))
    return pl.pallas_call(
        body, name=name, grid=(m_dim // tm, n_dim // tn), in_specs=in_specs,
        out_specs=pl.BlockSpec((tm, tn), lambda i, j: (i, j)),
        out_shape=jax.ShapeDtypeStruct((m_dim, n_dim), out_dtype),
        compiler_params=pltpu.CompilerParams(dimension_semantics=("parallel", "parallel")),
    )(*a_list, *b_list)


def _row_spec(w, tm):
    return pl.BlockSpec((tm, w), lambda i: (i, 0))


def _bparam_spec(w, tiles_per_batch):
    return pl.BlockSpec((None, 1, w), lambda i: (i // tiles_per_batch, 0, 0))


def _gparam_spec(shape):
    return pl.BlockSpec(shape, lambda i: (0, 0))


def ew_fwd(name, fn, rows, bparams, gparams, out_widths, out_dtypes, tm=EW_ROWS):
    n_rows = rows[0].shape[0]
    tm = min(tm, n_rows)
    tpb = max(SEQ // tm, 1)
    nr, nb, ng = len(rows), len(bparams), len(gparams)

    def body(*refs):
        vals = [r[...] for r in refs[:nr + nb + ng]]
        outs = fn(*vals)
        if not isinstance(outs, (tuple, list)):
            outs = (outs,)
        for o_ref, o in zip(refs[nr + nb + ng:], outs):
            o_ref[...] = o.astype(o_ref.dtype)

    in_specs = ([_row_spec(r.shape[1], tm) for r in rows] + [_bparam_spec(p.shape[2], tpb) for p in bparams]
                + [_gparam_spec(g.shape) for g in gparams])
    outs = pl.pallas_call(
        body, name=name, grid=(n_rows // tm,), in_specs=in_specs,
        out_specs=[_row_spec(w, tm) for w in out_widths],
        out_shape=[jax.ShapeDtypeStruct((n_rows, w), dt) for w, dt in zip(out_widths, out_dtypes)],
        compiler_params=pltpu.CompilerParams(dimension_semantics=("parallel",)),
    )(*rows, *bparams, *gparams)
    return outs


def ew_bwd(name, fn, rows, bparams, gparams, cts, row_grad_dtypes, adds=(), tm=EW_ROWS):
    n_rows = rows[0].shape[0]
    tm = min(tm, n_rows)
    tpb = max(SEQ // tm, 1)
    nr, nb, ng, nc = len(rows), len(bparams), len(gparams), len(cts)
    adds = list(adds) + [None] * (nr - len(adds))
    add_idx = [k for k in range(nr) if adds[k] is not None]
    want = [k for k in range(nr) if row_grad_dtypes[k] is not None]

    def body(*refs):
        pos = nr + nb + ng
        vals = [r[...] for r in refs[:pos]]
        ct_vals = [r[...].astype(F32) for r in refs[pos:pos + nc]]
        pos += nc
        add_vals = {k: refs[pos + q][...] for q, k in enumerate(add_idx)}
        pos += len(add_idx)
        out_refs = refs[pos:]
        f32_vals = [v.astype(F32) for v in vals]
        outs, vjp = jax.vjp(lambda *a: fn(*a), *f32_vals)
        single = not isinstance(outs, (tuple, list))
        grads = vjp(ct_vals[0].astype(outs.dtype) if single else tuple(c.astype(o.dtype) for c, o in zip(ct_vals, outs)))
        i = pl.program_id(0)
        q = 0
        for k in want:
            g = grads[k]
            if k in add_vals:
                g = g + add_vals[k].astype(F32)
            out_refs[q][...] = g.astype(out_refs[q].dtype)
            q += 1
        for k in range(nb):
            ref = out_refs[q]
            q += 1

            @pl.when(i % tpb == 0)
            def _():
                ref[...] = jnp.zeros_like(ref)

            ref[...] += grads[nr + k]
        for k in range(ng):
            ref = out_refs[q]
            q += 1

            @pl.when(i == 0)
            def _():
                ref[...] = jnp.zeros_like(ref)

            ref[...] += grads[nr + nb + k]

    in_specs = ([_row_spec(r.shape[1], tm) for r in rows] + [_bparam_spec(p.shape[2], tpb) for p in bparams]
                + [_gparam_spec(g.shape) for g in gparams] + [_row_spec(c.shape[1], tm) for c in cts]
                + [_row_spec(adds[k].shape[1], tm) for k in add_idx])
    out_specs = ([_row_spec(rows[k].shape[1], tm) for k in want] + [_bparam_spec(p.shape[2], tpb) for p in bparams]
                 + [_gparam_spec(g.shape) for g in gparams])
    out_shape = ([jax.ShapeDtypeStruct(rows[k].shape, row_grad_dtypes[k]) for k in want]
                 + [jax.ShapeDtypeStruct(p.shape, F32) for p in bparams] + [jax.ShapeDtypeStruct(g.shape, F32) for g in gparams])
    outs = pl.pallas_call(
        body, name=name, grid=(n_rows // tm,), in_specs=in_specs, out_specs=out_specs, out_shape=out_shape,
        compiler_params=pltpu.CompilerParams(dimension_semantics=("arbitrary",)),
    )(*rows, *bparams, *gparams, *cts, *[adds[k] for k in add_idx])
    d_rows = list(outs[:len(want)])
    d_b = list(outs[len(want):len(want) + nb])
    d_g = list(outs[len(want) + nb:])
    return d_rows, d_b, d_g


def f_norm_mod(x, shift, scale, gain):
    x = x.astype(F32)
    y = x * lax.rsqrt(jnp.mean(x * x, axis=-1, keepdims=True) + EPS)
    return (y * gain) * (1.0 + scale) + shift


def f_swiglu(a, b3):
    a = a.astype(F32)
    return (a * jax.nn.sigmoid(a)) * b3.astype(F32)


def f_resid(coef, x, y, gate):
    return x.astype(F32) + (coef * (1.0 + gate)) * y.astype(F32)


def f_rg_gates(pre_r, pre_i, xa, ba, bx, lam):
    r = jax.nn.sigmoid(pre_r + ba)
    i = jax.nn.sigmoid(pre_i + bx)
    softplus_neg_lam = jnp.maximum(-lam, 0.0) + jnp.log(1.0 + jnp.exp(-jnp.abs(lam)))
    log_a = (-RG_C) * r * softplus_neg_lam
    a = jnp.exp(log_a)
    u = jnp.sqrt(1.0 - a * a) * (i * xa)
    return a, u


def f_gelu_mul(gate, hs):
    g = gate.astype(F32)
    gelu = 0.5 * g * (1.0 + jnp.tanh(0.7978845608028654 * (g + 0.044715 * g * g * g)))
    return gelu * hs.astype(F32)


def f_log_sigmoid_bias(f, bf):
    z = f.astype(F32) + bf
    return jnp.minimum(z, 0.0) - jnp.log(1.0 + jnp.exp(-jnp.abs(z)))


def f_merge(mg, pa, pb, pc, merge_b):
    g = jax.nn.sigmoid(mg.astype(F32) + merge_b)
    return g[:, :D] * pa.astype(F32) + g[:, D:2 * D] * pb.astype(F32) + g[:, 2 * D:] * pc.astype(F32)


CONV_CB = 256
SCAN_CB = 512
CUM_RB = 512


def _shift_down(x, d):
    if d == 0:
        return x
    rows = lax.broadcasted_iota(jnp.int32, x.shape, 0)
    return jnp.where(rows >= d, pltpu.roll(x, d, axis=0), 0.0)


def _shift_up(x, d):
    if d == 0:
        return x
    s = x.shape[0]
    rows = lax.broadcasted_iota(jnp.int32, x.shape, 0)
    return jnp.where(rows < s - d, pltpu.roll(x, s - d, axis=0), 0.0)


def conv_fwd(x, w8, b):
    n, c = x.shape
    nb = n // SEQ

    def body(x_ref, w_ref, b_ref, y_ref):
        xv = x_ref[...]
        acc = jnp.broadcast_to(b_ref[...], xv.shape)
        for k in range(CONV_K):
            acc = acc + w_ref[k:k + 1, :] * _shift_down(xv, CONV_K - 1 - k)
        y_ref[...] = acc

    return pl.pallas_call(
        body, name="conv_fwd", grid=(c // CONV_CB, nb),
        in_specs=[pl.BlockSpec((SEQ, CONV_CB), lambda j, i: (i, j)), pl.BlockSpec((8, CONV_CB), lambda j, i: (0, j)),
                  pl.BlockSpec((1, CONV_CB), lambda j, i: (0, j))],
        out_specs=pl.BlockSpec((SEQ, CONV_CB), lambda j, i: (i, j)),
        out_shape=jax.ShapeDtypeStruct((n, c), F32),
        compiler_params=pltpu.CompilerParams(dimension_semantics=("parallel", "parallel")),
    )(x, w8, b)


def conv_bwd(x, w8, dy1, dy2):
    n, c = x.shape
    nb = n // SEQ

    def body(x_ref, w_ref, dy1_ref, dy2_ref, dx_ref, dwb_ref):
        xv = x_ref[...]
        dy = dy1_ref[...] + dy2_ref[...]
        dx = jnp.zeros_like(xv)
        parts = []
        for k in range(CONV_K):
            d = CONV_K - 1 - k
            dx = dx + w_ref[k:k + 1, :] * _shift_up(dy, d)
            parts.append(jnp.sum(dy * _shift_down(xv, d), axis=0, keepdims=True))
        parts.append(jnp.sum(dy, axis=0, keepdims=True))
        parts.append(jnp.zeros((8 - len(parts), xv.shape[1]), F32))
        dx_ref[...] = dx.astype(BF16)

        @pl.when(pl.program_id(1) == 0)
        def _():
            dwb_ref[...] = jnp.zeros_like(dwb_ref)

        dwb_ref[...] += jnp.concatenate(parts, axis=0)

    return pl.pallas_call(
        body, name="conv_bwd", grid=(c // CONV_CB, nb),
        in_specs=[pl.BlockSpec((SEQ, CONV_CB), lambda j, i: (i, j)), pl.BlockSpec((8, CONV_CB), lambda j, i: (0, j)),
                  pl.BlockSpec((SEQ, CONV_CB), lambda j, i: (i, j)), pl.BlockSpec((SEQ, CONV_CB), lambda j, i: (i, j))],
        out_specs=[pl.BlockSpec((SEQ, CONV_CB), lambda j, i: (i, j)), pl.BlockSpec((8, CONV_CB), lambda j, i: (0, j))],
        out_shape=[jax.ShapeDtypeStruct((n, c), BF16), jax.ShapeDtypeStruct((8, c), F32)],
        compiler_params=pltpu.CompilerParams(dimension_semantics=("parallel", "arbitrary")),
    )(x, w8, dy1, dy2)


def scan_fwd(a, u):
    n, c = a.shape

    def body(a_ref, u_ref, h_ref):
        def step(t, h):
            h = a_ref[pl.ds(t, 1), :] * h + u_ref[pl.ds(t, 1), :]
            h_ref[pl.ds(t, 1), :] = h
            return h

        lax.fori_loop(0, SEQ, step, jnp.zeros((1, SCAN_CB), F32), unroll=8)

    spec = pl.BlockSpec((SEQ, SCAN_CB), lambda i, j: (i, j))
    return pl.pallas_call(
        body, name="scan_fwd", grid=(n // SEQ, c // SCAN_CB), in_specs=[spec, spec], out_specs=spec,
        out_shape=jax.ShapeDtypeStruct((n, c), F32),
        compiler_params=pltpu.CompilerParams(dimension_semantics=("parallel", "parallel")),
    )(a, u)


def scan_bwd(a, h, g):
    n, c = a.shape

    def body(a_ref, h_ref, g_ref, da_ref, du_ref):
        def step(k, carry):
            t = SEQ - 1 - k
            dh = g_ref[pl.ds(t, 1), :] + carry
            du_ref[pl.ds(t, 1), :] = dh
            h_prev = jnp.where(t > 0, h_ref[pl.ds(jnp.maximum(t - 1, 0), 1), :], 0.0)
            da_ref[pl.ds(t, 1), :] = dh * h_prev
            return a_ref[pl.ds(t, 1), :] * dh

        lax.fori_loop(0, SEQ, step, jnp.zeros((1, SCAN_CB), F32), unroll=8)

    spec = pl.BlockSpec((SEQ, SCAN_CB), lambda i, j: (i, j))
    return pl.pallas_call(
        body, name="scan_bwd", grid=(n // SEQ, c // SCAN_CB), in_specs=[spec, spec, spec], out_specs=[spec, spec],
        out_shape=[jax.ShapeDtypeStruct((n, c), F32), jax.ShapeDtypeStruct((n, c), F32)],
        compiler_params=pltpu.CompilerParams(dimension_semantics=("parallel", "parallel")),
    )(a, h, g)


def _split3_dot(m, x):
    hi = x.astype(BF16)
    r1 = x - hi.astype(F32)
    mid = r1.astype(BF16)
    lo = (r1 - mid.astype(F32)).astype(BF16)
    dot = functools.partial(jnp.dot, preferred_element_type=F32)
    return dot(m, hi) + dot(m, mid) + dot(m, lo)


def seq_cumsum(name, xs, signs, reverse):
    n, w = xs[0].shape
    nx = len(xs)
    rb = min(CUM_RB, SEQ)

    def body(*refs):
        x = None
        for r, sg in zip(refs[:nx], signs):
            x = sg * r[...] if x is None else x + sg * r[...]
        q0 = pl.program_id(1) * rb
        row = q0 + lax.broadcasted_iota(jnp.int32, (rb, SEQ), 0)
        col = lax.broadcasted_iota(jnp.int32, (rb, SEQ), 1)
        tri = ((col >= row) if reverse else (col <= row)).astype(BF16)
        refs[nx][...] = _split3_dot(tri, x)

    return pl.pallas_call(
        body, name=name, grid=(n // SEQ, SEQ // rb),
        in_specs=[pl.BlockSpec((SEQ, w), lambda i, j: (i, 0)) for _ in xs],
        out_specs=pl.BlockSpec((rb, w), lambda i, j: (i * (SEQ // rb) + j, 0)),
        out_shape=jax.ShapeDtypeStruct((n, w), F32),
        compiler_params=pltpu.CompilerParams(dimension_semantics=("parallel", "parallel")),
    )(*xs)


N_PAIRS = N_HEADS // 2


def _dot_nt(a, b):
    return lax.dot_general(a, b, _DIMS["nt"], preferred_element_type=F32)


def _dot_tn(a, b):
    return lax.dot_general(a, b, _DIMS["tn"], preferred_element_type=F32)


def _dot_nn(a, b):
    return lax.dot_general(a, b, _DIMS["nn"], preferred_element_type=F32)


def _split2_dot(x, m):
    hi = x.astype(BF16)
    lo = (x - hi.astype(F32)).astype(BF16)
    return _dot_nn(hi, m) + _dot_nn(lo, m)


def _head_mask(j):
    lane = lax.broadcasted_iota(jnp.int32, (1, LANES), 1)
    return (lane // HEAD_DIM) == j


def _lane_pick(x, h):
    lane = lax.broadcasted_iota(jnp.int32, x.shape, 1)
    return jnp.sum(jnp.where(lane == h, x, 0.0), axis=1, keepdims=True)


def _lane_put(col, h):
    lane = lax.broadcasted_iota(jnp.int32, (col.shape[0], LANES), 1)
    return jnp.where(lane == h, col, 0.0)


def _softplus(z):
    return jnp.maximum(z, 0.0) + jnp.log(1.0 + jnp.exp(-jnp.abs(z)))


def _qkv_specs():
    return [pl.BlockSpec((SEQ, LANES), lambda b, p: (b, p)),
            pl.BlockSpec((SEQ, LANES), lambda b, p: (b, N_PAIRS + p)),
            pl.BlockSpec((SEQ, LANES), lambda b, p: (b, 2 * N_PAIRS + p))]


def _pair_spec():
    return pl.BlockSpec((SEQ, LANES), lambda b, p: (b, p))


def _below_diagonal(strictly):
    t = ATT_BLK
    row = lax.broadcasted_iota(jnp.int32, (t, t), 0)
    col = lax.broadcasted_iota(jnp.int32, (t, t), 1)
    return (row > col) if strictly else (row >= col)


def _over_key_blocks(qi, step, init, reverse):
    t = ATT_BLK
    q0 = pl.multiple_of(qi * t, t)

    def off_diagonal(kk, carry):
        ki = (qi - 1 - kk) if reverse else kk
        return step(pl.multiple_of(ki * t, t), carry, False)

    if reverse:
        return lax.fori_loop(0, qi, off_diagonal, step(q0, init, True))
    return step(q0, lax.fori_loop(0, qi, off_diagonal, init), True)


def _masked_q(qb, j):
    return (jnp.where(_head_mask(j), qb, 0.0) * ATT_SCALE).astype(BF16)


def sb_attn_fwd(qkv):
    n = qkv.shape[0]
    t = ATT_BLK

    def body(q_ref, k_ref, v_ref, o_ref, tot_ref):
        pair = pl.program_id(1)
        strict = _below_diagonal(True)
        later = strict.astype(BF16)

        @pl.when(pair == 0)
        def _():
            tot_ref[...] = jnp.zeros_like(tot_ref)

        def q_block(qi, _):
            q0 = pl.multiple_of(qi * t, t)
            qb = q_ref[pl.ds(q0, t), :]
            qms = [_masked_q(qb, j) for j in range(2)]

            def step(k0, carry, diagonal):
                kb = k_ref[pl.ds(k0, t), :].astype(BF16)
                vb = v_ref[pl.ds(k0, t), :].astype(BF16)
                new = []
                for j in range(2):
                    run_l, acc = carry[j]
                    z = _dot_nt(qms[j], kb)
                    sp = _softplus(z)
                    log_keep = jnp.where(strict, -sp, 0.0) if diagonal else -sp
                    att = jnp.exp((z - sp) + _split2_dot(log_keep, later) + run_l)
                    if diagonal:
                        att = jnp.where(strict, att, 0.0)
                    new.append((run_l + jnp.sum(log_keep, axis=1, keepdims=True), acc + _dot_nn(att.astype(BF16), vb)))
                return tuple(new)

            init = ((jnp.zeros((t, 1), F32), jnp.zeros((t, LANES), F32)),) * 2
            (tot0, acc0), (tot1, acc1) = _over_key_blocks(qi, step, init, reverse=True)
            o_ref[pl.ds(q0, t), :] = jnp.where(_head_mask(0), acc0, acc1)
            tot_ref[pl.ds(q0, t), :] += _lane_put(tot0, 2 * pair) + _lane_put(tot1, 2 * pair + 1)
            return 0

        lax.fori_loop(0, SEQ // t, q_block, 0)

    batch_spec = pl.BlockSpec((SEQ, LANES), lambda b, p: (b, 0))
    return pl.pallas_call(
        body, name="sb_attn_fwd", grid=(n // SEQ, N_PAIRS), in_specs=_qkv_specs(), out_specs=[_pair_spec(), batch_spec],
        out_shape=[jax.ShapeDtypeStruct((n, ATT_W), F32), jax.ShapeDtypeStruct((n, LANES), F32)],
        compiler_params=pltpu.CompilerParams(dimension_semantics=("parallel", "arbitrary")),
    )(qkv, qkv, qkv)


def sb_attn_bwd(qkv, tot, do):
    n = qkv.shape[0]
    t = ATT_BLK

    def body(q_ref, k_ref, v_ref, tot_ref, do_ref, dq_ref, dk_ref, dv_ref, dk_acc, dv_acc):
        pair = pl.program_id(1)
        strict = _below_diagonal(True)
        upto = jnp.logical_not(strict).astype(BF16)
        dk_acc[...] = jnp.zeros_like(dk_acc)
        dv_acc[...] = jnp.zeros_like(dv_acc)

        def q_block(qi, _):
            q0 = pl.multiple_of(qi * t, t)
            qb = q_ref[pl.ds(q0, t), :]
            tot_q = tot_ref[pl.ds(q0, t), :]
            dob = do_ref[pl.ds(q0, t), :].astype(F32)
            qms = [_masked_q(qb, j) for j in range(2)]
            doms = [jnp.where(_head_mask(j), dob, 0.0).astype(BF16) for j in range(2)]
            totals = [_lane_pick(tot_q, 2 * pair + j) for j in range(2)]

            def step(k0, carry, diagonal):
                kb = k_ref[pl.ds(k0, t), :].astype(BF16)
                vb = v_ref[pl.ds(k0, t), :].astype(BF16)
                new = []
                dk = jnp.zeros((t, LANES), F32)
                dv = jnp.zeros((t, LANES), F32)
                for j in range(2):
                    run_l, run_g, dq = carry[j]
                    z = _dot_nt(qms[j], kb)
                    sp = _softplus(z)
                    log_keep = jnp.where(strict, -sp, 0.0) if diagonal else -sp
                    log_beta = z - sp
                    att = jnp.exp(log_beta + (totals[j] - (run_l + _split2_dot(log_keep, upto))))
                    if diagonal:
                        att = jnp.where(strict, att, 0.0)
                    g = att * _dot_nt(doms[j], vb)
                    dv = dv + _dot_tn(att.astype(BF16), doms[j])
                    dz = g - jnp.exp(log_beta) * (run_g + _split2_dot(g, upto))
                    if diagonal:
                        dz = jnp.where(strict, dz, 0.0)
                    dz = dz.astype(BF16)
                    dk = dk + _dot_tn(dz, qms[j])
                    new.append((run_l + jnp.sum(log_keep, axis=1, keepdims=True), run_g + jnp.sum(g, axis=1, keepdims=True),
                                dq + _dot_nn(dz, kb)))
                dk_acc[pl.ds(k0, t), :] += dk
                dv_acc[pl.ds(k0, t), :] += dv
                return tuple(new)

            zero = jnp.zeros((t, 1), F32)
            init = ((zero, zero, jnp.zeros((t, LANES), F32)),) * 2
            (_, _, dq0), (_, _, dq1) = _over_key_blocks(qi, step, init, reverse=False)
            dq_ref[pl.ds(q0, t), :] = (jnp.where(_head_mask(0), dq0, dq1) * ATT_SCALE).astype(BF16)
            return 0

        lax.fori_loop(0, SEQ // t, q_block, 0)
        dk_ref[...] = dk_acc[...].astype(BF16)
        dv_ref[...] = dv_acc[...].astype(BF16)

    out = jax.ShapeDtypeStruct((n, ATT_W), BF16)
    batch_spec = pl.BlockSpec((SEQ, LANES), lambda b, p: (b, 0))
    return pl.pallas_call(
        body, name="sb_attn_bwd", grid=(n // SEQ, N_PAIRS), in_specs=_qkv_specs() + [batch_spec, _pair_spec()],
        out_specs=[_pair_spec()] * 3, out_shape=[out, out, out],
        scratch_shapes=[pltpu.VMEM((SEQ, LANES), F32), pltpu.VMEM((SEQ, LANES), F32)],
        compiler_params=pltpu.CompilerParams(dimension_semantics=("parallel", "parallel")),
    )(qkv, qkv, qkv, tot, do)


NEG_BIG = -1e30


def fox_attn_fwd(qkv, cum, cum_t):
    n = qkv.shape[0]
    t = ATT_BLK

    def body(q_ref, k_ref, v_ref, cum_ref, cumt_ref, o_ref, lse_ref):
        pair = pl.program_id(1)
        causal = _below_diagonal(False)

        @pl.when(pair == 0)
        def _():
            lse_ref[...] = jnp.zeros_like(lse_ref)

        def q_block(qi, _):
            q0 = pl.multiple_of(qi * t, t)
            qb = q_ref[pl.ds(q0, t), :]
            cum_q = cum_ref[pl.ds(q0, t), :]
            qms = [_masked_q(qb, j) for j in range(2)]
            cqs = [_lane_pick(cum_q, 2 * pair + j) for j in range(2)]

            def step(k0, carry, diagonal):
                kb = k_ref[pl.ds(k0, t), :].astype(BF16)
                vb = v_ref[pl.ds(k0, t), :].astype(BF16)
                new = []
                for j in range(2):
                    m, l, acc = carry[j]
                    z = _dot_nt(qms[j], kb) + cqs[j] - cumt_ref[pl.ds(2 * pair + j, 1), pl.ds(k0, t)]
                    if diagonal:
                        z = jnp.where(causal, z, NEG_BIG)
                    m_new = jnp.maximum(m, jnp.max(z, axis=1, keepdims=True))
                    p = jnp.exp(z - m_new)
                    alpha = jnp.exp(m - m_new)
                    new.append((m_new, alpha * l + jnp.sum(p, axis=1, keepdims=True), alpha * acc + _dot_nn(p.astype(BF16), vb)))
                return tuple(new)

            init = ((jnp.full((t, 1), NEG_BIG, F32), jnp.zeros((t, 1), F32), jnp.zeros((t, LANES), F32)),) * 2
            (m0, l0, acc0), (m1, l1, acc1) = _over_key_blocks(qi, step, init, reverse=False)
            o_ref[pl.ds(q0, t), :] = jnp.where(_head_mask(0), acc0 / l0, acc1 / l1)
            lse_ref[pl.ds(q0, t), :] += _lane_put(m0 + jnp.log(l0), 2 * pair) + _lane_put(m1 + jnp.log(l1), 2 * pair + 1)
            return 0

        lax.fori_loop(0, SEQ // t, q_block, 0)

    batch_spec = pl.BlockSpec((SEQ, LANES), lambda b, p: (b, 0))
    return pl.pallas_call(
        body, name="fox_attn_fwd", grid=(n // SEQ, N_PAIRS),
        in_specs=_qkv_specs() + [batch_spec, pl.BlockSpec((None, N_HEADS, SEQ), lambda b, p: (b, 0, 0))],
        out_specs=[_pair_spec(), batch_spec],
        out_shape=[jax.ShapeDtypeStruct((n, ATT_W), F32), jax.ShapeDtypeStruct((n, LANES), F32)],
        compiler_params=pltpu.CompilerParams(dimension_semantics=("parallel", "arbitrary")),
    )(qkv, qkv, qkv, cum, cum_t)


def fox_attn_bwd(qkv, cum, cum_t, lse, o, do):
    n = qkv.shape[0]
    t = ATT_BLK

    def body(q_ref, k_ref, v_ref, cum_ref, cumt_ref, lse_ref, o_ref, do_ref, dq_ref, dk_ref, dv_ref, dcq_ref, dck_ref,
             dk_acc, dv_acc):
        pair = pl.program_id(1)
        causal = _below_diagonal(False)
        dk_acc[...] = jnp.zeros_like(dk_acc)
        dv_acc[...] = jnp.zeros_like(dv_acc)

        @pl.when(pair == 0)
        def _():
            dcq_ref[...] = jnp.zeros_like(dcq_ref)
            dck_ref[...] = jnp.zeros_like(dck_ref)

        def q_block(qi, _):
            q0 = pl.multiple_of(qi * t, t)
            qb = q_ref[pl.ds(q0, t), :]
            ob = o_ref[pl.ds(q0, t), :]
            dob = do_ref[pl.ds(q0, t), :].astype(F32)
            cum_q = cum_ref[pl.ds(q0, t), :]
            lse_q = lse_ref[pl.ds(q0, t), :]
            qms = [_masked_q(qb, j) for j in range(2)]
            dom32 = [jnp.where(_head_mask(j), dob, 0.0) for j in range(2)]
            doms = [d.astype(BF16) for d in dom32]
            deltas = [jnp.sum(d * ob, axis=1, keepdims=True) for d in dom32]
            cqs = [_lane_pick(cum_q, 2 * pair + j) for j in range(2)]
            lqs = [_lane_pick(lse_q, 2 * pair + j) for j in range(2)]

            def step(k0, carry, diagonal):
                kb = k_ref[pl.ds(k0, t), :].astype(BF16)
                vb = v_ref[pl.ds(k0, t), :].astype(BF16)
                new = []
                dk = jnp.zeros((t, LANES), F32)
                dv = jnp.zeros((t, LANES), F32)
                for j in range(2):
                    dq, dcq = carry[j]
                    z = _dot_nt(qms[j], kb) + cqs[j] - cumt_ref[pl.ds(2 * pair + j, 1), pl.ds(k0, t)]
                    if diagonal:
                        z = jnp.where(causal, z, NEG_BIG)
                    p = jnp.exp(z - lqs[j])
                    dv = dv + _dot_tn(p.astype(BF16), doms[j])
                    dz = p * (_dot_nt(doms[j], vb) - deltas[j])
                    dzb = dz.astype(BF16)
                    dk = dk + _dot_tn(dzb, qms[j])
                    dck_ref[pl.ds(2 * pair + j, 1), pl.ds(k0, t)] += jnp.sum(dz, axis=0, keepdims=True)
                    new.append((dq + _dot_nn(dzb, kb), dcq + jnp.sum(dz, axis=1, keepdims=True)))
                dk_acc[pl.ds(k0, t), :] += dk
                dv_acc[pl.ds(k0, t), :] += dv
                return tuple(new)

            init = ((jnp.zeros((t, LANES), F32), jnp.zeros((t, 1), F32)),) * 2
            (dq0, dcq0), (dq1, dcq1) = _over_key_blocks(qi, step, init, reverse=False)
            dq_ref[pl.ds(q0, t), :] = (jnp.where(_head_mask(0), dq0, dq1) * ATT_SCALE).astype(BF16)
            dcq_ref[pl.ds(q0, t), :] += _lane_put(dcq0, 2 * pair) + _lane_put(dcq1, 2 * pair + 1)
            return 0

        lax.fori_loop(0, SEQ // t, q_block, 0)
        dk_ref[...] = dk_acc[...].astype(BF16)
        dv_ref[...] = dv_acc[...].astype(BF16)

    batch_spec = pl.BlockSpec((SEQ, LANES), lambda b, p: (b, 0))
    t_spec = pl.BlockSpec((None, N_HEADS, SEQ), lambda b, p: (b, 0, 0))
    out = jax.ShapeDtypeStruct((n, ATT_W), BF16)
    return pl.pallas_call(
        body, name="fox_attn_bwd", grid=(n // SEQ, N_PAIRS),
        in_specs=_qkv_specs() + [batch_spec, t_spec, batch_spec, _pair_spec(), _pair_spec()],
        out_specs=[_pair_spec()] * 3 + [batch_spec, t_spec],
        scratch_shapes=[pltpu.VMEM((SEQ, LANES), F32), pltpu.VMEM((SEQ, LANES), F32)],
        out_shape=[out, out, out, jax.ShapeDtypeStruct((n, LANES), F32), jax.ShapeDtypeStruct((n // SEQ, N_HEADS, SEQ), F32)],
        compiler_params=pltpu.CompilerParams(dimension_semantics=("parallel", "arbitrary")),
    )(qkv, qkv, qkv, cum, cum_t, lse, o, do)


_HBM = pl.BlockSpec(memory_space=pl.ANY)


def _my_place():
    return lax.axis_index("x"), lax.axis_index("y"), lax.axis_index("c")


def my_index():
    mx, my, mc = _my_place()
    return 4 * mx + 2 * my + mc


def all_gather(name, xs):
    single = not isinstance(xs, (list, tuple))
    xs = [xs] if single else list(xs)
    na = len(xs)

    def body(*refs):
        x_refs, out_refs = refs[:na], refs[na:2 * na]
        send_sems, recv_sems, local_sems = refs[2 * na:]
        mx, my, mc = _my_place()
        me, sibling = (mx, my, mc), (mx, my, 1 - mc)
        chips = [(1 - mx, my), (mx, 1 - my), (1 - mx, 1 - my)]

        def slot(a, px, py, pc):
            return out_refs[a].at[4 * px + 2 * py + pc]

        def copy(a, k, block, to, src=None):
            return pltpu.make_async_remote_copy(
                src_ref=slot(a, *block) if src is None else src, dst_ref=slot(a, *block),
                send_sem=send_sems.at[7 * a + k], recv_sem=recv_sems.at[7 * a + k], device_id=to, device_id_type=MESH)

        mine = [pltpu.make_async_copy(x_refs[a], slot(a, *me), local_sems.at[a]) for a in range(na)]
        for cp in mine:
            cp.start()
        first = []
        for j, chip in enumerate(chips):
            first += [copy(a, 1 + j, me, (*chip, mc), src=x_refs[a]) for a in range(na)]
        first += [copy(a, 0, me, sibling, src=x_refs[a]) for a in range(na)]
        for cp in first:
            cp.start()
        passed = []
        for j, chip in enumerate(chips):
            for a in range(na):
                copy(a, 1 + j, (*chip, mc), me).wait_recv()
                passed.append(copy(a, 4 + j, (*chip, mc), sibling))
                passed[-1].start()
        for a in range(na):
            copy(a, 0, sibling, me).wait_recv()
        for j, chip in enumerate(chips):
            for a in range(na):
                copy(a, 4 + j, (*chip, 1 - mc), me).wait_recv()
        for cp in first + passed:
            cp.wait_send()
        for cp in mine:
            cp.wait()

    outs = pl.pallas_call(
        body, name=name, in_specs=[_HBM] * na, out_specs=[_HBM] * na,
        out_shape=[jax.ShapeDtypeStruct((N_DEV,) + x.shape, x.dtype) for x in xs],
        scratch_shapes=[pltpu.SemaphoreType.DMA((7 * na,)), pltpu.SemaphoreType.DMA((7 * na,)), pltpu.SemaphoreType.DMA((na,))],
    )(*xs)
    return outs[0] if single else list(outs)


_SEM = pl.BlockSpec(memory_space=pltpu.SEMAPHORE)
_HBM_ONLY = pl.BlockSpec(memory_space=pltpu.HBM)
_EFFECT = pltpu.SideEffectType.DATAFLOW_SIDE_EFFECTING
N_PEERS = N_DEV - 1


def _peers():
    mx, my, mc = _my_place()
    return [((1 - mx) if (r >> 2) & 1 else mx, (1 - my) if (r >> 1) & 1 else my, (1 - mc) if r & 1 else mc)
            for r in range(1, N_DEV)]


def _exchange_copies(scatter, x_refs, land_refs, send_sems, recv_sems):
    me = my_index()
    copies = []
    for a, (x_ref, land_ref) in enumerate(zip(x_refs, land_refs)):
        for r, (px, py, pc) in enumerate(_peers()):
            src = x_ref.at[4 * px + 2 * py + pc] if scatter else x_ref
            dst = land_ref.at[r] if scatter else land_ref.at[me]
            copies.append(pltpu.make_async_remote_copy(
                src_ref=src, dst_ref=dst, send_sem=send_sems.at[N_PEERS * a + r], recv_sem=recv_sems.at[N_PEERS * a + r],
                device_id=(px, py, pc), device_id_type=MESH))
    return copies


def exchange_start(name, xs, scatter):
    na = len(xs)
    lands = [lax.empty((N_PEERS,) + x.shape[1:] if scatter else (N_DEV,) + x.shape, x.dtype) for x in xs]

    def body(*refs):
        x_refs, land_refs, send_sems, recv_sems = refs[:na], refs[na:2 * na], refs[2 * na], refs[2 * na + 1]
        token = refs[-1]
        for cp in _exchange_copies(scatter, x_refs, land_refs, send_sems, recv_sems):
            cp.start()
        token[...] = jnp.zeros_like(token)

    outs = pl.pallas_call(
        body, name=name,
        out_shape=(pltpu.SemaphoreType.DMA((N_PEERS * na,)), pltpu.SemaphoreType.DMA((N_PEERS * na,)),
                   *[pltpu.HBM(x.shape, x.dtype) for x in xs], *[pltpu.HBM(l.shape, l.dtype) for l in lands],
                   jax.ShapeDtypeStruct((8, LANES), F32)),
        in_specs=[_HBM_ONLY] * (2 * na),
        out_specs=(_SEM, _SEM, *[_HBM_ONLY] * (2 * na), pl.BlockSpec(memory_space=pltpu.VMEM)),
        input_output_aliases={i: 2 + i for i in range(2 * na)},
        compiler_params=pltpu.CompilerParams(has_side_effects=_EFFECT),
    )(*[pltpu.with_memory_space_constraint(x, pltpu.HBM) for x in xs],
      *[pltpu.with_memory_space_constraint(l, pltpu.HBM) for l in lands])
    return (scatter, outs[0], outs[1], outs[2:2 + na], outs[2 + na:2 + 2 * na]), outs[-1]


def exchange_finish(name, handle, after):
    scatter, send_sems, recv_sems, xs, lands = handle
    na = len(xs)

    def body(*refs):
        x_refs, land_refs, send_ref, recv_ref = refs[:na], refs[na:2 * na], refs[2 * na], refs[2 * na + 1]
        for cp in _exchange_copies(scatter, x_refs, land_refs, send_ref, recv_ref):
            cp.wait_send()
            cp.wait_recv()

    outs = pl.pallas_call(
        body, name=name,
        out_shape=tuple(pltpu.HBM(t.shape, t.dtype) for t in list(xs) + list(lands)),
        in_specs=[_HBM_ONLY] * (2 * na) + [_SEM, _SEM, _HBM],
        out_specs=tuple([_HBM_ONLY] * (2 * na)),
        input_output_aliases={i: i for i in range(2 * na)},
        compiler_params=pltpu.CompilerParams(has_side_effects=_EFFECT),
    )(*xs, *lands, send_sems, recv_sems, after)
    return list(outs[:na]), list(outs[na:])


def _pick_rows(n, target):
    best = None
    for t in range(8, min(n, target) + 1, 8):
        if n % t == 0:
            best = t
    return best if best is not None else n


def add_blocks(name, parts, out_dtype, rows=512):
    r, w = parts[0].shape
    tr = _pick_rows(r, rows)

    def body(*refs):
        acc = refs[0][...].astype(F32)
        for ref in refs[1:-1]:
            acc = acc + ref[...].astype(F32)
        refs[-1][...] = acc.astype(refs[-1].dtype)

    spec = pl.BlockSpec((tr, w), lambda i: (i, 0))
    return pl.pallas_call(
        body, name=name, grid=(r // tr,), in_specs=[spec] * len(parts), out_specs=spec,
        out_shape=jax.ShapeDtypeStruct((r, w), out_dtype),
        compiler_params=pltpu.CompilerParams(dimension_semantics=("parallel",)),
    )(*parts)


def sum_rows(name, x):
    def body(x_ref, o_ref):
        o_ref[...] = jnp.sum(x_ref[...], axis=0, keepdims=True)

    return pl.pallas_call(body, name=name, out_shape=jax.ShapeDtypeStruct((1, x.shape[1]), F32))(x)


def gather_start(name, blocks):
    return exchange_start(name, blocks, scatter=False)


def gather_finish(name, handle, after):
    sent, lands = exchange_finish(name, handle, after)
    me = my_index()
    return [lax.dynamic_update_index_in_dim(land, own, me, 0) for land, own in zip(lands, sent)]


def scatter_start(name, g8s):
    return exchange_start(name, g8s, scatter=True)


def scatter_finish(name, handle, after):
    sent, lands = exchange_finish(name, handle, after)
    me = my_index()
    outs = []
    for a, (land, g8) in enumerate(zip(lands, sent)):
        w = g8.shape[-1]
        own = lax.dynamic_index_in_dim(g8, me, axis=0, keepdims=False)
        outs.append(add_blocks(f"{name}_sum{a}", [own.reshape(-1, w)] + [land[k].reshape(-1, w) for k in range(N_PEERS)], F32)
                    .reshape(g8.shape[1:]))
    return outs


def _pack(arrays, width, row_mult, dtype, lead=0):
    parts, metas = [], []
    for a in arrays:
        lead_shape = a.shape[:lead]
        size = int(np.prod(a.shape[lead:]))
        chunk = row_mult * width
        padded = -(-size // chunk) * chunk
        flat = a.astype(dtype).reshape(lead_shape + (size,))
        if padded != size:
            flat = jnp.pad(flat, [(0, 0)] * lead + [(0, padded - size)])
        parts.append(flat.reshape(lead_shape + (padded // width, width)))
        metas.append((a.shape[lead:], size, padded // width))
    return jnp.concatenate(parts, axis=lead), metas


def _unpack(slab, metas, lead=0):
    out, r0 = [], 0
    for shape, size, rows in metas:
        part = lax.slice_in_dim(slab, r0, r0 + rows, axis=lead)
        lead_shape = part.shape[:lead]
        flat = part.reshape(lead_shape + (rows * part.shape[-1],))
        out.append(lax.slice_in_dim(flat, 0, size, axis=lead).reshape(lead_shape + tuple(shape)))
        r0 += rows
    return out


def _f_adamw(w, g, m, v):
    m = ADAM_B1 * m + (1.0 - ADAM_B1) * g
    v = ADAM_B2 * v + (1.0 - ADAM_B2) * (g * g)
    m_hat = m / (1.0 - ADAM_B1 ** ADAM_STEP)
    v_hat = v / (1.0 - ADAM_B2 ** ADAM_STEP)
    delta = (-ADAM_LR) * (m_hat / (jnp.sqrt(v_hat) + ADAM_EPS) + ADAM_WD * w)
    return delta, m, v


def adamw(name, w, g, m, v):
    shape = w.shape
    w2 = shape[-1]
    flat = [a.reshape(-1, w2) for a in (w, g, m, v)]
    tm = _pick_rows(flat[0].shape[0], 256)
    outs = ew_fwd(name, _f_adamw, flat, [], [], [w2] * 3, [F32] * 3, tm=tm)
    return [o.reshape(shape) for o in outs]


WEIGHTS = ["ffn1_norm", "ffn1_w1", "ffn1_w3", "ffn1_w2", "mix_norm", "w_in", "conv_w", "conv_b", "rg_wa", "rg_ba", "rg_wx",
           "rg_bx", "rg_lam", "fox_bf", "merge_b", "w_rg", "w_sb", "w_fox", "w_o", "ffn2_norm", "ffn2_w1", "ffn2_w3",
           "ffn2_w2", "ada_w", "ada_b", "final_norm", "final_ada_w", "final_ada_b"]
GATHERED = {"ffn1_w1": 2, "ffn1_w3": 2, "ffn1_w2": 1, "w_in": 2, "w_rg": 1, "w_sb": 2, "w_fox": 2, "w_o": 1,
            "ffn2_w1": 2, "ffn2_w3": 2, "ffn2_w2": 1}
REPLICATED = ["ffn1_norm", "mix_norm", "conv_b", "rg_wa", "rg_ba", "rg_wx", "rg_bx", "rg_lam", "fox_bf", "merge_b",
              "ffn2_norm", "final_norm"]
GROUPS = (("ffn1", ("ffn1_w1", "ffn1_w3", "ffn1_w2")), ("mix", ("w_in", "w_rg", "w_sb", "w_fox", "w_o")),
          ("ffn2", ("ffn2_w1", "ffn2_w3", "ffn2_w2")))
IN_CUTS = (0, 1024, 2048, 3584, 5120, 5128, 8200)


def _unshard(g, axis):
    g = jnp.moveaxis(g, 0, axis)
    shape = g.shape
    return g.reshape(shape[:axis] + (shape[axis] * shape[axis + 1],) + shape[axis + 2:])


def _reshard(full, axis):
    shape = full.shape
    g = full.reshape(shape[:axis] + (N_DEV, shape[axis] // N_DEV) + shape[axis + 1:])
    return jnp.moveaxis(g, axis, 0)


def _block_diag(w):
    nb, bd, _ = w.shape
    eye = jnp.eye(nb, dtype=bool)[:, None, :, None]
    return jnp.where(eye, w[:, :, None, :], 0.0).reshape(nb * bd, nb * bd)


def _diag_blocks(m, nb=RG_BLOCKS):
    bd = m.shape[0] // nb
    return jnp.stack([m[k * bd:(k + 1) * bd, k * bd:(k + 1) * bd] for k in range(nb)])


def _pad_lanes(a, width=LANES):
    return jnp.pad(a, [(0, 0)] * (a.ndim - 1) + [(0, width - a.shape[-1])])


def _bp(m, k, which):
    return m[:, k, which][:, None, :]


def _f_silu(c):
    return c * jax.nn.sigmoid(c)


def _f_add_bias(a, b):
    return a + b


FFN_TM = 512
FFN_TN = 1408


def ffn_up(name, h, w1, w3):
    n, k = h.shape
    f = w1.shape[1]
    tm, tn = min(FFN_TM, n), _pick_tile(f, FFN_TN)

    def body(h_ref, w1_ref, w3_ref, a_ref, b_ref, s_ref):
        hv = h_ref[...]
        a = jnp.dot(hv, w1_ref[...], preferred_element_type=F32)
        b = jnp.dot(hv, w3_ref[...], preferred_element_type=F32)
        a_ref[...] = a.astype(BF16)
        b_ref[...] = b.astype(BF16)
        s_ref[...] = ((a * jax.nn.sigmoid(a)) * b).astype(BF16)

    wspec = pl.BlockSpec((k, tn), lambda i, j: (0, j))
    ospec = pl.BlockSpec((tm, tn), lambda i, j: (i, j))
    out = jax.ShapeDtypeStruct((n, f), BF16)
    return pl.pallas_call(
        body, name=name, grid=(n // tm, f // tn), in_specs=[pl.BlockSpec((tm, k), lambda i, j: (i, 0)), wspec, wspec],
        out_specs=[ospec] * 3, out_shape=[out] * 3,
        compiler_params=pltpu.CompilerParams(dimension_semantics=("parallel", "parallel")),
    )(h, w1, w3)


def ffn_down_dx(name, dy, w2, a, b):
    n, k = dy.shape
    f = w2.shape[0]
    tm, tn = min(FFN_TM, n), _pick_tile(f, FFN_TN)

    def body(dy_ref, w2_ref, a_ref, b_ref, da_ref, db_ref):
        ds = _dot_nt(dy_ref[...], w2_ref[...])
        av = a_ref[...].astype(F32)
        sig = jax.nn.sigmoid(av)
        da_ref[...] = (ds * b_ref[...].astype(F32) * (sig * (1.0 + av * (1.0 - sig)))).astype(BF16)
        db_ref[...] = (ds * (av * sig)).astype(BF16)

    ospec = pl.BlockSpec((tm, tn), lambda i, j: (i, j))
    out = jax.ShapeDtypeStruct((n, f), BF16)
    return pl.pallas_call(
        body, name=name, grid=(n // tm, f // tn),
        in_specs=[pl.BlockSpec((tm, k), lambda i, j: (i, 0)), pl.BlockSpec((tn, k), lambda i, j: (j, 0)), ospec, ospec],
        out_specs=[ospec] * 2, out_shape=[out] * 2,
        compiler_params=pltpu.CompilerParams(dimension_semantics=("parallel", "parallel")),
    )(dy, w2, a, b)


def _ffn_fwd(tag, x, shift, scale, gate, gain, w1, w3, w2):
    h = ew_fwd(f"ffn_norm_{tag}", f_norm_mod, [x], [shift, scale], [gain], [D], [BF16])[0]
    a, b3, s = ffn_up(f"ffn_up_{tag}", h, w1, w3)
    y = matmul(f"ffn_down_{tag}", s, w2, "nn", tm=1024)
    xo = ew_fwd(f"ffn_res_{tag}", functools.partial(f_resid, 0.5), [x, y], [gate], [], [D], [F32])[0]
    return xo, (x, h, a, b3, s, y)


def _ffn_bwd(tag, dxo, saved, shift, scale, gate, gain, w1, w3, w2):
    x, h, a, b3, s, y = saved
    (dy,), (dgate,), _ = ew_bwd(f"ffn_res_bwd_{tag}", functools.partial(f_resid, 0.5), [x, y], [gate], [], [dxo], [None, BF16])
    da, db3 = ffn_down_dx(f"ffn_down_dx_{tag}", dy, w2, a, b3)
    dw2 = matmul(f"ffn_dw2_{tag}", s, dy, "tn", tm=1408, tn=256)
    dw1 = matmul(f"ffn_dw1_{tag}", h, da, "tn", tm=1024, tn=256)
    dw3 = matmul(f"ffn_dw3_{tag}", h, db3, "tn", tm=1024, tn=256)
    dh = matmul(f"ffn_up_dx_{tag}", [da, db3], [w1, w3], "nt", tm=1024)
    (dx,), (dshift, dscale), (dgain,) = ew_bwd(f"ffn_norm_bwd_{tag}", f_norm_mod, [x], [shift, scale], [gain], [dh], [F32],
                                               adds=[dxo])
    return dx, (dshift, dscale, dgate), dgain, dw1, dw3, dw2


def _mixer_fwd(tag, x, shift, scale, gate, p):
    h = ew_fwd(f"mix_norm_{tag}", f_norm_mod, [x], [shift, scale], [p["gain"]], [D], [BF16])[0]
    rgx = matmul(f"in_rgx_{tag}", h, p["w_rgx"], "nn")
    rgate = matmul(f"in_gate_{tag}", h, p["w_gate"], "nn")
    sbqkv = matmul(f"in_sb_{tag}", h, p["w_sbqkv"], "nn")
    foxqkv = matmul(f"in_fox_{tag}", h, p["w_foxqkv"], "nn")
    ff = matmul(f"in_forget_{tag}", h, p["w_f"], "nn")
    mg = matmul(f"in_merge_{tag}", h, p["w_merge"], "nn")
    xa = conv_fwd(rgx, p["conv_w8"], p["conv_b"])
    pre_r = matmul(f"rg_a_{tag}", xa, p["wa_bd"], "nn")
    pre_i = matmul(f"rg_x_{tag}", xa, p["wx_bd"], "nn")
    a, u = ew_fwd(f"rg_gates_{tag}", f_rg_gates, [pre_r, pre_i, xa], [], [p["ba"], p["bx"], p["lam"]], [D, D], [F32, F32])
    hs = scan_fwd(a, u)
    ya = ew_fwd(f"rg_out_{tag}", f_gelu_mul, [rgate, hs], [], [], [D], [BF16])[0]
    yb, sb_tot = sb_attn_fwd(sbqkv)
    lf = ew_fwd(f"fox_logf_{tag}", f_log_sigmoid_bias, [ff], [], [p["bf"]], [LANES], [F32])[0]
    cum = seq_cumsum(f"fox_cum_{tag}", [lf], [1.0], False)
    cum_t = cum.reshape(-1, SEQ, LANES)[:, :, :N_HEADS].transpose(0, 2, 1)
    yc, lse = fox_attn_fwd(foxqkv, cum, cum_t)
    pa = matmul(f"out_rg_{tag}", ya, p["w_rg"], "nn")
    pb = matmul(f"out_sb_{tag}", yb, p["w_sb"], "nn")
    pc = matmul(f"out_fox_{tag}", yc, p["w_fox"], "nn")
    mixed = ew_fwd(f"merge_{tag}", f_merge, [mg, pa, pb, pc], [], [p["merge_b"]], [D], [BF16])[0]
    y = matmul(f"out_o_{tag}", mixed, p["w_o"], "nn")
    xo = ew_fwd(f"mix_res_{tag}", functools.partial(f_resid, 1.0), [x, y], [gate], [], [D], [F32])[0]
    saved = dict(x=x, h=h, rgx=rgx, rgate=rgate, sbqkv=sbqkv, foxqkv=foxqkv, ff=ff, mg=mg, xa=xa, pre_r=pre_r, pre_i=pre_i,
                 a=a, hs=hs, ya=ya, yb=yb, sb_tot=sb_tot, cum=cum, cum_t=cum_t, yc=yc, lse=lse, pa=pa, pb=pb, pc=pc,
                 mixed=mixed, y=y)
    return xo, saved


def _mixer_bwd(tag, dxo, s, shift, scale, gate, p):
    (dy,), (dgate,), _ = ew_bwd(f"mix_res_bwd_{tag}", functools.partial(f_resid, 1.0), [s["x"], s["y"]], [gate], [], [dxo],
                                [None, BF16])
    dmixed = matmul(f"out_o_dx_{tag}", dy, p["w_o"], "nt")
    g = {"w_o": matmul(f"out_o_dw_{tag}", s["mixed"], dy, "tn", tm=1024, tn=256)}
    (dmg, dpa, dpb, dpc), _, (g["merge_b"],) = ew_bwd(
        f"merge_bwd_{tag}", f_merge, [s["mg"], s["pa"], s["pb"], s["pc"]], [], [p["merge_b"]], [dmixed], [BF16] * 4)
    dya = matmul(f"out_rg_dx_{tag}", dpa, p["w_rg"], "nt")
    g["w_rg"] = matmul(f"out_rg_dw_{tag}", s["ya"], dpa, "tn", tm=1024, tn=256)
    dyb = matmul(f"out_sb_dx_{tag}", dpb, p["w_sb"], "nt", out_dtype=BF16)
    g["w_sb"] = matmul(f"out_sb_dw_{tag}", s["yb"], dpb, "tn", tm=1024, tn=256)
    dyc = matmul(f"out_fox_dx_{tag}", dpc, p["w_fox"], "nt", out_dtype=BF16)
    g["w_fox"] = matmul(f"out_fox_dw_{tag}", s["yc"], dpc, "tn", tm=1024, tn=256)
    dq_c, dk_c, dv_c, dcq, dck = fox_attn_bwd(s["foxqkv"], s["cum"], s["cum_t"], s["lse"], s["yc"], dyc)
    dck_rows = _pad_lanes(dck.transpose(0, 2, 1).reshape(-1, N_HEADS))
    dlf = seq_cumsum(f"fox_cum_bwd_{tag}", [dcq, dck_rows], [1.0, -1.0], True)
    (dff,), _, (dbf,) = ew_bwd(f"fox_logf_bwd_{tag}", f_log_sigmoid_bias, [s["ff"]], [], [p["bf"]], [dlf], [BF16])
    g["fox_bf"] = dbf[0, :N_HEADS]
    dq_b, dk_b, dv_b = sb_attn_bwd(s["sbqkv"], s["sb_tot"], dyb)
    (drgate, dhs), _, _ = ew_bwd(f"rg_out_bwd_{tag}", f_gelu_mul, [s["rgate"], s["hs"]], [], [], [dya], [BF16, F32])
    da, du = scan_bwd(s["a"], s["hs"], dhs)
    (dpre_r, dpre_i, dxa1), _, (g["rg_ba"], g["rg_bx"], g["rg_lam"]) = ew_bwd(
        f"rg_gates_bwd_{tag}", f_rg_gates, [s["pre_r"], s["pre_i"], s["xa"]], [], [p["ba"], p["bx"], p["lam"]], [da, du],
        [BF16, BF16, F32])
    dxa2 = matmul(f"rg_dx_{tag}", [dpre_r, dpre_i], [p["wa_bd"], p["wx_bd"]], "nt")
    g["rg_wa"] = _diag_blocks(matmul(f"rg_a_dw_{tag}", s["xa"], dpre_r, "tn", tm=512, tn=256))
    g["rg_wx"] = _diag_blocks(matmul(f"rg_x_dw_{tag}", s["xa"], dpre_i, "tn", tm=512, tn=256))
    drgx, dwb = conv_bwd(s["rgx"], p["conv_w8"], dxa1, dxa2)
    g["conv_w"] = dwb[:CONV_K]
    g["conv_b"] = dwb[CONV_K]
    cots = [drgx, drgate, dq_b, dk_b, dv_b, dq_c, dk_c, dv_c, dff, dmg]
    w_sb3 = [p["w_sbqkv"][:, k * ATT_W:(k + 1) * ATT_W] for k in range(3)]
    w_fox3 = [p["w_foxqkv"][:, k * ATT_W:(k + 1) * ATT_W] for k in range(3)]
    ws = [p["w_rgx"], p["w_gate"]] + w_sb3 + w_fox3 + [p["w_f"], p["w_merge"]]
    dh = matmul(f"in_dx_{tag}", cots, ws, "nt", tm=512)
    dws = [matmul(f"in_dw{k}_{tag}", s["h"], ct, "tn", tm=1024, tn=256) for k, ct in enumerate(cots)]
    dws[8] = dws[8][:, :N_HEADS]
    g["w_in"] = jnp.concatenate(dws, axis=1)
    (dx,), (dshift, dscale), (g["mix_norm"],) = ew_bwd(f"mix_norm_bwd_{tag}", f_norm_mod, [s["x"]], [shift, scale], [p["gain"]],
                                                       [dh], [F32], adds=[dxo])
    return dx, (dshift, dscale, dgate), g


def _final_loss(x, target, shift, scale, gain):
    n = x.shape[0]
    tm = EW_ROWS
    tpb = SEQ // tm

    def body(x_ref, t_ref, sh_ref, sc_ref, g_ref, loss_ref, dx_ref, dsh_ref, dsc_ref, dg_ref):
        i = pl.program_id(0)
        out, vjp = jax.vjp(f_norm_mod, x_ref[...], sh_ref[...], sc_ref[...], g_ref[...])
        diff = out - t_ref[...]
        dx, dsh, dsc, dg = vjp(diff * (1.0 / D))
        dx_ref[...] = dx
        sq = jnp.sum(jnp.sum(diff * diff, axis=1, keepdims=True), axis=0, keepdims=True)

        @pl.when(i % tpb == 0)
        def _():
            dsh_ref[...] = jnp.zeros_like(dsh_ref)
            dsc_ref[...] = jnp.zeros_like(dsc_ref)

        @pl.when(i == 0)
        def _():
            dg_ref[...] = jnp.zeros_like(dg_ref)
            loss_ref[...] = jnp.zeros_like(loss_ref)

        dsh_ref[...] += dsh
        dsc_ref[...] += dsc
        dg_ref[...] += dg
        loss_ref[...] += jnp.broadcast_to(sq, (1, LANES)) * (0.5 / D)

    row, bp, gp = _row_spec(D, tm), _bparam_spec(D, tpb), _gparam_spec((1, D))
    return pl.pallas_call(
        body, name="final_loss", grid=(n // tm,), in_specs=[row, row, bp, bp, gp],
        out_specs=[_gparam_spec((1, LANES)), row, bp, bp, gp],
        out_shape=[jax.ShapeDtypeStruct((1, LANES), F32), jax.ShapeDtypeStruct((n, D), F32),
                   jax.ShapeDtypeStruct(shift.shape, F32), jax.ShapeDtypeStruct(scale.shape, F32),
                   jax.ShapeDtypeStruct((1, D), F32)],
        compiler_params=pltpu.CompilerParams(dimension_semantics=("arbitrary",)),
    )(x, target, shift, scale, gain)


def kernel(x, c, ffn1_norm, ffn1_w1, ffn1_w3, ffn1_w2, mix_norm, w_in, conv_w, conv_b, rg_wa, rg_ba, rg_wx, rg_bx, rg_lam, fox_bf, merge_b, w_rg, w_sb, w_fox, w_o, ffn2_norm, ffn2_w1, ffn2_w3, ffn2_w2, ada_w, ada_b, final_norm, final_ada_w, final_ada_b, loss_target, m_ffn1_norm, m_ffn1_w1, m_ffn1_w3, m_ffn1_w2, m_mix_norm, m_w_in, m_conv_w, m_conv_b, m_rg_wa, m_rg_ba, m_rg_wx, m_rg_bx, m_rg_lam, m_fox_bf, m_merge_b, m_w_rg, m_w_sb, m_w_fox, m_w_o, m_ffn2_norm, m_ffn2_w1, m_ffn2_w3, m_ffn2_w2, m_ada_w, m_ada_b, m_final_norm, m_final_ada_w, m_final_ada_b, v_ffn1_norm, v_ffn1_w1, v_ffn1_w3, v_ffn1_w2, v_mix_norm, v_w_in, v_conv_w, v_conv_b, v_rg_wa, v_rg_ba, v_rg_wx, v_rg_bx, v_rg_lam, v_fox_bf, v_merge_b, v_w_rg, v_w_sb, v_w_fox, v_w_o, v_ffn2_norm, v_ffn2_w1, v_ffn2_w3, v_ffn2_w2, v_ada_w, v_ada_b, v_final_norm, v_final_ada_w, v_final_ada_b):
    given = dict(zip(["x", "c"] + WEIGHTS + ["loss_target"] + ["m_" + n for n in WEIGHTS] + ["v_" + n for n in WEIGHTS],
                     (x, c, ffn1_norm, ffn1_w1, ffn1_w3, ffn1_w2, mix_norm, w_in, conv_w, conv_b, rg_wa, rg_ba, rg_wx, rg_bx, rg_lam, fox_bf, merge_b, w_rg, w_sb, w_fox, w_o, ffn2_norm, ffn2_w1, ffn2_w3, ffn2_w2, ada_w, ada_b, final_norm, final_ada_w, final_ada_b, loss_target, m_ffn1_norm, m_ffn1_w1, m_ffn1_w3, m_ffn1_w2, m_mix_norm, m_w_in, m_conv_w, m_conv_b, m_rg_wa, m_rg_ba, m_rg_wx, m_rg_bx, m_rg_lam, m_fox_bf, m_merge_b, m_w_rg, m_w_sb, m_w_fox, m_w_o, m_ffn2_norm, m_ffn2_w1, m_ffn2_w3, m_ffn2_w2, m_ada_w, m_ada_b, m_final_norm, m_final_ada_w, m_final_ada_b, v_ffn1_norm, v_ffn1_w1, v_ffn1_w3, v_ffn1_w2, v_mix_norm, v_w_in, v_conv_w, v_conv_b, v_rg_wa, v_rg_ba, v_rg_wx, v_rg_bx, v_rg_lam, v_fox_bf, v_merge_b, v_w_rg, v_w_sb, v_w_fox, v_w_o, v_ffn2_norm, v_ffn2_w1, v_ffn2_w3, v_ffn2_w2, v_ada_w, v_ada_b, v_final_norm, v_final_ada_w, v_final_ada_b)))
    idx = my_index()
    n_batch = N_DEV * B_LOC
    ada_cols = ada_w.shape[2]
    fin_cols = final_ada_w.shape[1]

    gather_handles, started = {}, jnp.zeros((), F32)
    for l in range(DEPTH):
        for group, members in GROUPS:
            gather_handles[l, group], token = gather_start(f"gather_start_{group}{l}", [given[n][l].astype(BF16) for n in members])
            started = started + token[0, 0]

    def weights_of(l, group, after):
        members = dict(GROUPS)[group]
        blocks = gather_finish(f"gather_finish_{group}{l}", gather_handles[l, group], after)
        return {n: _unshard(b, GATHERED[n] - 1) for n, b in zip(members, blocks)}

    small_in, small_in_meta = _pack([c, conv_w], LANES, 8, F32)
    small_in = small_in + started
    c_parts, conv_w_parts = _unpack(all_gather("gather_c_conv", small_in), small_in_meta, lead=1)
    c_all = c_parts.reshape(n_batch, D)
    conv_w_all = _unshard(conv_w_parts, 2)
    c_act = ew_fwd("c_silu", _f_silu, [c_all], [], [], [D], [F32])[0]
    mod_cols = [matmul(f"ada_proj_{l}", c_act, ada_w[l], "nn") for l in range(DEPTH)]
    mod_cols.append(matmul("ada_proj_final", c_act, final_ada_w, "nn"))
    mod_g = all_gather("gather_mod", jnp.concatenate(mod_cols, axis=1))
    mods = []
    for l in range(DEPTH):
        full = mod_g[:, :, l * ada_cols:(l + 1) * ada_cols].transpose(1, 0, 2).reshape(n_batch, N_DEV * ada_cols)
        full = ew_fwd(f"ada_bias_{l}", _f_add_bias, [full], [], [ada_b[l][None]], [full.shape[1]], [F32])[0]
        mods.append(lax.dynamic_slice_in_dim(full, idx * B_LOC, B_LOC, axis=0).reshape(B_LOC, 3, 3, D))
    fm = mod_g[:, :, DEPTH * ada_cols:].transpose(1, 0, 2).reshape(n_batch, N_DEV * fin_cols)
    fm = ew_fwd("ada_bias_final", _f_add_bias, [fm], [], [final_ada_b[None]], [fm.shape[1]], [F32])[0]
    fm = lax.dynamic_slice_in_dim(fm, idx * B_LOC, B_LOC, axis=0).reshape(B_LOC, 2, D)

    def mixer_params(l, w):
        wi = w["w_in"]
        cut = IN_CUTS
        return dict(
            gain=mix_norm[l][None], w_rgx=wi[:, cut[0]:cut[1]], w_gate=wi[:, cut[1]:cut[2]], w_sbqkv=wi[:, cut[2]:cut[3]],
            w_foxqkv=wi[:, cut[3]:cut[4]], w_f=_pad_lanes(wi[:, cut[4]:cut[5]]), w_merge=wi[:, cut[5]:cut[6]],
            conv_w8=jnp.pad(conv_w_all[l], ((0, 8 - CONV_K), (0, 0))), conv_b=conv_b[l][None],
            wa_bd=_block_diag(rg_wa[l]), wx_bd=_block_diag(rg_wx[l]), ba=rg_ba[l][None], bx=rg_bx[l][None], lam=rg_lam[l][None],
            bf=_pad_lanes(fox_bf[l][None]), merge_b=merge_b[l][None], w_rg=w["w_rg"], w_sb=w["w_sb"], w_fox=w["w_fox"],
            w_o=w["w_o"])

    n_tok = x.shape[0] * x.shape[1]
    h = x.reshape(n_tok, D)
    saved = []
    for l in range(DEPTH):
        m = mods[l]
        w1 = weights_of(l, "ffn1", m if l == 0 else h)
        h, s1 = _ffn_fwd(f"a{l}", h, _bp(m, 0, 0), _bp(m, 0, 1), _bp(m, 0, 2), ffn1_norm[l][None], w1["ffn1_w1"], w1["ffn1_w3"],
                         w1["ffn1_w2"])
        p = mixer_params(l, weights_of(l, "mix", h))
        h, s2 = _mixer_fwd(f"{l}", h, _bp(m, 1, 0), _bp(m, 1, 1), _bp(m, 1, 2), p)
        w3 = weights_of(l, "ffn2", h)
        h, s3 = _ffn_fwd(f"b{l}", h, _bp(m, 2, 0), _bp(m, 2, 1), _bp(m, 2, 2), ffn2_norm[l][None], w3["ffn2_w1"], w3["ffn2_w3"],
                         w3["ffn2_w2"])
        saved.append((s1, s2, s3, p, w1, w3))
    loss_row, dh, dfshift, dfscale, dgain_final = _final_loss(h, loss_target.reshape(n_tok, D), fm[:, 0][:, None, :],
                                                              fm[:, 1][:, None, :], final_norm[None])

    grads = {n: [None] * DEPTH for n in WEIGHTS}
    d_mods = [None] * DEPTH
    scatter_handles = {}
    after_start = jnp.zeros((), F32)

    def start_scatter(l, group):
        members = dict(GROUPS)[group]
        g8s = [_reshard(grads[n][l], GATHERED[n] - 1).astype(BF16) for n in members]
        scatter_handles[l, group], token = scatter_start(f"scatter_start_{group}{l}", g8s)
        return token[0, 0]

    for l in reversed(range(DEPTH)):
        m = mods[l]
        s1, s2, s3, p, w1, w3 = saved[l]
        dh, dm3, grads["ffn2_norm"][l], grads["ffn2_w1"][l], grads["ffn2_w3"][l], grads["ffn2_w2"][l] = _ffn_bwd(
            f"b{l}", dh, s3, _bp(m, 2, 0), _bp(m, 2, 1), _bp(m, 2, 2) + after_start, ffn2_norm[l][None], w3["ffn2_w1"],
            w3["ffn2_w3"], w3["ffn2_w2"])
        after_start = start_scatter(l, "ffn2")
        dh, dm2, gm = _mixer_bwd(f"{l}", dh, s2, _bp(m, 1, 0), _bp(m, 1, 1), _bp(m, 1, 2) + after_start, p)
        for n, gval in gm.items():
            grads[n][l] = gval
        after_start = start_scatter(l, "mix")
        dh, dm1, grads["ffn1_norm"][l], grads["ffn1_w1"][l], grads["ffn1_w3"][l], grads["ffn1_w2"][l] = _ffn_bwd(
            f"a{l}", dh, s1, _bp(m, 0, 0), _bp(m, 0, 1), _bp(m, 0, 2) + after_start, ffn1_norm[l][None], w1["ffn1_w1"],
            w1["ffn1_w3"], w1["ffn1_w2"])
        after_start = start_scatter(l, "ffn1")
        d_mods[l] = jnp.concatenate([t.reshape(B_LOC, D) for dm in (dm1, dm2, dm3) for t in dm], axis=1)
    grad_x = dh.reshape(x.shape)
    d_fm = jnp.concatenate([dfshift.reshape(B_LOC, D), dfscale.reshape(B_LOC, D)], axis=1) + after_start

    rep = {n: jnp.stack([t.reshape(given[n].shape[1:]) for t in grads[n]]) for n in REPLICATED if n != "final_norm"}
    rep["final_norm"] = dgain_final.reshape(D)
    rep["conv_w"] = jnp.stack(grads["conv_w"])
    rep_names = list(rep)
    rep_slab, rep_meta = _pack([rep[n] for n in rep_names], LANES, 8, F32)
    mod_slab, mod_meta = _pack(d_mods + [d_fm], LANES, 8, F32)
    small_g = all_gather("gather_small_grads", jnp.concatenate([mod_slab, rep_slab], axis=0))
    d_mod_all = [t.reshape(n_batch, -1) for t in _unpack(small_g[:, :mod_slab.shape[0]], mod_meta, lead=1)]
    rep_sum = add_blocks("sum_small_grads", [small_g[k, mod_slab.shape[0]:] for k in range(N_DEV)], F32)
    rep_grad = dict(zip(rep_names, _unpack(rep_sum, rep_meta)))
    final_g = {n: rep_grad[n] for n in REPLICATED}
    final_g["conv_w"] = lax.dynamic_slice_in_dim(rep_grad["conv_w"], idx * conv_w.shape[2], conv_w.shape[2], axis=2)
    final_g["ada_b"] = jnp.stack([sum_rows(f"ada_b_grad_{l}", d_mod_all[l])[0] for l in range(DEPTH)])
    final_g["final_ada_b"] = sum_rows("final_ada_b_grad", d_mod_all[DEPTH])[0]
    final_g["ada_w"] = jnp.stack([
        matmul(f"ada_w_grad_{l}", c_act, lax.dynamic_slice_in_dim(d_mod_all[l], idx * ada_cols, ada_cols, axis=1), "tn")
        for l in range(DEPTH)])
    final_g["final_ada_w"] = matmul(
        "final_ada_w_grad", c_act, lax.dynamic_slice_in_dim(d_mod_all[DEPTH], idx * fin_cols, fin_cols, axis=1), "tn")

    shard_g = {n: [None] * DEPTH for n in GATHERED}
    for l in reversed(range(DEPTH)):
        for group in ("ffn2", "mix", "ffn1"):
            sums = scatter_finish(f"scatter_finish_{group}{l}", scatter_handles[l, group], rep_sum)
            for n, gval in zip(dict(GROUPS)[group], sums):
                shard_g[n][l] = gval
    for n in GATHERED:
        final_g[n] = jnp.stack(shard_g[n])

    delta, new_m, new_v = {}, {}, {}
    sharded = list(GATHERED) + ["ada_w", "final_ada_w", "conv_w"]
    for n in sharded:
        delta[n], new_m[n], new_v[n] = adamw(f"adamw_{n}", given[n], final_g[n], given["m_" + n], given["v_" + n])
    rep_all = [n for n in WEIGHTS if n not in sharded]
    packed = [_pack([src[n] for n in rep_all], LANES, 8, F32)[0]
              for src in (given, final_g, {n: given["m_" + n] for n in rep_all}, {n: given["v_" + n] for n in rep_all})]
    rep_meta_all = _pack([given[n] for n in rep_all], LANES, 8, F32)[1]
    for store, slab_out in zip((delta, new_m, new_v), adamw("adamw_replicated", *packed)):
        store.update(zip(rep_all, _unpack(slab_out, rep_meta_all)))

    loss = lax.psum(loss_row[0, 0], ("x", "y", "c"))
    return (loss, grad_x, *[final_g[n] for n in WEIGHTS], *[delta[n] for n in WEIGHTS], *[new_m[n] for n in WEIGHTS],
            *[new_v[n] for n in WEIGHTS])
```

```python
import functools

import numpy as np
import jax
import jax.numpy as jnp
from jax import lax
from jax.experimental import pallas as pl
from jax.experimental.pallas import tpu as pltpu

F32 = jnp.float32
BF16 = jnp.bfloat16
MESH = pl.DeviceIdType.MESH

N_DEV = 8
D = 1024
SEQ = 2048
B_LOC = 2
N_TOK = B_LOC * SEQ
DEPTH = 2
D_FF = 2816
RG_BLOCKS = 16
RG_C = 8.0
N_HEADS = 8
HEAD_DIM = 64
ATT_W = N_HEADS * HEAD_DIM
LANES = 128
EPS = 1e-6
ATT_SCALE = HEAD_DIM ** -0.5
CONV_K = 4

ADAM_LR = 0.001
ADAM_B1 = 0.9
ADAM_B2 = 0.999
ADAM_EPS = 1e-08
ADAM_WD = 0.01
ADAM_STEP = 10

EW_ROWS = 256
ATT_BLK = 256


def _pick_tile(dim, target):
    best = None
    for t in range(LANES, min(dim, target) + 1, LANES):
        if dim % t == 0:
            best = t
    return best if best is not None else dim


_DIMS = {"nn": (((1,), (0,)), ((), ())), "nt": (((1,), (1,)), ((), ())), "tn": (((0,), (0,)), ((), ()))}


def matmul(name, a_list, b_list, mode, out_dtype=F32, tm=1024, tn=512):
    if not isinstance(a_list, (list, tuple)):
        a_list, b_list = [a_list], [b_list]
    n = len(a_list)
    m_dim = a_list[0].shape[1] if mode == "tn" else a_list[0].shape[0]
    n_dim = b_list[0].shape[0] if mode == "nt" else b_list[0].shape[1]
    tm, tn = _pick_tile(m_dim, tm), _pick_tile(n_dim, tn)
    dims = _DIMS[mode]

    def body(*refs):
        o_ref = refs[-1]
        acc = None
        for a_ref, b_ref in zip(refs[:n], refs[n:2 * n]):
            d = lax.dot_general(a_ref[...].astype(BF16), b_ref[...].astype(BF16), dims, preferred_element_type=F32)
            acc = d if acc is None else acc + d
        o_ref[...] = acc.astype(o_ref.dtype)

    in_specs = []
    for a in a_list:
        if mode == "tn":
            in_specs.append(pl.BlockSpec((a.shape[0], tm), lambda i, j: (0, i)))
        else:
            in_specs.append(pl.BlockSpec((tm, a.shape[1]), lambda i, j: (i, 0)))
    for b in b_list:
        if mode == "nt":
            in_specs.append(pl.BlockSpec((tn, b.shape[1]), lambda i, j: (j, 0)))
        else:
            in_specs.append(pl.BlockSpec((b.shape[0], tn), lambda i, j: (0, j)))
    return pl.pallas_call(
        body, name=name, grid=(m_dim // tm, n_dim // tn), in_specs=in_specs,
        out_specs=pl.BlockSpec((tm, tn), lambda i, j: (i, j)),
        out_shape=jax.ShapeDtypeStruct((m_dim, n_dim), out_dtype),
        compiler_params=pltpu.CompilerParams(dimension_semantics=("parallel", "parallel")),
    )(*a_list, *b_list)


def _row_spec(w, tm):
    return pl.BlockSpec((tm, w), lambda i: (i, 0))


def _bparam_spec(w, tiles_per_batch):
    return pl.BlockSpec((None, 1, w), lambda i: (i // tiles_per_batch, 0, 0))


def _gparam_spec(shape):
    return pl.BlockSpec(shape, lambda i: (0, 0))


def ew_fwd(name, fn, rows, bparams, gparams, out_widths, out_dtypes, tm=EW_ROWS):
    n_rows = rows[0].shape[0]
    tm = min(tm, n_rows)
    tpb = max(SEQ // tm, 1)
    nr, nb, ng = len(rows), len(bparams), len(gparams)

    def body(*refs):
        vals = [r[...] for r in refs[:nr + nb + ng]]
        outs = fn(*vals)
        if not isinstance(outs, (tuple, list)):
            outs = (outs,)
        for o_ref, o in zip(refs[nr + nb + ng:], outs):
            o_ref[...] = o.astype(o_ref.dtype)

    in_specs = ([_row_spec(r.shape[1], tm) for r in rows] + [_bparam_spec(p.shape[2], tpb) for p in bparams]
                + [_gparam_spec(g.shape) for g in gparams])
    outs = pl.pallas_call(
        body, name=name, grid=(n_rows // tm,), in_specs=in_specs,
        out_specs=[_row_spec(w, tm) for w in out_widths],
        out_shape=[jax.ShapeDtypeStruct((n_rows, w), dt) for w, dt in zip(out_widths, out_dtypes)],
        compiler_params=pltpu.CompilerParams(dimension_semantics=("parallel",)),
    )(*rows, *bparams, *gparams)
    return outs


def ew_bwd(name, fn, rows, bparams, gparams, cts, row_grad_dtypes, adds=(), tm=EW_ROWS):
    n_rows = rows[0].shape[0]
    tm = min(tm, n_rows)
    tpb = max(SEQ // tm, 1)
    nr, nb, ng, nc = len(rows), len(bparams), len(gparams), len(cts)
    adds = list(adds) + [None] * (nr - len(adds))
    add_idx = [k for k in range(nr) if adds[k] is not None]
    want = [k for k in range(nr) if row_grad_dtypes[k] is not None]

    def body(*refs):
        pos = nr + nb + ng
        vals = [r[...] for r in refs[:pos]]
        ct_vals = [r[...].astype(F32) for r in refs[pos:pos + nc]]
        pos += nc
        add_vals = {k: refs[pos + q][...] for q, k in enumerate(add_idx)}
        pos += len(add_idx)
        out_refs = refs[pos:]
        f32_vals = [v.astype(F32) for v in vals]
        outs, vjp = jax.vjp(lambda *a: fn(*a), *f32_vals)
        single = not isinstance(outs, (tuple, list))
        grads = vjp(ct_vals[0].astype(outs.dtype) if single else tuple(c.astype(o.dtype) for c, o in zip(ct_vals, outs)))
        i = pl.program_id(0)
        q = 0
        for k in want:
            g = grads[k]
            if k in add_vals:
                g = g + add_vals[k].astype(F32)
            out_refs[q][...] = g.astype(out_refs[q].dtype)
            q += 1
        for k in range(nb):
            ref = out_refs[q]
            q += 1

            @pl.when(i % tpb == 0)
            def _():
                ref[...] = jnp.zeros_like(ref)

            ref[...] += grads[nr + k]
        for k in range(ng):
            ref = out_refs[q]
            q += 1

            @pl.when(i == 0)
            def _():
                ref[...] = jnp.zeros_like(ref)

            ref[...] += grads[nr + nb + k]

    in_specs = ([_row_spec(r.shape[1], tm) for r in rows] + [_bparam_spec(p.shape[2], tpb) for p in bparams]
                + [_gparam_spec(g.shape) for g in gparams] + [_row_spec(c.shape[1], tm) for c in cts]
                + [_row_spec(adds[k].shape[1], tm) for k in add_idx])
    out_specs = ([_row_spec(rows[k].shape[1], tm) for k in want] + [_bparam_spec(p.shape[2], tpb) for p in bparams]
                 + [_gparam_spec(g.shape) for g in gparams])
    out_shape = ([jax.ShapeDtypeStruct(rows[k].shape, row_grad_dtypes[k]) for k in want]
                 + [jax.ShapeDtypeStruct(p.shape, F32) for p in bparams] + [jax.ShapeDtypeStruct(g.shape, F32) for g in gparams])
    outs = pl.pallas_call(
        body, name=name, grid=(n_rows // tm,), in_specs=in_specs, out_specs=out_specs, out_shape=out_shape,
        compiler_params=pltpu.CompilerParams(dimension_semantics=("arbitrary",)),
    )(*rows, *bparams, *gparams, *cts, *[adds[k] for k in add_idx])
    d_rows = list(outs[:len(want)])
    d_b = list(outs[len(want):len(want) + nb])
    d_g = list(outs[len(want) + nb:])
    return d_rows, d_b, d_g


def f_norm_mod(x, shift, scale, gain):
    x = x.astype(F32)
    y = x * lax.rsqrt(jnp.mean(x * x, axis=-1, keepdims=True) + EPS)
    return (y * gain) * (1.0 + scale) + shift


def f_swiglu(a, b3):
    a = a.astype(F32)
    return (a * jax.nn.sigmoid(a)) * b3.astype(F32)


def f_resid(coef, x, y, gate):
    return x.astype(F32) + (coef * (1.0 + gate)) * y.astype(F32)


def f_rg_gates(pre_r, pre_i, xa, ba, bx, lam):
    r = jax.nn.sigmoid(pre_r + ba)
    i = jax.nn.sigmoid(pre_i + bx)
    softplus_neg_lam = jnp.maximum(-lam, 0.0) + jnp.log(1.0 + jnp.exp(-jnp.abs(lam)))
    log_a = (-RG_C) * r * softplus_neg_lam
    a = jnp.exp(log_a)
    u = jnp.sqrt(1.0 - a * a) * (i * xa)
    return a, u


def f_gelu_mul(gate, hs):
    g = gate.astype(F32)
    gelu = 0.5 * g * (1.0 + jnp.tanh(0.7978845608028654 * (g + 0.044715 * g * g * g)))
    return gelu * hs.astype(F32)


def f_log_sigmoid_bias(f, bf):
    z = f.astype(F32) + bf
    return jnp.minimum(z, 0.0) - jnp.log(1.0 + jnp.exp(-jnp.abs(z)))


def f_merge(mg, pa, pb, pc, merge_b):
    g = jax.nn.sigmoid(mg.astype(F32) + merge_b)
    return g[:, :D] * pa.astype(F32) + g[:, D:2 * D] * pb.astype(F32) + g[:, 2 * D:] * pc.astype(F32)


CONV_CB = 256
SCAN_CB = 512
CUM_RB = 512


def _shift_down(x, d):
    if d == 0:
        return x
    rows = lax.broadcasted_iota(jnp.int32, x.shape, 0)
    return jnp.where(rows >= d, pltpu.roll(x, d, axis=0), 0.0)


def _shift_up(x, d):
    if d == 0:
        return x
    s = x.shape[0]
    rows = lax.broadcasted_iota(jnp.int32, x.shape, 0)
    return jnp.where(rows < s - d, pltpu.roll(x, s - d, axis=0), 0.0)


def conv_fwd(x, w8, b):
    n, c = x.shape
    nb = n // SEQ

    def body(x_ref, w_ref, b_ref, y_ref):
        xv = x_ref[...]
        acc = jnp.broadcast_to(b_ref[...], xv.shape)
        for k in range(CONV_K):
            acc = acc + w_ref[k:k + 1, :] * _shift_down(xv, CONV_K - 1 - k)
        y_ref[...] = acc

    return pl.pallas_call(
        body, name="conv_fwd", grid=(c // CONV_CB, nb),
        in_specs=[pl.BlockSpec((SEQ, CONV_CB), lambda j, i: (i, j)), pl.BlockSpec((8, CONV_CB), lambda j, i: (0, j)),
                  pl.BlockSpec((1, CONV_CB), lambda j, i: (0, j))],
        out_specs=pl.BlockSpec((SEQ, CONV_CB), lambda j, i: (i, j)),
        out_shape=jax.ShapeDtypeStruct((n, c), F32),
        compiler_params=pltpu.CompilerParams(dimension_semantics=("parallel", "parallel")),
    )(x, w8, b)


def conv_bwd(x, w8, dy1, dy2):
    n, c = x.shape
    nb = n // SEQ

    def body(x_ref, w_ref, dy1_ref, dy2_ref, dx_ref, dwb_ref):
        xv = x_ref[...]
        dy = dy1_ref[...] + dy2_ref[...]
        dx = jnp.zeros_like(xv)
        parts = []
        for k in range(CONV_K):
            d = CONV_K - 1 - k
            dx = dx + w_ref[k:k + 1, :] * _shift_up(dy, d)
            parts.append(jnp.sum(dy * _shift_down(xv, d), axis=0, keepdims=True))
        parts.append(jnp.sum(dy, axis=0, keepdims=True))
        parts.append(jnp.zeros((8 - len(parts), xv.shape[1]), F32))
        dx_ref[...] = dx.astype(BF16)

        @pl.when(pl.program_id(1) == 0)
        def _():
            dwb_ref[...] = jnp.zeros_like(dwb_ref)

        dwb_ref[...] += jnp.concatenate(parts, axis=0)

    return pl.pallas_call(
        body, name="conv_bwd", grid=(c // CONV_CB, nb),
        in_specs=[pl.BlockSpec((SEQ, CONV_CB), lambda j, i: (i, j)), pl.BlockSpec((8, CONV_CB), lambda j, i: (0, j)),
                  pl.BlockSpec((SEQ, CONV_CB), lambda j, i: (i, j)), pl.BlockSpec((SEQ, CONV_CB), lambda j, i: (i, j))],
        out_specs=[pl.BlockSpec((SEQ, CONV_CB), lambda j, i: (i, j)), pl.BlockSpec((8, CONV_CB), lambda j, i: (0, j))],
        out_shape=[jax.ShapeDtypeStruct((n, c), BF16), jax.ShapeDtypeStruct((8, c), F32)],
        compiler_params=pltpu.CompilerParams(dimension_semantics=("parallel", "arbitrary")),
    )(x, w8, dy1, dy2)


def scan_fwd(a, u):
    n, c = a.shape

    def body(a_ref, u_ref, h_ref):
        def step(t, h):
            h = a_ref[pl.ds(t, 1), :] * h + u_ref[pl.ds(t, 1), :]
            h_ref[pl.ds(t, 1), :] = h
            return h

        lax.fori_loop(0, SEQ, step, jnp.zeros((1, SCAN_CB), F32), unroll=8)

    spec = pl.BlockSpec((SEQ, SCAN_CB), lambda i, j: (i, j))
    return pl.pallas_call(
        body, name="scan_fwd", grid=(n // SEQ, c // SCAN_CB), in_specs=[spec, spec], out_specs=spec,
        out_shape=jax.ShapeDtypeStruct((n, c), F32),
        compiler_params=pltpu.CompilerParams(dimension_semantics=("parallel", "parallel")),
    )(a, u)


def scan_bwd(a, h, g):
    n, c = a.shape

    def body(a_ref, h_ref, g_ref, da_ref, du_ref):
        def step(k, carry):
            t = SEQ - 1 - k
            dh = g_ref[pl.ds(t, 1), :] + carry
            du_ref[pl.ds(t, 1), :] = dh
            h_prev = jnp.where(t > 0, h_ref[pl.ds(jnp.maximum(t - 1, 0), 1), :], 0.0)
            da_ref[pl.ds(t, 1), :] = dh * h_prev
            return a_ref[pl.ds(t, 1), :] * dh

        lax.fori_loop(0, SEQ, step, jnp.zeros((1, SCAN_CB), F32), unroll=8)

    spec = pl.BlockSpec((SEQ, SCAN_CB), lambda i, j: (i, j))
    return pl.pallas_call(
        body, name="scan_bwd", grid=(n // SEQ, c // SCAN_CB), in_specs=[spec, spec, spec], out_specs=[spec, spec],
        out_shape=[jax.ShapeDtypeStruct((n, c), F32), jax.ShapeDtypeStruct((n, c), F32)],
        compiler_params=pltpu.CompilerParams(dimension_semantics=("parallel", "parallel")),
    )(a, h, g)


def _split3_dot(m, x):
    hi = x.astype(BF16)
    r1 = x - hi.astype(F32)
    mid = r1.astype(BF16)
    lo = (r1 - mid.astype(F32)).astype(BF16)
    dot = functools.partial(jnp.dot, preferred_element_type=F32)
    return dot(m, hi) + dot(m, mid) + dot(m, lo)


def seq_cumsum(name, xs, signs, reverse):
    n, w = xs[0].shape
    nx = len(xs)
    rb = min(CUM_RB, SEQ)

    def body(*refs):
        x = None
        for r, sg in zip(refs[:nx], signs):
            x = sg * r[...] if x is None else x + sg * r[...]
        q0 = pl.program_id(1) * rb
        row = q0 + lax.broadcasted_iota(jnp.int32, (rb, SEQ), 0)
        col = lax.broadcasted_iota(jnp.int32, (rb, SEQ), 1)
        tri = ((col >= row) if reverse else (col <= row)).astype(BF16)
        refs[nx][...] = _split3_dot(tri, x)

    return pl.pallas_call(
        body, name=name, grid=(n // SEQ, SEQ // rb),
        in_specs=[pl.BlockSpec((SEQ, w), lambda i, j: (i, 0)) for _ in xs],
        out_specs=pl.BlockSpec((rb, w), lambda i, j: (i * (SEQ // rb) + j, 0)),
        out_shape=jax.ShapeDtypeStruct((n, w), F32),
        compiler_params=pltpu.CompilerParams(dimension_semantics=("parallel", "parallel")),
    )(*xs)


N_PAIRS = N_HEADS // 2


def _dot_nt(a, b):
    return lax.dot_general(a, b, _DIMS["nt"], preferred_element_type=F32)


def _dot_tn(a, b):
    return lax.dot_general(a, b, _DIMS["tn"], preferred_element_type=F32)


def _dot_nn(a, b):
    return lax.dot_general(a, b, _DIMS["nn"], preferred_element_type=F32)


def _split2_dot(x, m):
    hi = x.astype(BF16)
    lo = (x - hi.astype(F32)).astype(BF16)
    return _dot_nn(hi, m) + _dot_nn(lo, m)


def _head_mask(j):
    lane = lax.broadcasted_iota(jnp.int32, (1, LANES), 1)
    return (lane // HEAD_DIM) == j


def _lane_pick(x, h):
    lane = lax.broadcasted_iota(jnp.int32, x.shape, 1)
    return jnp.sum(jnp.where(lane == h, x, 0.0), axis=1, keepdims=True)


def _lane_put(col, h):
    lane = lax.broadcasted_iota(jnp.int32, (col.shape[0], LANES), 1)
    return jnp.where(lane == h, col, 0.0)


def _softplus(z):
    return jnp.maximum(z, 0.0) + jnp.log(1.0 + jnp.exp(-jnp.abs(z)))


def _qkv_specs():
    return [pl.BlockSpec((SEQ, LANES), lambda b, p: (b, p)),
            pl.BlockSpec((SEQ, LANES), lambda b, p: (b, N_PAIRS + p)),
            pl.BlockSpec((SEQ, LANES), lambda b, p: (b, 2 * N_PAIRS + p))]


def _pair_spec():
    return pl.BlockSpec((SEQ, LANES), lambda b, p: (b, p))


def _below_diagonal(strictly):
    t = ATT_BLK
    row = lax.broadcasted_iota(jnp.int32, (t, t), 0)
    col = lax.broadcasted_iota(jnp.int32, (t, t), 1)
    return (row > col) if strictly else (row >= col)


def _over_key_blocks(qi, step, init, reverse):
    t = ATT_BLK
    q0 = pl.multiple_of(qi * t, t)

    def off_diagonal(kk, carry):
        ki = (qi - 1 - kk) if reverse else kk
        return step(pl.multiple_of(ki * t, t), carry, False)

    if reverse:
        return lax.fori_loop(0, qi, off_diagonal, step(q0, init, True))
    return step(q0, lax.fori_loop(0, qi, off_diagonal, init), True)


def _masked_q(qb, j):
    return (jnp.where(_head_mask(j), qb, 0.0) * ATT_SCALE).astype(BF16)


def sb_attn_fwd(qkv):
    n = qkv.shape[0]
    t = ATT_BLK

    def body(q_ref, k_ref, v_ref, o_ref, tot_ref):
        pair = pl.program_id(1)
        strict = _below_diagonal(True)
        later = strict.astype(BF16)

        @pl.when(pair == 0)
        def _():
            tot_ref[...] = jnp.zeros_like(tot_ref)

        def q_block(qi, _):
            q0 = pl.multiple_of(qi * t, t)
            qb = q_ref[pl.ds(q0, t), :]
            qms = [_masked_q(qb, j) for j in range(2)]

            def step(k0, carry, diagonal):
                kb = k_ref[pl.ds(k0, t), :].astype(BF16)
                vb = v_ref[pl.ds(k0, t), :].astype(BF16)
                new = []
                for j in range(2):
                    run_l, acc = carry[j]
                    z = _dot_nt(qms[j], kb)
                    sp = _softplus(z)
                    log_keep = jnp.where(strict, -sp, 0.0) if diagonal else -sp
                    att = jnp.exp((z - sp) + _split2_dot(log_keep, later) + run_l)
                    if diagonal:
                        att = jnp.where(strict, att, 0.0)
                    new.append((run_l + jnp.sum(log_keep, axis=1, keepdims=True), acc + _dot_nn(att.astype(BF16), vb)))
                return tuple(new)

            init = ((jnp.zeros((t, 1), F32), jnp.zeros((t, LANES), F32)),) * 2
            (tot0, acc0), (tot1, acc1) = _over_key_blocks(qi, step, init, reverse=True)
            o_ref[pl.ds(q0, t), :] = jnp.where(_head_mask(0), acc0, acc1)
            tot_ref[pl.ds(q0, t), :] += _lane_put(tot0, 2 * pair) + _lane_put(tot1, 2 * pair + 1)
            return 0

        lax.fori_loop(0, SEQ // t, q_block, 0)

    batch_spec = pl.BlockSpec((SEQ, LANES), lambda b, p: (b, 0))
    return pl.pallas_call(
        body, name="sb_attn_fwd", grid=(n // SEQ, N_PAIRS), in_specs=_qkv_specs(), out_specs=[_pair_spec(), batch_spec],
        out_shape=[jax.ShapeDtypeStruct((n, ATT_W), F32), jax.ShapeDtypeStruct((n, LANES), F32)],
        compiler_params=pltpu.CompilerParams(dimension_semantics=("parallel", "arbitrary")),
    )(qkv, qkv, qkv)


def sb_attn_bwd(qkv, tot, do):
    n = qkv.shape[0]
    t = ATT_BLK

    def body(q_ref, k_ref, v_ref, tot_ref, do_ref, dq_ref, dk_ref, dv_ref, dk_acc, dv_acc):
        pair = pl.program_id(1)
        strict = _below_diagonal(True)
        upto = jnp.logical_not(strict).astype(BF16)
        dk_acc[...] = jnp.zeros_like(dk_acc)
        dv_acc[...] = jnp.zeros_like(dv_acc)

        def q_block(qi, _):
            q0 = pl.multiple_of(qi * t, t)
            qb = q_ref[pl.ds(q0, t), :]
            tot_q = tot_ref[pl.ds(q0, t), :]
            dob = do_ref[pl.ds(q0, t), :].astype(F32)
            qms = [_masked_q(qb, j) for j in range(2)]
            doms = [jnp.where(_head_mask(j), dob, 0.0).astype(BF16) for j in range(2)]
            totals = [_lane_pick(tot_q, 2 * pair + j) for j in range(2)]

            def step(k0, carry, diagonal):
                kb = k_ref[pl.ds(k0, t), :].astype(BF16)
                vb = v_ref[pl.ds(k0, t), :].astype(BF16)
                new = []
                dk = jnp.zeros((t, LANES), F32)
                dv = jnp.zeros((t, LANES), F32)
                for j in range(2):
                    run_l, run_g, dq = carry[j]
                    z = _dot_nt(qms[j], kb)
                    sp = _softplus(z)
                    log_keep = jnp.where(strict, -sp, 0.0) if diagonal else -sp
                    log_beta = z - sp
                    att = jnp.exp(log_beta + (totals[j] - (run_l + _split2_dot(log_keep, upto))))
                    if diagonal:
                        att = jnp.where(strict, att, 0.0)
                    g = att * _dot_nt(doms[j], vb)
                    dv = dv + _dot_tn(att.astype(BF16), doms[j])
                    dz = g - jnp.exp(log_beta) * (run_g + _split2_dot(g, upto))
                    if diagonal:
                        dz = jnp.where(strict, dz, 0.0)
                    dz = dz.astype(BF16)
                    dk = dk + _dot_tn(dz, qms[j])
                    new.append((run_l + jnp.sum(log_keep, axis=1, keepdims=True), run_g + jnp.sum(g, axis=1, keepdims=True),
                                dq + _dot_nn(dz, kb)))
                dk_acc[pl.ds(k0, t), :] += dk
                dv_acc[pl.ds(k0, t), :] += dv
                return tuple(new)

            zero = jnp.zeros((t, 1), F32)
            init = ((zero, zero, jnp.zeros((t, LANES), F32)),) * 2
            (_, _, dq0), (_, _, dq1) = _over_key_blocks(qi, step, init, reverse=False)
            dq_ref[pl.ds(q0, t), :] = (jnp.where(_head_mask(0), dq0, dq1) * ATT_SCALE).astype(BF16)
            return 0

        lax.fori_loop(0, SEQ // t, q_block, 0)
        dk_ref[...] = dk_acc[...].astype(BF16)
        dv_ref[...] = dv_acc[...].astype(BF16)

    out = jax.ShapeDtypeStruct((n, ATT_W), BF16)
    batch_spec = pl.BlockSpec((SEQ, LANES), lambda b, p: (b, 0))
    return pl.pallas_call(
        body, name="sb_attn_bwd", grid=(n // SEQ, N_PAIRS), in_specs=_qkv_specs() + [batch_spec, _pair_spec()],
        out_specs=[_pair_spec()] * 3, out_shape=[out, out, out],
        scratch_shapes=[pltpu.VMEM((SEQ, LANES), F32), pltpu.VMEM((SEQ, LANES), F32)],
        compiler_params=pltpu.CompilerParams(dimension_semantics=("parallel", "parallel")),
    )(qkv, qkv, qkv, tot, do)


NEG_BIG = -1e30


def fox_attn_fwd(qkv, cum, cum_t):
    n = qkv.shape[0]
    t = ATT_BLK

    def body(q_ref, k_ref, v_ref, cum_ref, cumt_ref, o_ref, lse_ref):
        pair = pl.program_id(1)
        causal = _below_diagonal(False)

        @pl.when(pair == 0)
        def _():
            lse_ref[...] = jnp.zeros_like(lse_ref)

        def q_block(qi, _):
            q0 = pl.multiple_of(qi * t, t)
            qb = q_ref[pl.ds(q0, t), :]
            cum_q = cum_ref[pl.ds(q0, t), :]
            qms = [_masked_q(qb, j) for j in range(2)]
            cqs = [_lane_pick(cum_q, 2 * pair + j) for j in range(2)]

            def step(k0, carry, diagonal):
                kb = k_ref[pl.ds(k0, t), :].astype(BF16)
                vb = v_ref[pl.ds(k0, t), :].astype(BF16)
                new = []
                for j in range(2):
                    m, l, acc = carry[j]
                    z = _dot_nt(qms[j], kb) + cqs[j] - cumt_ref[pl.ds(2 * pair + j, 1), pl.ds(k0, t)]
                    if diagonal:
                        z = jnp.where(causal, z, NEG_BIG)
                    m_new = jnp.maximum(m, jnp.max(z, axis=1, keepdims=True))
                    p = jnp.exp(z - m_new)
                    alpha = jnp.exp(m - m_new)
                    new.append((m_new, alpha * l + jnp.sum(p, axis=1, keepdims=True), alpha * acc + _dot_nn(p.astype(BF16), vb)))
                return tuple(new)

            init = ((jnp.full((t, 1), NEG_BIG, F32), jnp.zeros((t, 1), F32), jnp.zeros((t, LANES), F32)),) * 2
            (m0, l0, acc0), (m1, l1, acc1) = _over_key_blocks(qi, step, init, reverse=False)
            o_ref[pl.ds(q0, t), :] = jnp.where(_head_mask(0), acc0 / l0, acc1 / l1)
            lse_ref[pl.ds(q0, t), :] += _lane_put(m0 + jnp.log(l0), 2 * pair) + _lane_put(m1 + jnp.log(l1), 2 * pair + 1)
            return 0

        lax.fori_loop(0, SEQ // t, q_block, 0)

    batch_spec = pl.BlockSpec((SEQ, LANES), lambda b, p: (b, 0))
    return pl.pallas_call(
        body, name="fox_attn_fwd", grid=(n // SEQ, N_PAIRS),
        in_specs=_qkv_specs() + [batch_spec, pl.BlockSpec((None, N_HEADS, SEQ), lambda b, p: (b, 0, 0))],
        out_specs=[_pair_spec(), batch_spec],
        out_shape=[jax.ShapeDtypeStruct((n, ATT_W), F32), jax.ShapeDtypeStruct((n, LANES), F32)],
        compiler_params=pltpu.CompilerParams(dimension_semantics=("parallel", "arbitrary")),
    )(qkv, qkv, qkv, cum, cum_t)


def fox_attn_bwd(qkv, cum, cum_t, lse, o, do):
    n = qkv.shape[0]
    t = ATT_BLK

    def body(q_ref, k_ref, v_ref, cum_ref, cumt_ref, lse_ref, o_ref, do_ref, dq_ref, dk_ref, dv_ref, dcq_ref, dck_ref,
             dk_acc, dv_acc):
        pair = pl.program_id(1)
        causal = _below_diagonal(False)
        dk_acc[...] = jnp.zeros_like(dk_acc)
        dv_acc[...] = jnp.zeros_like(dv_acc)

        @pl.when(pair == 0)
        def _():
            dcq_ref[...] = jnp.zeros_like(dcq_ref)
            dck_ref[...] = jnp.zeros_like(dck_ref)

        def q_block(qi, _):
            q0 = pl.multiple_of(qi * t, t)
            qb = q_ref[pl.ds(q0, t), :]
            ob = o_ref[pl.ds(q0, t), :]
            dob = do_ref[pl.ds(q0, t), :].astype(F32)
            cum_q = cum_ref[pl.ds(q0, t), :]
            lse_q = lse_ref[pl.ds(q0, t), :]
            qms = [_masked_q(qb, j) for j in range(2)]
            dom32 = [jnp.where(_head_mask(j), dob, 0.0) for j in range(2)]
            doms = [d.astype(BF16) for d in dom32]
            deltas = [jnp.sum(d * ob, axis=1, keepdims=True) for d in dom32]
            cqs = [_lane_pick(cum_q, 2 * pair + j) for j in range(2)]
            lqs = [_lane_pick(lse_q, 2 * pair + j) for j in range(2)]

            def step(k0, carry, diagonal):
                kb = k_ref[pl.ds(k0, t), :].astype(BF16)
                vb = v_ref[pl.ds(k0, t), :].astype(BF16)
                new = []
                dk = jnp.zeros((t, LANES), F32)
                dv = jnp.zeros((t, LANES), F32)
                for j in range(2):
                    dq, dcq = carry[j]
                    z = _dot_nt(qms[j], kb) + cqs[j] - cumt_ref[pl.ds(2 * pair + j, 1), pl.ds(k0, t)]
                    if diagonal:
                        z = jnp.where(causal, z, NEG_BIG)
                    p = jnp.exp(z - lqs[j])
                    dv = dv + _dot_tn(p.astype(BF16), doms[j])
                    dz = p * (_dot_nt(doms[j], vb) - deltas[j])
                    dzb = dz.astype(BF16)
                    dk = dk + _dot_tn(dzb, qms[j])
                    dck_ref[pl.ds(2 * pair + j, 1), pl.ds(k0, t)] += jnp.sum(dz, axis=0, keepdims=True)
                    new.append((dq + _dot_nn(dzb, kb), dcq + jnp.sum(dz, axis=1, keepdims=True)))
                dk_acc[pl.ds(k0, t), :] += dk
                dv_acc[pl.ds(k0, t), :] += dv
                return tuple(new)

            init = ((jnp.zeros((t, LANES), F32), jnp.zeros((t, 1), F32)),) * 2
            (dq0, dcq0), (dq1, dcq1) = _over_key_blocks(qi, step, init, reverse=False)
            dq_ref[pl.ds(q0, t), :] = (jnp.where(_head_mask(0), dq0, dq1) * ATT_SCALE).astype(BF16)
            dcq_ref[pl.ds(q0, t), :] += _lane_put(dcq0, 2 * pair) + _lane_put(dcq1, 2 * pair + 1)
            return 0

        lax.fori_loop(0, SEQ // t, q_block, 0)
        dk_ref[...] = dk_acc[...].astype(BF16)
        dv_ref[...] = dv_acc[...].astype(BF16)

    batch_spec = pl.BlockSpec((SEQ, LANES), lambda b, p: (b, 0))
    t_spec = pl.BlockSpec((None, N_HEADS, SEQ), lambda b, p: (b, 0, 0))
    out = jax.ShapeDtypeStruct((n, ATT_W), BF16)
    return pl.pallas_call(
        body, name="fox_attn_bwd", grid=(n // SEQ, N_PAIRS),
        in_specs=_qkv_specs() + [batch_spec, t_spec, batch_spec, _pair_spec(), _pair_spec()],
        out_specs=[_pair_spec()] * 3 + [batch_spec, t_spec],
        scratch_shapes=[pltpu.VMEM((SEQ, LANES), F32), pltpu.VMEM((SEQ, LANES), F32)],
        out_shape=[out, out, out, jax.ShapeDtypeStruct((n, LANES), F32), jax.ShapeDtypeStruct((n // SEQ, N_HEADS, SEQ), F32)],
        compiler_params=pltpu.CompilerParams(dimension_semantics=("parallel", "arbitrary")),
    )(qkv, qkv, qkv, cum, cum_t, lse, o, do)


_HBM = pl.BlockSpec(memory_space=pl.ANY)


def _my_place():
    return lax.axis_index("x"), lax.axis_index("y"), lax.axis_index("c")


def my_index():
    mx, my, mc = _my_place()
    return 4 * mx + 2 * my + mc


def all_gather(name, xs):
    single = not isinstance(xs, (list, tuple))
    xs = [xs] if single else list(xs)
    na = len(xs)

    def body(*refs):
        x_refs, out_refs = refs[:na], refs[na:2 * na]
        send_sems, recv_sems, local_sems = refs[2 * na:]
        mx, my, mc = _my_place()
        me, sibling = (mx, my, mc), (mx, my, 1 - mc)
        chips = [(1 - mx, my), (mx, 1 - my), (1 - mx, 1 - my)]

        def slot(a, px, py, pc):
            return out_refs[a].at[4 * px + 2 * py + pc]

        def copy(a, k, block, to, src=None):
            return pltpu.make_async_remote_copy(
                src_ref=slot(a, *block) if src is None else src, dst_ref=slot(a, *block),
                send_sem=send_sems.at[7 * a + k], recv_sem=recv_sems.at[7 * a + k], device_id=to, device_id_type=MESH)

        mine = [pltpu.make_async_copy(x_refs[a], slot(a, *me), local_sems.at[a]) for a in range(na)]
        for cp in mine:
            cp.start()
        first = []
        for j, chip in enumerate(chips):
            first += [copy(a, 1 + j, me, (*chip, mc), src=x_refs[a]) for a in range(na)]
        first += [copy(a, 0, me, sibling, src=x_refs[a]) for a in range(na)]
        for cp in first:
            cp.start()
        passed = []
        for j, chip in enumerate(chips):
            for a in range(na):
                copy(a, 1 + j, (*chip, mc), me).wait_recv()
                passed.append(copy(a, 4 + j, (*chip, mc), sibling))
                passed[-1].start()
        for a in range(na):
            copy(a, 0, sibling, me).wait_recv()
        for j, chip in enumerate(chips):
            for a in range(na):
                copy(a, 4 + j, (*chip, 1 - mc), me).wait_recv()
        for cp in first + passed:
            cp.wait_send()
        for cp in mine:
            cp.wait()

    outs = pl.pallas_call(
        body, name=name, in_specs=[_HBM] * na, out_specs=[_HBM] * na,
        out_shape=[jax.ShapeDtypeStruct((N_DEV,) + x.shape, x.dtype) for x in xs],
        scratch_shapes=[pltpu.SemaphoreType.DMA((7 * na,)), pltpu.SemaphoreType.DMA((7 * na,)), pltpu.SemaphoreType.DMA((na,))],
    )(*xs)
    return outs[0] if single else list(outs)


_SEM = pl.BlockSpec(memory_space=pltpu.SEMAPHORE)
_HBM_ONLY = pl.BlockSpec(memory_space=pltpu.HBM)
_EFFECT = pltpu.SideEffectType.DATAFLOW_SIDE_EFFECTING
N_PEERS = N_DEV - 1


def _peers():
    mx, my, mc = _my_place()
    return [((1 - mx) if (r >> 2) & 1 else mx, (1 - my) if (r >> 1) & 1 else my, (1 - mc) if r & 1 else mc)
            for r in range(1, N_DEV)]


def _exchange_copies(scatter, x_refs, land_refs, send_sems, recv_sems):
    me = my_index()
    copies = []
    for a, (x_ref, land_ref) in enumerate(zip(x_refs, land_refs)):
        for r, (px, py, pc) in enumerate(_peers()):
            src = x_ref.at[4 * px + 2 * py + pc] if scatter else x_ref
            dst = land_ref.at[r] if scatter else land_ref.at[me]
            copies.append(pltpu.make_async_remote_copy(
                src_ref=src, dst_ref=dst, send_sem=send_sems.at[N_PEERS * a + r], recv_sem=recv_sems.at[N_PEERS * a + r],
                device_id=(px, py, pc), device_id_type=MESH))
    return copies


def exchange_start(name, xs, scatter):
    na = len(xs)
    lands = [lax.empty((N_PEERS,) + x.shape[1:] if scatter else (N_DEV,) + x.shape, x.dtype) for x in xs]

    def body(*refs):
        x_refs, land_refs, send_sems, recv_sems = refs[:na], refs[na:2 * na], refs[2 * na], refs[2 * na + 1]
        token = refs[-1]
        for cp in _exchange_copies(scatter, x_refs, land_refs, send_sems, recv_sems):
            cp.start()
        token[...] = jnp.zeros_like(token)

    outs = pl.pallas_call(
        body, name=name,
        out_shape=(pltpu.SemaphoreType.DMA((N_PEERS * na,)), pltpu.SemaphoreType.DMA((N_PEERS * na,)),
                   *[pltpu.HBM(x.shape, x.dtype) for x in xs], *[pltpu.HBM(l.shape, l.dtype) for l in lands],
                   jax.ShapeDtypeStruct((8, LANES), F32)),
        in_specs=[_HBM_ONLY] * (2 * na),
        out_specs=(_SEM, _SEM, *[_HBM_ONLY] * (2 * na), pl.BlockSpec(memory_space=pltpu.VMEM)),
        input_output_aliases={i: 2 + i for i in range(2 * na)},
        compiler_params=pltpu.CompilerParams(has_side_effects=_EFFECT),
    )(*[pltpu.with_memory_space_constraint(x, pltpu.HBM) for x in xs],
      *[pltpu.with_memory_space_constraint(l, pltpu.HBM) for l in lands])
    return (scatter, outs[0], outs[1], outs[2:2 + na], outs[2 + na:2 + 2 * na]), outs[-1]


def exchange_finish(name, handle, after):
    scatter, send_sems, recv_sems, xs, lands = handle
    na = len(xs)

    def body(*refs):
        x_refs, land_refs, send_ref, recv_ref = refs[:na], refs[na:2 * na], refs[2 * na], refs[2 * na + 1]
        for cp in _exchange_copies(scatter, x_refs, land_refs, send_ref, recv_ref):
            cp.wait_send()
            cp.wait_recv()

    outs = pl.pallas_call(
        body, name=name,
        out_shape=tuple(pltpu.HBM(t.shape, t.dtype) for t in list(xs) + list(lands)),
        in_specs=[_HBM_ONLY] * (2 * na) + [_SEM, _SEM, _HBM],
        out_specs=tuple([_HBM_ONLY] * (2 * na)),
        input_output_aliases={i: i for i in range(2 * na)},
        compiler_params=pltpu.CompilerParams(has_side_effects=_EFFECT),
    )(*xs, *lands, send_sems, recv_sems, after)
    return list(outs[:na]), list(outs[na:])


def _pick_rows(n, target):
    best = None
    for t in range(8, min(n, target) + 1, 8):
        if n % t == 0:
            best = t
    return best if best is not None else n


def add_blocks(name, parts, out_dtype, rows=512):
    r, w = parts[0].shape
    tr = _pick_rows(r, rows)

    def body(*refs):
        acc = refs[0][...].astype(F32)
        for ref in refs[1:-1]:
            acc = acc + ref[...].astype(F32)
        refs[-1][...] = acc.astype(refs[-1].dtype)

    spec = pl.BlockSpec((tr, w), lambda i: (i, 0))
    return pl.pallas_call(
        body, name=name, grid=(r // tr,), in_specs=[spec] * len(parts), out_specs=spec,
        out_shape=jax.ShapeDtypeStruct((r, w), out_dtype),
        compiler_params=pltpu.CompilerParams(dimension_semantics=("parallel",)),
    )(*parts)


def sum_rows(name, x):
    def body(x_ref, o_ref):
        o_ref[...] = jnp.sum(x_ref[...], axis=0, keepdims=True)

    return pl.pallas_call(body, name=name, out_shape=jax.ShapeDtypeStruct((1, x.shape[1]), F32))(x)


def gather_start(name, blocks):
    return exchange_start(name, blocks, scatter=False)


def gather_finish(name, handle, after):
    sent, lands = exchange_finish(name, handle, after)
    me = my_index()
    return [lax.dynamic_update_index_in_dim(land, own, me, 0) for land, own in zip(lands, sent)]


def scatter_start(name, g8s):
    return exchange_start(name, g8s, scatter=True)


def scatter_finish(name, handle, after):
    sent, lands = exchange_finish(name, handle, after)
    me = my_index()
    outs = []
    for a, (land, g8) in enumerate(zip(lands, sent)):
        w = g8.shape[-1]
        own = lax.dynamic_index_in_dim(g8, me, axis=0, keepdims=False)
        outs.append(add_blocks(f"{name}_sum{a}", [own.reshape(-1, w)] + [land[k].reshape(-1, w) for k in range(N_PEERS)], F32)
                    .reshape(g8.shape[1:]))
    return outs


def _pack(arrays, width, row_mult, dtype, lead=0):
    parts, metas = [], []
    for a in arrays:
        lead_shape = a.shape[:lead]
        size = int(np.prod(a.shape[lead:]))
        chunk = row_mult * width
        padded = -(-size // chunk) * chunk
        flat = a.astype(dtype).reshape(lead_shape + (size,))
        if padded != size:
            flat = jnp.pad(flat, [(0, 0)] * lead + [(0, padded - size)])
        parts.append(flat.reshape(lead_shape + (padded // width, width)))
        metas.append((a.shape[lead:], size, padded // width))
    return jnp.concatenate(parts, axis=lead), metas


def _unpack(slab, metas, lead=0):
    out, r0 = [], 0
    for shape, size, rows in metas:
        part = lax.slice_in_dim(slab, r0, r0 + rows, axis=lead)
        lead_shape = part.shape[:lead]
        flat = part.reshape(lead_shape + (rows * part.shape[-1],))
        out.append(lax.slice_in_dim(flat, 0, size, axis=lead).reshape(lead_shape + tuple(shape)))
        r0 += rows
    return out


def _f_adamw(w, g, m, v):
    m = ADAM_B1 * m + (1.0 - ADAM_B1) * g
    v = ADAM_B2 * v + (1.0 - ADAM_B2) * (g * g)
    m_hat = m / (1.0 - ADAM_B1 ** ADAM_STEP)
    v_hat = v / (1.0 - ADAM_B2 ** ADAM_STEP)
    delta = (-ADAM_LR) * (m_hat / (jnp.sqrt(v_hat) + ADAM_EPS) + ADAM_WD * w)
    return delta, m, v


def adamw(name, w, g, m, v):
    shape = w.shape
    w2 = shape[-1]
    flat = [a.reshape(-1, w2) for a in (w, g, m, v)]
    tm = _pick_rows(flat[0].shape[0], 256)
    outs = ew_fwd(name, _f_adamw, flat, [], [], [w2] * 3, [F32] * 3, tm=tm)
    return [o.reshape(shape) for o in outs]


WEIGHTS = ["ffn1_norm", "ffn1_w1", "ffn1_w3", "ffn1_w2", "mix_norm", "w_in", "conv_w", "conv_b", "rg_wa", "rg_ba", "rg_wx",
           "rg_bx", "rg_lam", "fox_bf", "merge_b", "w_rg", "w_sb", "w_fox", "w_o", "ffn2_norm", "ffn2_w1", "ffn2_w3",
           "ffn2_w2", "ada_w", "ada_b", "final_norm", "final_ada_w", "final_ada_b"]
GATHERED = {"ffn1_w1": 2, "ffn1_w3": 2, "ffn1_w2": 1, "w_in": 2, "w_rg": 1, "w_sb": 2, "w_fox": 2, "w_o": 1,
            "ffn2_w1": 2, "ffn2_w3": 2, "ffn2_w2": 1}
REPLICATED = ["ffn1_norm", "mix_norm", "conv_b", "rg_wa", "rg_ba", "rg_wx", "rg_bx", "rg_lam", "fox_bf", "merge_b",
              "ffn2_norm", "final_norm"]
GROUPS = (("ffn1", ("ffn1_w1", "ffn1_w3", "ffn1_w2")), ("mix", ("w_in", "w_rg", "w_sb", "w_fox", "w_o")),
          ("ffn2", ("ffn2_w1", "ffn2_w3", "ffn2_w2")))
IN_CUTS = (0, 1024, 2048, 3584, 5120, 5128, 8200)


def _unshard(g, axis):
    g = jnp.moveaxis(g, 0, axis)
    shape = g.shape
    return g.reshape(shape[:axis] + (shape[axis] * shape[axis + 1],) + shape[axis + 2:])


def _reshard(full, axis):
    shape = full.shape
    g = full.reshape(shape[:axis] + (N_DEV, shape[axis] // N_DEV) + shape[axis + 1:])
    return jnp.moveaxis(g, axis, 0)


def _block_diag(w):
    nb, bd, _ = w.shape
    eye = jnp.eye(nb, dtype=bool)[:, None, :, None]
    return jnp.where(eye, w[:, :, None, :], 0.0).reshape(nb * bd, nb * bd)


def _diag_blocks(m, nb=RG_BLOCKS):
    bd = m.shape[0] // nb
    return jnp.stack([m[k * bd:(k + 1) * bd, k * bd:(k + 1) * bd] for k in range(nb)])


def _pad_lanes(a, width=LANES):
    return jnp.pad(a, [(0, 0)] * (a.ndim - 1) + [(0, width - a.shape[-1])])


def _bp(m, k, which):
    return m[:, k, which][:, None, :]


def _f_silu(c):
    return c * jax.nn.sigmoid(c)


def _f_add_bias(a, b):
    return a + b


FFN_TM = 512
FFN_TN = 1408


def ffn_up(name, h, w1, w3):
    n, k = h.shape
    f = w1.shape[1]
    tm, tn = min(FFN_TM, n), _pick_tile(f, FFN_TN)

    def body(h_ref, w1_ref, w3_ref, a_ref, b_ref, s_ref):
        hv = h_ref[...]
        a = jnp.dot(hv, w1_ref[...], preferred_element_type=F32)
        b = jnp.dot(hv, w3_ref[...], preferred_element_type=F32)
        a_ref[...] = a.astype(BF16)
        b_ref[...] = b.astype(BF16)
        s_ref[...] = ((a * jax.nn.sigmoid(a)) * b).astype(BF16)

    wspec = pl.BlockSpec((k, tn), lambda i, j: (0, j))
    ospec = pl.BlockSpec((tm, tn), lambda i, j: (i, j))
    out = jax.ShapeDtypeStruct((n, f), BF16)
    return pl.pallas_call(
        body, name=name, grid=(n // tm, f // tn), in_specs=[pl.BlockSpec((tm, k), lambda i, j: (i, 0)), wspec, wspec],
        out_specs=[ospec] * 3, out_shape=[out] * 3,
        compiler_params=pltpu.CompilerParams(dimension_semantics=("parallel", "parallel")),
    )(h, w1, w3)


def ffn_down_dx(name, dy, w2, a, b):
    n, k = dy.shape
    f = w2.shape[0]
    tm, tn = min(FFN_TM, n), _pick_tile(f, FFN_TN)

    def body(dy_ref, w2_ref, a_ref, b_ref, da_ref, db_ref):
        ds = _dot_nt(dy_ref[...], w2_ref[...])
        av = a_ref[...].astype(F32)
        sig = jax.nn.sigmoid(av)
        da_ref[...] = (ds * b_ref[...].astype(F32) * (sig * (1.0 + av * (1.0 - sig)))).astype(BF16)
        db_ref[...] = (ds * (av * sig)).astype(BF16)

    ospec = pl.BlockSpec((tm, tn), lambda i, j: (i, j))
    out = jax.ShapeDtypeStruct((n, f), BF16)
    return pl.pallas_call(
        body, name=name, grid=(n // tm, f // tn),
        in_specs=[pl.BlockSpec((tm, k), lambda i, j: (i, 0)), pl.BlockSpec((tn, k), lambda i, j: (j, 0)), ospec, ospec],
        out_specs=[ospec] * 2, out_shape=[out] * 2,
        compiler_params=pltpu.CompilerParams(dimension_semantics=("parallel", "parallel")),
    )(dy, w2, a, b)


def _ffn_fwd(tag, x, shift, scale, gate, gain, w1, w3, w2):
    h = ew_fwd(f"ffn_norm_{tag}", f_norm_mod, [x], [shift, scale], [gain], [D], [BF16])[0]
    a, b3, s = ffn_up(f"ffn_up_{tag}", h, w1, w3)
    y = matmul(f"ffn_down_{tag}", s, w2, "nn", tm=1024)
    xo = ew_fwd(f"ffn_res_{tag}", functools.partial(f_resid, 0.5), [x, y], [gate], [], [D], [F32])[0]
    return xo, (x, h, a, b3, s, y)


def _ffn_bwd(tag, dxo, saved, shift, scale, gate, gain, w1, w3, w2):
    x, h, a, b3, s, y = saved
    (dy,), (dgate,), _ = ew_bwd(f"ffn_res_bwd_{tag}", functools.partial(f_resid, 0.5), [x, y], [gate], [], [dxo], [None, BF16])
    da, db3 = ffn_down_dx(f"ffn_down_dx_{tag}", dy, w2, a, b3)
    dw2 = matmul(f"ffn_dw2_{tag}", s, dy, "tn", tm=1408, tn=256)
    dw1 = matmul(f"ffn_dw1_{tag}", h, da, "tn", tm=1024, tn=256)
    dw3 = matmul(f"ffn_dw3_{tag}", h, db3, "tn", tm=1024, tn=256)
    dh = matmul(f"ffn_up_dx_{tag}", [da, db3], [w1, w3], "nt", tm=1024)
    (dx,), (dshift, dscale), (dgain,) = ew_bwd(f"ffn_norm_bwd_{tag}", f_norm_mod, [x], [shift, scale], [gain], [dh], [F32],
                                               adds=[dxo])
    return dx, (dshift, dscale, dgate), dgain, dw1, dw3, dw2


def _mixer_fwd(tag, x, shift, scale, gate, p):
    h = ew_fwd(f"mix_norm_{tag}", f_norm_mod, [x], [shift, scale], [p["gain"]], [D], [BF16])[0]
    rgx = matmul(f"in_rgx_{tag}", h, p["w_rgx"], "nn")
    rgate = matmul(f"in_gate_{tag}", h, p["w_gate"], "nn")
    sbqkv = matmul(f"in_sb_{tag}", h, p["w_sbqkv"], "nn")
    foxqkv = matmul(f"in_fox_{tag}", h, p["w_foxqkv"], "nn")
    ff = matmul(f"in_forget_{tag}", h, p["w_f"], "nn")
    mg = matmul(f"in_merge_{tag}", h, p["w_merge"], "nn")
    xa = conv_fwd(rgx, p["conv_w8"], p["conv_b"])
    pre_r = matmul(f"rg_a_{tag}", xa, p["wa_bd"], "nn")
    pre_i = matmul(f"rg_x_{tag}", xa, p["wx_bd"], "nn")
    a, u = ew_fwd(f"rg_gates_{tag}", f_rg_gates, [pre_r, pre_i, xa], [], [p["ba"], p["bx"], p["lam"]], [D, D], [F32, F32])
    hs = scan_fwd(a, u)
    ya = ew_fwd(f"rg_out_{tag}", f_gelu_mul, [rgate, hs], [], [], [D], [BF16])[0]
    yb, sb_tot = sb_attn_fwd(sbqkv)
    lf = ew_fwd(f"fox_logf_{tag}", f_log_sigmoid_bias, [ff], [], [p["bf"]], [LANES], [F32])[0]
    cum = seq_cumsum(f"fox_cum_{tag}", [lf], [1.0], False)
    cum_t = cum.reshape(-1, SEQ, LANES)[:, :, :N_HEADS].transpose(0, 2, 1)
    yc, lse = fox_attn_fwd(foxqkv, cum, cum_t)
    pa = matmul(f"out_rg_{tag}", ya, p["w_rg"], "nn")
    pb = matmul(f"out_sb_{tag}", yb, p["w_sb"], "nn")
    pc = matmul(f"out_fox_{tag}", yc, p["w_fox"], "nn")
    mixed = ew_fwd(f"merge_{tag}", f_merge, [mg, pa, pb, pc], [], [p["merge_b"]], [D], [BF16])[0]
    y = matmul(f"out_o_{tag}", mixed, p["w_o"], "nn")
    xo = ew_fwd(f"mix_res_{tag}", functools.partial(f_resid, 1.0), [x, y], [gate], [], [D], [F32])[0]
    saved = dict(x=x, h=h, rgx=rgx, rgate=rgate, sbqkv=sbqkv, foxqkv=foxqkv, ff=ff, mg=mg, xa=xa, pre_r=pre_r, pre_i=pre_i,
                 a=a, hs=hs, ya=ya, yb=yb, sb_tot=sb_tot, cum=cum, cum_t=cum_t, yc=yc, lse=lse, pa=pa, pb=pb, pc=pc,
                 mixed=mixed, y=y)
    return xo, saved


def _mixer_bwd(tag, dxo, s, shift, scale, gate, p):
    (dy,), (dgate,), _ = ew_bwd(f"mix_res_bwd_{tag}", functools.partial(f_resid, 1.0), [s["x"], s["y"]], [gate], [], [dxo],
                                [None, BF16])
    dmixed = matmul(f"out_o_dx_{tag}", dy, p["w_o"], "nt")
    g = {"w_o": matmul(f"out_o_dw_{tag}", s["mixed"], dy, "tn", tm=1024, tn=256)}
    (dmg, dpa, dpb, dpc), _, (g["merge_b"],) = ew_bwd(
        f"merge_bwd_{tag}", f_merge, [s["mg"], s["pa"], s["pb"], s["pc"]], [], [p["merge_b"]], [dmixed], [BF16] * 4)
    dya = matmul(f"out_rg_dx_{tag}", dpa, p["w_rg"], "nt")
    g["w_rg"] = matmul(f"out_rg_dw_{tag}", s["ya"], dpa, "tn", tm=1024, tn=256)
    dyb = matmul(f"out_sb_dx_{tag}", dpb, p["w_sb"], "nt", out_dtype=BF16)
    g["w_sb"] = matmul(f"out_sb_dw_{tag}", s["yb"], dpb, "tn", tm=1024, tn=256)
    dyc = matmul(f"out_fox_dx_{tag}", dpc, p["w_fox"], "nt", out_dtype=BF16)
    g["w_fox"] = matmul(f"out_fox_dw_{tag}", s["yc"], dpc, "tn", tm=1024, tn=256)
    dq_c, dk_c, dv_c, dcq, dck = fox_attn_bwd(s["foxqkv"], s["cum"], s["cum_t"], s["lse"], s["yc"], dyc)
    dck_rows = _pad_lanes(dck.transpose(0, 2, 1).reshape(-1, N_HEADS))
    dlf = seq_cumsum(f"fox_cum_bwd_{tag}", [dcq, dck_rows], [1.0, -1.0], True)
    (dff,), _, (dbf,) = ew_bwd(f"fox_logf_bwd_{tag}", f_log_sigmoid_bias, [s["ff"]], [], [p["bf"]], [dlf], [BF16])
    g["fox_bf"] = dbf[0, :N_HEADS]
    dq_b, dk_b, dv_b = sb_attn_bwd(s["sbqkv"], s["sb_tot"], dyb)
    (drgate, dhs), _, _ = ew_bwd(f"rg_out_bwd_{tag}", f_gelu_mul, [s["rgate"], s["hs"]], [], [], [dya], [BF16, F32])
    da, du = scan_bwd(s["a"], s["hs"], dhs)
    (dpre_r, dpre_i, dxa1), _, (g["rg_ba"], g["rg_bx"], g["rg_lam"]) = ew_bwd(
        f"rg_gates_bwd_{tag}", f_rg_gates, [s["pre_r"], s["pre_i"], s["xa"]], [], [p["ba"], p["bx"], p["lam"]], [da, du],
        [BF16, BF16, F32])
    dxa2 = matmul(f"rg_dx_{tag}", [dpre_r, dpre_i], [p["wa_bd"], p["wx_bd"]], "nt")
    g["rg_wa"] = _diag_blocks(matmul(f"rg_a_dw_{tag}", s["xa"], dpre_r, "tn", tm=512, tn=256))
    g["rg_wx"] = _diag_blocks(matmul(f"rg_x_dw_{tag}", s["xa"], dpre_i, "tn", tm=512, tn=256))
    drgx, dwb = conv_bwd(s["rgx"], p["conv_w8"], dxa1, dxa2)
    g["conv_w"] = dwb[:CONV_K]
    g["conv_b"] = dwb[CONV_K]
    cots = [drgx, drgate, dq_b, dk_b, dv_b, dq_c, dk_c, dv_c, dff, dmg]
    w_sb3 = [p["w_sbqkv"][:, k * ATT_W:(k + 1) * ATT_W] for k in range(3)]
    w_fox3 = [p["w_foxqkv"][:, k * ATT_W:(k + 1) * ATT_W] for k in range(3)]
    ws = [p["w_rgx"], p["w_gate"]] + w_sb3 + w_fox3 + [p["w_f"], p["w_merge"]]
    dh = matmul(f"in_dx_{tag}", cots, ws, "nt", tm=512)
    dws = [matmul(f"in_dw{k}_{tag}", s["h"], ct, "tn", tm=1024, tn=256) for k, ct in enumerate(cots)]
    dws[8] = dws[8][:, :N_HEADS]
    g["w_in"] = jnp.concatenate(dws, axis=1)
    (dx,), (dshift, dscale), (g["mix_norm"],) = ew_bwd(f"mix_norm_bwd_{tag}", f_norm_mod, [s["x"]], [shift, scale], [p["gain"]],
                                                       [dh], [F32], adds=[dxo])
    return dx, (dshift, dscale, dgate), g


def _final_loss(x, target, shift, scale, gain):
    n = x.shape[0]
    tm = EW_ROWS
    tpb = SEQ // tm

    def body(x_ref, t_ref, sh_ref, sc_ref, g_ref, loss_ref, dx_ref, dsh_ref, dsc_ref, dg_ref):
        i = pl.program_id(0)
        out, vjp = jax.vjp(f_norm_mod, x_ref[...], sh_ref[...], sc_ref[...], g_ref[...])
        diff = out - t_ref[...]
        dx, dsh, dsc, dg = vjp(diff * (1.0 / D))
        dx_ref[...] = dx
        sq = jnp.sum(jnp.sum(diff * diff, axis=1, keepdims=True), axis=0, keepdims=True)

        @pl.when(i % tpb == 0)
        def _():
            dsh_ref[...] = jnp.zeros_like(dsh_ref)
            dsc_ref[...] = jnp.zeros_like(dsc_ref)

        @pl.when(i == 0)
        def _():
            dg_ref[...] = jnp.zeros_like(dg_ref)
            loss_ref[...] = jnp.zeros_like(loss_ref)

        dsh_ref[...] += dsh
        dsc_ref[...] += dsc
        dg_ref[...] += dg
        loss_ref[...] += jnp.broadcast_to(sq, (1, LANES)) * (0.5 / D)

    row, bp, gp = _row_spec(D, tm), _bparam_spec(D, tpb), _gparam_spec((1, D))
    return pl.pallas_call(
        body, name="final_loss", grid=(n // tm,), in_specs=[row, row, bp, bp, gp],
        out_specs=[_gparam_spec((1, LANES)), row, bp, bp, gp],
        out_shape=[jax.ShapeDtypeStruct((1, LANES), F32), jax.ShapeDtypeStruct((n, D), F32),
                   jax.ShapeDtypeStruct(shift.shape, F32), jax.ShapeDtypeStruct(scale.shape, F32),
                   jax.ShapeDtypeStruct((1, D), F32)],
        compiler_params=pltpu.CompilerParams(dimension_semantics=("arbitrary",)),
    )(x, target, shift, scale, gain)


def kernel(x, c, ffn1_norm, ffn1_w1, ffn1_w3, ffn1_w2, mix_norm, w_in, conv_w, conv_b, rg_wa, rg_ba, rg_wx, rg_bx, rg_lam, fox_bf, merge_b, w_rg, w_sb, w_fox, w_o, ffn2_norm, ffn2_w1, ffn2_w3, ffn2_w2, ada_w, ada_b, final_norm, final_ada_w, final_ada_b, loss_target, m_ffn1_norm, m_ffn1_w1, m_ffn1_w3, m_ffn1_w2, m_mix_norm, m_w_in, m_conv_w, m_conv_b, m_rg_wa, m_rg_ba, m_rg_wx, m_rg_bx, m_rg_lam, m_fox_bf, m_merge_b, m_w_rg, m_w_sb, m_w_fox, m_w_o, m_ffn2_norm, m_ffn2_w1, m_ffn2_w3, m_ffn2_w2, m_ada_w, m_ada_b, m_final_norm, m_final_ada_w, m_final_ada_b, v_ffn1_norm, v_ffn1_w1, v_ffn1_w3, v_ffn1_w2, v_mix_norm, v_w_in, v_conv_w, v_conv_b, v_rg_wa, v_rg_ba, v_rg_wx, v_rg_bx, v_rg_lam, v_fox_bf, v_merge_b, v_w_rg, v_w_sb, v_w_fox, v_w_o, v_ffn2_norm, v_ffn2_w1, v_ffn2_w3, v_ffn2_w2, v_ada_w, v_ada_b, v_final_norm, v_final_ada_w, v_final_ada_b):
    given = dict(zip(["x", "c"] + WEIGHTS + ["loss_target"] + ["m_" + n for n in WEIGHTS] + ["v_" + n for n in WEIGHTS],
                     (x, c, ffn1_norm, ffn1_w1, ffn1_w3, ffn1_w2, mix_norm, w_in, conv_w, conv_b, rg_wa, rg_ba, rg_wx, rg_bx, rg_lam, fox_bf, merge_b, w_rg, w_sb, w_fox, w_o, ffn2_norm, ffn2_w1, ffn2_w3, ffn2_w2, ada_w, ada_b, final_norm, final_ada_w, final_ada_b, loss_target, m_ffn1_norm, m_ffn1_w1, m_ffn1_w3, m_ffn1_w2, m_mix_norm, m_w_in, m_conv_w, m_conv_b, m_rg_wa, m_rg_ba, m_rg_wx, m_rg_bx, m_rg_lam, m_fox_bf, m_merge_b, m_w_rg, m_w_sb, m_w_fox, m_w_o, m_ffn2_norm, m_ffn2_w1, m_ffn2_w3, m_ffn2_w2, m_ada_w, m_ada_b, m_final_norm, m_final_ada_w, m_final_ada_b, v_ffn1_norm, v_ffn1_w1, v_ffn1_w3, v_ffn1_w2, v_mix_norm, v_w_in, v_conv_w, v_conv_b, v_rg_wa, v_rg_ba, v_rg_wx, v_rg_bx, v_rg_lam, v_fox_bf, v_merge_b, v_w_rg, v_w_sb, v_w_fox, v_w_o, v_ffn2_norm, v_ffn2_w1, v_ffn2_w3, v_ffn2_w2, v_ada_w, v_ada_b, v_final_norm, v_final_ada_w, v_final_ada_b)))
    idx = my_index()
    n_batch = N_DEV * B_LOC
    ada_cols = ada_w.shape[2]
    fin_cols = final_ada_w.shape[1]

    small_in, small_in_meta = _pack([c, conv_w], LANES, 8, F32)
    c_parts, conv_w_parts = _unpack(all_gather("gather_c_conv", small_in), small_in_meta, lead=1)
    c_all = c_parts.reshape(n_batch, D)
    conv_w_all = _unshard(conv_w_parts, 2)
    c_act = ew_fwd("c_silu", _f_silu, [c_all], [], [], [D], [F32])[0]
    mod_cols = [matmul(f"ada_proj_{l}", c_act, ada_w[l], "nn") for l in range(DEPTH)]
    mod_cols.append(matmul("ada_proj_final", c_act, final_ada_w, "nn"))
    mod_g = all_gather("gather_mod", jnp.concatenate(mod_cols, axis=1))

    shards = {(l, group): [given[n][l].astype(BF16) for n in members] for l in range(DEPTH) for group, members in GROUPS}
    shards, mod_g = lax.optimization_barrier((shards, mod_g))
    gather_handles, started = {}, jnp.zeros((), F32)
    for l in range(DEPTH):
        for group, _ in GROUPS:
            gather_handles[l, group], token = gather_start(f"gather_start_{group}{l}", shards[l, group])
            started = started + token[0, 0]
    mod_g = mod_g + started

    def weights_of(l, group, after):
        members = dict(GROUPS)[group]
        blocks = gather_finish(f"gather_finish_{group}{l}", gather_handles[l, group], after)
        return {n: _unshard(b, GATHERED[n] - 1) for n, b in zip(members, blocks)}

    mods = []
    for l in range(DEPTH):
        full = mod_g[:, :, l * ada_cols:(l + 1) * ada_cols].transpose(1, 0, 2).reshape(n_batch, N_DEV * ada_cols)
        full = ew_fwd(f"ada_bias_{l}", _f_add_bias, [full], [], [ada_b[l][None]], [full.shape[1]], [F32])[0]
        mods.append(lax.dynamic_slice_in_dim(full, idx * B_LOC, B_LOC, axis=0).reshape(B_LOC, 3, 3, D))
    fm = mod_g[:, :, DEPTH * ada_cols:].transpose(1, 0, 2).reshape(n_batch, N_DEV * fin_cols)
    fm = ew_fwd("ada_bias_final", _f_add_bias, [fm], [], [final_ada_b[None]], [fm.shape[1]], [F32])[0]
    fm = lax.dynamic_slice_in_dim(fm, idx * B_LOC, B_LOC, axis=0).reshape(B_LOC, 2, D)

    def mixer_params(l, w):
        wi = w["w_in"]
        cut = IN_CUTS
        return dict(
            gain=mix_norm[l][None], w_rgx=wi[:, cut[0]:cut[1]], w_gate=wi[:, cut[1]:cut[2]], w_sbqkv=wi[:, cut[2]:cut[3]],
            w_foxqkv=wi[:, cut[3]:cut[4]], w_f=_pad_lanes(wi[:, cut[4]:cut[5]]), w_merge=wi[:, cut[5]:cut[6]],
            conv_w8=jnp.pad(conv_w_all[l], ((0, 8 - CONV_K), (0, 0))), conv_b=conv_b[l][None],
            wa_bd=_block_diag(rg_wa[l]), wx_bd=_block_diag(rg_wx[l]), ba=rg_ba[l][None], bx=rg_bx[l][None], lam=rg_lam[l][None],
            bf=_pad_lanes(fox_bf[l][None]), merge_b=merge_b[l][None], w_rg=w["w_rg"], w_sb=w["w_sb"], w_fox=w["w_fox"],
            w_o=w["w_o"])

    n_tok = x.shape[0] * x.shape[1]
    h = x.reshape(n_tok, D)
    saved = []
    for l in range(DEPTH):
        m = mods[l]
        w1 = weights_of(l, "ffn1", m if l == 0 else h)
        h, s1 = _ffn_fwd(f"a{l}", h, _bp(m, 0, 0), _bp(m, 0, 1), _bp(m, 0, 2), ffn1_norm[l][None], w1["ffn1_w1"], w1["ffn1_w3"],
                         w1["ffn1_w2"])
        p = mixer_params(l, weights_of(l, "mix", h))
        h, s2 = _mixer_fwd(f"{l}", h, _bp(m, 1, 0), _bp(m, 1, 1), _bp(m, 1, 2), p)
        w3 = weights_of(l, "ffn2", h)
        h, s3 = _ffn_fwd(f"b{l}", h, _bp(m, 2, 0), _bp(m, 2, 1), _bp(m, 2, 2), ffn2_norm[l][None], w3["ffn2_w1"], w3["ffn2_w3"],
                         w3["ffn2_w2"])
        saved.append((s1, s2, s3, p, w1, w3))
    loss_row, dh, dfshift, dfscale, dgain_final = _final_loss(h, loss_target.reshape(n_tok, D), fm[:, 0][:, None, :],
                                                              fm[:, 1][:, None, :], final_norm[None])

    grads = {n: [None] * DEPTH for n in WEIGHTS}
    d_mods = [None] * DEPTH
    scatter_handles = {}
    after_start = jnp.zeros((), F32)

    def scatter_blocks(l, group):
        return [_reshard(grads[n][l], GATHERED[n] - 1).astype(BF16) for n in dict(GROUPS)[group]]

    def start_scatter(l, group, g8s=None):
        g8s = scatter_blocks(l, group) if g8s is None else g8s
        scatter_handles[l, group], token = scatter_start(f"scatter_start_{group}{l}", g8s)
        return token[0, 0]

    for l in reversed(range(DEPTH)):
        m = mods[l]
        s1, s2, s3, p, w1, w3 = saved[l]
        dh, dm3, grads["ffn2_norm"][l], grads["ffn2_w1"][l], grads["ffn2_w3"][l], grads["ffn2_w2"][l] = _ffn_bwd(
            f"b{l}", dh, s3, _bp(m, 2, 0), _bp(m, 2, 1), _bp(m, 2, 2) + after_start, ffn2_norm[l][None], w3["ffn2_w1"],
            w3["ffn2_w3"], w3["ffn2_w2"])
        after_start = start_scatter(l, "ffn2")
        dh, dm2, gm = _mixer_bwd(f"{l}", dh, s2, _bp(m, 1, 0), _bp(m, 1, 1), _bp(m, 1, 2) + after_start, p)
        for n, gval in gm.items():
            grads[n][l] = gval
        after_start = start_scatter(l, "mix")
        dh, dm1, grads["ffn1_norm"][l], grads["ffn1_w1"][l], grads["ffn1_w3"][l], grads["ffn1_w2"][l] = _ffn_bwd(
            f"a{l}", dh, s1, _bp(m, 0, 0), _bp(m, 0, 1), _bp(m, 0, 2) + after_start, ffn1_norm[l][None], w1["ffn1_w1"],
            w1["ffn1_w3"], w1["ffn1_w2"])
        if l > 0:
            after_start = start_scatter(l, "ffn1")
        d_mods[l] = jnp.concatenate([t.reshape(B_LOC, D) for dm in (dm1, dm2, dm3) for t in dm], axis=1)
    grad_x = dh.reshape(x.shape)
    d_fm = jnp.concatenate([dfshift.reshape(B_LOC, D), dfscale.reshape(B_LOC, D)], axis=1)

    rep = {n: jnp.stack([t.reshape(given[n].shape[1:]) for t in grads[n]]) for n in REPLICATED if n != "final_norm"}
    rep["final_norm"] = dgain_final.reshape(D)
    rep["conv_w"] = jnp.stack(grads["conv_w"])
    rep_names = list(rep)
    rep_slab, rep_meta = _pack([rep[n] for n in rep_names], LANES, 8, F32)
    mod_slab, mod_meta = _pack(d_mods + [d_fm], LANES, 8, F32)
    small_g = all_gather("gather_small_grads", jnp.concatenate([mod_slab, rep_slab], axis=0))
    last_blocks, small_g = lax.optimization_barrier((scatter_blocks(0, "ffn1"), small_g))
    small_g = small_g + start_scatter(0, "ffn1", last_blocks)
    d_mod_all = [t.reshape(n_batch, -1) for t in _unpack(small_g[:, :mod_slab.shape[0]], mod_meta, lead=1)]
    rep_sum = add_blocks("sum_small_grads", [small_g[k, mod_slab.shape[0]:] for k in range(N_DEV)], F32)
    rep_grad = dict(zip(rep_names, _unpack(rep_sum, rep_meta)))
    final_g = {n: rep_grad[n] for n in REPLICATED}
    final_g["conv_w"] = lax.dynamic_slice_in_dim(rep_grad["conv_w"], idx * conv_w.shape[2], conv_w.shape[2], axis=2)
    final_g["ada_b"] = jnp.stack([sum_rows(f"ada_b_grad_{l}", d_mod_all[l])[0] for l in range(DEPTH)])
    final_g["final_ada_b"] = sum_rows("final_ada_b_grad", d_mod_all[DEPTH])[0]
    final_g["ada_w"] = jnp.stack([
        matmul(f"ada_w_grad_{l}", c_act, lax.dynamic_slice_in_dim(d_mod_all[l], idx * ada_cols, ada_cols, axis=1), "tn")
        for l in range(DEPTH)])
    final_g["final_ada_w"] = matmul(
        "final_ada_w_grad", c_act, lax.dynamic_slice_in_dim(d_mod_all[DEPTH], idx * fin_cols, fin_cols, axis=1), "tn")

    shard_g = {n: [None] * DEPTH for n in GATHERED}

    def finish_scatter(l, group, after):
        sums = scatter_finish(f"scatter_finish_{group}{l}", scatter_handles[l, group], after)
        for n, gval in zip(dict(GROUPS)[group], sums):
            shard_g[n][l] = gval

    for l in reversed(range(DEPTH)):
        for group in ("ffn2", "mix", "ffn1"):
            if (l, group) != (0, "ffn1"):
                finish_scatter(l, group, rep_sum)

    delta, new_m, new_v = {}, {}, {}
    last = dict(GROUPS)["ffn1"]
    sharded = [n for n in GATHERED if n not in last] + ["ada_w", "final_ada_w", "conv_w"] + list(last)
    for n in sharded:
        if n == last[0]:
            finish_scatter(0, "ffn1", delta["w_in"])
        if n in GATHERED:
            final_g[n] = jnp.stack(shard_g[n])
        delta[n], new_m[n], new_v[n] = adamw(f"adamw_{n}", given[n], final_g[n], given["m_" + n], given["v_" + n])
    rep_all = [n for n in WEIGHTS if n not in sharded]
    packed = [_pack([src[n] for n in rep_all], LANES, 8, F32)[0]
              for src in (given, final_g, {n: given["m_" + n] for n in rep_all}, {n: given["v_" + n] for n in rep_all})]
    rep_meta_all = _pack([given[n] for n in rep_all], LANES, 8, F32)[1]
    for store, slab_out in zip((delta, new_m, new_v), adamw("adamw_replicated", *packed)):
        store.update(zip(rep_all, _unpack(slab_out, rep_meta_all)))

    loss = lax.psum(loss_row[0, 0], ("x", "y", "c"))
    return (loss, grad_x, *[final_g[n] for n in WEIGHTS], *[delta[n] for n in WEIGHTS], *[new_m[n] for n in WEIGHTS],
            *[new_v[n] for n in WEIGHTS])
```

```python
import functools

import numpy as np
import jax
import jax.numpy as jnp
from jax import lax
from jax.experimental import pallas as pl
from jax.experimental.pallas import tpu as pltpu

F32 = jnp.float32
BF16 = jnp.bfloat16
MESH = pl.DeviceIdType.MESH

N_DEV = 8
D = 1024
SEQ = 2048
B_LOC = 2
N_TOK = B_LOC * SEQ
DEPTH = 2
D_FF = 2816
RG_BLOCKS = 16
RG_C = 8.0
N_HEADS = 8
HEAD_DIM = 64
ATT_W = N_HEADS * HEAD_DIM
LANES = 128
EPS = 1e-6
ATT_SCALE = HEAD_DIM ** -0.5
CONV_K = 4

ADAM_LR = 0.001
ADAM_B1 = 0.9
ADAM_B2 = 0.999
ADAM_EPS = 1e-08
ADAM_WD = 0.01
ADAM_STEP = 10

EW_ROWS = 512
EW_ROWS_WIDE = 256
ATT_BLK = 256


def _pick_tile(dim, target):
    best = None
    for t in range(LANES, min(dim, target) + 1, LANES):
        if dim % t == 0:
            best = t
    return best if best is not None else dim


_DIMS = {"nn": (((1,), (0,)), ((), ())), "nt": (((1,), (1,)), ((), ())), "tn": (((0,), (0,)), ((), ()))}


def matmul(name, a_list, b_list, mode, out_dtype=F32, tm=1024, tn=512):
    if not isinstance(a_list, (list, tuple)):
        a_list, b_list = [a_list], [b_list]
    n = len(a_list)
    m_dim = a_list[0].shape[1] if mode == "tn" else a_list[0].shape[0]
    n_dim = b_list[0].shape[0] if mode == "nt" else b_list[0].shape[1]
    tm, tn = _pick_tile(m_dim, tm), _pick_tile(n_dim, tn)
    dims = _DIMS[mode]

    def body(*refs):
        o_ref = refs[-1]
        acc = None
        for a_ref, b_ref in zip(refs[:n], refs[n:2 * n]):
            d = lax.dot_general(a_ref[...].astype(BF16), b_ref[...].astype(BF16), dims, preferred_element_type=F32)
            acc = d if acc is None else acc + d
        o_ref[...] = acc.astype(o_ref.dtype)

    in_specs = []
    for a in a_list:
        if mode == "tn":
            in_specs.append(pl.BlockSpec((a.shape[0], tm), lambda i, j: (0, i)))
        else:
            in_specs.append(pl.BlockSpec((tm, a.shape[1]), lambda i, j: (i, 0)))
    for b in b_list:
        if mode == "nt":
            in_specs.append(pl.BlockSpec((tn, b.shape[1]), lambda i, j: (j, 0)))
        else:
            in_specs.append(pl.BlockSpec((b.shape[0], tn), lambda i, j: (0, j)))
    return pl.pallas_call(
        body, name=name, grid=(m_dim // tm, n_dim // tn), in_specs=in_specs,
        out_specs=pl.BlockSpec((tm, tn), lambda i, j: (i, j)),
        out_shape=jax.ShapeDtypeStruct((m_dim, n_dim), out_dtype),
        compiler_params=pltpu.CompilerParams(dimension_semantics=("parallel", "parallel")),
    )(*a_list, *b_list)


def _row_spec(w, tm):
    return pl.BlockSpec((tm, w), lambda i: (i, 0))


def _bparam_spec(w, tiles_per_batch):
    return pl.BlockSpec((None, 1, w), lambda i: (i // tiles_per_batch, 0, 0))


def _gparam_spec(shape):
    return pl.BlockSpec(shape, lambda i: (0, 0))


def ew_fwd(name, fn, rows, bparams, gparams, out_widths, out_dtypes, tm=EW_ROWS):
    n_rows = rows[0].shape[0]
    tm = min(tm, n_rows, SEQ)
    tpb = max(SEQ // tm, 1)
    nr, nb, ng = len(rows), len(bparams), len(gparams)

    def body(*refs):
        vals = [r[...] for r in refs[:nr + nb + ng]]
        outs = fn(*vals)
        if not isinstance(outs, (tuple, list)):
            outs = (outs,)
        for o_ref, o in zip(refs[nr + nb + ng:], outs):
            o_ref[...] = o.astype(o_ref.dtype)

    in_specs = ([_row_spec(r.shape[1], tm) for r in rows] + [_bparam_spec(p.shape[2], tpb) for p in bparams]
                + [_gparam_spec(g.shape) for g in gparams])
    outs = pl.pallas_call(
        body, name=name, grid=(n_rows // tm,), in_specs=in_specs,
        out_specs=[_row_spec(w, tm) for w in out_widths],
        out_shape=[jax.ShapeDtypeStruct((n_rows, w), dt) for w, dt in zip(out_widths, out_dtypes)],
        compiler_params=pltpu.CompilerParams(dimension_semantics=("parallel",)),
    )(*rows, *bparams, *gparams)
    return outs


def ew_bwd(name, fn, rows, bparams, gparams, cts, row_grad_dtypes, adds=(), tm=EW_ROWS):
    n_rows = rows[0].shape[0]
    tm = min(tm, n_rows, SEQ)
    tpb = max(SEQ // tm, 1)
    nr, nb, ng, nc = len(rows), len(bparams), len(gparams), len(cts)
    adds = list(adds) + [None] * (nr - len(adds))
    add_idx = [k for k in range(nr) if adds[k] is not None]
    want = [k for k in range(nr) if row_grad_dtypes[k] is not None]

    def body(*refs):
        pos = nr + nb + ng
        vals = [r[...] for r in refs[:pos]]
        ct_vals = [r[...].astype(F32) for r in refs[pos:pos + nc]]
        pos += nc
        add_vals = {k: refs[pos + q][...] for q, k in enumerate(add_idx)}
        pos += len(add_idx)
        out_refs = refs[pos:]
        f32_vals = [v.astype(F32) for v in vals]
        outs, vjp = jax.vjp(lambda *a: fn(*a), *f32_vals)
        single = not isinstance(outs, (tuple, list))
        grads = vjp(ct_vals[0].astype(outs.dtype) if single else tuple(c.astype(o.dtype) for c, o in zip(ct_vals, outs)))
        i = pl.program_id(0)
        q = 0
        for k in want:
            g = grads[k]
            if k in add_vals:
                g = g + add_vals[k].astype(F32)
            out_refs[q][...] = g.astype(out_refs[q].dtype)
            q += 1
        for k in range(nb):
            ref = out_refs[q]
            q += 1

            @pl.when(i % tpb == 0)
            def _():
                ref[...] = jnp.zeros_like(ref)

            ref[...] += grads[nr + k]
        for k in range(ng):
            ref = out_refs[q]
            q += 1

            @pl.when(i == 0)
            def _():
                ref[...] = jnp.zeros_like(ref)

            ref[...] += grads[nr + nb + k]

    in_specs = ([_row_spec(r.shape[1], tm) for r in rows] + [_bparam_spec(p.shape[2], tpb) for p in bparams]
                + [_gparam_spec(g.shape) for g in gparams] + [_row_spec(c.shape[1], tm) for c in cts]
                + [_row_spec(adds[k].shape[1], tm) for k in add_idx])
    out_specs = ([_row_spec(rows[k].shape[1], tm) for k in want] + [_bparam_spec(p.shape[2], tpb) for p in bparams]
                 + [_gparam_spec(g.shape) for g in gparams])
    out_shape = ([jax.ShapeDtypeStruct(rows[k].shape, row_grad_dtypes[k]) for k in want]
                 + [jax.ShapeDtypeStruct(p.shape, F32) for p in bparams] + [jax.ShapeDtypeStruct(g.shape, F32) for g in gparams])
    outs = pl.pallas_call(
        body, name=name, grid=(n_rows // tm,), in_specs=in_specs, out_specs=out_specs, out_shape=out_shape,
        compiler_params=pltpu.CompilerParams(dimension_semantics=("arbitrary",)),
    )(*rows, *bparams, *gparams, *cts, *[adds[k] for k in add_idx])
    d_rows = list(outs[:len(want)])
    d_b = list(outs[len(want):len(want) + nb])
    d_g = list(outs[len(want) + nb:])
    return d_rows, d_b, d_g


def f_norm_mod(x, shift, scale, gain):
    x = x.astype(F32)
    y = x * lax.rsqrt(jnp.mean(x * x, axis=-1, keepdims=True) + EPS)
    return (y * gain) * (1.0 + scale) + shift


def f_swiglu(a, b3):
    a = a.astype(F32)
    return (a * jax.nn.sigmoid(a)) * b3.astype(F32)


def f_resid(coef, x, y, gate):
    return x.astype(F32) + (coef * (1.0 + gate)) * y.astype(F32)


def f_rg_gates(pre_r, pre_i, xa, ba, bx, lam):
    r = jax.nn.sigmoid(pre_r + ba)
    i = jax.nn.sigmoid(pre_i + bx)
    softplus_neg_lam = jnp.maximum(-lam, 0.0) + jnp.log(1.0 + jnp.exp(-jnp.abs(lam)))
    log_a = (-RG_C) * r * softplus_neg_lam
    a = jnp.exp(log_a)
    u = jnp.sqrt(1.0 - a * a) * (i * xa)
    return a, u


def f_gelu_mul(gate, hs):
    g = gate.astype(F32)
    gelu = 0.5 * g * (1.0 + jnp.tanh(0.7978845608028654 * (g + 0.044715 * g * g * g)))
    return gelu * hs.astype(F32)


def f_log_sigmoid_bias(f, bf):
    z = f.astype(F32) + bf
    return jnp.minimum(z, 0.0) - jnp.log(1.0 + jnp.exp(-jnp.abs(z)))


def f_merge(mg, pa, pb, pc, merge_b):
    g = jax.nn.sigmoid(mg.astype(F32) + merge_b)
    return g[:, :D] * pa.astype(F32) + g[:, D:2 * D] * pb.astype(F32) + g[:, 2 * D:] * pc.astype(F32)


CONV_CB = 256
SCAN_CB = 512
CUM_RB = 512


def _shift_down(x, d):
    if d == 0:
        return x
    rows = lax.broadcasted_iota(jnp.int32, x.shape, 0)
    return jnp.where(rows >= d, pltpu.roll(x, d, axis=0), 0.0)


def _shift_up(x, d):
    if d == 0:
        return x
    s = x.shape[0]
    rows = lax.broadcasted_iota(jnp.int32, x.shape, 0)
    return jnp.where(rows < s - d, pltpu.roll(x, s - d, axis=0), 0.0)


def conv_fwd(x, w8, b):
    n, c = x.shape
    nb = n // SEQ

    def body(x_ref, w_ref, b_ref, y_ref):
        xv = x_ref[...]
        acc = jnp.broadcast_to(b_ref[...], xv.shape)
        for k in range(CONV_K):
            acc = acc + w_ref[k:k + 1, :] * _shift_down(xv, CONV_K - 1 - k)
        y_ref[...] = acc

    return pl.pallas_call(
        body, name="conv_fwd", grid=(c // CONV_CB, nb),
        in_specs=[pl.BlockSpec((SEQ, CONV_CB), lambda j, i: (i, j)), pl.BlockSpec((8, CONV_CB), lambda j, i: (0, j)),
                  pl.BlockSpec((1, CONV_CB), lambda j, i: (0, j))],
        out_specs=pl.BlockSpec((SEQ, CONV_CB), lambda j, i: (i, j)),
        out_shape=jax.ShapeDtypeStruct((n, c), F32),
        compiler_params=pltpu.CompilerParams(dimension_semantics=("parallel", "parallel")),
    )(x, w8, b)


def conv_bwd(x, w8, dy1, dy2):
    n, c = x.shape
    nb = n // SEQ

    def body(x_ref, w_ref, dy1_ref, dy2_ref, dx_ref, dwb_ref):
        xv = x_ref[...]
        dy = dy1_ref[...] + dy2_ref[...]
        dx = jnp.zeros_like(xv)
        parts = []
        for k in range(CONV_K):
            d = CONV_K - 1 - k
            dx = dx + w_ref[k:k + 1, :] * _shift_up(dy, d)
            parts.append(jnp.sum(dy * _shift_down(xv, d), axis=0, keepdims=True))
        parts.append(jnp.sum(dy, axis=0, keepdims=True))
        parts.append(jnp.zeros((8 - len(parts), xv.shape[1]), F32))
        dx_ref[...] = dx.astype(BF16)

        @pl.when(pl.program_id(1) == 0)
        def _():
            dwb_ref[...] = jnp.zeros_like(dwb_ref)

        dwb_ref[...] += jnp.concatenate(parts, axis=0)

    return pl.pallas_call(
        body, name="conv_bwd", grid=(c // CONV_CB, nb),
        in_specs=[pl.BlockSpec((SEQ, CONV_CB), lambda j, i: (i, j)), pl.BlockSpec((8, CONV_CB), lambda j, i: (0, j)),
                  pl.BlockSpec((SEQ, CONV_CB), lambda j, i: (i, j)), pl.BlockSpec((SEQ, CONV_CB), lambda j, i: (i, j))],
        out_specs=[pl.BlockSpec((SEQ, CONV_CB), lambda j, i: (i, j)), pl.BlockSpec((8, CONV_CB), lambda j, i: (0, j))],
        out_shape=[jax.ShapeDtypeStruct((n, c), BF16), jax.ShapeDtypeStruct((8, c), F32)],
        compiler_params=pltpu.CompilerParams(dimension_semantics=("parallel", "arbitrary")),
    )(x, w8, dy1, dy2)


def scan_fwd(a, u):
    n, c = a.shape

    def body(a_ref, u_ref, h_ref):
        def step(t, h):
            h = a_ref[pl.ds(t, 1), :] * h + u_ref[pl.ds(t, 1), :]
            h_ref[pl.ds(t, 1), :] = h
            return h

        lax.fori_loop(0, SEQ, step, jnp.zeros((1, SCAN_CB), F32), unroll=8)

    spec = pl.BlockSpec((SEQ, SCAN_CB), lambda i, j: (i, j))
    return pl.pallas_call(
        body, name="scan_fwd", grid=(n // SEQ, c // SCAN_CB), in_specs=[spec, spec], out_specs=spec,
        out_shape=jax.ShapeDtypeStruct((n, c), F32),
        compiler_params=pltpu.CompilerParams(dimension_semantics=("parallel", "parallel")),
    )(a, u)


def scan_bwd(a, h, g):
    n, c = a.shape

    def body(a_ref, h_ref, g_ref, da_ref, du_ref):
        def step(k, carry):
            t = SEQ - 1 - k
            dh = g_ref[pl.ds(t, 1), :] + carry
            du_ref[pl.ds(t, 1), :] = dh
            h_prev = jnp.where(t > 0, h_ref[pl.ds(jnp.maximum(t - 1, 0), 1), :], 0.0)
            da_ref[pl.ds(t, 1), :] = dh * h_prev
            return a_ref[pl.ds(t, 1), :] * dh

        lax.fori_loop(0, SEQ, step, jnp.zeros((1, SCAN_CB), F32), unroll=8)

    spec = pl.BlockSpec((SEQ, SCAN_CB), lambda i, j: (i, j))
    return pl.pallas_call(
        body, name="scan_bwd", grid=(n // SEQ, c // SCAN_CB), in_specs=[spec, spec, spec], out_specs=[spec, spec],
        out_shape=[jax.ShapeDtypeStruct((n, c), F32), jax.ShapeDtypeStruct((n, c), F32)],
        compiler_params=pltpu.CompilerParams(dimension_semantics=("parallel", "parallel")),
    )(a, h, g)


def _split3_dot(m, x):
    hi = x.astype(BF16)
    r1 = x - hi.astype(F32)
    mid = r1.astype(BF16)
    lo = (r1 - mid.astype(F32)).astype(BF16)
    dot = functools.partial(jnp.dot, preferred_element_type=F32)
    return dot(m, hi) + dot(m, mid) + dot(m, lo)


def seq_cumsum(name, xs, signs, reverse):
    n, w = xs[0].shape
    nx = len(xs)
    rb = min(CUM_RB, SEQ)

    def body(*refs):
        x = None
        for r, sg in zip(refs[:nx], signs):
            x = sg * r[...] if x is None else x + sg * r[...]
        q0 = pl.program_id(1) * rb
        row = q0 + lax.broadcasted_iota(jnp.int32, (rb, SEQ), 0)
        col = lax.broadcasted_iota(jnp.int32, (rb, SEQ), 1)
        tri = ((col >= row) if reverse else (col <= row)).astype(BF16)
        refs[nx][...] = _split3_dot(tri, x)

    return pl.pallas_call(
        body, name=name, grid=(n // SEQ, SEQ // rb),
        in_specs=[pl.BlockSpec((SEQ, w), lambda i, j: (i, 0)) for _ in xs],
        out_specs=pl.BlockSpec((rb, w), lambda i, j: (i * (SEQ // rb) + j, 0)),
        out_shape=jax.ShapeDtypeStruct((n, w), F32),
        compiler_params=pltpu.CompilerParams(dimension_semantics=("parallel", "parallel")),
    )(*xs)


N_PAIRS = N_HEADS // 2


def _dot_nt(a, b):
    return lax.dot_general(a, b, _DIMS["nt"], preferred_element_type=F32)


def _dot_tn(a, b):
    return lax.dot_general(a, b, _DIMS["tn"], preferred_element_type=F32)


def _dot_nn(a, b):
    return lax.dot_general(a, b, _DIMS["nn"], preferred_element_type=F32)


def _split2_dot(x, m):
    hi = x.astype(BF16)
    lo = (x - hi.astype(F32)).astype(BF16)
    return _dot_nn(hi, m) + _dot_nn(lo, m)


def _head_mask(j):
    lane = lax.broadcasted_iota(jnp.int32, (1, LANES), 1)
    return (lane // HEAD_DIM) == j


def _lane_pick(x, h):
    lane = lax.broadcasted_iota(jnp.int32, x.shape, 1)
    return jnp.sum(jnp.where(lane == h, x, 0.0), axis=1, keepdims=True)


def _lane_put(col, h):
    lane = lax.broadcasted_iota(jnp.int32, (col.shape[0], LANES), 1)
    return jnp.where(lane == h, col, 0.0)


def _softplus(z):
    return jnp.maximum(z, 0.0) + jnp.log(1.0 + jnp.exp(-jnp.abs(z)))


def _qkv_specs():
    return [pl.BlockSpec((SEQ, LANES), lambda b, p: (b, p)),
            pl.BlockSpec((SEQ, LANES), lambda b, p: (b, N_PAIRS + p)),
            pl.BlockSpec((SEQ, LANES), lambda b, p: (b, 2 * N_PAIRS + p))]


def _pair_spec():
    return pl.BlockSpec((SEQ, LANES), lambda b, p: (b, p))


def _below_diagonal(strictly):
    t = ATT_BLK
    row = lax.broadcasted_iota(jnp.int32, (t, t), 0)
    col = lax.broadcasted_iota(jnp.int32, (t, t), 1)
    return (row > col) if strictly else (row >= col)


def _over_key_blocks(qi, step, init, reverse):
    t = ATT_BLK
    q0 = pl.multiple_of(qi * t, t)

    def off_diagonal(kk, carry):
        ki = (qi - 1 - kk) if reverse else kk
        return step(pl.multiple_of(ki * t, t), carry, False)

    if reverse:
        return lax.fori_loop(0, qi, off_diagonal, step(q0, init, True))
    return step(q0, lax.fori_loop(0, qi, off_diagonal, init), True)


def _masked_q(qb, j):
    return (jnp.where(_head_mask(j), qb, 0.0) * ATT_SCALE).astype(BF16)


def sb_attn_fwd(qkv):
    n = qkv.shape[0]
    t = ATT_BLK

    def body(q_ref, k_ref, v_ref, o_ref, tot_ref):
        pair = pl.program_id(1)
        strict = _below_diagonal(True)
        later = strict.astype(BF16)

        @pl.when(pair == 0)
        def _():
            tot_ref[...] = jnp.zeros_like(tot_ref)

        def q_block(qi, _):
            q0 = pl.multiple_of(qi * t, t)
            qb = q_ref[pl.ds(q0, t), :]
            qms = [_masked_q(qb, j) for j in range(2)]

            def step(k0, carry, diagonal):
                kb = k_ref[pl.ds(k0, t), :].astype(BF16)
                vb = v_ref[pl.ds(k0, t), :].astype(BF16)
                heads = range(2)
                zs = [_dot_nt(qms[j], kb) for j in heads]
                sps = [_softplus(z) for z in zs]
                log_keeps = [(jnp.where(strict, -sp, 0.0) if diagonal else -sp) for sp in sps]
                right_l = [_split2_dot(lk, later) for lk in log_keeps]
                atts = [jnp.exp((zs[j] - sps[j]) + right_l[j] + carry[j][0]) for j in heads]
                if diagonal:
                    atts = [jnp.where(strict, att, 0.0) for att in atts]
                return tuple((carry[j][0] + jnp.sum(log_keeps[j], axis=1, keepdims=True),
                              carry[j][1] + _dot_nn(atts[j].astype(BF16), vb)) for j in heads)

            init = ((jnp.zeros((t, 1), F32), jnp.zeros((t, LANES), F32)),) * 2
            (tot0, acc0), (tot1, acc1) = _over_key_blocks(qi, step, init, reverse=True)
            o_ref[pl.ds(q0, t), :] = jnp.where(_head_mask(0), acc0, acc1)
            tot_ref[pl.ds(q0, t), :] += _lane_put(tot0, 2 * pair) + _lane_put(tot1, 2 * pair + 1)
            return 0

        lax.fori_loop(0, SEQ // t, q_block, 0)

    batch_spec = pl.BlockSpec((SEQ, LANES), lambda b, p: (b, 0))
    return pl.pallas_call(
        body, name="sb_attn_fwd", grid=(n // SEQ, N_PAIRS), in_specs=_qkv_specs(), out_specs=[_pair_spec(), batch_spec],
        out_shape=[jax.ShapeDtypeStruct((n, ATT_W), F32), jax.ShapeDtypeStruct((n, LANES), F32)],
        compiler_params=pltpu.CompilerParams(dimension_semantics=("parallel", "arbitrary")),
    )(qkv, qkv, qkv)


def sb_attn_bwd(qkv, tot, do):
    n = qkv.shape[0]
    t = ATT_BLK

    def body(q_ref, k_ref, v_ref, tot_ref, do_ref, dq_ref, dk_ref, dv_ref, dk_acc, dv_acc):
        pair = pl.program_id(1)
        strict = _below_diagonal(True)
        upto = jnp.logical_not(strict).astype(BF16)
        dk_acc[...] = jnp.zeros_like(dk_acc)
        dv_acc[...] = jnp.zeros_like(dv_acc)

        def q_block(qi, _):
            q0 = pl.multiple_of(qi * t, t)
            qb = q_ref[pl.ds(q0, t), :]
            tot_q = tot_ref[pl.ds(q0, t), :]
            dob = do_ref[pl.ds(q0, t), :].astype(F32)
            qms = [_masked_q(qb, j) for j in range(2)]
            doms = [jnp.where(_head_mask(j), dob, 0.0).astype(BF16) for j in range(2)]
            totals = [_lane_pick(tot_q, 2 * pair + j) for j in range(2)]

            def step(k0, carry, diagonal):
                kb = k_ref[pl.ds(k0, t), :].astype(BF16)
                vb = v_ref[pl.ds(k0, t), :].astype(BF16)
                heads = range(2)
                zs = [_dot_nt(qms[j], kb) for j in heads]
                d_atts = [_dot_nt(doms[j], vb) for j in heads]
                sps = [_softplus(z) for z in zs]
                log_keeps = [(jnp.where(strict, -sp, 0.0) if diagonal else -sp) for sp in sps]
                log_betas = [z - sp for z, sp in zip(zs, sps)]
                left_l = [_split2_dot(lk, upto) for lk in log_keeps]
                atts = [jnp.exp(log_betas[j] + (totals[j] - (carry[j][0] + left_l[j]))) for j in heads]
                if diagonal:
                    atts = [jnp.where(strict, att, 0.0) for att in atts]
                gs = [att * d_att for att, d_att in zip(atts, d_atts)]
                dv = _dot_tn(atts[0].astype(BF16), doms[0]) + _dot_tn(atts[1].astype(BF16), doms[1])
                left_g = [_dot_nn(g.astype(BF16), upto) for g in gs]
                dzs = [gs[j] - jnp.exp(log_betas[j]) * (carry[j][1] + left_g[j]) for j in heads]
                if diagonal:
                    dzs = [jnp.where(strict, dz, 0.0) for dz in dzs]
                dzs = [dz.astype(BF16) for dz in dzs]
                dk = _dot_tn(dzs[0], qms[0]) + _dot_tn(dzs[1], qms[1])
                dk_acc[pl.ds(k0, t), :] += dk
                dv_acc[pl.ds(k0, t), :] += dv
                return tuple((carry[j][0] + jnp.sum(log_keeps[j], axis=1, keepdims=True),
                              carry[j][1] + jnp.sum(gs[j], axis=1, keepdims=True), carry[j][2] + _dot_nn(dzs[j], kb))
                             for j in heads)

            zero = jnp.zeros((t, 1), F32)
            init = ((zero, zero, jnp.zeros((t, LANES), F32)),) * 2
            (_, _, dq0), (_, _, dq1) = _over_key_blocks(qi, step, init, reverse=False)
            dq_ref[pl.ds(q0, t), :] = (jnp.where(_head_mask(0), dq0, dq1) * ATT_SCALE).astype(BF16)
            return 0

        lax.fori_loop(0, SEQ // t, q_block, 0)
        dk_ref[...] = dk_acc[...].astype(BF16)
        dv_ref[...] = dv_acc[...].astype(BF16)

    out = jax.ShapeDtypeStruct((n, ATT_W), BF16)
    batch_spec = pl.BlockSpec((SEQ, LANES), lambda b, p: (b, 0))
    return pl.pallas_call(
        body, name="sb_attn_bwd", grid=(n // SEQ, N_PAIRS), in_specs=_qkv_specs() + [batch_spec, _pair_spec()],
        out_specs=[_pair_spec()] * 3, out_shape=[out, out, out],
        scratch_shapes=[pltpu.VMEM((SEQ, LANES), F32), pltpu.VMEM((SEQ, LANES), F32)],
        compiler_params=pltpu.CompilerParams(dimension_semantics=("parallel", "parallel")),
    )(qkv, qkv, qkv, tot, do)


NEG_BIG = -1e30


def fox_attn_fwd(qkv, cum, cum_t):
    n = qkv.shape[0]
    t = ATT_BLK

    def body(q_ref, k_ref, v_ref, cum_ref, cumt_ref, o_ref, lse_ref):
        pair = pl.program_id(1)
        causal = _below_diagonal(False)

        @pl.when(pair == 0)
        def _():
            lse_ref[...] = jnp.zeros_like(lse_ref)

        def q_block(qi, _):
            q0 = pl.multiple_of(qi * t, t)
            qb = q_ref[pl.ds(q0, t), :]
            cum_q = cum_ref[pl.ds(q0, t), :]
            qms = [_masked_q(qb, j) for j in range(2)]
            cqs = [_lane_pick(cum_q, 2 * pair + j) for j in range(2)]

            def step(k0, carry, diagonal):
                kb = k_ref[pl.ds(k0, t), :].astype(BF16)
                vb = v_ref[pl.ds(k0, t), :].astype(BF16)
                heads = range(2)
                zs = [_dot_nt(qms[j], kb) + cqs[j] - cumt_ref[pl.ds(2 * pair + j, 1), pl.ds(k0, t)] for j in heads]
                if diagonal:
                    zs = [jnp.where(causal, z, NEG_BIG) for z in zs]
                m_new = [jnp.maximum(carry[j][0], jnp.max(zs[j], axis=1, keepdims=True)) for j in heads]
                ps = [jnp.exp(zs[j] - m_new[j]) for j in heads]
                alphas = [jnp.exp(carry[j][0] - m_new[j]) for j in heads]
                return tuple((m_new[j], alphas[j] * carry[j][1] + jnp.sum(ps[j], axis=1, keepdims=True),
                              alphas[j] * carry[j][2] + _dot_nn(ps[j].astype(BF16), vb)) for j in heads)

            init = ((jnp.full((t, 1), NEG_BIG, F32), jnp.zeros((t, 1), F32), jnp.zeros((t, LANES), F32)),) * 2
            (m0, l0, acc0), (m1, l1, acc1) = _over_key_blocks(qi, step, init, reverse=False)
            o_ref[pl.ds(q0, t), :] = jnp.where(_head_mask(0), acc0 / l0, acc1 / l1)
            lse_ref[pl.ds(q0, t), :] += _lane_put(m0 + jnp.log(l0), 2 * pair) + _lane_put(m1 + jnp.log(l1), 2 * pair + 1)
            return 0

        lax.fori_loop(0, SEQ // t, q_block, 0)

    batch_spec = pl.BlockSpec((SEQ, LANES), lambda b, p: (b, 0))
    return pl.pallas_call(
        body, name="fox_attn_fwd", grid=(n // SEQ, N_PAIRS),
        in_specs=_qkv_specs() + [batch_spec, pl.BlockSpec((None, N_HEADS, SEQ), lambda b, p: (b, 0, 0))],
        out_specs=[_pair_spec(), batch_spec],
        out_shape=[jax.ShapeDtypeStruct((n, ATT_W), F32), jax.ShapeDtypeStruct((n, LANES), F32)],
        compiler_params=pltpu.CompilerParams(dimension_semantics=("parallel", "arbitrary")),
    )(qkv, qkv, qkv, cum, cum_t)


def fox_attn_bwd(qkv, cum, cum_t, lse, o, do):
    n = qkv.shape[0]
    t = ATT_BLK

    def body(q_ref, k_ref, v_ref, cum_ref, cumt_ref, lse_ref, o_ref, do_ref, dq_ref, dk_ref, dv_ref, dcq_ref, dck_ref,
             dk_acc, dv_acc):
        pair = pl.program_id(1)
        causal = _below_diagonal(False)
        dk_acc[...] = jnp.zeros_like(dk_acc)
        dv_acc[...] = jnp.zeros_like(dv_acc)

        @pl.when(pair == 0)
        def _():
            dcq_ref[...] = jnp.zeros_like(dcq_ref)
            dck_ref[...] = jnp.zeros_like(dck_ref)

        def q_block(qi, _):
            q0 = pl.multiple_of(qi * t, t)
            qb = q_ref[pl.ds(q0, t), :]
            ob = o_ref[pl.ds(q0, t), :]
            dob = do_ref[pl.ds(q0, t), :].astype(F32)
            cum_q = cum_ref[pl.ds(q0, t), :]
            lse_q = lse_ref[pl.ds(q0, t), :]
            qms = [_masked_q(qb, j) for j in range(2)]
            dom32 = [jnp.where(_head_mask(j), dob, 0.0) for j in range(2)]
            doms = [d.astype(BF16) for d in dom32]
            deltas = [jnp.sum(d * ob, axis=1, keepdims=True) for d in dom32]
            cqs = [_lane_pick(cum_q, 2 * pair + j) for j in range(2)]
            lqs = [_lane_pick(lse_q, 2 * pair + j) for j in range(2)]

            def step(k0, carry, diagonal):
                kb = k_ref[pl.ds(k0, t), :].astype(BF16)
                vb = v_ref[pl.ds(k0, t), :].astype(BF16)
                heads = range(2)
                zs = [_dot_nt(qms[j], kb) + cqs[j] - cumt_ref[pl.ds(2 * pair + j, 1), pl.ds(k0, t)] for j in heads]
                d_ps = [_dot_nt(doms[j], vb) for j in heads]
                if diagonal:
                    zs = [jnp.where(causal, z, NEG_BIG) for z in zs]
                ps = [jnp.exp(zs[j] - lqs[j]) for j in heads]
                dv_acc[pl.ds(k0, t), :] += _dot_tn(ps[0].astype(BF16), doms[0]) + _dot_tn(ps[1].astype(BF16), doms[1])
                dzs = [ps[j] * (d_ps[j] - deltas[j]) for j in heads]
                dzb = [dz.astype(BF16) for dz in dzs]
                dk_acc[pl.ds(k0, t), :] += _dot_tn(dzb[0], qms[0]) + _dot_tn(dzb[1], qms[1])
                for j in heads:
                    dck_ref[pl.ds(2 * pair + j, 1), pl.ds(k0, t)] += jnp.sum(dzs[j], axis=0, keepdims=True)
                return tuple((carry[j][0] + _dot_nn(dzb[j], kb), carry[j][1] + jnp.sum(dzs[j], axis=1, keepdims=True))
                             for j in heads)

            init = ((jnp.zeros((t, LANES), F32), jnp.zeros((t, 1), F32)),) * 2
            (dq0, dcq0), (dq1, dcq1) = _over_key_blocks(qi, step, init, reverse=False)
            dq_ref[pl.ds(q0, t), :] = (jnp.where(_head_mask(0), dq0, dq1) * ATT_SCALE).astype(BF16)
            dcq_ref[pl.ds(q0, t), :] += _lane_put(dcq0, 2 * pair) + _lane_put(dcq1, 2 * pair + 1)
            return 0

        lax.fori_loop(0, SEQ // t, q_block, 0)
        dk_ref[...] = dk_acc[...].astype(BF16)
        dv_ref[...] = dv_acc[...].astype(BF16)

    batch_spec = pl.BlockSpec((SEQ, LANES), lambda b, p: (b, 0))
    t_spec = pl.BlockSpec((None, N_HEADS, SEQ), lambda b, p: (b, 0, 0))
    out = jax.ShapeDtypeStruct((n, ATT_W), BF16)
    return pl.pallas_call(
        body, name="fox_attn_bwd", grid=(n // SEQ, N_PAIRS),
        in_specs=_qkv_specs() + [batch_spec, t_spec, batch_spec, _pair_spec(), _pair_spec()],
        out_specs=[_pair_spec()] * 3 + [batch_spec, t_spec],
        scratch_shapes=[pltpu.VMEM((SEQ, LANES), F32), pltpu.VMEM((SEQ, LANES), F32)],
        out_shape=[out, out, out, jax.ShapeDtypeStruct((n, LANES), F32), jax.ShapeDtypeStruct((n // SEQ, N_HEADS, SEQ), F32)],
        compiler_params=pltpu.CompilerParams(dimension_semantics=("parallel", "arbitrary")),
    )(qkv, qkv, qkv, cum, cum_t, lse, o, do)


_HBM = pl.BlockSpec(memory_space=pl.ANY)


def _my_place():
    return lax.axis_index("x"), lax.axis_index("y"), lax.axis_index("c")


def my_index():
    mx, my, mc = _my_place()
    return 4 * mx + 2 * my + mc


def all_gather(name, xs):
    single = not isinstance(xs, (list, tuple))
    xs = [xs] if single else list(xs)
    na = len(xs)

    def body(*refs):
        x_refs, out_refs = refs[:na], refs[na:2 * na]
        send_sems, recv_sems, local_sems = refs[2 * na:]
        mx, my, mc = _my_place()
        me, sibling = (mx, my, mc), (mx, my, 1 - mc)
        chips = [(1 - mx, my), (mx, 1 - my), (1 - mx, 1 - my)]

        def slot(a, px, py, pc):
            return out_refs[a].at[4 * px + 2 * py + pc]

        def copy(a, k, block, to, src=None):
            return pltpu.make_async_remote_copy(
                src_ref=slot(a, *block) if src is None else src, dst_ref=slot(a, *block),
                send_sem=send_sems.at[7 * a + k], recv_sem=recv_sems.at[7 * a + k], device_id=to, device_id_type=MESH)

        mine = [pltpu.make_async_copy(x_refs[a], slot(a, *me), local_sems.at[a]) for a in range(na)]
        for cp in mine:
            cp.start()
        first = []
        for j, chip in enumerate(chips):
            first += [copy(a, 1 + j, me, (*chip, mc), src=x_refs[a]) for a in range(na)]
        first += [copy(a, 0, me, sibling, src=x_refs[a]) for a in range(na)]
        for cp in first:
            cp.start()
        passed = []
        for j, chip in enumerate(chips):
            for a in range(na):
                copy(a, 1 + j, (*chip, mc), me).wait_recv()
                passed.append(copy(a, 4 + j, (*chip, mc), sibling))
                passed[-1].start()
        for a in range(na):
            copy(a, 0, sibling, me).wait_recv()
        for j, chip in enumerate(chips):
            for a in range(na):
                copy(a, 4 + j, (*chip, 1 - mc), me).wait_recv()
        for cp in first + passed:
            cp.wait_send()
        for cp in mine:
            cp.wait()

    outs = pl.pallas_call(
        body, name=name, in_specs=[_HBM] * na, out_specs=[_HBM] * na,
        out_shape=[jax.ShapeDtypeStruct((N_DEV,) + x.shape, x.dtype) for x in xs],
        scratch_shapes=[pltpu.SemaphoreType.DMA((7 * na,)), pltpu.SemaphoreType.DMA((7 * na,)), pltpu.SemaphoreType.DMA((na,))],
    )(*xs)
    return outs[0] if single else list(outs)


_SEM = pl.BlockSpec(memory_space=pltpu.SEMAPHORE)
_HBM_ONLY = pl.BlockSpec(memory_space=pltpu.HBM)
_EFFECT = pltpu.SideEffectType.DATAFLOW_SIDE_EFFECTING
N_PEERS = N_DEV


def _peers():
    mx, my, mc = _my_place()
    return [((1 - mx) if (r >> 2) & 1 else mx, (1 - my) if (r >> 1) & 1 else my, (1 - mc) if r & 1 else mc)
            for r in range(N_DEV)]


def _exchange_copies(scatter, x_refs, land_refs, send_sems, recv_sems):
    me = my_index()
    copies = []
    for a, (x_ref, land_ref) in enumerate(zip(x_refs, land_refs)):
        for r, (px, py, pc) in enumerate(_peers()):
            src = x_ref.at[4 * px + 2 * py + pc] if scatter else x_ref
            dst = land_ref.at[r] if scatter else land_ref.at[me]
            copies.append(pltpu.make_async_remote_copy(
                src_ref=src, dst_ref=dst, send_sem=send_sems.at[N_PEERS * a + r], recv_sem=recv_sems.at[N_PEERS * a + r],
                device_id=(px, py, pc), device_id_type=MESH))
    return copies


def exchange_start(name, xs, scatter):
    na = len(xs)
    lands = [lax.empty((N_PEERS,) + x.shape[1:] if scatter else (N_DEV,) + x.shape, x.dtype) for x in xs]

    def body(*refs):
        x_refs, land_refs, send_sems, recv_sems = refs[:na], refs[na:2 * na], refs[2 * na], refs[2 * na + 1]
        token = refs[-1]
        for cp in _exchange_copies(scatter, x_refs, land_refs, send_sems, recv_sems):
            cp.start()
        token[...] = jnp.zeros_like(token)

    outs = pl.pallas_call(
        body, name=name,
        out_shape=(pltpu.SemaphoreType.DMA((N_PEERS * na,)), pltpu.SemaphoreType.DMA((N_PEERS * na,)),
                   *[pltpu.HBM(x.shape, x.dtype) for x in xs], *[pltpu.HBM(l.shape, l.dtype) for l in lands],
                   jax.ShapeDtypeStruct((8, LANES), F32)),
        in_specs=[_HBM_ONLY] * (2 * na),
        out_specs=(_SEM, _SEM, *[_HBM_ONLY] * (2 * na), pl.BlockSpec(memory_space=pltpu.VMEM)),
        input_output_aliases={i: 2 + i for i in range(2 * na)},
        compiler_params=pltpu.CompilerParams(has_side_effects=_EFFECT),
    )(*[pltpu.with_memory_space_constraint(x, pltpu.HBM) for x in xs],
      *[pltpu.with_memory_space_constraint(l, pltpu.HBM) for l in lands])
    return (scatter, outs[0], outs[1], outs[2:2 + na], outs[2 + na:2 + 2 * na]), outs[-1]


def exchange_finish(name, handle, after):
    scatter, send_sems, recv_sems, xs, lands = handle
    na = len(xs)

    def body(*refs):
        x_refs, land_refs, send_ref, recv_ref = refs[:na], refs[na:2 * na], refs[2 * na], refs[2 * na + 1]
        for cp in _exchange_copies(scatter, x_refs, land_refs, send_ref, recv_ref):
            cp.wait_send()
            cp.wait_recv()

    outs = pl.pallas_call(
        body, name=name,
        out_shape=tuple(pltpu.HBM(t.shape, t.dtype) for t in list(xs) + list(lands)),
        in_specs=[_HBM_ONLY] * (2 * na) + [_SEM, _SEM, _HBM],
        out_specs=tuple([_HBM_ONLY] * (2 * na)),
        input_output_aliases={i: i for i in range(2 * na)},
        compiler_params=pltpu.CompilerParams(has_side_effects=_EFFECT),
    )(*xs, *lands, send_sems, recv_sems, after)
    return list(outs[:na]), list(outs[na:])


def _pick_rows(n, target):
    best = None
    for t in range(8, min(n, target) + 1, 8):
        if n % t == 0:
            best = t
    return best if best is not None else n


def add_blocks(name, parts, out_dtype, rows=512):
    r, w = parts[0].shape
    tr = _pick_rows(r, rows)

    def body(*refs):
        acc = refs[0][...].astype(F32)
        for ref in refs[1:-1]:
            acc = acc + ref[...].astype(F32)
        refs[-1][...] = acc.astype(refs[-1].dtype)

    spec = pl.BlockSpec((tr, w), lambda i: (i, 0))
    return pl.pallas_call(
        body, name=name, grid=(r // tr,), in_specs=[spec] * len(parts), out_specs=spec,
        out_shape=jax.ShapeDtypeStruct((r, w), out_dtype),
        compiler_params=pltpu.CompilerParams(dimension_semantics=("parallel",)),
    )(*parts)


def sum_rows(name, x):
    def body(x_ref, o_ref):
        o_ref[...] = jnp.sum(x_ref[...], axis=0, keepdims=True)

    return pl.pallas_call(body, name=name, out_shape=jax.ShapeDtypeStruct((1, x.shape[1]), F32))(x)


def gather_start(name, blocks):
    return exchange_start(name, blocks, scatter=False)


def gather_finish(name, handle, after):
    return exchange_finish(name, handle, after)[1]


def scatter_start(name, g8s):
    return exchange_start(name, g8s, scatter=True)


def scatter_finish(name, handle, after):
    _, lands = exchange_finish(name, handle, after)
    outs = []
    for a, land in enumerate(lands):
        w = land.shape[-1]
        outs.append(add_blocks(f"{name}_sum{a}", [land[k].reshape(-1, w) for k in range(N_PEERS)], F32).reshape(land.shape[1:]))
    return outs


def _pack(arrays, width, row_mult, dtype, lead=0):
    parts, metas = [], []
    for a in arrays:
        lead_shape = a.shape[:lead]
        size = int(np.prod(a.shape[lead:]))
        chunk = row_mult * width
        padded = -(-size // chunk) * chunk
        flat = a.astype(dtype).reshape(lead_shape + (size,))
        if padded != size:
            flat = jnp.pad(flat, [(0, 0)] * lead + [(0, padded - size)])
        parts.append(flat.reshape(lead_shape + (padded // width, width)))
        metas.append((a.shape[lead:], size, padded // width))
    return jnp.concatenate(parts, axis=lead), metas


def _unpack(slab, metas, lead=0):
    out, r0 = [], 0
    for shape, size, rows in metas:
        part = lax.slice_in_dim(slab, r0, r0 + rows, axis=lead)
        lead_shape = part.shape[:lead]
        flat = part.reshape(lead_shape + (rows * part.shape[-1],))
        out.append(lax.slice_in_dim(flat, 0, size, axis=lead).reshape(lead_shape + tuple(shape)))
        r0 += rows
    return out


def _f_adamw(w, g, m, v):
    m = ADAM_B1 * m + (1.0 - ADAM_B1) * g
    v = ADAM_B2 * v + (1.0 - ADAM_B2) * (g * g)
    m_hat = m / (1.0 - ADAM_B1 ** ADAM_STEP)
    v_hat = v / (1.0 - ADAM_B2 ** ADAM_STEP)
    delta = (-ADAM_LR) * (m_hat / (jnp.sqrt(v_hat) + ADAM_EPS) + ADAM_WD * w)
    return delta, m, v


def adamw(name, w, g, m, v):
    shape = w.shape
    w2 = shape[-1]
    flat = [a.reshape(-1, w2) for a in (w, g, m, v)]
    tm = _pick_rows(flat[0].shape[0], 256)
    outs = ew_fwd(name, _f_adamw, flat, [], [], [w2] * 3, [F32] * 3, tm=tm)
    return [o.reshape(shape) for o in outs]


WEIGHTS = ["ffn1_norm", "ffn1_w1", "ffn1_w3", "ffn1_w2", "mix_norm", "w_in", "conv_w", "conv_b", "rg_wa", "rg_ba", "rg_wx",
           "rg_bx", "rg_lam", "fox_bf", "merge_b", "w_rg", "w_sb", "w_fox", "w_o", "ffn2_norm", "ffn2_w1", "ffn2_w3",
           "ffn2_w2", "ada_w", "ada_b", "final_norm", "final_ada_w", "final_ada_b"]
GATHERED = {"ffn1_w1": 2, "ffn1_w3": 2, "ffn1_w2": 1, "w_in": 2, "w_rg": 1, "w_sb": 2, "w_fox": 2, "w_o": 1,
            "ffn2_w1": 2, "ffn2_w3": 2, "ffn2_w2": 1}
REPLICATED = ["ffn1_norm", "mix_norm", "conv_b", "rg_wa", "rg_ba", "rg_wx", "rg_bx", "rg_lam", "fox_bf", "merge_b",
              "ffn2_norm", "final_norm"]
GROUPS = (("ffn1", ("ffn1_w1", "ffn1_w3", "ffn1_w2")), ("mix", ("w_in", "w_rg", "w_sb", "w_fox", "w_o")),
          ("ffn2", ("ffn2_w1", "ffn2_w3", "ffn2_w2")))
IN_CUTS = (0, 1024, 2048, 3584, 5120, 5128, 8200)


def _unshard(g, axis):
    g = jnp.moveaxis(g, 0, axis)
    shape = g.shape
    return g.reshape(shape[:axis] + (shape[axis] * shape[axis + 1],) + shape[axis + 2:])


def _reshard(full, axis):
    shape = full.shape
    g = full.reshape(shape[:axis] + (N_DEV, shape[axis] // N_DEV) + shape[axis + 1:])
    return jnp.moveaxis(g, axis, 0)


def _block_diag(w):
    nb, bd, _ = w.shape
    eye = jnp.eye(nb, dtype=bool)[:, None, :, None]
    return jnp.where(eye, w[:, :, None, :], 0.0).reshape(nb * bd, nb * bd)


def _diag_blocks(m, nb=RG_BLOCKS):
    bd = m.shape[0] // nb
    return jnp.stack([m[k * bd:(k + 1) * bd, k * bd:(k + 1) * bd] for k in range(nb)])


def _pad_lanes(a, width=LANES):
    return jnp.pad(a, [(0, 0)] * (a.ndim - 1) + [(0, width - a.shape[-1])])


def _bp(m, k, which):
    return m[:, k, which][:, None, :]


def _f_silu(c):
    return c * jax.nn.sigmoid(c)


def _f_add_bias(a, b):
    return a + b


FFN_TM = 512
FFN_TN = 1408


def ffn_up(name, h, w1, w3):
    n, k = h.shape
    f = w1.shape[1]
    tm, tn = min(FFN_TM, n), _pick_tile(f, FFN_TN)

    def body(h_ref, w1_ref, w3_ref, a_ref, b_ref, s_ref):
        hv = h_ref[...]
        a = jnp.dot(hv, w1_ref[...], preferred_element_type=F32)
        b = jnp.dot(hv, w3_ref[...], preferred_element_type=F32)
        a_ref[...] = a.astype(BF16)
        b_ref[...] = b.astype(BF16)
        s_ref[...] = ((a * jax.nn.sigmoid(a)) * b).astype(BF16)

    wspec = pl.BlockSpec((k, tn), lambda i, j: (0, j))
    ospec = pl.BlockSpec((tm, tn), lambda i, j: (i, j))
    out = jax.ShapeDtypeStruct((n, f), BF16)
    return pl.pallas_call(
        body, name=name, grid=(n // tm, f // tn), in_specs=[pl.BlockSpec((tm, k), lambda i, j: (i, 0)), wspec, wspec],
        out_specs=[ospec] * 3, out_shape=[out] * 3,
        compiler_params=pltpu.CompilerParams(dimension_semantics=("parallel", "parallel")),
    )(h, w1, w3)


def ffn_down_dx(name, dy, w2, a, b):
    n, k = dy.shape
    f = w2.shape[0]
    tm, tn = min(FFN_TM, n), _pick_tile(f, FFN_TN)

    def body(dy_ref, w2_ref, a_ref, b_ref, da_ref, db_ref):
        ds = _dot_nt(dy_ref[...], w2_ref[...])
        av = a_ref[...].astype(F32)
        sig = jax.nn.sigmoid(av)
        da_ref[...] = (ds * b_ref[...].astype(F32) * (sig * (1.0 + av * (1.0 - sig)))).astype(BF16)
        db_ref[...] = (ds * (av * sig)).astype(BF16)

    ospec = pl.BlockSpec((tm, tn), lambda i, j: (i, j))
    out = jax.ShapeDtypeStruct((n, f), BF16)
    return pl.pallas_call(
        body, name=name, grid=(n // tm, f // tn),
        in_specs=[pl.BlockSpec((tm, k), lambda i, j: (i, 0)), pl.BlockSpec((tn, k), lambda i, j: (j, 0)), ospec, ospec],
        out_specs=[ospec] * 2, out_shape=[out] * 2,
        compiler_params=pltpu.CompilerParams(dimension_semantics=("parallel", "parallel")),
    )(dy, w2, a, b)


def _ffn_fwd(tag, x, shift, scale, gate, gain, w1, w3, w2):
    h = ew_fwd(f"ffn_norm_{tag}", f_norm_mod, [x], [shift, scale], [gain], [D], [BF16])[0]
    a, b3, s = ffn_up(f"ffn_up_{tag}", h, w1, w3)
    y = matmul(f"ffn_down_{tag}", s, w2, "nn", tm=1024)
    xo = ew_fwd(f"ffn_res_{tag}", functools.partial(f_resid, 0.5), [x, y], [gate], [], [D], [F32])[0]
    return xo, (x, h, a, b3, s, y)


def _ffn_bwd(tag, dxo, saved, shift, scale, gate, gain, w1, w3, w2):
    x, h, a, b3, s, y = saved
    (dy,), (dgate,), _ = ew_bwd(f"ffn_res_bwd_{tag}", functools.partial(f_resid, 0.5), [x, y], [gate], [], [dxo], [None, BF16])
    da, db3 = ffn_down_dx(f"ffn_down_dx_{tag}", dy, w2, a, b3)
    dw2 = matmul(f"ffn_dw2_{tag}", s, dy, "tn", tm=1408, tn=256)
    dw1 = matmul(f"ffn_dw1_{tag}", h, da, "tn", tm=1024, tn=256)
    dw3 = matmul(f"ffn_dw3_{tag}", h, db3, "tn", tm=1024, tn=256)
    dh = matmul(f"ffn_up_dx_{tag}", [da, db3], [w1, w3], "nt", tm=1024)
    (dx,), (dshift, dscale), (dgain,) = ew_bwd(f"ffn_norm_bwd_{tag}", f_norm_mod, [x], [shift, scale], [gain], [dh], [F32],
                                               adds=[dxo])
    return dx, (dshift, dscale, dgate), dgain, dw1, dw3, dw2


def _mixer_fwd(tag, x, shift, scale, gate, p):
    h = ew_fwd(f"mix_norm_{tag}", f_norm_mod, [x], [shift, scale], [p["gain"]], [D], [BF16])[0]
    rgx = matmul(f"in_rgx_{tag}", h, p["w_rgx"], "nn")
    rgate = matmul(f"in_gate_{tag}", h, p["w_gate"], "nn")
    sbqkv = matmul(f"in_sb_{tag}", h, p["w_sbqkv"], "nn")
    foxqkv = matmul(f"in_fox_{tag}", h, p["w_foxqkv"], "nn")
    ff = matmul(f"in_forget_{tag}", h, p["w_f"], "nn")
    mg = matmul(f"in_merge_{tag}", h, p["w_merge"], "nn")
    xa = conv_fwd(rgx, p["conv_w8"], p["conv_b"])
    pre_r = matmul(f"rg_a_{tag}", xa, p["wa_bd"], "nn")
    pre_i = matmul(f"rg_x_{tag}", xa, p["wx_bd"], "nn")
    a, u = ew_fwd(f"rg_gates_{tag}", f_rg_gates, [pre_r, pre_i, xa], [], [p["ba"], p["bx"], p["lam"]], [D, D], [F32, F32],
                  tm=EW_ROWS_WIDE)
    hs = scan_fwd(a, u)
    ya = ew_fwd(f"rg_out_{tag}", f_gelu_mul, [rgate, hs], [], [], [D], [BF16])[0]
    yb, sb_tot = sb_attn_fwd(sbqkv)
    lf = ew_fwd(f"fox_logf_{tag}", f_log_sigmoid_bias, [ff], [], [p["bf"]], [LANES], [F32])[0]
    cum = seq_cumsum(f"fox_cum_{tag}", [lf], [1.0], False)
    cum_t = cum.reshape(-1, SEQ, LANES)[:, :, :N_HEADS].transpose(0, 2, 1)
    yc, lse = fox_attn_fwd(foxqkv, cum, cum_t)
    pa = matmul(f"out_rg_{tag}", ya, p["w_rg"], "nn")
    pb = matmul(f"out_sb_{tag}", yb, p["w_sb"], "nn")
    pc = matmul(f"out_fox_{tag}", yc, p["w_fox"], "nn")
    mixed = ew_fwd(f"merge_{tag}", f_merge, [mg, pa, pb, pc], [], [p["merge_b"]], [D], [BF16], tm=EW_ROWS_WIDE)[0]
    y = matmul(f"out_o_{tag}", mixed, p["w_o"], "nn")
    xo = ew_fwd(f"mix_res_{tag}", functools.partial(f_resid, 1.0), [x, y], [gate], [], [D], [F32])[0]
    saved = dict(x=x, h=h, rgx=rgx, rgate=rgate, sbqkv=sbqkv, foxqkv=foxqkv, ff=ff, mg=mg, xa=xa, pre_r=pre_r, pre_i=pre_i,
                 a=a, hs=hs, ya=ya, yb=yb, sb_tot=sb_tot, cum=cum, cum_t=cum_t, yc=yc, lse=lse, pa=pa, pb=pb, pc=pc,
                 mixed=mixed, y=y)
    return xo, saved


def _mixer_bwd(tag, dxo, s, shift, scale, gate, p):
    (dy,), (dgate,), _ = ew_bwd(f"mix_res_bwd_{tag}", functools.partial(f_resid, 1.0), [s["x"], s["y"]], [gate], [], [dxo],
                                [None, BF16])
    dmixed = matmul(f"out_o_dx_{tag}", dy, p["w_o"], "nt")
    g = {"w_o": matmul(f"out_o_dw_{tag}", s["mixed"], dy, "tn", tm=1024, tn=256)}
    (dmg, dpa, dpb, dpc), _, (g["merge_b"],) = ew_bwd(
        f"merge_bwd_{tag}", f_merge, [s["mg"], s["pa"], s["pb"], s["pc"]], [], [p["merge_b"]], [dmixed], [BF16] * 4,
        tm=EW_ROWS_WIDE)
    dya = matmul(f"out_rg_dx_{tag}", dpa, p["w_rg"], "nt")
    g["w_rg"] = matmul(f"out_rg_dw_{tag}", s["ya"], dpa, "tn", tm=1024, tn=256)
    dyb = matmul(f"out_sb_dx_{tag}", dpb, p["w_sb"], "nt", out_dtype=BF16)
    g["w_sb"] = matmul(f"out_sb_dw_{tag}", s["yb"], dpb, "tn", tm=1024, tn=256)
    dyc = matmul(f"out_fox_dx_{tag}", dpc, p["w_fox"], "nt", out_dtype=BF16)
    g["w_fox"] = matmul(f"out_fox_dw_{tag}", s["yc"], dpc, "tn", tm=1024, tn=256)
    dq_c, dk_c, dv_c, dcq, dck = fox_attn_bwd(s["foxqkv"], s["cum"], s["cum_t"], s["lse"], s["yc"], dyc)
    dck_rows = _pad_lanes(dck.transpose(0, 2, 1).reshape(-1, N_HEADS))
    dlf = seq_cumsum(f"fox_cum_bwd_{tag}", [dcq, dck_rows], [1.0, -1.0], True)
    (dff,), _, (dbf,) = ew_bwd(f"fox_logf_bwd_{tag}", f_log_sigmoid_bias, [s["ff"]], [], [p["bf"]], [dlf], [BF16])
    g["fox_bf"] = dbf[0, :N_HEADS]
    dq_b, dk_b, dv_b = sb_attn_bwd(s["sbqkv"], s["sb_tot"], dyb)
    (drgate, dhs), _, _ = ew_bwd(f"rg_out_bwd_{tag}", f_gelu_mul, [s["rgate"], s["hs"]], [], [], [dya], [BF16, F32],
                                 tm=EW_ROWS_WIDE)
    da, du = scan_bwd(s["a"], s["hs"], dhs)
    (dpre_r, dpre_i, dxa1), _, (g["rg_ba"], g["rg_bx"], g["rg_lam"]) = ew_bwd(
        f"rg_gates_bwd_{tag}", f_rg_gates, [s["pre_r"], s["pre_i"], s["xa"]], [], [p["ba"], p["bx"], p["lam"]], [da, du],
        [BF16, BF16, F32], tm=EW_ROWS_WIDE)
    dxa2 = matmul(f"rg_dx_{tag}", [dpre_r, dpre_i], [p["wa_bd"], p["wx_bd"]], "nt")
    g["rg_wa"] = _diag_blocks(matmul(f"rg_a_dw_{tag}", s["xa"], dpre_r, "tn", tm=512, tn=256))
    g["rg_wx"] = _diag_blocks(matmul(f"rg_x_dw_{tag}", s["xa"], dpre_i, "tn", tm=512, tn=256))
    drgx, dwb = conv_bwd(s["rgx"], p["conv_w8"], dxa1, dxa2)
    g["conv_w"] = dwb[:CONV_K]
    g["conv_b"] = dwb[CONV_K]
    cots = [drgx, drgate, dq_b, dk_b, dv_b, dq_c, dk_c, dv_c, dff, dmg]
    w_sb3 = [p["w_sbqkv"][:, k * ATT_W:(k + 1) * ATT_W] for k in range(3)]
    w_fox3 = [p["w_foxqkv"][:, k * ATT_W:(k + 1) * ATT_W] for k in range(3)]
    ws = [p["w_rgx"], p["w_gate"]] + w_sb3 + w_fox3 + [p["w_f"], p["w_merge"]]
    dh = matmul(f"in_dx_{tag}", cots, ws, "nt", tm=512)
    dws = [matmul(f"in_dw{k}_{tag}", s["h"], ct, "tn", tm=1024, tn=256) for k, ct in enumerate(cots)]
    dws[8] = dws[8][:, :N_HEADS]
    g["w_in"] = jnp.concatenate(dws, axis=1)
    (dx,), (dshift, dscale), (g["mix_norm"],) = ew_bwd(f"mix_norm_bwd_{tag}", f_norm_mod, [s["x"]], [shift, scale], [p["gain"]],
                                                       [dh], [F32], adds=[dxo])
    return dx, (dshift, dscale, dgate), g


def _final_loss(x, target, shift, scale, gain):
    n = x.shape[0]
    tm = min(EW_ROWS, SEQ)
    tpb = SEQ // tm

    def body(x_ref, t_ref, sh_ref, sc_ref, g_ref, loss_ref, dx_ref, dsh_ref, dsc_ref, dg_ref):
        i = pl.program_id(0)
        out, vjp = jax.vjp(f_norm_mod, x_ref[...], sh_ref[...], sc_ref[...], g_ref[...])
        diff = out - t_ref[...]
        dx, dsh, dsc, dg = vjp(diff * (1.0 / D))
        dx_ref[...] = dx
        sq = jnp.sum(jnp.sum(diff * diff, axis=1, keepdims=True), axis=0, keepdims=True)

        @pl.when(i % tpb == 0)
        def _():
            dsh_ref[...] = jnp.zeros_like(dsh_ref)
            dsc_ref[...] = jnp.zeros_like(dsc_ref)

        @pl.when(i == 0)
        def _():
            dg_ref[...] = jnp.zeros_like(dg_ref)
            loss_ref[...] = jnp.zeros_like(loss_ref)

        dsh_ref[...] += dsh
        dsc_ref[...] += dsc
        dg_ref[...] += dg
        loss_ref[...] += jnp.broadcast_to(sq, (1, LANES)) * (0.5 / D)

    row, bp, gp = _row_spec(D, tm), _bparam_spec(D, tpb), _gparam_spec((1, D))
    return pl.pallas_call(
        body, name="final_loss", grid=(n // tm,), in_specs=[row, row, bp, bp, gp],
        out_specs=[_gparam_spec((1, LANES)), row, bp, bp, gp],
        out_shape=[jax.ShapeDtypeStruct((1, LANES), F32), jax.ShapeDtypeStruct((n, D), F32),
                   jax.ShapeDtypeStruct(shift.shape, F32), jax.ShapeDtypeStruct(scale.shape, F32),
                   jax.ShapeDtypeStruct((1, D), F32)],
        compiler_params=pltpu.CompilerParams(dimension_semantics=("arbitrary",)),
    )(x, target, shift, scale, gain)


def kernel(x, c, ffn1_norm, ffn1_w1, ffn1_w3, ffn1_w2, mix_norm, w_in, conv_w, conv_b, rg_wa, rg_ba, rg_wx, rg_bx, rg_lam, fox_bf, merge_b, w_rg, w_sb, w_fox, w_o, ffn2_norm, ffn2_w1, ffn2_w3, ffn2_w2, ada_w, ada_b, final_norm, final_ada_w, final_ada_b, loss_target, m_ffn1_norm, m_ffn1_w1, m_ffn1_w3, m_ffn1_w2, m_mix_norm, m_w_in, m_conv_w, m_conv_b, m_rg_wa, m_rg_ba, m_rg_wx, m_rg_bx, m_rg_lam, m_fox_bf, m_merge_b, m_w_rg, m_w_sb, m_w_fox, m_w_o, m_ffn2_norm, m_ffn2_w1, m_ffn2_w3, m_ffn2_w2, m_ada_w, m_ada_b, m_final_norm, m_final_ada_w, m_final_ada_b, v_ffn1_norm, v_ffn1_w1, v_ffn1_w3, v_ffn1_w2, v_mix_norm, v_w_in, v_conv_w, v_conv_b, v_rg_wa, v_rg_ba, v_rg_wx, v_rg_bx, v_rg_lam, v_fox_bf, v_merge_b, v_w_rg, v_w_sb, v_w_fox, v_w_o, v_ffn2_norm, v_ffn2_w1, v_ffn2_w3, v_ffn2_w2, v_ada_w, v_ada_b, v_final_norm, v_final_ada_w, v_final_ada_b):
    given = dict(zip(["x", "c"] + WEIGHTS + ["loss_target"] + ["m_" + n for n in WEIGHTS] + ["v_" + n for n in WEIGHTS],
                     (x, c, ffn1_norm, ffn1_w1, ffn1_w3, ffn1_w2, mix_norm, w_in, conv_w, conv_b, rg_wa, rg_ba, rg_wx, rg_bx, rg_lam, fox_bf, merge_b, w_rg, w_sb, w_fox, w_o, ffn2_norm, ffn2_w1, ffn2_w3, ffn2_w2, ada_w, ada_b, final_norm, final_ada_w, final_ada_b, loss_target, m_ffn1_norm, m_ffn1_w1, m_ffn1_w3, m_ffn1_w2, m_mix_norm, m_w_in, m_conv_w, m_conv_b, m_rg_wa, m_rg_ba, m_rg_wx, m_rg_bx, m_rg_lam, m_fox_bf, m_merge_b, m_w_rg, m_w_sb, m_w_fox, m_w_o, m_ffn2_norm, m_ffn2_w1, m_ffn2_w3, m_ffn2_w2, m_ada_w, m_ada_b, m_final_norm, m_final_ada_w, m_final_ada_b, v_ffn1_norm, v_ffn1_w1, v_ffn1_w3, v_ffn1_w2, v_mix_norm, v_w_in, v_conv_w, v_conv_b, v_rg_wa, v_rg_ba, v_rg_wx, v_rg_bx, v_rg_lam, v_fox_bf, v_merge_b, v_w_rg, v_w_sb, v_w_fox, v_w_o, v_ffn2_norm, v_ffn2_w1, v_ffn2_w3, v_ffn2_w2, v_ada_w, v_ada_b, v_final_norm, v_final_ada_w, v_final_ada_b)))
    idx = my_index()
    n_batch = N_DEV * B_LOC
    ada_cols = ada_w.shape[2]
    fin_cols = final_ada_w.shape[1]

    small_in, small_in_meta = _pack([c, conv_w], LANES, 8, F32)
    c_parts, conv_w_parts = _unpack(all_gather("gather_c_conv", small_in), small_in_meta, lead=1)
    c_all = c_parts.reshape(n_batch, D)
    conv_w_all = _unshard(conv_w_parts, 2)
    c_act = ew_fwd("c_silu", _f_silu, [c_all], [], [], [D], [F32])[0]
    mod_cols = [matmul(f"ada_proj_{l}", c_act, ada_w[l], "nn") for l in range(DEPTH)]
    mod_cols.append(matmul("ada_proj_final", c_act, final_ada_w, "nn"))
    mod_g = all_gather("gather_mod", jnp.concatenate(mod_cols, axis=1))

    shards = {(l, group): [given[n][l].astype(BF16) for n in members] for l in range(DEPTH) for group, members in GROUPS}
    shards, mod_g = lax.optimization_barrier((shards, mod_g))
    gather_handles = {}
    first = (0, GROUPS[0][0])
    gather_handles[first], token = gather_start(f"gather_start_{first[1]}{first[0]}", shards[first])
    mod_g = mod_g + token[0, 0]
    first_blocks = gather_finish(f"gather_finish_{first[1]}{first[0]}", gather_handles[first], mod_g)
    shards, first_blocks = lax.optimization_barrier((shards, first_blocks))
    started = jnp.zeros((), F32)
    for l in range(DEPTH):
        for group, _ in GROUPS:
            if (l, group) != first:
                gather_handles[l, group], token = gather_start(f"gather_start_{group}{l}", shards[l, group])
                started = started + token[0, 0]
    mod_g = mod_g + started

    def weights_of(l, group, after):
        members = dict(GROUPS)[group]
        blocks = first_blocks if (l, group) == first else gather_finish(f"gather_finish_{group}{l}", gather_handles[l, group], after)
        return {n: _unshard(b, GATHERED[n] - 1) for n, b in zip(members, blocks)}

    mods = []
    for l in range(DEPTH):
        full = mod_g[:, :, l * ada_cols:(l + 1) * ada_cols].transpose(1, 0, 2).reshape(n_batch, N_DEV * ada_cols)
        full = ew_fwd(f"ada_bias_{l}", _f_add_bias, [full], [], [ada_b[l][None]], [full.shape[1]], [F32])[0]
        mods.append(lax.dynamic_slice_in_dim(full, idx * B_LOC, B_LOC, axis=0).reshape(B_LOC, 3, 3, D))
    fm = mod_g[:, :, DEPTH * ada_cols:].transpose(1, 0, 2).reshape(n_batch, N_DEV * fin_cols)
    fm = ew_fwd("ada_bias_final", _f_add_bias, [fm], [], [final_ada_b[None]], [fm.shape[1]], [F32])[0]
    fm = lax.dynamic_slice_in_dim(fm, idx * B_LOC, B_LOC, axis=0).reshape(B_LOC, 2, D)

    def mixer_params(l, w):
        wi = w["w_in"]
        cut = IN_CUTS
        return dict(
            gain=mix_norm[l][None], w_rgx=wi[:, cut[0]:cut[1]], w_gate=wi[:, cut[1]:cut[2]], w_sbqkv=wi[:, cut[2]:cut[3]],
            w_foxqkv=wi[:, cut[3]:cut[4]], w_f=_pad_lanes(wi[:, cut[4]:cut[5]]), w_merge=wi[:, cut[5]:cut[6]],
            conv_w8=jnp.pad(conv_w_all[l], ((0, 8 - CONV_K), (0, 0))), conv_b=conv_b[l][None],
            wa_bd=_block_diag(rg_wa[l]), wx_bd=_block_diag(rg_wx[l]), ba=rg_ba[l][None], bx=rg_bx[l][None], lam=rg_lam[l][None],
            bf=_pad_lanes(fox_bf[l][None]), merge_b=merge_b[l][None], w_rg=w["w_rg"], w_sb=w["w_sb"], w_fox=w["w_fox"],
            w_o=w["w_o"])

    n_tok = x.shape[0] * x.shape[1]
    h = x.reshape(n_tok, D)
    saved = []
    for l in range(DEPTH):
        m = mods[l]
        w1 = weights_of(l, "ffn1", m if l == 0 else h)
        h, s1 = _ffn_fwd(f"a{l}", h, _bp(m, 0, 0), _bp(m, 0, 1), _bp(m, 0, 2), ffn1_norm[l][None], w1["ffn1_w1"], w1["ffn1_w3"],
                         w1["ffn1_w2"])
        p = mixer_params(l, weights_of(l, "mix", h))
        h, s2 = _mixer_fwd(f"{l}", h, _bp(m, 1, 0), _bp(m, 1, 1), _bp(m, 1, 2), p)
        w3 = weights_of(l, "ffn2", h)
        h, s3 = _ffn_fwd(f"b{l}", h, _bp(m, 2, 0), _bp(m, 2, 1), _bp(m, 2, 2), ffn2_norm[l][None], w3["ffn2_w1"], w3["ffn2_w3"],
                         w3["ffn2_w2"])
        saved.append((s1, s2, s3, p, w1, w3))
    loss_row, dh, dfshift, dfscale, dgain_final = _final_loss(h, loss_target.reshape(n_tok, D), fm[:, 0][:, None, :],
                                                              fm[:, 1][:, None, :], final_norm[None])

    grads = {n: [None] * DEPTH for n in WEIGHTS}
    d_mods = [None] * DEPTH
    scatter_handles = {}
    after_start = jnp.zeros((), F32)

    def scatter_blocks(l, group):
        return [_reshard(grads[n][l], GATHERED[n] - 1).astype(BF16) for n in dict(GROUPS)[group]]

    def start_scatter(l, group, g8s=None):
        g8s = scatter_blocks(l, group) if g8s is None else g8s
        scatter_handles[l, group], token = scatter_start(f"scatter_start_{group}{l}", g8s)
        return token[0, 0]

    for l in reversed(range(DEPTH)):
        m = mods[l]
        s1, s2, s3, p, w1, w3 = saved[l]
        dh, dm3, grads["ffn2_norm"][l], grads["ffn2_w1"][l], grads["ffn2_w3"][l], grads["ffn2_w2"][l] = _ffn_bwd(
            f"b{l}", dh, s3, _bp(m, 2, 0), _bp(m, 2, 1), _bp(m, 2, 2) + after_start, ffn2_norm[l][None], w3["ffn2_w1"],
            w3["ffn2_w3"], w3["ffn2_w2"])
        after_start = start_scatter(l, "ffn2")
        dh, dm2, gm = _mixer_bwd(f"{l}", dh, s2, _bp(m, 1, 0), _bp(m, 1, 1), _bp(m, 1, 2) + after_start, p)
        for n, gval in gm.items():
            grads[n][l] = gval
        after_start = start_scatter(l, "mix")
        dh, dm1, grads["ffn1_norm"][l], grads["ffn1_w1"][l], grads["ffn1_w3"][l], grads["ffn1_w2"][l] = _ffn_bwd(
            f"a{l}", dh, s1, _bp(m, 0, 0), _bp(m, 0, 1), _bp(m, 0, 2) + after_start, ffn1_norm[l][None], w1["ffn1_w1"],
            w1["ffn1_w3"], w1["ffn1_w2"])
        if l > 0:
            after_start = start_scatter(l, "ffn1")
        d_mods[l] = jnp.concatenate([t.reshape(B_LOC, D) for dm in (dm1, dm2, dm3) for t in dm], axis=1)
    grad_x = dh.reshape(x.shape)
    d_fm = jnp.concatenate([dfshift.reshape(B_LOC, D), dfscale.reshape(B_LOC, D)], axis=1)

    rep = {n: jnp.stack([t.reshape(given[n].shape[1:]) for t in grads[n]]) for n in REPLICATED if n != "final_norm"}
    rep["final_norm"] = dgain_final.reshape(D)
    rep["conv_w"] = jnp.stack(grads["conv_w"])
    rep_names = list(rep)
    rep_slab, rep_meta = _pack([rep[n] for n in rep_names], LANES, 8, F32)
    mod_slab, mod_meta = _pack(d_mods + [d_fm], LANES, 8, F32)
    small_g = all_gather("gather_small_grads", jnp.concatenate([mod_slab, rep_slab], axis=0))
    last_blocks, small_g = lax.optimization_barrier((scatter_blocks(0, "ffn1"), small_g))
    small_g = small_g + start_scatter(0, "ffn1", last_blocks)
    d_mod_all = [t.reshape(n_batch, -1) for t in _unpack(small_g[:, :mod_slab.shape[0]], mod_meta, lead=1)]
    rep_sum = add_blocks("sum_small_grads", [small_g[k, mod_slab.shape[0]:] for k in range(N_DEV)], F32)
    rep_grad = dict(zip(rep_names, _unpack(rep_sum, rep_meta)))
    final_g = {n: rep_grad[n] for n in REPLICATED}
    final_g["conv_w"] = lax.dynamic_slice_in_dim(rep_grad["conv_w"], idx * conv_w.shape[2], conv_w.shape[2], axis=2)
    final_g["ada_b"] = jnp.stack([sum_rows(f"ada_b_grad_{l}", d_mod_all[l])[0] for l in range(DEPTH)])
    final_g["final_ada_b"] = sum_rows("final_ada_b_grad", d_mod_all[DEPTH])[0]
    final_g["ada_w"] = jnp.stack([
        matmul(f"ada_w_grad_{l}", c_act, lax.dynamic_slice_in_dim(d_mod_all[l], idx * ada_cols, ada_cols, axis=1), "tn")
        for l in range(DEPTH)])
    final_g["final_ada_w"] = matmul(
        "final_ada_w_grad", c_act, lax.dynamic_slice_in_dim(d_mod_all[DEPTH], idx * fin_cols, fin_cols, axis=1), "tn")

    shard_g = {n: [None] * DEPTH for n in GATHERED}

    def finish_scatter(l, group, after):
        sums = scatter_finish(f"scatter_finish_{group}{l}", scatter_handles[l, group], after)
        for n, gval in zip(dict(GROUPS)[group], sums):
            shard_g[n][l] = gval

    for l in reversed(range(DEPTH)):
        for group in ("ffn2", "mix", "ffn1"):
            if (l, group) != (0, "ffn1"):
                finish_scatter(l, group, rep_sum)

    delta, new_m, new_v = {}, {}, {}
    last = dict(GROUPS)["ffn1"]
    sharded = [n for n in GATHERED if n not in last] + ["ada_w", "final_ada_w", "conv_w"] + list(last)
    for n in sharded:
        if n == last[0]:
            finish_scatter(0, "ffn1", delta["w_in"])
        if n in GATHERED:
            final_g[n] = jnp.stack(shard_g[n])
        delta[n], new_m[n], new_v[n] = adamw(f"adamw_{n}", given[n], final_g[n], given["m_" + n], given["v_" + n])
    rep_all = [n for n in WEIGHTS if n not in sharded]
    packed = [_pack([src[n] for n in rep_all], LANES, 8, F32)[0]
              for src in (given, final_g, {n: given["m_" + n] for n in rep_all}, {n: given["v_" + n] for n in rep_all})]
    rep_meta_all = _pack([given[n] for n in rep_all], LANES, 8, F32)[1]
    for store, slab_out in zip((delta, new_m, new_v), adamw("adamw_replicated", *packed)):
        store.update(zip(rep_all, _unpack(slab_out, rep_meta_all)))

    loss = lax.psum(loss_row[0, 0], ("x", "y", "c"))
    return (loss, grad_x, *[final_g[n] for n in WEIGHTS], *[delta[n] for n in WEIGHTS], *[new_m[n] for n in WEIGHTS],
            *[new_v[n] for n in WEIGHTS])
```

```python
import functools

import numpy as np
import jax
import jax.numpy as jnp
from jax import lax
from jax.experimental import pallas as pl
from jax.experimental.pallas import tpu as pltpu

F32 = jnp.float32
BF16 = jnp.bfloat16
MESH = pl.DeviceIdType.MESH

N_DEV = 8
D = 1024
SEQ = 2048
B_LOC = 2
N_TOK = B_LOC * SEQ
DEPTH = 2
D_FF = 2816
RG_BLOCKS = 16
RG_C = 8.0
N_HEADS = 8
HEAD_DIM = 64
ATT_W = N_HEADS * HEAD_DIM
LANES = 128
EPS = 1e-6
ATT_SCALE = HEAD_DIM ** -0.5
CONV_K = 4

ADAM_LR = 0.001
ADAM_B1 = 0.9
ADAM_B2 = 0.999
ADAM_EPS = 1e-08
ADAM_WD = 0.01
ADAM_STEP = 10

EW_ROWS = 512
EW_ROWS_WIDE = 256
ATT_BLK = 256


def _pick_tile(dim, target):
    best = None
    for t in range(LANES, min(dim, target) + 1, LANES):
        if dim % t == 0:
            best = t
    return best if best is not None else dim


_DIMS = {"nn": (((1,), (0,)), ((), ())), "nt": (((1,), (1,)), ((), ())), "tn": (((0,), (0,)), ((), ()))}


def matmul(name, a_list, b_list, mode, out_dtype=F32, tm=1024, tn=512):
    if not isinstance(a_list, (list, tuple)):
        a_list, b_list = [a_list], [b_list]
    n = len(a_list)
    m_dim = a_list[0].shape[1] if mode == "tn" else a_list[0].shape[0]
    n_dim = b_list[0].shape[0] if mode == "nt" else b_list[0].shape[1]
    tm, tn = _pick_tile(m_dim, tm), _pick_tile(n_dim, tn)
    dims = _DIMS[mode]

    def body(*refs):
        o_ref = refs[-1]
        acc = None
        for a_ref, b_ref in zip(refs[:n], refs[n:2 * n]):
            d = lax.dot_general(a_ref[...].astype(BF16), b_ref[...].astype(BF16), dims, preferred_element_type=F32)
            acc = d if acc is None else acc + d
        o_ref[...] = acc.astype(o_ref.dtype)

    in_specs = []
    for a in a_list:
        if mode == "tn":
            in_specs.append(pl.BlockSpec((a.shape[0], tm), lambda i, j: (0, i)))
        else:
            in_specs.append(pl.BlockSpec((tm, a.shape[1]), lambda i, j: (i, 0)))
    for b in b_list:
        if mode == "nt":
            in_specs.append(pl.BlockSpec((tn, b.shape[1]), lambda i, j: (j, 0)))
        else:
            in_specs.append(pl.BlockSpec((b.shape[0], tn), lambda i, j: (0, j)))
    return pl.pallas_call(
        body, name=name, grid=(m_dim // tm, n_dim // tn), in_specs=in_specs,
        out_specs=pl.BlockSpec((tm, tn), lambda i, j: (i, j)),
        out_shape=jax.ShapeDtypeStruct((m_dim, n_dim), out_dtype),
        compiler_params=pltpu.CompilerParams(dimension_semantics=("parallel", "parallel")),
    )(*a_list, *b_list)


def _row_spec(w, tm):
    return pl.BlockSpec((tm, w), lambda i: (i, 0))


def _bparam_spec(w, tiles_per_batch):
    return pl.BlockSpec((None, 1, w), lambda i: (i // tiles_per_batch, 0, 0))


def _gparam_spec(shape):
    return pl.BlockSpec(shape, lambda i: (0, 0))


def ew_fwd(name, fn, rows, bparams, gparams, out_widths, out_dtypes, tm=EW_ROWS):
    n_rows = rows[0].shape[0]
    tm = min(tm, n_rows, SEQ)
    tpb = max(SEQ // tm, 1)
    nr, nb, ng = len(rows), len(bparams), len(gparams)

    def body(*refs):
        vals = [r[...] for r in refs[:nr + nb + ng]]
        outs = fn(*vals)
        if not isinstance(outs, (tuple, list)):
            outs = (outs,)
        for o_ref, o in zip(refs[nr + nb + ng:], outs):
            o_ref[...] = o.astype(o_ref.dtype)

    in_specs = ([_row_spec(r.shape[1], tm) for r in rows] + [_bparam_spec(p.shape[2], tpb) for p in bparams]
                + [_gparam_spec(g.shape) for g in gparams])
    outs = pl.pallas_call(
        body, name=name, grid=(n_rows // tm,), in_specs=in_specs,
        out_specs=[_row_spec(w, tm) for w in out_widths],
        out_shape=[jax.ShapeDtypeStruct((n_rows, w), dt) for w, dt in zip(out_widths, out_dtypes)],
        compiler_params=pltpu.CompilerParams(dimension_semantics=("parallel",)),
    )(*rows, *bparams, *gparams)
    return outs


def ew_bwd(name, fn, rows, bparams, gparams, cts, row_grad_dtypes, adds=(), tm=EW_ROWS):
    n_rows = rows[0].shape[0]
    tm = min(tm, n_rows, SEQ)
    tpb = max(SEQ // tm, 1)
    nr, nb, ng, nc = len(rows), len(bparams), len(gparams), len(cts)
    adds = list(adds) + [None] * (nr - len(adds))
    add_idx = [k for k in range(nr) if adds[k] is not None]
    want = [k for k in range(nr) if row_grad_dtypes[k] is not None]

    def body(*refs):
        pos = nr + nb + ng
        vals = [r[...] for r in refs[:pos]]
        ct_vals = [r[...].astype(F32) for r in refs[pos:pos + nc]]
        pos += nc
        add_vals = {k: refs[pos + q][...] for q, k in enumerate(add_idx)}
        pos += len(add_idx)
        out_refs = refs[pos:]
        f32_vals = [v.astype(F32) for v in vals]
        outs, vjp = jax.vjp(lambda *a: fn(*a), *f32_vals)
        single = not isinstance(outs, (tuple, list))
        grads = vjp(ct_vals[0].astype(outs.dtype) if single else tuple(c.astype(o.dtype) for c, o in zip(ct_vals, outs)))
        i = pl.program_id(0)
        q = 0
        for k in want:
            g = grads[k]
            if k in add_vals:
                g = g + add_vals[k].astype(F32)
            out_refs[q][...] = g.astype(out_refs[q].dtype)
            q += 1
        for k in range(nb):
            ref = out_refs[q]
            q += 1

            @pl.when(i % tpb == 0)
            def _():
                ref[...] = jnp.zeros_like(ref)

            ref[...] += grads[nr + k]
        for k in range(ng):
            ref = out_refs[q]
            q += 1

            @pl.when(i == 0)
            def _():
                ref[...] = jnp.zeros_like(ref)

            ref[...] += grads[nr + nb + k]

    in_specs = ([_row_spec(r.shape[1], tm) for r in rows] + [_bparam_spec(p.shape[2], tpb) for p in bparams]
                + [_gparam_spec(g.shape) for g in gparams] + [_row_spec(c.shape[1], tm) for c in cts]
                + [_row_spec(adds[k].shape[1], tm) for k in add_idx])
    out_specs = ([_row_spec(rows[k].shape[1], tm) for k in want] + [_bparam_spec(p.shape[2], tpb) for p in bparams]
                 + [_gparam_spec(g.shape) for g in gparams])
    out_shape = ([jax.ShapeDtypeStruct(rows[k].shape, row_grad_dtypes[k]) for k in want]
                 + [jax.ShapeDtypeStruct(p.shape, F32) for p in bparams] + [jax.ShapeDtypeStruct(g.shape, F32) for g in gparams])
    outs = pl.pallas_call(
        body, name=name, grid=(n_rows // tm,), in_specs=in_specs, out_specs=out_specs, out_shape=out_shape,
        compiler_params=pltpu.CompilerParams(dimension_semantics=("arbitrary",)),
    )(*rows, *bparams, *gparams, *cts, *[adds[k] for k in add_idx])
    d_rows = list(outs[:len(want)])
    d_b = list(outs[len(want):len(want) + nb])
    d_g = list(outs[len(want) + nb:])
    return d_rows, d_b, d_g


def f_norm_mod(x, shift, scale, gain):
    x = x.astype(F32)
    y = x * lax.rsqrt(jnp.mean(x * x, axis=-1, keepdims=True) + EPS)
    return (y * gain) * (1.0 + scale) + shift


def f_swiglu(a, b3):
    a = a.astype(F32)
    return (a * jax.nn.sigmoid(a)) * b3.astype(F32)


def f_resid(coef, x, y, gate):
    return x.astype(F32) + (coef * (1.0 + gate)) * y.astype(F32)


def f_rg_gates(pre_r, pre_i, xa, ba, bx, lam):
    r = jax.nn.sigmoid(pre_r + ba)
    i = jax.nn.sigmoid(pre_i + bx)
    softplus_neg_lam = jnp.maximum(-lam, 0.0) + jnp.log(1.0 + jnp.exp(-jnp.abs(lam)))
    log_a = (-RG_C) * r * softplus_neg_lam
    a = jnp.exp(log_a)
    u = jnp.sqrt(1.0 - a * a) * (i * xa)
    return a, u


def f_gelu_mul(gate, hs):
    g = gate.astype(F32)
    gelu = 0.5 * g * (1.0 + jnp.tanh(0.7978845608028654 * (g + 0.044715 * g * g * g)))
    return gelu * hs.astype(F32)


def f_log_sigmoid_bias(f, bf):
    z = f.astype(F32) + bf
    return jnp.minimum(z, 0.0) - jnp.log(1.0 + jnp.exp(-jnp.abs(z)))


def f_merge(mg, pa, pb, pc, merge_b):
    g = jax.nn.sigmoid(mg.astype(F32) + merge_b)
    return g[:, :D] * pa.astype(F32) + g[:, D:2 * D] * pb.astype(F32) + g[:, 2 * D:] * pc.astype(F32)


CONV_CB = 256
SCAN_CB = 512
CUM_RB = 512


def _shift_down(x, d):
    if d == 0:
        return x
    rows = lax.broadcasted_iota(jnp.int32, x.shape, 0)
    return jnp.where(rows >= d, pltpu.roll(x, d, axis=0), 0.0)


def _shift_up(x, d):
    if d == 0:
        return x
    s = x.shape[0]
    rows = lax.broadcasted_iota(jnp.int32, x.shape, 0)
    return jnp.where(rows < s - d, pltpu.roll(x, s - d, axis=0), 0.0)


def conv_fwd(x, w8, b):
    n, c = x.shape
    nb = n // SEQ

    def body(x_ref, w_ref, b_ref, y_ref):
        xv = x_ref[...]
        acc = jnp.broadcast_to(b_ref[...], xv.shape)
        for k in range(CONV_K):
            acc = acc + w_ref[k:k + 1, :] * _shift_down(xv, CONV_K - 1 - k)
        y_ref[...] = acc

    return pl.pallas_call(
        body, name="conv_fwd", grid=(c // CONV_CB, nb),
        in_specs=[pl.BlockSpec((SEQ, CONV_CB), lambda j, i: (i, j)), pl.BlockSpec((8, CONV_CB), lambda j, i: (0, j)),
                  pl.BlockSpec((1, CONV_CB), lambda j, i: (0, j))],
        out_specs=pl.BlockSpec((SEQ, CONV_CB), lambda j, i: (i, j)),
        out_shape=jax.ShapeDtypeStruct((n, c), F32),
        compiler_params=pltpu.CompilerParams(dimension_semantics=("parallel", "parallel")),
    )(x, w8, b)


def conv_bwd(x, w8, dy1, dy2):
    n, c = x.shape
    nb = n // SEQ

    def body(x_ref, w_ref, dy1_ref, dy2_ref, dx_ref, dwb_ref):
        xv = x_ref[...]
        dy = dy1_ref[...] + dy2_ref[...]
        dx = jnp.zeros_like(xv)
        parts = []
        for k in range(CONV_K):
            d = CONV_K - 1 - k
            dx = dx + w_ref[k:k + 1, :] * _shift_up(dy, d)
            parts.append(jnp.sum(dy * _shift_down(xv, d), axis=0, keepdims=True))
        parts.append(jnp.sum(dy, axis=0, keepdims=True))
        parts.append(jnp.zeros((8 - len(parts), xv.shape[1]), F32))
        dx_ref[...] = dx.astype(BF16)

        @pl.when(pl.program_id(1) == 0)
        def _():
            dwb_ref[...] = jnp.zeros_like(dwb_ref)

        dwb_ref[...] += jnp.concatenate(parts, axis=0)

    return pl.pallas_call(
        body, name="conv_bwd", grid=(c // CONV_CB, nb),
        in_specs=[pl.BlockSpec((SEQ, CONV_CB), lambda j, i: (i, j)), pl.BlockSpec((8, CONV_CB), lambda j, i: (0, j)),
                  pl.BlockSpec((SEQ, CONV_CB), lambda j, i: (i, j)), pl.BlockSpec((SEQ, CONV_CB), lambda j, i: (i, j))],
        out_specs=[pl.BlockSpec((SEQ, CONV_CB), lambda j, i: (i, j)), pl.BlockSpec((8, CONV_CB), lambda j, i: (0, j))],
        out_shape=[jax.ShapeDtypeStruct((n, c), BF16), jax.ShapeDtypeStruct((8, c), F32)],
        compiler_params=pltpu.CompilerParams(dimension_semantics=("parallel", "arbitrary")),
    )(x, w8, dy1, dy2)


def scan_fwd(a, u):
    n, c = a.shape

    def body(a_ref, u_ref, h_ref):
        def step(t, h):
            h = a_ref[pl.ds(t, 1), :] * h + u_ref[pl.ds(t, 1), :]
            h_ref[pl.ds(t, 1), :] = h
            return h

        lax.fori_loop(0, SEQ, step, jnp.zeros((1, SCAN_CB), F32), unroll=8)

    spec = pl.BlockSpec((SEQ, SCAN_CB), lambda i, j: (i, j))
    return pl.pallas_call(
        body, name="scan_fwd", grid=(n // SEQ, c // SCAN_CB), in_specs=[spec, spec], out_specs=spec,
        out_shape=jax.ShapeDtypeStruct((n, c), F32),
        compiler_params=pltpu.CompilerParams(dimension_semantics=("parallel", "parallel")),
    )(a, u)


def scan_bwd(a, h, g):
    n, c = a.shape

    def body(a_ref, h_ref, g_ref, da_ref, du_ref):
        def step(k, carry):
            t = SEQ - 1 - k
            dh = g_ref[pl.ds(t, 1), :] + carry
            du_ref[pl.ds(t, 1), :] = dh
            h_prev = jnp.where(t > 0, h_ref[pl.ds(jnp.maximum(t - 1, 0), 1), :], 0.0)
            da_ref[pl.ds(t, 1), :] = dh * h_prev
            return a_ref[pl.ds(t, 1), :] * dh

        lax.fori_loop(0, SEQ, step, jnp.zeros((1, SCAN_CB), F32), unroll=8)

    spec = pl.BlockSpec((SEQ, SCAN_CB), lambda i, j: (i, j))
    return pl.pallas_call(
        body, name="scan_bwd", grid=(n // SEQ, c // SCAN_CB), in_specs=[spec, spec, spec], out_specs=[spec, spec],
        out_shape=[jax.ShapeDtypeStruct((n, c), F32), jax.ShapeDtypeStruct((n, c), F32)],
        compiler_params=pltpu.CompilerParams(dimension_semantics=("parallel", "parallel")),
    )(a, h, g)


def _split3_dot(m, x):
    hi = x.astype(BF16)
    r1 = x - hi.astype(F32)
    mid = r1.astype(BF16)
    lo = (r1 - mid.astype(F32)).astype(BF16)
    dot = functools.partial(jnp.dot, preferred_element_type=F32)
    return dot(m, hi) + dot(m, mid) + dot(m, lo)


def seq_cumsum(name, xs, signs, reverse):
    n, w = xs[0].shape
    nx = len(xs)
    rb = min(CUM_RB, SEQ)

    def body(*refs):
        x = None
        for r, sg in zip(refs[:nx], signs):
            x = sg * r[...] if x is None else x + sg * r[...]
        q0 = pl.program_id(1) * rb
        row = q0 + lax.broadcasted_iota(jnp.int32, (rb, SEQ), 0)
        col = lax.broadcasted_iota(jnp.int32, (rb, SEQ), 1)
        tri = ((col >= row) if reverse else (col <= row)).astype(BF16)
        refs[nx][...] = _split3_dot(tri, x)

    return pl.pallas_call(
        body, name=name, grid=(n // SEQ, SEQ // rb),
        in_specs=[pl.BlockSpec((SEQ, w), lambda i, j: (i, 0)) for _ in xs],
        out_specs=pl.BlockSpec((rb, w), lambda i, j: (i * (SEQ // rb) + j, 0)),
        out_shape=jax.ShapeDtypeStruct((n, w), F32),
        compiler_params=pltpu.CompilerParams(dimension_semantics=("parallel", "parallel")),
    )(*xs)


N_PAIRS = N_HEADS // 2


def _dot_nt(a, b):
    return lax.dot_general(a, b, _DIMS["nt"], preferred_element_type=F32)


def _dot_tn(a, b):
    return lax.dot_general(a, b, _DIMS["tn"], preferred_element_type=F32)


def _dot_nn(a, b):
    return lax.dot_general(a, b, _DIMS["nn"], preferred_element_type=F32)


def _split2_dot(x, m):
    hi = x.astype(BF16)
    lo = (x - hi.astype(F32)).astype(BF16)
    return _dot_nn(hi, m) + _dot_nn(lo, m)


def _head_mask(j):
    lane = lax.broadcasted_iota(jnp.int32, (1, LANES), 1)
    return (lane // HEAD_DIM) == j


def _lane_pick(x, h):
    lane = lax.broadcasted_iota(jnp.int32, x.shape, 1)
    return jnp.sum(jnp.where(lane == h, x, 0.0), axis=1, keepdims=True)


def _lane_put(col, h):
    lane = lax.broadcasted_iota(jnp.int32, (col.shape[0], LANES), 1)
    return jnp.where(lane == h, col, 0.0)


def _softplus(z):
    return jnp.maximum(z, 0.0) + jnp.log(1.0 + jnp.exp(-jnp.abs(z)))


def _qkv_specs():
    return [pl.BlockSpec((SEQ, LANES), lambda b, p: (b, p)),
            pl.BlockSpec((SEQ, LANES), lambda b, p: (b, N_PAIRS + p)),
            pl.BlockSpec((SEQ, LANES), lambda b, p: (b, 2 * N_PAIRS + p))]


def _pair_spec():
    return pl.BlockSpec((SEQ, LANES), lambda b, p: (b, p))


def _below_diagonal(strictly):
    t = ATT_BLK
    row = lax.broadcasted_iota(jnp.int32, (t, t), 0)
    col = lax.broadcasted_iota(jnp.int32, (t, t), 1)
    return (row > col) if strictly else (row >= col)


def _over_key_blocks(qi, step, init, reverse):
    t = ATT_BLK
    q0 = pl.multiple_of(qi * t, t)

    def off_diagonal(kk, carry):
        ki = (qi - 1 - kk) if reverse else kk
        return step(pl.multiple_of(ki * t, t), carry, False)

    if reverse:
        return lax.fori_loop(0, qi, off_diagonal, step(q0, init, True))
    return step(q0, lax.fori_loop(0, qi, off_diagonal, init), True)


def _masked_q(qb, j):
    return (jnp.where(_head_mask(j), qb, 0.0) * ATT_SCALE).astype(BF16)


def sb_attn_fwd(qkv):
    n = qkv.shape[0]
    t = ATT_BLK

    def body(q_ref, k_ref, v_ref, o_ref, tot_ref):
        pair = pl.program_id(1)
        strict = _below_diagonal(True)
        later = strict.astype(BF16)

        @pl.when(pair == 0)
        def _():
            tot_ref[...] = jnp.zeros_like(tot_ref)

        def q_block(qi, _):
            q0 = pl.multiple_of(qi * t, t)
            qb = q_ref[pl.ds(q0, t), :]
            qms = [_masked_q(qb, j) for j in range(2)]

            def step(k0, carry, diagonal):
                kb = k_ref[pl.ds(k0, t), :].astype(BF16)
                vb = v_ref[pl.ds(k0, t), :].astype(BF16)
                heads = range(2)
                zs = [_dot_nt(qms[j], kb) for j in heads]
                sps = [_softplus(z) for z in zs]
                log_keeps = [(jnp.where(strict, -sp, 0.0) if diagonal else -sp) for sp in sps]
                right_l = [_split2_dot(lk, later) for lk in log_keeps]
                atts = [jnp.exp((zs[j] - sps[j]) + right_l[j] + carry[j][0]) for j in heads]
                if diagonal:
                    atts = [jnp.where(strict, att, 0.0) for att in atts]
                return tuple((carry[j][0] + jnp.sum(log_keeps[j], axis=1, keepdims=True),
                              carry[j][1] + _dot_nn(atts[j].astype(BF16), vb)) for j in heads)

            init = ((jnp.zeros((t, 1), F32), jnp.zeros((t, LANES), F32)),) * 2
            (tot0, acc0), (tot1, acc1) = _over_key_blocks(qi, step, init, reverse=True)
            o_ref[pl.ds(q0, t), :] = jnp.where(_head_mask(0), acc0, acc1)
            tot_ref[pl.ds(q0, t), :] += _lane_put(tot0, 2 * pair) + _lane_put(tot1, 2 * pair + 1)
            return 0

        lax.fori_loop(0, SEQ // t, q_block, 0)

    batch_spec = pl.BlockSpec((SEQ, LANES), lambda b, p: (b, 0))
    return pl.pallas_call(
        body, name="sb_attn_fwd", grid=(n // SEQ, N_PAIRS), in_specs=_qkv_specs(), out_specs=[_pair_spec(), batch_spec],
        out_shape=[jax.ShapeDtypeStruct((n, ATT_W), F32), jax.ShapeDtypeStruct((n, LANES), F32)],
        compiler_params=pltpu.CompilerParams(dimension_semantics=("parallel", "arbitrary")),
    )(qkv, qkv, qkv)


def sb_attn_bwd(qkv, tot, do):
    n = qkv.shape[0]
    t = ATT_BLK

    def body(q_ref, k_ref, v_ref, tot_ref, do_ref, dq_ref, dk_ref, dv_ref, dk_acc, dv_acc):
        pair = pl.program_id(1)
        strict = _below_diagonal(True)
        upto = jnp.logical_not(strict).astype(BF16)
        dk_acc[...] = jnp.zeros_like(dk_acc)
        dv_acc[...] = jnp.zeros_like(dv_acc)

        def q_block(qi, _):
            q0 = pl.multiple_of(qi * t, t)
            qb = q_ref[pl.ds(q0, t), :]
            tot_q = tot_ref[pl.ds(q0, t), :]
            dob = do_ref[pl.ds(q0, t), :].astype(F32)
            qms = [_masked_q(qb, j) for j in range(2)]
            doms = [jnp.where(_head_mask(j), dob, 0.0).astype(BF16) for j in range(2)]
            totals = [_lane_pick(tot_q, 2 * pair + j) for j in range(2)]

            def step(k0, carry, diagonal):
                kb = k_ref[pl.ds(k0, t), :].astype(BF16)
                vb = v_ref[pl.ds(k0, t), :].astype(BF16)
                heads = range(2)
                zs = [_dot_nt(qms[j], kb) for j in heads]
                d_atts = [_dot_nt(doms[j], vb) for j in heads]
                sps = [_softplus(z) for z in zs]
                log_keeps = [(jnp.where(strict, -sp, 0.0) if diagonal else -sp) for sp in sps]
                log_betas = [z - sp for z, sp in zip(zs, sps)]
                left_l = [_split2_dot(lk, upto) for lk in log_keeps]
                atts = [jnp.exp(log_betas[j] + (totals[j] - (carry[j][0] + left_l[j]))) for j in heads]
                if diagonal:
                    atts = [jnp.where(strict, att, 0.0) for att in atts]
                gs = [att * d_att for att, d_att in zip(atts, d_atts)]
                dv = _dot_tn(atts[0].astype(BF16), doms[0]) + _dot_tn(atts[1].astype(BF16), doms[1])
                left_g = [_dot_nn(g.astype(BF16), upto) for g in gs]
                dzs = [gs[j] - jnp.exp(log_betas[j]) * (carry[j][1] + left_g[j]) for j in heads]
                if diagonal:
                    dzs = [jnp.where(strict, dz, 0.0) for dz in dzs]
                dzs = [dz.astype(BF16) for dz in dzs]
                dk = _dot_tn(dzs[0], qms[0]) + _dot_tn(dzs[1], qms[1])
                dk_acc[pl.ds(k0, t), :] += dk
                dv_acc[pl.ds(k0, t), :] += dv
                return tuple((carry[j][0] + jnp.sum(log_keeps[j], axis=1, keepdims=True),
                              carry[j][1] + jnp.sum(gs[j], axis=1, keepdims=True), carry[j][2] + _dot_nn(dzs[j], kb))
                             for j in heads)

            zero = jnp.zeros((t, 1), F32)
            init = ((zero, zero, jnp.zeros((t, LANES), F32)),) * 2
            (_, _, dq0), (_, _, dq1) = _over_key_blocks(qi, step, init, reverse=False)
            dq_ref[pl.ds(q0, t), :] = (jnp.where(_head_mask(0), dq0, dq1) * ATT_SCALE).astype(BF16)
            return 0

        lax.fori_loop(0, SEQ // t, q_block, 0)
        dk_ref[...] = dk_acc[...].astype(BF16)
        dv_ref[...] = dv_acc[...].astype(BF16)

    out = jax.ShapeDtypeStruct((n, ATT_W), BF16)
    batch_spec = pl.BlockSpec((SEQ, LANES), lambda b, p: (b, 0))
    return pl.pallas_call(
        body, name="sb_attn_bwd", grid=(n // SEQ, N_PAIRS), in_specs=_qkv_specs() + [batch_spec, _pair_spec()],
        out_specs=[_pair_spec()] * 3, out_shape=[out, out, out],
        scratch_shapes=[pltpu.VMEM((SEQ, LANES), F32), pltpu.VMEM((SEQ, LANES), F32)],
        compiler_params=pltpu.CompilerParams(dimension_semantics=("parallel", "parallel")),
    )(qkv, qkv, qkv, tot, do)


NEG_BIG = -1e30


def fox_attn_fwd(qkv, cum, cum_t):
    n = qkv.shape[0]
    t = ATT_BLK

    def body(q_ref, k_ref, v_ref, cum_ref, cumt_ref, o_ref, lse_ref):
        pair = pl.program_id(1)
        causal = _below_diagonal(False)

        @pl.when(pair == 0)
        def _():
            lse_ref[...] = jnp.zeros_like(lse_ref)

        def q_block(qi, _):
            q0 = pl.multiple_of(qi * t, t)
            qb = q_ref[pl.ds(q0, t), :]
            cum_q = cum_ref[pl.ds(q0, t), :]
            qms = [_masked_q(qb, j) for j in range(2)]
            cqs = [_lane_pick(cum_q, 2 * pair + j) for j in range(2)]

            def step(k0, carry, diagonal):
                kb = k_ref[pl.ds(k0, t), :].astype(BF16)
                vb = v_ref[pl.ds(k0, t), :].astype(BF16)
                heads = range(2)
                zs = [_dot_nt(qms[j], kb) + cqs[j] - cumt_ref[pl.ds(2 * pair + j, 1), pl.ds(k0, t)] for j in heads]
                if diagonal:
                    zs = [jnp.where(causal, z, NEG_BIG) for z in zs]
                m_new = [jnp.maximum(carry[j][0], jnp.max(zs[j], axis=1, keepdims=True)) for j in heads]
                ps = [jnp.exp(zs[j] - m_new[j]) for j in heads]
                alphas = [jnp.exp(carry[j][0] - m_new[j]) for j in heads]
                return tuple((m_new[j], alphas[j] * carry[j][1] + jnp.sum(ps[j], axis=1, keepdims=True),
                              alphas[j] * carry[j][2] + _dot_nn(ps[j].astype(BF16), vb)) for j in heads)

            init = ((jnp.full((t, 1), NEG_BIG, F32), jnp.zeros((t, 1), F32), jnp.zeros((t, LANES), F32)),) * 2
            (m0, l0, acc0), (m1, l1, acc1) = _over_key_blocks(qi, step, init, reverse=False)
            o_ref[pl.ds(q0, t), :] = jnp.where(_head_mask(0), acc0 / l0, acc1 / l1)
            lse_ref[pl.ds(q0, t), :] += _lane_put(m0 + jnp.log(l0), 2 * pair) + _lane_put(m1 + jnp.log(l1), 2 * pair + 1)
            return 0

        lax.fori_loop(0, SEQ // t, q_block, 0)

    batch_spec = pl.BlockSpec((SEQ, LANES), lambda b, p: (b, 0))
    return pl.pallas_call(
        body, name="fox_attn_fwd", grid=(n // SEQ, N_PAIRS),
        in_specs=_qkv_specs() + [batch_spec, pl.BlockSpec((None, N_HEADS, SEQ), lambda b, p: (b, 0, 0))],
        out_specs=[_pair_spec(), batch_spec],
        out_shape=[jax.ShapeDtypeStruct((n, ATT_W), F32), jax.ShapeDtypeStruct((n, LANES), F32)],
        compiler_params=pltpu.CompilerParams(dimension_semantics=("parallel", "arbitrary")),
    )(qkv, qkv, qkv, cum, cum_t)


def fox_attn_bwd(qkv, cum, cum_t, lse, o, do):
    n = qkv.shape[0]
    t = ATT_BLK

    def body(q_ref, k_ref, v_ref, cum_ref, cumt_ref, lse_ref, o_ref, do_ref, dq_ref, dk_ref, dv_ref, dcq_ref, dck_ref,
             dk_acc, dv_acc):
        pair = pl.program_id(1)
        causal = _below_diagonal(False)
        dk_acc[...] = jnp.zeros_like(dk_acc)
        dv_acc[...] = jnp.zeros_like(dv_acc)

        @pl.when(pair == 0)
        def _():
            dcq_ref[...] = jnp.zeros_like(dcq_ref)
            dck_ref[...] = jnp.zeros_like(dck_ref)

        def q_block(qi, _):
            q0 = pl.multiple_of(qi * t, t)
            qb = q_ref[pl.ds(q0, t), :]
            ob = o_ref[pl.ds(q0, t), :]
            dob = do_ref[pl.ds(q0, t), :].astype(F32)
            cum_q = cum_ref[pl.ds(q0, t), :]
            lse_q = lse_ref[pl.ds(q0, t), :]
            qms = [_masked_q(qb, j) for j in range(2)]
            dom32 = [jnp.where(_head_mask(j), dob, 0.0) for j in range(2)]
            doms = [d.astype(BF16) for d in dom32]
            deltas = [jnp.sum(d * ob, axis=1, keepdims=True) for d in dom32]
            cqs = [_lane_pick(cum_q, 2 * pair + j) for j in range(2)]
            lqs = [_lane_pick(lse_q, 2 * pair + j) for j in range(2)]

            def step(k0, carry, diagonal):
                kb = k_ref[pl.ds(k0, t), :].astype(BF16)
                vb = v_ref[pl.ds(k0, t), :].astype(BF16)
                heads = range(2)
                zs = [_dot_nt(qms[j], kb) + cqs[j] - cumt_ref[pl.ds(2 * pair + j, 1), pl.ds(k0, t)] for j in heads]
                d_ps = [_dot_nt(doms[j], vb) for j in heads]
                if diagonal:
                    zs = [jnp.where(causal, z, NEG_BIG) for z in zs]
                ps = [jnp.exp(zs[j] - lqs[j]) for j in heads]
                dv_acc[pl.ds(k0, t), :] += _dot_tn(ps[0].astype(BF16), doms[0]) + _dot_tn(ps[1].astype(BF16), doms[1])
                dzs = [ps[j] * (d_ps[j] - deltas[j]) for j in heads]
                dzb = [dz.astype(BF16) for dz in dzs]
                dk_acc[pl.ds(k0, t), :] += _dot_tn(dzb[0], qms[0]) + _dot_tn(dzb[1], qms[1])
                for j in heads:
                    dck_ref[pl.ds(2 * pair + j, 1), pl.ds(k0, t)] += jnp.sum(dzs[j], axis=0, keepdims=True)
                return tuple((carry[j][0] + _dot_nn(dzb[j], kb), carry[j][1] + jnp.sum(dzs[j], axis=1, keepdims=True))
                             for j in heads)

            init = ((jnp.zeros((t, LANES), F32), jnp.zeros((t, 1), F32)),) * 2
            (dq0, dcq0), (dq1, dcq1) = _over_key_blocks(qi, step, init, reverse=False)
            dq_ref[pl.ds(q0, t), :] = (jnp.where(_head_mask(0), dq0, dq1) * ATT_SCALE).astype(BF16)
            dcq_ref[pl.ds(q0, t), :] += _lane_put(dcq0, 2 * pair) + _lane_put(dcq1, 2 * pair + 1)
            return 0

        lax.fori_loop(0, SEQ // t, q_block, 0)
        dk_ref[...] = dk_acc[...].astype(BF16)
        dv_ref[...] = dv_acc[...].astype(BF16)

    batch_spec = pl.BlockSpec((SEQ, LANES), lambda b, p: (b, 0))
    t_spec = pl.BlockSpec((None, N_HEADS, SEQ), lambda b, p: (b, 0, 0))
    out = jax.ShapeDtypeStruct((n, ATT_W), BF16)
    return pl.pallas_call(
        body, name="fox_attn_bwd", grid=(n // SEQ, N_PAIRS),
        in_specs=_qkv_specs() + [batch_spec, t_spec, batch_spec, _pair_spec(), _pair_spec()],
        out_specs=[_pair_spec()] * 3 + [batch_spec, t_spec],
        scratch_shapes=[pltpu.VMEM((SEQ, LANES), F32), pltpu.VMEM((SEQ, LANES), F32)],
        out_shape=[out, out, out, jax.ShapeDtypeStruct((n, LANES), F32), jax.ShapeDtypeStruct((n // SEQ, N_HEADS, SEQ), F32)],
        compiler_params=pltpu.CompilerParams(dimension_semantics=("parallel", "arbitrary")),
    )(qkv, qkv, qkv, cum, cum_t, lse, o, do)


_HBM = pl.BlockSpec(memory_space=pl.ANY)


def _my_place():
    return lax.axis_index("x"), lax.axis_index("y"), lax.axis_index("c")


def my_index():
    mx, my, mc = _my_place()
    return 4 * mx + 2 * my + mc


def all_gather(name, xs):
    single = not isinstance(xs, (list, tuple))
    xs = [xs] if single else list(xs)
    na = len(xs)

    def body(*refs):
        x_refs, out_refs = refs[:na], refs[na:2 * na]
        send_sems, recv_sems, local_sems = refs[2 * na:]
        mx, my, mc = _my_place()
        me, sibling = (mx, my, mc), (mx, my, 1 - mc)
        chips = [(1 - mx, my), (mx, 1 - my), (1 - mx, 1 - my)]

        def slot(a, px, py, pc):
            return out_refs[a].at[4 * px + 2 * py + pc]

        def copy(a, k, block, to, src=None):
            return pltpu.make_async_remote_copy(
                src_ref=slot(a, *block) if src is None else src, dst_ref=slot(a, *block),
                send_sem=send_sems.at[7 * a + k], recv_sem=recv_sems.at[7 * a + k], device_id=to, device_id_type=MESH)

        mine = [pltpu.make_async_copy(x_refs[a], slot(a, *me), local_sems.at[a]) for a in range(na)]
        for cp in mine:
            cp.start()
        first = []
        for j, chip in enumerate(chips):
            first += [copy(a, 1 + j, me, (*chip, mc), src=x_refs[a]) for a in range(na)]
        first += [copy(a, 0, me, sibling, src=x_refs[a]) for a in range(na)]
        for cp in first:
            cp.start()
        passed = []
        for j, chip in enumerate(chips):
            for a in range(na):
                copy(a, 1 + j, (*chip, mc), me).wait_recv()
                passed.append(copy(a, 4 + j, (*chip, mc), sibling))
                passed[-1].start()
        for a in range(na):
            copy(a, 0, sibling, me).wait_recv()
        for j, chip in enumerate(chips):
            for a in range(na):
                copy(a, 4 + j, (*chip, 1 - mc), me).wait_recv()
        for cp in first + passed:
            cp.wait_send()
        for cp in mine:
            cp.wait()

    outs = pl.pallas_call(
        body, name=name, in_specs=[_HBM] * na, out_specs=[_HBM] * na,
        out_shape=[jax.ShapeDtypeStruct((N_DEV,) + x.shape, x.dtype) for x in xs],
        scratch_shapes=[pltpu.SemaphoreType.DMA((7 * na,)), pltpu.SemaphoreType.DMA((7 * na,)), pltpu.SemaphoreType.DMA((na,))],
    )(*xs)
    return outs[0] if single else list(outs)


_SEM = pl.BlockSpec(memory_space=pltpu.SEMAPHORE)
_HBM_ONLY = pl.BlockSpec(memory_space=pltpu.HBM)
_EFFECT = pltpu.SideEffectType.DATAFLOW_SIDE_EFFECTING
N_PEERS = N_DEV


def _peers():
    mx, my, mc = _my_place()
    return [((1 - mx) if (r >> 2) & 1 else mx, (1 - my) if (r >> 1) & 1 else my, (1 - mc) if r & 1 else mc)
            for r in range(N_DEV)]


def _exchange_copies(scatter, x_refs, land_refs, send_sems, recv_sems):
    me = my_index()
    copies = []
    for a, (x_ref, land_ref) in enumerate(zip(x_refs, land_refs)):
        for r, (px, py, pc) in enumerate(_peers()):
            src = x_ref.at[4 * px + 2 * py + pc] if scatter else x_ref
            dst = land_ref.at[r] if scatter else land_ref.at[me]
            copies.append(pltpu.make_async_remote_copy(
                src_ref=src, dst_ref=dst, send_sem=send_sems.at[N_PEERS * a + r], recv_sem=recv_sems.at[N_PEERS * a + r],
                device_id=(px, py, pc), device_id_type=MESH))
    return copies


def exchange_start(name, xs, scatter):
    na = len(xs)
    lands = [lax.empty((N_PEERS,) + x.shape[1:] if scatter else (N_DEV,) + x.shape, x.dtype) for x in xs]

    def body(*refs):
        x_refs, land_refs, send_sems, recv_sems = refs[:na], refs[na:2 * na], refs[2 * na], refs[2 * na + 1]
        token = refs[-1]
        for cp in _exchange_copies(scatter, x_refs, land_refs, send_sems, recv_sems):
            cp.start()
        token[...] = jnp.zeros_like(token)

    outs = pl.pallas_call(
        body, name=name,
        out_shape=(pltpu.SemaphoreType.DMA((N_PEERS * na,)), pltpu.SemaphoreType.DMA((N_PEERS * na,)),
                   *[pltpu.HBM(x.shape, x.dtype) for x in xs], *[pltpu.HBM(l.shape, l.dtype) for l in lands],
                   jax.ShapeDtypeStruct((8, LANES), F32)),
        in_specs=[_HBM_ONLY] * (2 * na),
        out_specs=(_SEM, _SEM, *[_HBM_ONLY] * (2 * na), pl.BlockSpec(memory_space=pltpu.VMEM)),
        input_output_aliases={i: 2 + i for i in range(2 * na)},
        compiler_params=pltpu.CompilerParams(has_side_effects=_EFFECT),
    )(*[pltpu.with_memory_space_constraint(x, pltpu.HBM) for x in xs],
      *[pltpu.with_memory_space_constraint(l, pltpu.HBM) for l in lands])
    return (scatter, outs[0], outs[1], outs[2:2 + na], outs[2 + na:2 + 2 * na]), outs[-1]


def exchange_finish(name, handle, after):
    scatter, send_sems, recv_sems, xs, lands = handle
    na = len(xs)

    def body(*refs):
        x_refs, land_refs, send_ref, recv_ref = refs[:na], refs[na:2 * na], refs[2 * na], refs[2 * na + 1]
        for cp in _exchange_copies(scatter, x_refs, land_refs, send_ref, recv_ref):
            cp.wait_send()
            cp.wait_recv()

    outs = pl.pallas_call(
        body, name=name,
        out_shape=tuple(pltpu.HBM(t.shape, t.dtype) for t in list(xs) + list(lands)),
        in_specs=[_HBM_ONLY] * (2 * na) + [_SEM, _SEM, _HBM],
        out_specs=tuple([_HBM_ONLY] * (2 * na)),
        input_output_aliases={i: i for i in range(2 * na)},
        compiler_params=pltpu.CompilerParams(has_side_effects=_EFFECT),
    )(*xs, *lands, send_sems, recv_sems, after)
    return list(outs[:na]), list(outs[na:])


def _pick_rows(n, target):
    best = None
    for t in range(8, min(n, target) + 1, 8):
        if n % t == 0:
            best = t
    return best if best is not None else n


def add_blocks(name, parts, out_dtype, rows=512):
    r, w = parts[0].shape
    tr = _pick_rows(r, rows)

    def body(*refs):
        acc = refs[0][...].astype(F32)
        for ref in refs[1:-1]:
            acc = acc + ref[...].astype(F32)
        refs[-1][...] = acc.astype(refs[-1].dtype)

    spec = pl.BlockSpec((tr, w), lambda i: (i, 0))
    return pl.pallas_call(
        body, name=name, grid=(r // tr,), in_specs=[spec] * len(parts), out_specs=spec,
        out_shape=jax.ShapeDtypeStruct((r, w), out_dtype),
        compiler_params=pltpu.CompilerParams(dimension_semantics=("parallel",)),
    )(*parts)


def sum_rows(name, x):
    def body(x_ref, o_ref):
        o_ref[...] = jnp.sum(x_ref[...], axis=0, keepdims=True)

    return pl.pallas_call(body, name=name, out_shape=jax.ShapeDtypeStruct((1, x.shape[1]), F32))(x)


def gather_start(name, blocks):
    return exchange_start(name, blocks, scatter=False)


def gather_finish(name, handle, after):
    return exchange_finish(name, handle, after)[1]


def scatter_start(name, g8s):
    return exchange_start(name, g8s, scatter=True)


def scatter_finish(name, handle, after):
    _, lands = exchange_finish(name, handle, after)
    outs = []
    for a, land in enumerate(lands):
        w = land.shape[-1]
        outs.append(add_blocks(f"{name}_sum{a}", [land[k].reshape(-1, w) for k in range(N_PEERS)], F32).reshape(land.shape[1:]))
    return outs


def _pack(arrays, width, row_mult, dtype, lead=0):
    parts, metas = [], []
    for a in arrays:
        lead_shape = a.shape[:lead]
        size = int(np.prod(a.shape[lead:]))
        chunk = row_mult * width
        padded = -(-size // chunk) * chunk
        flat = a.astype(dtype).reshape(lead_shape + (size,))
        if padded != size:
            flat = jnp.pad(flat, [(0, 0)] * lead + [(0, padded - size)])
        parts.append(flat.reshape(lead_shape + (padded // width, width)))
        metas.append((a.shape[lead:], size, padded // width))
    return jnp.concatenate(parts, axis=lead), metas


def _unpack(slab, metas, lead=0):
    out, r0 = [], 0
    for shape, size, rows in metas:
        part = lax.slice_in_dim(slab, r0, r0 + rows, axis=lead)
        lead_shape = part.shape[:lead]
        flat = part.reshape(lead_shape + (rows * part.shape[-1],))
        out.append(lax.slice_in_dim(flat, 0, size, axis=lead).reshape(lead_shape + tuple(shape)))
        r0 += rows
    return out


def _f_adamw(w, g, m, v):
    m = ADAM_B1 * m + (1.0 - ADAM_B1) * g
    v = ADAM_B2 * v + (1.0 - ADAM_B2) * (g * g)
    m_hat = m / (1.0 - ADAM_B1 ** ADAM_STEP)
    v_hat = v / (1.0 - ADAM_B2 ** ADAM_STEP)
    delta = (-ADAM_LR) * (m_hat / (jnp.sqrt(v_hat) + ADAM_EPS) + ADAM_WD * w)
    return delta, m, v


def adamw(name, w, g, m, v):
    shape = w.shape
    w2 = shape[-1]
    flat = [a.reshape(-1, w2) for a in (w, g, m, v)]
    tm = _pick_rows(flat[0].shape[0], 256)
    outs = ew_fwd(name, _f_adamw, flat, [], [], [w2] * 3, [F32] * 3, tm=tm)
    return [o.reshape(shape) for o in outs]


WEIGHTS = ["ffn1_norm", "ffn1_w1", "ffn1_w3", "ffn1_w2", "mix_norm", "w_in", "conv_w", "conv_b", "rg_wa", "rg_ba", "rg_wx",
           "rg_bx", "rg_lam", "fox_bf", "merge_b", "w_rg", "w_sb", "w_fox", "w_o", "ffn2_norm", "ffn2_w1", "ffn2_w3",
           "ffn2_w2", "ada_w", "ada_b", "final_norm", "final_ada_w", "final_ada_b"]
GATHERED = {"ffn1_w1": 2, "ffn1_w3": 2, "ffn1_w2": 1, "w_in": 2, "w_rg": 1, "w_sb": 2, "w_fox": 2, "w_o": 1,
            "ffn2_w1": 2, "ffn2_w3": 2, "ffn2_w2": 1}
REPLICATED = ["ffn1_norm", "mix_norm", "conv_b", "rg_wa", "rg_ba", "rg_wx", "rg_bx", "rg_lam", "fox_bf", "merge_b",
              "ffn2_norm", "final_norm"]
GROUPS = (("ffn1", ("ffn1_w1", "ffn1_w3", "ffn1_w2")), ("mix", ("w_in", "w_rg", "w_sb", "w_fox", "w_o")),
          ("ffn2", ("ffn2_w1", "ffn2_w3", "ffn2_w2")))
IN_CUTS = (0, 1024, 2048, 3584, 5120, 5128, 8200)


def _unshard(g, axis):
    g = jnp.moveaxis(g, 0, axis)
    shape = g.shape
    return g.reshape(shape[:axis] + (shape[axis] * shape[axis + 1],) + shape[axis + 2:])


def _reshard(full, axis):
    shape = full.shape
    g = full.reshape(shape[:axis] + (N_DEV, shape[axis] // N_DEV) + shape[axis + 1:])
    return jnp.moveaxis(g, axis, 0)


def _block_diag(w):
    nb, bd, _ = w.shape
    eye = jnp.eye(nb, dtype=bool)[:, None, :, None]
    return jnp.where(eye, w[:, :, None, :], 0.0).reshape(nb * bd, nb * bd)


def _diag_blocks(m, nb=RG_BLOCKS):
    bd = m.shape[0] // nb
    return jnp.stack([m[k * bd:(k + 1) * bd, k * bd:(k + 1) * bd] for k in range(nb)])


def _pad_lanes(a, width=LANES):
    return jnp.pad(a, [(0, 0)] * (a.ndim - 1) + [(0, width - a.shape[-1])])


def _bp(m, k, which):
    return m[:, k, which][:, None, :]


def _f_silu(c):
    return c * jax.nn.sigmoid(c)


def _f_add_bias(a, b):
    return a + b


FFN_TM = 512
FFN_TN = 1408


def ffn_up(name, h, w1, w3):
    n, k = h.shape
    f = w1.shape[1]
    tm, tn = min(FFN_TM, n), _pick_tile(f, FFN_TN)

    def body(h_ref, w1_ref, w3_ref, a_ref, b_ref, s_ref):
        hv = h_ref[...]
        a = jnp.dot(hv, w1_ref[...], preferred_element_type=F32)
        b = jnp.dot(hv, w3_ref[...], preferred_element_type=F32)
        a_ref[...] = a.astype(BF16)
        b_ref[...] = b.astype(BF16)
        s_ref[...] = ((a * jax.nn.sigmoid(a)) * b).astype(BF16)

    wspec = pl.BlockSpec((k, tn), lambda i, j: (0, j))
    ospec = pl.BlockSpec((tm, tn), lambda i, j: (i, j))
    out = jax.ShapeDtypeStruct((n, f), BF16)
    return pl.pallas_call(
        body, name=name, grid=(n // tm, f // tn), in_specs=[pl.BlockSpec((tm, k), lambda i, j: (i, 0)), wspec, wspec],
        out_specs=[ospec] * 3, out_shape=[out] * 3,
        compiler_params=pltpu.CompilerParams(dimension_semantics=("parallel", "parallel")),
    )(h, w1, w3)


def ffn_down_dx(name, dy, w2, a, b):
    n, k = dy.shape
    f = w2.shape[0]
    tm, tn = min(FFN_TM, n), _pick_tile(f, FFN_TN)

    def body(dy_ref, w2_ref, a_ref, b_ref, da_ref, db_ref):
        ds = _dot_nt(dy_ref[...], w2_ref[...])
        av = a_ref[...].astype(F32)
        sig = jax.nn.sigmoid(av)
        da_ref[...] = (ds * b_ref[...].astype(F32) * (sig * (1.0 + av * (1.0 - sig)))).astype(BF16)
        db_ref[...] = (ds * (av * sig)).astype(BF16)

    ospec = pl.BlockSpec((tm, tn), lambda i, j: (i, j))
    out = jax.ShapeDtypeStruct((n, f), BF16)
    return pl.pallas_call(
        body, name=name, grid=(n // tm, f // tn),
        in_specs=[pl.BlockSpec((tm, k), lambda i, j: (i, 0)), pl.BlockSpec((tn, k), lambda i, j: (j, 0)), ospec, ospec],
        out_specs=[ospec] * 2, out_shape=[out] * 2,
        compiler_params=pltpu.CompilerParams(dimension_semantics=("parallel", "parallel")),
    )(dy, w2, a, b)


def _ffn_fwd(tag, x, shift, scale, gate, gain, w1, w3, w2):
    h = ew_fwd(f"ffn_norm_{tag}", f_norm_mod, [x], [shift, scale], [gain], [D], [BF16])[0]
    a, b3, s = ffn_up(f"ffn_up_{tag}", h, w1, w3)
    y = matmul(f"ffn_down_{tag}", s, w2, "nn", tm=1024)
    xo = ew_fwd(f"ffn_res_{tag}", functools.partial(f_resid, 0.5), [x, y], [gate], [], [D], [F32])[0]
    return xo, (x, h, a, b3, s, y)


def _ffn_bwd(tag, dxo, saved, shift, scale, gate, gain, w1, w3, w2):
    x, h, a, b3, s, y = saved
    (dy,), (dgate,), _ = ew_bwd(f"ffn_res_bwd_{tag}", functools.partial(f_resid, 0.5), [x, y], [gate], [], [dxo], [None, BF16])
    da, db3 = ffn_down_dx(f"ffn_down_dx_{tag}", dy, w2, a, b3)
    dw2 = matmul(f"ffn_dw2_{tag}", s, dy, "tn", tm=1408, tn=256)
    dw1 = matmul(f"ffn_dw1_{tag}", h, da, "tn", tm=1024, tn=256)
    dw3 = matmul(f"ffn_dw3_{tag}", h, db3, "tn", tm=1024, tn=256)
    dh = matmul(f"ffn_up_dx_{tag}", [da, db3], [w1, w3], "nt", tm=1024)
    (dx,), (dshift, dscale), (dgain,) = ew_bwd(f"ffn_norm_bwd_{tag}", f_norm_mod, [x], [shift, scale], [gain], [dh], [F32],
                                               adds=[dxo])
    return dx, (dshift, dscale, dgate), dgain, dw1, dw3, dw2


def _mixer_fwd(tag, x, shift, scale, gate, p):
    h = ew_fwd(f"mix_norm_{tag}", f_norm_mod, [x], [shift, scale], [p["gain"]], [D], [BF16])[0]
    rgx = matmul(f"in_rgx_{tag}", h, p["w_rgx"], "nn")
    rgate = matmul(f"in_gate_{tag}", h, p["w_gate"], "nn")
    sbqkv = matmul(f"in_sb_{tag}", h, p["w_sbqkv"], "nn")
    foxqkv = matmul(f"in_fox_{tag}", h, p["w_foxqkv"], "nn")
    ff = matmul(f"in_forget_{tag}", h, p["w_f"], "nn")
    mg = matmul(f"in_merge_{tag}", h, p["w_merge"], "nn")
    xa = conv_fwd(rgx, p["conv_w8"], p["conv_b"])
    pre_r = matmul(f"rg_a_{tag}", xa, p["wa_bd"], "nn")
    pre_i = matmul(f"rg_x_{tag}", xa, p["wx_bd"], "nn")
    a, u = ew_fwd(f"rg_gates_{tag}", f_rg_gates, [pre_r, pre_i, xa], [], [p["ba"], p["bx"], p["lam"]], [D, D], [F32, F32],
                  tm=EW_ROWS_WIDE)
    hs = scan_fwd(a, u)
    ya = ew_fwd(f"rg_out_{tag}", f_gelu_mul, [rgate, hs], [], [], [D], [BF16])[0]
    yb, sb_tot = sb_attn_fwd(sbqkv)
    lf = ew_fwd(f"fox_logf_{tag}", f_log_sigmoid_bias, [ff], [], [p["bf"]], [LANES], [F32])[0]
    cum = seq_cumsum(f"fox_cum_{tag}", [lf], [1.0], False)
    cum_t = cum.reshape(-1, SEQ, LANES)[:, :, :N_HEADS].transpose(0, 2, 1)
    yc, lse = fox_attn_fwd(foxqkv, cum, cum_t)
    pa = matmul(f"out_rg_{tag}", ya, p["w_rg"], "nn")
    pb = matmul(f"out_sb_{tag}", yb, p["w_sb"], "nn")
    pc = matmul(f"out_fox_{tag}", yc, p["w_fox"], "nn")
    mixed = ew_fwd(f"merge_{tag}", f_merge, [mg, pa, pb, pc], [], [p["merge_b"]], [D], [BF16], tm=EW_ROWS_WIDE)[0]
    y = matmul(f"out_o_{tag}", mixed, p["w_o"], "nn")
    xo = ew_fwd(f"mix_res_{tag}", functools.partial(f_resid, 1.0), [x, y], [gate], [], [D], [F32])[0]
    saved = dict(x=x, h=h, rgx=rgx, rgate=rgate, sbqkv=sbqkv, foxqkv=foxqkv, ff=ff, mg=mg, xa=xa, pre_r=pre_r, pre_i=pre_i,
                 a=a, hs=hs, ya=ya, yb=yb, sb_tot=sb_tot, cum=cum, cum_t=cum_t, yc=yc, lse=lse, pa=pa, pb=pb, pc=pc,
                 mixed=mixed, y=y)
    return xo, saved


def _mixer_bwd(tag, dxo, s, shift, scale, gate, p):
    (dy,), (dgate,), _ = ew_bwd(f"mix_res_bwd_{tag}", functools.partial(f_resid, 1.0), [s["x"], s["y"]], [gate], [], [dxo],
                                [None, BF16])
    dmixed = matmul(f"out_o_dx_{tag}", dy, p["w_o"], "nt")
    g = {"w_o": matmul(f"out_o_dw_{tag}", s["mixed"], dy, "tn", tm=1024, tn=256)}
    (dmg, dpa, dpb, dpc), _, (g["merge_b"],) = ew_bwd(
        f"merge_bwd_{tag}", f_merge, [s["mg"], s["pa"], s["pb"], s["pc"]], [], [p["merge_b"]], [dmixed], [BF16] * 4,
        tm=EW_ROWS_WIDE)
    dya = matmul(f"out_rg_dx_{tag}", dpa, p["w_rg"], "nt")
    g["w_rg"] = matmul(f"out_rg_dw_{tag}", s["ya"], dpa, "tn", tm=1024, tn=256)
    dyb = matmul(f"out_sb_dx_{tag}", dpb, p["w_sb"], "nt", out_dtype=BF16)
    g["w_sb"] = matmul(f"out_sb_dw_{tag}", s["yb"], dpb, "tn", tm=1024, tn=256)
    dyc = matmul(f"out_fox_dx_{tag}", dpc, p["w_fox"], "nt", out_dtype=BF16)
    g["w_fox"] = matmul(f"out_fox_dw_{tag}", s["yc"], dpc, "tn", tm=1024, tn=256)
    dq_c, dk_c, dv_c, dcq, dck = fox_attn_bwd(s["foxqkv"], s["cum"], s["cum_t"], s["lse"], s["yc"], dyc)
    dck_rows = _pad_lanes(dck.transpose(0, 2, 1).reshape(-1, N_HEADS))
    dlf = seq_cumsum(f"fox_cum_bwd_{tag}", [dcq, dck_rows], [1.0, -1.0], True)
    (dff,), _, (dbf,) = ew_bwd(f"fox_logf_bwd_{tag}", f_log_sigmoid_bias, [s["ff"]], [], [p["bf"]], [dlf], [BF16])
    g["fox_bf"] = dbf[0, :N_HEADS]
    dq_b, dk_b, dv_b = sb_attn_bwd(s["sbqkv"], s["sb_tot"], dyb)
    (drgate, dhs), _, _ = ew_bwd(f"rg_out_bwd_{tag}", f_gelu_mul, [s["rgate"], s["hs"]], [], [], [dya], [BF16, F32],
                                 tm=EW_ROWS_WIDE)
    da, du = scan_bwd(s["a"], s["hs"], dhs)
    (dpre_r, dpre_i, dxa1), _, (g["rg_ba"], g["rg_bx"], g["rg_lam"]) = ew_bwd(
        f"rg_gates_bwd_{tag}", f_rg_gates, [s["pre_r"], s["pre_i"], s["xa"]], [], [p["ba"], p["bx"], p["lam"]], [da, du],
        [BF16, BF16, F32], tm=EW_ROWS_WIDE)
    dxa2 = matmul(f"rg_dx_{tag}", [dpre_r, dpre_i], [p["wa_bd"], p["wx_bd"]], "nt")
    g["rg_wa"] = _diag_blocks(matmul(f"rg_a_dw_{tag}", s["xa"], dpre_r, "tn", tm=512, tn=256))
    g["rg_wx"] = _diag_blocks(matmul(f"rg_x_dw_{tag}", s["xa"], dpre_i, "tn", tm=512, tn=256))
    drgx, dwb = conv_bwd(s["rgx"], p["conv_w8"], dxa1, dxa2)
    g["conv_w"] = dwb[:CONV_K]
    g["conv_b"] = dwb[CONV_K]
    cots = [drgx, drgate, dq_b, dk_b, dv_b, dq_c, dk_c, dv_c, dff, dmg]
    w_sb3 = [p["w_sbqkv"][:, k * ATT_W:(k + 1) * ATT_W] for k in range(3)]
    w_fox3 = [p["w_foxqkv"][:, k * ATT_W:(k + 1) * ATT_W] for k in range(3)]
    ws = [p["w_rgx"], p["w_gate"]] + w_sb3 + w_fox3 + [p["w_f"], p["w_merge"]]
    dh = matmul(f"in_dx_{tag}", cots, ws, "nt", tm=512)
    dws = [matmul(f"in_dw{k}_{tag}", s["h"], ct, "tn", tm=1024, tn=256) for k, ct in enumerate(cots)]
    dws[8] = dws[8][:, :N_HEADS]
    g["w_in"] = jnp.concatenate(dws, axis=1)
    (dx,), (dshift, dscale), (g["mix_norm"],) = ew_bwd(f"mix_norm_bwd_{tag}", f_norm_mod, [s["x"]], [shift, scale], [p["gain"]],
                                                       [dh], [F32], adds=[dxo])
    return dx, (dshift, dscale, dgate), g


def _final_loss(x, target, shift, scale, gain):
    n = x.shape[0]
    tm = min(EW_ROWS, SEQ)
    tpb = SEQ // tm

    def body(x_ref, t_ref, sh_ref, sc_ref, g_ref, loss_ref, dx_ref, dsh_ref, dsc_ref, dg_ref):
        i = pl.program_id(0)
        out, vjp = jax.vjp(f_norm_mod, x_ref[...], sh_ref[...], sc_ref[...], g_ref[...])
        diff = out - t_ref[...]
        dx, dsh, dsc, dg = vjp(diff * (1.0 / D))
        dx_ref[...] = dx
        sq = jnp.sum(jnp.sum(diff * diff, axis=1, keepdims=True), axis=0, keepdims=True)

        @pl.when(i % tpb == 0)
        def _():
            dsh_ref[...] = jnp.zeros_like(dsh_ref)
            dsc_ref[...] = jnp.zeros_like(dsc_ref)

        @pl.when(i == 0)
        def _():
            dg_ref[...] = jnp.zeros_like(dg_ref)
            loss_ref[...] = jnp.zeros_like(loss_ref)

        dsh_ref[...] += dsh
        dsc_ref[...] += dsc
        dg_ref[...] += dg
        loss_ref[...] += jnp.broadcast_to(sq, (1, LANES)) * (0.5 / D)

    row, bp, gp = _row_spec(D, tm), _bparam_spec(D, tpb), _gparam_spec((1, D))
    return pl.pallas_call(
        body, name="final_loss", grid=(n // tm,), in_specs=[row, row, bp, bp, gp],
        out_specs=[_gparam_spec((1, LANES)), row, bp, bp, gp],
        out_shape=[jax.ShapeDtypeStruct((1, LANES), F32), jax.ShapeDtypeStruct((n, D), F32),
                   jax.ShapeDtypeStruct(shift.shape, F32), jax.ShapeDtypeStruct(scale.shape, F32),
                   jax.ShapeDtypeStruct((1, D), F32)],
        compiler_params=pltpu.CompilerParams(dimension_semantics=("arbitrary",)),
    )(x, target, shift, scale, gain)


def kernel(x, c, ffn1_norm, ffn1_w1, ffn1_w3, ffn1_w2, mix_norm, w_in, conv_w, conv_b, rg_wa, rg_ba, rg_wx, rg_bx, rg_lam, fox_bf, merge_b, w_rg, w_sb, w_fox, w_o, ffn2_norm, ffn2_w1, ffn2_w3, ffn2_w2, ada_w, ada_b, final_norm, final_ada_w, final_ada_b, loss_target, m_ffn1_norm, m_ffn1_w1, m_ffn1_w3, m_ffn1_w2, m_mix_norm, m_w_in, m_conv_w, m_conv_b, m_rg_wa, m_rg_ba, m_rg_wx, m_rg_bx, m_rg_lam, m_fox_bf, m_merge_b, m_w_rg, m_w_sb, m_w_fox, m_w_o, m_ffn2_norm, m_ffn2_w1, m_ffn2_w3, m_ffn2_w2, m_ada_w, m_ada_b, m_final_norm, m_final_ada_w, m_final_ada_b, v_ffn1_norm, v_ffn1_w1, v_ffn1_w3, v_ffn1_w2, v_mix_norm, v_w_in, v_conv_w, v_conv_b, v_rg_wa, v_rg_ba, v_rg_wx, v_rg_bx, v_rg_lam, v_fox_bf, v_merge_b, v_w_rg, v_w_sb, v_w_fox, v_w_o, v_ffn2_norm, v_ffn2_w1, v_ffn2_w3, v_ffn2_w2, v_ada_w, v_ada_b, v_final_norm, v_final_ada_w, v_final_ada_b):
    given = dict(zip(["x", "c"] + WEIGHTS + ["loss_target"] + ["m_" + n for n in WEIGHTS] + ["v_" + n for n in WEIGHTS],
                     (x, c, ffn1_norm, ffn1_w1, ffn1_w3, ffn1_w2, mix_norm, w_in, conv_w, conv_b, rg_wa, rg_ba, rg_wx, rg_bx, rg_lam, fox_bf, merge_b, w_rg, w_sb, w_fox, w_o, ffn2_norm, ffn2_w1, ffn2_w3, ffn2_w2, ada_w, ada_b, final_norm, final_ada_w, final_ada_b, loss_target, m_ffn1_norm, m_ffn1_w1, m_ffn1_w3, m_ffn1_w2, m_mix_norm, m_w_in, m_conv_w, m_conv_b, m_rg_wa, m_rg_ba, m_rg_wx, m_rg_bx, m_rg_lam, m_fox_bf, m_merge_b, m_w_rg, m_w_sb, m_w_fox, m_w_o, m_ffn2_norm, m_ffn2_w1, m_ffn2_w3, m_ffn2_w2, m_ada_w, m_ada_b, m_final_norm, m_final_ada_w, m_final_ada_b, v_ffn1_norm, v_ffn1_w1, v_ffn1_w3, v_ffn1_w2, v_mix_norm, v_w_in, v_conv_w, v_conv_b, v_rg_wa, v_rg_ba, v_rg_wx, v_rg_bx, v_rg_lam, v_fox_bf, v_merge_b, v_w_rg, v_w_sb, v_w_fox, v_w_o, v_ffn2_norm, v_ffn2_w1, v_ffn2_w3, v_ffn2_w2, v_ada_w, v_ada_b, v_final_norm, v_final_ada_w, v_final_ada_b)))
    idx = my_index()
    n_batch = N_DEV * B_LOC
    ada_cols = ada_w.shape[2]
    fin_cols = final_ada_w.shape[1]

    small_in, small_in_meta = _pack([c, conv_w], LANES, 8, F32)
    c_parts, conv_w_parts = _unpack(all_gather("gather_c_conv", small_in), small_in_meta, lead=1)
    c_all = c_parts.reshape(n_batch, D)
    conv_w_all = _unshard(conv_w_parts, 2)
    c_act = ew_fwd("c_silu", _f_silu, [c_all], [], [], [D], [F32])[0]
    mod_cols = [matmul(f"ada_proj_{l}", c_act, ada_w[l], "nn") for l in range(DEPTH)]
    mod_cols.append(matmul("ada_proj_final", c_act, final_ada_w, "nn"))
    mod_g = all_gather("gather_mod", jnp.concatenate(mod_cols, axis=1))

    shards = {(l, group): [given[n][l].astype(BF16) for n in members] for l in range(DEPTH) for group, members in GROUPS}
    waves = [[(0, "ffn1")], [(0, "mix")], [(0, "ffn2")] + [(l, group) for l in range(1, DEPTH) for group, _ in GROUPS]]
    gather_handles, landed = {}, {}

    def start_wave(wave, behind, carrier):
        blocks, behind = lax.optimization_barrier(({key: shards[key] for key in wave}, behind))
        for key in wave:
            gather_handles[key], token = gather_start(f"gather_start_{key[1]}{key[0]}", blocks[key])
            carrier = carrier + token[0, 0]
        return behind, carrier

    def weights_of(l, group, after):
        key = (l, group)
        if key not in landed:
            landed[key] = gather_finish(f"gather_finish_{group}{l}", gather_handles[key], after)
        return {n: _unshard(b, GATHERED[n] - 1) for n, b in zip(dict(GROUPS)[group], landed[key])}

    _, mod_g = start_wave(waves[0], mod_g, mod_g)
    landed[0, "ffn1"] = gather_finish("gather_finish_ffn10", gather_handles[0, "ffn1"], mod_g)
    landed[0, "ffn1"], mod_g = start_wave(waves[1], landed[0, "ffn1"], mod_g)

    mods = []
    for l in range(DEPTH):
        full = mod_g[:, :, l * ada_cols:(l + 1) * ada_cols].transpose(1, 0, 2).reshape(n_batch, N_DEV * ada_cols)
        full = ew_fwd(f"ada_bias_{l}", _f_add_bias, [full], [], [ada_b[l][None]], [full.shape[1]], [F32])[0]
        mods.append(lax.dynamic_slice_in_dim(full, idx * B_LOC, B_LOC, axis=0).reshape(B_LOC, 3, 3, D))
    fm = mod_g[:, :, DEPTH * ada_cols:].transpose(1, 0, 2).reshape(n_batch, N_DEV * fin_cols)
    fm = ew_fwd("ada_bias_final", _f_add_bias, [fm], [], [final_ada_b[None]], [fm.shape[1]], [F32])[0]
    fm = lax.dynamic_slice_in_dim(fm, idx * B_LOC, B_LOC, axis=0).reshape(B_LOC, 2, D)

    def mixer_params(l, w):
        wi = w["w_in"]
        cut = IN_CUTS
        return dict(
            gain=mix_norm[l][None], w_rgx=wi[:, cut[0]:cut[1]], w_gate=wi[:, cut[1]:cut[2]], w_sbqkv=wi[:, cut[2]:cut[3]],
            w_foxqkv=wi[:, cut[3]:cut[4]], w_f=_pad_lanes(wi[:, cut[4]:cut[5]]), w_merge=wi[:, cut[5]:cut[6]],
            conv_w8=jnp.pad(conv_w_all[l], ((0, 8 - CONV_K), (0, 0))), conv_b=conv_b[l][None],
            wa_bd=_block_diag(rg_wa[l]), wx_bd=_block_diag(rg_wx[l]), ba=rg_ba[l][None], bx=rg_bx[l][None], lam=rg_lam[l][None],
            bf=_pad_lanes(fox_bf[l][None]), merge_b=merge_b[l][None], w_rg=w["w_rg"], w_sb=w["w_sb"], w_fox=w["w_fox"],
            w_o=w["w_o"])

    n_tok = x.shape[0] * x.shape[1]
    h = x.reshape(n_tok, D)
    saved = []
    for l in range(DEPTH):
        m = mods[l]
        w1 = weights_of(l, "ffn1", m if l == 0 else h)
        h, s1 = _ffn_fwd(f"a{l}", h, _bp(m, 0, 0), _bp(m, 0, 1), _bp(m, 0, 2), ffn1_norm[l][None], w1["ffn1_w1"], w1["ffn1_w3"],
                         w1["ffn1_w2"])
        w2 = weights_of(l, "mix", h)
        if l == 0:
            landed[0, "mix"], m = start_wave(waves[2], landed[0, "mix"], m)
        p = mixer_params(l, w2)
        h, s2 = _mixer_fwd(f"{l}", h, _bp(m, 1, 0), _bp(m, 1, 1), _bp(m, 1, 2), p)
        w3 = weights_of(l, "ffn2", h)
        h, s3 = _ffn_fwd(f"b{l}", h, _bp(m, 2, 0), _bp(m, 2, 1), _bp(m, 2, 2), ffn2_norm[l][None], w3["ffn2_w1"], w3["ffn2_w3"],
                         w3["ffn2_w2"])
        saved.append((s1, s2, s3, p, w1, w3))
    loss_row, dh, dfshift, dfscale, dgain_final = _final_loss(h, loss_target.reshape(n_tok, D), fm[:, 0][:, None, :],
                                                              fm[:, 1][:, None, :], final_norm[None])

    grads = {n: [None] * DEPTH for n in WEIGHTS}
    d_mods = [None] * DEPTH
    scatter_handles = {}
    after_start = jnp.zeros((), F32)

    def scatter_blocks(l, group):
        return [_reshard(grads[n][l], GATHERED[n] - 1).astype(BF16) for n in dict(GROUPS)[group]]

    def start_scatter(l, group, g8s=None):
        g8s = scatter_blocks(l, group) if g8s is None else g8s
        scatter_handles[l, group], token = scatter_start(f"scatter_start_{group}{l}", g8s)
        return token[0, 0]

    for l in reversed(range(DEPTH)):
        m = mods[l]
        s1, s2, s3, p, w1, w3 = saved[l]
        dh, dm3, grads["ffn2_norm"][l], grads["ffn2_w1"][l], grads["ffn2_w3"][l], grads["ffn2_w2"][l] = _ffn_bwd(
            f"b{l}", dh, s3, _bp(m, 2, 0), _bp(m, 2, 1), _bp(m, 2, 2) + after_start, ffn2_norm[l][None], w3["ffn2_w1"],
            w3["ffn2_w3"], w3["ffn2_w2"])
        after_start = start_scatter(l, "ffn2")
        dh, dm2, gm = _mixer_bwd(f"{l}", dh, s2, _bp(m, 1, 0), _bp(m, 1, 1), _bp(m, 1, 2) + after_start, p)
        for n, gval in gm.items():
            grads[n][l] = gval
        after_start = start_scatter(l, "mix")
        dh, dm1, grads["ffn1_norm"][l], grads["ffn1_w1"][l], grads["ffn1_w3"][l], grads["ffn1_w2"][l] = _ffn_bwd(
            f"a{l}", dh, s1, _bp(m, 0, 0), _bp(m, 0, 1), _bp(m, 0, 2) + after_start, ffn1_norm[l][None], w1["ffn1_w1"],
            w1["ffn1_w3"], w1["ffn1_w2"])
        if l > 0:
            after_start = start_scatter(l, "ffn1")
        d_mods[l] = jnp.concatenate([t.reshape(B_LOC, D) for dm in (dm1, dm2, dm3) for t in dm], axis=1)
    grad_x = dh.reshape(x.shape)
    d_fm = jnp.concatenate([dfshift.reshape(B_LOC, D), dfscale.reshape(B_LOC, D)], axis=1)

    rep = {n: jnp.stack([t.reshape(given[n].shape[1:]) for t in grads[n]]) for n in REPLICATED if n != "final_norm"}
    rep["final_norm"] = dgain_final.reshape(D)
    rep["conv_w"] = jnp.stack(grads["conv_w"])
    rep_names = list(rep)
    rep_slab, rep_meta = _pack([rep[n] for n in rep_names], LANES, 8, F32)
    mod_slab, mod_meta = _pack(d_mods + [d_fm], LANES, 8, F32)
    small_g = all_gather("gather_small_grads", jnp.concatenate([mod_slab, rep_slab], axis=0))
    last_blocks, small_g = lax.optimization_barrier((scatter_blocks(0, "ffn1"), small_g))
    small_g = small_g + start_scatter(0, "ffn1", last_blocks)
    d_mod_all = [t.reshape(n_batch, -1) for t in _unpack(small_g[:, :mod_slab.shape[0]], mod_meta, lead=1)]
    rep_sum = add_blocks("sum_small_grads", [small_g[k, mod_slab.shape[0]:] for k in range(N_DEV)], F32)
    rep_grad = dict(zip(rep_names, _unpack(rep_sum, rep_meta)))
    final_g = {n: rep_grad[n] for n in REPLICATED}
    final_g["conv_w"] = lax.dynamic_slice_in_dim(rep_grad["conv_w"], idx * conv_w.shape[2], conv_w.shape[2], axis=2)
    final_g["ada_b"] = jnp.stack([sum_rows(f"ada_b_grad_{l}", d_mod_all[l])[0] for l in range(DEPTH)])
    final_g["final_ada_b"] = sum_rows("final_ada_b_grad", d_mod_all[DEPTH])[0]
    final_g["ada_w"] = jnp.stack([
        matmul(f"ada_w_grad_{l}", c_act, lax.dynamic_slice_in_dim(d_mod_all[l], idx * ada_cols, ada_cols, axis=1), "tn")
        for l in range(DEPTH)])
    final_g["final_ada_w"] = matmul(
        "final_ada_w_grad", c_act, lax.dynamic_slice_in_dim(d_mod_all[DEPTH], idx * fin_cols, fin_cols, axis=1), "tn")

    shard_g = {n: [None] * DEPTH for n in GATHERED}

    def finish_scatter(l, group, after):
        sums = scatter_finish(f"scatter_finish_{group}{l}", scatter_handles[l, group], after)
        for n, gval in zip(dict(GROUPS)[group], sums):
            shard_g[n][l] = gval

    for l in reversed(range(DEPTH)):
        for group in ("ffn2", "mix", "ffn1"):
            if (l, group) != (0, "ffn1"):
                finish_scatter(l, group, rep_sum)

    delta, new_m, new_v = {}, {}, {}
    last = dict(GROUPS)["ffn1"]
    sharded = [n for n in GATHERED if n not in last] + ["ada_w", "final_ada_w", "conv_w"] + list(last)
    for n in sharded:
        if n == last[0]:
            finish_scatter(0, "ffn1", delta["w_in"])
        if n in GATHERED:
            final_g[n] = jnp.stack(shard_g[n])
        delta[n], new_m[n], new_v[n] = adamw(f"adamw_{n}", given[n], final_g[n], given["m_" + n], given["v_" + n])
    rep_all = [n for n in WEIGHTS if n not in sharded]
    packed = [_pack([src[n] for n in rep_all], LANES, 8, F32)[0]
              for src in (given, final_g, {n: given["m_" + n] for n in rep_all}, {n: given["v_" + n] for n in rep_all})]
    rep_meta_all = _pack([given[n] for n in rep_all], LANES, 8, F32)[1]
    for store, slab_out in zip((delta, new_m, new_v), adamw("adamw_replicated", *packed)):
        store.update(zip(rep_all, _unpack(slab_out, rep_meta_all)))

    loss = lax.psum(loss_row[0, 0], ("x", "y", "c"))
    return (loss, grad_x, *[final_g[n] for n in WEIGHTS], *[delta[n] for n in WEIGHTS], *[new_m[n] for n in WEIGHTS],
            *[new_v[n] for n in WEIGHTS])
```

```python
import functools

import numpy as np
import jax
import jax.numpy as jnp
from jax import lax
from jax.experimental import pallas as pl
from jax.experimental.pallas import tpu as pltpu

F32 = jnp.float32
BF16 = jnp.bfloat16
MESH = pl.DeviceIdType.MESH

N_DEV = 8
D = 1024
SEQ = 2048
B_LOC = 2
N_TOK = B_LOC * SEQ
DEPTH = 2
D_FF = 2816
RG_BLOCKS = 16
RG_C = 8.0
N_HEADS = 8
HEAD_DIM = 64
ATT_W = N_HEADS * HEAD_DIM
LANES = 128
EPS = 1e-6
ATT_SCALE = HEAD_DIM ** -0.5
CONV_K = 4

ADAM_LR = 0.001
ADAM_B1 = 0.9
ADAM_B2 = 0.999
ADAM_EPS = 1e-08
ADAM_WD = 0.01
ADAM_STEP = 10

EW_ROWS = 512
EW_ROWS_WIDE = 256
ATT_BLK = 512


def _pick_tile(dim, target):
    best = None
    for t in range(LANES, min(dim, target) + 1, LANES):
        if dim % t == 0:
            best = t
    return best if best is not None else dim


_DIMS = {"nn": (((1,), (0,)), ((), ())), "nt": (((1,), (1,)), ((), ())), "tn": (((0,), (0,)), ((), ()))}


def matmul(name, a_list, b_list, mode, out_dtype=F32, tm=1024, tn=512):
    if not isinstance(a_list, (list, tuple)):
        a_list, b_list = [a_list], [b_list]
    n = len(a_list)
    m_dim = a_list[0].shape[1] if mode == "tn" else a_list[0].shape[0]
    n_dim = b_list[0].shape[0] if mode == "nt" else b_list[0].shape[1]
    tm, tn = _pick_tile(m_dim, tm), _pick_tile(n_dim, tn)
    dims = _DIMS[mode]

    def body(*refs):
        o_ref = refs[-1]
        acc = None
        for a_ref, b_ref in zip(refs[:n], refs[n:2 * n]):
            d = lax.dot_general(a_ref[...].astype(BF16), b_ref[...].astype(BF16), dims, preferred_element_type=F32)
            acc = d if acc is None else acc + d
        o_ref[...] = acc.astype(o_ref.dtype)

    in_specs = []
    for a in a_list:
        if mode == "tn":
            in_specs.append(pl.BlockSpec((a.shape[0], tm), lambda i, j: (0, i)))
        else:
            in_specs.append(pl.BlockSpec((tm, a.shape[1]), lambda i, j: (i, 0)))
    for b in b_list:
        if mode == "nt":
            in_specs.append(pl.BlockSpec((tn, b.shape[1]), lambda i, j: (j, 0)))
        else:
            in_specs.append(pl.BlockSpec((b.shape[0], tn), lambda i, j: (0, j)))
    return pl.pallas_call(
        body, name=name, grid=(m_dim // tm, n_dim // tn), in_specs=in_specs,
        out_specs=pl.BlockSpec((tm, tn), lambda i, j: (i, j)),
        out_shape=jax.ShapeDtypeStruct((m_dim, n_dim), out_dtype),
        compiler_params=pltpu.CompilerParams(dimension_semantics=("parallel", "parallel")),
    )(*a_list, *b_list)


def _row_spec(w, tm):
    return pl.BlockSpec((tm, w), lambda i: (i, 0))


def _bparam_spec(w, tiles_per_batch):
    return pl.BlockSpec((None, 1, w), lambda i: (i // tiles_per_batch, 0, 0))


def _gparam_spec(shape):
    return pl.BlockSpec(shape, lambda i: (0, 0))


def ew_fwd(name, fn, rows, bparams, gparams, out_widths, out_dtypes, tm=EW_ROWS):
    n_rows = rows[0].shape[0]
    tm = min(tm, n_rows, SEQ)
    tpb = max(SEQ // tm, 1)
    nr, nb, ng = len(rows), len(bparams), len(gparams)

    def body(*refs):
        vals = [r[...] for r in refs[:nr + nb + ng]]
        outs = fn(*vals)
        if not isinstance(outs, (tuple, list)):
            outs = (outs,)
        for o_ref, o in zip(refs[nr + nb + ng:], outs):
            o_ref[...] = o.astype(o_ref.dtype)

    in_specs = ([_row_spec(r.shape[1], tm) for r in rows] + [_bparam_spec(p.shape[2], tpb) for p in bparams]
                + [_gparam_spec(g.shape) for g in gparams])
    outs = pl.pallas_call(
        body, name=name, grid=(n_rows // tm,), in_specs=in_specs,
        out_specs=[_row_spec(w, tm) for w in out_widths],
        out_shape=[jax.ShapeDtypeStruct((n_rows, w), dt) for w, dt in zip(out_widths, out_dtypes)],
        compiler_params=pltpu.CompilerParams(dimension_semantics=("parallel",)),
    )(*rows, *bparams, *gparams)
    return outs


def ew_bwd(name, fn, rows, bparams, gparams, cts, row_grad_dtypes, adds=(), tm=EW_ROWS):
    n_rows = rows[0].shape[0]
    tm = min(tm, n_rows, SEQ)
    tpb = max(SEQ // tm, 1)
    nr, nb, ng, nc = len(rows), len(bparams), len(gparams), len(cts)
    adds = list(adds) + [None] * (nr - len(adds))
    add_idx = [k for k in range(nr) if adds[k] is not None]
    want = [k for k in range(nr) if row_grad_dtypes[k] is not None]

    def body(*refs):
        pos = nr + nb + ng
        vals = [r[...] for r in refs[:pos]]
        ct_vals = [r[...].astype(F32) for r in refs[pos:pos + nc]]
        pos += nc
        add_vals = {k: refs[pos + q][...] for q, k in enumerate(add_idx)}
        pos += len(add_idx)
        out_refs = refs[pos:]
        f32_vals = [v.astype(F32) for v in vals]
        outs, vjp = jax.vjp(lambda *a: fn(*a), *f32_vals)
        single = not isinstance(outs, (tuple, list))
        grads = vjp(ct_vals[0].astype(outs.dtype) if single else tuple(c.astype(o.dtype) for c, o in zip(ct_vals, outs)))
        i = pl.program_id(0)
        q = 0
        for k in want:
            g = grads[k]
            if k in add_vals:
                g = g + add_vals[k].astype(F32)
            out_refs[q][...] = g.astype(out_refs[q].dtype)
            q += 1
        for k in range(nb):
            ref = out_refs[q]
            q += 1

            @pl.when(i % tpb == 0)
            def _():
                ref[...] = jnp.zeros_like(ref)

            ref[...] += grads[nr + k]
        for k in range(ng):
            ref = out_refs[q]
            q += 1

            @pl.when(i == 0)
            def _():
                ref[...] = jnp.zeros_like(ref)

            ref[...] += grads[nr + nb + k]

    in_specs = ([_row_spec(r.shape[1], tm) for r in rows] + [_bparam_spec(p.shape[2], tpb) for p in bparams]
                + [_gparam_spec(g.shape) for g in gparams] + [_row_spec(c.shape[1], tm) for c in cts]
                + [_row_spec(adds[k].shape[1], tm) for k in add_idx])
    out_specs = ([_row_spec(rows[k].shape[1], tm) for k in want] + [_bparam_spec(p.shape[2], tpb) for p in bparams]
                 + [_gparam_spec(g.shape) for g in gparams])
    out_shape = ([jax.ShapeDtypeStruct(rows[k].shape, row_grad_dtypes[k]) for k in want]
                 + [jax.ShapeDtypeStruct(p.shape, F32) for p in bparams] + [jax.ShapeDtypeStruct(g.shape, F32) for g in gparams])
    outs = pl.pallas_call(
        body, name=name, grid=(n_rows // tm,), in_specs=in_specs, out_specs=out_specs, out_shape=out_shape,
        compiler_params=pltpu.CompilerParams(dimension_semantics=("arbitrary",)),
    )(*rows, *bparams, *gparams, *cts, *[adds[k] for k in add_idx])
    d_rows = list(outs[:len(want)])
    d_b = list(outs[len(want):len(want) + nb])
    d_g = list(outs[len(want) + nb:])
    return d_rows, d_b, d_g


def f_norm_mod(x, shift, scale, gain):
    x = x.astype(F32)
    y = x * lax.rsqrt(jnp.mean(x * x, axis=-1, keepdims=True) + EPS)
    return (y * gain) * (1.0 + scale) + shift


def f_swiglu(a, b3):
    a = a.astype(F32)
    return (a * jax.nn.sigmoid(a)) * b3.astype(F32)


def f_resid(coef, x, y, gate):
    return x.astype(F32) + (coef * (1.0 + gate)) * y.astype(F32)


def f_rg_gates(pre_r, pre_i, xa, ba, bx, lam):
    r = jax.nn.sigmoid(pre_r + ba)
    i = jax.nn.sigmoid(pre_i + bx)
    softplus_neg_lam = jnp.maximum(-lam, 0.0) + jnp.log(1.0 + jnp.exp(-jnp.abs(lam)))
    log_a = (-RG_C) * r * softplus_neg_lam
    a = jnp.exp(log_a)
    u = jnp.sqrt(1.0 - a * a) * (i * xa)
    return a, u


def f_gelu_mul(gate, hs):
    g = gate.astype(F32)
    gelu = 0.5 * g * (1.0 + jnp.tanh(0.7978845608028654 * (g + 0.044715 * g * g * g)))
    return gelu * hs.astype(F32)


def f_log_sigmoid_bias(f, bf):
    z = f.astype(F32) + bf
    return jnp.minimum(z, 0.0) - jnp.log(1.0 + jnp.exp(-jnp.abs(z)))


def f_merge(mg, pa, pb, pc, merge_b):
    g = jax.nn.sigmoid(mg.astype(F32) + merge_b)
    return g[:, :D] * pa.astype(F32) + g[:, D:2 * D] * pb.astype(F32) + g[:, 2 * D:] * pc.astype(F32)


CONV_CB = 256
SCAN_CB = 512
SCAN_CHAINS = 4
CUM_RB = 512


def _shift_down(x, d):
    if d == 0:
        return x
    rows = lax.broadcasted_iota(jnp.int32, x.shape, 0)
    return jnp.where(rows >= d, pltpu.roll(x, d, axis=0), 0.0)


def _shift_up(x, d):
    if d == 0:
        return x
    s = x.shape[0]
    rows = lax.broadcasted_iota(jnp.int32, x.shape, 0)
    return jnp.where(rows < s - d, pltpu.roll(x, s - d, axis=0), 0.0)


def conv_fwd(x, w8, b):
    n, c = x.shape
    nb = n // SEQ

    def body(x_ref, w_ref, b_ref, y_ref):
        xv = x_ref[...]
        acc = jnp.broadcast_to(b_ref[...], xv.shape)
        for k in range(CONV_K):
            acc = acc + w_ref[k:k + 1, :] * _shift_down(xv, CONV_K - 1 - k)
        y_ref[...] = acc

    return pl.pallas_call(
        body, name="conv_fwd", grid=(c // CONV_CB, nb),
        in_specs=[pl.BlockSpec((SEQ, CONV_CB), lambda j, i: (i, j)), pl.BlockSpec((8, CONV_CB), lambda j, i: (0, j)),
                  pl.BlockSpec((1, CONV_CB), lambda j, i: (0, j))],
        out_specs=pl.BlockSpec((SEQ, CONV_CB), lambda j, i: (i, j)),
        out_shape=jax.ShapeDtypeStruct((n, c), F32),
        compiler_params=pltpu.CompilerParams(dimension_semantics=("parallel", "parallel")),
    )(x, w8, b)


def conv_bwd(x, w8, dy1, dy2):
    n, c = x.shape
    nb = n // SEQ

    def body(x_ref, w_ref, dy1_ref, dy2_ref, dx_ref, dwb_ref):
        xv = x_ref[...]
        dy = dy1_ref[...] + dy2_ref[...]
        dx = jnp.zeros_like(xv)
        parts = []
        for k in range(CONV_K):
            d = CONV_K - 1 - k
            dx = dx + w_ref[k:k + 1, :] * _shift_up(dy, d)
            parts.append(jnp.sum(dy * _shift_down(xv, d), axis=0, keepdims=True))
        parts.append(jnp.sum(dy, axis=0, keepdims=True))
        parts.append(jnp.zeros((8 - len(parts), xv.shape[1]), F32))
        dx_ref[...] = dx.astype(BF16)

        @pl.when(pl.program_id(1) == 0)
        def _():
            dwb_ref[...] = jnp.zeros_like(dwb_ref)

        dwb_ref[...] += jnp.concatenate(parts, axis=0)

    return pl.pallas_call(
        body, name="conv_bwd", grid=(c // CONV_CB, nb),
        in_specs=[pl.BlockSpec((SEQ, CONV_CB), lambda j, i: (i, j)), pl.BlockSpec((8, CONV_CB), lambda j, i: (0, j)),
                  pl.BlockSpec((SEQ, CONV_CB), lambda j, i: (i, j)), pl.BlockSpec((SEQ, CONV_CB), lambda j, i: (i, j))],
        out_specs=[pl.BlockSpec((SEQ, CONV_CB), lambda j, i: (i, j)), pl.BlockSpec((8, CONV_CB), lambda j, i: (0, j))],
        out_shape=[jax.ShapeDtypeStruct((n, c), BF16), jax.ShapeDtypeStruct((8, c), F32)],
        compiler_params=pltpu.CompilerParams(dimension_semantics=("parallel", "arbitrary")),
    )(x, w8, dy1, dy2)


def scan_fwd(a, u):
    n, c = a.shape
    q = SEQ // SCAN_CHAINS

    def body(a_ref, u_ref, h_ref, p_ref):
        def step(t, carry):
            hs, ps = carry
            new_h, new_p = [], []
            for k in range(SCAN_CHAINS):
                row = k * q + t
                av = a_ref[pl.ds(row, 1), :]
                hk = av * hs[k] + u_ref[pl.ds(row, 1), :]
                h_ref[pl.ds(row, 1), :] = hk
                new_h.append(hk)
                pk = av * ps[k]
                if k > 0:
                    p_ref[pl.ds(row, 1), :] = pk
                new_p.append(pk)
            return tuple(new_h), tuple(new_p)

        zero, one = jnp.zeros((1, SCAN_CB), F32), jnp.ones((1, SCAN_CB), F32)
        lax.fori_loop(0, q, step, ((zero,) * SCAN_CHAINS, (one,) * SCAN_CHAINS), unroll=4)
        for k in range(1, SCAN_CHAINS):
            rows = pl.ds(k * q, q)
            h_ref[rows, :] = h_ref[rows, :] + p_ref[rows, :] * h_ref[pl.ds(k * q - 1, 1), :]

    spec = pl.BlockSpec((SEQ, SCAN_CB), lambda i, j: (i, j))
    return pl.pallas_call(
        body, name="scan_fwd", grid=(n // SEQ, c // SCAN_CB), in_specs=[spec, spec], out_specs=spec,
        out_shape=jax.ShapeDtypeStruct((n, c), F32), scratch_shapes=[pltpu.VMEM((SEQ, SCAN_CB), F32)],
        compiler_params=pltpu.CompilerParams(dimension_semantics=("parallel", "parallel")),
    )(a, u)


def scan_bwd(a, h, g):
    n, c = a.shape
    q = SEQ // SCAN_CHAINS
    cb = SCAN_CB // 2

    def body(a_ref, h_ref, g_ref, da_ref, du_ref, r_ref):
        def step(j, carry):
            cs, rs = carry
            new_c, new_r = [], []
            for k in range(SCAN_CHAINS):
                row = k * q + (q - 1 - j)
                dh = g_ref[pl.ds(row, 1), :] + cs[k]
                du_ref[pl.ds(row, 1), :] = dh
                av = a_ref[pl.ds(row, 1), :]
                if k < SCAN_CHAINS - 1:
                    r_ref[pl.ds(row, 1), :] = rs[k]
                new_c.append(av * dh)
                new_r.append(av * rs[k])
            return tuple(new_c), tuple(new_r)

        zero, one = jnp.zeros((1, cb), F32), jnp.ones((1, cb), F32)
        lax.fori_loop(0, q, step, ((zero,) * SCAN_CHAINS, (one,) * SCAN_CHAINS), unroll=4)
        for k in reversed(range(SCAN_CHAINS - 1)):
            rows, nxt = pl.ds(k * q, q), pl.ds((k + 1) * q, 1)
            du_ref[rows, :] = du_ref[rows, :] + r_ref[rows, :] * (a_ref[nxt, :] * du_ref[nxt, :])
        da_ref[...] = du_ref[...] * _shift_down(h_ref[...], 1)

    spec = pl.BlockSpec((SEQ, cb), lambda i, j: (i, j))
    return pl.pallas_call(
        body, name="scan_bwd", grid=(n // SEQ, c // cb), in_specs=[spec, spec, spec], out_specs=[spec, spec],
        out_shape=[jax.ShapeDtypeStruct((n, c), F32), jax.ShapeDtypeStruct((n, c), F32)],
        scratch_shapes=[pltpu.VMEM((SEQ, cb), F32)],
        compiler_params=pltpu.CompilerParams(dimension_semantics=("parallel", "parallel")),
    )(a, h, g)


def _split3_dot(m, x):
    hi = x.astype(BF16)
    r1 = x - hi.astype(F32)
    mid = r1.astype(BF16)
    lo = (r1 - mid.astype(F32)).astype(BF16)
    dot = functools.partial(jnp.dot, preferred_element_type=F32)
    return dot(m, hi) + dot(m, mid) + dot(m, lo)


def seq_cumsum(name, xs, signs, reverse):
    n, w = xs[0].shape
    nx = len(xs)
    rb = min(CUM_RB, SEQ)

    def body(*refs):
        x = None
        for r, sg in zip(refs[:nx], signs):
            x = sg * r[...] if x is None else x + sg * r[...]
        q0 = pl.program_id(1) * rb
        row = q0 + lax.broadcasted_iota(jnp.int32, (rb, SEQ), 0)
        col = lax.broadcasted_iota(jnp.int32, (rb, SEQ), 1)
        tri = ((col >= row) if reverse else (col <= row)).astype(BF16)
        refs[nx][...] = _split3_dot(tri, x)

    return pl.pallas_call(
        body, name=name, grid=(n // SEQ, SEQ // rb),
        in_specs=[pl.BlockSpec((SEQ, w), lambda i, j: (i, 0)) for _ in xs],
        out_specs=pl.BlockSpec((rb, w), lambda i, j: (i * (SEQ // rb) + j, 0)),
        out_shape=jax.ShapeDtypeStruct((n, w), F32),
        compiler_params=pltpu.CompilerParams(dimension_semantics=("parallel", "parallel")),
    )(*xs)


N_PAIRS = N_HEADS // 2


def _dot_nt(a, b):
    return lax.dot_general(a, b, _DIMS["nt"], preferred_element_type=F32)


def _dot_tn(a, b):
    return lax.dot_general(a, b, _DIMS["tn"], preferred_element_type=F32)


def _dot_nn(a, b):
    return lax.dot_general(a, b, _DIMS["nn"], preferred_element_type=F32)


def _split2_dot(x, m):
    hi = x.astype(BF16)
    lo = (x - hi.astype(F32)).astype(BF16)
    return _dot_nn(hi, m) + _dot_nn(lo, m)


def _head_mask(j):
    lane = lax.broadcasted_iota(jnp.int32, (1, LANES), 1)
    return (lane // HEAD_DIM) == j


def _lane_pick(x, h):
    lane = lax.broadcasted_iota(jnp.int32, x.shape, 1)
    return jnp.sum(jnp.where(lane == h, x, 0.0), axis=1, keepdims=True)


def _lane_put(col, h):
    lane = lax.broadcasted_iota(jnp.int32, (col.shape[0], LANES), 1)
    return jnp.where(lane == h, col, 0.0)


def _softplus(z):
    return jnp.maximum(z, 0.0) + jnp.log(1.0 + jnp.exp(-jnp.abs(z)))


def _qkv_specs():
    return [pl.BlockSpec((SEQ, LANES), lambda b, p: (b, p)),
            pl.BlockSpec((SEQ, LANES), lambda b, p: (b, N_PAIRS + p)),
            pl.BlockSpec((SEQ, LANES), lambda b, p: (b, 2 * N_PAIRS + p))]


def _pair_spec():
    return pl.BlockSpec((SEQ, LANES), lambda b, p: (b, p))


def _below_diagonal(strictly):
    t = ATT_BLK
    row = lax.broadcasted_iota(jnp.int32, (t, t), 0)
    col = lax.broadcasted_iota(jnp.int32, (t, t), 1)
    return (row > col) if strictly else (row >= col)


def _over_key_blocks(qi, step, init, reverse):
    t = ATT_BLK
    q0 = pl.multiple_of(qi * t, t)

    def off_diagonal(kk, carry):
        ki = (qi - 1 - kk) if reverse else kk
        return step(pl.multiple_of(ki * t, t), carry, False)

    if reverse:
        return lax.fori_loop(0, qi, off_diagonal, step(q0, init, True))
    return step(q0, lax.fori_loop(0, qi, off_diagonal, init), True)


def _masked_q(qb, j):
    return (jnp.where(_head_mask(j), qb, 0.0) * ATT_SCALE).astype(BF16)


def sb_attn_fwd(qkv):
    n = qkv.shape[0]
    t = ATT_BLK

    def body(q_ref, k_ref, v_ref, o_ref, tot_ref):
        pair = pl.program_id(1)
        strict = _below_diagonal(True)
        later = strict.astype(BF16)

        @pl.when(pair == 0)
        def _():
            tot_ref[...] = jnp.zeros_like(tot_ref)

        def q_block(qi, _):
            q0 = pl.multiple_of(qi * t, t)
            qb = q_ref[pl.ds(q0, t), :]
            qms = [_masked_q(qb, j) for j in range(2)]

            def step(k0, carry, diagonal):
                kb = k_ref[pl.ds(k0, t), :].astype(BF16)
                vb = v_ref[pl.ds(k0, t), :].astype(BF16)
                heads = range(2)
                zs = [_dot_nt(qms[j], kb) for j in heads]
                sps = [_softplus(z) for z in zs]
                log_keeps = [(jnp.where(strict, -sp, 0.0) if diagonal else -sp) for sp in sps]
                right_l = [_split2_dot(lk, later) for lk in log_keeps]
                atts = [jnp.exp((zs[j] - sps[j]) + right_l[j] + carry[j][0]) for j in heads]
                if diagonal:
                    atts = [jnp.where(strict, att, 0.0) for att in atts]
                return tuple((carry[j][0] + jnp.sum(log_keeps[j], axis=1, keepdims=True),
                              carry[j][1] + _dot_nn(atts[j].astype(BF16), vb)) for j in heads)

            init = ((jnp.zeros((t, 1), F32), jnp.zeros((t, LANES), F32)),) * 2
            (tot0, acc0), (tot1, acc1) = _over_key_blocks(qi, step, init, reverse=True)
            o_ref[pl.ds(q0, t), :] = jnp.where(_head_mask(0), acc0, acc1)
            tot_ref[pl.ds(q0, t), :] += _lane_put(tot0, 2 * pair) + _lane_put(tot1, 2 * pair + 1)
            return 0

        lax.fori_loop(0, SEQ // t, q_block, 0)

    batch_spec = pl.BlockSpec((SEQ, LANES), lambda b, p: (b, 0))
    return pl.pallas_call(
        body, name="sb_attn_fwd", grid=(n // SEQ, N_PAIRS), in_specs=_qkv_specs(), out_specs=[_pair_spec(), batch_spec],
        out_shape=[jax.ShapeDtypeStruct((n, ATT_W), F32), jax.ShapeDtypeStruct((n, LANES), F32)],
        compiler_params=pltpu.CompilerParams(dimension_semantics=("parallel", "arbitrary")),
    )(qkv, qkv, qkv)


def sb_attn_bwd(qkv, tot, do):
    n = qkv.shape[0]
    t = ATT_BLK

    def body(q_ref, k_ref, v_ref, tot_ref, do_ref, dq_ref, dk_ref, dv_ref, dk_acc, dv_acc):
        pair = pl.program_id(1)
        strict = _below_diagonal(True)
        upto = jnp.logical_not(strict).astype(BF16)
        dk_acc[...] = jnp.zeros_like(dk_acc)
        dv_acc[...] = jnp.zeros_like(dv_acc)

        def q_block(qi, _):
            q0 = pl.multiple_of(qi * t, t)
            qb = q_ref[pl.ds(q0, t), :]
            tot_q = tot_ref[pl.ds(q0, t), :]
            dob = do_ref[pl.ds(q0, t), :].astype(F32)
            qms = [_masked_q(qb, j) for j in range(2)]
            doms = [jnp.where(_head_mask(j), dob, 0.0).astype(BF16) for j in range(2)]
            totals = [_lane_pick(tot_q, 2 * pair + j) for j in range(2)]

            def step(k0, carry, diagonal):
                kb = k_ref[pl.ds(k0, t), :].astype(BF16)
                vb = v_ref[pl.ds(k0, t), :].astype(BF16)
                heads = range(2)
                zs = [_dot_nt(qms[j], kb) for j in heads]
                d_atts = [_dot_nt(doms[j], vb) for j in heads]
                sps = [_softplus(z) for z in zs]
                log_keeps = [(jnp.where(strict, -sp, 0.0) if diagonal else -sp) for sp in sps]
                log_betas = [z - sp for z, sp in zip(zs, sps)]
                left_l = [_split2_dot(lk, upto) for lk in log_keeps]
                atts = [jnp.exp(log_betas[j] + (totals[j] - (carry[j][0] + left_l[j]))) for j in heads]
                if diagonal:
                    atts = [jnp.where(strict, att, 0.0) for att in atts]
                gs = [att * d_att for att, d_att in zip(atts, d_atts)]
                dv = _dot_tn(atts[0].astype(BF16), doms[0]) + _dot_tn(atts[1].astype(BF16), doms[1])
                left_g = [_dot_nn(g.astype(BF16), upto) for g in gs]
                dzs = [gs[j] - jnp.exp(log_betas[j]) * (carry[j][1] + left_g[j]) for j in heads]
                if diagonal:
                    dzs = [jnp.where(strict, dz, 0.0) for dz in dzs]
                dzs = [dz.astype(BF16) for dz in dzs]
                dk = _dot_tn(dzs[0], qms[0]) + _dot_tn(dzs[1], qms[1])
                dk_acc[pl.ds(k0, t), :] += dk
                dv_acc[pl.ds(k0, t), :] += dv
                return tuple((carry[j][0] + jnp.sum(log_keeps[j], axis=1, keepdims=True),
                              carry[j][1] + jnp.sum(gs[j], axis=1, keepdims=True), carry[j][2] + _dot_nn(dzs[j], kb))
                             for j in heads)

            zero = jnp.zeros((t, 1), F32)
            init = ((zero, zero, jnp.zeros((t, LANES), F32)),) * 2
            (_, _, dq0), (_, _, dq1) = _over_key_blocks(qi, step, init, reverse=False)
            dq_ref[pl.ds(q0, t), :] = (jnp.where(_head_mask(0), dq0, dq1) * ATT_SCALE).astype(BF16)
            return 0

        lax.fori_loop(0, SEQ // t, q_block, 0)
        dk_ref[...] = dk_acc[...].astype(BF16)
        dv_ref[...] = dv_acc[...].astype(BF16)

    out = jax.ShapeDtypeStruct((n, ATT_W), BF16)
    batch_spec = pl.BlockSpec((SEQ, LANES), lambda b, p: (b, 0))
    return pl.pallas_call(
        body, name="sb_attn_bwd", grid=(n // SEQ, N_PAIRS), in_specs=_qkv_specs() + [batch_spec, _pair_spec()],
        out_specs=[_pair_spec()] * 3, out_shape=[out, out, out],
        scratch_shapes=[pltpu.VMEM((SEQ, LANES), F32), pltpu.VMEM((SEQ, LANES), F32)],
        compiler_params=pltpu.CompilerParams(dimension_semantics=("parallel", "parallel")),
    )(qkv, qkv, qkv, tot, do)


NEG_BIG = -1e30


def fox_attn_fwd(qkv, cum, cum_t):
    n = qkv.shape[0]
    t = ATT_BLK

    def body(q_ref, k_ref, v_ref, cum_ref, cumt_ref, o_ref, lse_ref):
        pair = pl.program_id(1)
        causal = _below_diagonal(False)

        @pl.when(pair == 0)
        def _():
            lse_ref[...] = jnp.zeros_like(lse_ref)

        def q_block(qi, _):
            q0 = pl.multiple_of(qi * t, t)
            qb = q_ref[pl.ds(q0, t), :]
            cum_q = cum_ref[pl.ds(q0, t), :]
            qms = [_masked_q(qb, j) for j in range(2)]
            cqs = [_lane_pick(cum_q, 2 * pair + j) for j in range(2)]

            def step(k0, carry, diagonal):
                kb = k_ref[pl.ds(k0, t), :].astype(BF16)
                vb = v_ref[pl.ds(k0, t), :].astype(BF16)
                heads = range(2)
                zs = [_dot_nt(qms[j], kb) + cqs[j] - cumt_ref[pl.ds(2 * pair + j, 1), pl.ds(k0, t)] for j in heads]
                if diagonal:
                    zs = [jnp.where(causal, z, NEG_BIG) for z in zs]
                m_new = [jnp.maximum(carry[j][0], jnp.max(zs[j], axis=1, keepdims=True)) for j in heads]
                ps = [jnp.exp(zs[j] - m_new[j]) for j in heads]
                alphas = [jnp.exp(carry[j][0] - m_new[j]) for j in heads]
                return tuple((m_new[j], alphas[j] * carry[j][1] + jnp.sum(ps[j], axis=1, keepdims=True),
                              alphas[j] * carry[j][2] + _dot_nn(ps[j].astype(BF16), vb)) for j in heads)

            init = ((jnp.full((t, 1), NEG_BIG, F32), jnp.zeros((t, 1), F32), jnp.zeros((t, LANES), F32)),) * 2
            (m0, l0, acc0), (m1, l1, acc1) = _over_key_blocks(qi, step, init, reverse=False)
            o_ref[pl.ds(q0, t), :] = jnp.where(_head_mask(0), acc0 / l0, acc1 / l1)
            lse_ref[pl.ds(q0, t), :] += _lane_put(m0 + jnp.log(l0), 2 * pair) + _lane_put(m1 + jnp.log(l1), 2 * pair + 1)
            return 0

        lax.fori_loop(0, SEQ // t, q_block, 0)

    batch_spec = pl.BlockSpec((SEQ, LANES), lambda b, p: (b, 0))
    return pl.pallas_call(
        body, name="fox_attn_fwd", grid=(n // SEQ, N_PAIRS),
        in_specs=_qkv_specs() + [batch_spec, pl.BlockSpec((None, N_HEADS, SEQ), lambda b, p: (b, 0, 0))],
        out_specs=[_pair_spec(), batch_spec],
        out_shape=[jax.ShapeDtypeStruct((n, ATT_W), F32), jax.ShapeDtypeStruct((n, LANES), F32)],
        compiler_params=pltpu.CompilerParams(dimension_semantics=("parallel", "arbitrary")),
    )(qkv, qkv, qkv, cum, cum_t)


def fox_attn_bwd(qkv, cum, cum_t, lse, o, do):
    n = qkv.shape[0]
    t = ATT_BLK

    def body(q_ref, k_ref, v_ref, cum_ref, cumt_ref, lse_ref, o_ref, do_ref, dq_ref, dk_ref, dv_ref, dcq_ref, dck_ref,
             dk_acc, dv_acc):
        pair = pl.program_id(1)
        causal = _below_diagonal(False)
        dk_acc[...] = jnp.zeros_like(dk_acc)
        dv_acc[...] = jnp.zeros_like(dv_acc)

        @pl.when(pair == 0)
        def _():
            dcq_ref[...] = jnp.zeros_like(dcq_ref)
            dck_ref[...] = jnp.zeros_like(dck_ref)

        def q_block(qi, _):
            q0 = pl.multiple_of(qi * t, t)
            qb = q_ref[pl.ds(q0, t), :]
            ob = o_ref[pl.ds(q0, t), :]
            dob = do_ref[pl.ds(q0, t), :].astype(F32)
            cum_q = cum_ref[pl.ds(q0, t), :]
            lse_q = lse_ref[pl.ds(q0, t), :]
            qms = [_masked_q(qb, j) for j in range(2)]
            dom32 = [jnp.where(_head_mask(j), dob, 0.0) for j in range(2)]
            doms = [d.astype(BF16) for d in dom32]
            deltas = [jnp.sum(d * ob, axis=1, keepdims=True) for d in dom32]
            cqs = [_lane_pick(cum_q, 2 * pair + j) for j in range(2)]
            lqs = [_lane_pick(lse_q, 2 * pair + j) for j in range(2)]

            def step(k0, carry, diagonal):
                kb = k_ref[pl.ds(k0, t), :].astype(BF16)
                vb = v_ref[pl.ds(k0, t), :].astype(BF16)
                heads = range(2)
                zs = [_dot_nt(qms[j], kb) + cqs[j] - cumt_ref[pl.ds(2 * pair + j, 1), pl.ds(k0, t)] for j in heads]
                d_ps = [_dot_nt(doms[j], vb) for j in heads]
                if diagonal:
                    zs = [jnp.where(causal, z, NEG_BIG) for z in zs]
                ps = [jnp.exp(zs[j] - lqs[j]) for j in heads]
                dv_acc[pl.ds(k0, t), :] += _dot_tn(ps[0].astype(BF16), doms[0]) + _dot_tn(ps[1].astype(BF16), doms[1])
                dzs = [ps[j] * (d_ps[j] - deltas[j]) for j in heads]
                dzb = [dz.astype(BF16) for dz in dzs]
                dk_acc[pl.ds(k0, t), :] += _dot_tn(dzb[0], qms[0]) + _dot_tn(dzb[1], qms[1])
                for j in heads:
                    dck_ref[pl.ds(2 * pair + j, 1), pl.ds(k0, t)] += jnp.sum(dzs[j], axis=0, keepdims=True)
                return tuple((carry[j][0] + _dot_nn(dzb[j], kb), carry[j][1] + jnp.sum(dzs[j], axis=1, keepdims=True))
                             for j in heads)

            init = ((jnp.zeros((t, LANES), F32), jnp.zeros((t, 1), F32)),) * 2
            (dq0, dcq0), (dq1, dcq1) = _over_key_blocks(qi, step, init, reverse=False)
            dq_ref[pl.ds(q0, t), :] = (jnp.where(_head_mask(0), dq0, dq1) * ATT_SCALE).astype(BF16)
            dcq_ref[pl.ds(q0, t), :] += _lane_put(dcq0, 2 * pair) + _lane_put(dcq1, 2 * pair + 1)
            return 0

        lax.fori_loop(0, SEQ // t, q_block, 0)
        dk_ref[...] = dk_acc[...].astype(BF16)
        dv_ref[...] = dv_acc[...].astype(BF16)

    batch_spec = pl.BlockSpec((SEQ, LANES), lambda b, p: (b, 0))
    t_spec = pl.BlockSpec((None, N_HEADS, SEQ), lambda b, p: (b, 0, 0))
    out = jax.ShapeDtypeStruct((n, ATT_W), BF16)
    return pl.pallas_call(
        body, name="fox_attn_bwd", grid=(n // SEQ, N_PAIRS),
        in_specs=_qkv_specs() + [batch_spec, t_spec, batch_spec, _pair_spec(), _pair_spec()],
        out_specs=[_pair_spec()] * 3 + [batch_spec, t_spec],
        scratch_shapes=[pltpu.VMEM((SEQ, LANES), F32), pltpu.VMEM((SEQ, LANES), F32)],
        out_shape=[out, out, out, jax.ShapeDtypeStruct((n, LANES), F32), jax.ShapeDtypeStruct((n // SEQ, N_HEADS, SEQ), F32)],
        compiler_params=pltpu.CompilerParams(dimension_semantics=("parallel", "arbitrary")),
    )(qkv, qkv, qkv, cum, cum_t, lse, o, do)


_HBM = pl.BlockSpec(memory_space=pl.ANY)


def _my_place():
    return lax.axis_index("x"), lax.axis_index("y"), lax.axis_index("c")


def my_index():
    mx, my, mc = _my_place()
    return 4 * mx + 2 * my + mc


def all_gather(name, xs):
    single = not isinstance(xs, (list, tuple))
    xs = [xs] if single else list(xs)
    na = len(xs)

    def body(*refs):
        x_refs, out_refs = refs[:na], refs[na:2 * na]
        send_sems, recv_sems, local_sems = refs[2 * na:]
        mx, my, mc = _my_place()
        me, sibling = (mx, my, mc), (mx, my, 1 - mc)
        chips = [(1 - mx, my), (mx, 1 - my), (1 - mx, 1 - my)]

        def slot(a, px, py, pc):
            return out_refs[a].at[4 * px + 2 * py + pc]

        def copy(a, k, block, to, src=None):
            return pltpu.make_async_remote_copy(
                src_ref=slot(a, *block) if src is None else src, dst_ref=slot(a, *block),
                send_sem=send_sems.at[7 * a + k], recv_sem=recv_sems.at[7 * a + k], device_id=to, device_id_type=MESH)

        mine = [pltpu.make_async_copy(x_refs[a], slot(a, *me), local_sems.at[a]) for a in range(na)]
        for cp in mine:
            cp.start()
        first = []
        for j, chip in enumerate(chips):
            first += [copy(a, 1 + j, me, (*chip, mc), src=x_refs[a]) for a in range(na)]
        first += [copy(a, 0, me, sibling, src=x_refs[a]) for a in range(na)]
        for cp in first:
            cp.start()
        passed = []
        for j, chip in enumerate(chips):
            for a in range(na):
                copy(a, 1 + j, (*chip, mc), me).wait_recv()
                passed.append(copy(a, 4 + j, (*chip, mc), sibling))
                passed[-1].start()
        for a in range(na):
            copy(a, 0, sibling, me).wait_recv()
        for j, chip in enumerate(chips):
            for a in range(na):
                copy(a, 4 + j, (*chip, 1 - mc), me).wait_recv()
        for cp in first + passed:
            cp.wait_send()
        for cp in mine:
            cp.wait()

    outs = pl.pallas_call(
        body, name=name, in_specs=[_HBM] * na, out_specs=[_HBM] * na,
        out_shape=[jax.ShapeDtypeStruct((N_DEV,) + x.shape, x.dtype) for x in xs],
        scratch_shapes=[pltpu.SemaphoreType.DMA((7 * na,)), pltpu.SemaphoreType.DMA((7 * na,)), pltpu.SemaphoreType.DMA((na,))],
    )(*xs)
    return outs[0] if single else list(outs)


_SEM = pl.BlockSpec(memory_space=pltpu.SEMAPHORE)
_HBM_ONLY = pl.BlockSpec(memory_space=pltpu.HBM)
_EFFECT = pltpu.SideEffectType.DATAFLOW_SIDE_EFFECTING
N_PEERS = N_DEV


def _peers():
    mx, my, mc = _my_place()
    return [((1 - mx) if (r >> 2) & 1 else mx, (1 - my) if (r >> 1) & 1 else my, (1 - mc) if r & 1 else mc)
            for r in range(N_DEV)]


def _exchange_copies(scatter, x_refs, land_refs, send_sems, recv_sems):
    me = my_index()
    copies = []
    for a, (x_ref, land_ref) in enumerate(zip(x_refs, land_refs)):
        for r, (px, py, pc) in enumerate(_peers()):
            src = x_ref.at[4 * px + 2 * py + pc] if scatter else x_ref
            dst = land_ref.at[r] if scatter else land_ref.at[me]
            copies.append(pltpu.make_async_remote_copy(
                src_ref=src, dst_ref=dst, send_sem=send_sems.at[N_PEERS * a + r], recv_sem=recv_sems.at[N_PEERS * a + r],
                device_id=(px, py, pc), device_id_type=MESH))
    return copies


def exchange_start(name, xs, scatter):
    na = len(xs)
    lands = [lax.empty((N_PEERS,) + x.shape[1:] if scatter else (N_DEV,) + x.shape, x.dtype) for x in xs]

    def body(*refs):
        x_refs, land_refs, send_sems, recv_sems = refs[:na], refs[na:2 * na], refs[2 * na], refs[2 * na + 1]
        token = refs[-1]
        for cp in _exchange_copies(scatter, x_refs, land_refs, send_sems, recv_sems):
            cp.start()
        token[...] = jnp.zeros_like(token)

    outs = pl.pallas_call(
        body, name=name,
        out_shape=(pltpu.SemaphoreType.DMA((N_PEERS * na,)), pltpu.SemaphoreType.DMA((N_PEERS * na,)),
                   *[pltpu.HBM(x.shape, x.dtype) for x in xs], *[pltpu.HBM(l.shape, l.dtype) for l in lands],
                   jax.ShapeDtypeStruct((8, LANES), F32)),
        in_specs=[_HBM_ONLY] * (2 * na),
        out_specs=(_SEM, _SEM, *[_HBM_ONLY] * (2 * na), pl.BlockSpec(memory_space=pltpu.VMEM)),
        input_output_aliases={i: 2 + i for i in range(2 * na)},
        compiler_params=pltpu.CompilerParams(has_side_effects=_EFFECT),
    )(*[pltpu.with_memory_space_constraint(x, pltpu.HBM) for x in xs],
      *[pltpu.with_memory_space_constraint(l, pltpu.HBM) for l in lands])
    return (scatter, outs[0], outs[1], outs[2:2 + na], outs[2 + na:2 + 2 * na]), outs[-1]


def exchange_finish(name, handle, after):
    scatter, send_sems, recv_sems, xs, lands = handle
    na = len(xs)

    def body(*refs):
        x_refs, land_refs, send_ref, recv_ref = refs[:na], refs[na:2 * na], refs[2 * na], refs[2 * na + 1]
        for cp in _exchange_copies(scatter, x_refs, land_refs, send_ref, recv_ref):
            cp.wait_send()
            cp.wait_recv()

    outs = pl.pallas_call(
        body, name=name,
        out_shape=tuple(pltpu.HBM(t.shape, t.dtype) for t in list(xs) + list(lands)),
        in_specs=[_HBM_ONLY] * (2 * na) + [_SEM, _SEM, _HBM],
        out_specs=tuple([_HBM_ONLY] * (2 * na)),
        input_output_aliases={i: i for i in range(2 * na)},
        compiler_params=pltpu.CompilerParams(has_side_effects=_EFFECT),
    )(*xs, *lands, send_sems, recv_sems, after)
    return list(outs[:na]), list(outs[na:])


def _pick_rows(n, target):
    best = None
    for t in range(8, min(n, target) + 1, 8):
        if n % t == 0:
            best = t
    return best if best is not None else n


def add_blocks(name, parts, out_dtype, rows=512):
    r, w = parts[0].shape
    tr = _pick_rows(r, rows)

    def body(*refs):
        acc = refs[0][...].astype(F32)
        for ref in refs[1:-1]:
            acc = acc + ref[...].astype(F32)
        refs[-1][...] = acc.astype(refs[-1].dtype)

    spec = pl.BlockSpec((tr, w), lambda i: (i, 0))
    return pl.pallas_call(
        body, name=name, grid=(r // tr,), in_specs=[spec] * len(parts), out_specs=spec,
        out_shape=jax.ShapeDtypeStruct((r, w), out_dtype),
        compiler_params=pltpu.CompilerParams(dimension_semantics=("parallel",)),
    )(*parts)


def sum_rows(name, x):
    def body(x_ref, o_ref):
        o_ref[...] = jnp.sum(x_ref[...], axis=0, keepdims=True)

    return pl.pallas_call(body, name=name, out_shape=jax.ShapeDtypeStruct((1, x.shape[1]), F32))(x)


def gather_start(name, blocks):
    return exchange_start(name, blocks, scatter=False)


def gather_finish(name, handle, after):
    return exchange_finish(name, handle, after)[1]


def scatter_start(name, g8s):
    return exchange_start(name, g8s, scatter=True)


def scatter_finish(name, handle, after):
    _, lands = exchange_finish(name, handle, after)
    outs = []
    for a, land in enumerate(lands):
        w = land.shape[-1]
        outs.append(add_blocks(f"{name}_sum{a}", [land[k].reshape(-1, w) for k in range(N_PEERS)], F32).reshape(land.shape[1:]))
    return outs


def _pack(arrays, width, row_mult, dtype, lead=0):
    parts, metas = [], []
    for a in arrays:
        lead_shape = a.shape[:lead]
        size = int(np.prod(a.shape[lead:]))
        chunk = row_mult * width
        padded = -(-size // chunk) * chunk
        flat = a.astype(dtype).reshape(lead_shape + (size,))
        if padded != size:
            flat = jnp.pad(flat, [(0, 0)] * lead + [(0, padded - size)])
        parts.append(flat.reshape(lead_shape + (padded // width, width)))
        metas.append((a.shape[lead:], size, padded // width))
    return jnp.concatenate(parts, axis=lead), metas


def _unpack(slab, metas, lead=0):
    out, r0 = [], 0
    for shape, size, rows in metas:
        part = lax.slice_in_dim(slab, r0, r0 + rows, axis=lead)
        lead_shape = part.shape[:lead]
        flat = part.reshape(lead_shape + (rows * part.shape[-1],))
        out.append(lax.slice_in_dim(flat, 0, size, axis=lead).reshape(lead_shape + tuple(shape)))
        r0 += rows
    return out


def _f_adamw(w, g, m, v):
    m = ADAM_B1 * m + (1.0 - ADAM_B1) * g
    v = ADAM_B2 * v + (1.0 - ADAM_B2) * (g * g)
    m_hat = m / (1.0 - ADAM_B1 ** ADAM_STEP)
    v_hat = v / (1.0 - ADAM_B2 ** ADAM_STEP)
    delta = (-ADAM_LR) * (m_hat / (jnp.sqrt(v_hat) + ADAM_EPS) + ADAM_WD * w)
    return delta, m, v


def adamw(name, w, g, m, v):
    shape = w.shape
    w2 = shape[-1]
    flat = [a.reshape(-1, w2) for a in (w, g, m, v)]
    tm = _pick_rows(flat[0].shape[0], 256)
    outs = ew_fwd(name, _f_adamw, flat, [], [], [w2] * 3, [F32] * 3, tm=tm)
    return [o.reshape(shape) for o in outs]


WEIGHTS = ["ffn1_norm", "ffn1_w1", "ffn1_w3", "ffn1_w2", "mix_norm", "w_in", "conv_w", "conv_b", "rg_wa", "rg_ba", "rg_wx",
           "rg_bx", "rg_lam", "fox_bf", "merge_b", "w_rg", "w_sb", "w_fox", "w_o", "ffn2_norm", "ffn2_w1", "ffn2_w3",
           "ffn2_w2", "ada_w", "ada_b", "final_norm", "final_ada_w", "final_ada_b"]
GATHERED = {"ffn1_w1": 2, "ffn1_w3": 2, "ffn1_w2": 1, "w_in": 2, "w_rg": 1, "w_sb": 2, "w_fox": 2, "w_o": 1,
            "ffn2_w1": 2, "ffn2_w3": 2, "ffn2_w2": 1}
REPLICATED = ["ffn1_norm", "mix_norm", "conv_b", "rg_wa", "rg_ba", "rg_wx", "rg_bx", "rg_lam", "fox_bf", "merge_b",
              "ffn2_norm", "final_norm"]
GROUPS = (("ffn1", ("ffn1_w1", "ffn1_w3", "ffn1_w2")), ("mix", ("w_in", "w_rg", "w_sb", "w_fox", "w_o")),
          ("ffn2", ("ffn2_w1", "ffn2_w3", "ffn2_w2")))
IN_CUTS = (0, 1024, 2048, 3584, 5120, 5128, 8200)


def _unshard(g, axis):
    g = jnp.moveaxis(g, 0, axis)
    shape = g.shape
    return g.reshape(shape[:axis] + (shape[axis] * shape[axis + 1],) + shape[axis + 2:])


def _reshard(full, axis):
    shape = full.shape
    g = full.reshape(shape[:axis] + (N_DEV, shape[axis] // N_DEV) + shape[axis + 1:])
    return jnp.moveaxis(g, axis, 0)


def _block_diag(w):
    nb, bd, _ = w.shape
    eye = jnp.eye(nb, dtype=bool)[:, None, :, None]
    return jnp.where(eye, w[:, :, None, :], 0.0).reshape(nb * bd, nb * bd)


def _diag_blocks(m, nb=RG_BLOCKS):
    bd = m.shape[0] // nb
    return jnp.stack([m[k * bd:(k + 1) * bd, k * bd:(k + 1) * bd] for k in range(nb)])


def _pad_lanes(a, width=LANES):
    return jnp.pad(a, [(0, 0)] * (a.ndim - 1) + [(0, width - a.shape[-1])])


def _bp(m, k, which):
    return m[:, k, which][:, None, :]


def _f_silu(c):
    return c * jax.nn.sigmoid(c)


def _f_add_bias(a, b):
    return a + b


FFN_TM = 512
FFN_TN = 1408


def ffn_up(name, h, w1, w3):
    n, k = h.shape
    f = w1.shape[1]
    tm, tn = min(FFN_TM, n), _pick_tile(f, FFN_TN)

    def body(h_ref, w1_ref, w3_ref, a_ref, b_ref, s_ref):
        hv = h_ref[...]
        a = jnp.dot(hv, w1_ref[...], preferred_element_type=F32)
        b = jnp.dot(hv, w3_ref[...], preferred_element_type=F32)
        a_ref[...] = a.astype(BF16)
        b_ref[...] = b.astype(BF16)
        s_ref[...] = ((a * jax.nn.sigmoid(a)) * b).astype(BF16)

    wspec = pl.BlockSpec((k, tn), lambda i, j: (0, j))
    ospec = pl.BlockSpec((tm, tn), lambda i, j: (i, j))
    out = jax.ShapeDtypeStruct((n, f), BF16)
    return pl.pallas_call(
        body, name=name, grid=(n // tm, f // tn), in_specs=[pl.BlockSpec((tm, k), lambda i, j: (i, 0)), wspec, wspec],
        out_specs=[ospec] * 3, out_shape=[out] * 3,
        compiler_params=pltpu.CompilerParams(dimension_semantics=("parallel", "parallel")),
    )(h, w1, w3)


def ffn_down_dx(name, dy, w2, a, b):
    n, k = dy.shape
    f = w2.shape[0]
    tm, tn = min(FFN_TM, n), _pick_tile(f, FFN_TN)

    def body(dy_ref, w2_ref, a_ref, b_ref, da_ref, db_ref):
        ds = _dot_nt(dy_ref[...], w2_ref[...])
        av = a_ref[...].astype(F32)
        sig = jax.nn.sigmoid(av)
        da_ref[...] = (ds * b_ref[...].astype(F32) * (sig * (1.0 + av * (1.0 - sig)))).astype(BF16)
        db_ref[...] = (ds * (av * sig)).astype(BF16)

    ospec = pl.BlockSpec((tm, tn), lambda i, j: (i, j))
    out = jax.ShapeDtypeStruct((n, f), BF16)
    return pl.pallas_call(
        body, name=name, grid=(n // tm, f // tn),
        in_specs=[pl.BlockSpec((tm, k), lambda i, j: (i, 0)), pl.BlockSpec((tn, k), lambda i, j: (j, 0)), ospec, ospec],
        out_specs=[ospec] * 2, out_shape=[out] * 2,
        compiler_params=pltpu.CompilerParams(dimension_semantics=("parallel", "parallel")),
    )(dy, w2, a, b)


def _ffn_fwd(tag, x, shift, scale, gate, gain, w1, w3, w2):
    h = ew_fwd(f"ffn_norm_{tag}", f_norm_mod, [x], [shift, scale], [gain], [D], [BF16])[0]
    a, b3, s = ffn_up(f"ffn_up_{tag}", h, w1, w3)
    y = matmul(f"ffn_down_{tag}", s, w2, "nn", tm=1024)
    xo = ew_fwd(f"ffn_res_{tag}", functools.partial(f_resid, 0.5), [x, y], [gate], [], [D], [F32])[0]
    return xo, (x, h, a, b3, s, y)


def _ffn_bwd(tag, dxo, saved, shift, scale, gate, gain, w1, w3, w2):
    x, h, a, b3, s, y = saved
    (dy,), (dgate,), _ = ew_bwd(f"ffn_res_bwd_{tag}", functools.partial(f_resid, 0.5), [x, y], [gate], [], [dxo], [None, BF16])
    da, db3 = ffn_down_dx(f"ffn_down_dx_{tag}", dy, w2, a, b3)
    dw2 = matmul(f"ffn_dw2_{tag}", s, dy, "tn", tm=1408, tn=256)
    dw1 = matmul(f"ffn_dw1_{tag}", h, da, "tn", tm=1024, tn=256)
    dw3 = matmul(f"ffn_dw3_{tag}", h, db3, "tn", tm=1024, tn=256)
    dh = matmul(f"ffn_up_dx_{tag}", [da, db3], [w1, w3], "nt", tm=1024)
    (dx,), (dshift, dscale), (dgain,) = ew_bwd(f"ffn_norm_bwd_{tag}", f_norm_mod, [x], [shift, scale], [gain], [dh], [F32],
                                               adds=[dxo])
    return dx, (dshift, dscale, dgate), dgain, dw1, dw3, dw2


def _mixer_fwd(tag, x, shift, scale, gate, p):
    h = ew_fwd(f"mix_norm_{tag}", f_norm_mod, [x], [shift, scale], [p["gain"]], [D], [BF16])[0]
    rgx = matmul(f"in_rgx_{tag}", h, p["w_rgx"], "nn")
    rgate = matmul(f"in_gate_{tag}", h, p["w_gate"], "nn")
    sbqkv = matmul(f"in_sb_{tag}", h, p["w_sbqkv"], "nn")
    foxqkv = matmul(f"in_fox_{tag}", h, p["w_foxqkv"], "nn")
    ff = matmul(f"in_forget_{tag}", h, p["w_f"], "nn")
    mg = matmul(f"in_merge_{tag}", h, p["w_merge"], "nn")
    xa = conv_fwd(rgx, p["conv_w8"], p["conv_b"])
    pre_r = matmul(f"rg_a_{tag}", xa, p["wa_bd"], "nn")
    pre_i = matmul(f"rg_x_{tag}", xa, p["wx_bd"], "nn")
    a, u = ew_fwd(f"rg_gates_{tag}", f_rg_gates, [pre_r, pre_i, xa], [], [p["ba"], p["bx"], p["lam"]], [D, D], [F32, F32],
                  tm=EW_ROWS_WIDE)
    hs = scan_fwd(a, u)
    ya = ew_fwd(f"rg_out_{tag}", f_gelu_mul, [rgate, hs], [], [], [D], [BF16])[0]
    yb, sb_tot = sb_attn_fwd(sbqkv)
    lf = ew_fwd(f"fox_logf_{tag}", f_log_sigmoid_bias, [ff], [], [p["bf"]], [LANES], [F32])[0]
    cum = seq_cumsum(f"fox_cum_{tag}", [lf], [1.0], False)
    cum_t = cum.reshape(-1, SEQ, LANES)[:, :, :N_HEADS].transpose(0, 2, 1)
    yc, lse = fox_attn_fwd(foxqkv, cum, cum_t)
    pa = matmul(f"out_rg_{tag}", ya, p["w_rg"], "nn")
    pb = matmul(f"out_sb_{tag}", yb, p["w_sb"], "nn")
    pc = matmul(f"out_fox_{tag}", yc, p["w_fox"], "nn")
    mixed = ew_fwd(f"merge_{tag}", f_merge, [mg, pa, pb, pc], [], [p["merge_b"]], [D], [BF16], tm=EW_ROWS_WIDE)[0]
    y = matmul(f"out_o_{tag}", mixed, p["w_o"], "nn")
    xo = ew_fwd(f"mix_res_{tag}", functools.partial(f_resid, 1.0), [x, y], [gate], [], [D], [F32])[0]
    saved = dict(x=x, h=h, rgx=rgx, rgate=rgate, sbqkv=sbqkv, foxqkv=foxqkv, ff=ff, mg=mg, xa=xa, pre_r=pre_r, pre_i=pre_i,
                 a=a, hs=hs, ya=ya, yb=yb, sb_tot=sb_tot, cum=cum, cum_t=cum_t, yc=yc, lse=lse, pa=pa, pb=pb, pc=pc,
                 mixed=mixed, y=y)
    return xo, saved


def _mixer_bwd(tag, dxo, s, shift, scale, gate, p):
    (dy,), (dgate,), _ = ew_bwd(f"mix_res_bwd_{tag}", functools.partial(f_resid, 1.0), [s["x"], s["y"]], [gate], [], [dxo],
                                [None, BF16])
    dmixed = matmul(f"out_o_dx_{tag}", dy, p["w_o"], "nt")
    g = {"w_o": matmul(f"out_o_dw_{tag}", s["mixed"], dy, "tn", tm=1024, tn=256)}
    (dmg, dpa, dpb, dpc), _, (g["merge_b"],) = ew_bwd(
        f"merge_bwd_{tag}", f_merge, [s["mg"], s["pa"], s["pb"], s["pc"]], [], [p["merge_b"]], [dmixed], [BF16] * 4,
        tm=EW_ROWS_WIDE)
    dya = matmul(f"out_rg_dx_{tag}", dpa, p["w_rg"], "nt")
    g["w_rg"] = matmul(f"out_rg_dw_{tag}", s["ya"], dpa, "tn", tm=1024, tn=256)
    dyb = matmul(f"out_sb_dx_{tag}", dpb, p["w_sb"], "nt", out_dtype=BF16)
    g["w_sb"] = matmul(f"out_sb_dw_{tag}", s["yb"], dpb, "tn", tm=1024, tn=256)
    dyc = matmul(f"out_fox_dx_{tag}", dpc, p["w_fox"], "nt", out_dtype=BF16)
    g["w_fox"] = matmul(f"out_fox_dw_{tag}", s["yc"], dpc, "tn", tm=1024, tn=256)
    dq_c, dk_c, dv_c, dcq, dck = fox_attn_bwd(s["foxqkv"], s["cum"], s["cum_t"], s["lse"], s["yc"], dyc)
    dck_rows = _pad_lanes(dck.transpose(0, 2, 1).reshape(-1, N_HEADS))
    dlf = seq_cumsum(f"fox_cum_bwd_{tag}", [dcq, dck_rows], [1.0, -1.0], True)
    (dff,), _, (dbf,) = ew_bwd(f"fox_logf_bwd_{tag}", f_log_sigmoid_bias, [s["ff"]], [], [p["bf"]], [dlf], [BF16])
    g["fox_bf"] = dbf[0, :N_HEADS]
    dq_b, dk_b, dv_b = sb_attn_bwd(s["sbqkv"], s["sb_tot"], dyb)
    (drgate, dhs), _, _ = ew_bwd(f"rg_out_bwd_{tag}", f_gelu_mul, [s["rgate"], s["hs"]], [], [], [dya], [BF16, F32],
                                 tm=EW_ROWS_WIDE)
    da, du = scan_bwd(s["a"], s["hs"], dhs)
    (dpre_r, dpre_i, dxa1), _, (g["rg_ba"], g["rg_bx"], g["rg_lam"]) = ew_bwd(
        f"rg_gates_bwd_{tag}", f_rg_gates, [s["pre_r"], s["pre_i"], s["xa"]], [], [p["ba"], p["bx"], p["lam"]], [da, du],
        [BF16, BF16, F32], tm=EW_ROWS_WIDE)
    dxa2 = matmul(f"rg_dx_{tag}", [dpre_r, dpre_i], [p["wa_bd"], p["wx_bd"]], "nt")
    g["rg_wa"] = _diag_blocks(matmul(f"rg_a_dw_{tag}", s["xa"], dpre_r, "tn", tm=512, tn=256))
    g["rg_wx"] = _diag_blocks(matmul(f"rg_x_dw_{tag}", s["xa"], dpre_i, "tn", tm=512, tn=256))
    drgx, dwb = conv_bwd(s["rgx"], p["conv_w8"], dxa1, dxa2)
    g["conv_w"] = dwb[:CONV_K]
    g["conv_b"] = dwb[CONV_K]
    cots = [drgx, drgate, dq_b, dk_b, dv_b, dq_c, dk_c, dv_c, dff, dmg]
    w_sb3 = [p["w_sbqkv"][:, k * ATT_W:(k + 1) * ATT_W] for k in range(3)]
    w_fox3 = [p["w_foxqkv"][:, k * ATT_W:(k + 1) * ATT_W] for k in range(3)]
    ws = [p["w_rgx"], p["w_gate"]] + w_sb3 + w_fox3 + [p["w_f"], p["w_merge"]]
    dh = matmul(f"in_dx_{tag}", cots, ws, "nt", tm=512)
    dws = [matmul(f"in_dw{k}_{tag}", s["h"], ct, "tn", tm=1024, tn=256) for k, ct in enumerate(cots)]
    dws[8] = dws[8][:, :N_HEADS]
    g["w_in"] = jnp.concatenate(dws, axis=1)
    (dx,), (dshift, dscale), (g["mix_norm"],) = ew_bwd(f"mix_norm_bwd_{tag}", f_norm_mod, [s["x"]], [shift, scale], [p["gain"]],
                                                       [dh], [F32], adds=[dxo])
    return dx, (dshift, dscale, dgate), g


def _final_loss(x, target, shift, scale, gain):
    n = x.shape[0]
    tm = min(EW_ROWS, SEQ)
    tpb = SEQ // tm

    def body(x_ref, t_ref, sh_ref, sc_ref, g_ref, loss_ref, dx_ref, dsh_ref, dsc_ref, dg_ref):
        i = pl.program_id(0)
        out, vjp = jax.vjp(f_norm_mod, x_ref[...], sh_ref[...], sc_ref[...], g_ref[...])
        diff = out - t_ref[...]
        dx, dsh, dsc, dg = vjp(diff * (1.0 / D))
        dx_ref[...] = dx
        sq = jnp.sum(jnp.sum(diff * diff, axis=1, keepdims=True), axis=0, keepdims=True)

        @pl.when(i % tpb == 0)
        def _():
            dsh_ref[...] = jnp.zeros_like(dsh_ref)
            dsc_ref[...] = jnp.zeros_like(dsc_ref)

        @pl.when(i == 0)
        def _():
            dg_ref[...] = jnp.zeros_like(dg_ref)
            loss_ref[...] = jnp.zeros_like(loss_ref)

        dsh_ref[...] += dsh
        dsc_ref[...] += dsc
        dg_ref[...] += dg
        loss_ref[...] += jnp.broadcast_to(sq, (1, LANES)) * (0.5 / D)

    row, bp, gp = _row_spec(D, tm), _bparam_spec(D, tpb), _gparam_spec((1, D))
    return pl.pallas_call(
        body, name="final_loss", grid=(n // tm,), in_specs=[row, row, bp, bp, gp],
        out_specs=[_gparam_spec((1, LANES)), row, bp, bp, gp],
        out_shape=[jax.ShapeDtypeStruct((1, LANES), F32), jax.ShapeDtypeStruct((n, D), F32),
                   jax.ShapeDtypeStruct(shift.shape, F32), jax.ShapeDtypeStruct(scale.shape, F32),
                   jax.ShapeDtypeStruct((1, D), F32)],
        compiler_params=pltpu.CompilerParams(dimension_semantics=("arbitrary",)),
    )(x, target, shift, scale, gain)


def kernel(x, c, ffn1_norm, ffn1_w1, ffn1_w3, ffn1_w2, mix_norm, w_in, conv_w, conv_b, rg_wa, rg_ba, rg_wx, rg_bx, rg_lam, fox_bf, merge_b, w_rg, w_sb, w_fox, w_o, ffn2_norm, ffn2_w1, ffn2_w3, ffn2_w2, ada_w, ada_b, final_norm, final_ada_w, final_ada_b, loss_target, m_ffn1_norm, m_ffn1_w1, m_ffn1_w3, m_ffn1_w2, m_mix_norm, m_w_in, m_conv_w, m_conv_b, m_rg_wa, m_rg_ba, m_rg_wx, m_rg_bx, m_rg_lam, m_fox_bf, m_merge_b, m_w_rg, m_w_sb, m_w_fox, m_w_o, m_ffn2_norm, m_ffn2_w1, m_ffn2_w3, m_ffn2_w2, m_ada_w, m_ada_b, m_final_norm, m_final_ada_w, m_final_ada_b, v_ffn1_norm, v_ffn1_w1, v_ffn1_w3, v_ffn1_w2, v_mix_norm, v_w_in, v_conv_w, v_conv_b, v_rg_wa, v_rg_ba, v_rg_wx, v_rg_bx, v_rg_lam, v_fox_bf, v_merge_b, v_w_rg, v_w_sb, v_w_fox, v_w_o, v_ffn2_norm, v_ffn2_w1, v_ffn2_w3, v_ffn2_w2, v_ada_w, v_ada_b, v_final_norm, v_final_ada_w, v_final_ada_b):
    given = dict(zip(["x", "c"] + WEIGHTS + ["loss_target"] + ["m_" + n for n in WEIGHTS] + ["v_" + n for n in WEIGHTS],
                     (x, c, ffn1_norm, ffn1_w1, ffn1_w3, ffn1_w2, mix_norm, w_in, conv_w, conv_b, rg_wa, rg_ba, rg_wx, rg_bx, rg_lam, fox_bf, merge_b, w_rg, w_sb, w_fox, w_o, ffn2_norm, ffn2_w1, ffn2_w3, ffn2_w2, ada_w, ada_b, final_norm, final_ada_w, final_ada_b, loss_target, m_ffn1_norm, m_ffn1_w1, m_ffn1_w3, m_ffn1_w2, m_mix_norm, m_w_in, m_conv_w, m_conv_b, m_rg_wa, m_rg_ba, m_rg_wx, m_rg_bx, m_rg_lam, m_fox_bf, m_merge_b, m_w_rg, m_w_sb, m_w_fox, m_w_o, m_ffn2_norm, m_ffn2_w1, m_ffn2_w3, m_ffn2_w2, m_ada_w, m_ada_b, m_final_norm, m_final_ada_w, m_final_ada_b, v_ffn1_norm, v_ffn1_w1, v_ffn1_w3, v_ffn1_w2, v_mix_norm, v_w_in, v_conv_w, v_conv_b, v_rg_wa, v_rg_ba, v_rg_wx, v_rg_bx, v_rg_lam, v_fox_bf, v_merge_b, v_w_rg, v_w_sb, v_w_fox, v_w_o, v_ffn2_norm, v_ffn2_w1, v_ffn2_w3, v_ffn2_w2, v_ada_w, v_ada_b, v_final_norm, v_final_ada_w, v_final_ada_b)))
    idx = my_index()
    n_batch = N_DEV * B_LOC
    ada_cols = ada_w.shape[2]
    fin_cols = final_ada_w.shape[1]

    small_in, small_in_meta = _pack([c, conv_w], LANES, 8, F32)
    c_parts, conv_w_parts = _unpack(all_gather("gather_c_conv", small_in), small_in_meta, lead=1)
    c_all = c_parts.reshape(n_batch, D)
    conv_w_all = _unshard(conv_w_parts, 2)
    c_act = ew_fwd("c_silu", _f_silu, [c_all], [], [], [D], [F32])[0]
    mod_cols = [matmul(f"ada_proj_{l}", c_act, ada_w[l], "nn") for l in range(DEPTH)]
    mod_cols.append(matmul("ada_proj_final", c_act, final_ada_w, "nn"))
    mod_g = all_gather("gather_mod", jnp.concatenate(mod_cols, axis=1))

    shards = {(l, group): [given[n][l].astype(BF16) for n in members] for l in range(DEPTH) for group, members in GROUPS}
    waves = [[(0, "ffn1")], [(0, "mix")], [(0, "ffn2")] + [(l, group) for l in range(1, DEPTH) for group, _ in GROUPS]]
    gather_handles, landed = {}, {}

    def start_wave(wave, behind, carrier):
        blocks, behind = lax.optimization_barrier(({key: shards[key] for key in wave}, behind))
        for key in wave:
            gather_handles[key], token = gather_start(f"gather_start_{key[1]}{key[0]}", blocks[key])
            carrier = carrier + token[0, 0]
        return behind, carrier

    def weights_of(l, group, after):
        key = (l, group)
        if key not in landed:
            landed[key] = gather_finish(f"gather_finish_{group}{l}", gather_handles[key], after)
        return {n: _unshard(b, GATHERED[n] - 1) for n, b in zip(dict(GROUPS)[group], landed[key])}

    _, mod_g = start_wave(waves[0], mod_g, mod_g)
    landed[0, "ffn1"] = gather_finish("gather_finish_ffn10", gather_handles[0, "ffn1"], mod_g)
    landed[0, "ffn1"], mod_g = start_wave(waves[1], landed[0, "ffn1"], mod_g)

    mods = []
    for l in range(DEPTH):
        full = mod_g[:, :, l * ada_cols:(l + 1) * ada_cols].transpose(1, 0, 2).reshape(n_batch, N_DEV * ada_cols)
        full = ew_fwd(f"ada_bias_{l}", _f_add_bias, [full], [], [ada_b[l][None]], [full.shape[1]], [F32])[0]
        mods.append(lax.dynamic_slice_in_dim(full, idx * B_LOC, B_LOC, axis=0).reshape(B_LOC, 3, 3, D))
    fm = mod_g[:, :, DEPTH * ada_cols:].transpose(1, 0, 2).reshape(n_batch, N_DEV * fin_cols)
    fm = ew_fwd("ada_bias_final", _f_add_bias, [fm], [], [final_ada_b[None]], [fm.shape[1]], [F32])[0]
    fm = lax.dynamic_slice_in_dim(fm, idx * B_LOC, B_LOC, axis=0).reshape(B_LOC, 2, D)

    def mixer_params(l, w):
        wi = w["w_in"]
        cut = IN_CUTS
        return dict(
            gain=mix_norm[l][None], w_rgx=wi[:, cut[0]:cut[1]], w_gate=wi[:, cut[1]:cut[2]], w_sbqkv=wi[:, cut[2]:cut[3]],
            w_foxqkv=wi[:, cut[3]:cut[4]], w_f=_pad_lanes(wi[:, cut[4]:cut[5]]), w_merge=wi[:, cut[5]:cut[6]],
            conv_w8=jnp.pad(conv_w_all[l], ((0, 8 - CONV_K), (0, 0))), conv_b=conv_b[l][None],
            wa_bd=_block_diag(rg_wa[l]), wx_bd=_block_diag(rg_wx[l]), ba=rg_ba[l][None], bx=rg_bx[l][None], lam=rg_lam[l][None],
            bf=_pad_lanes(fox_bf[l][None]), merge_b=merge_b[l][None], w_rg=w["w_rg"], w_sb=w["w_sb"], w_fox=w["w_fox"],
            w_o=w["w_o"])

    n_tok = x.shape[0] * x.shape[1]
    h = x.reshape(n_tok, D)
    saved = []
    for l in range(DEPTH):
        m = mods[l]
        w1 = weights_of(l, "ffn1", m if l == 0 else h)
        h, s1 = _ffn_fwd(f"a{l}", h, _bp(m, 0, 0), _bp(m, 0, 1), _bp(m, 0, 2), ffn1_norm[l][None], w1["ffn1_w1"], w1["ffn1_w3"],
                         w1["ffn1_w2"])
        w2 = weights_of(l, "mix", h)
        if l == 0:
            landed[0, "mix"], m = start_wave(waves[2], landed[0, "mix"], m)
        p = mixer_params(l, w2)
        h, s2 = _mixer_fwd(f"{l}", h, _bp(m, 1, 0), _bp(m, 1, 1), _bp(m, 1, 2), p)
        w3 = weights_of(l, "ffn2", h)
        h, s3 = _ffn_fwd(f"b{l}", h, _bp(m, 2, 0), _bp(m, 2, 1), _bp(m, 2, 2), ffn2_norm[l][None], w3["ffn2_w1"], w3["ffn2_w3"],
                         w3["ffn2_w2"])
        saved.append((s1, s2, s3, p, w1, w3))
    loss_row, dh, dfshift, dfscale, dgain_final = _final_loss(h, loss_target.reshape(n_tok, D), fm[:, 0][:, None, :],
                                                              fm[:, 1][:, None, :], final_norm[None])

    grads = {n: [None] * DEPTH for n in WEIGHTS}
    d_mods = [None] * DEPTH
    scatter_handles = {}
    after_start = jnp.zeros((), F32)

    def scatter_blocks(l, group):
        return [_reshard(grads[n][l], GATHERED[n] - 1).astype(BF16) for n in dict(GROUPS)[group]]

    def start_scatter(l, group, g8s=None):
        g8s = scatter_blocks(l, group) if g8s is None else g8s
        scatter_handles[l, group], token = scatter_start(f"scatter_start_{group}{l}", g8s)
        return token[0, 0]

    for l in reversed(range(DEPTH)):
        m = mods[l]
        s1, s2, s3, p, w1, w3 = saved[l]
        dh, dm3, grads["ffn2_norm"][l], grads["ffn2_w1"][l], grads["ffn2_w3"][l], grads["ffn2_w2"][l] = _ffn_bwd(
            f"b{l}", dh, s3, _bp(m, 2, 0), _bp(m, 2, 1), _bp(m, 2, 2) + after_start, ffn2_norm[l][None], w3["ffn2_w1"],
            w3["ffn2_w3"], w3["ffn2_w2"])
        after_start = start_scatter(l, "ffn2")
        dh, dm2, gm = _mixer_bwd(f"{l}", dh, s2, _bp(m, 1, 0), _bp(m, 1, 1), _bp(m, 1, 2) + after_start, p)
        for n, gval in gm.items():
            grads[n][l] = gval
        after_start = start_scatter(l, "mix")
        dh, dm1, grads["ffn1_norm"][l], grads["ffn1_w1"][l], grads["ffn1_w3"][l], grads["ffn1_w2"][l] = _ffn_bwd(
            f"a{l}", dh, s1, _bp(m, 0, 0), _bp(m, 0, 1), _bp(m, 0, 2) + after_start, ffn1_norm[l][None], w1["ffn1_w1"],
            w1["ffn1_w3"], w1["ffn1_w2"])
        if l > 0:
            after_start = start_scatter(l, "ffn1")
        d_mods[l] = jnp.concatenate([t.reshape(B_LOC, D) for dm in (dm1, dm2, dm3) for t in dm], axis=1)
    grad_x = dh.reshape(x.shape)
    d_fm = jnp.concatenate([dfshift.reshape(B_LOC, D), dfscale.reshape(B_LOC, D)], axis=1)

    rep = {n: jnp.stack([t.reshape(given[n].shape[1:]) for t in grads[n]]) for n in REPLICATED if n != "final_norm"}
    rep["final_norm"] = dgain_final.reshape(D)
    rep["conv_w"] = jnp.stack(grads["conv_w"])
    rep_names = list(rep)
    rep_slab, rep_meta = _pack([rep[n] for n in rep_names], LANES, 8, F32)
    mod_slab, mod_meta = _pack(d_mods + [d_fm], LANES, 8, F32)
    small_g = all_gather("gather_small_grads", jnp.concatenate([mod_slab, rep_slab], axis=0))
    last_blocks, small_g = lax.optimization_barrier((scatter_blocks(0, "ffn1"), small_g))
    small_g = small_g + start_scatter(0, "ffn1", last_blocks)
    d_mod_all = [t.reshape(n_batch, -1) for t in _unpack(small_g[:, :mod_slab.shape[0]], mod_meta, lead=1)]
    rep_sum = add_blocks("sum_small_grads", [small_g[k, mod_slab.shape[0]:] for k in range(N_DEV)], F32)
    rep_grad = dict(zip(rep_names, _unpack(rep_sum, rep_meta)))
    final_g = {n: rep_grad[n] for n in REPLICATED}
    final_g["conv_w"] = lax.dynamic_slice_in_dim(rep_grad["conv_w"], idx * conv_w.shape[2], conv_w.shape[2], axis=2)
    final_g["ada_b"] = jnp.stack([sum_rows(f"ada_b_grad_{l}", d_mod_all[l])[0] for l in range(DEPTH)])
    final_g["final_ada_b"] = sum_rows("final_ada_b_grad", d_mod_all[DEPTH])[0]
    final_g["ada_w"] = jnp.stack([
        matmul(f"ada_w_grad_{l}", c_act, lax.dynamic_slice_in_dim(d_mod_all[l], idx * ada_cols, ada_cols, axis=1), "tn")
        for l in range(DEPTH)])
    final_g["final_ada_w"] = matmul(
        "final_ada_w_grad", c_act, lax.dynamic_slice_in_dim(d_mod_all[DEPTH], idx * fin_cols, fin_cols, axis=1), "tn")

    shard_g = {n: [None] * DEPTH for n in GATHERED}

    def finish_scatter(l, group, after):
        sums = scatter_finish(f"scatter_finish_{group}{l}", scatter_handles[l, group], after)
        for n, gval in zip(dict(GROUPS)[group], sums):
            shard_g[n][l] = gval

    for l in reversed(range(DEPTH)):
        for group in ("ffn2", "mix", "ffn1"):
            if (l, group) != (0, "ffn1"):
                finish_scatter(l, group, rep_sum)

    delta, new_m, new_v = {}, {}, {}
    last = dict(GROUPS)["ffn1"]
    sharded = [n for n in GATHERED if n not in last] + ["ada_w", "final_ada_w", "conv_w"] + list(last)
    for n in sharded:
        if n == last[0]:
            finish_scatter(0, "ffn1", delta["w_in"])
        if n in GATHERED:
            final_g[n] = jnp.stack(shard_g[n])
        delta[n], new_m[n], new_v[n] = adamw(f"adamw_{n}", given[n], final_g[n], given["m_" + n], given["v_" + n])
    rep_all = [n for n in WEIGHTS if n not in sharded]
    packed = [_pack([src[n] for n in rep_all], LANES, 8, F32)[0]
              for src in (given, final_g, {n: given["m_" + n] for n in rep_all}, {n: given["v_" + n] for n in rep_all})]
    rep_meta_all = _pack([given[n] for n in rep_all], LANES, 8, F32)[1]
    for store, slab_out in zip((delta, new_m, new_v), adamw("adamw_replicated", *packed)):
        store.update(zip(rep_all, _unpack(slab_out, rep_meta_all)))

    loss = lax.psum(loss_row[0, 0], ("x", "y", "c"))
    return (loss, grad_x, *[final_g[n] for n in WEIGHTS], *[delta[n] for n in WEIGHTS], *[new_m[n] for n in WEIGHTS],
            *[new_v[n] for n in WEIGHTS])
```

```python
import functools

import numpy as np
import jax
import jax.numpy as jnp
from jax import lax
from jax.experimental import pallas as pl
from jax.experimental.pallas import tpu as pltpu

F32 = jnp.float32
BF16 = jnp.bfloat16
MESH = pl.DeviceIdType.MESH

N_DEV = 8
D = 1024
SEQ = 2048
B_LOC = 2
N_TOK = B_LOC * SEQ
DEPTH = 2
D_FF = 2816
RG_BLOCKS = 16
RG_C = 8.0
N_HEADS = 8
HEAD_DIM = 64
ATT_W = N_HEADS * HEAD_DIM
LANES = 128
EPS = 1e-6
ATT_SCALE = HEAD_DIM ** -0.5
CONV_K = 4

ADAM_LR = 0.001
ADAM_B1 = 0.9
ADAM_B2 = 0.999
ADAM_EPS = 1e-08
ADAM_WD = 0.01
ADAM_STEP = 10

EW_ROWS = 512
EW_ROWS_WIDE = 256
ATT_BLK = 512


def _pick_tile(dim, target):
    best = None
    for t in range(LANES, min(dim, target) + 1, LANES):
        if dim % t == 0:
            best = t
    return best if best is not None else dim


_DIMS = {"nn": (((1,), (0,)), ((), ())), "nt": (((1,), (1,)), ((), ())), "tn": (((0,), (0,)), ((), ()))}


def matmul(name, a_list, b_list, mode, out_dtype=F32, tm=1024, tn=512):
    if not isinstance(a_list, (list, tuple)):
        a_list, b_list = [a_list], [b_list]
    n = len(a_list)
    m_dim = a_list[0].shape[1] if mode == "tn" else a_list[0].shape[0]
    n_dim = b_list[0].shape[0] if mode == "nt" else b_list[0].shape[1]
    tm, tn = _pick_tile(m_dim, tm), _pick_tile(n_dim, tn)
    dims = _DIMS[mode]

    def body(*refs):
        o_ref = refs[-1]
        acc = None
        for a_ref, b_ref in zip(refs[:n], refs[n:2 * n]):
            d = lax.dot_general(a_ref[...].astype(BF16), b_ref[...].astype(BF16), dims, preferred_element_type=F32)
            acc = d if acc is None else acc + d
        o_ref[...] = acc.astype(o_ref.dtype)

    in_specs = []
    for a in a_list:
        if mode == "tn":
            in_specs.append(pl.BlockSpec((a.shape[0], tm), lambda i, j: (0, i)))
        else:
            in_specs.append(pl.BlockSpec((tm, a.shape[1]), lambda i, j: (i, 0)))
    for b in b_list:
        if mode == "nt":
            in_specs.append(pl.BlockSpec((tn, b.shape[1]), lambda i, j: (j, 0)))
        else:
            in_specs.append(pl.BlockSpec((b.shape[0], tn), lambda i, j: (0, j)))
    return pl.pallas_call(
        body, name=name, grid=(m_dim // tm, n_dim // tn), in_specs=in_specs,
        out_specs=pl.BlockSpec((tm, tn), lambda i, j: (i, j)),
        out_shape=jax.ShapeDtypeStruct((m_dim, n_dim), out_dtype),
        compiler_params=pltpu.CompilerParams(dimension_semantics=("parallel", "parallel")),
    )(*a_list, *b_list)


def _row_spec(w, tm):
    return pl.BlockSpec((tm, w), lambda i: (i, 0))


def _bparam_spec(w, tiles_per_batch):
    return pl.BlockSpec((None, 1, w), lambda i: (i // tiles_per_batch, 0, 0))


def _gparam_spec(shape):
    return pl.BlockSpec(shape, lambda i: (0, 0))


def ew_fwd(name, fn, rows, bparams, gparams, out_widths, out_dtypes, tm=EW_ROWS):
    n_rows = rows[0].shape[0]
    tm = min(tm, n_rows, SEQ)
    tpb = max(SEQ // tm, 1)
    nr, nb, ng = len(rows), len(bparams), len(gparams)

    def body(*refs):
        vals = [r[...] for r in refs[:nr + nb + ng]]
        outs = fn(*vals)
        if not isinstance(outs, (tuple, list)):
            outs = (outs,)
        for o_ref, o in zip(refs[nr + nb + ng:], outs):
            o_ref[...] = o.astype(o_ref.dtype)

    in_specs = ([_row_spec(r.shape[1], tm) for r in rows] + [_bparam_spec(p.shape[2], tpb) for p in bparams]
                + [_gparam_spec(g.shape) for g in gparams])
    outs = pl.pallas_call(
        body, name=name, grid=(n_rows // tm,), in_specs=in_specs,
        out_specs=[_row_spec(w, tm) for w in out_widths],
        out_shape=[jax.ShapeDtypeStruct((n_rows, w), dt) for w, dt in zip(out_widths, out_dtypes)],
        compiler_params=pltpu.CompilerParams(dimension_semantics=("parallel",)),
    )(*rows, *bparams, *gparams)
    return outs


def ew_bwd(name, fn, rows, bparams, gparams, cts, row_grad_dtypes, adds=(), tm=EW_ROWS):
    n_rows = rows[0].shape[0]
    tm = min(tm, n_rows, SEQ)
    tpb = max(SEQ // tm, 1)
    nr, nb, ng, nc = len(rows), len(bparams), len(gparams), len(cts)
    adds = list(adds) + [None] * (nr - len(adds))
    add_idx = [k for k in range(nr) if adds[k] is not None]
    want = [k for k in range(nr) if row_grad_dtypes[k] is not None]

    def body(*refs):
        pos = nr + nb + ng
        vals = [r[...] for r in refs[:pos]]
        ct_vals = [r[...].astype(F32) for r in refs[pos:pos + nc]]
        pos += nc
        add_vals = {k: refs[pos + q][...] for q, k in enumerate(add_idx)}
        pos += len(add_idx)
        out_refs = refs[pos:]
        f32_vals = [v.astype(F32) for v in vals]
        outs, vjp = jax.vjp(lambda *a: fn(*a), *f32_vals)
        single = not isinstance(outs, (tuple, list))
        grads = vjp(ct_vals[0].astype(outs.dtype) if single else tuple(c.astype(o.dtype) for c, o in zip(ct_vals, outs)))
        i = pl.program_id(0)
        q = 0
        for k in want:
            g = grads[k]
            if k in add_vals:
                g = g + add_vals[k].astype(F32)
            out_refs[q][...] = g.astype(out_refs[q].dtype)
            q += 1
        for k in range(nb):
            ref = out_refs[q]
            q += 1

            @pl.when(i % tpb == 0)
            def _():
                ref[...] = jnp.zeros_like(ref)

            ref[...] += grads[nr + k]
        for k in range(ng):
            ref = out_refs[q]
            q += 1

            @pl.when(i == 0)
            def _():
                ref[...] = jnp.zeros_like(ref)

            ref[...] += grads[nr + nb + k]

    in_specs = ([_row_spec(r.shape[1], tm) for r in rows] + [_bparam_spec(p.shape[2], tpb) for p in bparams]
                + [_gparam_spec(g.shape) for g in gparams] + [_row_spec(c.shape[1], tm) for c in cts]
                + [_row_spec(adds[k].shape[1], tm) for k in add_idx])
    out_specs = ([_row_spec(rows[k].shape[1], tm) for k in want] + [_bparam_spec(p.shape[2], tpb) for p in bparams]
                 + [_gparam_spec(g.shape) for g in gparams])
    out_shape = ([jax.ShapeDtypeStruct(rows[k].shape, row_grad_dtypes[k]) for k in want]
                 + [jax.ShapeDtypeStruct(p.shape, F32) for p in bparams] + [jax.ShapeDtypeStruct(g.shape, F32) for g in gparams])
    outs = pl.pallas_call(
        body, name=name, grid=(n_rows // tm,), in_specs=in_specs, out_specs=out_specs, out_shape=out_shape,
        compiler_params=pltpu.CompilerParams(dimension_semantics=("arbitrary",)),
    )(*rows, *bparams, *gparams, *cts, *[adds[k] for k in add_idx])
    d_rows = list(outs[:len(want)])
    d_b = list(outs[len(want):len(want) + nb])
    d_g = list(outs[len(want) + nb:])
    return d_rows, d_b, d_g


def f_norm_mod(x, shift, scale, gain):
    x = x.astype(F32)
    y = x * lax.rsqrt(jnp.mean(x * x, axis=-1, keepdims=True) + EPS)
    return (y * gain) * (1.0 + scale) + shift


def f_swiglu(a, b3):
    a = a.astype(F32)
    return (a * jax.nn.sigmoid(a)) * b3.astype(F32)


def f_resid(coef, x, y, gate):
    return x.astype(F32) + (coef * (1.0 + gate)) * y.astype(F32)


def f_rg_gates(pre_r, pre_i, xa, ba, bx, lam):
    r = jax.nn.sigmoid(pre_r + ba)
    i = jax.nn.sigmoid(pre_i + bx)
    softplus_neg_lam = jnp.maximum(-lam, 0.0) + jnp.log(1.0 + jnp.exp(-jnp.abs(lam)))
    log_a = (-RG_C) * r * softplus_neg_lam
    a = jnp.exp(log_a)
    u = jnp.sqrt(1.0 - a * a) * (i * xa)
    return a, u


def f_gelu_mul(gate, hs):
    g = gate.astype(F32)
    gelu = 0.5 * g * (1.0 + jnp.tanh(0.7978845608028654 * (g + 0.044715 * g * g * g)))
    return gelu * hs.astype(F32)


def f_log_sigmoid_bias(f, bf):
    z = f.astype(F32) + bf
    return jnp.minimum(z, 0.0) - jnp.log(1.0 + jnp.exp(-jnp.abs(z)))


def f_merge(mg, pa, pb, pc, merge_b):
    g = jax.nn.sigmoid(mg.astype(F32) + merge_b)
    return g[:, :D] * pa.astype(F32) + g[:, D:2 * D] * pb.astype(F32) + g[:, 2 * D:] * pc.astype(F32)


CONV_CB = 256
SCAN_CB = 512
SCAN_CHAINS = 4
CUM_RB = 512


def _shift_down(x, d):
    if d == 0:
        return x
    rows = lax.broadcasted_iota(jnp.int32, x.shape, 0)
    return jnp.where(rows >= d, pltpu.roll(x, d, axis=0), 0.0)


def _shift_up(x, d):
    if d == 0:
        return x
    s = x.shape[0]
    rows = lax.broadcasted_iota(jnp.int32, x.shape, 0)
    return jnp.where(rows < s - d, pltpu.roll(x, s - d, axis=0), 0.0)


def conv_fwd(x, w8, b):
    n, c = x.shape
    nb = n // SEQ

    def body(x_ref, w_ref, b_ref, y_ref):
        xv = x_ref[...]
        acc = jnp.broadcast_to(b_ref[...], xv.shape)
        for k in range(CONV_K):
            acc = acc + w_ref[k:k + 1, :] * _shift_down(xv, CONV_K - 1 - k)
        y_ref[...] = acc

    return pl.pallas_call(
        body, name="conv_fwd", grid=(c // CONV_CB, nb),
        in_specs=[pl.BlockSpec((SEQ, CONV_CB), lambda j, i: (i, j)), pl.BlockSpec((8, CONV_CB), lambda j, i: (0, j)),
                  pl.BlockSpec((1, CONV_CB), lambda j, i: (0, j))],
        out_specs=pl.BlockSpec((SEQ, CONV_CB), lambda j, i: (i, j)),
        out_shape=jax.ShapeDtypeStruct((n, c), F32),
        compiler_params=pltpu.CompilerParams(dimension_semantics=("parallel", "parallel")),
    )(x, w8, b)


def conv_bwd(x, w8, dy1, dy2):
    n, c = x.shape
    nb = n // SEQ

    def body(x_ref, w_ref, dy1_ref, dy2_ref, dx_ref, dwb_ref):
        xv = x_ref[...]
        dy = dy1_ref[...] + dy2_ref[...]
        dx = jnp.zeros_like(xv)
        parts = []
        for k in range(CONV_K):
            d = CONV_K - 1 - k
            dx = dx + w_ref[k:k + 1, :] * _shift_up(dy, d)
            parts.append(jnp.sum(dy * _shift_down(xv, d), axis=0, keepdims=True))
        parts.append(jnp.sum(dy, axis=0, keepdims=True))
        parts.append(jnp.zeros((8 - len(parts), xv.shape[1]), F32))
        dx_ref[...] = dx.astype(BF16)

        @pl.when(pl.program_id(1) == 0)
        def _():
            dwb_ref[...] = jnp.zeros_like(dwb_ref)

        dwb_ref[...] += jnp.concatenate(parts, axis=0)

    return pl.pallas_call(
        body, name="conv_bwd", grid=(c // CONV_CB, nb),
        in_specs=[pl.BlockSpec((SEQ, CONV_CB), lambda j, i: (i, j)), pl.BlockSpec((8, CONV_CB), lambda j, i: (0, j)),
                  pl.BlockSpec((SEQ, CONV_CB), lambda j, i: (i, j)), pl.BlockSpec((SEQ, CONV_CB), lambda j, i: (i, j))],
        out_specs=[pl.BlockSpec((SEQ, CONV_CB), lambda j, i: (i, j)), pl.BlockSpec((8, CONV_CB), lambda j, i: (0, j))],
        out_shape=[jax.ShapeDtypeStruct((n, c), BF16), jax.ShapeDtypeStruct((8, c), F32)],
        compiler_params=pltpu.CompilerParams(dimension_semantics=("parallel", "arbitrary")),
    )(x, w8, dy1, dy2)


def scan_fwd(a, u):
    n, c = a.shape
    q = SEQ // SCAN_CHAINS

    def body(a_ref, u_ref, h_ref, p_ref):
        def step(t, carry):
            hs, ps = carry
            new_h, new_p = [], []
            for k in range(SCAN_CHAINS):
                row = k * q + t
                av = a_ref[pl.ds(row, 1), :]
                hk = av * hs[k] + u_ref[pl.ds(row, 1), :]
                h_ref[pl.ds(row, 1), :] = hk
                new_h.append(hk)
                pk = av * ps[k]
                if k > 0:
                    p_ref[pl.ds(row, 1), :] = pk
                new_p.append(pk)
            return tuple(new_h), tuple(new_p)

        zero, one = jnp.zeros((1, SCAN_CB), F32), jnp.ones((1, SCAN_CB), F32)
        lax.fori_loop(0, q, step, ((zero,) * SCAN_CHAINS, (one,) * SCAN_CHAINS), unroll=4)
        for k in range(1, SCAN_CHAINS):
            rows = pl.ds(k * q, q)
            h_ref[rows, :] = h_ref[rows, :] + p_ref[rows, :] * h_ref[pl.ds(k * q - 1, 1), :]

    spec = pl.BlockSpec((SEQ, SCAN_CB), lambda i, j: (i, j))
    return pl.pallas_call(
        body, name="scan_fwd", grid=(n // SEQ, c // SCAN_CB), in_specs=[spec, spec], out_specs=spec,
        out_shape=jax.ShapeDtypeStruct((n, c), F32), scratch_shapes=[pltpu.VMEM((SEQ, SCAN_CB), F32)],
        compiler_params=pltpu.CompilerParams(dimension_semantics=("parallel", "parallel")),
    )(a, u)


def scan_bwd(a, h, g):
    n, c = a.shape
    q = SEQ // SCAN_CHAINS
    cb = SCAN_CB // 2

    def body(a_ref, h_ref, g_ref, da_ref, du_ref, r_ref):
        def step(j, carry):
            cs, rs = carry
            new_c, new_r = [], []
            for k in range(SCAN_CHAINS):
                row = k * q + (q - 1 - j)
                dh = g_ref[pl.ds(row, 1), :] + cs[k]
                du_ref[pl.ds(row, 1), :] = dh
                av = a_ref[pl.ds(row, 1), :]
                if k < SCAN_CHAINS - 1:
                    r_ref[pl.ds(row, 1), :] = rs[k]
                new_c.append(av * dh)
                new_r.append(av * rs[k])
            return tuple(new_c), tuple(new_r)

        zero, one = jnp.zeros((1, cb), F32), jnp.ones((1, cb), F32)
        lax.fori_loop(0, q, step, ((zero,) * SCAN_CHAINS, (one,) * SCAN_CHAINS), unroll=4)
        for k in reversed(range(SCAN_CHAINS - 1)):
            rows, nxt = pl.ds(k * q, q), pl.ds((k + 1) * q, 1)
            du_ref[rows, :] = du_ref[rows, :] + r_ref[rows, :] * (a_ref[nxt, :] * du_ref[nxt, :])
        da_ref[...] = du_ref[...] * _shift_down(h_ref[...], 1)

    spec = pl.BlockSpec((SEQ, cb), lambda i, j: (i, j))
    return pl.pallas_call(
        body, name="scan_bwd", grid=(n // SEQ, c // cb), in_specs=[spec, spec, spec], out_specs=[spec, spec],
        out_shape=[jax.ShapeDtypeStruct((n, c), F32), jax.ShapeDtypeStruct((n, c), F32)],
        scratch_shapes=[pltpu.VMEM((SEQ, cb), F32)],
        compiler_params=pltpu.CompilerParams(dimension_semantics=("parallel", "parallel")),
    )(a, h, g)


def _split3_dot(m, x):
    hi = x.astype(BF16)
    r1 = x - hi.astype(F32)
    mid = r1.astype(BF16)
    lo = (r1 - mid.astype(F32)).astype(BF16)
    dot = functools.partial(jnp.dot, preferred_element_type=F32)
    return dot(m, hi) + dot(m, mid) + dot(m, lo)


def seq_cumsum(name, xs, signs, reverse):
    n, w = xs[0].shape
    nx = len(xs)
    rb = min(CUM_RB, SEQ)

    def body(*refs):
        x = None
        for r, sg in zip(refs[:nx], signs):
            x = sg * r[...] if x is None else x + sg * r[...]
        q0 = pl.program_id(1) * rb
        row = q0 + lax.broadcasted_iota(jnp.int32, (rb, SEQ), 0)
        col = lax.broadcasted_iota(jnp.int32, (rb, SEQ), 1)
        tri = ((col >= row) if reverse else (col <= row)).astype(BF16)
        refs[nx][...] = _split3_dot(tri, x)

    return pl.pallas_call(
        body, name=name, grid=(n // SEQ, SEQ // rb),
        in_specs=[pl.BlockSpec((SEQ, w), lambda i, j: (i, 0)) for _ in xs],
        out_specs=pl.BlockSpec((rb, w), lambda i, j: (i * (SEQ // rb) + j, 0)),
        out_shape=jax.ShapeDtypeStruct((n, w), F32),
        compiler_params=pltpu.CompilerParams(dimension_semantics=("parallel", "parallel")),
    )(*xs)


N_PAIRS = N_HEADS // 2


def _dot_nt(a, b):
    return lax.dot_general(a, b, _DIMS["nt"], preferred_element_type=F32)


def _dot_tn(a, b):
    return lax.dot_general(a, b, _DIMS["tn"], preferred_element_type=F32)


def _dot_nn(a, b):
    return lax.dot_general(a, b, _DIMS["nn"], preferred_element_type=F32)


def _split2_dot(x, m):
    hi = x.astype(BF16)
    lo = (x - hi.astype(F32)).astype(BF16)
    return _dot_nn(hi, m) + _dot_nn(lo, m)


def _head_mask(j):
    lane = lax.broadcasted_iota(jnp.int32, (1, LANES), 1)
    return (lane // HEAD_DIM) == j


def _lane_pick(x, h):
    lane = lax.broadcasted_iota(jnp.int32, x.shape, 1)
    return jnp.sum(jnp.where(lane == h, x, 0.0), axis=1, keepdims=True)


def _lane_put(col, h):
    lane = lax.broadcasted_iota(jnp.int32, (col.shape[0], LANES), 1)
    return jnp.where(lane == h, col, 0.0)


def _softplus(z):
    return jnp.maximum(z, 0.0) + jnp.log(1.0 + jnp.exp(-jnp.abs(z)))


def _qkv_specs():
    return [pl.BlockSpec((SEQ, LANES), lambda b, p: (b, p)),
            pl.BlockSpec((SEQ, LANES), lambda b, p: (b, N_PAIRS + p)),
            pl.BlockSpec((SEQ, LANES), lambda b, p: (b, 2 * N_PAIRS + p))]


def _pair_spec():
    return pl.BlockSpec((SEQ, LANES), lambda b, p: (b, p))


def _below_diagonal(strictly):
    t = ATT_BLK
    row = lax.broadcasted_iota(jnp.int32, (t, t), 0)
    col = lax.broadcasted_iota(jnp.int32, (t, t), 1)
    return (row > col) if strictly else (row >= col)


def _over_key_blocks(qi, step, init, reverse):
    t = ATT_BLK
    q0 = pl.multiple_of(qi * t, t)

    def off_diagonal(kk, carry):
        ki = (qi - 1 - kk) if reverse else kk
        return step(pl.multiple_of(ki * t, t), carry, False)

    if reverse:
        return lax.fori_loop(0, qi, off_diagonal, step(q0, init, True))
    return step(q0, lax.fori_loop(0, qi, off_diagonal, init), True)


def _masked_q(qb, j):
    return (jnp.where(_head_mask(j), qb, 0.0) * ATT_SCALE).astype(BF16)


def sb_attn_fwd(qkv):
    n = qkv.shape[0]
    t = ATT_BLK

    def body(q_ref, k_ref, v_ref, o_ref, tot_ref):
        pair = pl.program_id(1)
        strict = _below_diagonal(True)
        later = strict.astype(BF16)

        @pl.when(pair == 0)
        def _():
            tot_ref[...] = jnp.zeros_like(tot_ref)

        def q_block(qi, _):
            q0 = pl.multiple_of(qi * t, t)
            qb = q_ref[pl.ds(q0, t), :]
            qms = [_masked_q(qb, j) for j in range(2)]

            def step(k0, carry, diagonal):
                kb = k_ref[pl.ds(k0, t), :].astype(BF16)
                vb = v_ref[pl.ds(k0, t), :].astype(BF16)
                heads = range(2)
                zs = [_dot_nt(qms[j], kb) for j in heads]
                sps = [_softplus(z) for z in zs]
                log_keeps = [(jnp.where(strict, -sp, 0.0) if diagonal else -sp) for sp in sps]
                right_l = [_split2_dot(lk, later) for lk in log_keeps]
                atts = [jnp.exp((zs[j] - sps[j]) + right_l[j] + carry[j][0]) for j in heads]
                if diagonal:
                    atts = [jnp.where(strict, att, 0.0) for att in atts]
                return tuple((carry[j][0] + jnp.sum(log_keeps[j], axis=1, keepdims=True),
                              carry[j][1] + _dot_nn(atts[j].astype(BF16), vb)) for j in heads)

            init = ((jnp.zeros((t, 1), F32), jnp.zeros((t, LANES), F32)),) * 2
            (tot0, acc0), (tot1, acc1) = _over_key_blocks(qi, step, init, reverse=True)
            o_ref[pl.ds(q0, t), :] = jnp.where(_head_mask(0), acc0, acc1)
            tot_ref[pl.ds(q0, t), :] += _lane_put(tot0, 2 * pair) + _lane_put(tot1, 2 * pair + 1)
            return 0

        lax.fori_loop(0, SEQ // t, q_block, 0)

    batch_spec = pl.BlockSpec((SEQ, LANES), lambda b, p: (b, 0))
    return pl.pallas_call(
        body, name="sb_attn_fwd", grid=(n // SEQ, N_PAIRS), in_specs=_qkv_specs(), out_specs=[_pair_spec(), batch_spec],
        out_shape=[jax.ShapeDtypeStruct((n, ATT_W), F32), jax.ShapeDtypeStruct((n, LANES), F32)],
        compiler_params=pltpu.CompilerParams(dimension_semantics=("parallel", "arbitrary")),
    )(qkv, qkv, qkv)


def sb_attn_bwd(qkv, tot, do):
    n = qkv.shape[0]
    t = ATT_BLK

    def body(q_ref, k_ref, v_ref, tot_ref, do_ref, dq_ref, dk_ref, dv_ref, dk_acc, dv_acc):
        pair = pl.program_id(1)
        strict = _below_diagonal(True)
        upto = jnp.logical_not(strict).astype(BF16)
        dk_acc[...] = jnp.zeros_like(dk_acc)
        dv_acc[...] = jnp.zeros_like(dv_acc)

        def q_block(qi, _):
            q0 = pl.multiple_of(qi * t, t)
            qb = q_ref[pl.ds(q0, t), :]
            tot_q = tot_ref[pl.ds(q0, t), :]
            dob = do_ref[pl.ds(q0, t), :].astype(F32)
            qms = [_masked_q(qb, j) for j in range(2)]
            doms = [jnp.where(_head_mask(j), dob, 0.0).astype(BF16) for j in range(2)]
            totals = [_lane_pick(tot_q, 2 * pair + j) for j in range(2)]

            def step(k0, carry, diagonal):
                kb = k_ref[pl.ds(k0, t), :].astype(BF16)
                vb = v_ref[pl.ds(k0, t), :].astype(BF16)
                heads = range(2)
                zs = [_dot_nt(qms[j], kb) for j in heads]
                d_atts = [_dot_nt(doms[j], vb) for j in heads]
                sps = [_softplus(z) for z in zs]
                log_keeps = [(jnp.where(strict, -sp, 0.0) if diagonal else -sp) for sp in sps]
                log_betas = [z - sp for z, sp in zip(zs, sps)]
                left_l = [_split2_dot(lk, upto) for lk in log_keeps]
                atts = [jnp.exp(log_betas[j] + (totals[j] - (carry[j][0] + left_l[j]))) for j in heads]
                if diagonal:
                    atts = [jnp.where(strict, att, 0.0) for att in atts]
                gs = [att * d_att for att, d_att in zip(atts, d_atts)]
                dv = _dot_tn(atts[0].astype(BF16), doms[0]) + _dot_tn(atts[1].astype(BF16), doms[1])
                left_g = [_dot_nn(g.astype(BF16), upto) for g in gs]
                dzs = [gs[j] - jnp.exp(log_betas[j]) * (carry[j][1] + left_g[j]) for j in heads]
                if diagonal:
                    dzs = [jnp.where(strict, dz, 0.0) for dz in dzs]
                dzs = [dz.astype(BF16) for dz in dzs]
                dk = _dot_tn(dzs[0], qms[0]) + _dot_tn(dzs[1], qms[1])
                dk_acc[pl.ds(k0, t), :] += dk
                dv_acc[pl.ds(k0, t), :] += dv
                return tuple((carry[j][0] + jnp.sum(log_keeps[j], axis=1, keepdims=True),
                              carry[j][1] + jnp.sum(gs[j], axis=1, keepdims=True), carry[j][2] + _dot_nn(dzs[j], kb))
                             for j in heads)

            zero = jnp.zeros((t, 1), F32)
            init = ((zero, zero, jnp.zeros((t, LANES), F32)),) * 2
            (_, _, dq0), (_, _, dq1) = _over_key_blocks(qi, step, init, reverse=False)
            dq_ref[pl.ds(q0, t), :] = (jnp.where(_head_mask(0), dq0, dq1) * ATT_SCALE).astype(BF16)
            return 0

        lax.fori_loop(0, SEQ // t, q_block, 0)
        dk_ref[...] = dk_acc[...].astype(BF16)
        dv_ref[...] = dv_acc[...].astype(BF16)

    out = jax.ShapeDtypeStruct((n, ATT_W), BF16)
    batch_spec = pl.BlockSpec((SEQ, LANES), lambda b, p: (b, 0))
    return pl.pallas_call(
        body, name="sb_attn_bwd", grid=(n // SEQ, N_PAIRS), in_specs=_qkv_specs() + [batch_spec, _pair_spec()],
        out_specs=[_pair_spec()] * 3, out_shape=[out, out, out],
        scratch_shapes=[pltpu.VMEM((SEQ, LANES), F32), pltpu.VMEM((SEQ, LANES), F32)],
        compiler_params=pltpu.CompilerParams(dimension_semantics=("parallel", "parallel")),
    )(qkv, qkv, qkv, tot, do)


NEG_BIG = -1e30


def fox_attn_fwd(qkv, cum, cum_t):
    n = qkv.shape[0]
    t = ATT_BLK

    def body(q_ref, k_ref, v_ref, cum_ref, cumt_ref, o_ref, lse_ref):
        pair = pl.program_id(1)
        causal = _below_diagonal(False)

        @pl.when(pair == 0)
        def _():
            lse_ref[...] = jnp.zeros_like(lse_ref)

        def q_block(qi, _):
            q0 = pl.multiple_of(qi * t, t)
            qb = q_ref[pl.ds(q0, t), :]
            cum_q = cum_ref[pl.ds(q0, t), :]
            qms = [_masked_q(qb, j) for j in range(2)]
            cqs = [_lane_pick(cum_q, 2 * pair + j) for j in range(2)]

            def step(k0, carry, diagonal):
                kb = k_ref[pl.ds(k0, t), :].astype(BF16)
                vb = v_ref[pl.ds(k0, t), :].astype(BF16)
                heads = range(2)
                zs = [_dot_nt(qms[j], kb) + cqs[j] - cumt_ref[pl.ds(2 * pair + j, 1), pl.ds(k0, t)] for j in heads]
                if diagonal:
                    zs = [jnp.where(causal, z, NEG_BIG) for z in zs]
                m_new = [jnp.maximum(carry[j][0], jnp.max(zs[j], axis=1, keepdims=True)) for j in heads]
                ps = [jnp.exp(zs[j] - m_new[j]) for j in heads]
                alphas = [jnp.exp(carry[j][0] - m_new[j]) for j in heads]
                return tuple((m_new[j], alphas[j] * carry[j][1] + jnp.sum(ps[j], axis=1, keepdims=True),
                              alphas[j] * carry[j][2] + _dot_nn(ps[j].astype(BF16), vb)) for j in heads)

            init = ((jnp.full((t, 1), NEG_BIG, F32), jnp.zeros((t, 1), F32), jnp.zeros((t, LANES), F32)),) * 2
            (m0, l0, acc0), (m1, l1, acc1) = _over_key_blocks(qi, step, init, reverse=False)
            o_ref[pl.ds(q0, t), :] = jnp.where(_head_mask(0), acc0 / l0, acc1 / l1)
            lse_ref[pl.ds(q0, t), :] += _lane_put(m0 + jnp.log(l0), 2 * pair) + _lane_put(m1 + jnp.log(l1), 2 * pair + 1)
            return 0

        lax.fori_loop(0, SEQ // t, q_block, 0)

    batch_spec = pl.BlockSpec((SEQ, LANES), lambda b, p: (b, 0))
    return pl.pallas_call(
        body, name="fox_attn_fwd", grid=(n // SEQ, N_PAIRS),
        in_specs=_qkv_specs() + [batch_spec, pl.BlockSpec((None, N_HEADS, SEQ), lambda b, p: (b, 0, 0))],
        out_specs=[_pair_spec(), batch_spec],
        out_shape=[jax.ShapeDtypeStruct((n, ATT_W), F32), jax.ShapeDtypeStruct((n, LANES), F32)],
        compiler_params=pltpu.CompilerParams(dimension_semantics=("parallel", "arbitrary")),
    )(qkv, qkv, qkv, cum, cum_t)


def fox_attn_bwd(qkv, cum, cum_t, lse, o, do):
    n = qkv.shape[0]
    t = ATT_BLK

    def body(q_ref, k_ref, v_ref, cum_ref, cumt_ref, lse_ref, o_ref, do_ref, dq_ref, dk_ref, dv_ref, dcq_ref, dck_ref,
             dk_acc, dv_acc):
        pair = pl.program_id(1)
        causal = _below_diagonal(False)
        dk_acc[...] = jnp.zeros_like(dk_acc)
        dv_acc[...] = jnp.zeros_like(dv_acc)

        @pl.when(pair == 0)
        def _():
            dcq_ref[...] = jnp.zeros_like(dcq_ref)
            dck_ref[...] = jnp.zeros_like(dck_ref)

        def q_block(qi, _):
            q0 = pl.multiple_of(qi * t, t)
            qb = q_ref[pl.ds(q0, t), :]
            ob = o_ref[pl.ds(q0, t), :]
            dob = do_ref[pl.ds(q0, t), :].astype(F32)
            cum_q = cum_ref[pl.ds(q0, t), :]
            lse_q = lse_ref[pl.ds(q0, t), :]
            qms = [_masked_q(qb, j) for j in range(2)]
            dom32 = [jnp.where(_head_mask(j), dob, 0.0) for j in range(2)]
            doms = [d.astype(BF16) for d in dom32]
            deltas = [jnp.sum(d * ob, axis=1, keepdims=True) for d in dom32]
            cqs = [_lane_pick(cum_q, 2 * pair + j) for j in range(2)]
            lqs = [_lane_pick(lse_q, 2 * pair + j) for j in range(2)]

            def step(k0, carry, diagonal):
                kb = k_ref[pl.ds(k0, t), :].astype(BF16)
                vb = v_ref[pl.ds(k0, t), :].astype(BF16)
                heads = range(2)
                zs = [_dot_nt(qms[j], kb) + cqs[j] - cumt_ref[pl.ds(2 * pair + j, 1), pl.ds(k0, t)] for j in heads]
                d_ps = [_dot_nt(doms[j], vb) for j in heads]
                if diagonal:
                    zs = [jnp.where(causal, z, NEG_BIG) for z in zs]
                ps = [jnp.exp(zs[j] - lqs[j]) for j in heads]
                dv_acc[pl.ds(k0, t), :] += _dot_tn(ps[0].astype(BF16), doms[0]) + _dot_tn(ps[1].astype(BF16), doms[1])
                dzs = [ps[j] * (d_ps[j] - deltas[j]) for j in heads]
                dzb = [dz.astype(BF16) for dz in dzs]
                dk_acc[pl.ds(k0, t), :] += _dot_tn(dzb[0], qms[0]) + _dot_tn(dzb[1], qms[1])
                for j in heads:
                    dck_ref[pl.ds(2 * pair + j, 1), pl.ds(k0, t)] += jnp.sum(dzs[j], axis=0, keepdims=True)
                return tuple((carry[j][0] + _dot_nn(dzb[j], kb), carry[j][1] + jnp.sum(dzs[j], axis=1, keepdims=True))
                             for j in heads)

            init = ((jnp.zeros((t, LANES), F32), jnp.zeros((t, 1), F32)),) * 2
            (dq0, dcq0), (dq1, dcq1) = _over_key_blocks(qi, step, init, reverse=False)
            dq_ref[pl.ds(q0, t), :] = (jnp.where(_head_mask(0), dq0, dq1) * ATT_SCALE).astype(BF16)
            dcq_ref[pl.ds(q0, t), :] += _lane_put(dcq0, 2 * pair) + _lane_put(dcq1, 2 * pair + 1)
            return 0

        lax.fori_loop(0, SEQ // t, q_block, 0)
        dk_ref[...] = dk_acc[...].astype(BF16)
        dv_ref[...] = dv_acc[...].astype(BF16)

    batch_spec = pl.BlockSpec((SEQ, LANES), lambda b, p: (b, 0))
    t_spec = pl.BlockSpec((None, N_HEADS, SEQ), lambda b, p: (b, 0, 0))
    out = jax.ShapeDtypeStruct((n, ATT_W), BF16)
    return pl.pallas_call(
        body, name="fox_attn_bwd", grid=(n // SEQ, N_PAIRS),
        in_specs=_qkv_specs() + [batch_spec, t_spec, batch_spec, _pair_spec(), _pair_spec()],
        out_specs=[_pair_spec()] * 3 + [batch_spec, t_spec],
        scratch_shapes=[pltpu.VMEM((SEQ, LANES), F32), pltpu.VMEM((SEQ, LANES), F32)],
        out_shape=[out, out, out, jax.ShapeDtypeStruct((n, LANES), F32), jax.ShapeDtypeStruct((n // SEQ, N_HEADS, SEQ), F32)],
        compiler_params=pltpu.CompilerParams(dimension_semantics=("parallel", "arbitrary")),
    )(qkv, qkv, qkv, cum, cum_t, lse, o, do)


_HBM = pl.BlockSpec(memory_space=pl.ANY)


def _my_place():
    return lax.axis_index("x"), lax.axis_index("y"), lax.axis_index("c")


def my_index():
    mx, my, mc = _my_place()
    return 4 * mx + 2 * my + mc


def all_gather(name, xs):
    single = not isinstance(xs, (list, tuple))
    xs = [xs] if single else list(xs)
    na = len(xs)

    def body(*refs):
        x_refs, out_refs = refs[:na], refs[na:2 * na]
        send_sems, recv_sems, local_sems = refs[2 * na:]
        mx, my, mc = _my_place()
        me, sibling = (mx, my, mc), (mx, my, 1 - mc)
        chips = [(1 - mx, my), (mx, 1 - my), (1 - mx, 1 - my)]

        def slot(a, px, py, pc):
            return out_refs[a].at[4 * px + 2 * py + pc]

        def copy(a, k, block, to, src=None):
            return pltpu.make_async_remote_copy(
                src_ref=slot(a, *block) if src is None else src, dst_ref=slot(a, *block),
                send_sem=send_sems.at[7 * a + k], recv_sem=recv_sems.at[7 * a + k], device_id=to, device_id_type=MESH)

        mine = [pltpu.make_async_copy(x_refs[a], slot(a, *me), local_sems.at[a]) for a in range(na)]
        for cp in mine:
            cp.start()
        first = []
        for j, chip in enumerate(chips):
            first += [copy(a, 1 + j, me, (*chip, mc), src=x_refs[a]) for a in range(na)]
        first += [copy(a, 0, me, sibling, src=x_refs[a]) for a in range(na)]
        for cp in first:
            cp.start()
        passed = []
        for j, chip in enumerate(chips):
            for a in range(na):
                copy(a, 1 + j, (*chip, mc), me).wait_recv()
                passed.append(copy(a, 4 + j, (*chip, mc), sibling))
                passed[-1].start()
        for a in range(na):
            copy(a, 0, sibling, me).wait_recv()
        for j, chip in enumerate(chips):
            for a in range(na):
                copy(a, 4 + j, (*chip, 1 - mc), me).wait_recv()
        for cp in first + passed:
            cp.wait_send()
        for cp in mine:
            cp.wait()

    outs = pl.pallas_call(
        body, name=name, in_specs=[_HBM] * na, out_specs=[_HBM] * na,
        out_shape=[jax.ShapeDtypeStruct((N_DEV,) + x.shape, x.dtype) for x in xs],
        scratch_shapes=[pltpu.SemaphoreType.DMA((7 * na,)), pltpu.SemaphoreType.DMA((7 * na,)), pltpu.SemaphoreType.DMA((na,))],
    )(*xs)
    return outs[0] if single else list(outs)


_SEM = pl.BlockSpec(memory_space=pltpu.SEMAPHORE)
_HBM_ONLY = pl.BlockSpec(memory_space=pltpu.HBM)
_EFFECT = pltpu.SideEffectType.DATAFLOW_SIDE_EFFECTING
N_PEERS = N_DEV


def _peers():
    mx, my, mc = _my_place()
    return [((1 - mx) if (r >> 2) & 1 else mx, (1 - my) if (r >> 1) & 1 else my, (1 - mc) if r & 1 else mc)
            for r in range(N_DEV)]


def _exchange_copies(scatter, x_refs, land_refs, send_sems, recv_sems):
    me = my_index()
    copies = []
    for a, (x_ref, land_ref) in enumerate(zip(x_refs, land_refs)):
        for r, (px, py, pc) in enumerate(_peers()):
            src = x_ref.at[4 * px + 2 * py + pc] if scatter else x_ref
            dst = land_ref.at[r] if scatter else land_ref.at[me]
            copies.append(pltpu.make_async_remote_copy(
                src_ref=src, dst_ref=dst, send_sem=send_sems.at[N_PEERS * a + r], recv_sem=recv_sems.at[N_PEERS * a + r],
                device_id=(px, py, pc), device_id_type=MESH))
    return copies


def exchange_start(name, xs, scatter):
    na = len(xs)
    lands = [lax.empty((N_PEERS,) + x.shape[1:] if scatter else (N_DEV,) + x.shape, x.dtype) for x in xs]

    def body(*refs):
        x_refs, land_refs, send_sems, recv_sems = refs[:na], refs[na:2 * na], refs[2 * na], refs[2 * na + 1]
        token = refs[-1]
        for cp in _exchange_copies(scatter, x_refs, land_refs, send_sems, recv_sems):
            cp.start()
        token[...] = jnp.zeros_like(token)

    outs = pl.pallas_call(
        body, name=name,
        out_shape=(pltpu.SemaphoreType.DMA((N_PEERS * na,)), pltpu.SemaphoreType.DMA((N_PEERS * na,)),
                   *[pltpu.HBM(x.shape, x.dtype) for x in xs], *[pltpu.HBM(l.shape, l.dtype) for l in lands],
                   jax.ShapeDtypeStruct((8, LANES), F32)),
        in_specs=[_HBM_ONLY] * (2 * na),
        out_specs=(_SEM, _SEM, *[_HBM_ONLY] * (2 * na), pl.BlockSpec(memory_space=pltpu.VMEM)),
        input_output_aliases={i: 2 + i for i in range(2 * na)},
        compiler_params=pltpu.CompilerParams(has_side_effects=_EFFECT),
    )(*[pltpu.with_memory_space_constraint(x, pltpu.HBM) for x in xs],
      *[pltpu.with_memory_space_constraint(l, pltpu.HBM) for l in lands])
    return (scatter, outs[0], outs[1], outs[2:2 + na], outs[2 + na:2 + 2 * na]), outs[-1]


def exchange_finish(name, handle, after):
    scatter, send_sems, recv_sems, xs, lands = handle
    na = len(xs)

    def body(*refs):
        x_refs, land_refs, send_ref, recv_ref = refs[:na], refs[na:2 * na], refs[2 * na], refs[2 * na + 1]
        for cp in _exchange_copies(scatter, x_refs, land_refs, send_ref, recv_ref):
            cp.wait_send()
            cp.wait_recv()

    outs = pl.pallas_call(
        body, name=name,
        out_shape=tuple(pltpu.HBM(t.shape, t.dtype) for t in list(xs) + list(lands)),
        in_specs=[_HBM_ONLY] * (2 * na) + [_SEM, _SEM, _HBM],
        out_specs=tuple([_HBM_ONLY] * (2 * na)),
        input_output_aliases={i: i for i in range(2 * na)},
        compiler_params=pltpu.CompilerParams(has_side_effects=_EFFECT),
    )(*xs, *lands, send_sems, recv_sems, after)
    return list(outs[:na]), list(outs[na:])


def _pick_rows(n, target):
    best = None
    for t in range(8, min(n, target) + 1, 8):
        if n % t == 0:
            best = t
    return best if best is not None else n


def add_blocks(name, parts, out_dtype, rows=512):
    r, w = parts[0].shape
    tr = _pick_rows(r, rows)

    def body(*refs):
        acc = refs[0][...].astype(F32)
        for ref in refs[1:-1]:
            acc = acc + ref[...].astype(F32)
        refs[-1][...] = acc.astype(refs[-1].dtype)

    spec = pl.BlockSpec((tr, w), lambda i: (i, 0))
    return pl.pallas_call(
        body, name=name, grid=(r // tr,), in_specs=[spec] * len(parts), out_specs=spec,
        out_shape=jax.ShapeDtypeStruct((r, w), out_dtype),
        compiler_params=pltpu.CompilerParams(dimension_semantics=("parallel",)),
    )(*parts)


def sum_rows(name, x):
    def body(x_ref, o_ref):
        o_ref[...] = jnp.sum(x_ref[...], axis=0, keepdims=True)

    return pl.pallas_call(body, name=name, out_shape=jax.ShapeDtypeStruct((1, x.shape[1]), F32))(x)


def gather_start(name, blocks):
    return exchange_start(name, blocks, scatter=False)


def gather_finish(name, handle, after):
    return exchange_finish(name, handle, after)[1]


def scatter_start(name, g8s):
    return exchange_start(name, g8s, scatter=True)


def scatter_finish(name, handle, after):
    _, lands = exchange_finish(name, handle, after)
    outs = []
    for a, land in enumerate(lands):
        w = land.shape[-1]
        outs.append(add_blocks(f"{name}_sum{a}", [land[k].reshape(-1, w) for k in range(N_PEERS)], F32).reshape(land.shape[1:]))
    return outs


def _pack(arrays, width, row_mult, dtype, lead=0):
    parts, metas = [], []
    for a in arrays:
        lead_shape = a.shape[:lead]
        size = int(np.prod(a.shape[lead:]))
        chunk = row_mult * width
        padded = -(-size // chunk) * chunk
        flat = a.astype(dtype).reshape(lead_shape + (size,))
        if padded != size:
            flat = jnp.pad(flat, [(0, 0)] * lead + [(0, padded - size)])
        parts.append(flat.reshape(lead_shape + (padded // width, width)))
        metas.append((a.shape[lead:], size, padded // width))
    return jnp.concatenate(parts, axis=lead), metas


def _unpack(slab, metas, lead=0):
    out, r0 = [], 0
    for shape, size, rows in metas:
        part = lax.slice_in_dim(slab, r0, r0 + rows, axis=lead)
        lead_shape = part.shape[:lead]
        flat = part.reshape(lead_shape + (rows * part.shape[-1],))
        out.append(lax.slice_in_dim(flat, 0, size, axis=lead).reshape(lead_shape + tuple(shape)))
        r0 += rows
    return out


def _f_adamw(w, g, m, v):
    m = ADAM_B1 * m + (1.0 - ADAM_B1) * g
    v = ADAM_B2 * v + (1.0 - ADAM_B2) * (g * g)
    m_hat = m / (1.0 - ADAM_B1 ** ADAM_STEP)
    v_hat = v / (1.0 - ADAM_B2 ** ADAM_STEP)
    delta = (-ADAM_LR) * (m_hat / (jnp.sqrt(v_hat) + ADAM_EPS) + ADAM_WD * w)
    return delta, m, v


def adamw(name, w, g, m, v):
    shape = w.shape
    w2 = shape[-1]
    flat = [a.reshape(-1, w2) for a in (w, g, m, v)]
    tm = _pick_rows(flat[0].shape[0], 256)
    outs = ew_fwd(name, _f_adamw, flat, [], [], [w2] * 3, [F32] * 3, tm=tm)
    return [o.reshape(shape) for o in outs]


WEIGHTS = ["ffn1_norm", "ffn1_w1", "ffn1_w3", "ffn1_w2", "mix_norm", "w_in", "conv_w", "conv_b", "rg_wa", "rg_ba", "rg_wx",
           "rg_bx", "rg_lam", "fox_bf", "merge_b", "w_rg", "w_sb", "w_fox", "w_o", "ffn2_norm", "ffn2_w1", "ffn2_w3",
           "ffn2_w2", "ada_w", "ada_b", "final_norm", "final_ada_w", "final_ada_b"]
GATHERED = {"ffn1_w1": 2, "ffn1_w3": 2, "ffn1_w2": 1, "w_in": 2, "w_rg": 1, "w_sb": 2, "w_fox": 2, "w_o": 1,
            "ffn2_w1": 2, "ffn2_w3": 2, "ffn2_w2": 1}
REPLICATED = ["ffn1_norm", "mix_norm", "conv_b", "rg_wa", "rg_ba", "rg_wx", "rg_bx", "rg_lam", "fox_bf", "merge_b",
              "ffn2_norm", "final_norm"]
GROUPS = (("ffn1", ("ffn1_w1", "ffn1_w3", "ffn1_w2")), ("mix", ("w_in", "w_rg", "w_sb", "w_fox", "w_o")),
          ("ffn2", ("ffn2_w1", "ffn2_w3", "ffn2_w2")))
IN_CUTS = (0, 1024, 2048, 3584, 5120, 5128, 8200)


def _unshard(g, axis):
    g = jnp.moveaxis(g, 0, axis)
    shape = g.shape
    return g.reshape(shape[:axis] + (shape[axis] * shape[axis + 1],) + shape[axis + 2:])


def _reshard(full, axis):
    shape = full.shape
    g = full.reshape(shape[:axis] + (N_DEV, shape[axis] // N_DEV) + shape[axis + 1:])
    return jnp.moveaxis(g, axis, 0)


def _block_diag(w):
    nb, bd, _ = w.shape
    eye = jnp.eye(nb, dtype=bool)[:, None, :, None]
    return jnp.where(eye, w[:, :, None, :], 0.0).reshape(nb * bd, nb * bd)


def _diag_blocks(m, nb=RG_BLOCKS):
    bd = m.shape[0] // nb
    return jnp.stack([m[k * bd:(k + 1) * bd, k * bd:(k + 1) * bd] for k in range(nb)])


def _pad_lanes(a, width=LANES):
    return jnp.pad(a, [(0, 0)] * (a.ndim - 1) + [(0, width - a.shape[-1])])


def _bp(m, k, which):
    return m[:, k, which][:, None, :]


def _f_silu(c):
    return c * jax.nn.sigmoid(c)


def _f_add_bias(a, b):
    return a + b


FFN_TM = 512
FFN_TN = 1408
FFN_SUB = 256


def ffn_up(name, h, w1, w3):
    n, k = h.shape
    f = w1.shape[1]
    tm, tn = min(FFN_TM, n), _pick_tile(f, FFN_TN)

    def body(h_ref, w1_ref, w3_ref, a_ref, b_ref, s_ref):
        subs = [pl.ds(r, FFN_SUB) for r in range(0, tm, FFN_SUB)] if tm % FFN_SUB == 0 else [pl.ds(0, tm)]
        hs = [h_ref[rows, :] for rows in subs]
        a_s = [jnp.dot(hv, w1_ref[...], preferred_element_type=F32) for hv in hs]
        b_s = [jnp.dot(hv, w3_ref[...], preferred_element_type=F32) for hv in hs]
        for rows, a, b in zip(subs, a_s, b_s):
            a_ref[rows, :] = a.astype(BF16)
            b_ref[rows, :] = b.astype(BF16)
            s_ref[rows, :] = ((a * jax.nn.sigmoid(a)) * b).astype(BF16)

    wspec = pl.BlockSpec((k, tn), lambda i, j: (0, j))
    ospec = pl.BlockSpec((tm, tn), lambda i, j: (i, j))
    out = jax.ShapeDtypeStruct((n, f), BF16)
    return pl.pallas_call(
        body, name=name, grid=(n // tm, f // tn), in_specs=[pl.BlockSpec((tm, k), lambda i, j: (i, 0)), wspec, wspec],
        out_specs=[ospec] * 3, out_shape=[out] * 3,
        compiler_params=pltpu.CompilerParams(dimension_semantics=("parallel", "parallel")),
    )(h, w1, w3)


def ffn_down_dx(name, dy, w2, a, b):
    n, k = dy.shape
    f = w2.shape[0]
    tm, tn = min(FFN_TM, n), _pick_tile(f, FFN_TN)

    def body(dy_ref, w2_ref, a_ref, b_ref, da_ref, db_ref):
        subs = [pl.ds(r, FFN_SUB) for r in range(0, tm, FFN_SUB)] if tm % FFN_SUB == 0 else [pl.ds(0, tm)]
        ds_s = [_dot_nt(dy_ref[rows, :], w2_ref[...]) for rows in subs]
        for rows, ds in zip(subs, ds_s):
            av = a_ref[rows, :].astype(F32)
            sig = jax.nn.sigmoid(av)
            da_ref[rows, :] = (ds * b_ref[rows, :].astype(F32) * (sig * (1.0 + av * (1.0 - sig)))).astype(BF16)
            db_ref[rows, :] = (ds * (av * sig)).astype(BF16)

    ospec = pl.BlockSpec((tm, tn), lambda i, j: (i, j))
    out = jax.ShapeDtypeStruct((n, f), BF16)
    return pl.pallas_call(
        body, name=name, grid=(n // tm, f // tn),
        in_specs=[pl.BlockSpec((tm, k), lambda i, j: (i, 0)), pl.BlockSpec((tn, k), lambda i, j: (j, 0)), ospec, ospec],
        out_specs=[ospec] * 2, out_shape=[out] * 2,
        compiler_params=pltpu.CompilerParams(dimension_semantics=("parallel", "parallel")),
    )(dy, w2, a, b)


def _ffn_fwd(tag, x, shift, scale, gate, gain, w1, w3, w2):
    h = ew_fwd(f"ffn_norm_{tag}", f_norm_mod, [x], [shift, scale], [gain], [D], [BF16])[0]
    a, b3, s = ffn_up(f"ffn_up_{tag}", h, w1, w3)
    y = matmul(f"ffn_down_{tag}", s, w2, "nn", tm=1024)
    xo = ew_fwd(f"ffn_res_{tag}", functools.partial(f_resid, 0.5), [x, y], [gate], [], [D], [F32])[0]
    return xo, (x, h, a, b3, s, y)


def _ffn_bwd(tag, dxo, saved, shift, scale, gate, gain, w1, w3, w2):
    x, h, a, b3, s, y = saved
    (dy,), (dgate,), _ = ew_bwd(f"ffn_res_bwd_{tag}", functools.partial(f_resid, 0.5), [x, y], [gate], [], [dxo], [None, BF16])
    da, db3 = ffn_down_dx(f"ffn_down_dx_{tag}", dy, w2, a, b3)
    dw2 = matmul(f"ffn_dw2_{tag}", s, dy, "tn", tm=1408, tn=256)
    dw1 = matmul(f"ffn_dw1_{tag}", h, da, "tn", tm=1024, tn=256)
    dw3 = matmul(f"ffn_dw3_{tag}", h, db3, "tn", tm=1024, tn=256)
    dh = matmul(f"ffn_up_dx_{tag}", [da, db3], [w1, w3], "nt", tm=1024)
    (dx,), (dshift, dscale), (dgain,) = ew_bwd(f"ffn_norm_bwd_{tag}", f_norm_mod, [x], [shift, scale], [gain], [dh], [F32],
                                               adds=[dxo])
    return dx, (dshift, dscale, dgate), dgain, dw1, dw3, dw2


def _mixer_fwd(tag, x, shift, scale, gate, p):
    h = ew_fwd(f"mix_norm_{tag}", f_norm_mod, [x], [shift, scale], [p["gain"]], [D], [BF16])[0]
    rgx = matmul(f"in_rgx_{tag}", h, p["w_rgx"], "nn")
    rgate = matmul(f"in_gate_{tag}", h, p["w_gate"], "nn")
    sbqkv = matmul(f"in_sb_{tag}", h, p["w_sbqkv"], "nn")
    foxqkv = matmul(f"in_fox_{tag}", h, p["w_foxqkv"], "nn")
    ff = matmul(f"in_forget_{tag}", h, p["w_f"], "nn")
    mg = matmul(f"in_merge_{tag}", h, p["w_merge"], "nn")
    xa = conv_fwd(rgx, p["conv_w8"], p["conv_b"])
    pre_r = matmul(f"rg_a_{tag}", xa, p["wa_bd"], "nn")
    pre_i = matmul(f"rg_x_{tag}", xa, p["wx_bd"], "nn")
    a, u = ew_fwd(f"rg_gates_{tag}", f_rg_gates, [pre_r, pre_i, xa], [], [p["ba"], p["bx"], p["lam"]], [D, D], [F32, F32],
                  tm=EW_ROWS_WIDE)
    hs = scan_fwd(a, u)
    ya = ew_fwd(f"rg_out_{tag}", f_gelu_mul, [rgate, hs], [], [], [D], [BF16])[0]
    yb, sb_tot = sb_attn_fwd(sbqkv)
    lf = ew_fwd(f"fox_logf_{tag}", f_log_sigmoid_bias, [ff], [], [p["bf"]], [LANES], [F32])[0]
    cum = seq_cumsum(f"fox_cum_{tag}", [lf], [1.0], False)
    cum_t = cum.reshape(-1, SEQ, LANES)[:, :, :N_HEADS].transpose(0, 2, 1)
    yc, lse = fox_attn_fwd(foxqkv, cum, cum_t)
    pa = matmul(f"out_rg_{tag}", ya, p["w_rg"], "nn")
    pb = matmul(f"out_sb_{tag}", yb, p["w_sb"], "nn")
    pc = matmul(f"out_fox_{tag}", yc, p["w_fox"], "nn")
    mixed = ew_fwd(f"merge_{tag}", f_merge, [mg, pa, pb, pc], [], [p["merge_b"]], [D], [BF16], tm=EW_ROWS_WIDE)[0]
    y = matmul(f"out_o_{tag}", mixed, p["w_o"], "nn")
    xo = ew_fwd(f"mix_res_{tag}", functools.partial(f_resid, 1.0), [x, y], [gate], [], [D], [F32])[0]
    saved = dict(x=x, h=h, rgx=rgx, rgate=rgate, sbqkv=sbqkv, foxqkv=foxqkv, ff=ff, mg=mg, xa=xa, pre_r=pre_r, pre_i=pre_i,
                 a=a, hs=hs, ya=ya, yb=yb, sb_tot=sb_tot, cum=cum, cum_t=cum_t, yc=yc, lse=lse, pa=pa, pb=pb, pc=pc,
                 mixed=mixed, y=y)
    return xo, saved


def _mixer_bwd(tag, dxo, s, shift, scale, gate, p):
    (dy,), (dgate,), _ = ew_bwd(f"mix_res_bwd_{tag}", functools.partial(f_resid, 1.0), [s["x"], s["y"]], [gate], [], [dxo],
                                [None, BF16])
    dmixed = matmul(f"out_o_dx_{tag}", dy, p["w_o"], "nt")
    g = {"w_o": matmul(f"out_o_dw_{tag}", s["mixed"], dy, "tn", tm=1024, tn=256)}
    (dmg, dpa, dpb, dpc), _, (g["merge_b"],) = ew_bwd(
        f"merge_bwd_{tag}", f_merge, [s["mg"], s["pa"], s["pb"], s["pc"]], [], [p["merge_b"]], [dmixed], [BF16] * 4,
        tm=EW_ROWS_WIDE)
    dya = matmul(f"out_rg_dx_{tag}", dpa, p["w_rg"], "nt")
    g["w_rg"] = matmul(f"out_rg_dw_{tag}", s["ya"], dpa, "tn", tm=1024, tn=256)
    dyb = matmul(f"out_sb_dx_{tag}", dpb, p["w_sb"], "nt", out_dtype=BF16)
    g["w_sb"] = matmul(f"out_sb_dw_{tag}", s["yb"], dpb, "tn", tm=1024, tn=256)
    dyc = matmul(f"out_fox_dx_{tag}", dpc, p["w_fox"], "nt", out_dtype=BF16)
    g["w_fox"] = matmul(f"out_fox_dw_{tag}", s["yc"], dpc, "tn", tm=1024, tn=256)
    dq_c, dk_c, dv_c, dcq, dck = fox_attn_bwd(s["foxqkv"], s["cum"], s["cum_t"], s["lse"], s["yc"], dyc)
    dck_rows = _pad_lanes(dck.transpose(0, 2, 1).reshape(-1, N_HEADS))
    dlf = seq_cumsum(f"fox_cum_bwd_{tag}", [dcq, dck_rows], [1.0, -1.0], True)
    (dff,), _, (dbf,) = ew_bwd(f"fox_logf_bwd_{tag}", f_log_sigmoid_bias, [s["ff"]], [], [p["bf"]], [dlf], [BF16])
    g["fox_bf"] = dbf[0, :N_HEADS]
    dq_b, dk_b, dv_b = sb_attn_bwd(s["sbqkv"], s["sb_tot"], dyb)
    (drgate, dhs), _, _ = ew_bwd(f"rg_out_bwd_{tag}", f_gelu_mul, [s["rgate"], s["hs"]], [], [], [dya], [BF16, F32],
                                 tm=EW_ROWS_WIDE)
    da, du = scan_bwd(s["a"], s["hs"], dhs)
    (dpre_r, dpre_i, dxa1), _, (g["rg_ba"], g["rg_bx"], g["rg_lam"]) = ew_bwd(
        f"rg_gates_bwd_{tag}", f_rg_gates, [s["pre_r"], s["pre_i"], s["xa"]], [], [p["ba"], p["bx"], p["lam"]], [da, du],
        [BF16, BF16, F32], tm=EW_ROWS_WIDE)
    dxa2 = matmul(f"rg_dx_{tag}", [dpre_r, dpre_i], [p["wa_bd"], p["wx_bd"]], "nt")
    g["rg_wa"] = _diag_blocks(matmul(f"rg_a_dw_{tag}", s["xa"], dpre_r, "tn", tm=512, tn=256))
    g["rg_wx"] = _diag_blocks(matmul(f"rg_x_dw_{tag}", s["xa"], dpre_i, "tn", tm=512, tn=256))
    drgx, dwb = conv_bwd(s["rgx"], p["conv_w8"], dxa1, dxa2)
    g["conv_w"] = dwb[:CONV_K]
    g["conv_b"] = dwb[CONV_K]
    cots = [drgx, drgate, dq_b, dk_b, dv_b, dq_c, dk_c, dv_c, dff, dmg]
    w_sb3 = [p["w_sbqkv"][:, k * ATT_W:(k + 1) * ATT_W] for k in range(3)]
    w_fox3 = [p["w_foxqkv"][:, k * ATT_W:(k + 1) * ATT_W] for k in range(3)]
    ws = [p["w_rgx"], p["w_gate"]] + w_sb3 + w_fox3 + [p["w_f"], p["w_merge"]]
    dh = matmul(f"in_dx_{tag}", cots, ws, "nt", tm=512)
    dws = [matmul(f"in_dw{k}_{tag}", s["h"], ct, "tn", tm=1024, tn=256) for k, ct in enumerate(cots)]
    dws[8] = dws[8][:, :N_HEADS]
    g["w_in"] = jnp.concatenate(dws, axis=1)
    (dx,), (dshift, dscale), (g["mix_norm"],) = ew_bwd(f"mix_norm_bwd_{tag}", f_norm_mod, [s["x"]], [shift, scale], [p["gain"]],
                                                       [dh], [F32], adds=[dxo])
    return dx, (dshift, dscale, dgate), g


def _final_loss(x, target, shift, scale, gain):
    n = x.shape[0]
    tm = min(EW_ROWS, SEQ)
    tpb = SEQ // tm

    def body(x_ref, t_ref, sh_ref, sc_ref, g_ref, loss_ref, dx_ref, dsh_ref, dsc_ref, dg_ref):
        i = pl.program_id(0)
        out, vjp = jax.vjp(f_norm_mod, x_ref[...], sh_ref[...], sc_ref[...], g_ref[...])
        diff = out - t_ref[...]
        dx, dsh, dsc, dg = vjp(diff * (1.0 / D))
        dx_ref[...] = dx
        sq = jnp.sum(jnp.sum(diff * diff, axis=1, keepdims=True), axis=0, keepdims=True)

        @pl.when(i % tpb == 0)
        def _():
            dsh_ref[...] = jnp.zeros_like(dsh_ref)
            dsc_ref[...] = jnp.zeros_like(dsc_ref)

        @pl.when(i == 0)
        def _():
            dg_ref[...] = jnp.zeros_like(dg_ref)
            loss_ref[...] = jnp.zeros_like(loss_ref)

        dsh_ref[...] += dsh
        dsc_ref[...] += dsc
        dg_ref[...] += dg
        loss_ref[...] += jnp.broadcast_to(sq, (1, LANES)) * (0.5 / D)

    row, bp, gp = _row_spec(D, tm), _bparam_spec(D, tpb), _gparam_spec((1, D))
    return pl.pallas_call(
        body, name="final_loss", grid=(n // tm,), in_specs=[row, row, bp, bp, gp],
        out_specs=[_gparam_spec((1, LANES)), row, bp, bp, gp],
        out_shape=[jax.ShapeDtypeStruct((1, LANES), F32), jax.ShapeDtypeStruct((n, D), F32),
                   jax.ShapeDtypeStruct(shift.shape, F32), jax.ShapeDtypeStruct(scale.shape, F32),
                   jax.ShapeDtypeStruct((1, D), F32)],
        compiler_params=pltpu.CompilerParams(dimension_semantics=("arbitrary",)),
    )(x, target, shift, scale, gain)


def kernel(x, c, ffn1_norm, ffn1_w1, ffn1_w3, ffn1_w2, mix_norm, w_in, conv_w, conv_b, rg_wa, rg_ba, rg_wx, rg_bx, rg_lam, fox_bf, merge_b, w_rg, w_sb, w_fox, w_o, ffn2_norm, ffn2_w1, ffn2_w3, ffn2_w2, ada_w, ada_b, final_norm, final_ada_w, final_ada_b, loss_target, m_ffn1_norm, m_ffn1_w1, m_ffn1_w3, m_ffn1_w2, m_mix_norm, m_w_in, m_conv_w, m_conv_b, m_rg_wa, m_rg_ba, m_rg_wx, m_rg_bx, m_rg_lam, m_fox_bf, m_merge_b, m_w_rg, m_w_sb, m_w_fox, m_w_o, m_ffn2_norm, m_ffn2_w1, m_ffn2_w3, m_ffn2_w2, m_ada_w, m_ada_b, m_final_norm, m_final_ada_w, m_final_ada_b, v_ffn1_norm, v_ffn1_w1, v_ffn1_w3, v_ffn1_w2, v_mix_norm, v_w_in, v_conv_w, v_conv_b, v_rg_wa, v_rg_ba, v_rg_wx, v_rg_bx, v_rg_lam, v_fox_bf, v_merge_b, v_w_rg, v_w_sb, v_w_fox, v_w_o, v_ffn2_norm, v_ffn2_w1, v_ffn2_w3, v_ffn2_w2, v_ada_w, v_ada_b, v_final_norm, v_final_ada_w, v_final_ada_b):
    given = dict(zip(["x", "c"] + WEIGHTS + ["loss_target"] + ["m_" + n for n in WEIGHTS] + ["v_" + n for n in WEIGHTS],
                     (x, c, ffn1_norm, ffn1_w1, ffn1_w3, ffn1_w2, mix_norm, w_in, conv_w, conv_b, rg_wa, rg_ba, rg_wx, rg_bx, rg_lam, fox_bf, merge_b, w_rg, w_sb, w_fox, w_o, ffn2_norm, ffn2_w1, ffn2_w3, ffn2_w2, ada_w, ada_b, final_norm, final_ada_w, final_ada_b, loss_target, m_ffn1_norm, m_ffn1_w1, m_ffn1_w3, m_ffn1_w2, m_mix_norm, m_w_in, m_conv_w, m_conv_b, m_rg_wa, m_rg_ba, m_rg_wx, m_rg_bx, m_rg_lam, m_fox_bf, m_merge_b, m_w_rg, m_w_sb, m_w_fox, m_w_o, m_ffn2_norm, m_ffn2_w1, m_ffn2_w3, m_ffn2_w2, m_ada_w, m_ada_b, m_final_norm, m_final_ada_w, m_final_ada_b, v_ffn1_norm, v_ffn1_w1, v_ffn1_w3, v_ffn1_w2, v_mix_norm, v_w_in, v_conv_w, v_conv_b, v_rg_wa, v_rg_ba, v_rg_wx, v_rg_bx, v_rg_lam, v_fox_bf, v_merge_b, v_w_rg, v_w_sb, v_w_fox, v_w_o, v_ffn2_norm, v_ffn2_w1, v_ffn2_w3, v_ffn2_w2, v_ada_w, v_ada_b, v_final_norm, v_final_ada_w, v_final_ada_b)))
    idx = my_index()
    n_batch = N_DEV * B_LOC
    ada_cols = ada_w.shape[2]
    fin_cols = final_ada_w.shape[1]

    small_in, small_in_meta = _pack([c, conv_w], LANES, 8, F32)
    c_parts, conv_w_parts = _unpack(all_gather("gather_c_conv", small_in), small_in_meta, lead=1)
    c_all = c_parts.reshape(n_batch, D)
    conv_w_all = _unshard(conv_w_parts, 2)
    c_act = ew_fwd("c_silu", _f_silu, [c_all], [], [], [D], [F32])[0]
    mod_cols = [matmul(f"ada_proj_{l}", c_act, ada_w[l], "nn") for l in range(DEPTH)]
    mod_cols.append(matmul("ada_proj_final", c_act, final_ada_w, "nn"))
    mod_g = all_gather("gather_mod", jnp.concatenate(mod_cols, axis=1))

    shards = {(l, group): [given[n][l].astype(BF16) for n in members] for l in range(DEPTH) for group, members in GROUPS}
    waves = [[(0, "ffn1")], [(0, "mix")], [(0, "ffn2")] + [(l, group) for l in range(1, DEPTH) for group, _ in GROUPS]]
    gather_handles, landed = {}, {}

    def start_wave(wave, behind, carrier):
        blocks, behind = lax.optimization_barrier(({key: shards[key] for key in wave}, behind))
        for key in wave:
            gather_handles[key], token = gather_start(f"gather_start_{key[1]}{key[0]}", blocks[key])
            carrier = carrier + token[0, 0]
        return behind, carrier

    def weights_of(l, group, after):
        key = (l, group)
        if key not in landed:
            landed[key] = gather_finish(f"gather_finish_{group}{l}", gather_handles[key], after)
        return {n: _unshard(b, GATHERED[n] - 1) for n, b in zip(dict(GROUPS)[group], landed[key])}

    first_blocks, mod_g = lax.optimization_barrier((shards[0, "ffn1"], mod_g))
    landed[0, "ffn1"] = all_gather("gather_first", first_blocks)
    landed[0, "ffn1"], mod_g = start_wave(waves[1], landed[0, "ffn1"], mod_g)

    mods = []
    for l in range(DEPTH):
        full = mod_g[:, :, l * ada_cols:(l + 1) * ada_cols].transpose(1, 0, 2).reshape(n_batch, N_DEV * ada_cols)
        full = ew_fwd(f"ada_bias_{l}", _f_add_bias, [full], [], [ada_b[l][None]], [full.shape[1]], [F32])[0]
        mods.append(lax.dynamic_slice_in_dim(full, idx * B_LOC, B_LOC, axis=0).reshape(B_LOC, 3, 3, D))
    fm = mod_g[:, :, DEPTH * ada_cols:].transpose(1, 0, 2).reshape(n_batch, N_DEV * fin_cols)
    fm = ew_fwd("ada_bias_final", _f_add_bias, [fm], [], [final_ada_b[None]], [fm.shape[1]], [F32])[0]
    fm = lax.dynamic_slice_in_dim(fm, idx * B_LOC, B_LOC, axis=0).reshape(B_LOC, 2, D)

    def mixer_params(l, w):
        wi = w["w_in"]
        cut = IN_CUTS
        return dict(
            gain=mix_norm[l][None], w_rgx=wi[:, cut[0]:cut[1]], w_gate=wi[:, cut[1]:cut[2]], w_sbqkv=wi[:, cut[2]:cut[3]],
            w_foxqkv=wi[:, cut[3]:cut[4]], w_f=_pad_lanes(wi[:, cut[4]:cut[5]]), w_merge=wi[:, cut[5]:cut[6]],
            conv_w8=jnp.pad(conv_w_all[l], ((0, 8 - CONV_K), (0, 0))), conv_b=conv_b[l][None],
            wa_bd=_block_diag(rg_wa[l]), wx_bd=_block_diag(rg_wx[l]), ba=rg_ba[l][None], bx=rg_bx[l][None], lam=rg_lam[l][None],
            bf=_pad_lanes(fox_bf[l][None]), merge_b=merge_b[l][None], w_rg=w["w_rg"], w_sb=w["w_sb"], w_fox=w["w_fox"],
            w_o=w["w_o"])

    n_tok = x.shape[0] * x.shape[1]
    h = x.reshape(n_tok, D)
    saved = []
    for l in range(DEPTH):
        m = mods[l]
        w1 = weights_of(l, "ffn1", m if l == 0 else h)
        h, s1 = _ffn_fwd(f"a{l}", h, _bp(m, 0, 0), _bp(m, 0, 1), _bp(m, 0, 2), ffn1_norm[l][None], w1["ffn1_w1"], w1["ffn1_w3"],
                         w1["ffn1_w2"])
        w2 = weights_of(l, "mix", h)
        if l == 0:
            landed[0, "mix"], m = start_wave(waves[2], landed[0, "mix"], m)
        p = mixer_params(l, w2)
        h, s2 = _mixer_fwd(f"{l}", h, _bp(m, 1, 0), _bp(m, 1, 1), _bp(m, 1, 2), p)
        w3 = weights_of(l, "ffn2", h)
        h, s3 = _ffn_fwd(f"b{l}", h, _bp(m, 2, 0), _bp(m, 2, 1), _bp(m, 2, 2), ffn2_norm[l][None], w3["ffn2_w1"], w3["ffn2_w3"],
                         w3["ffn2_w2"])
        saved.append((s1, s2, s3, p, w1, w3))
    loss_row, dh, dfshift, dfscale, dgain_final = _final_loss(h, loss_target.reshape(n_tok, D), fm[:, 0][:, None, :],
                                                              fm[:, 1][:, None, :], final_norm[None])

    grads = {n: [None] * DEPTH for n in WEIGHTS}
    d_mods = [None] * DEPTH
    scatter_handles = {}
    after_start = jnp.zeros((), F32)

    def scatter_blocks(l, group):
        return [_reshard(grads[n][l], GATHERED[n] - 1).astype(BF16) for n in dict(GROUPS)[group]]

    def start_scatter(l, group, g8s=None):
        g8s = scatter_blocks(l, group) if g8s is None else g8s
        scatter_handles[l, group], token = scatter_start(f"scatter_start_{group}{l}", g8s)
        return token[0, 0]

    for l in reversed(range(DEPTH)):
        m = mods[l]
        s1, s2, s3, p, w1, w3 = saved[l]
        dh, dm3, grads["ffn2_norm"][l], grads["ffn2_w1"][l], grads["ffn2_w3"][l], grads["ffn2_w2"][l] = _ffn_bwd(
            f"b{l}", dh, s3, _bp(m, 2, 0), _bp(m, 2, 1), _bp(m, 2, 2) + after_start, ffn2_norm[l][None], w3["ffn2_w1"],
            w3["ffn2_w3"], w3["ffn2_w2"])
        after_start = start_scatter(l, "ffn2")
        dh, dm2, gm = _mixer_bwd(f"{l}", dh, s2, _bp(m, 1, 0), _bp(m, 1, 1), _bp(m, 1, 2) + after_start, p)
        for n, gval in gm.items():
            grads[n][l] = gval
        after_start = start_scatter(l, "mix")
        dh, dm1, grads["ffn1_norm"][l], grads["ffn1_w1"][l], grads["ffn1_w3"][l], grads["ffn1_w2"][l] = _ffn_bwd(
            f"a{l}", dh, s1, _bp(m, 0, 0), _bp(m, 0, 1), _bp(m, 0, 2) + after_start, ffn1_norm[l][None], w1["ffn1_w1"],
            w1["ffn1_w3"], w1["ffn1_w2"])
        if l > 0:
            after_start = start_scatter(l, "ffn1")
        d_mods[l] = jnp.concatenate([t.reshape(B_LOC, D) for dm in (dm1, dm2, dm3) for t in dm], axis=1)
    grad_x = dh.reshape(x.shape)
    d_fm = jnp.concatenate([dfshift.reshape(B_LOC, D), dfscale.reshape(B_LOC, D)], axis=1)

    rep = {n: jnp.stack([t.reshape(given[n].shape[1:]) for t in grads[n]]) for n in REPLICATED if n != "final_norm"}
    rep["final_norm"] = dgain_final.reshape(D)
    rep["conv_w"] = jnp.stack(grads["conv_w"])
    rep_names = list(rep)
    rep_slab, rep_meta = _pack([rep[n] for n in rep_names], LANES, 8, F32)
    mod_slab, mod_meta = _pack(d_mods + [d_fm], LANES, 8, F32)
    small_g = all_gather("gather_small_grads", jnp.concatenate([mod_slab, rep_slab], axis=0))
    last_blocks, small_g = lax.optimization_barrier((scatter_blocks(0, "ffn1"), small_g))
    small_g = small_g + start_scatter(0, "ffn1", last_blocks)
    d_mod_all = [t.reshape(n_batch, -1) for t in _unpack(small_g[:, :mod_slab.shape[0]], mod_meta, lead=1)]
    rep_sum = add_blocks("sum_small_grads", [small_g[k, mod_slab.shape[0]:] for k in range(N_DEV)], F32)
    rep_grad = dict(zip(rep_names, _unpack(rep_sum, rep_meta)))
    final_g = {n: rep_grad[n] for n in REPLICATED}
    final_g["conv_w"] = lax.dynamic_slice_in_dim(rep_grad["conv_w"], idx * conv_w.shape[2], conv_w.shape[2], axis=2)
    final_g["ada_b"] = jnp.stack([sum_rows(f"ada_b_grad_{l}", d_mod_all[l])[0] for l in range(DEPTH)])
    final_g["final_ada_b"] = sum_rows("final_ada_b_grad", d_mod_all[DEPTH])[0]
    final_g["ada_w"] = jnp.stack([
        matmul(f"ada_w_grad_{l}", c_act, lax.dynamic_slice_in_dim(d_mod_all[l], idx * ada_cols, ada_cols, axis=1), "tn")
        for l in range(DEPTH)])
    final_g["final_ada_w"] = matmul(
        "final_ada_w_grad", c_act, lax.dynamic_slice_in_dim(d_mod_all[DEPTH], idx * fin_cols, fin_cols, axis=1), "tn")

    shard_g = {n: [None] * DEPTH for n in GATHERED}

    def finish_scatter(l, group, after):
        sums = scatter_finish(f"scatter_finish_{group}{l}", scatter_handles[l, group], after)
        for n, gval in zip(dict(GROUPS)[group], sums):
            shard_g[n][l] = gval

    for l in reversed(range(DEPTH)):
        for group in ("ffn2", "mix", "ffn1"):
            if (l, group) != (0, "ffn1"):
                finish_scatter(l, group, rep_sum)

    delta, new_m, new_v = {}, {}, {}
    last = dict(GROUPS)["ffn1"]
    sharded = [n for n in GATHERED if n not in last] + ["ada_w", "final_ada_w", "conv_w"] + list(last)
    for n in sharded:
        if n == last[0]:
            finish_scatter(0, "ffn1", delta["w_in"])
        if n in GATHERED:
            final_g[n] = jnp.stack(shard_g[n])
        delta[n], new_m[n], new_v[n] = adamw(f"adamw_{n}", given[n], final_g[n], given["m_" + n], given["v_" + n])
    rep_all = [n for n in WEIGHTS if n not in sharded]
    packed = [_pack([src[n] for n in rep_all], LANES, 8, F32)[0]
              for src in (given, final_g, {n: given["m_" + n] for n in rep_all}, {n: given["v_" + n] for n in rep_all})]
    rep_meta_all = _pack([given[n] for n in rep_all], LANES, 8, F32)[1]
    slab_rows = packed[0].shape[0]
    packed = [jnp.pad(t, ((0, -slab_rows % 256), (0, 0))) for t in packed]
    for store, slab_out in zip((delta, new_m, new_v), adamw("adamw_replicated", *packed)):
        store.update(zip(rep_all, _unpack(slab_out[:slab_rows], rep_meta_all)))

    loss = lax.psum(loss_row[0, 0], ("x", "y", "c"))
    return (loss, grad_x, *[final_g[n] for n in WEIGHTS], *[delta[n] for n in WEIGHTS], *[new_m[n] for n in WEIGHTS],
            *[new_v[n] for n in WEIGHTS])
```

```python
import functools

import numpy as np
import jax
import jax.numpy as jnp
from jax import lax
from jax.experimental import pallas as pl
from jax.experimental.pallas import tpu as pltpu

F32 = jnp.float32
BF16 = jnp.bfloat16
MESH = pl.DeviceIdType.MESH

N_DEV = 8
D = 1024
SEQ = 2048
B_LOC = 2
N_TOK = B_LOC * SEQ
DEPTH = 2
D_FF = 2816
RG_BLOCKS = 16
RG_C = 8.0
N_HEADS = 8
HEAD_DIM = 64
ATT_W = N_HEADS * HEAD_DIM
LANES = 128
EPS = 1e-6
ATT_SCALE = HEAD_DIM ** -0.5
CONV_K = 4

ADAM_LR = 0.001
ADAM_B1 = 0.9
ADAM_B2 = 0.999
ADAM_EPS = 1e-08
ADAM_WD = 0.01
ADAM_STEP = 10

EW_ROWS = 512
EW_ROWS_WIDE = 256
ATT_BLK = 512


def _pick_tile(dim, target):
    best = None
    for t in range(LANES, min(dim, target) + 1, LANES):
        if dim % t == 0:
            best = t
    return best if best is not None else dim


_DIMS = {"nn": (((1,), (0,)), ((), ())), "nt": (((1,), (1,)), ((), ())), "tn": (((0,), (0,)), ((), ()))}


def matmul(name, a_list, b_list, mode, out_dtype=F32, tm=1024, tn=512):
    if not isinstance(a_list, (list, tuple)):
        a_list, b_list = [a_list], [b_list]
    n = len(a_list)
    m_dim = a_list[0].shape[1] if mode == "tn" else a_list[0].shape[0]
    n_dim = b_list[0].shape[0] if mode == "nt" else b_list[0].shape[1]
    tm, tn = _pick_tile(m_dim, tm), _pick_tile(n_dim, tn)
    dims = _DIMS[mode]

    def body(*refs):
        o_ref = refs[-1]
        acc = None
        for a_ref, b_ref in zip(refs[:n], refs[n:2 * n]):
            d = lax.dot_general(a_ref[...].astype(BF16), b_ref[...].astype(BF16), dims, preferred_element_type=F32)
            acc = d if acc is None else acc + d
        o_ref[...] = acc.astype(o_ref.dtype)

    in_specs = []
    for a in a_list:
        if mode == "tn":
            in_specs.append(pl.BlockSpec((a.shape[0], tm), lambda i, j: (0, i)))
        else:
            in_specs.append(pl.BlockSpec((tm, a.shape[1]), lambda i, j: (i, 0)))
    for b in b_list:
        if mode == "nt":
            in_specs.append(pl.BlockSpec((tn, b.shape[1]), lambda i, j: (j, 0)))
        else:
            in_specs.append(pl.BlockSpec((b.shape[0], tn), lambda i, j: (0, j)))
    return pl.pallas_call(
        body, name=name, grid=(m_dim // tm, n_dim // tn), in_specs=in_specs,
        out_specs=pl.BlockSpec((tm, tn), lambda i, j: (i, j)),
        out_shape=jax.ShapeDtypeStruct((m_dim, n_dim), out_dtype),
        compiler_params=pltpu.CompilerParams(dimension_semantics=("parallel", "parallel")),
    )(*a_list, *b_list)


def _row_spec(w, tm):
    return pl.BlockSpec((tm, w), lambda i: (i, 0))


def _bparam_spec(w, tiles_per_batch):
    return pl.BlockSpec((None, 1, w), lambda i: (i // tiles_per_batch, 0, 0))


def _gparam_spec(shape):
    return pl.BlockSpec(shape, lambda i: (0, 0))


def ew_fwd(name, fn, rows, bparams, gparams, out_widths, out_dtypes, tm=EW_ROWS):
    n_rows = rows[0].shape[0]
    tm = min(tm, n_rows, SEQ)
    tpb = max(SEQ // tm, 1)
    nr, nb, ng = len(rows), len(bparams), len(gparams)

    def body(*refs):
        vals = [r[...] for r in refs[:nr + nb + ng]]
        outs = fn(*vals)
        if not isinstance(outs, (tuple, list)):
            outs = (outs,)
        for o_ref, o in zip(refs[nr + nb + ng:], outs):
            o_ref[...] = o.astype(o_ref.dtype)

    in_specs = ([_row_spec(r.shape[1], tm) for r in rows] + [_bparam_spec(p.shape[2], tpb) for p in bparams]
                + [_gparam_spec(g.shape) for g in gparams])
    outs = pl.pallas_call(
        body, name=name, grid=(n_rows // tm,), in_specs=in_specs,
        out_specs=[_row_spec(w, tm) for w in out_widths],
        out_shape=[jax.ShapeDtypeStruct((n_rows, w), dt) for w, dt in zip(out_widths, out_dtypes)],
        compiler_params=pltpu.CompilerParams(dimension_semantics=("parallel",)),
    )(*rows, *bparams, *gparams)
    return outs


def ew_bwd(name, fn, rows, bparams, gparams, cts, row_grad_dtypes, adds=(), tm=EW_ROWS):
    n_rows = rows[0].shape[0]
    tm = min(tm, n_rows, SEQ)
    tpb = max(SEQ // tm, 1)
    nr, nb, ng, nc = len(rows), len(bparams), len(gparams), len(cts)
    adds = list(adds) + [None] * (nr - len(adds))
    add_idx = [k for k in range(nr) if adds[k] is not None]
    want = [k for k in range(nr) if row_grad_dtypes[k] is not None]

    def body(*refs):
        pos = nr + nb + ng
        vals = [r[...] for r in refs[:pos]]
        ct_vals = [r[...].astype(F32) for r in refs[pos:pos + nc]]
        pos += nc
        add_vals = {k: refs[pos + q][...] for q, k in enumerate(add_idx)}
        pos += len(add_idx)
        out_refs = refs[pos:]
        f32_vals = [v.astype(F32) for v in vals]
        outs, vjp = jax.vjp(lambda *a: fn(*a), *f32_vals)
        single = not isinstance(outs, (tuple, list))
        grads = vjp(ct_vals[0].astype(outs.dtype) if single else tuple(c.astype(o.dtype) for c, o in zip(ct_vals, outs)))
        i = pl.program_id(0)
        q = 0
        for k in want:
            g = grads[k]
            if k in add_vals:
                g = g + add_vals[k].astype(F32)
            out_refs[q][...] = g.astype(out_refs[q].dtype)
            q += 1
        for k in range(nb):
            ref = out_refs[q]
            q += 1

            @pl.when(i % tpb == 0)
            def _():
                ref[...] = jnp.zeros_like(ref)

            ref[...] += grads[nr + k]
        for k in range(ng):
            ref = out_refs[q]
            q += 1

            @pl.when(i == 0)
            def _():
                ref[...] = jnp.zeros_like(ref)

            ref[...] += grads[nr + nb + k]

    in_specs = ([_row_spec(r.shape[1], tm) for r in rows] + [_bparam_spec(p.shape[2], tpb) for p in bparams]
                + [_gparam_spec(g.shape) for g in gparams] + [_row_spec(c.shape[1], tm) for c in cts]
                + [_row_spec(adds[k].shape[1], tm) for k in add_idx])
    out_specs = ([_row_spec(rows[k].shape[1], tm) for k in want] + [_bparam_spec(p.shape[2], tpb) for p in bparams]
                 + [_gparam_spec(g.shape) for g in gparams])
    out_shape = ([jax.ShapeDtypeStruct(rows[k].shape, row_grad_dtypes[k]) for k in want]
                 + [jax.ShapeDtypeStruct(p.shape, F32) for p in bparams] + [jax.ShapeDtypeStruct(g.shape, F32) for g in gparams])
    outs = pl.pallas_call(
        body, name=name, grid=(n_rows // tm,), in_specs=in_specs, out_specs=out_specs, out_shape=out_shape,
        compiler_params=pltpu.CompilerParams(dimension_semantics=("arbitrary",)),
    )(*rows, *bparams, *gparams, *cts, *[adds[k] for k in add_idx])
    d_rows = list(outs[:len(want)])
    d_b = list(outs[len(want):len(want) + nb])
    d_g = list(outs[len(want) + nb:])
    return d_rows, d_b, d_g


def f_norm_mod(x, shift, scale, gain):
    x = x.astype(F32)
    y = x * lax.rsqrt(jnp.mean(x * x, axis=-1, keepdims=True) + EPS)
    return (y * gain) * (1.0 + scale) + shift


def f_swiglu(a, b3):
    a = a.astype(F32)
    return (a * jax.nn.sigmoid(a)) * b3.astype(F32)


def f_resid(coef, x, y, gate):
    return x.astype(F32) + (coef * (1.0 + gate)) * y.astype(F32)


def f_rg_gates(pre_r, pre_i, xa, ba, bx, lam):
    r = jax.nn.sigmoid(pre_r + ba)
    i = jax.nn.sigmoid(pre_i + bx)
    softplus_neg_lam = jnp.maximum(-lam, 0.0) + jnp.log(1.0 + jnp.exp(-jnp.abs(lam)))
    log_a = (-RG_C) * r * softplus_neg_lam
    a = jnp.exp(log_a)
    u = jnp.sqrt(1.0 - a * a) * (i * xa)
    return a, u


def f_gelu_mul(gate, hs):
    g = gate.astype(F32)
    gelu = 0.5 * g * (1.0 + jnp.tanh(0.7978845608028654 * (g + 0.044715 * g * g * g)))
    return gelu * hs.astype(F32)


def f_log_sigmoid_bias(f, bf):
    z = f.astype(F32) + bf
    return jnp.minimum(z, 0.0) - jnp.log(1.0 + jnp.exp(-jnp.abs(z)))


def f_merge(mg, pa, pb, pc, merge_b):
    g = jax.nn.sigmoid(mg.astype(F32) + merge_b)
    return g[:, :D] * pa.astype(F32) + g[:, D:2 * D] * pb.astype(F32) + g[:, 2 * D:] * pc.astype(F32)


CONV_CB = 256
SCAN_CB = 512
SCAN_CHAINS = 4
CUM_RB = 512


def _shift_down(x, d):
    if d == 0:
        return x
    rows = lax.broadcasted_iota(jnp.int32, x.shape, 0)
    return jnp.where(rows >= d, pltpu.roll(x, d, axis=0), 0.0)


def _shift_up(x, d):
    if d == 0:
        return x
    s = x.shape[0]
    rows = lax.broadcasted_iota(jnp.int32, x.shape, 0)
    return jnp.where(rows < s - d, pltpu.roll(x, s - d, axis=0), 0.0)


def conv_fwd(x, w8, b):
    n, c = x.shape
    nb = n // SEQ

    def body(x_ref, w_ref, b_ref, y_ref):
        xv = x_ref[...]
        acc = jnp.broadcast_to(b_ref[...], xv.shape)
        for k in range(CONV_K):
            acc = acc + w_ref[k:k + 1, :] * _shift_down(xv, CONV_K - 1 - k)
        y_ref[...] = acc

    return pl.pallas_call(
        body, name="conv_fwd", grid=(c // CONV_CB, nb),
        in_specs=[pl.BlockSpec((SEQ, CONV_CB), lambda j, i: (i, j)), pl.BlockSpec((8, CONV_CB), lambda j, i: (0, j)),
                  pl.BlockSpec((1, CONV_CB), lambda j, i: (0, j))],
        out_specs=pl.BlockSpec((SEQ, CONV_CB), lambda j, i: (i, j)),
        out_shape=jax.ShapeDtypeStruct((n, c), F32),
        compiler_params=pltpu.CompilerParams(dimension_semantics=("parallel", "parallel")),
    )(x, w8, b)


def conv_bwd(x, w8, dy1, dy2):
    n, c = x.shape
    nb = n // SEQ

    def body(x_ref, w_ref, dy1_ref, dy2_ref, dx_ref, dwb_ref):
        xv = x_ref[...]
        dy = dy1_ref[...] + dy2_ref[...]
        dx = jnp.zeros_like(xv)
        parts = []
        for k in range(CONV_K):
            d = CONV_K - 1 - k
            dx = dx + w_ref[k:k + 1, :] * _shift_up(dy, d)
            parts.append(jnp.sum(dy * _shift_down(xv, d), axis=0, keepdims=True))
        parts.append(jnp.sum(dy, axis=0, keepdims=True))
        parts.append(jnp.zeros((8 - len(parts), xv.shape[1]), F32))
        dx_ref[...] = dx.astype(BF16)

        @pl.when(pl.program_id(1) == 0)
        def _():
            dwb_ref[...] = jnp.zeros_like(dwb_ref)

        dwb_ref[...] += jnp.concatenate(parts, axis=0)

    return pl.pallas_call(
        body, name="conv_bwd", grid=(c // CONV_CB, nb),
        in_specs=[pl.BlockSpec((SEQ, CONV_CB), lambda j, i: (i, j)), pl.BlockSpec((8, CONV_CB), lambda j, i: (0, j)),
                  pl.BlockSpec((SEQ, CONV_CB), lambda j, i: (i, j)), pl.BlockSpec((SEQ, CONV_CB), lambda j, i: (i, j))],
        out_specs=[pl.BlockSpec((SEQ, CONV_CB), lambda j, i: (i, j)), pl.BlockSpec((8, CONV_CB), lambda j, i: (0, j))],
        out_shape=[jax.ShapeDtypeStruct((n, c), BF16), jax.ShapeDtypeStruct((8, c), F32)],
        compiler_params=pltpu.CompilerParams(dimension_semantics=("parallel", "arbitrary")),
    )(x, w8, dy1, dy2)


def scan_fwd(a, u):
    n, c = a.shape
    q = SEQ // SCAN_CHAINS

    def body(a_ref, u_ref, h_ref, p_ref):
        def step(t, carry):
            hs, ps = carry
            new_h, new_p = [], []
            for k in range(SCAN_CHAINS):
                row = k * q + t
                av = a_ref[pl.ds(row, 1), :]
                hk = av * hs[k] + u_ref[pl.ds(row, 1), :]
                h_ref[pl.ds(row, 1), :] = hk
                new_h.append(hk)
                pk = av * ps[k]
                if k > 0:
                    p_ref[pl.ds(row, 1), :] = pk
                new_p.append(pk)
            return tuple(new_h), tuple(new_p)

        zero, one = jnp.zeros((1, SCAN_CB), F32), jnp.ones((1, SCAN_CB), F32)
        lax.fori_loop(0, q, step, ((zero,) * SCAN_CHAINS, (one,) * SCAN_CHAINS), unroll=4)
        for k in range(1, SCAN_CHAINS):
            rows = pl.ds(k * q, q)
            h_ref[rows, :] = h_ref[rows, :] + p_ref[rows, :] * h_ref[pl.ds(k * q - 1, 1), :]

    spec = pl.BlockSpec((SEQ, SCAN_CB), lambda i, j: (i, j))
    return pl.pallas_call(
        body, name="scan_fwd", grid=(n // SEQ, c // SCAN_CB), in_specs=[spec, spec], out_specs=spec,
        out_shape=jax.ShapeDtypeStruct((n, c), F32), scratch_shapes=[pltpu.VMEM((SEQ, SCAN_CB), F32)],
        compiler_params=pltpu.CompilerParams(dimension_semantics=("parallel", "parallel")),
    )(a, u)


def scan_bwd(a, h, g):
    n, c = a.shape
    q = SEQ // SCAN_CHAINS
    cb = SCAN_CB // 2

    def body(a_ref, h_ref, g_ref, da_ref, du_ref, r_ref):
        def step(j, carry):
            cs, rs = carry
            new_c, new_r = [], []
            for k in range(SCAN_CHAINS):
                row = k * q + (q - 1 - j)
                dh = g_ref[pl.ds(row, 1), :] + cs[k]
                du_ref[pl.ds(row, 1), :] = dh
                av = a_ref[pl.ds(row, 1), :]
                if k < SCAN_CHAINS - 1:
                    r_ref[pl.ds(row, 1), :] = rs[k]
                new_c.append(av * dh)
                new_r.append(av * rs[k])
            return tuple(new_c), tuple(new_r)

        zero, one = jnp.zeros((1, cb), F32), jnp.ones((1, cb), F32)
        lax.fori_loop(0, q, step, ((zero,) * SCAN_CHAINS, (one,) * SCAN_CHAINS), unroll=4)
        for k in reversed(range(SCAN_CHAINS - 1)):
            rows, nxt = pl.ds(k * q, q), pl.ds((k + 1) * q, 1)
            du_ref[rows, :] = du_ref[rows, :] + r_ref[rows, :] * (a_ref[nxt, :] * du_ref[nxt, :])
        da_ref[...] = du_ref[...] * _shift_down(h_ref[...], 1)

    spec = pl.BlockSpec((SEQ, cb), lambda i, j: (i, j))
    return pl.pallas_call(
        body, name="scan_bwd", grid=(n // SEQ, c // cb), in_specs=[spec, spec, spec], out_specs=[spec, spec],
        out_shape=[jax.ShapeDtypeStruct((n, c), F32), jax.ShapeDtypeStruct((n, c), F32)],
        scratch_shapes=[pltpu.VMEM((SEQ, cb), F32)],
        compiler_params=pltpu.CompilerParams(dimension_semantics=("parallel", "parallel")),
    )(a, h, g)


def _split3_dot(m, x):
    hi = x.astype(BF16)
    r1 = x - hi.astype(F32)
    mid = r1.astype(BF16)
    lo = (r1 - mid.astype(F32)).astype(BF16)
    dot = functools.partial(jnp.dot, preferred_element_type=F32)
    return dot(m, hi) + dot(m, mid) + dot(m, lo)


def seq_cumsum(name, xs, signs, reverse):
    n, w = xs[0].shape
    nx = len(xs)
    rb = min(CUM_RB, SEQ)

    def body(*refs):
        x = None
        for r, sg in zip(refs[:nx], signs):
            x = sg * r[...] if x is None else x + sg * r[...]
        q0 = pl.program_id(1) * rb
        row = q0 + lax.broadcasted_iota(jnp.int32, (rb, SEQ), 0)
        col = lax.broadcasted_iota(jnp.int32, (rb, SEQ), 1)
        tri = ((col >= row) if reverse else (col <= row)).astype(BF16)
        refs[nx][...] = _split3_dot(tri, x)

    return pl.pallas_call(
        body, name=name, grid=(n // SEQ, SEQ // rb),
        in_specs=[pl.BlockSpec((SEQ, w), lambda i, j: (i, 0)) for _ in xs],
        out_specs=pl.BlockSpec((rb, w), lambda i, j: (i * (SEQ // rb) + j, 0)),
        out_shape=jax.ShapeDtypeStruct((n, w), F32),
        compiler_params=pltpu.CompilerParams(dimension_semantics=("parallel", "parallel")),
    )(*xs)


N_PAIRS = N_HEADS // 2


def _dot_nt(a, b):
    return lax.dot_general(a, b, _DIMS["nt"], preferred_element_type=F32)


def _dot_tn(a, b):
    return lax.dot_general(a, b, _DIMS["tn"], preferred_element_type=F32)


def _dot_nn(a, b):
    return lax.dot_general(a, b, _DIMS["nn"], preferred_element_type=F32)


def _split2_dot(x, m):
    hi = x.astype(BF16)
    lo = (x - hi.astype(F32)).astype(BF16)
    return _dot_nn(hi, m) + _dot_nn(lo, m)


def _head_mask(j):
    lane = lax.broadcasted_iota(jnp.int32, (1, LANES), 1)
    return (lane // HEAD_DIM) == j


def _lane_pick(x, h):
    lane = lax.broadcasted_iota(jnp.int32, x.shape, 1)
    return jnp.sum(jnp.where(lane == h, x, 0.0), axis=1, keepdims=True)


def _lane_put(col, h):
    lane = lax.broadcasted_iota(jnp.int32, (col.shape[0], LANES), 1)
    return jnp.where(lane == h, col, 0.0)


def _softplus(z):
    return jnp.maximum(z, 0.0) + jnp.log(1.0 + jnp.exp(-jnp.abs(z)))


def _qkv_specs():
    return [pl.BlockSpec((SEQ, LANES), lambda b, p: (b, p)),
            pl.BlockSpec((SEQ, LANES), lambda b, p: (b, N_PAIRS + p)),
            pl.BlockSpec((SEQ, LANES), lambda b, p: (b, 2 * N_PAIRS + p))]


def _pair_spec():
    return pl.BlockSpec((SEQ, LANES), lambda b, p: (b, p))


def _below_diagonal(strictly):
    t = ATT_BLK
    row = lax.broadcasted_iota(jnp.int32, (t, t), 0)
    col = lax.broadcasted_iota(jnp.int32, (t, t), 1)
    return (row > col) if strictly else (row >= col)


def _over_key_blocks(qi, step, init, reverse):
    t = ATT_BLK
    q0 = pl.multiple_of(qi * t, t)

    def off_diagonal(kk, carry):
        ki = (qi - 1 - kk) if reverse else kk
        return step(pl.multiple_of(ki * t, t), carry, False)

    if reverse:
        return lax.fori_loop(0, qi, off_diagonal, step(q0, init, True))
    return step(q0, lax.fori_loop(0, qi, off_diagonal, init), True)


def _masked_q(qb, j):
    return (jnp.where(_head_mask(j), qb, 0.0) * ATT_SCALE).astype(BF16)


def sb_attn_fwd(qkv):
    n = qkv.shape[0]
    t = ATT_BLK

    def body(q_ref, k_ref, v_ref, o_ref, tot_ref):
        pair = pl.program_id(1)
        strict = _below_diagonal(True)
        later = strict.astype(BF16)

        @pl.when(pair == 0)
        def _():
            tot_ref[...] = jnp.zeros_like(tot_ref)

        def q_block(qi, _):
            q0 = pl.multiple_of(qi * t, t)
            qb = q_ref[pl.ds(q0, t), :]
            qms = [_masked_q(qb, j) for j in range(2)]

            def step(k0, carry, diagonal):
                kb = k_ref[pl.ds(k0, t), :].astype(BF16)
                vb = v_ref[pl.ds(k0, t), :].astype(BF16)
                heads = range(2)
                zs = [_dot_nt(qms[j], kb) for j in heads]
                sps = [_softplus(z) for z in zs]
                log_keeps = [(jnp.where(strict, -sp, 0.0) if diagonal else -sp) for sp in sps]
                right_l = [_split2_dot(lk, later) for lk in log_keeps]
                atts = [jnp.exp((zs[j] - sps[j]) + right_l[j] + carry[j][0]) for j in heads]
                if diagonal:
                    atts = [jnp.where(strict, att, 0.0) for att in atts]
                return tuple((carry[j][0] + jnp.sum(log_keeps[j], axis=1, keepdims=True),
                              carry[j][1] + _dot_nn(atts[j].astype(BF16), vb)) for j in heads)

            init = ((jnp.zeros((t, 1), F32), jnp.zeros((t, LANES), F32)),) * 2
            (tot0, acc0), (tot1, acc1) = _over_key_blocks(qi, step, init, reverse=True)
            o_ref[pl.ds(q0, t), :] = jnp.where(_head_mask(0), acc0, acc1)
            tot_ref[pl.ds(q0, t), :] += _lane_put(tot0, 2 * pair) + _lane_put(tot1, 2 * pair + 1)
            return 0

        lax.fori_loop(0, SEQ // t, q_block, 0)

    batch_spec = pl.BlockSpec((SEQ, LANES), lambda b, p: (b, 0))
    return pl.pallas_call(
        body, name="sb_attn_fwd", grid=(n // SEQ, N_PAIRS), in_specs=_qkv_specs(), out_specs=[_pair_spec(), batch_spec],
        out_shape=[jax.ShapeDtypeStruct((n, ATT_W), F32), jax.ShapeDtypeStruct((n, LANES), F32)],
        compiler_params=pltpu.CompilerParams(dimension_semantics=("parallel", "arbitrary")),
    )(qkv, qkv, qkv)


def sb_attn_bwd(qkv, tot, do):
    n = qkv.shape[0]
    t = ATT_BLK

    def body(q_ref, k_ref, v_ref, tot_ref, do_ref, dq_ref, dk_ref, dv_ref, dk_acc, dv_acc):
        pair = pl.program_id(1)
        strict = _below_diagonal(True)
        upto = jnp.logical_not(strict).astype(BF16)
        dk_acc[...] = jnp.zeros_like(dk_acc)
        dv_acc[...] = jnp.zeros_like(dv_acc)

        def q_block(qi, _):
            q0 = pl.multiple_of(qi * t, t)
            qb = q_ref[pl.ds(q0, t), :]
            tot_q = tot_ref[pl.ds(q0, t), :]
            dob = do_ref[pl.ds(q0, t), :].astype(F32)
            qms = [_masked_q(qb, j) for j in range(2)]
            doms = [jnp.where(_head_mask(j), dob, 0.0).astype(BF16) for j in range(2)]
            totals = [_lane_pick(tot_q, 2 * pair + j) for j in range(2)]

            def step(k0, carry, diagonal):
                kb = k_ref[pl.ds(k0, t), :].astype(BF16)
                vb = v_ref[pl.ds(k0, t), :].astype(BF16)
                heads = range(2)
                zs = [_dot_nt(qms[j], kb) for j in heads]
                d_atts = [_dot_nt(doms[j], vb) for j in heads]
                sps = [_softplus(z) for z in zs]
                log_keeps = [(jnp.where(strict, -sp, 0.0) if diagonal else -sp) for sp in sps]
                log_betas = [z - sp for z, sp in zip(zs, sps)]
                left_l = [_split2_dot(lk, upto) for lk in log_keeps]
                atts = [jnp.exp(log_betas[j] + (totals[j] - (carry[j][0] + left_l[j]))) for j in heads]
                if diagonal:
                    atts = [jnp.where(strict, att, 0.0) for att in atts]
                gs = [att * d_att for att, d_att in zip(atts, d_atts)]
                dv = _dot_tn(atts[0].astype(BF16), doms[0]) + _dot_tn(atts[1].astype(BF16), doms[1])
                left_g = [_dot_nn(g.astype(BF16), upto) for g in gs]
                dzs = [gs[j] - jnp.exp(log_betas[j]) * (carry[j][1] + left_g[j]) for j in heads]
                if diagonal:
                    dzs = [jnp.where(strict, dz, 0.0) for dz in dzs]
                dzs = [dz.astype(BF16) for dz in dzs]
                dk = _dot_tn(dzs[0], qms[0]) + _dot_tn(dzs[1], qms[1])
                dk_acc[pl.ds(k0, t), :] += dk
                dv_acc[pl.ds(k0, t), :] += dv
                return tuple((carry[j][0] + jnp.sum(log_keeps[j], axis=1, keepdims=True),
                              carry[j][1] + jnp.sum(gs[j], axis=1, keepdims=True), carry[j][2] + _dot_nn(dzs[j], kb))
                             for j in heads)

            zero = jnp.zeros((t, 1), F32)
            init = ((zero, zero, jnp.zeros((t, LANES), F32)),) * 2
            (_, _, dq0), (_, _, dq1) = _over_key_blocks(qi, step, init, reverse=False)
            dq_ref[pl.ds(q0, t), :] = (jnp.where(_head_mask(0), dq0, dq1) * ATT_SCALE).astype(BF16)
            return 0

        lax.fori_loop(0, SEQ // t, q_block, 0)
        dk_ref[...] = dk_acc[...].astype(BF16)
        dv_ref[...] = dv_acc[...].astype(BF16)

    out = jax.ShapeDtypeStruct((n, ATT_W), BF16)
    batch_spec = pl.BlockSpec((SEQ, LANES), lambda b, p: (b, 0))
    return pl.pallas_call(
        body, name="sb_attn_bwd", grid=(n // SEQ, N_PAIRS), in_specs=_qkv_specs() + [batch_spec, _pair_spec()],
        out_specs=[_pair_spec()] * 3, out_shape=[out, out, out],
        scratch_shapes=[pltpu.VMEM((SEQ, LANES), F32), pltpu.VMEM((SEQ, LANES), F32)],
        compiler_params=pltpu.CompilerParams(dimension_semantics=("parallel", "parallel")),
    )(qkv, qkv, qkv, tot, do)


NEG_BIG = -1e30


def fox_attn_fwd(qkv, cum, cum_t):
    n = qkv.shape[0]
    t = ATT_BLK

    def body(q_ref, k_ref, v_ref, cum_ref, cumt_ref, o_ref, lse_ref):
        pair = pl.program_id(1)
        causal = _below_diagonal(False)

        @pl.when(pair == 0)
        def _():
            lse_ref[...] = jnp.zeros_like(lse_ref)

        def q_block(qi, _):
            q0 = pl.multiple_of(qi * t, t)
            qb = q_ref[pl.ds(q0, t), :]
            cum_q = cum_ref[pl.ds(q0, t), :]
            qms = [_masked_q(qb, j) for j in range(2)]
            cqs = [_lane_pick(cum_q, 2 * pair + j) for j in range(2)]

            def step(k0, carry, diagonal):
                kb = k_ref[pl.ds(k0, t), :].astype(BF16)
                vb = v_ref[pl.ds(k0, t), :].astype(BF16)
                heads = range(2)
                zs = [_dot_nt(qms[j], kb) + cqs[j] - cumt_ref[pl.ds(2 * pair + j, 1), pl.ds(k0, t)] for j in heads]
                if diagonal:
                    zs = [jnp.where(causal, z, NEG_BIG) for z in zs]
                m_new = [jnp.maximum(carry[j][0], jnp.max(zs[j], axis=1, keepdims=True)) for j in heads]
                ps = [jnp.exp(zs[j] - m_new[j]) for j in heads]
                alphas = [jnp.exp(carry[j][0] - m_new[j]) for j in heads]
                return tuple((m_new[j], alphas[j] * carry[j][1] + jnp.sum(ps[j], axis=1, keepdims=True),
                              alphas[j] * carry[j][2] + _dot_nn(ps[j].astype(BF16), vb)) for j in heads)

            init = ((jnp.full((t, 1), NEG_BIG, F32), jnp.zeros((t, 1), F32), jnp.zeros((t, LANES), F32)),) * 2
            (m0, l0, acc0), (m1, l1, acc1) = _over_key_blocks(qi, step, init, reverse=False)
            o_ref[pl.ds(q0, t), :] = jnp.where(_head_mask(0), acc0 / l0, acc1 / l1)
            lse_ref[pl.ds(q0, t), :] += _lane_put(m0 + jnp.log(l0), 2 * pair) + _lane_put(m1 + jnp.log(l1), 2 * pair + 1)
            return 0

        lax.fori_loop(0, SEQ // t, q_block, 0)

    batch_spec = pl.BlockSpec((SEQ, LANES), lambda b, p: (b, 0))
    return pl.pallas_call(
        body, name="fox_attn_fwd", grid=(n // SEQ, N_PAIRS),
        in_specs=_qkv_specs() + [batch_spec, pl.BlockSpec((None, N_HEADS, SEQ), lambda b, p: (b, 0, 0))],
        out_specs=[_pair_spec(), batch_spec],
        out_shape=[jax.ShapeDtypeStruct((n, ATT_W), F32), jax.ShapeDtypeStruct((n, LANES), F32)],
        compiler_params=pltpu.CompilerParams(dimension_semantics=("parallel", "arbitrary")),
    )(qkv, qkv, qkv, cum, cum_t)


def fox_attn_bwd(qkv, cum, cum_t, lse, o, do):
    n = qkv.shape[0]
    t = ATT_BLK

    def body(q_ref, k_ref, v_ref, cum_ref, cumt_ref, lse_ref, o_ref, do_ref, dq_ref, dk_ref, dv_ref, dcq_ref, dck_ref,
             dk_acc, dv_acc):
        pair = pl.program_id(1)
        causal = _below_diagonal(False)
        dk_acc[...] = jnp.zeros_like(dk_acc)
        dv_acc[...] = jnp.zeros_like(dv_acc)

        @pl.when(pair == 0)
        def _():
            dcq_ref[...] = jnp.zeros_like(dcq_ref)
            dck_ref[...] = jnp.zeros_like(dck_ref)

        def q_block(qi, _):
            q0 = pl.multiple_of(qi * t, t)
            qb = q_ref[pl.ds(q0, t), :]
            ob = o_ref[pl.ds(q0, t), :]
            dob = do_ref[pl.ds(q0, t), :].astype(F32)
            cum_q = cum_ref[pl.ds(q0, t), :]
            lse_q = lse_ref[pl.ds(q0, t), :]
            qms = [_masked_q(qb, j) for j in range(2)]
            dom32 = [jnp.where(_head_mask(j), dob, 0.0) for j in range(2)]
            doms = [d.astype(BF16) for d in dom32]
            deltas = [jnp.sum(d * ob, axis=1, keepdims=True) for d in dom32]
            cqs = [_lane_pick(cum_q, 2 * pair + j) for j in range(2)]
            lqs = [_lane_pick(lse_q, 2 * pair + j) for j in range(2)]

            def step(k0, carry, diagonal):
                kb = k_ref[pl.ds(k0, t), :].astype(BF16)
                vb = v_ref[pl.ds(k0, t), :].astype(BF16)
                heads = range(2)
                zs = [_dot_nt(qms[j], kb) + cqs[j] - cumt_ref[pl.ds(2 * pair + j, 1), pl.ds(k0, t)] for j in heads]
                d_ps = [_dot_nt(doms[j], vb) for j in heads]
                if diagonal:
                    zs = [jnp.where(causal, z, NEG_BIG) for z in zs]
                ps = [jnp.exp(zs[j] - lqs[j]) for j in heads]
                dv_acc[pl.ds(k0, t), :] += _dot_tn(ps[0].astype(BF16), doms[0]) + _dot_tn(ps[1].astype(BF16), doms[1])
                dzs = [ps[j] * (d_ps[j] - deltas[j]) for j in heads]
                dzb = [dz.astype(BF16) for dz in dzs]
                dk_acc[pl.ds(k0, t), :] += _dot_tn(dzb[0], qms[0]) + _dot_tn(dzb[1], qms[1])
                for j in heads:
                    dck_ref[pl.ds(2 * pair + j, 1), pl.ds(k0, t)] += jnp.sum(dzs[j], axis=0, keepdims=True)
                return tuple((carry[j][0] + _dot_nn(dzb[j], kb), carry[j][1] + jnp.sum(dzs[j], axis=1, keepdims=True))
                             for j in heads)

            init = ((jnp.zeros((t, LANES), F32), jnp.zeros((t, 1), F32)),) * 2
            (dq0, dcq0), (dq1, dcq1) = _over_key_blocks(qi, step, init, reverse=False)
            dq_ref[pl.ds(q0, t), :] = (jnp.where(_head_mask(0), dq0, dq1) * ATT_SCALE).astype(BF16)
            dcq_ref[pl.ds(q0, t), :] += _lane_put(dcq0, 2 * pair) + _lane_put(dcq1, 2 * pair + 1)
            return 0

        lax.fori_loop(0, SEQ // t, q_block, 0)
        dk_ref[...] = dk_acc[...].astype(BF16)
        dv_ref[...] = dv_acc[...].astype(BF16)

    batch_spec = pl.BlockSpec((SEQ, LANES), lambda b, p: (b, 0))
    t_spec = pl.BlockSpec((None, N_HEADS, SEQ), lambda b, p: (b, 0, 0))
    out = jax.ShapeDtypeStruct((n, ATT_W), BF16)
    return pl.pallas_call(
        body, name="fox_attn_bwd", grid=(n // SEQ, N_PAIRS),
        in_specs=_qkv_specs() + [batch_spec, t_spec, batch_spec, _pair_spec(), _pair_spec()],
        out_specs=[_pair_spec()] * 3 + [batch_spec, t_spec],
        scratch_shapes=[pltpu.VMEM((SEQ, LANES), F32), pltpu.VMEM((SEQ, LANES), F32)],
        out_shape=[out, out, out, jax.ShapeDtypeStruct((n, LANES), F32), jax.ShapeDtypeStruct((n // SEQ, N_HEADS, SEQ), F32)],
        compiler_params=pltpu.CompilerParams(dimension_semantics=("parallel", "arbitrary")),
    )(qkv, qkv, qkv, cum, cum_t, lse, o, do)


_HBM = pl.BlockSpec(memory_space=pl.ANY)


def _my_place():
    return lax.axis_index("x"), lax.axis_index("y"), lax.axis_index("c")


def my_index():
    mx, my, mc = _my_place()
    return 4 * mx + 2 * my + mc


def all_gather(name, xs):
    single = not isinstance(xs, (list, tuple))
    xs = [xs] if single else list(xs)
    na = len(xs)

    def body(*refs):
        x_refs, out_refs = refs[:na], refs[na:2 * na]
        send_sems, recv_sems, local_sems = refs[2 * na:]
        mx, my, mc = _my_place()
        me, sibling = (mx, my, mc), (mx, my, 1 - mc)
        chips = [(1 - mx, my), (mx, 1 - my), (1 - mx, 1 - my)]

        def slot(a, px, py, pc):
            return out_refs[a].at[4 * px + 2 * py + pc]

        def copy(a, k, block, to, src=None):
            return pltpu.make_async_remote_copy(
                src_ref=slot(a, *block) if src is None else src, dst_ref=slot(a, *block),
                send_sem=send_sems.at[7 * a + k], recv_sem=recv_sems.at[7 * a + k], device_id=to, device_id_type=MESH)

        mine = [pltpu.make_async_copy(x_refs[a], slot(a, *me), local_sems.at[a]) for a in range(na)]
        for cp in mine:
            cp.start()
        first = []
        for j, chip in enumerate(chips):
            first += [copy(a, 1 + j, me, (*chip, mc), src=x_refs[a]) for a in range(na)]
        first += [copy(a, 0, me, sibling, src=x_refs[a]) for a in range(na)]
        for cp in first:
            cp.start()
        passed = []
        for j, chip in enumerate(chips):
            for a in range(na):
                copy(a, 1 + j, (*chip, mc), me).wait_recv()
                passed.append(copy(a, 4 + j, (*chip, mc), sibling))
                passed[-1].start()
        for a in range(na):
            copy(a, 0, sibling, me).wait_recv()
        for j, chip in enumerate(chips):
            for a in range(na):
                copy(a, 4 + j, (*chip, 1 - mc), me).wait_recv()
        for cp in first + passed:
            cp.wait_send()
        for cp in mine:
            cp.wait()

    outs = pl.pallas_call(
        body, name=name, in_specs=[_HBM] * na, out_specs=[_HBM] * na,
        out_shape=[jax.ShapeDtypeStruct((N_DEV,) + x.shape, x.dtype) for x in xs],
        scratch_shapes=[pltpu.SemaphoreType.DMA((7 * na,)), pltpu.SemaphoreType.DMA((7 * na,)), pltpu.SemaphoreType.DMA((na,))],
    )(*xs)
    return outs[0] if single else list(outs)


_SEM = pl.BlockSpec(memory_space=pltpu.SEMAPHORE)
_HBM_ONLY = pl.BlockSpec(memory_space=pltpu.HBM)
_EFFECT = pltpu.SideEffectType.DATAFLOW_SIDE_EFFECTING
N_PEERS = N_DEV


def _peers():
    mx, my, mc = _my_place()
    return [((1 - mx) if (r >> 2) & 1 else mx, (1 - my) if (r >> 1) & 1 else my, (1 - mc) if r & 1 else mc)
            for r in range(N_DEV)]


def _exchange_copies(scatter, x_refs, land_refs, send_sems, recv_sems):
    me = my_index()
    copies = []
    for a, (x_ref, land_ref) in enumerate(zip(x_refs, land_refs)):
        for r, (px, py, pc) in enumerate(_peers()):
            src = x_ref.at[4 * px + 2 * py + pc] if scatter else x_ref
            dst = land_ref.at[r] if scatter else land_ref.at[me]
            copies.append(pltpu.make_async_remote_copy(
                src_ref=src, dst_ref=dst, send_sem=send_sems.at[N_PEERS * a + r], recv_sem=recv_sems.at[N_PEERS * a + r],
                device_id=(px, py, pc), device_id_type=MESH))
    return copies


def exchange_start(name, xs, scatter):
    na = len(xs)
    lands = [lax.empty((N_PEERS,) + x.shape[1:] if scatter else (N_DEV,) + x.shape, x.dtype) for x in xs]

    def body(*refs):
        x_refs, land_refs, send_sems, recv_sems = refs[:na], refs[na:2 * na], refs[2 * na], refs[2 * na + 1]
        token = refs[-1]
        for cp in _exchange_copies(scatter, x_refs, land_refs, send_sems, recv_sems):
            cp.start()
        token[...] = jnp.zeros_like(token)

    outs = pl.pallas_call(
        body, name=name,
        out_shape=(pltpu.SemaphoreType.DMA((N_PEERS * na,)), pltpu.SemaphoreType.DMA((N_PEERS * na,)),
                   *[pltpu.HBM(x.shape, x.dtype) for x in xs], *[pltpu.HBM(l.shape, l.dtype) for l in lands],
                   jax.ShapeDtypeStruct((8, LANES), F32)),
        in_specs=[_HBM_ONLY] * (2 * na),
        out_specs=(_SEM, _SEM, *[_HBM_ONLY] * (2 * na), pl.BlockSpec(memory_space=pltpu.VMEM)),
        input_output_aliases={i: 2 + i for i in range(2 * na)},
        compiler_params=pltpu.CompilerParams(has_side_effects=_EFFECT),
    )(*[pltpu.with_memory_space_constraint(x, pltpu.HBM) for x in xs],
      *[pltpu.with_memory_space_constraint(l, pltpu.HBM) for l in lands])
    return (scatter, outs[0], outs[1], outs[2:2 + na], outs[2 + na:2 + 2 * na]), outs[-1]


def exchange_finish(name, handle, after):
    scatter, send_sems, recv_sems, xs, lands = handle
    na = len(xs)

    def body(*refs):
        x_refs, land_refs, send_ref, recv_ref = refs[:na], refs[na:2 * na], refs[2 * na], refs[2 * na + 1]
        for cp in _exchange_copies(scatter, x_refs, land_refs, send_ref, recv_ref):
            cp.wait_send()
            cp.wait_recv()

    outs = pl.pallas_call(
        body, name=name,
        out_shape=tuple(pltpu.HBM(t.shape, t.dtype) for t in list(xs) + list(lands)),
        in_specs=[_HBM_ONLY] * (2 * na) + [_SEM, _SEM, _HBM],
        out_specs=tuple([_HBM_ONLY] * (2 * na)),
        input_output_aliases={i: i for i in range(2 * na)},
        compiler_params=pltpu.CompilerParams(has_side_effects=_EFFECT),
    )(*xs, *lands, send_sems, recv_sems, after)
    return list(outs[:na]), list(outs[na:])


def _pick_rows(n, target):
    best = None
    for t in range(8, min(n, target) + 1, 8):
        if n % t == 0:
            best = t
    return best if best is not None else n


def add_blocks(name, parts, out_dtype, rows=512):
    r, w = parts[0].shape
    tr = _pick_rows(r, rows)

    def body(*refs):
        acc = refs[0][...].astype(F32)
        for ref in refs[1:-1]:
            acc = acc + ref[...].astype(F32)
        refs[-1][...] = acc.astype(refs[-1].dtype)

    spec = pl.BlockSpec((tr, w), lambda i: (i, 0))
    return pl.pallas_call(
        body, name=name, grid=(r // tr,), in_specs=[spec] * len(parts), out_specs=spec,
        out_shape=jax.ShapeDtypeStruct((r, w), out_dtype),
        compiler_params=pltpu.CompilerParams(dimension_semantics=("parallel",)),
    )(*parts)


def sum_rows(name, x):
    def body(x_ref, o_ref):
        o_ref[...] = jnp.sum(x_ref[...], axis=0, keepdims=True)

    return pl.pallas_call(body, name=name, out_shape=jax.ShapeDtypeStruct((1, x.shape[1]), F32))(x)


def gather_start(name, blocks):
    return exchange_start(name, blocks, scatter=False)


def gather_finish(name, handle, after):
    return exchange_finish(name, handle, after)[1]


def scatter_start(name, g8s):
    return exchange_start(name, g8s, scatter=True)


def scatter_finish(name, handle, after):
    _, lands = exchange_finish(name, handle, after)
    outs = []
    for a, land in enumerate(lands):
        w = land.shape[-1]
        outs.append(add_blocks(f"{name}_sum{a}", [land[k].reshape(-1, w) for k in range(N_PEERS)], F32).reshape(land.shape[1:]))
    return outs


def _pack(arrays, width, row_mult, dtype, lead=0):
    parts, metas = [], []
    for a in arrays:
        lead_shape = a.shape[:lead]
        size = int(np.prod(a.shape[lead:]))
        chunk = row_mult * width
        padded = -(-size // chunk) * chunk
        flat = a.astype(dtype).reshape(lead_shape + (size,))
        if padded != size:
            flat = jnp.pad(flat, [(0, 0)] * lead + [(0, padded - size)])
        parts.append(flat.reshape(lead_shape + (padded // width, width)))
        metas.append((a.shape[lead:], size, padded // width))
    return jnp.concatenate(parts, axis=lead), metas


def _unpack(slab, metas, lead=0):
    out, r0 = [], 0
    for shape, size, rows in metas:
        part = lax.slice_in_dim(slab, r0, r0 + rows, axis=lead)
        lead_shape = part.shape[:lead]
        flat = part.reshape(lead_shape + (rows * part.shape[-1],))
        out.append(lax.slice_in_dim(flat, 0, size, axis=lead).reshape(lead_shape + tuple(shape)))
        r0 += rows
    return out


def _f_adamw(w, g, m, v):
    m = ADAM_B1 * m + (1.0 - ADAM_B1) * g
    v = ADAM_B2 * v + (1.0 - ADAM_B2) * (g * g)
    m_hat = m / (1.0 - ADAM_B1 ** ADAM_STEP)
    v_hat = v / (1.0 - ADAM_B2 ** ADAM_STEP)
    delta = (-ADAM_LR) * (m_hat / (jnp.sqrt(v_hat) + ADAM_EPS) + ADAM_WD * w)
    return delta, m, v


def adamw(name, w, g, m, v):
    shape = w.shape
    w2 = shape[-1]
    flat = [a.reshape(-1, w2) for a in (w, g, m, v)]
    tm = _pick_rows(flat[0].shape[0], 256)
    outs = ew_fwd(name, _f_adamw, flat, [], [], [w2] * 3, [F32] * 3, tm=tm)
    return [o.reshape(shape) for o in outs]


WEIGHTS = ["ffn1_norm", "ffn1_w1", "ffn1_w3", "ffn1_w2", "mix_norm", "w_in", "conv_w", "conv_b", "rg_wa", "rg_ba", "rg_wx",
           "rg_bx", "rg_lam", "fox_bf", "merge_b", "w_rg", "w_sb", "w_fox", "w_o", "ffn2_norm", "ffn2_w1", "ffn2_w3",
           "ffn2_w2", "ada_w", "ada_b", "final_norm", "final_ada_w", "final_ada_b"]
GATHERED = {"ffn1_w1": 2, "ffn1_w3": 2, "ffn1_w2": 1, "w_in": 2, "w_rg": 1, "w_sb": 2, "w_fox": 2, "w_o": 1,
            "ffn2_w1": 2, "ffn2_w3": 2, "ffn2_w2": 1}
REPLICATED = ["ffn1_norm", "mix_norm", "conv_b", "rg_wa", "rg_ba", "rg_wx", "rg_bx", "rg_lam", "fox_bf", "merge_b",
              "ffn2_norm", "final_norm"]
GROUPS = (("ffn1", ("ffn1_w1", "ffn1_w3", "ffn1_w2")), ("mix", ("w_in", "w_rg", "w_sb", "w_fox", "w_o")),
          ("ffn2", ("ffn2_w1", "ffn2_w3", "ffn2_w2")))
IN_CUTS = (0, 1024, 2048, 3584, 5120, 5128, 8200)


def _unshard(g, axis):
    g = jnp.moveaxis(g, 0, axis)
    shape = g.shape
    return g.reshape(shape[:axis] + (shape[axis] * shape[axis + 1],) + shape[axis + 2:])


def _reshard(full, axis):
    shape = full.shape
    g = full.reshape(shape[:axis] + (N_DEV, shape[axis] // N_DEV) + shape[axis + 1:])
    return jnp.moveaxis(g, axis, 0)


def _block_diag(w):
    nb, bd, _ = w.shape
    eye = jnp.eye(nb, dtype=bool)[:, None, :, None]
    return jnp.where(eye, w[:, :, None, :], 0.0).reshape(nb * bd, nb * bd)


def _diag_blocks(m, nb=RG_BLOCKS):
    bd = m.shape[0] // nb
    return jnp.stack([m[k * bd:(k + 1) * bd, k * bd:(k + 1) * bd] for k in range(nb)])


def _pad_lanes(a, width=LANES):
    return jnp.pad(a, [(0, 0)] * (a.ndim - 1) + [(0, width - a.shape[-1])])


def _bp(m, k, which):
    return m[:, k, which][:, None, :]


def _f_silu(c):
    return c * jax.nn.sigmoid(c)


def _f_add_bias(a, b):
    return a + b


FFN_TM = 512
FFN_TN = 1408
FFN_SUB = 256


def ffn_up(name, h, w1, w3):
    n, k = h.shape
    f = w1.shape[1]
    tm, tn = min(FFN_TM, n), _pick_tile(f, FFN_TN)

    def body(h_ref, w1_ref, w3_ref, a_ref, b_ref, s_ref):
        subs = [pl.ds(r, FFN_SUB) for r in range(0, tm, FFN_SUB)] if tm % FFN_SUB == 0 else [pl.ds(0, tm)]
        hs = [h_ref[rows, :] for rows in subs]
        a_s = [jnp.dot(hv, w1_ref[...], preferred_element_type=F32) for hv in hs]
        b_s = [jnp.dot(hv, w3_ref[...], preferred_element_type=F32) for hv in hs]
        for rows, a, b in zip(subs, a_s, b_s):
            a_ref[rows, :] = a.astype(BF16)
            b_ref[rows, :] = b.astype(BF16)
            s_ref[rows, :] = ((a * jax.nn.sigmoid(a)) * b).astype(BF16)

    wspec = pl.BlockSpec((k, tn), lambda i, j: (0, j))
    ospec = pl.BlockSpec((tm, tn), lambda i, j: (i, j))
    out = jax.ShapeDtypeStruct((n, f), BF16)
    return pl.pallas_call(
        body, name=name, grid=(n // tm, f // tn), in_specs=[pl.BlockSpec((tm, k), lambda i, j: (i, 0)), wspec, wspec],
        out_specs=[ospec] * 3, out_shape=[out] * 3,
        compiler_params=pltpu.CompilerParams(dimension_semantics=("parallel", "parallel")),
    )(h, w1, w3)


def ffn_down_dx(name, dy, w2, a, b):
    n, k = dy.shape
    f = w2.shape[0]
    tm, tn = min(FFN_TM, n), _pick_tile(f, FFN_TN)

    def body(dy_ref, w2_ref, a_ref, b_ref, da_ref, db_ref):
        subs = [pl.ds(r, FFN_SUB) for r in range(0, tm, FFN_SUB)] if tm % FFN_SUB == 0 else [pl.ds(0, tm)]
        ds_s = [_dot_nt(dy_ref[rows, :], w2_ref[...]) for rows in subs]
        for rows, ds in zip(subs, ds_s):
            av = a_ref[rows, :].astype(F32)
            sig = jax.nn.sigmoid(av)
            da_ref[rows, :] = (ds * b_ref[rows, :].astype(F32) * (sig * (1.0 + av * (1.0 - sig)))).astype(BF16)
            db_ref[rows, :] = (ds * (av * sig)).astype(BF16)

    ospec = pl.BlockSpec((tm, tn), lambda i, j: (i, j))
    out = jax.ShapeDtypeStruct((n, f), BF16)
    return pl.pallas_call(
        body, name=name, grid=(n // tm, f // tn),
        in_specs=[pl.BlockSpec((tm, k), lambda i, j: (i, 0)), pl.BlockSpec((tn, k), lambda i, j: (j, 0)), ospec, ospec],
        out_specs=[ospec] * 2, out_shape=[out] * 2,
        compiler_params=pltpu.CompilerParams(dimension_semantics=("parallel", "parallel")),
    )(dy, w2, a, b)


def _ffn_fwd(tag, x, shift, scale, gate, gain, w1, w3, w2):
    h = ew_fwd(f"ffn_norm_{tag}", f_norm_mod, [x], [shift, scale], [gain], [D], [BF16])[0]
    a, b3, s = ffn_up(f"ffn_up_{tag}", h, w1, w3)
    y = matmul(f"ffn_down_{tag}", s, w2, "nn", tm=1024)
    xo = ew_fwd(f"ffn_res_{tag}", functools.partial(f_resid, 0.5), [x, y], [gate], [], [D], [F32])[0]
    return xo, (x, h, a, b3, s, y)


def _ffn_bwd(tag, dxo, saved, shift, scale, gate, gain, w1, w3, w2):
    x, h, a, b3, s, y = saved
    (dy,), (dgate,), _ = ew_bwd(f"ffn_res_bwd_{tag}", functools.partial(f_resid, 0.5), [x, y], [gate], [], [dxo], [None, BF16])
    da, db3 = ffn_down_dx(f"ffn_down_dx_{tag}", dy, w2, a, b3)
    dw2 = matmul(f"ffn_dw2_{tag}", s, dy, "tn", tm=1408, tn=256, out_dtype=BF16)
    dw1 = matmul(f"ffn_dw1_{tag}", h, da, "tn", tm=1024, tn=256, out_dtype=BF16)
    dw3 = matmul(f"ffn_dw3_{tag}", h, db3, "tn", tm=1024, tn=256, out_dtype=BF16)
    dh = matmul(f"ffn_up_dx_{tag}", [da, db3], [w1, w3], "nt", tm=1024)
    (dx,), (dshift, dscale), (dgain,) = ew_bwd(f"ffn_norm_bwd_{tag}", f_norm_mod, [x], [shift, scale], [gain], [dh], [F32],
                                               adds=[dxo])
    return dx, (dshift, dscale, dgate), dgain, dw1, dw3, dw2


def _mixer_fwd(tag, x, shift, scale, gate, p):
    h = ew_fwd(f"mix_norm_{tag}", f_norm_mod, [x], [shift, scale], [p["gain"]], [D], [BF16])[0]
    rgx = matmul(f"in_rgx_{tag}", h, p["w_rgx"], "nn")
    rgate = matmul(f"in_gate_{tag}", h, p["w_gate"], "nn")
    sbqkv = matmul(f"in_sb_{tag}", h, p["w_sbqkv"], "nn")
    foxqkv = matmul(f"in_fox_{tag}", h, p["w_foxqkv"], "nn")
    ff = matmul(f"in_forget_{tag}", h, p["w_f"], "nn")
    mg = matmul(f"in_merge_{tag}", h, p["w_merge"], "nn")
    xa = conv_fwd(rgx, p["conv_w8"], p["conv_b"])
    pre_r = matmul(f"rg_a_{tag}", xa, p["wa_bd"], "nn")
    pre_i = matmul(f"rg_x_{tag}", xa, p["wx_bd"], "nn")
    a, u = ew_fwd(f"rg_gates_{tag}", f_rg_gates, [pre_r, pre_i, xa], [], [p["ba"], p["bx"], p["lam"]], [D, D], [F32, F32],
                  tm=EW_ROWS_WIDE)
    hs = scan_fwd(a, u)
    ya = ew_fwd(f"rg_out_{tag}", f_gelu_mul, [rgate, hs], [], [], [D], [BF16])[0]
    yb, sb_tot = sb_attn_fwd(sbqkv)
    lf = ew_fwd(f"fox_logf_{tag}", f_log_sigmoid_bias, [ff], [], [p["bf"]], [LANES], [F32])[0]
    cum = seq_cumsum(f"fox_cum_{tag}", [lf], [1.0], False)
    cum_t = cum.reshape(-1, SEQ, LANES)[:, :, :N_HEADS].transpose(0, 2, 1)
    yc, lse = fox_attn_fwd(foxqkv, cum, cum_t)
    pa = matmul(f"out_rg_{tag}", ya, p["w_rg"], "nn")
    pb = matmul(f"out_sb_{tag}", yb, p["w_sb"], "nn")
    pc = matmul(f"out_fox_{tag}", yc, p["w_fox"], "nn")
    mixed = ew_fwd(f"merge_{tag}", f_merge, [mg, pa, pb, pc], [], [p["merge_b"]], [D], [BF16], tm=EW_ROWS_WIDE)[0]
    y = matmul(f"out_o_{tag}", mixed, p["w_o"], "nn")
    xo = ew_fwd(f"mix_res_{tag}", functools.partial(f_resid, 1.0), [x, y], [gate], [], [D], [F32])[0]
    saved = dict(x=x, h=h, rgx=rgx, rgate=rgate, sbqkv=sbqkv, foxqkv=foxqkv, ff=ff, mg=mg, xa=xa, pre_r=pre_r, pre_i=pre_i,
                 a=a, hs=hs, ya=ya, yb=yb, sb_tot=sb_tot, cum=cum, cum_t=cum_t, yc=yc, lse=lse, pa=pa, pb=pb, pc=pc,
                 mixed=mixed, y=y)
    return xo, saved


def _mixer_bwd(tag, dxo, s, shift, scale, gate, p):
    (dy,), (dgate,), _ = ew_bwd(f"mix_res_bwd_{tag}", functools.partial(f_resid, 1.0), [s["x"], s["y"]], [gate], [], [dxo],
                                [None, BF16])
    dmixed = matmul(f"out_o_dx_{tag}", dy, p["w_o"], "nt")
    g = {"w_o": matmul(f"out_o_dw_{tag}", s["mixed"], dy, "tn", tm=1024, tn=256, out_dtype=BF16)}
    (dmg, dpa, dpb, dpc), _, (g["merge_b"],) = ew_bwd(
        f"merge_bwd_{tag}", f_merge, [s["mg"], s["pa"], s["pb"], s["pc"]], [], [p["merge_b"]], [dmixed], [BF16] * 4,
        tm=EW_ROWS_WIDE)
    dya = matmul(f"out_rg_dx_{tag}", dpa, p["w_rg"], "nt")
    g["w_rg"] = matmul(f"out_rg_dw_{tag}", s["ya"], dpa, "tn", tm=1024, tn=256, out_dtype=BF16)
    dyb = matmul(f"out_sb_dx_{tag}", dpb, p["w_sb"], "nt", out_dtype=BF16)
    g["w_sb"] = matmul(f"out_sb_dw_{tag}", s["yb"], dpb, "tn", tm=1024, tn=256, out_dtype=BF16)
    dyc = matmul(f"out_fox_dx_{tag}", dpc, p["w_fox"], "nt", out_dtype=BF16)
    g["w_fox"] = matmul(f"out_fox_dw_{tag}", s["yc"], dpc, "tn", tm=1024, tn=256, out_dtype=BF16)
    dq_c, dk_c, dv_c, dcq, dck = fox_attn_bwd(s["foxqkv"], s["cum"], s["cum_t"], s["lse"], s["yc"], dyc)
    dck_rows = _pad_lanes(dck.transpose(0, 2, 1).reshape(-1, N_HEADS))
    dlf = seq_cumsum(f"fox_cum_bwd_{tag}", [dcq, dck_rows], [1.0, -1.0], True)
    (dff,), _, (dbf,) = ew_bwd(f"fox_logf_bwd_{tag}", f_log_sigmoid_bias, [s["ff"]], [], [p["bf"]], [dlf], [BF16])
    g["fox_bf"] = dbf[0, :N_HEADS]
    dq_b, dk_b, dv_b = sb_attn_bwd(s["sbqkv"], s["sb_tot"], dyb)
    (drgate, dhs), _, _ = ew_bwd(f"rg_out_bwd_{tag}", f_gelu_mul, [s["rgate"], s["hs"]], [], [], [dya], [BF16, F32],
                                 tm=EW_ROWS_WIDE)
    da, du = scan_bwd(s["a"], s["hs"], dhs)
    (dpre_r, dpre_i, dxa1), _, (g["rg_ba"], g["rg_bx"], g["rg_lam"]) = ew_bwd(
        f"rg_gates_bwd_{tag}", f_rg_gates, [s["pre_r"], s["pre_i"], s["xa"]], [], [p["ba"], p["bx"], p["lam"]], [da, du],
        [BF16, BF16, F32], tm=EW_ROWS_WIDE)
    dxa2 = matmul(f"rg_dx_{tag}", [dpre_r, dpre_i], [p["wa_bd"], p["wx_bd"]], "nt")
    g["rg_wa"] = _diag_blocks(matmul(f"rg_a_dw_{tag}", s["xa"], dpre_r, "tn", tm=512, tn=256))
    g["rg_wx"] = _diag_blocks(matmul(f"rg_x_dw_{tag}", s["xa"], dpre_i, "tn", tm=512, tn=256))
    drgx, dwb = conv_bwd(s["rgx"], p["conv_w8"], dxa1, dxa2)
    g["conv_w"] = dwb[:CONV_K]
    g["conv_b"] = dwb[CONV_K]
    cots = [drgx, drgate, dq_b, dk_b, dv_b, dq_c, dk_c, dv_c, dff, dmg]
    w_sb3 = [p["w_sbqkv"][:, k * ATT_W:(k + 1) * ATT_W] for k in range(3)]
    w_fox3 = [p["w_foxqkv"][:, k * ATT_W:(k + 1) * ATT_W] for k in range(3)]
    ws = [p["w_rgx"], p["w_gate"]] + w_sb3 + w_fox3 + [p["w_f"], p["w_merge"]]
    dh = matmul(f"in_dx_{tag}", cots, ws, "nt", tm=512)
    dws = [matmul(f"in_dw{k}_{tag}", s["h"], ct, "tn", tm=1024, tn=256, out_dtype=BF16) for k, ct in enumerate(cots)]
    dws[8] = dws[8][:, :N_HEADS]
    g["w_in"] = jnp.concatenate(dws, axis=1)
    (dx,), (dshift, dscale), (g["mix_norm"],) = ew_bwd(f"mix_norm_bwd_{tag}", f_norm_mod, [s["x"]], [shift, scale], [p["gain"]],
                                                       [dh], [F32], adds=[dxo])
    return dx, (dshift, dscale, dgate), g


def _final_loss(x, target, shift, scale, gain):
    n = x.shape[0]
    tm = min(EW_ROWS, SEQ)
    tpb = SEQ // tm

    def body(x_ref, t_ref, sh_ref, sc_ref, g_ref, loss_ref, dx_ref, dsh_ref, dsc_ref, dg_ref):
        i = pl.program_id(0)
        out, vjp = jax.vjp(f_norm_mod, x_ref[...], sh_ref[...], sc_ref[...], g_ref[...])
        diff = out - t_ref[...]
        dx, dsh, dsc, dg = vjp(diff * (1.0 / D))
        dx_ref[...] = dx
        sq = jnp.sum(jnp.sum(diff * diff, axis=1, keepdims=True), axis=0, keepdims=True)

        @pl.when(i % tpb == 0)
        def _():
            dsh_ref[...] = jnp.zeros_like(dsh_ref)
            dsc_ref[...] = jnp.zeros_like(dsc_ref)

        @pl.when(i == 0)
        def _():
            dg_ref[...] = jnp.zeros_like(dg_ref)
            loss_ref[...] = jnp.zeros_like(loss_ref)

        dsh_ref[...] += dsh
        dsc_ref[...] += dsc
        dg_ref[...] += dg
        loss_ref[...] += jnp.broadcast_to(sq, (1, LANES)) * (0.5 / D)

    row, bp, gp = _row_spec(D, tm), _bparam_spec(D, tpb), _gparam_spec((1, D))
    return pl.pallas_call(
        body, name="final_loss", grid=(n // tm,), in_specs=[row, row, bp, bp, gp],
        out_specs=[_gparam_spec((1, LANES)), row, bp, bp, gp],
        out_shape=[jax.ShapeDtypeStruct((1, LANES), F32), jax.ShapeDtypeStruct((n, D), F32),
                   jax.ShapeDtypeStruct(shift.shape, F32), jax.ShapeDtypeStruct(scale.shape, F32),
                   jax.ShapeDtypeStruct((1, D), F32)],
        compiler_params=pltpu.CompilerParams(dimension_semantics=("arbitrary",)),
    )(x, target, shift, scale, gain)


def kernel(x, c, ffn1_norm, ffn1_w1, ffn1_w3, ffn1_w2, mix_norm, w_in, conv_w, conv_b, rg_wa, rg_ba, rg_wx, rg_bx, rg_lam, fox_bf, merge_b, w_rg, w_sb, w_fox, w_o, ffn2_norm, ffn2_w1, ffn2_w3, ffn2_w2, ada_w, ada_b, final_norm, final_ada_w, final_ada_b, loss_target, m_ffn1_norm, m_ffn1_w1, m_ffn1_w3, m_ffn1_w2, m_mix_norm, m_w_in, m_conv_w, m_conv_b, m_rg_wa, m_rg_ba, m_rg_wx, m_rg_bx, m_rg_lam, m_fox_bf, m_merge_b, m_w_rg, m_w_sb, m_w_fox, m_w_o, m_ffn2_norm, m_ffn2_w1, m_ffn2_w3, m_ffn2_w2, m_ada_w, m_ada_b, m_final_norm, m_final_ada_w, m_final_ada_b, v_ffn1_norm, v_ffn1_w1, v_ffn1_w3, v_ffn1_w2, v_mix_norm, v_w_in, v_conv_w, v_conv_b, v_rg_wa, v_rg_ba, v_rg_wx, v_rg_bx, v_rg_lam, v_fox_bf, v_merge_b, v_w_rg, v_w_sb, v_w_fox, v_w_o, v_ffn2_norm, v_ffn2_w1, v_ffn2_w3, v_ffn2_w2, v_ada_w, v_ada_b, v_final_norm, v_final_ada_w, v_final_ada_b):
    given = dict(zip(["x", "c"] + WEIGHTS + ["loss_target"] + ["m_" + n for n in WEIGHTS] + ["v_" + n for n in WEIGHTS],
                     (x, c, ffn1_norm, ffn1_w1, ffn1_w3, ffn1_w2, mix_norm, w_in, conv_w, conv_b, rg_wa, rg_ba, rg_wx, rg_bx, rg_lam, fox_bf, merge_b, w_rg, w_sb, w_fox, w_o, ffn2_norm, ffn2_w1, ffn2_w3, ffn2_w2, ada_w, ada_b, final_norm, final_ada_w, final_ada_b, loss_target, m_ffn1_norm, m_ffn1_w1, m_ffn1_w3, m_ffn1_w2, m_mix_norm, m_w_in, m_conv_w, m_conv_b, m_rg_wa, m_rg_ba, m_rg_wx, m_rg_bx, m_rg_lam, m_fox_bf, m_merge_b, m_w_rg, m_w_sb, m_w_fox, m_w_o, m_ffn2_norm, m_ffn2_w1, m_ffn2_w3, m_ffn2_w2, m_ada_w, m_ada_b, m_final_norm, m_final_ada_w, m_final_ada_b, v_ffn1_norm, v_ffn1_w1, v_ffn1_w3, v_ffn1_w2, v_mix_norm, v_w_in, v_conv_w, v_conv_b, v_rg_wa, v_rg_ba, v_rg_wx, v_rg_bx, v_rg_lam, v_fox_bf, v_merge_b, v_w_rg, v_w_sb, v_w_fox, v_w_o, v_ffn2_norm, v_ffn2_w1, v_ffn2_w3, v_ffn2_w2, v_ada_w, v_ada_b, v_final_norm, v_final_ada_w, v_final_ada_b)))
    idx = my_index()
    n_batch = N_DEV * B_LOC
    ada_cols = ada_w.shape[2]
    fin_cols = final_ada_w.shape[1]

    small_in, small_in_meta = _pack([c, conv_w], LANES, 8, F32)
    c_parts, conv_w_parts = _unpack(all_gather("gather_c_conv", small_in), small_in_meta, lead=1)
    c_all = c_parts.reshape(n_batch, D)
    conv_w_all = _unshard(conv_w_parts, 2)
    c_act = ew_fwd("c_silu", _f_silu, [c_all], [], [], [D], [F32])[0]
    mod_cols = [matmul(f"ada_proj_{l}", c_act, ada_w[l], "nn") for l in range(DEPTH)]
    mod_cols.append(matmul("ada_proj_final", c_act, final_ada_w, "nn"))
    mod_g = all_gather("gather_mod", jnp.concatenate(mod_cols, axis=1))

    shards = {(l, group): [given[n][l].astype(BF16) for n in members] for l in range(DEPTH) for group, members in GROUPS}
    waves = [[(0, "ffn1")], [(0, "mix")], [(0, "ffn2")] + [(l, group) for l in range(1, DEPTH) for group, _ in GROUPS]]
    gather_handles, landed = {}, {}

    def start_wave(wave, behind, carrier):
        blocks, behind = lax.optimization_barrier(({key: shards[key] for key in wave}, behind))
        for key in wave:
            gather_handles[key], token = gather_start(f"gather_start_{key[1]}{key[0]}", blocks[key])
            carrier = carrier + token[0, 0]
        return behind, carrier

    def weights_of(l, group, after):
        key = (l, group)
        if key not in landed:
            landed[key] = gather_finish(f"gather_finish_{group}{l}", gather_handles[key], after)
        return {n: _unshard(b, GATHERED[n] - 1) for n, b in zip(dict(GROUPS)[group], landed[key])}

    first_blocks, mod_g = lax.optimization_barrier((shards[0, "ffn1"], mod_g))
    landed[0, "ffn1"] = all_gather("gather_first", first_blocks)
    landed[0, "ffn1"], mod_g = start_wave(waves[1], landed[0, "ffn1"], mod_g)

    mods = []
    for l in range(DEPTH):
        full = mod_g[:, :, l * ada_cols:(l + 1) * ada_cols].transpose(1, 0, 2).reshape(n_batch, N_DEV * ada_cols)
        full = ew_fwd(f"ada_bias_{l}", _f_add_bias, [full], [], [ada_b[l][None]], [full.shape[1]], [F32])[0]
        mods.append(lax.dynamic_slice_in_dim(full, idx * B_LOC, B_LOC, axis=0).reshape(B_LOC, 3, 3, D))
    fm = mod_g[:, :, DEPTH * ada_cols:].transpose(1, 0, 2).reshape(n_batch, N_DEV * fin_cols)
    fm = ew_fwd("ada_bias_final", _f_add_bias, [fm], [], [final_ada_b[None]], [fm.shape[1]], [F32])[0]
    fm = lax.dynamic_slice_in_dim(fm, idx * B_LOC, B_LOC, axis=0).reshape(B_LOC, 2, D)

    def mixer_params(l, w):
        wi = w["w_in"]
        cut = IN_CUTS
        return dict(
            gain=mix_norm[l][None], w_rgx=wi[:, cut[0]:cut[1]], w_gate=wi[:, cut[1]:cut[2]], w_sbqkv=wi[:, cut[2]:cut[3]],
            w_foxqkv=wi[:, cut[3]:cut[4]], w_f=_pad_lanes(wi[:, cut[4]:cut[5]]), w_merge=wi[:, cut[5]:cut[6]],
            conv_w8=jnp.pad(conv_w_all[l], ((0, 8 - CONV_K), (0, 0))), conv_b=conv_b[l][None],
            wa_bd=_block_diag(rg_wa[l]), wx_bd=_block_diag(rg_wx[l]), ba=rg_ba[l][None], bx=rg_bx[l][None], lam=rg_lam[l][None],
            bf=_pad_lanes(fox_bf[l][None]), merge_b=merge_b[l][None], w_rg=w["w_rg"], w_sb=w["w_sb"], w_fox=w["w_fox"],
            w_o=w["w_o"])

    n_tok = x.shape[0] * x.shape[1]
    h = x.reshape(n_tok, D)
    saved = []
    for l in range(DEPTH):
        m = mods[l]
        w1 = weights_of(l, "ffn1", m if l == 0 else h)
        h, s1 = _ffn_fwd(f"a{l}", h, _bp(m, 0, 0), _bp(m, 0, 1), _bp(m, 0, 2), ffn1_norm[l][None], w1["ffn1_w1"], w1["ffn1_w3"],
                         w1["ffn1_w2"])
        w2 = weights_of(l, "mix", h)
        if l == 0:
            landed[0, "mix"], m = start_wave(waves[2], landed[0, "mix"], m)
        p = mixer_params(l, w2)
        h, s2 = _mixer_fwd(f"{l}", h, _bp(m, 1, 0), _bp(m, 1, 1), _bp(m, 1, 2), p)
        w3 = weights_of(l, "ffn2", h)
        h, s3 = _ffn_fwd(f"b{l}", h, _bp(m, 2, 0), _bp(m, 2, 1), _bp(m, 2, 2), ffn2_norm[l][None], w3["ffn2_w1"], w3["ffn2_w3"],
                         w3["ffn2_w2"])
        saved.append((s1, s2, s3, p, w1, w3))
    loss_row, dh, dfshift, dfscale, dgain_final = _final_loss(h, loss_target.reshape(n_tok, D), fm[:, 0][:, None, :],
                                                              fm[:, 1][:, None, :], final_norm[None])

    grads = {n: [None] * DEPTH for n in WEIGHTS}
    d_mods = [None] * DEPTH
    scatter_handles = {}
    after_start = jnp.zeros((), F32)

    def scatter_blocks(l, group):
        return [_reshard(grads[n][l], GATHERED[n] - 1).astype(BF16) for n in dict(GROUPS)[group]]

    def start_scatter(l, group, g8s=None):
        g8s = scatter_blocks(l, group) if g8s is None else g8s
        scatter_handles[l, group], token = scatter_start(f"scatter_start_{group}{l}", g8s)
        return token[0, 0]

    for l in reversed(range(DEPTH)):
        m = mods[l]
        s1, s2, s3, p, w1, w3 = saved[l]
        dh, dm3, grads["ffn2_norm"][l], grads["ffn2_w1"][l], grads["ffn2_w3"][l], grads["ffn2_w2"][l] = _ffn_bwd(
            f"b{l}", dh, s3, _bp(m, 2, 0), _bp(m, 2, 1), _bp(m, 2, 2) + after_start, ffn2_norm[l][None], w3["ffn2_w1"],
            w3["ffn2_w3"], w3["ffn2_w2"])
        after_start = start_scatter(l, "ffn2")
        dh, dm2, gm = _mixer_bwd(f"{l}", dh, s2, _bp(m, 1, 0), _bp(m, 1, 1), _bp(m, 1, 2) + after_start, p)
        for n, gval in gm.items():
            grads[n][l] = gval
        after_start = start_scatter(l, "mix")
        dh, dm1, grads["ffn1_norm"][l], grads["ffn1_w1"][l], grads["ffn1_w3"][l], grads["ffn1_w2"][l] = _ffn_bwd(
            f"a{l}", dh, s1, _bp(m, 0, 0), _bp(m, 0, 1), _bp(m, 0, 2) + after_start, ffn1_norm[l][None], w1["ffn1_w1"],
            w1["ffn1_w3"], w1["ffn1_w2"])
        if l > 0:
            after_start = start_scatter(l, "ffn1")
        d_mods[l] = jnp.concatenate([t.reshape(B_LOC, D) for dm in (dm1, dm2, dm3) for t in dm], axis=1)
    grad_x = dh.reshape(x.shape)
    d_fm = jnp.concatenate([dfshift.reshape(B_LOC, D), dfscale.reshape(B_LOC, D)], axis=1)

    rep = {n: jnp.stack([t.reshape(given[n].shape[1:]) for t in grads[n]]) for n in REPLICATED if n != "final_norm"}
    rep["final_norm"] = dgain_final.reshape(D)
    rep["conv_w"] = jnp.stack(grads["conv_w"])
    rep_names = list(rep)
    rep_slab, rep_meta = _pack([rep[n] for n in rep_names], LANES, 8, F32)
    mod_slab, mod_meta = _pack(d_mods + [d_fm], LANES, 8, F32)
    small_g = all_gather("gather_small_grads", jnp.concatenate([mod_slab, rep_slab], axis=0))
    last_blocks, small_g = lax.optimization_barrier((scatter_blocks(0, "ffn1"), small_g))
    small_g = small_g + start_scatter(0, "ffn1", last_blocks)
    d_mod_all = [t.reshape(n_batch, -1) for t in _unpack(small_g[:, :mod_slab.shape[0]], mod_meta, lead=1)]
    rep_sum = add_blocks("sum_small_grads", [small_g[k, mod_slab.shape[0]:] for k in range(N_DEV)], F32)
    rep_grad = dict(zip(rep_names, _unpack(rep_sum, rep_meta)))
    final_g = {n: rep_grad[n] for n in REPLICATED}
    final_g["conv_w"] = lax.dynamic_slice_in_dim(rep_grad["conv_w"], idx * conv_w.shape[2], conv_w.shape[2], axis=2)
    final_g["ada_b"] = jnp.stack([sum_rows(f"ada_b_grad_{l}", d_mod_all[l])[0] for l in range(DEPTH)])
    final_g["final_ada_b"] = sum_rows("final_ada_b_grad", d_mod_all[DEPTH])[0]
    final_g["ada_w"] = jnp.stack([
        matmul(f"ada_w_grad_{l}", c_act, lax.dynamic_slice_in_dim(d_mod_all[l], idx * ada_cols, ada_cols, axis=1), "tn")
        for l in range(DEPTH)])
    final_g["final_ada_w"] = matmul(
        "final_ada_w_grad", c_act, lax.dynamic_slice_in_dim(d_mod_all[DEPTH], idx * fin_cols, fin_cols, axis=1), "tn")

    shard_g = {n: [None] * DEPTH for n in GATHERED}

    def finish_scatter(l, group, after):
        sums = scatter_finish(f"scatter_finish_{group}{l}", scatter_handles[l, group], after)
        for n, gval in zip(dict(GROUPS)[group], sums):
            shard_g[n][l] = gval

    for l in reversed(range(DEPTH)):
        for group in ("ffn2", "mix", "ffn1"):
            if (l, group) != (0, "ffn1"):
                finish_scatter(l, group, rep_sum)

    delta, new_m, new_v = {}, {}, {}
    last = dict(GROUPS)["ffn1"]
    sharded = [n for n in GATHERED if n not in last] + ["ada_w", "final_ada_w", "conv_w"] + list(last)
    for n in sharded:
        if n == last[0]:
            finish_scatter(0, "ffn1", delta["w_in"])
        if n in GATHERED:
            final_g[n] = jnp.stack(shard_g[n])
        delta[n], new_m[n], new_v[n] = adamw(f"adamw_{n}", given[n], final_g[n], given["m_" + n], given["v_" + n])
    for n in WEIGHTS:
        if n not in sharded:
            delta[n], new_m[n], new_v[n] = adamw(f"adamw_{n}", given[n], final_g[n], given["m_" + n], given["v_" + n])

    loss = lax.psum(loss_row[0, 0], ("x", "y", "c"))
    return (loss, grad_x, *[final_g[n] for n in WEIGHTS], *[delta[n] for n in WEIGHTS], *[new_m[n] for n in WEIGHTS],
            *[new_v[n] for n in WEIGHTS])
```

```python
import functools

import numpy as np
import jax
import jax.numpy as jnp
from jax import lax
from jax.experimental import pallas as pl
from jax.experimental.pallas import tpu as pltpu

F32 = jnp.float32
BF16 = jnp.bfloat16
MESH = pl.DeviceIdType.MESH

N_DEV = 8
D = 1024
SEQ = 2048
B_LOC = 2
N_TOK = B_LOC * SEQ
DEPTH = 2
D_FF = 2816
RG_BLOCKS = 16
RG_C = 8.0
N_HEADS = 8
HEAD_DIM = 64
ATT_W = N_HEADS * HEAD_DIM
LANES = 128
EPS = 1e-6
ATT_SCALE = HEAD_DIM ** -0.5
CONV_K = 4

ADAM_LR = 0.001
ADAM_B1 = 0.9
ADAM_B2 = 0.999
ADAM_EPS = 1e-08
ADAM_WD = 0.01
ADAM_STEP = 10

EW_ROWS = 512
EW_ROWS_WIDE = 256
ATT_BLK = 512


def _pick_tile(dim, target):
    best = None
    for t in range(LANES, min(dim, target) + 1, LANES):
        if dim % t == 0:
            best = t
    return best if best is not None else dim


_DIMS = {"nn": (((1,), (0,)), ((), ())), "nt": (((1,), (1,)), ((), ())), "tn": (((0,), (0,)), ((), ()))}


def matmul(name, a_list, b_list, mode, out_dtype=F32, tm=1024, tn=512):
    if not isinstance(a_list, (list, tuple)):
        a_list, b_list = [a_list], [b_list]
    n = len(a_list)
    m_dim = a_list[0].shape[1] if mode == "tn" else a_list[0].shape[0]
    n_dim = b_list[0].shape[0] if mode == "nt" else b_list[0].shape[1]
    tm, tn = _pick_tile(m_dim, tm), _pick_tile(n_dim, tn)
    dims = _DIMS[mode]

    def body(*refs):
        o_ref = refs[-1]
        acc = None
        for a_ref, b_ref in zip(refs[:n], refs[n:2 * n]):
            d = lax.dot_general(a_ref[...].astype(BF16), b_ref[...].astype(BF16), dims, preferred_element_type=F32)
            acc = d if acc is None else acc + d
        o_ref[...] = acc.astype(o_ref.dtype)

    in_specs = []
    for a in a_list:
        if mode == "tn":
            in_specs.append(pl.BlockSpec((a.shape[0], tm), lambda i, j: (0, i)))
        else:
            in_specs.append(pl.BlockSpec((tm, a.shape[1]), lambda i, j: (i, 0)))
    for b in b_list:
        if mode == "nt":
            in_specs.append(pl.BlockSpec((tn, b.shape[1]), lambda i, j: (j, 0)))
        else:
            in_specs.append(pl.BlockSpec((b.shape[0], tn), lambda i, j: (0, j)))
    return pl.pallas_call(
        body, name=name, grid=(m_dim // tm, n_dim // tn), in_specs=in_specs,
        out_specs=pl.BlockSpec((tm, tn), lambda i, j: (i, j)),
        out_shape=jax.ShapeDtypeStruct((m_dim, n_dim), out_dtype),
        compiler_params=pltpu.CompilerParams(dimension_semantics=("parallel", "parallel")),
    )(*a_list, *b_list)


def _row_spec(w, tm):
    return pl.BlockSpec((tm, w), lambda i: (i, 0))


def _bparam_spec(w, tiles_per_batch):
    return pl.BlockSpec((None, 1, w), lambda i: (i // tiles_per_batch, 0, 0))


def _gparam_spec(shape):
    return pl.BlockSpec(shape, lambda i: (0, 0))


def ew_fwd(name, fn, rows, bparams, gparams, out_widths, out_dtypes, tm=EW_ROWS):
    n_rows = rows[0].shape[0]
    tm = min(tm, n_rows, SEQ)
    tpb = max(SEQ // tm, 1)
    nr, nb, ng = len(rows), len(bparams), len(gparams)

    def body(*refs):
        vals = [r[...] for r in refs[:nr + nb + ng]]
        outs = fn(*vals)
        if not isinstance(outs, (tuple, list)):
            outs = (outs,)
        for o_ref, o in zip(refs[nr + nb + ng:], outs):
            o_ref[...] = o.astype(o_ref.dtype)

    in_specs = ([_row_spec(r.shape[1], tm) for r in rows] + [_bparam_spec(p.shape[2], tpb) for p in bparams]
                + [_gparam_spec(g.shape) for g in gparams])
    outs = pl.pallas_call(
        body, name=name, grid=(n_rows // tm,), in_specs=in_specs,
        out_specs=[_row_spec(w, tm) for w in out_widths],
        out_shape=[jax.ShapeDtypeStruct((n_rows, w), dt) for w, dt in zip(out_widths, out_dtypes)],
        compiler_params=pltpu.CompilerParams(dimension_semantics=("parallel",)),
    )(*rows, *bparams, *gparams)
    return outs


def ew_bwd(name, fn, rows, bparams, gparams, cts, row_grad_dtypes, adds=(), tm=EW_ROWS):
    n_rows = rows[0].shape[0]
    tm = min(tm, n_rows, SEQ)
    tpb = max(SEQ // tm, 1)
    nr, nb, ng, nc = len(rows), len(bparams), len(gparams), len(cts)
    adds = list(adds) + [None] * (nr - len(adds))
    add_idx = [k for k in range(nr) if adds[k] is not None]
    want = [k for k in range(nr) if row_grad_dtypes[k] is not None]

    def body(*refs):
        pos = nr + nb + ng
        vals = [r[...] for r in refs[:pos]]
        ct_vals = [r[...].astype(F32) for r in refs[pos:pos + nc]]
        pos += nc
        add_vals = {k: refs[pos + q][...] for q, k in enumerate(add_idx)}
        pos += len(add_idx)
        out_refs = refs[pos:]
        f32_vals = [v.astype(F32) for v in vals]
        outs, vjp = jax.vjp(lambda *a: fn(*a), *f32_vals)
        single = not isinstance(outs, (tuple, list))
        grads = vjp(ct_vals[0].astype(outs.dtype) if single else tuple(c.astype(o.dtype) for c, o in zip(ct_vals, outs)))
        i = pl.program_id(0)
        q = 0
        for k in want:
            g = grads[k]
            if k in add_vals:
                g = g + add_vals[k].astype(F32)
            out_refs[q][...] = g.astype(out_refs[q].dtype)
            q += 1
        for k in range(nb):
            ref = out_refs[q]
            q += 1

            @pl.when(i % tpb == 0)
            def _():
                ref[...] = jnp.zeros_like(ref)

            ref[...] += grads[nr + k]
        for k in range(ng):
            ref = out_refs[q]
            q += 1

            @pl.when(i == 0)
            def _():
                ref[...] = jnp.zeros_like(ref)

            ref[...] += grads[nr + nb + k]

    in_specs = ([_row_spec(r.shape[1], tm) for r in rows] + [_bparam_spec(p.shape[2], tpb) for p in bparams]
                + [_gparam_spec(g.shape) for g in gparams] + [_row_spec(c.shape[1], tm) for c in cts]
                + [_row_spec(adds[k].shape[1], tm) for k in add_idx])
    out_specs = ([_row_spec(rows[k].shape[1], tm) for k in want] + [_bparam_spec(p.shape[2], tpb) for p in bparams]
                 + [_gparam_spec(g.shape) for g in gparams])
    out_shape = ([jax.ShapeDtypeStruct(rows[k].shape, row_grad_dtypes[k]) for k in want]
                 + [jax.ShapeDtypeStruct(p.shape, F32) for p in bparams] + [jax.ShapeDtypeStruct(g.shape, F32) for g in gparams])
    outs = pl.pallas_call(
        body, name=name, grid=(n_rows // tm,), in_specs=in_specs, out_specs=out_specs, out_shape=out_shape,
        compiler_params=pltpu.CompilerParams(dimension_semantics=("arbitrary",)),
    )(*rows, *bparams, *gparams, *cts, *[adds[k] for k in add_idx])
    d_rows = list(outs[:len(want)])
    d_b = list(outs[len(want):len(want) + nb])
    d_g = list(outs[len(want) + nb:])
    return d_rows, d_b, d_g


def f_norm_mod(x, shift, scale, gain):
    x = x.astype(F32)
    y = x * lax.rsqrt(jnp.mean(x * x, axis=-1, keepdims=True) + EPS)
    return (y * gain) * (1.0 + scale) + shift


def f_swiglu(a, b3):
    a = a.astype(F32)
    return (a * jax.nn.sigmoid(a)) * b3.astype(F32)


def f_resid(coef, x, y, gate):
    return x.astype(F32) + (coef * (1.0 + gate)) * y.astype(F32)


def f_rg_gates(pre_r, pre_i, xa, ba, bx, lam):
    r = jax.nn.sigmoid(pre_r + ba)
    i = jax.nn.sigmoid(pre_i + bx)
    softplus_neg_lam = jnp.maximum(-lam, 0.0) + jnp.log(1.0 + jnp.exp(-jnp.abs(lam)))
    log_a = (-RG_C) * r * softplus_neg_lam
    a = jnp.exp(log_a)
    u = jnp.sqrt(1.0 - a * a) * (i * xa)
    return a, u


def f_gelu_mul(gate, hs):
    g = gate.astype(F32)
    gelu = 0.5 * g * (1.0 + jnp.tanh(0.7978845608028654 * (g + 0.044715 * g * g * g)))
    return gelu * hs.astype(F32)


def f_log_sigmoid_bias(f, bf):
    z = f.astype(F32) + bf
    return jnp.minimum(z, 0.0) - jnp.log(1.0 + jnp.exp(-jnp.abs(z)))


def f_merge(mg, pa, pb, pc, merge_b):
    g = jax.nn.sigmoid(mg.astype(F32) + merge_b)
    return g[:, :D] * pa.astype(F32) + g[:, D:2 * D] * pb.astype(F32) + g[:, 2 * D:] * pc.astype(F32)


CONV_CB = 256
SCAN_CB = 512
SCAN_CHAINS = 4
CUM_RB = 512


def _shift_down(x, d):
    if d == 0:
        return x
    rows = lax.broadcasted_iota(jnp.int32, x.shape, 0)
    return jnp.where(rows >= d, pltpu.roll(x, d, axis=0), 0.0)


def _shift_up(x, d):
    if d == 0:
        return x
    s = x.shape[0]
    rows = lax.broadcasted_iota(jnp.int32, x.shape, 0)
    return jnp.where(rows < s - d, pltpu.roll(x, s - d, axis=0), 0.0)


def conv_fwd(x, w8, b):
    n, c = x.shape
    nb = n // SEQ

    def body(x_ref, w_ref, b_ref, y_ref):
        xv = x_ref[...]
        acc = jnp.broadcast_to(b_ref[...], xv.shape)
        for k in range(CONV_K):
            acc = acc + w_ref[k:k + 1, :] * _shift_down(xv, CONV_K - 1 - k)
        y_ref[...] = acc

    return pl.pallas_call(
        body, name="conv_fwd", grid=(c // CONV_CB, nb),
        in_specs=[pl.BlockSpec((SEQ, CONV_CB), lambda j, i: (i, j)), pl.BlockSpec((8, CONV_CB), lambda j, i: (0, j)),
                  pl.BlockSpec((1, CONV_CB), lambda j, i: (0, j))],
        out_specs=pl.BlockSpec((SEQ, CONV_CB), lambda j, i: (i, j)),
        out_shape=jax.ShapeDtypeStruct((n, c), F32),
        compiler_params=pltpu.CompilerParams(dimension_semantics=("parallel", "parallel")),
    )(x, w8, b)


def conv_bwd(x, w8, dy1, dy2):
    n, c = x.shape
    nb = n // SEQ

    def body(x_ref, w_ref, dy1_ref, dy2_ref, dx_ref, dwb_ref):
        xv = x_ref[...]
        dy = dy1_ref[...] + dy2_ref[...]
        dx = jnp.zeros_like(xv)
        parts = []
        for k in range(CONV_K):
            d = CONV_K - 1 - k
            dx = dx + w_ref[k:k + 1, :] * _shift_up(dy, d)
            parts.append(jnp.sum(dy * _shift_down(xv, d), axis=0, keepdims=True))
        parts.append(jnp.sum(dy, axis=0, keepdims=True))
        parts.append(jnp.zeros((8 - len(parts), xv.shape[1]), F32))
        dx_ref[...] = dx.astype(BF16)

        @pl.when(pl.program_id(1) == 0)
        def _():
            dwb_ref[...] = jnp.zeros_like(dwb_ref)

        dwb_ref[...] += jnp.concatenate(parts, axis=0)

    return pl.pallas_call(
        body, name="conv_bwd", grid=(c // CONV_CB, nb),
        in_specs=[pl.BlockSpec((SEQ, CONV_CB), lambda j, i: (i, j)), pl.BlockSpec((8, CONV_CB), lambda j, i: (0, j)),
                  pl.BlockSpec((SEQ, CONV_CB), lambda j, i: (i, j)), pl.BlockSpec((SEQ, CONV_CB), lambda j, i: (i, j))],
        out_specs=[pl.BlockSpec((SEQ, CONV_CB), lambda j, i: (i, j)), pl.BlockSpec((8, CONV_CB), lambda j, i: (0, j))],
        out_shape=[jax.ShapeDtypeStruct((n, c), BF16), jax.ShapeDtypeStruct((8, c), F32)],
        compiler_params=pltpu.CompilerParams(dimension_semantics=("parallel", "arbitrary")),
    )(x, w8, dy1, dy2)


def scan_fwd(a, u):
    n, c = a.shape
    q = SEQ // SCAN_CHAINS

    def body(a_ref, u_ref, h_ref, p_ref):
        def step(t, carry):
            hs, ps = carry
            new_h, new_p = [], []
            for k in range(SCAN_CHAINS):
                row = k * q + t
                av = a_ref[pl.ds(row, 1), :]
                hk = av * hs[k] + u_ref[pl.ds(row, 1), :]
                h_ref[pl.ds(row, 1), :] = hk
                new_h.append(hk)
                pk = av * ps[k]
                if k > 0:
                    p_ref[pl.ds(row, 1), :] = pk
                new_p.append(pk)
            return tuple(new_h), tuple(new_p)

        zero, one = jnp.zeros((1, SCAN_CB), F32), jnp.ones((1, SCAN_CB), F32)
        lax.fori_loop(0, q, step, ((zero,) * SCAN_CHAINS, (one,) * SCAN_CHAINS), unroll=4)
        for k in range(1, SCAN_CHAINS):
            rows = pl.ds(k * q, q)
            h_ref[rows, :] = h_ref[rows, :] + p_ref[rows, :] * h_ref[pl.ds(k * q - 1, 1), :]

    spec = pl.BlockSpec((SEQ, SCAN_CB), lambda i, j: (i, j))
    return pl.pallas_call(
        body, name="scan_fwd", grid=(n // SEQ, c // SCAN_CB), in_specs=[spec, spec], out_specs=spec,
        out_shape=jax.ShapeDtypeStruct((n, c), F32), scratch_shapes=[pltpu.VMEM((SEQ, SCAN_CB), F32)],
        compiler_params=pltpu.CompilerParams(dimension_semantics=("parallel", "parallel")),
    )(a, u)


def scan_bwd(a, h, g):
    n, c = a.shape
    q = SEQ // SCAN_CHAINS
    cb = SCAN_CB // 2

    def body(a_ref, h_ref, g_ref, da_ref, du_ref, r_ref):
        def step(j, carry):
            cs, rs = carry
            new_c, new_r = [], []
            for k in range(SCAN_CHAINS):
                row = k * q + (q - 1 - j)
                dh = g_ref[pl.ds(row, 1), :] + cs[k]
                du_ref[pl.ds(row, 1), :] = dh
                av = a_ref[pl.ds(row, 1), :]
                if k < SCAN_CHAINS - 1:
                    r_ref[pl.ds(row, 1), :] = rs[k]
                new_c.append(av * dh)
                new_r.append(av * rs[k])
            return tuple(new_c), tuple(new_r)

        zero, one = jnp.zeros((1, cb), F32), jnp.ones((1, cb), F32)
        lax.fori_loop(0, q, step, ((zero,) * SCAN_CHAINS, (one,) * SCAN_CHAINS), unroll=4)
        for k in reversed(range(SCAN_CHAINS - 1)):
            rows, nxt = pl.ds(k * q, q), pl.ds((k + 1) * q, 1)
            du_ref[rows, :] = du_ref[rows, :] + r_ref[rows, :] * (a_ref[nxt, :] * du_ref[nxt, :])
        da_ref[...] = du_ref[...] * _shift_down(h_ref[...], 1)

    spec = pl.BlockSpec((SEQ, cb), lambda i, j: (i, j))
    return pl.pallas_call(
        body, name="scan_bwd", grid=(n // SEQ, c // cb), in_specs=[spec, spec, spec], out_specs=[spec, spec],
        out_shape=[jax.ShapeDtypeStruct((n, c), F32), jax.ShapeDtypeStruct((n, c), F32)],
        scratch_shapes=[pltpu.VMEM((SEQ, cb), F32)],
        compiler_params=pltpu.CompilerParams(dimension_semantics=("parallel", "parallel")),
    )(a, h, g)


def _split3_dot(m, x):
    hi = x.astype(BF16)
    r1 = x - hi.astype(F32)
    mid = r1.astype(BF16)
    lo = (r1 - mid.astype(F32)).astype(BF16)
    dot = functools.partial(jnp.dot, preferred_element_type=F32)
    return dot(m, hi) + dot(m, mid) + dot(m, lo)


def seq_cumsum(name, xs, signs, reverse):
    n, w = xs[0].shape
    nx = len(xs)
    rb = min(CUM_RB, SEQ)

    def body(*refs):
        x = None
        for r, sg in zip(refs[:nx], signs):
            x = sg * r[...] if x is None else x + sg * r[...]
        q0 = pl.program_id(1) * rb
        row = q0 + lax.broadcasted_iota(jnp.int32, (rb, SEQ), 0)
        col = lax.broadcasted_iota(jnp.int32, (rb, SEQ), 1)
        tri = ((col >= row) if reverse else (col <= row)).astype(BF16)
        refs[nx][...] = _split3_dot(tri, x)

    return pl.pallas_call(
        body, name=name, grid=(n // SEQ, SEQ // rb),
        in_specs=[pl.BlockSpec((SEQ, w), lambda i, j: (i, 0)) for _ in xs],
        out_specs=pl.BlockSpec((rb, w), lambda i, j: (i * (SEQ // rb) + j, 0)),
        out_shape=jax.ShapeDtypeStruct((n, w), F32),
        compiler_params=pltpu.CompilerParams(dimension_semantics=("parallel", "parallel")),
    )(*xs)


N_PAIRS = N_HEADS // 2


def _dot_nt(a, b):
    return lax.dot_general(a, b, _DIMS["nt"], preferred_element_type=F32)


def _dot_tn(a, b):
    return lax.dot_general(a, b, _DIMS["tn"], preferred_element_type=F32)


def _dot_nn(a, b):
    return lax.dot_general(a, b, _DIMS["nn"], preferred_element_type=F32)


def _split2_dot(x, m):
    hi = x.astype(BF16)
    lo = (x - hi.astype(F32)).astype(BF16)
    return _dot_nn(hi, m) + _dot_nn(lo, m)


def _head_mask(j):
    lane = lax.broadcasted_iota(jnp.int32, (1, LANES), 1)
    return (lane // HEAD_DIM) == j


def _lane_pick(x, h):
    lane = lax.broadcasted_iota(jnp.int32, x.shape, 1)
    return jnp.sum(jnp.where(lane == h, x, 0.0), axis=1, keepdims=True)


def _lane_put(col, h):
    lane = lax.broadcasted_iota(jnp.int32, (col.shape[0], LANES), 1)
    return jnp.where(lane == h, col, 0.0)


def _softplus(z):
    return jnp.maximum(z, 0.0) + jnp.log(1.0 + jnp.exp(-jnp.abs(z)))


def _qkv_specs():
    return [pl.BlockSpec((SEQ, LANES), lambda b, p: (b, p)),
            pl.BlockSpec((SEQ, LANES), lambda b, p: (b, N_PAIRS + p)),
            pl.BlockSpec((SEQ, LANES), lambda b, p: (b, 2 * N_PAIRS + p))]


def _pair_spec():
    return pl.BlockSpec((SEQ, LANES), lambda b, p: (b, p))


def _below_diagonal(strictly):
    t = ATT_BLK
    row = lax.broadcasted_iota(jnp.int32, (t, t), 0)
    col = lax.broadcasted_iota(jnp.int32, (t, t), 1)
    return (row > col) if strictly else (row >= col)


def _over_key_blocks(qi, step, init, reverse):
    t = ATT_BLK
    q0 = pl.multiple_of(qi * t, t)

    def off_diagonal(kk, carry):
        ki = (qi - 1 - kk) if reverse else kk
        return step(pl.multiple_of(ki * t, t), carry, False)

    if reverse:
        return lax.fori_loop(0, qi, off_diagonal, step(q0, init, True))
    return step(q0, lax.fori_loop(0, qi, off_diagonal, init), True)


def _masked_q(qb, j):
    return (jnp.where(_head_mask(j), qb, 0.0) * ATT_SCALE).astype(BF16)


def sb_attn_fwd(qkv):
    n = qkv.shape[0]
    t = ATT_BLK

    def body(q_ref, k_ref, v_ref, o_ref, tot_ref):
        pair = pl.program_id(1)
        strict = _below_diagonal(True)
        later = strict.astype(BF16)

        @pl.when(pair == 0)
        def _():
            tot_ref[...] = jnp.zeros_like(tot_ref)

        def q_block(qi, _):
            q0 = pl.multiple_of(qi * t, t)
            qb = q_ref[pl.ds(q0, t), :]
            qms = [_masked_q(qb, j) for j in range(2)]

            def step(k0, carry, diagonal):
                kb = k_ref[pl.ds(k0, t), :].astype(BF16)
                vb = v_ref[pl.ds(k0, t), :].astype(BF16)
                heads = range(2)
                zs = [_dot_nt(qms[j], kb) for j in heads]
                sps = [_softplus(z) for z in zs]
                log_keeps = [(jnp.where(strict, -sp, 0.0) if diagonal else -sp) for sp in sps]
                right_l = [_split2_dot(lk, later) for lk in log_keeps]
                atts = [jnp.exp((zs[j] - sps[j]) + right_l[j] + carry[j][0]) for j in heads]
                if diagonal:
                    atts = [jnp.where(strict, att, 0.0) for att in atts]
                return tuple((carry[j][0] + jnp.sum(log_keeps[j], axis=1, keepdims=True),
                              carry[j][1] + _dot_nn(atts[j].astype(BF16), vb)) for j in heads)

            init = ((jnp.zeros((t, 1), F32), jnp.zeros((t, LANES), F32)),) * 2
            (tot0, acc0), (tot1, acc1) = _over_key_blocks(qi, step, init, reverse=True)
            o_ref[pl.ds(q0, t), :] = jnp.where(_head_mask(0), acc0, acc1)
            tot_ref[pl.ds(q0, t), :] += _lane_put(tot0, 2 * pair) + _lane_put(tot1, 2 * pair + 1)
            return 0

        lax.fori_loop(0, SEQ // t, q_block, 0)

    batch_spec = pl.BlockSpec((SEQ, LANES), lambda b, p: (b, 0))
    return pl.pallas_call(
        body, name="sb_attn_fwd", grid=(n // SEQ, N_PAIRS), in_specs=_qkv_specs(), out_specs=[_pair_spec(), batch_spec],
        out_shape=[jax.ShapeDtypeStruct((n, ATT_W), F32), jax.ShapeDtypeStruct((n, LANES), F32)],
        compiler_params=pltpu.CompilerParams(dimension_semantics=("parallel", "arbitrary")),
    )(qkv, qkv, qkv)


def sb_attn_bwd(qkv, tot, do):
    n = qkv.shape[0]
    t = ATT_BLK

    def body(q_ref, k_ref, v_ref, tot_ref, do_ref, dq_ref, dk_ref, dv_ref, dk_acc, dv_acc):
        pair = pl.program_id(1)
        strict = _below_diagonal(True)
        upto = jnp.logical_not(strict).astype(BF16)
        dk_acc[...] = jnp.zeros_like(dk_acc)
        dv_acc[...] = jnp.zeros_like(dv_acc)

        def q_block(qi, _):
            q0 = pl.multiple_of(qi * t, t)
            qb = q_ref[pl.ds(q0, t), :]
            tot_q = tot_ref[pl.ds(q0, t), :]
            dob = do_ref[pl.ds(q0, t), :].astype(F32)
            qms = [_masked_q(qb, j) for j in range(2)]
            doms = [jnp.where(_head_mask(j), dob, 0.0).astype(BF16) for j in range(2)]
            totals = [_lane_pick(tot_q, 2 * pair + j) for j in range(2)]

            def step(k0, carry, diagonal):
                kb = k_ref[pl.ds(k0, t), :].astype(BF16)
                vb = v_ref[pl.ds(k0, t), :].astype(BF16)
                heads = range(2)
                zs = [_dot_nt(qms[j], kb) for j in heads]
                d_atts = [_dot_nt(doms[j], vb) for j in heads]
                sps = [_softplus(z) for z in zs]
                log_keeps = [(jnp.where(strict, -sp, 0.0) if diagonal else -sp) for sp in sps]
                log_betas = [z - sp for z, sp in zip(zs, sps)]
                left_l = [_split2_dot(lk, upto) for lk in log_keeps]
                atts = [jnp.exp(log_betas[j] + (totals[j] - (carry[j][0] + left_l[j]))) for j in heads]
                if diagonal:
                    atts = [jnp.where(strict, att, 0.0) for att in atts]
                gs = [att * d_att for att, d_att in zip(atts, d_atts)]
                dv = _dot_tn(atts[0].astype(BF16), doms[0]) + _dot_tn(atts[1].astype(BF16), doms[1])
                left_g = [_dot_nn(g.astype(BF16), upto) for g in gs]
                dzs = [gs[j] - jnp.exp(log_betas[j]) * (carry[j][1] + left_g[j]) for j in heads]
                if diagonal:
                    dzs = [jnp.where(strict, dz, 0.0) for dz in dzs]
                dzs = [dz.astype(BF16) for dz in dzs]
                dk = _dot_tn(dzs[0], qms[0]) + _dot_tn(dzs[1], qms[1])
                dk_acc[pl.ds(k0, t), :] += dk
                dv_acc[pl.ds(k0, t), :] += dv
                return tuple((carry[j][0] + jnp.sum(log_keeps[j], axis=1, keepdims=True),
                              carry[j][1] + jnp.sum(gs[j], axis=1, keepdims=True), carry[j][2] + _dot_nn(dzs[j], kb))
                             for j in heads)

            zero = jnp.zeros((t, 1), F32)
            init = ((zero, zero, jnp.zeros((t, LANES), F32)),) * 2
            (_, _, dq0), (_, _, dq1) = _over_key_blocks(qi, step, init, reverse=False)
            dq_ref[pl.ds(q0, t), :] = (jnp.where(_head_mask(0), dq0, dq1) * ATT_SCALE).astype(BF16)
            return 0

        lax.fori_loop(0, SEQ // t, q_block, 0)
        dk_ref[...] = dk_acc[...].astype(BF16)
        dv_ref[...] = dv_acc[...].astype(BF16)

    out = jax.ShapeDtypeStruct((n, ATT_W), BF16)
    batch_spec = pl.BlockSpec((SEQ, LANES), lambda b, p: (b, 0))
    return pl.pallas_call(
        body, name="sb_attn_bwd", grid=(n // SEQ, N_PAIRS), in_specs=_qkv_specs() + [batch_spec, _pair_spec()],
        out_specs=[_pair_spec()] * 3, out_shape=[out, out, out],
        scratch_shapes=[pltpu.VMEM((SEQ, LANES), F32), pltpu.VMEM((SEQ, LANES), F32)],
        compiler_params=pltpu.CompilerParams(dimension_semantics=("parallel", "parallel")),
    )(qkv, qkv, qkv, tot, do)


NEG_BIG = -1e30


def fox_attn_fwd(qkv, cum, cum_t):
    n = qkv.shape[0]
    t = ATT_BLK

    def body(q_ref, k_ref, v_ref, cum_ref, cumt_ref, o_ref, lse_ref):
        pair = pl.program_id(1)
        causal = _below_diagonal(False)

        @pl.when(pair == 0)
        def _():
            lse_ref[...] = jnp.zeros_like(lse_ref)

        def q_block(qi, _):
            q0 = pl.multiple_of(qi * t, t)
            qb = q_ref[pl.ds(q0, t), :]
            cum_q = cum_ref[pl.ds(q0, t), :]
            qms = [_masked_q(qb, j) for j in range(2)]
            cqs = [_lane_pick(cum_q, 2 * pair + j) for j in range(2)]

            def step(k0, carry, diagonal):
                kb = k_ref[pl.ds(k0, t), :].astype(BF16)
                vb = v_ref[pl.ds(k0, t), :].astype(BF16)
                heads = range(2)
                zs = [_dot_nt(qms[j], kb) + cqs[j] - cumt_ref[pl.ds(2 * pair + j, 1), pl.ds(k0, t)] for j in heads]
                if diagonal:
                    zs = [jnp.where(causal, z, NEG_BIG) for z in zs]
                m_new = [jnp.maximum(carry[j][0], jnp.max(zs[j], axis=1, keepdims=True)) for j in heads]
                ps = [jnp.exp(zs[j] - m_new[j]) for j in heads]
                alphas = [jnp.exp(carry[j][0] - m_new[j]) for j in heads]
                return tuple((m_new[j], alphas[j] * carry[j][1] + jnp.sum(ps[j], axis=1, keepdims=True),
                              alphas[j] * carry[j][2] + _dot_nn(ps[j].astype(BF16), vb)) for j in heads)

            init = ((jnp.full((t, 1), NEG_BIG, F32), jnp.zeros((t, 1), F32), jnp.zeros((t, LANES), F32)),) * 2
            (m0, l0, acc0), (m1, l1, acc1) = _over_key_blocks(qi, step, init, reverse=False)
            o_ref[pl.ds(q0, t), :] = jnp.where(_head_mask(0), acc0 / l0, acc1 / l1)
            lse_ref[pl.ds(q0, t), :] += _lane_put(m0 + jnp.log(l0), 2 * pair) + _lane_put(m1 + jnp.log(l1), 2 * pair + 1)
            return 0

        lax.fori_loop(0, SEQ // t, q_block, 0)

    batch_spec = pl.BlockSpec((SEQ, LANES), lambda b, p: (b, 0))
    return pl.pallas_call(
        body, name="fox_attn_fwd", grid=(n // SEQ, N_PAIRS),
        in_specs=_qkv_specs() + [batch_spec, pl.BlockSpec((None, N_HEADS, SEQ), lambda b, p: (b, 0, 0))],
        out_specs=[_pair_spec(), batch_spec],
        out_shape=[jax.ShapeDtypeStruct((n, ATT_W), F32), jax.ShapeDtypeStruct((n, LANES), F32)],
        compiler_params=pltpu.CompilerParams(dimension_semantics=("parallel", "arbitrary")),
    )(qkv, qkv, qkv, cum, cum_t)


def fox_attn_bwd(qkv, cum, cum_t, lse, o, do):
    n = qkv.shape[0]
    t = ATT_BLK

    def body(q_ref, k_ref, v_ref, cum_ref, cumt_ref, lse_ref, o_ref, do_ref, dq_ref, dk_ref, dv_ref, dcq_ref, dck_ref,
             dk_acc, dv_acc):
        pair = pl.program_id(1)
        causal = _below_diagonal(False)
        dk_acc[...] = jnp.zeros_like(dk_acc)
        dv_acc[...] = jnp.zeros_like(dv_acc)

        @pl.when(pair == 0)
        def _():
            dcq_ref[...] = jnp.zeros_like(dcq_ref)
            dck_ref[...] = jnp.zeros_like(dck_ref)

        def q_block(qi, _):
            q0 = pl.multiple_of(qi * t, t)
            qb = q_ref[pl.ds(q0, t), :]
            ob = o_ref[pl.ds(q0, t), :]
            dob = do_ref[pl.ds(q0, t), :].astype(F32)
            cum_q = cum_ref[pl.ds(q0, t), :]
            lse_q = lse_ref[pl.ds(q0, t), :]
            qms = [_masked_q(qb, j) for j in range(2)]
            dom32 = [jnp.where(_head_mask(j), dob, 0.0) for j in range(2)]
            doms = [d.astype(BF16) for d in dom32]
            deltas = [jnp.sum(d * ob, axis=1, keepdims=True) for d in dom32]
            cqs = [_lane_pick(cum_q, 2 * pair + j) for j in range(2)]
            lqs = [_lane_pick(lse_q, 2 * pair + j) for j in range(2)]

            def step(k0, carry, diagonal):
                kb = k_ref[pl.ds(k0, t), :].astype(BF16)
                vb = v_ref[pl.ds(k0, t), :].astype(BF16)
                heads = range(2)
                zs = [_dot_nt(qms[j], kb) + cqs[j] - cumt_ref[pl.ds(2 * pair + j, 1), pl.ds(k0, t)] for j in heads]
                d_ps = [_dot_nt(doms[j], vb) for j in heads]
                if diagonal:
                    zs = [jnp.where(causal, z, NEG_BIG) for z in zs]
                ps = [jnp.exp(zs[j] - lqs[j]) for j in heads]
                dv_acc[pl.ds(k0, t), :] += _dot_tn(ps[0].astype(BF16), doms[0]) + _dot_tn(ps[1].astype(BF16), doms[1])
                dzs = [ps[j] * (d_ps[j] - deltas[j]) for j in heads]
                dzb = [dz.astype(BF16) for dz in dzs]
                dk_acc[pl.ds(k0, t), :] += _dot_tn(dzb[0], qms[0]) + _dot_tn(dzb[1], qms[1])
                for j in heads:
                    dck_ref[pl.ds(2 * pair + j, 1), pl.ds(k0, t)] += jnp.sum(dzs[j], axis=0, keepdims=True)
                return tuple((carry[j][0] + _dot_nn(dzb[j], kb), carry[j][1] + jnp.sum(dzs[j], axis=1, keepdims=True))
                             for j in heads)

            init = ((jnp.zeros((t, LANES), F32), jnp.zeros((t, 1), F32)),) * 2
            (dq0, dcq0), (dq1, dcq1) = _over_key_blocks(qi, step, init, reverse=False)
            dq_ref[pl.ds(q0, t), :] = (jnp.where(_head_mask(0), dq0, dq1) * ATT_SCALE).astype(BF16)
            dcq_ref[pl.ds(q0, t), :] += _lane_put(dcq0, 2 * pair) + _lane_put(dcq1, 2 * pair + 1)
            return 0

        lax.fori_loop(0, SEQ // t, q_block, 0)
        dk_ref[...] = dk_acc[...].astype(BF16)
        dv_ref[...] = dv_acc[...].astype(BF16)

    batch_spec = pl.BlockSpec((SEQ, LANES), lambda b, p: (b, 0))
    t_spec = pl.BlockSpec((None, N_HEADS, SEQ), lambda b, p: (b, 0, 0))
    out = jax.ShapeDtypeStruct((n, ATT_W), BF16)
    return pl.pallas_call(
        body, name="fox_attn_bwd", grid=(n // SEQ, N_PAIRS),
        in_specs=_qkv_specs() + [batch_spec, t_spec, batch_spec, _pair_spec(), _pair_spec()],
        out_specs=[_pair_spec()] * 3 + [batch_spec, t_spec],
        scratch_shapes=[pltpu.VMEM((SEQ, LANES), F32), pltpu.VMEM((SEQ, LANES), F32)],
        out_shape=[out, out, out, jax.ShapeDtypeStruct((n, LANES), F32), jax.ShapeDtypeStruct((n // SEQ, N_HEADS, SEQ), F32)],
        compiler_params=pltpu.CompilerParams(dimension_semantics=("parallel", "arbitrary")),
    )(qkv, qkv, qkv, cum, cum_t, lse, o, do)


_HBM = pl.BlockSpec(memory_space=pl.ANY)


def _my_place():
    return lax.axis_index("x"), lax.axis_index("y"), lax.axis_index("c")


def my_index():
    mx, my, mc = _my_place()
    return 4 * mx + 2 * my + mc


def all_gather(name, xs):
    single = not isinstance(xs, (list, tuple))
    xs = [xs] if single else list(xs)
    na = len(xs)

    def body(*refs):
        x_refs, out_refs = refs[:na], refs[na:2 * na]
        send_sems, recv_sems, local_sems = refs[2 * na:]
        mx, my, mc = _my_place()
        me, sibling = (mx, my, mc), (mx, my, 1 - mc)
        chips = [(1 - mx, my), (mx, 1 - my), (1 - mx, 1 - my)]

        def slot(a, px, py, pc):
            return out_refs[a].at[4 * px + 2 * py + pc]

        def copy(a, k, block, to, src=None):
            return pltpu.make_async_remote_copy(
                src_ref=slot(a, *block) if src is None else src, dst_ref=slot(a, *block),
                send_sem=send_sems.at[7 * a + k], recv_sem=recv_sems.at[7 * a + k], device_id=to, device_id_type=MESH)

        mine = [pltpu.make_async_copy(x_refs[a], slot(a, *me), local_sems.at[a]) for a in range(na)]
        for cp in mine:
            cp.start()
        first = []
        for j, chip in enumerate(chips):
            first += [copy(a, 1 + j, me, (*chip, mc), src=x_refs[a]) for a in range(na)]
        first += [copy(a, 0, me, sibling, src=x_refs[a]) for a in range(na)]
        for cp in first:
            cp.start()
        passed = []
        for j, chip in enumerate(chips):
            for a in range(na):
                copy(a, 1 + j, (*chip, mc), me).wait_recv()
                passed.append(copy(a, 4 + j, (*chip, mc), sibling))
                passed[-1].start()
        for a in range(na):
            copy(a, 0, sibling, me).wait_recv()
        for j, chip in enumerate(chips):
            for a in range(na):
                copy(a, 4 + j, (*chip, 1 - mc), me).wait_recv()
        for cp in first + passed:
            cp.wait_send()
        for cp in mine:
            cp.wait()

    outs = pl.pallas_call(
        body, name=name, in_specs=[_HBM] * na, out_specs=[_HBM] * na,
        out_shape=[jax.ShapeDtypeStruct((N_DEV,) + x.shape, x.dtype) for x in xs],
        scratch_shapes=[pltpu.SemaphoreType.DMA((7 * na,)), pltpu.SemaphoreType.DMA((7 * na,)), pltpu.SemaphoreType.DMA((na,))],
    )(*xs)
    return outs[0] if single else list(outs)


_SEM = pl.BlockSpec(memory_space=pltpu.SEMAPHORE)
_HBM_ONLY = pl.BlockSpec(memory_space=pltpu.HBM)
_EFFECT = pltpu.SideEffectType.DATAFLOW_SIDE_EFFECTING
N_PEERS = N_DEV


def _peers():
    mx, my, mc = _my_place()
    return [((1 - mx) if (r >> 2) & 1 else mx, (1 - my) if (r >> 1) & 1 else my, (1 - mc) if r & 1 else mc)
            for r in range(N_DEV)]


def _exchange_copies(scatter, x_refs, land_refs, send_sems, recv_sems):
    me = my_index()
    copies = []
    for a, (x_ref, land_ref) in enumerate(zip(x_refs, land_refs)):
        for r, (px, py, pc) in enumerate(_peers()):
            src = x_ref.at[4 * px + 2 * py + pc] if scatter else x_ref
            dst = land_ref.at[r] if scatter else land_ref.at[me]
            copies.append(pltpu.make_async_remote_copy(
                src_ref=src, dst_ref=dst, send_sem=send_sems.at[N_PEERS * a + r], recv_sem=recv_sems.at[N_PEERS * a + r],
                device_id=(px, py, pc), device_id_type=MESH))
    return copies


def exchange_start(name, xs, scatter):
    na = len(xs)
    lands = [lax.empty((N_PEERS,) + x.shape[1:] if scatter else (N_DEV,) + x.shape, x.dtype) for x in xs]

    def body(*refs):
        x_refs, land_refs, send_sems, recv_sems = refs[:na], refs[na:2 * na], refs[2 * na], refs[2 * na + 1]
        token = refs[-1]
        for cp in _exchange_copies(scatter, x_refs, land_refs, send_sems, recv_sems):
            cp.start()
        token[...] = jnp.zeros_like(token)

    outs = pl.pallas_call(
        body, name=name,
        out_shape=(pltpu.SemaphoreType.DMA((N_PEERS * na,)), pltpu.SemaphoreType.DMA((N_PEERS * na,)),
                   *[pltpu.HBM(x.shape, x.dtype) for x in xs], *[pltpu.HBM(l.shape, l.dtype) for l in lands],
                   jax.ShapeDtypeStruct((8, LANES), F32)),
        in_specs=[_HBM_ONLY] * (2 * na),
        out_specs=(_SEM, _SEM, *[_HBM_ONLY] * (2 * na), pl.BlockSpec(memory_space=pltpu.VMEM)),
        input_output_aliases={i: 2 + i for i in range(2 * na)},
        compiler_params=pltpu.CompilerParams(has_side_effects=_EFFECT),
    )(*[pltpu.with_memory_space_constraint(x, pltpu.HBM) for x in xs],
      *[pltpu.with_memory_space_constraint(l, pltpu.HBM) for l in lands])
    return (scatter, outs[0], outs[1], outs[2:2 + na], outs[2 + na:2 + 2 * na]), outs[-1]


def exchange_finish(name, handle, after):
    scatter, send_sems, recv_sems, xs, lands = handle
    na = len(xs)

    def body(*refs):
        x_refs, land_refs, send_ref, recv_ref = refs[:na], refs[na:2 * na], refs[2 * na], refs[2 * na + 1]
        for cp in _exchange_copies(scatter, x_refs, land_refs, send_ref, recv_ref):
            cp.wait_send()
            cp.wait_recv()

    outs = pl.pallas_call(
        body, name=name,
        out_shape=tuple(pltpu.HBM(t.shape, t.dtype) for t in list(xs) + list(lands)),
        in_specs=[_HBM_ONLY] * (2 * na) + [_SEM, _SEM, _HBM],
        out_specs=tuple([_HBM_ONLY] * (2 * na)),
        input_output_aliases={i: i for i in range(2 * na)},
        compiler_params=pltpu.CompilerParams(has_side_effects=_EFFECT),
    )(*xs, *lands, send_sems, recv_sems, after)
    return list(outs[:na]), list(outs[na:])


def _pick_rows(n, target):
    best = None
    for t in range(8, min(n, target) + 1, 8):
        if n % t == 0:
            best = t
    return best if best is not None else n


def add_blocks(name, parts, out_dtype, rows=512):
    r, w = parts[0].shape
    tr = _pick_rows(r, rows)

    def body(*refs):
        acc = refs[0][...].astype(F32)
        for ref in refs[1:-1]:
            acc = acc + ref[...].astype(F32)
        refs[-1][...] = acc.astype(refs[-1].dtype)

    spec = pl.BlockSpec((tr, w), lambda i: (i, 0))
    return pl.pallas_call(
        body, name=name, grid=(r // tr,), in_specs=[spec] * len(parts), out_specs=spec,
        out_shape=jax.ShapeDtypeStruct((r, w), out_dtype),
        compiler_params=pltpu.CompilerParams(dimension_semantics=("parallel",)),
    )(*parts)


def sum_rows(name, x):
    def body(x_ref, o_ref):
        o_ref[...] = jnp.sum(x_ref[...], axis=0, keepdims=True)

    return pl.pallas_call(body, name=name, out_shape=jax.ShapeDtypeStruct((1, x.shape[1]), F32))(x)


def gather_start(name, blocks):
    return exchange_start(name, blocks, scatter=False)


def gather_finish(name, handle, after):
    return exchange_finish(name, handle, after)[1]


def scatter_start(name, g8s):
    return exchange_start(name, g8s, scatter=True)


def scatter_finish(name, handle, after):
    _, lands = exchange_finish(name, handle, after)
    outs = []
    for a, land in enumerate(lands):
        w = land.shape[-1]
        outs.append(add_blocks(f"{name}_sum{a}", [land[k].reshape(-1, w) for k in range(N_PEERS)], F32).reshape(land.shape[1:]))
    return outs


def _pack(arrays, width, row_mult, dtype, lead=0):
    parts, metas = [], []
    for a in arrays:
        lead_shape = a.shape[:lead]
        size = int(np.prod(a.shape[lead:]))
        chunk = row_mult * width
        padded = -(-size // chunk) * chunk
        flat = a.astype(dtype).reshape(lead_shape + (size,))
        if padded != size:
            flat = jnp.pad(flat, [(0, 0)] * lead + [(0, padded - size)])
        parts.append(flat.reshape(lead_shape + (padded // width, width)))
        metas.append((a.shape[lead:], size, padded // width))
    return jnp.concatenate(parts, axis=lead), metas


def _unpack(slab, metas, lead=0):
    out, r0 = [], 0
    for shape, size, rows in metas:
        part = lax.slice_in_dim(slab, r0, r0 + rows, axis=lead)
        lead_shape = part.shape[:lead]
        flat = part.reshape(lead_shape + (rows * part.shape[-1],))
        out.append(lax.slice_in_dim(flat, 0, size, axis=lead).reshape(lead_shape + tuple(shape)))
        r0 += rows
    return out


def _f_adamw(w, g, m, v):
    m = ADAM_B1 * m + (1.0 - ADAM_B1) * g
    v = ADAM_B2 * v + (1.0 - ADAM_B2) * (g * g)
    m_hat = m / (1.0 - ADAM_B1 ** ADAM_STEP)
    v_hat = v / (1.0 - ADAM_B2 ** ADAM_STEP)
    delta = (-ADAM_LR) * (m_hat / (jnp.sqrt(v_hat) + ADAM_EPS) + ADAM_WD * w)
    return delta, m, v


def adamw(name, w, g, m, v):
    shape = w.shape
    w2 = shape[-1]
    flat = [a.reshape(-1, w2) for a in (w, g, m, v)]
    tm = _pick_rows(flat[0].shape[0], 256)
    outs = ew_fwd(name, _f_adamw, flat, [], [], [w2] * 3, [F32] * 3, tm=tm)
    return [o.reshape(shape) for o in outs]


WEIGHTS = ["ffn1_norm", "ffn1_w1", "ffn1_w3", "ffn1_w2", "mix_norm", "w_in", "conv_w", "conv_b", "rg_wa", "rg_ba", "rg_wx",
           "rg_bx", "rg_lam", "fox_bf", "merge_b", "w_rg", "w_sb", "w_fox", "w_o", "ffn2_norm", "ffn2_w1", "ffn2_w3",
           "ffn2_w2", "ada_w", "ada_b", "final_norm", "final_ada_w", "final_ada_b"]
GATHERED = {"ffn1_w1": 2, "ffn1_w3": 2, "ffn1_w2": 1, "w_in": 2, "w_rg": 1, "w_sb": 2, "w_fox": 2, "w_o": 1,
            "ffn2_w1": 2, "ffn2_w3": 2, "ffn2_w2": 1}
REPLICATED = ["ffn1_norm", "mix_norm", "conv_b", "rg_wa", "rg_ba", "rg_wx", "rg_bx", "rg_lam", "fox_bf", "merge_b",
              "ffn2_norm", "final_norm"]
GROUPS = (("ffn1", ("ffn1_w1", "ffn1_w3", "ffn1_w2")), ("mix", ("w_in", "w_rg", "w_sb", "w_fox", "w_o")),
          ("ffn2", ("ffn2_w1", "ffn2_w3", "ffn2_w2")))
IN_CUTS = (0, 1024, 2048, 3584, 5120, 5128, 8200)


def _unshard(g, axis):
    g = jnp.moveaxis(g, 0, axis)
    shape = g.shape
    return g.reshape(shape[:axis] + (shape[axis] * shape[axis + 1],) + shape[axis + 2:])


def _reshard(full, axis):
    shape = full.shape
    g = full.reshape(shape[:axis] + (N_DEV, shape[axis] // N_DEV) + shape[axis + 1:])
    return jnp.moveaxis(g, axis, 0)


def _block_diag(w):
    nb, bd, _ = w.shape
    eye = jnp.eye(nb, dtype=bool)[:, None, :, None]
    return jnp.where(eye, w[:, :, None, :], 0.0).reshape(nb * bd, nb * bd)


def _diag_blocks(m, nb=RG_BLOCKS):
    bd = m.shape[0] // nb
    return jnp.stack([m[k * bd:(k + 1) * bd, k * bd:(k + 1) * bd] for k in range(nb)])


def _pad_lanes(a, width=LANES):
    return jnp.pad(a, [(0, 0)] * (a.ndim - 1) + [(0, width - a.shape[-1])])


def _bp(m, k, which):
    return m[:, k, which][:, None, :]


def _f_silu(c):
    return c * jax.nn.sigmoid(c)


def _f_add_bias(a, b):
    return a + b


FFN_TM = 512
FFN_TN = 1408
FFN_SUB = 256


def ffn_up(name, h, w1, w3):
    n, k = h.shape
    f = w1.shape[1]
    tm, tn = min(FFN_TM, n), _pick_tile(f, FFN_TN)

    def body(h_ref, w1_ref, w3_ref, a_ref, b_ref, s_ref):
        subs = [pl.ds(r, FFN_SUB) for r in range(0, tm, FFN_SUB)] if tm % FFN_SUB == 0 else [pl.ds(0, tm)]
        hs = [h_ref[rows, :] for rows in subs]
        a_s = [jnp.dot(hv, w1_ref[...], preferred_element_type=F32) for hv in hs]
        b_s = [jnp.dot(hv, w3_ref[...], preferred_element_type=F32) for hv in hs]
        for rows, a, b in zip(subs, a_s, b_s):
            a_ref[rows, :] = a.astype(BF16)
            b_ref[rows, :] = b.astype(BF16)
            s_ref[rows, :] = ((a * jax.nn.sigmoid(a)) * b).astype(BF16)

    wspec = pl.BlockSpec((k, tn), lambda i, j: (0, j))
    ospec = pl.BlockSpec((tm, tn), lambda i, j: (i, j))
    out = jax.ShapeDtypeStruct((n, f), BF16)
    return pl.pallas_call(
        body, name=name, grid=(n // tm, f // tn), in_specs=[pl.BlockSpec((tm, k), lambda i, j: (i, 0)), wspec, wspec],
        out_specs=[ospec] * 3, out_shape=[out] * 3,
        compiler_params=pltpu.CompilerParams(dimension_semantics=("parallel", "parallel")),
    )(h, w1, w3)


def ffn_down_dx(name, dy, w2, a, b):
    n, k = dy.shape
    f = w2.shape[0]
    tm, tn = min(FFN_TM, n), _pick_tile(f, FFN_TN)

    def body(dy_ref, w2_ref, a_ref, b_ref, da_ref, db_ref):
        subs = [pl.ds(r, FFN_SUB) for r in range(0, tm, FFN_SUB)] if tm % FFN_SUB == 0 else [pl.ds(0, tm)]
        ds_s = [_dot_nt(dy_ref[rows, :], w2_ref[...]) for rows in subs]
        for rows, ds in zip(subs, ds_s):
            av = a_ref[rows, :].astype(F32)
            sig = jax.nn.sigmoid(av)
            da_ref[rows, :] = (ds * b_ref[rows, :].astype(F32) * (sig * (1.0 + av * (1.0 - sig)))).astype(BF16)
            db_ref[rows, :] = (ds * (av * sig)).astype(BF16)

    ospec = pl.BlockSpec((tm, tn), lambda i, j: (i, j))
    out = jax.ShapeDtypeStruct((n, f), BF16)
    return pl.pallas_call(
        body, name=name, grid=(n // tm, f // tn),
        in_specs=[pl.BlockSpec((tm, k), lambda i, j: (i, 0)), pl.BlockSpec((tn, k), lambda i, j: (j, 0)), ospec, ospec],
        out_specs=[ospec] * 2, out_shape=[out] * 2,
        compiler_params=pltpu.CompilerParams(dimension_semantics=("parallel", "parallel")),
    )(dy, w2, a, b)


def _ffn_fwd(tag, x, shift, scale, gate, gain, w1, w3, w2):
    h = ew_fwd(f"ffn_norm_{tag}", f_norm_mod, [x], [shift, scale], [gain], [D], [BF16])[0]
    a, b3, s = ffn_up(f"ffn_up_{tag}", h, w1, w3)
    y = matmul(f"ffn_down_{tag}", s, w2, "nn", tm=1024)
    xo = ew_fwd(f"ffn_res_{tag}", functools.partial(f_resid, 0.5), [x, y], [gate], [], [D], [F32])[0]
    return xo, (x, h, a, b3, s, y)


def _ffn_bwd(tag, dxo, saved, shift, scale, gate, gain, w1, w3, w2):
    x, h, a, b3, s, y = saved
    (dy,), (dgate,), _ = ew_bwd(f"ffn_res_bwd_{tag}", functools.partial(f_resid, 0.5), [x, y], [gate], [], [dxo], [None, BF16])
    da, db3 = ffn_down_dx(f"ffn_down_dx_{tag}", dy, w2, a, b3)
    dw2 = matmul(f"ffn_dw2_{tag}", s, dy, "tn", tm=1408, tn=256, out_dtype=BF16)
    dw1 = matmul(f"ffn_dw1_{tag}", h, da, "tn", tm=1024, tn=256, out_dtype=BF16)
    dw3 = matmul(f"ffn_dw3_{tag}", h, db3, "tn", tm=1024, tn=256, out_dtype=BF16)
    dh = matmul(f"ffn_up_dx_{tag}", [da, db3], [w1, w3], "nt", tm=1024)
    (dx,), (dshift, dscale), (dgain,) = ew_bwd(f"ffn_norm_bwd_{tag}", f_norm_mod, [x], [shift, scale], [gain], [dh], [F32],
                                               adds=[dxo])
    return dx, (dshift, dscale, dgate), dgain, dw1, dw3, dw2


def _mixer_fwd(tag, x, shift, scale, gate, p):
    h = ew_fwd(f"mix_norm_{tag}", f_norm_mod, [x], [shift, scale], [p["gain"]], [D], [BF16])[0]
    rgx = matmul(f"in_rgx_{tag}", h, p["w_rgx"], "nn")
    rgate = matmul(f"in_gate_{tag}", h, p["w_gate"], "nn", out_dtype=BF16)
    sbqkv = matmul(f"in_sb_{tag}", h, p["w_sbqkv"], "nn", out_dtype=BF16)
    foxqkv = matmul(f"in_fox_{tag}", h, p["w_foxqkv"], "nn", out_dtype=BF16)
    ff = matmul(f"in_forget_{tag}", h, p["w_f"], "nn")
    mg = matmul(f"in_merge_{tag}", h, p["w_merge"], "nn", out_dtype=BF16)
    xa = conv_fwd(rgx, p["conv_w8"], p["conv_b"])
    pre_r = matmul(f"rg_a_{tag}", xa, p["wa_bd"], "nn", out_dtype=BF16)
    pre_i = matmul(f"rg_x_{tag}", xa, p["wx_bd"], "nn", out_dtype=BF16)
    a, u = ew_fwd(f"rg_gates_{tag}", f_rg_gates, [pre_r, pre_i, xa], [], [p["ba"], p["bx"], p["lam"]], [D, D], [F32, F32],
                  tm=EW_ROWS_WIDE)
    hs = scan_fwd(a, u)
    ya = ew_fwd(f"rg_out_{tag}", f_gelu_mul, [rgate, hs], [], [], [D], [BF16])[0]
    yb, sb_tot = sb_attn_fwd(sbqkv)
    lf = ew_fwd(f"fox_logf_{tag}", f_log_sigmoid_bias, [ff], [], [p["bf"]], [LANES], [F32])[0]
    cum = seq_cumsum(f"fox_cum_{tag}", [lf], [1.0], False)
    cum_t = cum.reshape(-1, SEQ, LANES)[:, :, :N_HEADS].transpose(0, 2, 1)
    yc, lse = fox_attn_fwd(foxqkv, cum, cum_t)
    pa = matmul(f"out_rg_{tag}", ya, p["w_rg"], "nn", out_dtype=BF16)
    pb = matmul(f"out_sb_{tag}", yb, p["w_sb"], "nn", out_dtype=BF16)
    pc = matmul(f"out_fox_{tag}", yc, p["w_fox"], "nn", out_dtype=BF16)
    mixed = ew_fwd(f"merge_{tag}", f_merge, [mg, pa, pb, pc], [], [p["merge_b"]], [D], [BF16], tm=EW_ROWS_WIDE)[0]
    y = matmul(f"out_o_{tag}", mixed, p["w_o"], "nn")
    xo = ew_fwd(f"mix_res_{tag}", functools.partial(f_resid, 1.0), [x, y], [gate], [], [D], [F32])[0]
    saved = dict(x=x, h=h, rgx=rgx, rgate=rgate, sbqkv=sbqkv, foxqkv=foxqkv, ff=ff, mg=mg, xa=xa, pre_r=pre_r, pre_i=pre_i,
                 a=a, hs=hs, ya=ya, yb=yb, sb_tot=sb_tot, cum=cum, cum_t=cum_t, yc=yc, lse=lse, pa=pa, pb=pb, pc=pc,
                 mixed=mixed, y=y)
    return xo, saved


def _mixer_bwd(tag, dxo, s, shift, scale, gate, p):
    (dy,), (dgate,), _ = ew_bwd(f"mix_res_bwd_{tag}", functools.partial(f_resid, 1.0), [s["x"], s["y"]], [gate], [], [dxo],
                                [None, BF16])
    dmixed = matmul(f"out_o_dx_{tag}", dy, p["w_o"], "nt")
    g = {"w_o": matmul(f"out_o_dw_{tag}", s["mixed"], dy, "tn", tm=1024, tn=256, out_dtype=BF16)}
    (dmg, dpa, dpb, dpc), _, (g["merge_b"],) = ew_bwd(
        f"merge_bwd_{tag}", f_merge, [s["mg"], s["pa"], s["pb"], s["pc"]], [], [p["merge_b"]], [dmixed], [BF16] * 4,
        tm=EW_ROWS_WIDE)
    dya = matmul(f"out_rg_dx_{tag}", dpa, p["w_rg"], "nt")
    g["w_rg"] = matmul(f"out_rg_dw_{tag}", s["ya"], dpa, "tn", tm=1024, tn=256, out_dtype=BF16)
    dyb = matmul(f"out_sb_dx_{tag}", dpb, p["w_sb"], "nt", out_dtype=BF16)
    g["w_sb"] = matmul(f"out_sb_dw_{tag}", s["yb"], dpb, "tn", tm=1024, tn=256, out_dtype=BF16)
    dyc = matmul(f"out_fox_dx_{tag}", dpc, p["w_fox"], "nt", out_dtype=BF16)
    g["w_fox"] = matmul(f"out_fox_dw_{tag}", s["yc"], dpc, "tn", tm=1024, tn=256, out_dtype=BF16)
    dq_c, dk_c, dv_c, dcq, dck = fox_attn_bwd(s["foxqkv"], s["cum"], s["cum_t"], s["lse"], s["yc"], dyc)
    dck_rows = _pad_lanes(dck.transpose(0, 2, 1).reshape(-1, N_HEADS))
    dlf = seq_cumsum(f"fox_cum_bwd_{tag}", [dcq, dck_rows], [1.0, -1.0], True)
    (dff,), _, (dbf,) = ew_bwd(f"fox_logf_bwd_{tag}", f_log_sigmoid_bias, [s["ff"]], [], [p["bf"]], [dlf], [BF16])
    g["fox_bf"] = dbf[0, :N_HEADS]
    dq_b, dk_b, dv_b = sb_attn_bwd(s["sbqkv"], s["sb_tot"], dyb)
    (drgate, dhs), _, _ = ew_bwd(f"rg_out_bwd_{tag}", f_gelu_mul, [s["rgate"], s["hs"]], [], [], [dya], [BF16, F32],
                                 tm=EW_ROWS_WIDE)
    da, du = scan_bwd(s["a"], s["hs"], dhs)
    (dpre_r, dpre_i, dxa1), _, (g["rg_ba"], g["rg_bx"], g["rg_lam"]) = ew_bwd(
        f"rg_gates_bwd_{tag}", f_rg_gates, [s["pre_r"], s["pre_i"], s["xa"]], [], [p["ba"], p["bx"], p["lam"]], [da, du],
        [BF16, BF16, F32], tm=EW_ROWS_WIDE)
    dxa2 = matmul(f"rg_dx_{tag}", [dpre_r, dpre_i], [p["wa_bd"], p["wx_bd"]], "nt")
    g["rg_wa"] = _diag_blocks(matmul(f"rg_a_dw_{tag}", s["xa"], dpre_r, "tn", tm=512, tn=256))
    g["rg_wx"] = _diag_blocks(matmul(f"rg_x_dw_{tag}", s["xa"], dpre_i, "tn", tm=512, tn=256))
    drgx, dwb = conv_bwd(s["rgx"], p["conv_w8"], dxa1, dxa2)
    g["conv_w"] = dwb[:CONV_K]
    g["conv_b"] = dwb[CONV_K]
    cots = [drgx, drgate, dq_b, dk_b, dv_b, dq_c, dk_c, dv_c, dff, dmg]
    w_sb3 = [p["w_sbqkv"][:, k * ATT_W:(k + 1) * ATT_W] for k in range(3)]
    w_fox3 = [p["w_foxqkv"][:, k * ATT_W:(k + 1) * ATT_W] for k in range(3)]
    ws = [p["w_rgx"], p["w_gate"]] + w_sb3 + w_fox3 + [p["w_f"], p["w_merge"]]
    dh = matmul(f"in_dx_{tag}", cots, ws, "nt", tm=512)
    dws = [matmul(f"in_dw{k}_{tag}", s["h"], ct, "tn", tm=1024, tn=256, out_dtype=BF16) for k, ct in enumerate(cots)]
    dws[8] = dws[8][:, :N_HEADS]
    g["w_in"] = jnp.concatenate(dws, axis=1)
    (dx,), (dshift, dscale), (g["mix_norm"],) = ew_bwd(f"mix_norm_bwd_{tag}", f_norm_mod, [s["x"]], [shift, scale], [p["gain"]],
                                                       [dh], [F32], adds=[dxo])
    return dx, (dshift, dscale, dgate), g


def _final_loss(x, target, shift, scale, gain):
    n = x.shape[0]
    tm = min(EW_ROWS, SEQ)
    tpb = SEQ // tm

    def body(x_ref, t_ref, sh_ref, sc_ref, g_ref, loss_ref, dx_ref, dsh_ref, dsc_ref, dg_ref):
        i = pl.program_id(0)
        out, vjp = jax.vjp(f_norm_mod, x_ref[...], sh_ref[...], sc_ref[...], g_ref[...])
        diff = out - t_ref[...]
        dx, dsh, dsc, dg = vjp(diff * (1.0 / D))
        dx_ref[...] = dx
        sq = jnp.sum(jnp.sum(diff * diff, axis=1, keepdims=True), axis=0, keepdims=True)

        @pl.when(i % tpb == 0)
        def _():
            dsh_ref[...] = jnp.zeros_like(dsh_ref)
            dsc_ref[...] = jnp.zeros_like(dsc_ref)

        @pl.when(i == 0)
        def _():
            dg_ref[...] = jnp.zeros_like(dg_ref)
            loss_ref[...] = jnp.zeros_like(loss_ref)

        dsh_ref[...] += dsh
        dsc_ref[...] += dsc
        dg_ref[...] += dg
        loss_ref[...] += jnp.broadcast_to(sq, (1, LANES)) * (0.5 / D)

    row, bp, gp = _row_spec(D, tm), _bparam_spec(D, tpb), _gparam_spec((1, D))
    return pl.pallas_call(
        body, name="final_loss", grid=(n // tm,), in_specs=[row, row, bp, bp, gp],
        out_specs=[_gparam_spec((1, LANES)), row, bp, bp, gp],
        out_shape=[jax.ShapeDtypeStruct((1, LANES), F32), jax.ShapeDtypeStruct((n, D), F32),
                   jax.ShapeDtypeStruct(shift.shape, F32), jax.ShapeDtypeStruct(scale.shape, F32),
                   jax.ShapeDtypeStruct((1, D), F32)],
        compiler_params=pltpu.CompilerParams(dimension_semantics=("arbitrary",)),
    )(x, target, shift, scale, gain)


def kernel(x, c, ffn1_norm, ffn1_w1, ffn1_w3, ffn1_w2, mix_norm, w_in, conv_w, conv_b, rg_wa, rg_ba, rg_wx, rg_bx, rg_lam, fox_bf, merge_b, w_rg, w_sb, w_fox, w_o, ffn2_norm, ffn2_w1, ffn2_w3, ffn2_w2, ada_w, ada_b, final_norm, final_ada_w, final_ada_b, loss_target, m_ffn1_norm, m_ffn1_w1, m_ffn1_w3, m_ffn1_w2, m_mix_norm, m_w_in, m_conv_w, m_conv_b, m_rg_wa, m_rg_ba, m_rg_wx, m_rg_bx, m_rg_lam, m_fox_bf, m_merge_b, m_w_rg, m_w_sb, m_w_fox, m_w_o, m_ffn2_norm, m_ffn2_w1, m_ffn2_w3, m_ffn2_w2, m_ada_w, m_ada_b, m_final_norm, m_final_ada_w, m_final_ada_b, v_ffn1_norm, v_ffn1_w1, v_ffn1_w3, v_ffn1_w2, v_mix_norm, v_w_in, v_conv_w, v_conv_b, v_rg_wa, v_rg_ba, v_rg_wx, v_rg_bx, v_rg_lam, v_fox_bf, v_merge_b, v_w_rg, v_w_sb, v_w_fox, v_w_o, v_ffn2_norm, v_ffn2_w1, v_ffn2_w3, v_ffn2_w2, v_ada_w, v_ada_b, v_final_norm, v_final_ada_w, v_final_ada_b):
    given = dict(zip(["x", "c"] + WEIGHTS + ["loss_target"] + ["m_" + n for n in WEIGHTS] + ["v_" + n for n in WEIGHTS],
                     (x, c, ffn1_norm, ffn1_w1, ffn1_w3, ffn1_w2, mix_norm, w_in, conv_w, conv_b, rg_wa, rg_ba, rg_wx, rg_bx, rg_lam, fox_bf, merge_b, w_rg, w_sb, w_fox, w_o, ffn2_norm, ffn2_w1, ffn2_w3, ffn2_w2, ada_w, ada_b, final_norm, final_ada_w, final_ada_b, loss_target, m_ffn1_norm, m_ffn1_w1, m_ffn1_w3, m_ffn1_w2, m_mix_norm, m_w_in, m_conv_w, m_conv_b, m_rg_wa, m_rg_ba, m_rg_wx, m_rg_bx, m_rg_lam, m_fox_bf, m_merge_b, m_w_rg, m_w_sb, m_w_fox, m_w_o, m_ffn2_norm, m_ffn2_w1, m_ffn2_w3, m_ffn2_w2, m_ada_w, m_ada_b, m_final_norm, m_final_ada_w, m_final_ada_b, v_ffn1_norm, v_ffn1_w1, v_ffn1_w3, v_ffn1_w2, v_mix_norm, v_w_in, v_conv_w, v_conv_b, v_rg_wa, v_rg_ba, v_rg_wx, v_rg_bx, v_rg_lam, v_fox_bf, v_merge_b, v_w_rg, v_w_sb, v_w_fox, v_w_o, v_ffn2_norm, v_ffn2_w1, v_ffn2_w3, v_ffn2_w2, v_ada_w, v_ada_b, v_final_norm, v_final_ada_w, v_final_ada_b)))
    idx = my_index()
    n_batch = N_DEV * B_LOC
    ada_cols = ada_w.shape[2]
    fin_cols = final_ada_w.shape[1]

    small_in, small_in_meta = _pack([c, conv_w], LANES, 8, F32)
    c_parts, conv_w_parts = _unpack(all_gather("gather_c_conv", small_in), small_in_meta, lead=1)
    c_all = c_parts.reshape(n_batch, D)
    conv_w_all = _unshard(conv_w_parts, 2)
    c_act = ew_fwd("c_silu", _f_silu, [c_all], [], [], [D], [F32])[0]
    mod_cols = [matmul(f"ada_proj_{l}", c_act, ada_w[l], "nn") for l in range(DEPTH)]
    mod_cols.append(matmul("ada_proj_final", c_act, final_ada_w, "nn"))
    mod_g = all_gather("gather_mod", jnp.concatenate(mod_cols, axis=1))

    shards = {(l, group): [given[n][l].astype(BF16) for n in members] for l in range(DEPTH) for group, members in GROUPS}
    waves = [[(0, "ffn1")], [(0, "mix")], [(0, "ffn2")] + [(l, group) for l in range(1, DEPTH) for group, _ in GROUPS]]
    gather_handles, landed = {}, {}

    def start_wave(wave, behind, carrier):
        blocks, behind = lax.optimization_barrier(({key: shards[key] for key in wave}, behind))
        for key in wave:
            gather_handles[key], token = gather_start(f"gather_start_{key[1]}{key[0]}", blocks[key])
            carrier = carrier + token[0, 0]
        return behind, carrier

    def weights_of(l, group, after):
        key = (l, group)
        if key not in landed:
            landed[key] = gather_finish(f"gather_finish_{group}{l}", gather_handles[key], after)
        return {n: _unshard(b, GATHERED[n] - 1) for n, b in zip(dict(GROUPS)[group], landed[key])}

    first_blocks, mod_g = lax.optimization_barrier((shards[0, "ffn1"], mod_g))
    landed[0, "ffn1"] = all_gather("gather_first", first_blocks)
    landed[0, "ffn1"], mod_g = start_wave(waves[1], landed[0, "ffn1"], mod_g)

    mods = []
    for l in range(DEPTH):
        full = mod_g[:, :, l * ada_cols:(l + 1) * ada_cols].transpose(1, 0, 2).reshape(n_batch, N_DEV * ada_cols)
        full = ew_fwd(f"ada_bias_{l}", _f_add_bias, [full], [], [ada_b[l][None]], [full.shape[1]], [F32])[0]
        mods.append(lax.dynamic_slice_in_dim(full, idx * B_LOC, B_LOC, axis=0).reshape(B_LOC, 3, 3, D))
    fm = mod_g[:, :, DEPTH * ada_cols:].transpose(1, 0, 2).reshape(n_batch, N_DEV * fin_cols)
    fm = ew_fwd("ada_bias_final", _f_add_bias, [fm], [], [final_ada_b[None]], [fm.shape[1]], [F32])[0]
    fm = lax.dynamic_slice_in_dim(fm, idx * B_LOC, B_LOC, axis=0).reshape(B_LOC, 2, D)

    def mixer_params(l, w):
        wi = w["w_in"]
        cut = IN_CUTS
        return dict(
            gain=mix_norm[l][None], w_rgx=wi[:, cut[0]:cut[1]], w_gate=wi[:, cut[1]:cut[2]], w_sbqkv=wi[:, cut[2]:cut[3]],
            w_foxqkv=wi[:, cut[3]:cut[4]], w_f=_pad_lanes(wi[:, cut[4]:cut[5]]), w_merge=wi[:, cut[5]:cut[6]],
            conv_w8=jnp.pad(conv_w_all[l], ((0, 8 - CONV_K), (0, 0))), conv_b=conv_b[l][None],
            wa_bd=_block_diag(rg_wa[l]), wx_bd=_block_diag(rg_wx[l]), ba=rg_ba[l][None], bx=rg_bx[l][None], lam=rg_lam[l][None],
            bf=_pad_lanes(fox_bf[l][None]), merge_b=merge_b[l][None], w_rg=w["w_rg"], w_sb=w["w_sb"], w_fox=w["w_fox"],
            w_o=w["w_o"])

    n_tok = x.shape[0] * x.shape[1]
    h = x.reshape(n_tok, D)
    saved = []
    for l in range(DEPTH):
        m = mods[l]
        w1 = weights_of(l, "ffn1", m if l == 0 else h)
        h, s1 = _ffn_fwd(f"a{l}", h, _bp(m, 0, 0), _bp(m, 0, 1), _bp(m, 0, 2), ffn1_norm[l][None], w1["ffn1_w1"], w1["ffn1_w3"],
                         w1["ffn1_w2"])
        w2 = weights_of(l, "mix", h)
        if l == 0:
            landed[0, "mix"], m = start_wave(waves[2], landed[0, "mix"], m)
        p = mixer_params(l, w2)
        h, s2 = _mixer_fwd(f"{l}", h, _bp(m, 1, 0), _bp(m, 1, 1), _bp(m, 1, 2), p)
        w3 = weights_of(l, "ffn2", h)
        h, s3 = _ffn_fwd(f"b{l}", h, _bp(m, 2, 0), _bp(m, 2, 1), _bp(m, 2, 2), ffn2_norm[l][None], w3["ffn2_w1"], w3["ffn2_w3"],
                         w3["ffn2_w2"])
        saved.append((s1, s2, s3, p, w1, w3))
    loss_row, dh, dfshift, dfscale, dgain_final = _final_loss(h, loss_target.reshape(n_tok, D), fm[:, 0][:, None, :],
                                                              fm[:, 1][:, None, :], final_norm[None])

    grads = {n: [None] * DEPTH for n in WEIGHTS}
    d_mods = [None] * DEPTH
    scatter_handles = {}
    after_start = jnp.zeros((), F32)

    def scatter_blocks(l, group):
        return [_reshard(grads[n][l], GATHERED[n] - 1).astype(BF16) for n in dict(GROUPS)[group]]

    def start_scatter(l, group, g8s=None):
        g8s = scatter_blocks(l, group) if g8s is None else g8s
        scatter_handles[l, group], token = scatter_start(f"scatter_start_{group}{l}", g8s)
        return token[0, 0]

    for l in reversed(range(DEPTH)):
        m = mods[l]
        s1, s2, s3, p, w1, w3 = saved[l]
        dh, dm3, grads["ffn2_norm"][l], grads["ffn2_w1"][l], grads["ffn2_w3"][l], grads["ffn2_w2"][l] = _ffn_bwd(
            f"b{l}", dh, s3, _bp(m, 2, 0), _bp(m, 2, 1), _bp(m, 2, 2) + after_start, ffn2_norm[l][None], w3["ffn2_w1"],
            w3["ffn2_w3"], w3["ffn2_w2"])
        after_start = start_scatter(l, "ffn2")
        dh, dm2, gm = _mixer_bwd(f"{l}", dh, s2, _bp(m, 1, 0), _bp(m, 1, 1), _bp(m, 1, 2) + after_start, p)
        for n, gval in gm.items():
            grads[n][l] = gval
        after_start = start_scatter(l, "mix")
        dh, dm1, grads["ffn1_norm"][l], grads["ffn1_w1"][l], grads["ffn1_w3"][l], grads["ffn1_w2"][l] = _ffn_bwd(
            f"a{l}", dh, s1, _bp(m, 0, 0), _bp(m, 0, 1), _bp(m, 0, 2) + after_start, ffn1_norm[l][None], w1["ffn1_w1"],
            w1["ffn1_w3"], w1["ffn1_w2"])
        if l > 0:
            after_start = start_scatter(l, "ffn1")
        d_mods[l] = jnp.concatenate([t.reshape(B_LOC, D) for dm in (dm1, dm2, dm3) for t in dm], axis=1)
    grad_x = dh.reshape(x.shape)
    d_fm = jnp.concatenate([dfshift.reshape(B_LOC, D), dfscale.reshape(B_LOC, D)], axis=1)

    rep = {n: jnp.stack([t.reshape(given[n].shape[1:]) for t in grads[n]]) for n in REPLICATED if n != "final_norm"}
    rep["final_norm"] = dgain_final.reshape(D)
    rep["conv_w"] = jnp.stack(grads["conv_w"])
    rep_names = list(rep)
    rep_slab, rep_meta = _pack([rep[n] for n in rep_names], LANES, 8, F32)
    mod_slab, mod_meta = _pack(d_mods + [d_fm], LANES, 8, F32)
    small_g = all_gather("gather_small_grads", jnp.concatenate([mod_slab, rep_slab], axis=0))
    last_blocks, small_g = lax.optimization_barrier((scatter_blocks(0, "ffn1"), small_g))
    small_g = small_g + start_scatter(0, "ffn1", last_blocks)
    d_mod_all = [t.reshape(n_batch, -1) for t in _unpack(small_g[:, :mod_slab.shape[0]], mod_meta, lead=1)]
    rep_sum = add_blocks("sum_small_grads", [small_g[k, mod_slab.shape[0]:] for k in range(N_DEV)], F32)
    rep_grad = dict(zip(rep_names, _unpack(rep_sum, rep_meta)))
    final_g = {n: rep_grad[n] for n in REPLICATED}
    final_g["conv_w"] = lax.dynamic_slice_in_dim(rep_grad["conv_w"], idx * conv_w.shape[2], conv_w.shape[2], axis=2)
    final_g["ada_b"] = jnp.stack([sum_rows(f"ada_b_grad_{l}", d_mod_all[l])[0] for l in range(DEPTH)])
    final_g["final_ada_b"] = sum_rows("final_ada_b_grad", d_mod_all[DEPTH])[0]
    final_g["ada_w"] = jnp.stack([
        matmul(f"ada_w_grad_{l}", c_act, lax.dynamic_slice_in_dim(d_mod_all[l], idx * ada_cols, ada_cols, axis=1), "tn")
        for l in range(DEPTH)])
    final_g["final_ada_w"] = matmul(
        "final_ada_w_grad", c_act, lax.dynamic_slice_in_dim(d_mod_all[DEPTH], idx * fin_cols, fin_cols, axis=1), "tn")

    shard_g = {n: [None] * DEPTH for n in GATHERED}

    def finish_scatter(l, group, after):
        sums = scatter_finish(f"scatter_finish_{group}{l}", scatter_handles[l, group], after)
        for n, gval in zip(dict(GROUPS)[group], sums):
            shard_g[n][l] = gval

    for l in reversed(range(DEPTH)):
        for group in ("ffn2", "mix", "ffn1"):
            if (l, group) != (0, "ffn1"):
                finish_scatter(l, group, rep_sum)

    delta, new_m, new_v = {}, {}, {}
    last = dict(GROUPS)["ffn1"]
    sharded = [n for n in GATHERED if n not in last] + ["ada_w", "final_ada_w", "conv_w"] + list(last)
    for n in sharded:
        if n == last[0]:
            finish_scatter(0, "ffn1", delta["w_in"])
        if n in GATHERED:
            final_g[n] = jnp.stack(shard_g[n])
        delta[n], new_m[n], new_v[n] = adamw(f"adamw_{n}", given[n], final_g[n], given["m_" + n], given["v_" + n])
    for n in WEIGHTS:
        if n not in sharded:
            delta[n], new_m[n], new_v[n] = adamw(f"adamw_{n}", given[n], final_g[n], given["m_" + n], given["v_" + n])

    loss = lax.psum(loss_row[0, 0], ("x", "y", "c"))
    return (loss, grad_x, *[final_g[n] for n in WEIGHTS], *[delta[n] for n in WEIGHTS], *[new_m[n] for n in WEIGHTS],
            *[new_v[n] for n in WEIGHTS])
```

```python
import functools

import numpy as np
import jax
import jax.numpy as jnp
from jax import lax
from jax.experimental import pallas as pl
from jax.experimental.pallas import tpu as pltpu

F32 = jnp.float32
BF16 = jnp.bfloat16
MESH = pl.DeviceIdType.MESH

N_DEV = 8
D = 1024
SEQ = 2048
B_LOC = 2
N_TOK = B_LOC * SEQ
DEPTH = 2
D_FF = 2816
RG_BLOCKS = 16
RG_C = 8.0
N_HEADS = 8
HEAD_DIM = 64
ATT_W = N_HEADS * HEAD_DIM
LANES = 128
EPS = 1e-6
ATT_SCALE = HEAD_DIM ** -0.5
CONV_K = 4

ADAM_LR = 0.001
ADAM_B1 = 0.9
ADAM_B2 = 0.999
ADAM_EPS = 1e-08
ADAM_WD = 0.01
ADAM_STEP = 10

EW_ROWS = 512
EW_ROWS_WIDE = 256
ATT_BLK = 512


def _pick_tile(dim, target):
    best = None
    for t in range(LANES, min(dim, target) + 1, LANES):
        if dim % t == 0:
            best = t
    return best if best is not None else dim


_DIMS = {"nn": (((1,), (0,)), ((), ())), "nt": (((1,), (1,)), ((), ())), "tn": (((0,), (0,)), ((), ()))}


def matmul(name, a_list, b_list, mode, out_dtype=F32, tm=1024, tn=512):
    if not isinstance(a_list, (list, tuple)):
        a_list, b_list = [a_list], [b_list]
    n = len(a_list)
    m_dim = a_list[0].shape[1] if mode == "tn" else a_list[0].shape[0]
    n_dim = b_list[0].shape[0] if mode == "nt" else b_list[0].shape[1]
    tm, tn = _pick_tile(m_dim, tm), _pick_tile(n_dim, tn)
    dims = _DIMS[mode]

    def body(*refs):
        o_ref = refs[-1]
        acc = None
        for a_ref, b_ref in zip(refs[:n], refs[n:2 * n]):
            d = lax.dot_general(a_ref[...].astype(BF16), b_ref[...].astype(BF16), dims, preferred_element_type=F32)
            acc = d if acc is None else acc + d
        o_ref[...] = acc.astype(o_ref.dtype)

    in_specs = []
    for a in a_list:
        if mode == "tn":
            in_specs.append(pl.BlockSpec((a.shape[0], tm), lambda i, j: (0, i)))
        else:
            in_specs.append(pl.BlockSpec((tm, a.shape[1]), lambda i, j: (i, 0)))
    for b in b_list:
        if mode == "nt":
            in_specs.append(pl.BlockSpec((tn, b.shape[1]), lambda i, j: (j, 0)))
        else:
            in_specs.append(pl.BlockSpec((b.shape[0], tn), lambda i, j: (0, j)))
    return pl.pallas_call(
        body, name=name, grid=(m_dim // tm, n_dim // tn), in_specs=in_specs,
        out_specs=pl.BlockSpec((tm, tn), lambda i, j: (i, j)),
        out_shape=jax.ShapeDtypeStruct((m_dim, n_dim), out_dtype),
        compiler_params=pltpu.CompilerParams(dimension_semantics=("parallel", "parallel")),
    )(*a_list, *b_list)


def _row_spec(w, tm):
    return pl.BlockSpec((tm, w), lambda i: (i, 0))


def _bparam_spec(w, tiles_per_batch):
    return pl.BlockSpec((None, 1, w), lambda i: (i // tiles_per_batch, 0, 0))


def _gparam_spec(shape):
    return pl.BlockSpec(shape, lambda i: (0, 0))


def ew_fwd(name, fn, rows, bparams, gparams, out_widths, out_dtypes, tm=EW_ROWS):
    n_rows = rows[0].shape[0]
    tm = min(tm, n_rows, SEQ)
    tpb = max(SEQ // tm, 1)
    nr, nb, ng = len(rows), len(bparams), len(gparams)

    def body(*refs):
        vals = [r[...] for r in refs[:nr + nb + ng]]
        outs = fn(*vals)
        if not isinstance(outs, (tuple, list)):
            outs = (outs,)
        for o_ref, o in zip(refs[nr + nb + ng:], outs):
            o_ref[...] = o.astype(o_ref.dtype)

    in_specs = ([_row_spec(r.shape[1], tm) for r in rows] + [_bparam_spec(p.shape[2], tpb) for p in bparams]
                + [_gparam_spec(g.shape) for g in gparams])
    outs = pl.pallas_call(
        body, name=name, grid=(n_rows // tm,), in_specs=in_specs,
        out_specs=[_row_spec(w, tm) for w in out_widths],
        out_shape=[jax.ShapeDtypeStruct((n_rows, w), dt) for w, dt in zip(out_widths, out_dtypes)],
        compiler_params=pltpu.CompilerParams(dimension_semantics=("parallel",)),
    )(*rows, *bparams, *gparams)
    return outs


def ew_bwd(name, fn, rows, bparams, gparams, cts, row_grad_dtypes, adds=(), tm=EW_ROWS):
    n_rows = rows[0].shape[0]
    tm = min(tm, n_rows, SEQ)
    tpb = max(SEQ // tm, 1)
    nr, nb, ng, nc = len(rows), len(bparams), len(gparams), len(cts)
    adds = list(adds) + [None] * (nr - len(adds))
    add_idx = [k for k in range(nr) if adds[k] is not None]
    want = [k for k in range(nr) if row_grad_dtypes[k] is not None]

    def body(*refs):
        pos = nr + nb + ng
        vals = [r[...] for r in refs[:pos]]
        ct_vals = [r[...].astype(F32) for r in refs[pos:pos + nc]]
        pos += nc
        add_vals = {k: refs[pos + q][...] for q, k in enumerate(add_idx)}
        pos += len(add_idx)
        out_refs = refs[pos:]
        f32_vals = [v.astype(F32) for v in vals]
        outs, vjp = jax.vjp(lambda *a: fn(*a), *f32_vals)
        single = not isinstance(outs, (tuple, list))
        grads = vjp(ct_vals[0].astype(outs.dtype) if single else tuple(c.astype(o.dtype) for c, o in zip(ct_vals, outs)))
        i = pl.program_id(0)
        q = 0
        for k in want:
            g = grads[k]
            if k in add_vals:
                g = g + add_vals[k].astype(F32)
            out_refs[q][...] = g.astype(out_refs[q].dtype)
            q += 1
        for k in range(nb):
            ref = out_refs[q]
            q += 1

            @pl.when(i % tpb == 0)
            def _():
                ref[...] = jnp.zeros_like(ref)

            ref[...] += grads[nr + k]
        for k in range(ng):
            ref = out_refs[q]
            q += 1

            @pl.when(i == 0)
            def _():
                ref[...] = jnp.zeros_like(ref)

            ref[...] += grads[nr + nb + k]

    in_specs = ([_row_spec(r.shape[1], tm) for r in rows] + [_bparam_spec(p.shape[2], tpb) for p in bparams]
                + [_gparam_spec(g.shape) for g in gparams] + [_row_spec(c.shape[1], tm) for c in cts]
                + [_row_spec(adds[k].shape[1], tm) for k in add_idx])
    out_specs = ([_row_spec(rows[k].shape[1], tm) for k in want] + [_bparam_spec(p.shape[2], tpb) for p in bparams]
                 + [_gparam_spec(g.shape) for g in gparams])
    out_shape = ([jax.ShapeDtypeStruct(rows[k].shape, row_grad_dtypes[k]) for k in want]
                 + [jax.ShapeDtypeStruct(p.shape, F32) for p in bparams] + [jax.ShapeDtypeStruct(g.shape, F32) for g in gparams])
    outs = pl.pallas_call(
        body, name=name, grid=(n_rows // tm,), in_specs=in_specs, out_specs=out_specs, out_shape=out_shape,
        compiler_params=pltpu.CompilerParams(dimension_semantics=("arbitrary",)),
    )(*rows, *bparams, *gparams, *cts, *[adds[k] for k in add_idx])
    d_rows = list(outs[:len(want)])
    d_b = list(outs[len(want):len(want) + nb])
    d_g = list(outs[len(want) + nb:])
    return d_rows, d_b, d_g


def f_norm_mod(x, shift, scale, gain):
    x = x.astype(F32)
    y = x * lax.rsqrt(jnp.mean(x * x, axis=-1, keepdims=True) + EPS)
    return (y * gain) * (1.0 + scale) + shift


def f_swiglu(a, b3):
    a = a.astype(F32)
    return (a * jax.nn.sigmoid(a)) * b3.astype(F32)


def f_resid(coef, x, y, gate):
    return x.astype(F32) + (coef * (1.0 + gate)) * y.astype(F32)


def f_rg_gates(pre_r, pre_i, xa, ba, bx, lam):
    r = jax.nn.sigmoid(pre_r + ba)
    i = jax.nn.sigmoid(pre_i + bx)
    softplus_neg_lam = jnp.maximum(-lam, 0.0) + jnp.log(1.0 + jnp.exp(-jnp.abs(lam)))
    log_a = (-RG_C) * r * softplus_neg_lam
    a = jnp.exp(log_a)
    u = jnp.sqrt(1.0 - a * a) * (i * xa)
    return a, u


def f_gelu_mul(gate, hs):
    g = gate.astype(F32)
    gelu = 0.5 * g * (1.0 + jnp.tanh(0.7978845608028654 * (g + 0.044715 * g * g * g)))
    return gelu * hs.astype(F32)


def f_log_sigmoid_bias(f, bf):
    z = f.astype(F32) + bf
    return jnp.minimum(z, 0.0) - jnp.log(1.0 + jnp.exp(-jnp.abs(z)))


def f_merge(mg, pa, pb, pc, merge_b):
    g = jax.nn.sigmoid(mg.astype(F32) + merge_b)
    return g[:, :D] * pa.astype(F32) + g[:, D:2 * D] * pb.astype(F32) + g[:, 2 * D:] * pc.astype(F32)


CONV_CB = 256
SCAN_CB = 512
SCAN_CHAINS = 4
CUM_RB = 512


def _shift_down(x, d):
    if d == 0:
        return x
    rows = lax.broadcasted_iota(jnp.int32, x.shape, 0)
    return jnp.where(rows >= d, pltpu.roll(x, d, axis=0), 0.0)


def _shift_up(x, d):
    if d == 0:
        return x
    s = x.shape[0]
    rows = lax.broadcasted_iota(jnp.int32, x.shape, 0)
    return jnp.where(rows < s - d, pltpu.roll(x, s - d, axis=0), 0.0)


def conv_fwd(x, w8, b):
    n, c = x.shape
    nb = n // SEQ

    def body(x_ref, w_ref, b_ref, y_ref):
        xv = x_ref[...]
        acc = jnp.broadcast_to(b_ref[...], xv.shape)
        for k in range(CONV_K):
            acc = acc + w_ref[k:k + 1, :] * _shift_down(xv, CONV_K - 1 - k)
        y_ref[...] = acc

    return pl.pallas_call(
        body, name="conv_fwd", grid=(c // CONV_CB, nb),
        in_specs=[pl.BlockSpec((SEQ, CONV_CB), lambda j, i: (i, j)), pl.BlockSpec((8, CONV_CB), lambda j, i: (0, j)),
                  pl.BlockSpec((1, CONV_CB), lambda j, i: (0, j))],
        out_specs=pl.BlockSpec((SEQ, CONV_CB), lambda j, i: (i, j)),
        out_shape=jax.ShapeDtypeStruct((n, c), F32),
        compiler_params=pltpu.CompilerParams(dimension_semantics=("parallel", "parallel")),
    )(x, w8, b)


def conv_bwd(x, w8, dy1, dy2):
    n, c = x.shape
    nb = n // SEQ

    def body(x_ref, w_ref, dy1_ref, dy2_ref, dx_ref, dwb_ref):
        xv = x_ref[...]
        dy = dy1_ref[...] + dy2_ref[...]
        dx = jnp.zeros_like(xv)
        parts = []
        for k in range(CONV_K):
            d = CONV_K - 1 - k
            dx = dx + w_ref[k:k + 1, :] * _shift_up(dy, d)
            parts.append(jnp.sum(dy * _shift_down(xv, d), axis=0, keepdims=True))
        parts.append(jnp.sum(dy, axis=0, keepdims=True))
        parts.append(jnp.zeros((8 - len(parts), xv.shape[1]), F32))
        dx_ref[...] = dx.astype(BF16)

        @pl.when(pl.program_id(1) == 0)
        def _():
            dwb_ref[...] = jnp.zeros_like(dwb_ref)

        dwb_ref[...] += jnp.concatenate(parts, axis=0)

    return pl.pallas_call(
        body, name="conv_bwd", grid=(c // CONV_CB, nb),
        in_specs=[pl.BlockSpec((SEQ, CONV_CB), lambda j, i: (i, j)), pl.BlockSpec((8, CONV_CB), lambda j, i: (0, j)),
                  pl.BlockSpec((SEQ, CONV_CB), lambda j, i: (i, j)), pl.BlockSpec((SEQ, CONV_CB), lambda j, i: (i, j))],
        out_specs=[pl.BlockSpec((SEQ, CONV_CB), lambda j, i: (i, j)), pl.BlockSpec((8, CONV_CB), lambda j, i: (0, j))],
        out_shape=[jax.ShapeDtypeStruct((n, c), BF16), jax.ShapeDtypeStruct((8, c), F32)],
        compiler_params=pltpu.CompilerParams(dimension_semantics=("parallel", "arbitrary")),
    )(x, w8, dy1, dy2)


def scan_fwd(a, u):
    n, c = a.shape
    q = SEQ // SCAN_CHAINS

    def body(a_ref, u_ref, h_ref, p_ref):
        def step(t, carry):
            hs, ps = carry
            new_h, new_p = [], []
            for k in range(SCAN_CHAINS):
                row = k * q + t
                av = a_ref[pl.ds(row, 1), :]
                hk = av * hs[k] + u_ref[pl.ds(row, 1), :]
                h_ref[pl.ds(row, 1), :] = hk
                new_h.append(hk)
                pk = av * ps[k]
                if k > 0:
                    p_ref[pl.ds(row, 1), :] = pk
                new_p.append(pk)
            return tuple(new_h), tuple(new_p)

        zero, one = jnp.zeros((1, SCAN_CB), F32), jnp.ones((1, SCAN_CB), F32)
        lax.fori_loop(0, q, step, ((zero,) * SCAN_CHAINS, (one,) * SCAN_CHAINS), unroll=4)
        for k in range(1, SCAN_CHAINS):
            rows = pl.ds(k * q, q)
            h_ref[rows, :] = h_ref[rows, :] + p_ref[rows, :] * h_ref[pl.ds(k * q - 1, 1), :]

    spec = pl.BlockSpec((SEQ, SCAN_CB), lambda i, j: (i, j))
    return pl.pallas_call(
        body, name="scan_fwd", grid=(n // SEQ, c // SCAN_CB), in_specs=[spec, spec], out_specs=spec,
        out_shape=jax.ShapeDtypeStruct((n, c), F32), scratch_shapes=[pltpu.VMEM((SEQ, SCAN_CB), F32)],
        compiler_params=pltpu.CompilerParams(dimension_semantics=("parallel", "parallel")),
    )(a, u)


def scan_bwd(a, h, g):
    n, c = a.shape
    q = SEQ // SCAN_CHAINS
    cb = SCAN_CB // 2

    def body(a_ref, h_ref, g_ref, da_ref, du_ref, r_ref):
        def step(j, carry):
            cs, rs = carry
            new_c, new_r = [], []
            for k in range(SCAN_CHAINS):
                row = k * q + (q - 1 - j)
                dh = g_ref[pl.ds(row, 1), :] + cs[k]
                du_ref[pl.ds(row, 1), :] = dh
                av = a_ref[pl.ds(row, 1), :]
                if k < SCAN_CHAINS - 1:
                    r_ref[pl.ds(row, 1), :] = rs[k]
                new_c.append(av * dh)
                new_r.append(av * rs[k])
            return tuple(new_c), tuple(new_r)

        zero, one = jnp.zeros((1, cb), F32), jnp.ones((1, cb), F32)
        lax.fori_loop(0, q, step, ((zero,) * SCAN_CHAINS, (one,) * SCAN_CHAINS), unroll=4)
        for k in reversed(range(SCAN_CHAINS - 1)):
            rows, nxt = pl.ds(k * q, q), pl.ds((k + 1) * q, 1)
            du_ref[rows, :] = du_ref[rows, :] + r_ref[rows, :] * (a_ref[nxt, :] * du_ref[nxt, :])
        da_ref[...] = du_ref[...] * _shift_down(h_ref[...], 1)

    spec = pl.BlockSpec((SEQ, cb), lambda i, j: (i, j))
    return pl.pallas_call(
        body, name="scan_bwd", grid=(n // SEQ, c // cb), in_specs=[spec, spec, spec], out_specs=[spec, spec],
        out_shape=[jax.ShapeDtypeStruct((n, c), F32), jax.ShapeDtypeStruct((n, c), F32)],
        scratch_shapes=[pltpu.VMEM((SEQ, cb), F32)],
        compiler_params=pltpu.CompilerParams(dimension_semantics=("parallel", "parallel")),
    )(a, h, g)


def _split3_dot(m, x):
    hi = x.astype(BF16)
    r1 = x - hi.astype(F32)
    mid = r1.astype(BF16)
    lo = (r1 - mid.astype(F32)).astype(BF16)
    dot = functools.partial(jnp.dot, preferred_element_type=F32)
    return dot(m, hi) + dot(m, mid) + dot(m, lo)


def seq_cumsum(name, xs, signs, reverse):
    n, w = xs[0].shape
    nx = len(xs)
    rb = min(CUM_RB, SEQ)

    def body(*refs):
        x = None
        for r, sg in zip(refs[:nx], signs):
            x = sg * r[...] if x is None else x + sg * r[...]
        q0 = pl.program_id(1) * rb
        row = q0 + lax.broadcasted_iota(jnp.int32, (rb, SEQ), 0)
        col = lax.broadcasted_iota(jnp.int32, (rb, SEQ), 1)
        tri = ((col >= row) if reverse else (col <= row)).astype(BF16)
        refs[nx][...] = _split3_dot(tri, x)

    return pl.pallas_call(
        body, name=name, grid=(n // SEQ, SEQ // rb),
        in_specs=[pl.BlockSpec((SEQ, w), lambda i, j: (i, 0)) for _ in xs],
        out_specs=pl.BlockSpec((rb, w), lambda i, j: (i * (SEQ // rb) + j, 0)),
        out_shape=jax.ShapeDtypeStruct((n, w), F32),
        compiler_params=pltpu.CompilerParams(dimension_semantics=("parallel", "parallel")),
    )(*xs)


N_PAIRS = N_HEADS // 2


def _dot_nt(a, b):
    return lax.dot_general(a, b, _DIMS["nt"], preferred_element_type=F32)


def _dot_tn(a, b):
    return lax.dot_general(a, b, _DIMS["tn"], preferred_element_type=F32)


def _dot_nn(a, b):
    return lax.dot_general(a, b, _DIMS["nn"], preferred_element_type=F32)


def _split2_dot(x, m):
    hi = x.astype(BF16)
    lo = (x - hi.astype(F32)).astype(BF16)
    return _dot_nn(hi, m) + _dot_nn(lo, m)


def _head_mask(j):
    lane = lax.broadcasted_iota(jnp.int32, (1, LANES), 1)
    return (lane // HEAD_DIM) == j


def _lane_pick(x, h):
    lane = lax.broadcasted_iota(jnp.int32, x.shape, 1)
    return jnp.sum(jnp.where(lane == h, x, 0.0), axis=1, keepdims=True)


def _lane_put(col, h):
    lane = lax.broadcasted_iota(jnp.int32, (col.shape[0], LANES), 1)
    return jnp.where(lane == h, col, 0.0)


def _softplus(z):
    return jnp.maximum(z, 0.0) + jnp.log(1.0 + jnp.exp(-jnp.abs(z)))


def _qkv_specs():
    return [pl.BlockSpec((SEQ, LANES), lambda b, p: (b, p)),
            pl.BlockSpec((SEQ, LANES), lambda b, p: (b, N_PAIRS + p)),
            pl.BlockSpec((SEQ, LANES), lambda b, p: (b, 2 * N_PAIRS + p))]


def _pair_spec():
    return pl.BlockSpec((SEQ, LANES), lambda b, p: (b, p))


def _below_diagonal(strictly):
    t = ATT_BLK
    row = lax.broadcasted_iota(jnp.int32, (t, t), 0)
    col = lax.broadcasted_iota(jnp.int32, (t, t), 1)
    return (row > col) if strictly else (row >= col)


def _over_key_blocks(qi, step, init, reverse):
    t = ATT_BLK
    q0 = pl.multiple_of(qi * t, t)

    def off_diagonal(kk, carry):
        ki = (qi - 1 - kk) if reverse else kk
        return step(pl.multiple_of(ki * t, t), carry, False)

    if reverse:
        return lax.fori_loop(0, qi, off_diagonal, step(q0, init, True))
    return step(q0, lax.fori_loop(0, qi, off_diagonal, init), True)


def _masked_q(qb, j):
    return (jnp.where(_head_mask(j), qb, 0.0) * ATT_SCALE).astype(BF16)


def sb_attn_fwd(qkv):
    n = qkv.shape[0]
    t = ATT_BLK

    def body(q_ref, k_ref, v_ref, o_ref, tot_ref):
        pair = pl.program_id(1)
        strict = _below_diagonal(True)
        later = strict.astype(BF16)

        @pl.when(pair == 0)
        def _():
            tot_ref[...] = jnp.zeros_like(tot_ref)

        def q_block(qi, _):
            q0 = pl.multiple_of(qi * t, t)
            qb = q_ref[pl.ds(q0, t), :]
            qms = [_masked_q(qb, j) for j in range(2)]

            def step(k0, carry, diagonal):
                kb = k_ref[pl.ds(k0, t), :].astype(BF16)
                vb = v_ref[pl.ds(k0, t), :].astype(BF16)
                heads = range(2)
                zs = [_dot_nt(qms[j], kb) for j in heads]
                sps = [_softplus(z) for z in zs]
                log_keeps = [(jnp.where(strict, -sp, 0.0) if diagonal else -sp) for sp in sps]
                right_l = [_split2_dot(lk, later) for lk in log_keeps]
                atts = [jnp.exp((zs[j] - sps[j]) + right_l[j] + carry[j][0]) for j in heads]
                if diagonal:
                    atts = [jnp.where(strict, att, 0.0) for att in atts]
                return tuple((carry[j][0] + jnp.sum(log_keeps[j], axis=1, keepdims=True),
                              carry[j][1] + _dot_nn(atts[j].astype(BF16), vb)) for j in heads)

            init = ((jnp.zeros((t, 1), F32), jnp.zeros((t, LANES), F32)),) * 2
            (tot0, acc0), (tot1, acc1) = _over_key_blocks(qi, step, init, reverse=True)
            o_ref[pl.ds(q0, t), :] = jnp.where(_head_mask(0), acc0, acc1)
            tot_ref[pl.ds(q0, t), :] += _lane_put(tot0, 2 * pair) + _lane_put(tot1, 2 * pair + 1)
            return 0

        lax.fori_loop(0, SEQ // t, q_block, 0)

    batch_spec = pl.BlockSpec((SEQ, LANES), lambda b, p: (b, 0))
    return pl.pallas_call(
        body, name="sb_attn_fwd", grid=(n // SEQ, N_PAIRS), in_specs=_qkv_specs(), out_specs=[_pair_spec(), batch_spec],
        out_shape=[jax.ShapeDtypeStruct((n, ATT_W), F32), jax.ShapeDtypeStruct((n, LANES), F32)],
        compiler_params=pltpu.CompilerParams(dimension_semantics=("parallel", "arbitrary")),
    )(qkv, qkv, qkv)


def sb_attn_bwd(qkv, tot, do):
    n = qkv.shape[0]
    t = ATT_BLK

    def body(q_ref, k_ref, v_ref, tot_ref, do_ref, dq_ref, dk_ref, dv_ref, dk_acc, dv_acc):
        pair = pl.program_id(1)
        strict = _below_diagonal(True)
        upto = jnp.logical_not(strict).astype(BF16)
        dk_acc[...] = jnp.zeros_like(dk_acc)
        dv_acc[...] = jnp.zeros_like(dv_acc)

        def q_block(qi, _):
            q0 = pl.multiple_of(qi * t, t)
            qb = q_ref[pl.ds(q0, t), :]
            tot_q = tot_ref[pl.ds(q0, t), :]
            dob = do_ref[pl.ds(q0, t), :].astype(F32)
            qms = [_masked_q(qb, j) for j in range(2)]
            doms = [jnp.where(_head_mask(j), dob, 0.0).astype(BF16) for j in range(2)]
            totals = [_lane_pick(tot_q, 2 * pair + j) for j in range(2)]

            def step(k0, carry, diagonal):
                kb = k_ref[pl.ds(k0, t), :].astype(BF16)
                vb = v_ref[pl.ds(k0, t), :].astype(BF16)
                heads = range(2)
                zs = [_dot_nt(qms[j], kb) for j in heads]
                d_atts = [_dot_nt(doms[j], vb) for j in heads]
                sps = [_softplus(z) for z in zs]
                log_keeps = [(jnp.where(strict, -sp, 0.0) if diagonal else -sp) for sp in sps]
                log_betas = [z - sp for z, sp in zip(zs, sps)]
                left_l = [_split2_dot(lk, upto) for lk in log_keeps]
                atts = [jnp.exp(log_betas[j] + (totals[j] - (carry[j][0] + left_l[j]))) for j in heads]
                if diagonal:
                    atts = [jnp.where(strict, att, 0.0) for att in atts]
                gs = [att * d_att for att, d_att in zip(atts, d_atts)]
                dv = _dot_tn(atts[0].astype(BF16), doms[0]) + _dot_tn(atts[1].astype(BF16), doms[1])
                left_g = [_dot_nn(g.astype(BF16), upto) for g in gs]
                dzs = [gs[j] - jnp.exp(log_betas[j]) * (carry[j][1] + left_g[j]) for j in heads]
                if diagonal:
                    dzs = [jnp.where(strict, dz, 0.0) for dz in dzs]
                dzs = [dz.astype(BF16) for dz in dzs]
                dk = _dot_tn(dzs[0], qms[0]) + _dot_tn(dzs[1], qms[1])
                dk_acc[pl.ds(k0, t), :] += dk
                dv_acc[pl.ds(k0, t), :] += dv
                return tuple((carry[j][0] + jnp.sum(log_keeps[j], axis=1, keepdims=True),
                              carry[j][1] + jnp.sum(gs[j], axis=1, keepdims=True), carry[j][2] + _dot_nn(dzs[j], kb))
                             for j in heads)

            zero = jnp.zeros((t, 1), F32)
            init = ((zero, zero, jnp.zeros((t, LANES), F32)),) * 2
            (_, _, dq0), (_, _, dq1) = _over_key_blocks(qi, step, init, reverse=False)
            dq_ref[pl.ds(q0, t), :] = (jnp.where(_head_mask(0), dq0, dq1) * ATT_SCALE).astype(BF16)
            return 0

        lax.fori_loop(0, SEQ // t, q_block, 0)
        dk_ref[...] = dk_acc[...].astype(BF16)
        dv_ref[...] = dv_acc[...].astype(BF16)

    out = jax.ShapeDtypeStruct((n, ATT_W), BF16)
    batch_spec = pl.BlockSpec((SEQ, LANES), lambda b, p: (b, 0))
    return pl.pallas_call(
        body, name="sb_attn_bwd", grid=(n // SEQ, N_PAIRS), in_specs=_qkv_specs() + [batch_spec, _pair_spec()],
        out_specs=[_pair_spec()] * 3, out_shape=[out, out, out],
        scratch_shapes=[pltpu.VMEM((SEQ, LANES), F32), pltpu.VMEM((SEQ, LANES), F32)],
        compiler_params=pltpu.CompilerParams(dimension_semantics=("parallel", "parallel")),
    )(qkv, qkv, qkv, tot, do)


NEG_BIG = -1e30


def fox_attn_fwd(qkv, cum, cum_t):
    n = qkv.shape[0]
    t = ATT_BLK

    def body(q_ref, k_ref, v_ref, cum_ref, cumt_ref, o_ref, lse_ref):
        pair = pl.program_id(1)
        causal = _below_diagonal(False)

        @pl.when(pair == 0)
        def _():
            lse_ref[...] = jnp.zeros_like(lse_ref)

        def q_block(qi, _):
            q0 = pl.multiple_of(qi * t, t)
            qb = q_ref[pl.ds(q0, t), :]
            cum_q = cum_ref[pl.ds(q0, t), :]
            qms = [_masked_q(qb, j) for j in range(2)]
            cqs = [_lane_pick(cum_q, 2 * pair + j) for j in range(2)]

            def step(k0, carry, diagonal):
                kb = k_ref[pl.ds(k0, t), :].astype(BF16)
                vb = v_ref[pl.ds(k0, t), :].astype(BF16)
                heads = range(2)
                zs = [_dot_nt(qms[j], kb) + cqs[j] - cumt_ref[pl.ds(2 * pair + j, 1), pl.ds(k0, t)] for j in heads]
                if diagonal:
                    zs = [jnp.where(causal, z, NEG_BIG) for z in zs]
                m_new = [jnp.maximum(carry[j][0], jnp.max(zs[j], axis=1, keepdims=True)) for j in heads]
                ps = [jnp.exp(zs[j] - m_new[j]) for j in heads]
                alphas = [jnp.exp(carry[j][0] - m_new[j]) for j in heads]
                return tuple((m_new[j], alphas[j] * carry[j][1] + jnp.sum(ps[j], axis=1, keepdims=True),
                              alphas[j] * carry[j][2] + _dot_nn(ps[j].astype(BF16), vb)) for j in heads)

            init = ((jnp.full((t, 1), NEG_BIG, F32), jnp.zeros((t, 1), F32), jnp.zeros((t, LANES), F32)),) * 2
            (m0, l0, acc0), (m1, l1, acc1) = _over_key_blocks(qi, step, init, reverse=False)
            o_ref[pl.ds(q0, t), :] = jnp.where(_head_mask(0), acc0 / l0, acc1 / l1)
            lse_ref[pl.ds(q0, t), :] += _lane_put(m0 + jnp.log(l0), 2 * pair) + _lane_put(m1 + jnp.log(l1), 2 * pair + 1)
            return 0

        lax.fori_loop(0, SEQ // t, q_block, 0)

    batch_spec = pl.BlockSpec((SEQ, LANES), lambda b, p: (b, 0))
    return pl.pallas_call(
        body, name="fox_attn_fwd", grid=(n // SEQ, N_PAIRS),
        in_specs=_qkv_specs() + [batch_spec, pl.BlockSpec((None, N_HEADS, SEQ), lambda b, p: (b, 0, 0))],
        out_specs=[_pair_spec(), batch_spec],
        out_shape=[jax.ShapeDtypeStruct((n, ATT_W), F32), jax.ShapeDtypeStruct((n, LANES), F32)],
        compiler_params=pltpu.CompilerParams(dimension_semantics=("parallel", "arbitrary")),
    )(qkv, qkv, qkv, cum, cum_t)


def fox_attn_bwd(qkv, cum, cum_t, lse, o, do):
    n = qkv.shape[0]
    t = ATT_BLK

    def body(q_ref, k_ref, v_ref, cum_ref, cumt_ref, lse_ref, o_ref, do_ref, dq_ref, dk_ref, dv_ref, dcq_ref, dck_ref,
             dk_acc, dv_acc):
        pair = pl.program_id(1)
        causal = _below_diagonal(False)
        dk_acc[...] = jnp.zeros_like(dk_acc)
        dv_acc[...] = jnp.zeros_like(dv_acc)

        @pl.when(pair == 0)
        def _():
            dcq_ref[...] = jnp.zeros_like(dcq_ref)
            dck_ref[...] = jnp.zeros_like(dck_ref)

        def q_block(qi, _):
            q0 = pl.multiple_of(qi * t, t)
            qb = q_ref[pl.ds(q0, t), :]
            ob = o_ref[pl.ds(q0, t), :]
            dob = do_ref[pl.ds(q0, t), :].astype(F32)
            cum_q = cum_ref[pl.ds(q0, t), :]
            lse_q = lse_ref[pl.ds(q0, t), :]
            qms = [_masked_q(qb, j) for j in range(2)]
            dom32 = [jnp.where(_head_mask(j), dob, 0.0) for j in range(2)]
            doms = [d.astype(BF16) for d in dom32]
            deltas = [jnp.sum(d * ob, axis=1, keepdims=True) for d in dom32]
            cqs = [_lane_pick(cum_q, 2 * pair + j) for j in range(2)]
            lqs = [_lane_pick(lse_q, 2 * pair + j) for j in range(2)]

            def step(k0, carry, diagonal):
                kb = k_ref[pl.ds(k0, t), :].astype(BF16)
                vb = v_ref[pl.ds(k0, t), :].astype(BF16)
                heads = range(2)
                zs = [_dot_nt(qms[j], kb) + cqs[j] - cumt_ref[pl.ds(2 * pair + j, 1), pl.ds(k0, t)] for j in heads]
                d_ps = [_dot_nt(doms[j], vb) for j in heads]
                if diagonal:
                    zs = [jnp.where(causal, z, NEG_BIG) for z in zs]
                ps = [jnp.exp(zs[j] - lqs[j]) for j in heads]
                dv_acc[pl.ds(k0, t), :] += _dot_tn(ps[0].astype(BF16), doms[0]) + _dot_tn(ps[1].astype(BF16), doms[1])
                dzs = [ps[j] * (d_ps[j] - deltas[j]) for j in heads]
                dzb = [dz.astype(BF16) for dz in dzs]
                dk_acc[pl.ds(k0, t), :] += _dot_tn(dzb[0], qms[0]) + _dot_tn(dzb[1], qms[1])
                for j in heads:
                    dck_ref[pl.ds(2 * pair + j, 1), pl.ds(k0, t)] += jnp.sum(dzs[j], axis=0, keepdims=True)
                return tuple((carry[j][0] + _dot_nn(dzb[j], kb), carry[j][1] + jnp.sum(dzs[j], axis=1, keepdims=True))
                             for j in heads)

            init = ((jnp.zeros((t, LANES), F32), jnp.zeros((t, 1), F32)),) * 2
            (dq0, dcq0), (dq1, dcq1) = _over_key_blocks(qi, step, init, reverse=False)
            dq_ref[pl.ds(q0, t), :] = (jnp.where(_head_mask(0), dq0, dq1) * ATT_SCALE).astype(BF16)
            dcq_ref[pl.ds(q0, t), :] += _lane_put(dcq0, 2 * pair) + _lane_put(dcq1, 2 * pair + 1)
            return 0

        lax.fori_loop(0, SEQ // t, q_block, 0)
        dk_ref[...] = dk_acc[...].astype(BF16)
        dv_ref[...] = dv_acc[...].astype(BF16)

    batch_spec = pl.BlockSpec((SEQ, LANES), lambda b, p: (b, 0))
    t_spec = pl.BlockSpec((None, N_HEADS, SEQ), lambda b, p: (b, 0, 0))
    out = jax.ShapeDtypeStruct((n, ATT_W), BF16)
    return pl.pallas_call(
        body, name="fox_attn_bwd", grid=(n // SEQ, N_PAIRS),
        in_specs=_qkv_specs() + [batch_spec, t_spec, batch_spec, _pair_spec(), _pair_spec()],
        out_specs=[_pair_spec()] * 3 + [batch_spec, t_spec],
        scratch_shapes=[pltpu.VMEM((SEQ, LANES), F32), pltpu.VMEM((SEQ, LANES), F32)],
        out_shape=[out, out, out, jax.ShapeDtypeStruct((n, LANES), F32), jax.ShapeDtypeStruct((n // SEQ, N_HEADS, SEQ), F32)],
        compiler_params=pltpu.CompilerParams(dimension_semantics=("parallel", "arbitrary")),
    )(qkv, qkv, qkv, cum, cum_t, lse, o, do)


_HBM = pl.BlockSpec(memory_space=pl.ANY)


def _my_place():
    return lax.axis_index("x"), lax.axis_index("y"), lax.axis_index("c")


def my_index():
    mx, my, mc = _my_place()
    return 4 * mx + 2 * my + mc


def all_gather(name, xs):
    single = not isinstance(xs, (list, tuple))
    xs = [xs] if single else list(xs)
    na = len(xs)

    def body(*refs):
        x_refs, out_refs = refs[:na], refs[na:2 * na]
        send_sems, recv_sems, local_sems = refs[2 * na:]
        mx, my, mc = _my_place()
        me, sibling = (mx, my, mc), (mx, my, 1 - mc)
        chips = [(1 - mx, my), (mx, 1 - my), (1 - mx, 1 - my)]

        def slot(a, px, py, pc):
            return out_refs[a].at[4 * px + 2 * py + pc]

        def copy(a, k, block, to, src=None):
            return pltpu.make_async_remote_copy(
                src_ref=slot(a, *block) if src is None else src, dst_ref=slot(a, *block),
                send_sem=send_sems.at[7 * a + k], recv_sem=recv_sems.at[7 * a + k], device_id=to, device_id_type=MESH)

        mine = [pltpu.make_async_copy(x_refs[a], slot(a, *me), local_sems.at[a]) for a in range(na)]
        for cp in mine:
            cp.start()
        first = []
        for j, chip in enumerate(chips):
            first += [copy(a, 1 + j, me, (*chip, mc), src=x_refs[a]) for a in range(na)]
        first += [copy(a, 0, me, sibling, src=x_refs[a]) for a in range(na)]
        for cp in first:
            cp.start()
        passed = []
        for j, chip in enumerate(chips):
            for a in range(na):
                copy(a, 1 + j, (*chip, mc), me).wait_recv()
                passed.append(copy(a, 4 + j, (*chip, mc), sibling))
                passed[-1].start()
        for a in range(na):
            copy(a, 0, sibling, me).wait_recv()
        for j, chip in enumerate(chips):
            for a in range(na):
                copy(a, 4 + j, (*chip, 1 - mc), me).wait_recv()
        for cp in first + passed:
            cp.wait_send()
        for cp in mine:
            cp.wait()

    outs = pl.pallas_call(
        body, name=name, in_specs=[_HBM] * na, out_specs=[_HBM] * na,
        out_shape=[jax.ShapeDtypeStruct((N_DEV,) + x.shape, x.dtype) for x in xs],
        scratch_shapes=[pltpu.SemaphoreType.DMA((7 * na,)), pltpu.SemaphoreType.DMA((7 * na,)), pltpu.SemaphoreType.DMA((na,))],
    )(*xs)
    return outs[0] if single else list(outs)


_SEM = pl.BlockSpec(memory_space=pltpu.SEMAPHORE)
_HBM_ONLY = pl.BlockSpec(memory_space=pltpu.HBM)
_EFFECT = pltpu.SideEffectType.DATAFLOW_SIDE_EFFECTING
N_PEERS = N_DEV


def _peers():
    mx, my, mc = _my_place()
    return [((1 - mx) if (r >> 2) & 1 else mx, (1 - my) if (r >> 1) & 1 else my, (1 - mc) if r & 1 else mc)
            for r in range(N_DEV)]


def _exchange_copies(scatter, x_refs, land_refs, send_sems, recv_sems):
    me = my_index()
    copies = []
    for a, (x_ref, land_ref) in enumerate(zip(x_refs, land_refs)):
        for r, (px, py, pc) in enumerate(_peers()):
            src = x_ref.at[4 * px + 2 * py + pc] if scatter else x_ref
            dst = land_ref.at[r] if scatter else land_ref.at[me]
            copies.append(pltpu.make_async_remote_copy(
                src_ref=src, dst_ref=dst, send_sem=send_sems.at[N_PEERS * a + r], recv_sem=recv_sems.at[N_PEERS * a + r],
                device_id=(px, py, pc), device_id_type=MESH))
    return copies


def exchange_start(name, xs, scatter):
    na = len(xs)
    lands = [lax.empty((N_PEERS,) + x.shape[1:] if scatter else (N_DEV,) + x.shape, x.dtype) for x in xs]

    def body(*refs):
        x_refs, land_refs, send_sems, recv_sems = refs[:na], refs[na:2 * na], refs[2 * na], refs[2 * na + 1]
        token = refs[-1]
        for cp in _exchange_copies(scatter, x_refs, land_refs, send_sems, recv_sems):
            cp.start()
        token[...] = jnp.zeros_like(token)

    outs = pl.pallas_call(
        body, name=name,
        out_shape=(pltpu.SemaphoreType.DMA((N_PEERS * na,)), pltpu.SemaphoreType.DMA((N_PEERS * na,)),
                   *[pltpu.HBM(x.shape, x.dtype) for x in xs], *[pltpu.HBM(l.shape, l.dtype) for l in lands],
                   jax.ShapeDtypeStruct((8, LANES), F32)),
        in_specs=[_HBM_ONLY] * (2 * na),
        out_specs=(_SEM, _SEM, *[_HBM_ONLY] * (2 * na), pl.BlockSpec(memory_space=pltpu.VMEM)),
        input_output_aliases={i: 2 + i for i in range(2 * na)},
        compiler_params=pltpu.CompilerParams(has_side_effects=_EFFECT),
    )(*[pltpu.with_memory_space_constraint(x, pltpu.HBM) for x in xs],
      *[pltpu.with_memory_space_constraint(l, pltpu.HBM) for l in lands])
    return (scatter, outs[0], outs[1], outs[2:2 + na], outs[2 + na:2 + 2 * na]), outs[-1]


def exchange_finish(name, handle, after):
    scatter, send_sems, recv_sems, xs, lands = handle
    na = len(xs)

    def body(*refs):
        x_refs, land_refs, send_ref, recv_ref = refs[:na], refs[na:2 * na], refs[2 * na], refs[2 * na + 1]
        for cp in _exchange_copies(scatter, x_refs, land_refs, send_ref, recv_ref):
            cp.wait_send()
            cp.wait_recv()

    outs = pl.pallas_call(
        body, name=name,
        out_shape=tuple(pltpu.HBM(t.shape, t.dtype) for t in list(xs) + list(lands)),
        in_specs=[_HBM_ONLY] * (2 * na) + [_SEM, _SEM, _HBM],
        out_specs=tuple([_HBM_ONLY] * (2 * na)),
        input_output_aliases={i: i for i in range(2 * na)},
        compiler_params=pltpu.CompilerParams(has_side_effects=_EFFECT),
    )(*xs, *lands, send_sems, recv_sems, after)
    return list(outs[:na]), list(outs[na:])


def _pick_rows(n, target):
    best = None
    for t in range(8, min(n, target) + 1, 8):
        if n % t == 0:
            best = t
    return best if best is not None else n


def add_blocks(name, parts, out_dtype, rows=512):
    r, w = parts[0].shape
    tr = _pick_rows(r, rows)

    def body(*refs):
        acc = refs[0][...].astype(F32)
        for ref in refs[1:-1]:
            acc = acc + ref[...].astype(F32)
        refs[-1][...] = acc.astype(refs[-1].dtype)

    spec = pl.BlockSpec((tr, w), lambda i: (i, 0))
    return pl.pallas_call(
        body, name=name, grid=(r // tr,), in_specs=[spec] * len(parts), out_specs=spec,
        out_shape=jax.ShapeDtypeStruct((r, w), out_dtype),
        compiler_params=pltpu.CompilerParams(dimension_semantics=("parallel",)),
    )(*parts)


def sum_rows(name, x):
    def body(x_ref, o_ref):
        o_ref[...] = jnp.sum(x_ref[...], axis=0, keepdims=True)

    return pl.pallas_call(body, name=name, out_shape=jax.ShapeDtypeStruct((1, x.shape[1]), F32))(x)


def gather_start(name, blocks):
    return exchange_start(name, blocks, scatter=False)


def gather_finish(name, handle, after):
    return exchange_finish(name, handle, after)[1]


def scatter_start(name, g8s):
    return exchange_start(name, g8s, scatter=True)


def scatter_finish(name, handle, after):
    _, lands = exchange_finish(name, handle, after)
    outs = []
    for a, land in enumerate(lands):
        w = land.shape[-1]
        outs.append(add_blocks(f"{name}_sum{a}", [land[k].reshape(-1, w) for k in range(N_PEERS)], F32).reshape(land.shape[1:]))
    return outs


def _pack(arrays, width, row_mult, dtype, lead=0):
    parts, metas = [], []
    for a in arrays:
        lead_shape = a.shape[:lead]
        size = int(np.prod(a.shape[lead:]))
        chunk = row_mult * width
        padded = -(-size // chunk) * chunk
        flat = a.astype(dtype).reshape(lead_shape + (size,))
        if padded != size:
            flat = jnp.pad(flat, [(0, 0)] * lead + [(0, padded - size)])
        parts.append(flat.reshape(lead_shape + (padded // width, width)))
        metas.append((a.shape[lead:], size, padded // width))
    return jnp.concatenate(parts, axis=lead), metas


def _unpack(slab, metas, lead=0):
    out, r0 = [], 0
    for shape, size, rows in metas:
        part = lax.slice_in_dim(slab, r0, r0 + rows, axis=lead)
        lead_shape = part.shape[:lead]
        flat = part.reshape(lead_shape + (rows * part.shape[-1],))
        out.append(lax.slice_in_dim(flat, 0, size, axis=lead).reshape(lead_shape + tuple(shape)))
        r0 += rows
    return out


def _f_adamw(w, g, m, v):
    m = ADAM_B1 * m + (1.0 - ADAM_B1) * g
    v = ADAM_B2 * v + (1.0 - ADAM_B2) * (g * g)
    m_hat = m / (1.0 - ADAM_B1 ** ADAM_STEP)
    v_hat = v / (1.0 - ADAM_B2 ** ADAM_STEP)
    delta = (-ADAM_LR) * (m_hat / (jnp.sqrt(v_hat) + ADAM_EPS) + ADAM_WD * w)
    return delta, m, v


def adamw(name, w, g, m, v):
    shape = w.shape
    w2 = shape[-1]
    flat = [a.reshape(-1, w2) for a in (w, g, m, v)]
    tm = _pick_rows(flat[0].shape[0], 256)
    outs = ew_fwd(name, _f_adamw, flat, [], [], [w2] * 3, [F32] * 3, tm=tm)
    return [o.reshape(shape) for o in outs]


WEIGHTS = ["ffn1_norm", "ffn1_w1", "ffn1_w3", "ffn1_w2", "mix_norm", "w_in", "conv_w", "conv_b", "rg_wa", "rg_ba", "rg_wx",
           "rg_bx", "rg_lam", "fox_bf", "merge_b", "w_rg", "w_sb", "w_fox", "w_o", "ffn2_norm", "ffn2_w1", "ffn2_w3",
           "ffn2_w2", "ada_w", "ada_b", "final_norm", "final_ada_w", "final_ada_b"]
GATHERED = {"ffn1_w1": 2, "ffn1_w3": 2, "ffn1_w2": 1, "w_in": 2, "w_rg": 1, "w_sb": 2, "w_fox": 2, "w_o": 1,
            "ffn2_w1": 2, "ffn2_w3": 2, "ffn2_w2": 1}
REPLICATED = ["ffn1_norm", "mix_norm", "conv_b", "rg_wa", "rg_ba", "rg_wx", "rg_bx", "rg_lam", "fox_bf", "merge_b",
              "ffn2_norm", "final_norm"]
GROUPS = (("ffn1", ("ffn1_w1", "ffn1_w3", "ffn1_w2")), ("mix", ("w_in", "w_rg", "w_sb", "w_fox", "w_o")),
          ("ffn2", ("ffn2_w1", "ffn2_w3", "ffn2_w2")))
IN_CUTS = (0, 1024, 2048, 3584, 5120, 5128, 8200)


def _unshard(g, axis):
    g = jnp.moveaxis(g, 0, axis)
    shape = g.shape
    return g.reshape(shape[:axis] + (shape[axis] * shape[axis + 1],) + shape[axis + 2:])


def _reshard(full, axis):
    shape = full.shape
    g = full.reshape(shape[:axis] + (N_DEV, shape[axis] // N_DEV) + shape[axis + 1:])
    return jnp.moveaxis(g, axis, 0)


def _pair_blocks(w):
    nb, bd, _ = w.shape
    w2 = w.reshape(nb // 2, 2, bd, bd)
    eye = jnp.eye(2, dtype=bool)[None, :, None, :, None]
    return jnp.where(eye, w2[:, :, :, None, :], 0.0).reshape(nb // 2, 2 * bd, 2 * bd)


def _unpair_blocks(g):
    n, s, _ = g.shape
    g5 = g.reshape(n, 2, s // 2, 2, s // 2)
    return jnp.stack([g5[:, k, :, k, :] for k in range(2)], axis=1).reshape(2 * n, s // 2, s // 2)


def group_matmul(name, a_list, w_list, mode, out_dtype=F32, tm=1024):
    n = len(a_list)
    m, c = a_list[0].shape
    tm = _pick_tile(m, tm)
    dims = _DIMS[mode]

    def body(*refs):
        acc = None
        for a_ref, w_ref in zip(refs[:n], refs[n:2 * n]):
            d = lax.dot_general(a_ref[...].astype(BF16), w_ref[...].astype(BF16), dims, preferred_element_type=F32)
            acc = d if acc is None else acc + d
        refs[-1][...] = acc.astype(refs[-1].dtype)

    a_spec = pl.BlockSpec((tm, LANES), lambda i, g: (i, g))
    w_spec = pl.BlockSpec((None, LANES, LANES), lambda i, g: (g, 0, 0))
    return pl.pallas_call(
        body, name=name, grid=(m // tm, c // LANES), in_specs=[a_spec] * n + [w_spec] * n, out_specs=a_spec,
        out_shape=jax.ShapeDtypeStruct((m, c), out_dtype),
        compiler_params=pltpu.CompilerParams(dimension_semantics=("parallel", "parallel")),
    )(*a_list, *w_list)


def group_matmul_grad(name, a, b):
    m, c = a.shape

    def body(a_ref, b_ref, o_ref):
        o_ref[...] = _dot_tn(a_ref[...].astype(BF16), b_ref[...].astype(BF16))

    spec = pl.BlockSpec((m, LANES), lambda g: (0, g))
    return pl.pallas_call(
        body, name=name, grid=(c // LANES,), in_specs=[spec, spec],
        out_specs=pl.BlockSpec((None, LANES, LANES), lambda g: (g, 0, 0)),
        out_shape=jax.ShapeDtypeStruct((c // LANES, LANES, LANES), F32),
        compiler_params=pltpu.CompilerParams(dimension_semantics=("parallel",)),
    )(a, b)


def _pad_lanes(a, width=LANES):
    return jnp.pad(a, [(0, 0)] * (a.ndim - 1) + [(0, width - a.shape[-1])])


def _bp(m, k, which):
    return m[:, k, which][:, None, :]


def _f_silu(c):
    return c * jax.nn.sigmoid(c)


def _f_add_bias(a, b):
    return a + b


FFN_TM = 512
FFN_TN = 1408
FFN_SUB = 256


def ffn_up(name, h, w1, w3):
    n, k = h.shape
    f = w1.shape[1]
    tm, tn = min(FFN_TM, n), _pick_tile(f, FFN_TN)

    def body(h_ref, w1_ref, w3_ref, a_ref, b_ref, s_ref):
        subs = [pl.ds(r, FFN_SUB) for r in range(0, tm, FFN_SUB)] if tm % FFN_SUB == 0 else [pl.ds(0, tm)]
        hs = [h_ref[rows, :] for rows in subs]
        a_s = [jnp.dot(hv, w1_ref[...], preferred_element_type=F32) for hv in hs]
        b_s = [jnp.dot(hv, w3_ref[...], preferred_element_type=F32) for hv in hs]
        for rows, a, b in zip(subs, a_s, b_s):
            a_ref[rows, :] = a.astype(BF16)
            b_ref[rows, :] = b.astype(BF16)
            s_ref[rows, :] = ((a * jax.nn.sigmoid(a)) * b).astype(BF16)

    wspec = pl.BlockSpec((k, tn), lambda i, j: (0, j))
    ospec = pl.BlockSpec((tm, tn), lambda i, j: (i, j))
    out = jax.ShapeDtypeStruct((n, f), BF16)
    return pl.pallas_call(
        body, name=name, grid=(n // tm, f // tn), in_specs=[pl.BlockSpec((tm, k), lambda i, j: (i, 0)), wspec, wspec],
        out_specs=[ospec] * 3, out_shape=[out] * 3,
        compiler_params=pltpu.CompilerParams(dimension_semantics=("parallel", "parallel")),
    )(h, w1, w3)


def ffn_down_dx(name, dy, w2, a, b):
    n, k = dy.shape
    f = w2.shape[0]
    tm, tn = min(FFN_TM, n), _pick_tile(f, FFN_TN)

    def body(dy_ref, w2_ref, a_ref, b_ref, da_ref, db_ref):
        subs = [pl.ds(r, FFN_SUB) for r in range(0, tm, FFN_SUB)] if tm % FFN_SUB == 0 else [pl.ds(0, tm)]
        ds_s = [_dot_nt(dy_ref[rows, :], w2_ref[...]) for rows in subs]
        for rows, ds in zip(subs, ds_s):
            av = a_ref[rows, :].astype(F32)
            sig = jax.nn.sigmoid(av)
            da_ref[rows, :] = (ds * b_ref[rows, :].astype(F32) * (sig * (1.0 + av * (1.0 - sig)))).astype(BF16)
            db_ref[rows, :] = (ds * (av * sig)).astype(BF16)

    ospec = pl.BlockSpec((tm, tn), lambda i, j: (i, j))
    out = jax.ShapeDtypeStruct((n, f), BF16)
    return pl.pallas_call(
        body, name=name, grid=(n // tm, f // tn),
        in_specs=[pl.BlockSpec((tm, k), lambda i, j: (i, 0)), pl.BlockSpec((tn, k), lambda i, j: (j, 0)), ospec, ospec],
        out_specs=[ospec] * 2, out_shape=[out] * 2,
        compiler_params=pltpu.CompilerParams(dimension_semantics=("parallel", "parallel")),
    )(dy, w2, a, b)


def _ffn_fwd(tag, x, shift, scale, gate, gain, w1, w3, w2):
    h = ew_fwd(f"ffn_norm_{tag}", f_norm_mod, [x], [shift, scale], [gain], [D], [BF16])[0]
    a, b3, s = ffn_up(f"ffn_up_{tag}", h, w1, w3)
    y = matmul(f"ffn_down_{tag}", s, w2, "nn", tm=1024)
    xo = ew_fwd(f"ffn_res_{tag}", functools.partial(f_resid, 0.5), [x, y], [gate], [], [D], [F32])[0]
    return xo, (x, h, a, b3, s, y)


def _ffn_bwd(tag, dxo, saved, shift, scale, gate, gain, w1, w3, w2):
    x, h, a, b3, s, y = saved
    (dy,), (dgate,), _ = ew_bwd(f"ffn_res_bwd_{tag}", functools.partial(f_resid, 0.5), [x, y], [gate], [], [dxo], [None, BF16])
    da, db3 = ffn_down_dx(f"ffn_down_dx_{tag}", dy, w2, a, b3)
    dw2 = matmul(f"ffn_dw2_{tag}", s, dy, "tn", tm=1408, tn=256, out_dtype=BF16)
    dw1 = matmul(f"ffn_dw1_{tag}", h, da, "tn", tm=1024, tn=256, out_dtype=BF16)
    dw3 = matmul(f"ffn_dw3_{tag}", h, db3, "tn", tm=1024, tn=256, out_dtype=BF16)
    dh = matmul(f"ffn_up_dx_{tag}", [da, db3], [w1, w3], "nt", tm=1024)
    (dx,), (dshift, dscale), (dgain,) = ew_bwd(f"ffn_norm_bwd_{tag}", f_norm_mod, [x], [shift, scale], [gain], [dh], [F32],
                                               adds=[dxo])
    return dx, (dshift, dscale, dgate), dgain, dw1, dw3, dw2


def _mixer_fwd(tag, x, shift, scale, gate, p):
    h = ew_fwd(f"mix_norm_{tag}", f_norm_mod, [x], [shift, scale], [p["gain"]], [D], [BF16])[0]
    rgx = matmul(f"in_rgx_{tag}", h, p["w_rgx"], "nn")
    rgate = matmul(f"in_gate_{tag}", h, p["w_gate"], "nn", out_dtype=BF16)
    sbqkv = matmul(f"in_sb_{tag}", h, p["w_sbqkv"], "nn", out_dtype=BF16)
    foxqkv = matmul(f"in_fox_{tag}", h, p["w_foxqkv"], "nn", out_dtype=BF16)
    ff = matmul(f"in_forget_{tag}", h, p["w_f"], "nn")
    mg = matmul(f"in_merge_{tag}", h, p["w_merge"], "nn", out_dtype=BF16)
    xa = conv_fwd(rgx, p["conv_w8"], p["conv_b"])
    pre_r = group_matmul(f"rg_a_{tag}", [xa], [p["wa_p"]], "nn", out_dtype=BF16)
    pre_i = group_matmul(f"rg_x_{tag}", [xa], [p["wx_p"]], "nn", out_dtype=BF16)
    a, u = ew_fwd(f"rg_gates_{tag}", f_rg_gates, [pre_r, pre_i, xa], [], [p["ba"], p["bx"], p["lam"]], [D, D], [F32, F32],
                  tm=EW_ROWS_WIDE)
    hs = scan_fwd(a, u)
    ya = ew_fwd(f"rg_out_{tag}", f_gelu_mul, [rgate, hs], [], [], [D], [BF16])[0]
    yb, sb_tot = sb_attn_fwd(sbqkv)
    lf = ew_fwd(f"fox_logf_{tag}", f_log_sigmoid_bias, [ff], [], [p["bf"]], [LANES], [F32])[0]
    cum = seq_cumsum(f"fox_cum_{tag}", [lf], [1.0], False)
    cum_t = cum.reshape(-1, SEQ, LANES)[:, :, :N_HEADS].transpose(0, 2, 1)
    yc, lse = fox_attn_fwd(foxqkv, cum, cum_t)
    pa = matmul(f"out_rg_{tag}", ya, p["w_rg"], "nn", out_dtype=BF16)
    pb = matmul(f"out_sb_{tag}", yb, p["w_sb"], "nn", out_dtype=BF16)
    pc = matmul(f"out_fox_{tag}", yc, p["w_fox"], "nn", out_dtype=BF16)
    mixed = ew_fwd(f"merge_{tag}", f_merge, [mg, pa, pb, pc], [], [p["merge_b"]], [D], [BF16], tm=EW_ROWS_WIDE)[0]
    y = matmul(f"out_o_{tag}", mixed, p["w_o"], "nn")
    xo = ew_fwd(f"mix_res_{tag}", functools.partial(f_resid, 1.0), [x, y], [gate], [], [D], [F32])[0]
    saved = dict(x=x, h=h, rgx=rgx, rgate=rgate, sbqkv=sbqkv, foxqkv=foxqkv, ff=ff, mg=mg, xa=xa, pre_r=pre_r, pre_i=pre_i,
                 a=a, hs=hs, ya=ya, yb=yb, sb_tot=sb_tot, cum=cum, cum_t=cum_t, yc=yc, lse=lse, pa=pa, pb=pb, pc=pc,
                 mixed=mixed, y=y)
    return xo, saved


def _mixer_bwd(tag, dxo, s, shift, scale, gate, p):
    (dy,), (dgate,), _ = ew_bwd(f"mix_res_bwd_{tag}", functools.partial(f_resid, 1.0), [s["x"], s["y"]], [gate], [], [dxo],
                                [None, BF16])
    dmixed = matmul(f"out_o_dx_{tag}", dy, p["w_o"], "nt")
    g = {"w_o": matmul(f"out_o_dw_{tag}", s["mixed"], dy, "tn", tm=1024, tn=256, out_dtype=BF16)}
    (dmg, dpa, dpb, dpc), _, (g["merge_b"],) = ew_bwd(
        f"merge_bwd_{tag}", f_merge, [s["mg"], s["pa"], s["pb"], s["pc"]], [], [p["merge_b"]], [dmixed], [BF16] * 4,
        tm=EW_ROWS_WIDE)
    dya = matmul(f"out_rg_dx_{tag}", dpa, p["w_rg"], "nt")
    g["w_rg"] = matmul(f"out_rg_dw_{tag}", s["ya"], dpa, "tn", tm=1024, tn=256, out_dtype=BF16)
    dyb = matmul(f"out_sb_dx_{tag}", dpb, p["w_sb"], "nt", out_dtype=BF16)
    g["w_sb"] = matmul(f"out_sb_dw_{tag}", s["yb"], dpb, "tn", tm=1024, tn=256, out_dtype=BF16)
    dyc = matmul(f"out_fox_dx_{tag}", dpc, p["w_fox"], "nt", out_dtype=BF16)
    g["w_fox"] = matmul(f"out_fox_dw_{tag}", s["yc"], dpc, "tn", tm=1024, tn=256, out_dtype=BF16)
    dq_c, dk_c, dv_c, dcq, dck = fox_attn_bwd(s["foxqkv"], s["cum"], s["cum_t"], s["lse"], s["yc"], dyc)
    dck_rows = _pad_lanes(dck.transpose(0, 2, 1).reshape(-1, N_HEADS))
    dlf = seq_cumsum(f"fox_cum_bwd_{tag}", [dcq, dck_rows], [1.0, -1.0], True)
    (dff,), _, (dbf,) = ew_bwd(f"fox_logf_bwd_{tag}", f_log_sigmoid_bias, [s["ff"]], [], [p["bf"]], [dlf], [BF16])
    g["fox_bf"] = dbf[0, :N_HEADS]
    dq_b, dk_b, dv_b = sb_attn_bwd(s["sbqkv"], s["sb_tot"], dyb)
    (drgate, dhs), _, _ = ew_bwd(f"rg_out_bwd_{tag}", f_gelu_mul, [s["rgate"], s["hs"]], [], [], [dya], [BF16, F32],
                                 tm=EW_ROWS_WIDE)
    da, du = scan_bwd(s["a"], s["hs"], dhs)
    (dpre_r, dpre_i, dxa1), _, (g["rg_ba"], g["rg_bx"], g["rg_lam"]) = ew_bwd(
        f"rg_gates_bwd_{tag}", f_rg_gates, [s["pre_r"], s["pre_i"], s["xa"]], [], [p["ba"], p["bx"], p["lam"]], [da, du],
        [BF16, BF16, F32], tm=EW_ROWS_WIDE)
    dxa2 = group_matmul(f"rg_dx_{tag}", [dpre_r, dpre_i], [p["wa_p"], p["wx_p"]], "nt")
    g["rg_wa"] = _unpair_blocks(group_matmul_grad(f"rg_a_dw_{tag}", s["xa"], dpre_r))
    g["rg_wx"] = _unpair_blocks(group_matmul_grad(f"rg_x_dw_{tag}", s["xa"], dpre_i))
    drgx, dwb = conv_bwd(s["rgx"], p["conv_w8"], dxa1, dxa2)
    g["conv_w"] = dwb[:CONV_K]
    g["conv_b"] = dwb[CONV_K]
    cots = [drgx, drgate, dq_b, dk_b, dv_b, dq_c, dk_c, dv_c, dff, dmg]
    w_sb3 = [p["w_sbqkv"][:, k * ATT_W:(k + 1) * ATT_W] for k in range(3)]
    w_fox3 = [p["w_foxqkv"][:, k * ATT_W:(k + 1) * ATT_W] for k in range(3)]
    ws = [p["w_rgx"], p["w_gate"]] + w_sb3 + w_fox3 + [p["w_f"], p["w_merge"]]
    dh = matmul(f"in_dx_{tag}", cots, ws, "nt", tm=512)
    dws = [matmul(f"in_dw{k}_{tag}", s["h"], ct, "tn", tm=1024, tn=256, out_dtype=BF16) for k, ct in enumerate(cots)]
    dws[8] = dws[8][:, :N_HEADS]
    g["w_in"] = jnp.concatenate(dws, axis=1)
    (dx,), (dshift, dscale), (g["mix_norm"],) = ew_bwd(f"mix_norm_bwd_{tag}", f_norm_mod, [s["x"]], [shift, scale], [p["gain"]],
                                                       [dh], [F32], adds=[dxo])
    return dx, (dshift, dscale, dgate), g


def _final_loss(x, target, shift, scale, gain):
    n = x.shape[0]
    tm = min(EW_ROWS, SEQ)
    tpb = SEQ // tm

    def body(x_ref, t_ref, sh_ref, sc_ref, g_ref, loss_ref, dx_ref, dsh_ref, dsc_ref, dg_ref):
        i = pl.program_id(0)
        out, vjp = jax.vjp(f_norm_mod, x_ref[...], sh_ref[...], sc_ref[...], g_ref[...])
        diff = out - t_ref[...]
        dx, dsh, dsc, dg = vjp(diff * (1.0 / D))
        dx_ref[...] = dx
        sq = jnp.sum(jnp.sum(diff * diff, axis=1, keepdims=True), axis=0, keepdims=True)

        @pl.when(i % tpb == 0)
        def _():
            dsh_ref[...] = jnp.zeros_like(dsh_ref)
            dsc_ref[...] = jnp.zeros_like(dsc_ref)

        @pl.when(i == 0)
        def _():
            dg_ref[...] = jnp.zeros_like(dg_ref)
            loss_ref[...] = jnp.zeros_like(loss_ref)

        dsh_ref[...] += dsh
        dsc_ref[...] += dsc
        dg_ref[...] += dg
        loss_ref[...] += jnp.broadcast_to(sq, (1, LANES)) * (0.5 / D)

    row, bp, gp = _row_spec(D, tm), _bparam_spec(D, tpb), _gparam_spec((1, D))
    return pl.pallas_call(
        body, name="final_loss", grid=(n // tm,), in_specs=[row, row, bp, bp, gp],
        out_specs=[_gparam_spec((1, LANES)), row, bp, bp, gp],
        out_shape=[jax.ShapeDtypeStruct((1, LANES), F32), jax.ShapeDtypeStruct((n, D), F32),
                   jax.ShapeDtypeStruct(shift.shape, F32), jax.ShapeDtypeStruct(scale.shape, F32),
                   jax.ShapeDtypeStruct((1, D), F32)],
        compiler_params=pltpu.CompilerParams(dimension_semantics=("arbitrary",)),
    )(x, target, shift, scale, gain)


def kernel(x, c, ffn1_norm, ffn1_w1, ffn1_w3, ffn1_w2, mix_norm, w_in, conv_w, conv_b, rg_wa, rg_ba, rg_wx, rg_bx, rg_lam, fox_bf, merge_b, w_rg, w_sb, w_fox, w_o, ffn2_norm, ffn2_w1, ffn2_w3, ffn2_w2, ada_w, ada_b, final_norm, final_ada_w, final_ada_b, loss_target, m_ffn1_norm, m_ffn1_w1, m_ffn1_w3, m_ffn1_w2, m_mix_norm, m_w_in, m_conv_w, m_conv_b, m_rg_wa, m_rg_ba, m_rg_wx, m_rg_bx, m_rg_lam, m_fox_bf, m_merge_b, m_w_rg, m_w_sb, m_w_fox, m_w_o, m_ffn2_norm, m_ffn2_w1, m_ffn2_w3, m_ffn2_w2, m_ada_w, m_ada_b, m_final_norm, m_final_ada_w, m_final_ada_b, v_ffn1_norm, v_ffn1_w1, v_ffn1_w3, v_ffn1_w2, v_mix_norm, v_w_in, v_conv_w, v_conv_b, v_rg_wa, v_rg_ba, v_rg_wx, v_rg_bx, v_rg_lam, v_fox_bf, v_merge_b, v_w_rg, v_w_sb, v_w_fox, v_w_o, v_ffn2_norm, v_ffn2_w1, v_ffn2_w3, v_ffn2_w2, v_ada_w, v_ada_b, v_final_norm, v_final_ada_w, v_final_ada_b):
    given = dict(zip(["x", "c"] + WEIGHTS + ["loss_target"] + ["m_" + n for n in WEIGHTS] + ["v_" + n for n in WEIGHTS],
                     (x, c, ffn1_norm, ffn1_w1, ffn1_w3, ffn1_w2, mix_norm, w_in, conv_w, conv_b, rg_wa, rg_ba, rg_wx, rg_bx, rg_lam, fox_bf, merge_b, w_rg, w_sb, w_fox, w_o, ffn2_norm, ffn2_w1, ffn2_w3, ffn2_w2, ada_w, ada_b, final_norm, final_ada_w, final_ada_b, loss_target, m_ffn1_norm, m_ffn1_w1, m_ffn1_w3, m_ffn1_w2, m_mix_norm, m_w_in, m_conv_w, m_conv_b, m_rg_wa, m_rg_ba, m_rg_wx, m_rg_bx, m_rg_lam, m_fox_bf, m_merge_b, m_w_rg, m_w_sb, m_w_fox, m_w_o, m_ffn2_norm, m_ffn2_w1, m_ffn2_w3, m_ffn2_w2, m_ada_w, m_ada_b, m_final_norm, m_final_ada_w, m_final_ada_b, v_ffn1_norm, v_ffn1_w1, v_ffn1_w3, v_ffn1_w2, v_mix_norm, v_w_in, v_conv_w, v_conv_b, v_rg_wa, v_rg_ba, v_rg_wx, v_rg_bx, v_rg_lam, v_fox_bf, v_merge_b, v_w_rg, v_w_sb, v_w_fox, v_w_o, v_ffn2_norm, v_ffn2_w1, v_ffn2_w3, v_ffn2_w2, v_ada_w, v_ada_b, v_final_norm, v_final_ada_w, v_final_ada_b)))
    idx = my_index()
    n_batch = N_DEV * B_LOC
    ada_cols = ada_w.shape[2]
    fin_cols = final_ada_w.shape[1]

    small_in, small_in_meta = _pack([c, conv_w], LANES, 8, F32)
    c_parts, conv_w_parts = _unpack(all_gather("gather_c_conv", small_in), small_in_meta, lead=1)
    c_all = c_parts.reshape(n_batch, D)
    conv_w_all = _unshard(conv_w_parts, 2)
    c_act = ew_fwd("c_silu", _f_silu, [c_all], [], [], [D], [F32])[0]
    mod_cols = [matmul(f"ada_proj_{l}", c_act, ada_w[l], "nn") for l in range(DEPTH)]
    mod_cols.append(matmul("ada_proj_final", c_act, final_ada_w, "nn"))
    mod_g = all_gather("gather_mod", jnp.concatenate(mod_cols, axis=1))

    shards = {(l, group): [given[n][l].astype(BF16) for n in members] for l in range(DEPTH) for group, members in GROUPS}
    waves = [[(0, "ffn1")], [(0, "mix")], [(0, "ffn2")] + [(l, group) for l in range(1, DEPTH) for group, _ in GROUPS]]
    gather_handles, landed = {}, {}

    def start_wave(wave, behind, carrier):
        blocks, behind = lax.optimization_barrier(({key: shards[key] for key in wave}, behind))
        for key in wave:
            gather_handles[key], token = gather_start(f"gather_start_{key[1]}{key[0]}", blocks[key])
            carrier = carrier + token[0, 0]
        return behind, carrier

    def weights_of(l, group, after):
        key = (l, group)
        if key not in landed:
            landed[key] = gather_finish(f"gather_finish_{group}{l}", gather_handles[key], after)
        return {n: _unshard(b, GATHERED[n] - 1) for n, b in zip(dict(GROUPS)[group], landed[key])}

    first_blocks, mod_g = lax.optimization_barrier((shards[0, "ffn1"], mod_g))
    landed[0, "ffn1"] = all_gather("gather_first", first_blocks)
    landed[0, "ffn1"], mod_g = start_wave(waves[1], landed[0, "ffn1"], mod_g)

    mods = []
    for l in range(DEPTH):
        full = mod_g[:, :, l * ada_cols:(l + 1) * ada_cols].transpose(1, 0, 2).reshape(n_batch, N_DEV * ada_cols)
        full = ew_fwd(f"ada_bias_{l}", _f_add_bias, [full], [], [ada_b[l][None]], [full.shape[1]], [F32])[0]
        mods.append(lax.dynamic_slice_in_dim(full, idx * B_LOC, B_LOC, axis=0).reshape(B_LOC, 3, 3, D))
    fm = mod_g[:, :, DEPTH * ada_cols:].transpose(1, 0, 2).reshape(n_batch, N_DEV * fin_cols)
    fm = ew_fwd("ada_bias_final", _f_add_bias, [fm], [], [final_ada_b[None]], [fm.shape[1]], [F32])[0]
    fm = lax.dynamic_slice_in_dim(fm, idx * B_LOC, B_LOC, axis=0).reshape(B_LOC, 2, D)

    def mixer_params(l, w):
        wi = w["w_in"]
        cut = IN_CUTS
        return dict(
            gain=mix_norm[l][None], w_rgx=wi[:, cut[0]:cut[1]], w_gate=wi[:, cut[1]:cut[2]], w_sbqkv=wi[:, cut[2]:cut[3]],
            w_foxqkv=wi[:, cut[3]:cut[4]], w_f=_pad_lanes(wi[:, cut[4]:cut[5]]), w_merge=wi[:, cut[5]:cut[6]],
            conv_w8=jnp.pad(conv_w_all[l], ((0, 8 - CONV_K), (0, 0))), conv_b=conv_b[l][None],
            wa_p=_pair_blocks(rg_wa[l]), wx_p=_pair_blocks(rg_wx[l]), ba=rg_ba[l][None], bx=rg_bx[l][None], lam=rg_lam[l][None],
            bf=_pad_lanes(fox_bf[l][None]), merge_b=merge_b[l][None], w_rg=w["w_rg"], w_sb=w["w_sb"], w_fox=w["w_fox"],
            w_o=w["w_o"])

    n_tok = x.shape[0] * x.shape[1]
    h = x.reshape(n_tok, D)
    saved = []
    for l in range(DEPTH):
        m = mods[l]
        w1 = weights_of(l, "ffn1", m if l == 0 else h)
        h, s1 = _ffn_fwd(f"a{l}", h, _bp(m, 0, 0), _bp(m, 0, 1), _bp(m, 0, 2), ffn1_norm[l][None], w1["ffn1_w1"], w1["ffn1_w3"],
                         w1["ffn1_w2"])
        w2 = weights_of(l, "mix", h)
        if l == 0:
            landed[0, "mix"], m = start_wave(waves[2], landed[0, "mix"], m)
        p = mixer_params(l, w2)
        h, s2 = _mixer_fwd(f"{l}", h, _bp(m, 1, 0), _bp(m, 1, 1), _bp(m, 1, 2), p)
        w3 = weights_of(l, "ffn2", h)
        h, s3 = _ffn_fwd(f"b{l}", h, _bp(m, 2, 0), _bp(m, 2, 1), _bp(m, 2, 2), ffn2_norm[l][None], w3["ffn2_w1"], w3["ffn2_w3"],
                         w3["ffn2_w2"])
        saved.append((s1, s2, s3, p, w1, w3))
    loss_row, dh, dfshift, dfscale, dgain_final = _final_loss(h, loss_target.reshape(n_tok, D), fm[:, 0][:, None, :],
                                                              fm[:, 1][:, None, :], final_norm[None])

    grads = {n: [None] * DEPTH for n in WEIGHTS}
    d_mods = [None] * DEPTH
    scatter_handles = {}
    after_start = jnp.zeros((), F32)

    def scatter_blocks(l, group):
        return [_reshard(grads[n][l], GATHERED[n] - 1).astype(BF16) for n in dict(GROUPS)[group]]

    def start_scatter(l, group, g8s=None):
        g8s = scatter_blocks(l, group) if g8s is None else g8s
        scatter_handles[l, group], token = scatter_start(f"scatter_start_{group}{l}", g8s)
        return token[0, 0]

    for l in reversed(range(DEPTH)):
        m = mods[l]
        s1, s2, s3, p, w1, w3 = saved[l]
        dh, dm3, grads["ffn2_norm"][l], grads["ffn2_w1"][l], grads["ffn2_w3"][l], grads["ffn2_w2"][l] = _ffn_bwd(
            f"b{l}", dh, s3, _bp(m, 2, 0), _bp(m, 2, 1), _bp(m, 2, 2) + after_start, ffn2_norm[l][None], w3["ffn2_w1"],
            w3["ffn2_w3"], w3["ffn2_w2"])
        after_start = start_scatter(l, "ffn2")
        dh, dm2, gm = _mixer_bwd(f"{l}", dh, s2, _bp(m, 1, 0), _bp(m, 1, 1), _bp(m, 1, 2) + after_start, p)
        for n, gval in gm.items():
            grads[n][l] = gval
        after_start = start_scatter(l, "mix")
        dh, dm1, grads["ffn1_norm"][l], grads["ffn1_w1"][l], grads["ffn1_w3"][l], grads["ffn1_w2"][l] = _ffn_bwd(
            f"a{l}", dh, s1, _bp(m, 0, 0), _bp(m, 0, 1), _bp(m, 0, 2) + after_start, ffn1_norm[l][None], w1["ffn1_w1"],
            w1["ffn1_w3"], w1["ffn1_w2"])
        if l > 0:
            after_start = start_scatter(l, "ffn1")
        d_mods[l] = jnp.concatenate([t.reshape(B_LOC, D) for dm in (dm1, dm2, dm3) for t in dm], axis=1)
    grad_x = dh.reshape(x.shape)
    d_fm = jnp.concatenate([dfshift.reshape(B_LOC, D), dfscale.reshape(B_LOC, D)], axis=1)

    rep = {n: jnp.stack([t.reshape(given[n].shape[1:]) for t in grads[n]]) for n in REPLICATED if n != "final_norm"}
    rep["final_norm"] = dgain_final.reshape(D)
    rep["conv_w"] = jnp.stack(grads["conv_w"])
    rep_names = list(rep)
    rep_slab, rep_meta = _pack([rep[n] for n in rep_names], LANES, 8, F32)
    mod_slab, mod_meta = _pack(d_mods + [d_fm], LANES, 8, F32)
    small_g = all_gather("gather_small_grads", jnp.concatenate([mod_slab, rep_slab], axis=0))
    last_blocks, small_g = lax.optimization_barrier((scatter_blocks(0, "ffn1"), small_g))
    small_g = small_g + start_scatter(0, "ffn1", last_blocks)
    d_mod_all = [t.reshape(n_batch, -1) for t in _unpack(small_g[:, :mod_slab.shape[0]], mod_meta, lead=1)]
    rep_sum = add_blocks("sum_small_grads", [small_g[k, mod_slab.shape[0]:] for k in range(N_DEV)], F32)
    rep_grad = dict(zip(rep_names, _unpack(rep_sum, rep_meta)))
    final_g = {n: rep_grad[n] for n in REPLICATED}
    final_g["conv_w"] = lax.dynamic_slice_in_dim(rep_grad["conv_w"], idx * conv_w.shape[2], conv_w.shape[2], axis=2)
    final_g["ada_b"] = jnp.stack([sum_rows(f"ada_b_grad_{l}", d_mod_all[l])[0] for l in range(DEPTH)])
    final_g["final_ada_b"] = sum_rows("final_ada_b_grad", d_mod_all[DEPTH])[0]
    final_g["ada_w"] = jnp.stack([
        matmul(f"ada_w_grad_{l}", c_act, lax.dynamic_slice_in_dim(d_mod_all[l], idx * ada_cols, ada_cols, axis=1), "tn")
        for l in range(DEPTH)])
    final_g["final_ada_w"] = matmul(
        "final_ada_w_grad", c_act, lax.dynamic_slice_in_dim(d_mod_all[DEPTH], idx * fin_cols, fin_cols, axis=1), "tn")

    shard_g = {n: [None] * DEPTH for n in GATHERED}

    def finish_scatter(l, group, after):
        sums = scatter_finish(f"scatter_finish_{group}{l}", scatter_handles[l, group], after)
        for n, gval in zip(dict(GROUPS)[group], sums):
            shard_g[n][l] = gval

    for l in reversed(range(DEPTH)):
        for group in ("ffn2", "mix", "ffn1"):
            if (l, group) != (0, "ffn1"):
                finish_scatter(l, group, rep_sum)

    delta, new_m, new_v = {}, {}, {}
    last = dict(GROUPS)["ffn1"]
    sharded = [n for n in GATHERED if n not in last] + ["ada_w", "final_ada_w", "conv_w"] + list(last)
    for n in sharded:
        if n == last[0]:
            finish_scatter(0, "ffn1", delta["w_in"])
        if n in GATHERED:
            final_g[n] = jnp.stack(shard_g[n])
        delta[n], new_m[n], new_v[n] = adamw(f"adamw_{n}", given[n], final_g[n], given["m_" + n], given["v_" + n])
    for n in WEIGHTS:
        if n not in sharded:
            delta[n], new_m[n], new_v[n] = adamw(f"adamw_{n}", given[n], final_g[n], given["m_" + n], given["v_" + n])

    loss = lax.psum(loss_row[0, 0], ("x", "y", "c"))
    return (loss, grad_x, *[final_g[n] for n in WEIGHTS], *[delta[n] for n in WEIGHTS], *[new_m[n] for n in WEIGHTS],
            *[new_v[n] for n in WEIGHTS])
```
